```python
import math
import jax, jax.numpy as jnp
from jax import lax
import numpy as np

D_MODEL = 1024
BATCH = 8
SEQ = 2048
DEPTH = 2

N_BRANCH = 4
BRANCH_W = D_MODEL // 2
DN_HEADS = 4
DN_DK = BRANCH_W // DN_HEADS
DN_DV = BRANCH_W // DN_HEADS
DN_CHUNK = 64
CONV_W = 4
GM_GROUPS = 4
GM_CH = BRANCH_W // GM_GROUPS
GM_CHUNK = 128
SW_HEADS = 8
SW_KV_HEADS = 2
SW_HD = BRANCH_W // SW_HEADS
SW_WINDOW = 128
SW_BLOCK = 128
MEM_LEN = 256
XM_HEADS = 4
XM_HD = BRANCH_W // XM_HEADS

EPS = 1e-6
NEG_INF = -1e30

IN_SPLITS = [
    3 * BRANCH_W,
    BRANCH_W,
    DN_HEADS,
    DN_HEADS,
    2 * BRANCH_W,
    BRANCH_W,
    SW_HEADS * SW_HD,
    SW_KV_HEADS * SW_HD,
    SW_KV_HEADS * SW_HD,
    BRANCH_W,
    XM_HEADS * XM_HD,
    BRANCH_W,
    N_BRANCH * D_MODEL,
]
D_IN = sum(IN_SPLITS)

kernel_name = "hybrid_parallel_gated_deltanet_gmlp_swa_memory"


def _rmsnorm(x, g):
    xf = x.astype(jnp.float32)
    y = xf * lax.rsqrt(jnp.mean(xf * xf, axis=-1, keepdims=True) + EPS)
    return (y * g.astype(jnp.float32)).astype(x.dtype)


def _l2norm(x):
    return x * lax.rsqrt(jnp.sum(x * x, axis=-1, keepdims=True) + EPS)


def _split_cols(cols):
    idx, acc = [], 0
    for s in IN_SPLITS[:-1]:
        acc += s
        idx.append(acc)
    return jnp.split(cols, idx, axis=-1)


def _causal_dwconv(x, w):
    c = x.shape[-1]
    return lax.conv_general_dilated(
        x, w[:, None, :].astype(x.dtype), window_strides=(1,),
        padding=[(CONV_W - 1, 0)], dimension_numbers=("NWC", "WIO", "NWC"),
        feature_group_count=c)


def _gated_delta_chunked(q, k, v, g, beta):
    B, T, H, dk = q.shape
    dv = v.shape[-1]
    C = DN_CHUNK
    n = T // C

    def blk(a):
        return jnp.moveaxis(a.reshape((B, n, C, H) + a.shape[3:]), 3, 1)

    q, k, v, g, beta = blk(q), blk(k), blk(v), blk(g), blk(beta)
    gc = jnp.cumsum(g, axis=-1)
    diff = gc[..., :, None] - gc[..., None, :]
    ii = jnp.arange(C)
    strict = ii[:, None] > ii[None, :]
    incl = ii[:, None] >= ii[None, :]
    kb = k * beta[..., None]
    L = jnp.where(strict, jnp.einsum("bhncd,bhnsd->bhncs", kb, k)
                  * jnp.exp(jnp.where(strict, diff, 0.0)), 0.0)
    eye = jnp.eye(C, dtype=q.dtype)
    rhs = jnp.concatenate([v * beta[..., None], kb * jnp.exp(gc)[..., None]], axis=-1)
    sol = lax.linalg.triangular_solve(eye + L, rhs, left_side=True, lower=True,
                                      unit_diagonal=True)
    u, w = sol[..., :dv], sol[..., dv:]
    a_qk = jnp.where(incl, jnp.einsum("bhncd,bhnsd->bhncs", q, k)
                     * jnp.exp(jnp.where(incl, diff, 0.0)), 0.0)
    g_last = gc[..., -1]
    qg = q * jnp.exp(gc)[..., None]
    kd = k * jnp.exp(g_last[..., None] - gc)[..., None]
    d_last = jnp.exp(g_last)
    xs = tuple(jnp.moveaxis(a, 2, 0) for a in (qg, kd, u, w, a_qk, d_last))

    def step(S, inp):
        qg_c, kd_c, u_c, w_c, a_c, d_c = inp
        v_new = u_c - jnp.einsum("bhck,bhkv->bhcv", w_c, S)
        o = (jnp.einsum("bhck,bhkv->bhcv", qg_c, S)
             + jnp.einsum("bhcs,bhsv->bhcv", a_c, v_new))
        S = S * d_c[..., None, None] + jnp.einsum("bhck,bhcv->bhkv", kd_c, v_new)
        return S, o

    S0 = jnp.zeros((B, H, dk, dv), jnp.float32)
    _, o = lax.scan(step, S0, xs)
    o = jnp.moveaxis(o, 0, 2)
    return jnp.moveaxis(o, 1, 3).reshape(B, T, H, dv)


def _deltanet_branch(qkv, z, b_logit, a_logit, conv_w, a_log, dt_bias, o_norm):
    B, T, _ = qkv.shape
    qkv = jax.nn.silu(_causal_dwconv(qkv, conv_w)).astype(jnp.float32)
    q, k, v = jnp.split(qkv, 3, axis=-1)
    q = _l2norm(q.reshape(B, T, DN_HEADS, DN_DK)) * (DN_DK ** -0.5)
    k = _l2norm(k.reshape(B, T, DN_HEADS, DN_DK))
    v = v.reshape(B, T, DN_HEADS, DN_DV)
    beta = jax.nn.sigmoid(b_logit.astype(jnp.float32))
    g = -jnp.exp(a_log.astype(jnp.float32)) * jax.nn.softplus(
        a_logit.astype(jnp.float32) + dt_bias.astype(jnp.float32))
    o = _gated_delta_chunked(q, k, v, g, beta)
    o = _rmsnorm(o, o_norm) * jax.nn.silu(z.astype(jnp.float32).reshape(B, T, DN_HEADS, DN_DV))
    return o.reshape(B, T, BRANCH_W).astype(z.dtype)


def _spatial_gating_branch(uv, z, v_gain, w_s, b_s):
    B, T, _ = uv.shape
    u, v = jnp.split(jax.nn.gelu(uv), 2, axis=-1)
    v = _rmsnorm(v, v_gain)
    n = T // GM_CHUNK
    vb = v.reshape(B, n, GM_CHUNK, GM_GROUPS, GM_CH)
    causal = jnp.tril(jnp.ones((GM_CHUNK, GM_CHUNK), dtype=bool))
    ws = jnp.where(causal, w_s, 0.0).astype(v.dtype)
    s = jnp.einsum("gpq,bnqgc->bnpgc", ws, vb) + b_s.T.astype(v.dtype)[None, None, :, :, None]
    return u * s.reshape(B, T, BRANCH_W) * jax.nn.silu(z)


def _sliding_window_branch(q, k, v, z, sinks):
    B, T, _ = q.shape
    P = SW_BLOCK
    n = T // P
    G = SW_HEADS // SW_KV_HEADS
    q = q.reshape(B, n, P, SW_KV_HEADS, G, SW_HD)
    k = k.reshape(B, n, P, SW_KV_HEADS, SW_HD)
    v = v.reshape(B, n, P, SW_KV_HEADS, SW_HD)
    pad = jnp.zeros_like(k[:, :1])
    kb = jnp.concatenate([jnp.concatenate([pad, k[:, :-1]], axis=1), k], axis=2)
    vb = jnp.concatenate([jnp.concatenate([pad, v[:, :-1]], axis=1), v], axis=2)
    s = jnp.einsum("bnqkgd,bnskd->bnkgqs", q, kb).astype(jnp.float32) * (SW_HD ** -0.5)
    qi = jnp.arange(P)[:, None]
    kj = jnp.arange(2 * P)[None, :]
    dist = qi + P - kj
    blk = jnp.arange(n)[:, None, None]
    valid = (dist >= 0) & (dist < SW_WINDOW) & (blk * P - P + kj >= 0)
    s = jnp.where(valid[None, :, None, None], s, NEG_INF)
    sink = jnp.broadcast_to(
        sinks.astype(jnp.float32).reshape(SW_KV_HEADS, G)[None, None, :, :, None, None],
        s.shape[:-1] + (1,))
    p = jax.nn.softmax(jnp.concatenate([s, sink], axis=-1), axis=-1)[..., :-1]
    o = jnp.einsum("bnkgqs,bnskd->bnqkgd", p.astype(vb.dtype), vb).reshape(B, T, BRANCH_W)
    return o * jax.nn.silu(z)


def _memory_branch(q, z, mem_kv):
    B, T, _ = q.shape
    q = q.reshape(B, T, XM_HEADS, XM_HD)
    mk, mv = jnp.split(mem_kv, 2, axis=-1)
    mk = mk.reshape(B, -1, XM_HEADS, XM_HD)
    mv = mv.reshape(B, -1, XM_HEADS, XM_HD)
    s = jnp.einsum("bthd,bmhd->bhtm", q, mk).astype(jnp.float32) * (XM_HD ** -0.5)
    p = jax.nn.softmax(s, axis=-1)
    o = jnp.einsum("bhtm,bmhd->bthd", p.astype(mv.dtype), mv).reshape(B, T, BRANCH_W)
    return o * jax.nn.silu(z)


def _layer(x, mem, norm_pre, norm_post, norm_mem, w_in, conv_w, a_log, dt_bias, dn_norm,
           gm_norm, spatial_w, spatial_b, sinks, w_mem_kv, w_up, w_out):
    B, T, D = x.shape
    h = _rmsnorm(x, norm_pre)
    cols = h @ w_in
    (dn_qkv, dn_z, dn_b, dn_a, gm_uv, gm_z, sw_q, sw_k, sw_v, sw_z,
     xm_q, xm_z, gate_logits) = _split_cols(cols)
    mem_kv = _rmsnorm(mem, norm_mem) @ w_mem_kv
    y_a = _deltanet_branch(dn_qkv, dn_z, dn_b, dn_a, conv_w, a_log, dt_bias, dn_norm)
    y_b = _spatial_gating_branch(gm_uv, gm_z, gm_norm, spatial_w, spatial_b)
    y_c = _sliding_window_branch(sw_q, sw_k, sw_v, sw_z, sinks)
    y_m = _memory_branch(xm_q, xm_z, mem_kv)
    ys = jnp.stack([y_a, y_b, y_c, y_m], axis=2)
    proj = jnp.einsum("btnc,ncd->btnd", ys, w_up)
    gates = jax.nn.sigmoid(gate_logits.reshape(B, T, N_BRANCH, D))
    merged = jnp.sum(gates * proj, axis=2)
    out = merged @ w_out
    return x + _rmsnorm(out, norm_post)


def _fwd_setup_inputs(seed: int = 0) -> dict:
    key = jax.random.key(seed)
    ks = jax.random.split(key, 20)
    f32 = jnp.float32
    nrm = lambda k, shape, scale: jax.random.normal(k, shape, f32) * scale
    dt = jnp.exp(jax.random.uniform(ks[8], (DEPTH, DN_HEADS), f32,
                                    math.log(1e-3), math.log(1e-1)))
    return {
        "x": nrm(ks[0], (BATCH, SEQ, D_MODEL), 1.0),
        "mem": nrm(ks[1], (BATCH, MEM_LEN, D_MODEL), 1.0),
        "norm_pre": 1.0 + nrm(ks[2], (DEPTH, D_MODEL), 0.1),
        "norm_post": 1.0 + nrm(ks[3], (DEPTH, D_MODEL), 0.1),
        "norm_mem": 1.0 + nrm(ks[4], (DEPTH, D_MODEL), 0.1),
        "w_in": nrm(ks[5], (DEPTH, D_MODEL, D_IN), D_MODEL ** -0.5),
        "conv_w": nrm(ks[6], (DEPTH, CONV_W, 3 * BRANCH_W), CONV_W ** -0.5),
        "a_log": jnp.log(jax.random.uniform(ks[7], (DEPTH, DN_HEADS), f32, 1.0, 16.0)),
        "dt_bias": dt + jnp.log(-jnp.expm1(-dt)),
        "dn_norm": 1.0 + nrm(ks[9], (DEPTH, DN_DV), 0.1),
        "gm_norm": 1.0 + nrm(ks[10], (DEPTH, BRANCH_W), 0.1),
        "spatial_w": nrm(ks[11], (DEPTH, GM_GROUPS, GM_CHUNK, GM_CHUNK), GM_CHUNK ** -0.5),
        "spatial_b": 1.0 + nrm(ks[12], (DEPTH, GM_GROUPS, GM_CHUNK), 0.1),
        "sinks": nrm(ks[13], (DEPTH, SW_HEADS), 1.0),
        "w_mem_kv": nrm(ks[14], (DEPTH, D_MODEL, 2 * BRANCH_W), D_MODEL ** -0.5),
        "w_up": nrm(ks[15], (DEPTH, N_BRANCH, BRANCH_W, D_MODEL), BRANCH_W ** -0.5),
        "w_out": nrm(ks[16], (DEPTH, D_MODEL, D_MODEL), D_MODEL ** -0.5),
    }


def _fwd_reference(x, mem, norm_pre, norm_post, norm_mem, w_in, conv_w, a_log, dt_bias, dn_norm,
              gm_norm, spatial_w, spatial_b, sinks, w_mem_kv, w_up, w_out):
    for l in range(DEPTH):
        x = _layer(x, mem, norm_pre[l], norm_post[l], norm_mem[l], w_in[l], conv_w[l],
                   a_log[l], dt_bias[l], dn_norm[l], gm_norm[l], spatial_w[l], spatial_b[l],
                   sinks[l], w_mem_kv[l], w_up[l], w_out[l])
    return x


import jax as _jax
import jax.numpy as _jnp

TWIN_FORMAT = 'train_step'
FWD_PARAMS = ['x', 'mem', 'norm_pre', 'norm_post', 'norm_mem', 'w_in', 'conv_w', 'a_log', 'dt_bias', 'dn_norm', 'gm_norm', 'spatial_w', 'spatial_b', 'sinks', 'w_mem_kv', 'w_up', 'w_out']
TWIN_WEIGHTS = ['norm_pre', 'norm_post', 'norm_mem', 'w_in', 'conv_w', 'a_log', 'dt_bias', 'dn_norm', 'gm_norm', 'spatial_w', 'spatial_b', 'sinks', 'w_mem_kv', 'w_up', 'w_out']
TWIN_DIFF_INPUT = 'x'
TWIN_INPUTS = ['x', 'mem', 'norm_pre', 'norm_post', 'norm_mem', 'w_in', 'conv_w', 'a_log', 'dt_bias', 'dn_norm', 'gm_norm', 'spatial_w', 'spatial_b', 'sinks', 'w_mem_kv', 'w_up', 'w_out', 'loss_target', 'm_norm_pre', 'm_norm_post', 'm_norm_mem', 'm_w_in', 'm_conv_w', 'm_a_log', 'm_dt_bias', 'm_dn_norm', 'm_gm_norm', 'm_spatial_w', 'm_spatial_b', 'm_sinks', 'm_w_mem_kv', 'm_w_up', 'm_w_out', 'v_norm_pre', 'v_norm_post', 'v_norm_mem', 'v_w_in', 'v_conv_w', 'v_a_log', 'v_dt_bias', 'v_dn_norm', 'v_gm_norm', 'v_spatial_w', 'v_spatial_b', 'v_sinks', 'v_w_mem_kv', 'v_w_up', 'v_w_out']
TWIN_OUTPUTS = ['loss', 'grad_x', 'grad_norm_pre', 'grad_norm_post', 'grad_norm_mem', 'grad_w_in', 'grad_conv_w', 'grad_a_log', 'grad_dt_bias', 'grad_dn_norm', 'grad_gm_norm', 'grad_spatial_w', 'grad_spatial_b', 'grad_sinks', 'grad_w_mem_kv', 'grad_w_up', 'grad_w_out', 'delta_norm_pre', 'delta_norm_post', 'delta_norm_mem', 'delta_w_in', 'delta_conv_w', 'delta_a_log', 'delta_dt_bias', 'delta_dn_norm', 'delta_gm_norm', 'delta_spatial_w', 'delta_spatial_b', 'delta_sinks', 'delta_w_mem_kv', 'delta_w_up', 'delta_w_out', 'new_m_norm_pre', 'new_m_norm_post', 'new_m_norm_mem', 'new_m_w_in', 'new_m_conv_w', 'new_m_a_log', 'new_m_dt_bias', 'new_m_dn_norm', 'new_m_gm_norm', 'new_m_spatial_w', 'new_m_spatial_b', 'new_m_sinks', 'new_m_w_mem_kv', 'new_m_w_up', 'new_m_w_out', 'new_v_norm_pre', 'new_v_norm_post', 'new_v_norm_mem', 'new_v_w_in', 'new_v_conv_w', 'new_v_a_log', 'new_v_dt_bias', 'new_v_dn_norm', 'new_v_gm_norm', 'new_v_spatial_w', 'new_v_spatial_b', 'new_v_sinks', 'new_v_w_mem_kv', 'new_v_w_up', 'new_v_w_out']
TWIN_LEAF_KINDS = {'loss': 'loss', 'grad_x': 'grad_x', 'grad_norm_pre': 'grad_w', 'grad_norm_post': 'grad_w', 'grad_norm_mem': 'grad_w', 'grad_w_in': 'grad_w', 'grad_conv_w': 'grad_w', 'grad_a_log': 'grad_w', 'grad_dt_bias': 'grad_w', 'grad_dn_norm': 'grad_w', 'grad_gm_norm': 'grad_w', 'grad_spatial_w': 'grad_w', 'grad_spatial_b': 'grad_w', 'grad_sinks': 'grad_w', 'grad_w_mem_kv': 'grad_w', 'grad_w_up': 'grad_w', 'grad_w_out': 'grad_w', 'delta_norm_pre': 'delta_w', 'delta_norm_post': 'delta_w', 'delta_norm_mem': 'delta_w', 'delta_w_in': 'delta_w', 'delta_conv_w': 'delta_w', 'delta_a_log': 'delta_w', 'delta_dt_bias': 'delta_w', 'delta_dn_norm': 'delta_w', 'delta_gm_norm': 'delta_w', 'delta_spatial_w': 'delta_w', 'delta_spatial_b': 'delta_w', 'delta_sinks': 'delta_w', 'delta_w_mem_kv': 'delta_w', 'delta_w_up': 'delta_w', 'delta_w_out': 'delta_w', 'new_m_norm_pre': 'new_m', 'new_m_norm_post': 'new_m', 'new_m_norm_mem': 'new_m', 'new_m_w_in': 'new_m', 'new_m_conv_w': 'new_m', 'new_m_a_log': 'new_m', 'new_m_dt_bias': 'new_m', 'new_m_dn_norm': 'new_m', 'new_m_gm_norm': 'new_m', 'new_m_spatial_w': 'new_m', 'new_m_spatial_b': 'new_m', 'new_m_sinks': 'new_m', 'new_m_w_mem_kv': 'new_m', 'new_m_w_up': 'new_m', 'new_m_w_out': 'new_m', 'new_v_norm_pre': 'new_v', 'new_v_norm_post': 'new_v', 'new_v_norm_mem': 'new_v', 'new_v_w_in': 'new_v', 'new_v_conv_w': 'new_v', 'new_v_a_log': 'new_v', 'new_v_dt_bias': 'new_v', 'new_v_dn_norm': 'new_v', 'new_v_gm_norm': 'new_v', 'new_v_spatial_w': 'new_v', 'new_v_spatial_b': 'new_v', 'new_v_sinks': 'new_v', 'new_v_w_mem_kv': 'new_v', 'new_v_w_up': 'new_v', 'new_v_w_out': 'new_v'}


def _forward(args):
    return _fwd_reference(*[args[k] for k in FWD_PARAMS])


def _output_shape():
    out = _jax.eval_shape(lambda: _forward(_fwd_setup_inputs(0)))
    return out.shape, out.dtype

N_MICROBATCH = 1
ADAM_LR = 0.001
ADAM_B1 = 0.9
ADAM_B2 = 0.999
ADAM_EPS = 1e-08
ADAM_WD = 0.01
ADAM_STEP = 10
PER_EXAMPLE_BATCH_AXIS = {'x': 0, 'mem': 0, 'loss_target': 0}
SHARED_INPUTS = []
_WEIGHT_DTYPES = {'norm_pre': _jnp.float32, 'norm_post': _jnp.float32, 'norm_mem': _jnp.float32, 'w_in': _jnp.float32, 'conv_w': _jnp.float32, 'a_log': _jnp.float32, 'dt_bias': _jnp.float32, 'dn_norm': _jnp.float32, 'gm_norm': _jnp.float32, 'spatial_w': _jnp.float32, 'spatial_b': _jnp.float32, 'sinks': _jnp.float32, 'w_mem_kv': _jnp.float32, 'w_up': _jnp.float32, 'w_out': _jnp.float32}
MOMENT_SCALE = {'norm_pre': 5.663067e-01, 'norm_post': 1.609327e+01, 'norm_mem': 4.399178e-02, 'w_in': 1.806846e-01, 'conv_w': 3.845264e-01, 'a_log': 9.286209e-01, 'dt_bias': 8.832318e-01, 'dn_norm': 1.828172e+00, 'gm_norm': 1.449047e-01, 'spatial_w': 1.413790e-01, 'spatial_b': 2.026890e-01, 'sinks': 8.278106e-02, 'w_mem_kv': 4.099735e-02, 'w_up': 3.236827e-01, 'w_out': 6.790569e-01}


def _to_microbatches(a, axis):
    t = _jnp.moveaxis(a, axis, 0)
    t = t.reshape((N_MICROBATCH, t.shape[0] // N_MICROBATCH) + t.shape[1:])
    return _jnp.moveaxis(t, 1, axis + 1)


def setup_inputs(seed: int = 0) -> dict:
    inp = _fwd_setup_inputs(seed)
    key = _jax.random.fold_in(_jax.random.key(seed), 7919)
    shape, _ = _output_shape()
    out = dict(inp)
    out["loss_target"] = _jax.random.normal(_jax.random.fold_in(key, 0), shape, _jnp.float32)
    for i, name in enumerate(TWIN_WEIGHTS):
        w = inp[name].astype(_jnp.float32)
        if MOMENT_SCALE is None:
            s = _jnp.sqrt(_jnp.mean(_jnp.square(w)) + 1e-30)
        else:
            s = MOMENT_SCALE[name]
        km, kv = _jax.random.split(_jax.random.fold_in(key, i + 1))
        out[name] = w
        out["m_" + name] = s * _jax.random.normal(km, w.shape, _jnp.float32)
        out["v_" + name] = (s * s) * _jax.random.uniform(kv, w.shape, _jnp.float32, 0.5, 1.5)
    if N_MICROBATCH > 1:
        for name, axis in PER_EXAMPLE_BATCH_AXIS.items():
            out[name] = _to_microbatches(out[name], axis)
    return {'x': out['x'], 'mem': out['mem'], 'norm_pre': out['norm_pre'], 'norm_post': out['norm_post'], 'norm_mem': out['norm_mem'], 'w_in': out['w_in'], 'conv_w': out['conv_w'], 'a_log': out['a_log'], 'dt_bias': out['dt_bias'], 'dn_norm': out['dn_norm'], 'gm_norm': out['gm_norm'], 'spatial_w': out['spatial_w'], 'spatial_b': out['spatial_b'], 'sinks': out['sinks'], 'w_mem_kv': out['w_mem_kv'], 'w_up': out['w_up'], 'w_out': out['w_out'], 'loss_target': out['loss_target'], 'm_norm_pre': out['m_norm_pre'], 'm_norm_post': out['m_norm_post'], 'm_norm_mem': out['m_norm_mem'], 'm_w_in': out['m_w_in'], 'm_conv_w': out['m_conv_w'], 'm_a_log': out['m_a_log'], 'm_dt_bias': out['m_dt_bias'], 'm_dn_norm': out['m_dn_norm'], 'm_gm_norm': out['m_gm_norm'], 'm_spatial_w': out['m_spatial_w'], 'm_spatial_b': out['m_spatial_b'], 'm_sinks': out['m_sinks'], 'm_w_mem_kv': out['m_w_mem_kv'], 'm_w_up': out['m_w_up'], 'm_w_out': out['m_w_out'], 'v_norm_pre': out['v_norm_pre'], 'v_norm_post': out['v_norm_post'], 'v_norm_mem': out['v_norm_mem'], 'v_w_in': out['v_w_in'], 'v_conv_w': out['v_conv_w'], 'v_a_log': out['v_a_log'], 'v_dt_bias': out['v_dt_bias'], 'v_dn_norm': out['v_dn_norm'], 'v_gm_norm': out['v_gm_norm'], 'v_spatial_w': out['v_spatial_w'], 'v_spatial_b': out['v_spatial_b'], 'v_sinks': out['v_sinks'], 'v_w_mem_kv': out['v_w_mem_kv'], 'v_w_up': out['v_w_up'], 'v_w_out': out['v_w_out']}


def _loss(weights, diff, rest, loss_target):
    with _jax.named_scope("forward"):
        args = {**rest, TWIN_DIFF_INPUT: diff, **{k: w.astype(_WEIGHT_DTYPES[k]) for k, w in weights.items()}}
        y = _forward(args)
    with _jax.named_scope("loss_head"):
        err = _jnp.square(y.astype(_jnp.float32) - loss_target)
        return 0.5 * _jnp.sum(_jnp.mean(err, axis=-1)) if err.ndim else 0.5 * err


def _adamw(w, g, m, v):
    m = ADAM_B1 * m + (1.0 - ADAM_B1) * g
    v = ADAM_B2 * v + (1.0 - ADAM_B2) * _jnp.square(g)
    m_hat = m / (1.0 - ADAM_B1 ** ADAM_STEP)
    v_hat = v / (1.0 - ADAM_B2 ** ADAM_STEP)
    delta = -ADAM_LR * (m_hat / (_jnp.sqrt(v_hat) + ADAM_EPS) + ADAM_WD * w)
    return delta, m, v


def reference(x, mem, norm_pre, norm_post, norm_mem, w_in, conv_w, a_log, dt_bias, dn_norm, gm_norm, spatial_w, spatial_b, sinks, w_mem_kv, w_up, w_out, loss_target, m_norm_pre, m_norm_post, m_norm_mem, m_w_in, m_conv_w, m_a_log, m_dt_bias, m_dn_norm, m_gm_norm, m_spatial_w, m_spatial_b, m_sinks, m_w_mem_kv, m_w_up, m_w_out, v_norm_pre, v_norm_post, v_norm_mem, v_w_in, v_conv_w, v_a_log, v_dt_bias, v_dn_norm, v_gm_norm, v_spatial_w, v_spatial_b, v_sinks, v_w_mem_kv, v_w_up, v_w_out):
    given = dict(x=x, mem=mem, norm_pre=norm_pre, norm_post=norm_post, norm_mem=norm_mem, w_in=w_in, conv_w=conv_w, a_log=a_log, dt_bias=dt_bias, dn_norm=dn_norm, gm_norm=gm_norm, spatial_w=spatial_w, spatial_b=spatial_b, sinks=sinks, w_mem_kv=w_mem_kv, w_up=w_up, w_out=w_out, loss_target=loss_target, m_norm_pre=m_norm_pre, m_norm_post=m_norm_post, m_norm_mem=m_norm_mem, m_w_in=m_w_in, m_conv_w=m_conv_w, m_a_log=m_a_log, m_dt_bias=m_dt_bias, m_dn_norm=m_dn_norm, m_gm_norm=m_gm_norm, m_spatial_w=m_spatial_w, m_spatial_b=m_spatial_b, m_sinks=m_sinks, m_w_mem_kv=m_w_mem_kv, m_w_up=m_w_up, m_w_out=m_w_out, v_norm_pre=v_norm_pre, v_norm_post=v_norm_post, v_norm_mem=v_norm_mem, v_w_in=v_w_in, v_conv_w=v_conv_w, v_a_log=v_a_log, v_dt_bias=v_dt_bias, v_dn_norm=v_dn_norm, v_gm_norm=v_gm_norm, v_spatial_w=v_spatial_w, v_spatial_b=v_spatial_b, v_sinks=v_sinks, v_w_mem_kv=v_w_mem_kv, v_w_up=v_w_up, v_w_out=v_w_out)
    weights = {n: given[n] for n in TWIN_WEIGHTS}
    shared = {n: given[n] for n in SHARED_INPUTS}
    per_example = {n: given[n] for n in ['x', 'mem']}
    grad_fn = _jax.value_and_grad(_loss, argnums=(0, 1))

    def one_microbatch(ex, loss_target):
        ex = dict(ex)
        diff = ex.pop(TWIN_DIFF_INPUT)
        return grad_fn(weights, diff, {**shared, **ex}, loss_target)

    if N_MICROBATCH == 1:
        loss, (grad_w, grad_x) = one_microbatch(per_example, given["loss_target"])
    else:
        def body(carry, xs):
            loss_sum, grad_sum = carry
            l_k, (gw_k, gx_k) = one_microbatch(xs[0], xs[1])
            with _jax.named_scope("update"):
                return (loss_sum + l_k, _jax.tree.map(_jnp.add, grad_sum, gw_k)), gx_k

        init = (_jnp.zeros((), _jnp.float32), _jax.tree.map(_jnp.zeros_like, weights))
        (loss, grad_w), grad_x = _jax.lax.scan(body, init, (per_example, given["loss_target"]))
    with _jax.named_scope("update"):
        delta_w, new_m, new_v = {}, {}, {}
        for n in TWIN_WEIGHTS:
            delta_w[n], new_m[n], new_v[n] = _adamw(weights[n], grad_w[n], given["m_" + n], given["v_" + n])
    return (loss, grad_x, *[grad_w[n] for n in TWIN_WEIGHTS], *[delta_w[n] for n in TWIN_WEIGHTS],
            *[new_m[n] for n in TWIN_WEIGHTS], *[new_v[n] for n in TWIN_WEIGHTS])
```

```python
import collections
import functools

import jax
import jax.numpy as jnp
from jax import lax
from jax.experimental import pallas as pl
from jax.experimental.pallas import tpu as pltpu

F32 = jnp.float32
BF16 = jnp.bfloat16

D_MODEL = 1024
BRANCH_W = 512
MEM_LEN = 256
N_LAYER = 2
N_CHIP = 4
EPS = 1e-6
NEG_INF = -1e30
DN_CHUNK = 64
LANES = 128
VMEM_LIMIT = 48 * 1024 * 1024

ADAM_LR, ADAM_B1, ADAM_B2, ADAM_EPS, ADAM_WD, ADAM_STEP = 0.001, 0.9, 0.999, 1e-08, 0.01, 10

N_PAD = 10240
O_GATE, O_BUV, O_AQ, O_AK, O_AV, O_AZ, O_BZ = 0, 4096, 5120, 5632, 6144, 6656, 7168
O_CQ, O_CZ, O_MQ, O_MZ, O_CKV, O_BA = 7680, 8192, 8704, 9216, 9728, 9984
_PAD_SEGS = [(5896, 4096), (2056, 1024), (0, 512), (512, 512), (1024, 512), (1536, 512), (3080, 512),
             (3592, 512), (4360, 512), (4872, 512), (5384, 512), (4104, 128), (4232, 128), (2048, 8),
             (None, 120), (None, 128)]
D_IN = 9992
SHARD_IN = D_IN // N_CHIP


def _to_pad(w):
    parts = [jnp.zeros((w.shape[0], n), w.dtype) if s is None else w[:, s:s + n] for s, n in _PAD_SEGS]
    return jnp.concatenate(parts, axis=1)


def _from_pad(dw):
    segs, off = [], 0
    for s, n in _PAD_SEGS:
        if s is not None:
            segs.append((s, off, n))
        off += n
    return jnp.concatenate([dw[:, o:o + n] for _, o, n in sorted(segs)], axis=1)


def _dot(a, b, dims, prec):
    if prec == "bf16":
        return lax.dot_general(a.astype(BF16), b.astype(BF16), (dims, ((), ())), preferred_element_type=F32)
    return lax.dot_general(a, b, (dims, ((), ())), precision=lax.Precision.HIGHEST, preferred_element_type=F32)


_NN, _NT, _TN = ((1,), (0,)), ((1,), (1,)), ((0,), (0,))


def _make_mm(prec):
    @jax.custom_vjp
    def nn(a, b):
        return _dot(a, b, _NN, prec)

    @jax.custom_vjp
    def nt(a, b):
        return _dot(a, b, _NT, prec)

    @jax.custom_vjp
    def tn(a, b):
        return _dot(a, b, _TN, prec)

    nn.defvjp(lambda a, b: (nn(a, b), (a, b)), lambda r, g: (nt(g, r[1]), tn(r[0], g)))
    nt.defvjp(lambda a, b: (nt(a, b), (a, b)), lambda r, g: (nn(g, r[1]), tn(g, r[0])))
    tn.defvjp(lambda a, b: (tn(a, b), (a, b)), lambda r, g: (nt(r[1], g), nn(r[0], g)))
    return nn, nt, tn


_nn16, _nt16, _tn16 = _make_mm("bf16")
_nn32, _nt32, _tn32 = _make_mm("f32")


def _make_slice(axis):
    @functools.partial(jax.custom_vjp, nondiff_argnums=(1, 2, 3))
    def sl(x, a, b, n):
        return x[a:b] if axis == 0 else x[:, a:b]

    def fwd(x, a, b, n):
        return sl(x, a, b, n), None

    def bwd(a, b, n, _, g):
        parts = []
        if a > 0:
            parts.append(jnp.zeros((a, g.shape[1]) if axis == 0 else (g.shape[0], a), g.dtype))
        parts.append(g)
        if n - b > 0:
            parts.append(jnp.zeros((n - b, g.shape[1]) if axis == 0 else (g.shape[0], n - b), g.dtype))
        return (jnp.concatenate(parts, axis=axis),)

    sl.defvjp(fwd, bwd)
    return sl


_sl0, _sl1 = _make_slice(0), _make_slice(1)


def _rowsl(x, a, b):
    return _sl0(x, a, b, x.shape[0])


def _cols(x, a, b):
    return _sl1(x, a, b, x.shape[1])


@functools.partial(jax.custom_vjp, nondiff_argnums=(1,))
def _rollr(x, s):
    return pltpu.roll(x, s, 0)


_rollr.defvjp(lambda x, s: (_rollr(x, s), None),
              lambda s, _, g: (pltpu.roll(g, g.shape[0] - s, 0),))


def _iota(shape, axis):
    return lax.broadcasted_iota(jnp.int32, shape, axis)


def _sigmoid(x):
    return lax.logistic(x)


def _silu(x):
    return x * _sigmoid(x)


def _gelu(x):
    return 0.5 * x * (1.0 + jnp.tanh(0.7978845608028654 * (x + 0.044715 * (x * x * x))))


def _softplus(x):
    return jnp.maximum(x, 0.0) + jnp.log(1.0 + jnp.exp(-jnp.abs(x)))


def _rms(x, g):
    return x * lax.rsqrt(jnp.mean(x * x, axis=-1, keepdims=True) + EPS) * g


def _lane_pick(x, lane):
    return jnp.sum(x * (_iota((1, x.shape[1]), 1) == lane).astype(F32), axis=1, keepdims=True)


Row = collections.namedtuple("Row", "arr w cb hb grad", defaults=(0, True))


def _full_spec(shape):
    return pl.BlockSpec(shape, lambda i, _n=len(shape): (0,) * _n)


def _load_params(refs):
    return [[p[g].astype(F32) for g in range(p.shape[0])] if len(p.shape) == 3 else p[...].astype(F32)
            for p in refs]


def _rows_fwd(name, fn, rows, params, outs, tr, carry=None):
    T = rows[0].arr.shape[0]
    n = T // tr
    halos = [r for r in rows if r.hb]
    nr, nh, npar, no = len(rows), len(halos), len(params), len(outs)

    def body(*refs):
        row_refs, halo_refs = refs[:nr], refs[nr:nr + nh]
        par_refs = refs[nr + nh:nr + nh + npar]
        out_refs = refs[nr + nh + npar:nr + nh + npar + no]
        rest = refs[nr + nh + npar + no:]
        first = pl.program_id(0) == 0
        cvals = None
        if carry is not None:
            csave_ref, carry_ref = rest

            @pl.when(first)
            def _():
                carry_ref[...] = jnp.zeros_like(carry_ref)

            cvals = [carry_ref[g] for g in range(carry[0])]
            for g in range(carry[0]):
                csave_ref[0, g] = cvals[g]
        c_out, o = fn(first, cvals, [r[...].astype(F32) for r in row_refs],
                      [h[...].astype(F32) for h in halo_refs], _load_params(par_refs))
        for r, v in zip(out_refs, o):
            r[...] = v.astype(r.dtype)
        if carry is not None:
            for g in range(carry[0]):
                carry_ref[g] = c_out[g]

    in_specs = [pl.BlockSpec((tr, r.w), lambda i, c=r.cb: (i, c)) for r in rows]
    in_specs += [pl.BlockSpec((r.hb, r.w), lambda i, c=r.cb, q=tr // r.hb: (jnp.maximum(i * q - 1, 0), c))
                 for r in halos]
    in_specs += [_full_spec(p.shape) for p in params]
    out_shape = [jax.ShapeDtypeStruct((T, w), dt) for w, dt in outs]
    out_specs = [pl.BlockSpec((tr, w), lambda i: (i, 0)) for w, _ in outs]
    scratch = []
    if carry is not None:
        out_shape.append(jax.ShapeDtypeStruct((n,) + carry, F32))
        out_specs.append(pl.BlockSpec((1,) + carry, lambda i: (i, 0, 0, 0)))
        scratch.append(pltpu.VMEM(carry, F32))
    return pl.pallas_call(
        body, name=name, grid=(n,), in_specs=in_specs, out_specs=out_specs, out_shape=out_shape,
        scratch_shapes=scratch,
        compiler_params=pltpu.CompilerParams(dimension_semantics=("arbitrary",), vmem_limit_bytes=VMEM_LIMIT),
    )(*[r.arr for r in rows], *[r.arr for r in halos], *params)


def _rows_bwd(name, fn, rows, params, douts, tr, carry=None, csave=None):
    T = rows[0].arr.shape[0]
    n = T // tr
    halos = [r for r in rows if r.hb]
    grows = [r for r in rows if r.grad]
    nr, nh, npar, nd, ng = len(rows), len(halos), len(params), len(douts), len(grows)
    nc = 0 if carry is None else 1

    def body(*refs):
        row_refs, halo_refs = refs[:nr], refs[nr:nr + nh]
        par_refs = refs[nr + nh:nr + nh + npar]
        k = nr + nh + npar
        csave_ref = refs[k] if nc else None
        dout_refs = refs[k + nc:k + nc + nd]
        k = k + nc + nd
        drow_refs, dpar_refs = refs[k:k + ng], refs[k + ng:k + ng + npar]
        k = k + ng + npar
        dcarry_ref = refs[k] if nc else None
        hgrad_refs = refs[k + nc:]
        i = pl.program_id(0)
        first_tile = i == n - 1

        @pl.when(i == 0)
        def _():
            for r in dpar_refs:
                r[...] = jnp.zeros_like(r)
            for r in hgrad_refs:
                r[...] = jnp.zeros_like(r)
            if nc:
                dcarry_ref[...] = jnp.zeros_like(dcarry_ref)

        rv = [r[...].astype(F32) for r in row_refs]
        hv = [h[...].astype(F32) for h in halo_refs]
        pv = _load_params(par_refs)
        dov = [d[...] for d in dout_refs]
        if nc:
            cv = [csave_ref[0, g] for g in range(carry[0])]
            _, vjp = jax.vjp(lambda c, r, h, p: fn(first_tile, c, r, h, p), cv, rv, hv, pv)
            dc, dr, dh, dp = vjp(([dcarry_ref[g] for g in range(carry[0])], dov))
            for g in range(carry[0]):
                dcarry_ref[g] = dc[g]
        else:
            _, vjp = jax.vjp(lambda r, h, p: fn(first_tile, None, r, h, p)[1], rv, hv, pv)
            dr, dh, dp = vjp(dov)
        gi = hi = 0
        for kk, r in enumerate(rows):
            if r.grad:
                drow_refs[gi][...] = dr[kk]
                if r.hb:
                    drow_refs[gi][tr - r.hb:, :] += hgrad_refs[hi][...]
                gi += 1
            if r.hb:
                hgrad_refs[hi][...] = dh[hi]
                hi += 1
        for r, d in zip(dpar_refs, dp):
            if len(r.shape) == 3:
                for g in range(r.shape[0]):
                    r[g] += d[g]
            else:
                r[...] += d

    rev = lambda i: n - 1 - i
    in_specs = [pl.BlockSpec((tr, r.w), lambda i, c=r.cb: (rev(i), c)) for r in rows]
    in_specs += [pl.BlockSpec((r.hb, r.w), lambda i, c=r.cb, q=tr // r.hb: (jnp.maximum(rev(i) * q - 1, 0), c))
                 for r in halos]
    in_specs += [_full_spec(p.shape) for p in params]
    args = [r.arr for r in rows] + [r.arr for r in halos] + list(params)
    scratch = []
    if nc:
        in_specs.append(pl.BlockSpec((1,) + carry, lambda i: (rev(i), 0, 0, 0)))
        args.append(csave)
        scratch.append(pltpu.VMEM(carry, F32))
    in_specs += [pl.BlockSpec((tr, d.shape[1]), lambda i: (rev(i), 0)) for d in douts]
    args += list(douts)
    scratch += [pltpu.VMEM((r.hb, r.w), F32) for r in halos]
    out_shape = [jax.ShapeDtypeStruct((T, r.w), F32) for r in grows]
    out_specs = [pl.BlockSpec((tr, r.w), lambda i: (rev(i), 0)) for r in grows]
    out_shape += [jax.ShapeDtypeStruct(p.shape, F32) for p in params]
    out_specs += [_full_spec(p.shape) for p in params]
    res = pl.pallas_call(
        body, name=name, grid=(n,), in_specs=in_specs, out_specs=out_specs, out_shape=out_shape,
        scratch_shapes=scratch,
        compiler_params=pltpu.CompilerParams(dimension_semantics=("arbitrary",), vmem_limit_bytes=VMEM_LIMIT),
    )(*args)
    return list(res[:ng]), list(res[ng:])


def _matmul(name, a, b, kind, out_dtype, tm, tn, tk):
    if kind == "tn":
        (K, M), N = a.shape, b.shape[1]
    else:
        (M, K), N = a.shape, (b.shape[0] if kind == "nt" else b.shape[1])
    tm, tn, tk = min(tm, M), min(tn, N), min(tk, K)
    nk = K // tk
    dims = {"nn": _NN, "nt": _NT, "tn": _TN}[kind]

    def body(a_ref, b_ref, o_ref, acc_ref):
        k = pl.program_id(2)

        @pl.when(k == 0)
        def _():
            acc_ref[...] = jnp.zeros_like(acc_ref)

        acc_ref[...] += lax.dot_general(a_ref[...], b_ref[...], (dims, ((), ())), preferred_element_type=F32)

        @pl.when(k == nk - 1)
        def _():
            o_ref[...] = acc_ref[...].astype(o_ref.dtype)

    a_spec = pl.BlockSpec((tk, tm), lambda i, j, k: (k, i)) if kind == "tn" else pl.BlockSpec((tm, tk), lambda i, j, k: (i, k))
    b_spec = pl.BlockSpec((tn, tk), lambda i, j, k: (j, k)) if kind == "nt" else pl.BlockSpec((tk, tn), lambda i, j, k: (k, j))
    return pl.pallas_call(
        body, name=name, grid=(M // tm, N // tn, nk), in_specs=[a_spec, b_spec],
        out_specs=pl.BlockSpec((tm, tn), lambda i, j, k: (i, j)),
        out_shape=jax.ShapeDtypeStruct((M, N), out_dtype),
        scratch_shapes=[pltpu.VMEM((tm, tn), F32)],
        compiler_params=pltpu.CompilerParams(dimension_semantics=("arbitrary", "arbitrary", "arbitrary"),
                                             vmem_limit_bytes=VMEM_LIMIT),
    )(a, b)


def _pre_fn(first, _, rows, halos, params):
    return None, [_rms(rows[0], params[0])]


def _pre_fn_res(first, _, rows, halos, params):
    return None, [_rms(rows[0], params[0]), rows[0]]


def _memkv_fn(first, _, rows, halos, params):
    g, w = params
    return None, [_nn16(_rms(rows[0], g), w)]


def _conv_silu(x, halo, w4, keep_halo):
    tr = x.shape[0]
    halo = halo * keep_halo
    rid = _iota((tr, 1), 0)
    acc = w4[3] * x
    for s in (1, 2, 3):
        hs = jnp.concatenate([_rollr(halo, s), jnp.zeros((tr - halo.shape[0], x.shape[1]), F32)], axis=0)
        acc = acc + w4[3 - s] * jnp.where(rid < s, hs, _rollr(x, s))
    return _silu(acc)


def _dn_fn(first, S, rows, halos, params):
    qp, kp, vp, z, ba = rows
    conv, a_vec, dt_vec, dnorm = params
    tr = qp.shape[0]
    keep = jnp.where(first, 0.0, 1.0)
    q = _conv_silu(qp, halos[0], [conv[3 * j + 0] for j in range(4)], keep)
    k = _conv_silu(kp, halos[1], [conv[3 * j + 1] for j in range(4)], keep)
    v = _conv_silu(vp, halos[2], [conv[3 * j + 2] for j in range(4)], keep)
    qh, kh, vh = [], [], []
    for h in range(4):
        a, b = h * LANES, (h + 1) * LANES
        xq, xk = _cols(q, a, b), _cols(k, a, b)
        qh.append(xq * lax.rsqrt(jnp.sum(xq * xq, axis=1, keepdims=True) + EPS) * (LANES ** -0.5))
        kh.append(xk * lax.rsqrt(jnp.sum(xk * xk, axis=1, keepdims=True) + EPS))
        vh.append(_cols(v, a, b))
    beta_all = _sigmoid(ba)
    g_all = -jnp.exp(a_vec) * _softplus(ba + dt_vec)
    C = DN_CHUNK
    ii, jj = _iota((C, C), 0), _iota((C, C), 1)
    strict, incl = ii > jj, ii >= jj
    eye = (ii == jj).astype(F32)
    last_row = (_iota((C, 1), 0) == C - 1).astype(F32)
    S = list(S)
    ychunks = []
    for c in range(tr // C):
        r0, r1 = c * C, (c + 1) * C
        gcs = _nn32(incl.astype(F32), _rowsl(g_all, r0, r1))
        bch = _rowsl(beta_all, r0, r1)
        zc = _rowsl(z, r0, r1)
        yh = []
        for h in range(4):
            qc, kc, vc = _rowsl(qh[h], r0, r1), _rowsl(kh[h], r0, r1), _rowsl(vh[h], r0, r1)
            beta = _lane_pick(bch, h)
            gc = _lane_pick(gcs, 4 + h)
            gc_row = jnp.sum(eye * gc, axis=0, keepdims=True)
            dec = jnp.exp(jnp.where(incl, gc - gc_row, 0.0))
            kb = kc * beta
            P = -jnp.where(strict, _nt16(kb, kc) * dec, 0.0)
            tinv = eye + P
            for _ in range(5):
                P = _nn32(P, P)
                tinv = tinv + _nn32(tinv, P)
            egc = jnp.exp(gc)
            u = _nn32(tinv, vc * beta)
            w = _nn32(tinv, kb * egc)
            aqk = jnp.where(incl, _nt16(qc, kc) * dec, 0.0)
            glast = jnp.sum(gc * last_row, axis=0, keepdims=True)
            vnew = u - _nn16(w, S[h])
            o = _nn16(qc * egc, S[h]) + _nn16(aqk, vnew)
            S[h] = S[h] * jnp.exp(glast) + _tn16(kc * jnp.exp(glast - gc), vnew)
            yh.append(_rms(o, dnorm) * _silu(_cols(zc, h * LANES, (h + 1) * LANES)))
        ychunks.append(jnp.concatenate(yh, axis=1))
    return S, [jnp.concatenate(ychunks, axis=0)]


def _gm_fn(first, _, rows, halos, params):
    uv, z = rows
    gnorm, ws, bs = params
    tr = uv.shape[0]
    guv = _gelu(uv)
    u = _cols(guv, 0, BRANCH_W)
    v = _rms(_cols(guv, BRANCH_W, 2 * BRANCH_W), gnorm)
    ii, jj = _iota((LANES, LANES), 0), _iota((LANES, LANES), 1)
    eye = (ii == jj).astype(F32)
    wsm = [jnp.where(ii >= jj, ws[g], 0.0) for g in range(4)]
    bcol = [jnp.sum(eye * bs[g], axis=1, keepdims=True) for g in range(4)]
    chunks = []
    for c in range(tr // LANES):
        vc = _rowsl(v, c * LANES, (c + 1) * LANES)
        chunks.append(jnp.concatenate(
            [_nn16(wsm[g], _cols(vc, g * LANES, (g + 1) * LANES)) + bcol[g] for g in range(4)], axis=1))
    return None, [u * jnp.concatenate(chunks, axis=0) * _silu(z)]


def _swa_fn(first, _, rows, halos, params):
    q, kvc, z = rows
    sink_vec = params[0]
    P = LANES
    kv = jnp.concatenate([halos[0], kvc], axis=0)
    k, v = _cols(kv, 0, P), _cols(kv, P, 2 * P)
    r, cc = _iota((P, P), 0), _iota((P, P), 1)
    lane = _iota((1, P), 1)
    dist = _iota((P, 2 * P), 0) + P - _iota((P, 2 * P), 1)
    kmin = jnp.where(first, P, 0)
    valid = (dist >= 0) & (dist < P) & (_iota((P, 2 * P), 1) >= kmin)
    blocks = [None] * 4
    for kh in range(2):
        dup = (r == kh * 64 + (cc & 63)).astype(F32)
        kk, vv = _nn16(k, dup), _nn16(v, dup)
        for g in range(4):
            h = kh * 4 + g
            half = ((lane >= 64) == (h % 2 == 1)).astype(F32)
            qb = _cols(q, (h // 2) * P, (h // 2 + 1) * P) * half
            s = jnp.where(valid, _nt16(qb, kk) * 0.125, NEG_INF)
            sink = _lane_pick(sink_vec, h)
            m = lax.stop_gradient(jnp.maximum(jnp.max(s, axis=1, keepdims=True), sink))
            e = jnp.exp(s - m)
            p = e / (jnp.sum(e, axis=1, keepdims=True) + jnp.exp(sink - m))
            o = _nn16(p, vv) * half
            blocks[h // 2] = o if blocks[h // 2] is None else blocks[h // 2] + o
    return None, [jnp.concatenate(blocks, axis=1) * _silu(z)]


def _mem_fn(first, _, rows, halos, params):
    q, z = rows
    mkv = params[0]
    outs = []
    for h in range(4):
        a, b = h * LANES, (h + 1) * LANES
        s = _nt16(_cols(q, a, b), _cols(mkv, a, b)) * (LANES ** -0.5)
        m = lax.stop_gradient(jnp.max(s, axis=1, keepdims=True))
        e = jnp.exp(s - m)
        p = e / jnp.sum(e, axis=1, keepdims=True)
        outs.append(_nn16(p, _cols(mkv, BRANCH_W + a, BRANCH_W + b)))
    return None, [jnp.concatenate(outs, axis=1) * _silu(z)]


def _up_fn(first, _, rows, halos, params):
    y, gl = rows
    return None, [_sigmoid(gl) * _nn16(y, params[0])]


def _out_fn(first, _, rows, halos, params):
    x, m0, m1, m2, m3 = rows
    w, g = params
    return None, [x + _rms(_nn16(m0 + m1 + m2 + m3, w), g)]


def _loss_fn(first, _, rows, halos, params):
    y, t = rows
    d = y - t
    lrow = 0.5 * jnp.mean(d * d, axis=1, keepdims=True)
    return None, [d * (1.0 / D_MODEL), jnp.broadcast_to(lrow, (y.shape[0], LANES))]


TR = 256
DN_TR = 128
CARRY = (4, LANES, LANES)


def _branch_rows(cols):
    a = [Row(cols, 512, O_AQ // 512, 8), Row(cols, 512, O_AK // 512, 8), Row(cols, 512, O_AV // 512, 8),
         Row(cols, 512, O_AZ // 512), Row(cols, LANES, O_BA // LANES)]
    b = [Row(cols, 1024, O_BUV // 1024), Row(cols, 512, O_BZ // 512)]
    c = [Row(cols, 512, O_CQ // 512), Row(cols, 256, O_CKV // 256, LANES), Row(cols, 512, O_CZ // 512)]
    m = [Row(cols, 512, O_MQ // 512), Row(cols, 512, O_MZ // 512)]
    return a, b, c, m


def _layer_fwd(x, mem, W):
    h = _rows_fwd("prenorm_fwd", _pre_fn, [Row(x, D_MODEL, 0)], [W["norm_pre"]], [(D_MODEL, BF16)], TR)[0]
    cols = _matmul("in_proj_fwd", h, W["w_pad"], "nn", F32, 1024, 512, 1024)
    mem_kv = _rows_fwd("memkv_fwd", _memkv_fn, [Row(mem, D_MODEL, 0)], [W["norm_mem"], W["w_mem_kv"]],
                       [(D_MODEL, F32)], MEM_LEN)[0]
    ra, rb, rc, rm = _branch_rows(cols)
    y_a, csave = _rows_fwd("dn_fwd", _dn_fn, ra, [W["conv"], W["a_vec"], W["dt_vec"], W["dn_norm"]],
                           [(BRANCH_W, F32)], DN_TR, CARRY)
    y_b = _rows_fwd("gm_fwd", _gm_fn, rb, [W["gm_norm"], W["spatial_w"], W["spatial_b"]], [(BRANCH_W, F32)], TR)[0]
    y_c = _rows_fwd("swa_fwd", _swa_fn, rc, [W["sink_vec"]], [(BRANCH_W, F32)], LANES)[0]
    y_m = _rows_fwd("mem_fwd", _mem_fn, rm, [mem_kv], [(BRANCH_W, F32)], TR)[0]
    ys = [y_a, y_b, y_c, y_m]
    ms = [_rows_fwd("up_fwd", _up_fn, [Row(ys[n], BRANCH_W, 0), Row(cols, D_MODEL, n)], [W["w_up"][n]],
                    [(D_MODEL, F32)], TR)[0] for n in range(4)]
    x_new = _rows_fwd("out_fwd", _out_fn, [Row(x, D_MODEL, 0)] + [Row(m, D_MODEL, 0) for m in ms],
                      [W["w_out"], W["norm_post"]], [(D_MODEL, F32)], TR)[0]
    return x_new, dict(x=x, h=h, cols=cols, mem_kv=mem_kv, csave=csave, ys=ys, ms=ms)


def _layer_bwd(dxn, mem, W, sv):
    x, cols = sv["x"], sv["cols"]
    (dx_res, dm), (dw_out, dnorm_post) = _rows_bwd(
        "out_bwd", _out_fn, [Row(x, D_MODEL, 0), Row(sv["ms"][0], D_MODEL, 0)]
        + [Row(m, D_MODEL, 0, 0, False) for m in sv["ms"][1:]], [W["w_out"], W["norm_post"]], [dxn], TR)
    dys, dgl, dw_up = [], [], []
    for n in range(4):
        (dy, dg), (dwu,) = _rows_bwd("up_bwd", _up_fn, [Row(sv["ys"][n], BRANCH_W, 0), Row(cols, D_MODEL, n)],
                                     [W["w_up"][n]], [dm], TR)
        dys.append(dy), dgl.append(dg), dw_up.append(dwu)
    ra, rb, rc, rm = _branch_rows(cols)
    (dqp, dkp, dvp, dz_a, dba), (dconv, da_vec, ddt_vec, ddn_norm) = _rows_bwd(
        "dn_bwd", _dn_fn, ra, [W["conv"], W["a_vec"], W["dt_vec"], W["dn_norm"]], [dys[0]], DN_TR, CARRY, sv["csave"])
    (duv, dz_b), (dgm_norm, dws, dbs) = _rows_bwd(
        "gm_bwd", _gm_fn, rb, [W["gm_norm"], W["spatial_w"], W["spatial_b"]], [dys[1]], TR)
    (dq_c, dkv_c, dz_c), (dsink,) = _rows_bwd("swa_bwd", _swa_fn, rc, [W["sink_vec"]], [dys[2]], LANES)
    (dq_m, dz_m), (dmem_kv,) = _rows_bwd("mem_bwd", _mem_fn, rm, [sv["mem_kv"]], [dys[3]], TR)
    _, (dnorm_mem, dw_mem_kv) = _rows_bwd("memkv_bwd", _memkv_fn, [Row(mem, D_MODEL, 0, 0, False)],
                                          [W["norm_mem"], W["w_mem_kv"]], [dmem_kv], MEM_LEN)
    T = x.shape[0]
    dcols = jnp.concatenate(dgl + [duv, dqp, dkp, dvp, dz_a, dz_b, dq_c, dz_c, dq_m, dz_m, dkv_c, dba,
                                   jnp.zeros((T, LANES), F32)], axis=1).astype(BF16)
    dw_pad = _matmul("in_proj_dw", sv["h"], dcols, "tn", F32, 1024, 1024, 1024)
    dh = _matmul("in_proj_dx", dcols, W["w_pad"], "nt", F32, 1024, 1024, 1024)
    (dx,), (dnorm_pre,) = _rows_bwd("prenorm_bwd", _pre_fn_res, [Row(x, D_MODEL, 0)], [W["norm_pre"]], [dh, dx_res], TR)
    grads = dict(norm_pre=dnorm_pre, norm_post=dnorm_post, norm_mem=dnorm_mem, w_pad=dw_pad, conv=dconv,
                 a_vec=da_vec, dt_vec=ddt_vec, dn_norm=ddn_norm, gm_norm=dgm_norm, spatial_w=dws, spatial_b=dbs,
                 sink_vec=dsink, w_mem_kv=dw_mem_kv, w_up=dw_up, w_out=dw_out)
    return dx, grads


def _lane_vec(v, off):
    return jnp.zeros((1, LANES), F32).at[0, off:off + v.shape[0]].set(v)


def _layer_weights(l, w_in_full, conv_w, w_mem_kv, w_up, w_out, small):
    return dict(
        w_pad=_to_pad(w_in_full).astype(BF16),
        conv=conv_w.reshape(4, 3, BRANCH_W).reshape(12, 1, BRANCH_W),
        w_mem_kv=w_mem_kv, w_up=[w_up[n] for n in range(4)], w_out=w_out,
        norm_pre=small["norm_pre"][l][None], norm_post=small["norm_post"][l][None],
        norm_mem=small["norm_mem"][l][None],
        a_vec=_lane_vec(small["a_log"][l], 4), dt_vec=_lane_vec(small["dt_bias"][l], 4),
        dn_norm=small["dn_norm"][l][None], gm_norm=small["gm_norm"][l][None],
        spatial_w=small["spatial_w"][l], spatial_b=small["spatial_b"][l][:, None, :],
        sink_vec=_lane_vec(small["sinks"][l], 0))


def _local_step(x, mem, target, Ws):
    saved = []
    for W in Ws:
        x, sv = _layer_fwd(x, mem, W)
        saved.append(sv)
    dy, lrows = _rows_fwd("loss", _loss_fn, [Row(x, D_MODEL, 0), Row(target, D_MODEL, 0)], [],
                          [(D_MODEL, F32), (LANES, F32)], TR)
    loss = jnp.sum(lrows[:, 0])
    grads = []
    for W, sv in zip(reversed(Ws), reversed(saved)):
        dy, g = _layer_bwd(dy, mem, W, sv)
        grads.append(g)
    return loss, dy, grads[::-1]


_MESH = pl.DeviceIdType.MESH
_ANY = pl.BlockSpec(memory_space=pl.ANY)


def _position():
    return lax.axis_index("x"), lax.axis_index("y"), lax.axis_index("c")


def _remote(src, dst, send_sem, recv_sem, dev):
    return pltpu.make_async_remote_copy(src_ref=src, dst_ref=dst, send_sem=send_sem, recv_sem=recv_sem,
                                        device_id=dev, device_id_type=_MESH)


def _hbm_call(name, body, arrs, out_shapes, n_send, n_local):
    return pl.pallas_call(
        body, name=name, in_specs=[_ANY] * len(arrs), out_specs=[_ANY] * len(out_shapes), out_shape=out_shapes,
        scratch_shapes=[pltpu.SemaphoreType.DMA((n_send,)), pltpu.SemaphoreType.DMA((n_send,)),
                        pltpu.SemaphoreType.DMA((max(n_local, 1),))],
        compiler_params=pltpu.CompilerParams(has_side_effects=True),
    )(*arrs)


def _other_chips(x, y):
    return [(1 - x, y), (x, 1 - y), (1 - x, 1 - y)]


def _allgather_chips(arrs):
    n = len(arrs)

    def body(*refs):
        ins, outs = refs[:n], refs[n:2 * n]
        send_sems, recv_sems, loc_sems = refs[2 * n:]
        x, y, c = _position()
        me = 2 * x + y
        local = [pltpu.make_async_copy(ins[a], outs[a].at[me], loc_sems.at[a]) for a in range(n)]
        for cp in local:
            cp.start()
        sends = []
        for a in range(n):
            for j, (px, py) in enumerate(_other_chips(x, y)):
                sends.append(_remote(ins[a], outs[a].at[me], send_sems.at[3 * a + j], recv_sems.at[3 * a + j], (px, py, c)))
                sends[-1].start()
        for a in range(n):
            for j, (px, py) in enumerate(_other_chips(x, y)):
                _remote(ins[a], outs[a].at[2 * px + py], send_sems.at[3 * a + j], recv_sems.at[3 * a + j],
                        (px, py, c)).wait_recv()
        for cp in sends:
            cp.wait_send()
        for cp in local:
            cp.wait()

    return _hbm_call("allgather_chips", body, arrs,
                     [jax.ShapeDtypeStruct((N_CHIP,) + a.shape, a.dtype) for a in arrs], 3 * n, n)


def _pair_exchange(arrs):
    n = len(arrs)

    def body(*refs):
        ins, outs = refs[:n], refs[n:2 * n]
        send_sems, recv_sems, _ = refs[2 * n:]
        x, y, c = _position()
        cps = [_remote(ins[a].at[1 - c], outs[a], send_sems.at[a], recv_sems.at[a], (x, y, 1 - c)) for a in range(n)]
        for cp in cps:
            cp.start()
        for cp in cps:
            cp.wait_recv()
        for cp in cps:
            cp.wait_send()

    return _hbm_call("pair_exchange", body, arrs, [jax.ShapeDtypeStruct(a.shape[1:], a.dtype) for a in arrs], n, 0)


def _chip_scatter(arrs):
    n = len(arrs)

    def body(*refs):
        ins, outs = refs[:n], refs[n:2 * n]
        send_sems, recv_sems, loc_sems = refs[2 * n:]
        x, y, c = _position()
        me = 2 * x + y
        local = [pltpu.make_async_copy(ins[a].at[me], outs[a].at[me], loc_sems.at[a]) for a in range(n)]
        for cp in local:
            cp.start()
        sends = []
        for a in range(n):
            for j, (px, py) in enumerate(_other_chips(x, y)):
                sends.append(_remote(ins[a].at[2 * px + py], outs[a].at[me], send_sems.at[3 * a + j],
                                     recv_sems.at[3 * a + j], (px, py, c)))
                sends[-1].start()
        for a in range(n):
            for j, (px, py) in enumerate(_other_chips(x, y)):
                _remote(ins[a].at[me], outs[a].at[2 * px + py], send_sems.at[3 * a + j], recv_sems.at[3 * a + j],
                        (px, py, c)).wait_recv()
        for cp in sends:
            cp.wait_send()
        for cp in local:
            cp.wait()

    return _hbm_call("chip_scatter", body, arrs, [jax.ShapeDtypeStruct(a.shape, a.dtype) for a in arrs], 3 * n, n)


def _pair_share(arrs):
    n = len(arrs)

    def body(*refs):
        ins, outs = refs[:n], refs[n:2 * n]
        send_sems, recv_sems, loc_sems = refs[2 * n:]
        x, y, c = _position()
        local = [pltpu.make_async_copy(ins[a], outs[a].at[c], loc_sems.at[a]) for a in range(n)]
        for cp in local:
            cp.start()
        cps = [_remote(ins[a], outs[a].at[c], send_sems.at[a], recv_sems.at[a], (x, y, 1 - c)) for a in range(n)]
        for cp in cps:
            cp.start()
        for a in range(n):
            _remote(ins[a], outs[a].at[1 - c], send_sems.at[a], recv_sems.at[a], (x, y, 1 - c)).wait_recv()
        for cp in cps:
            cp.wait_send()
        for cp in local:
            cp.wait()

    return _hbm_call("pair_share", body, arrs, [jax.ShapeDtypeStruct((2,) + a.shape, a.dtype) for a in arrs], n, n)


N_DEV = 8


def _allreduce_small(g):
    def body(g_ref, o_ref, buf, send_sems, recv_sems):
        x, y, c = _position()
        me = 4 * x + 2 * y + c
        buf[me] = g_ref[...]
        peers = []
        for j in range(1, N_DEV):
            px = 1 - x if j & 4 else x
            py = 1 - y if j & 2 else y
            pc = 1 - c if j & 1 else c
            peers.append((px, py, pc))
        sends = [_remote(g_ref, buf.at[me], send_sems.at[j], recv_sems.at[j], p) for j, p in enumerate(peers)]
        for cp in sends:
            cp.start()
        for j, (px, py, pc) in enumerate(peers):
            _remote(g_ref, buf.at[4 * px + 2 * py + pc], send_sems.at[j], recv_sems.at[j], (px, py, pc)).wait_recv()
        for cp in sends:
            cp.wait_send()
        acc = buf[0]
        for s in range(1, N_DEV):
            acc = acc + buf[s]
        o_ref[...] = acc

    vmem = pl.BlockSpec(memory_space=pltpu.VMEM)
    return pl.pallas_call(
        body, name="allreduce_small", in_specs=[vmem], out_specs=vmem, out_shape=jax.ShapeDtypeStruct(g.shape, F32),
        scratch_shapes=[pltpu.VMEM((N_DEV,) + g.shape, F32), pltpu.SemaphoreType.DMA((N_DEV - 1,)),
                        pltpu.SemaphoreType.DMA((N_DEV - 1,))],
        compiler_params=pltpu.CompilerParams(vmem_limit_bytes=VMEM_LIMIT),
    )(g)


EW_ROWS = 128


def _ew(name, fn, ins, n_out):
    def dims(a):
        return a[0].shape[1:] if isinstance(a, tuple) else a.shape

    R, w = dims(ins[0])
    tr = EW_ROWS if R % EW_ROWS == 0 else R

    def body(*refs):
        outs = fn(*[r[...] for r in refs[:len(ins)]])
        for r, v in zip(refs[len(ins):], outs):
            r[...] = v

    in_specs = [pl.BlockSpec((None, tr, w), lambda i, s=a[1]: (s, i, 0)) if isinstance(a, tuple)
                else pl.BlockSpec((tr, w), lambda i: (i, 0)) for a in ins]
    return pl.pallas_call(
        body, name=name, grid=(R // tr,), in_specs=in_specs,
        out_specs=[pl.BlockSpec((tr, w), lambda i: (i, 0))] * n_out,
        out_shape=[jax.ShapeDtypeStruct((R, w), F32)] * n_out,
        compiler_params=pltpu.CompilerParams(dimension_semantics=("arbitrary",), vmem_limit_bytes=VMEM_LIMIT),
    )(*[a[0] if isinstance(a, tuple) else a for a in ins])


def _adamw_fn(w, g, m, v):
    m = ADAM_B1 * m + (1.0 - ADAM_B1) * g
    v = ADAM_B2 * v + (1.0 - ADAM_B2) * (g * g)
    m_hat = m / (1.0 - ADAM_B1 ** ADAM_STEP)
    v_hat = v / (1.0 - ADAM_B2 ** ADAM_STEP)
    delta = -ADAM_LR * (m_hat / (jnp.sqrt(v_hat) + ADAM_EPS) + ADAM_WD * w)
    return delta, m, v


def _adamw(name, w, g, m, v):
    shape = w.shape
    two = lambda a: a.reshape(-1, shape[-1])
    return [o.reshape(shape) for o in _ew(name, _adamw_fn, [two(w), two(g), two(m), two(v)], 3)]


_SMALL = [("norm_pre", (2, 1024)), ("norm_post", (2, 1024)), ("norm_mem", (2, 1024)), ("a_log", (2, 4)),
          ("dt_bias", (2, 4)), ("dn_norm", (2, 128)), ("gm_norm", (2, 512)), ("spatial_w", (2, 4, 128, 128)),
          ("spatial_b", (2, 4, 128)), ("sinks", (2, 8))]
_SMALL_ROWS = 144
_BIG = ["w_in", "conv_w", "w_mem_kv", "w_up", "w_out"]
_NAMES = ["norm_pre", "norm_post", "norm_mem", "w_in", "conv_w", "a_log", "dt_bias", "dn_norm", "gm_norm",
          "spatial_w", "spatial_b", "sinks", "w_mem_kv", "w_up", "w_out"]


def _pack_small(d):
    flat = jnp.concatenate([d[n].reshape(-1) for n, _ in _SMALL])
    return jnp.pad(flat, (0, _SMALL_ROWS * 1024 - flat.shape[0])).reshape(_SMALL_ROWS, 1024)


def _unpack_small(p):
    flat, out, off = p.reshape(-1), {}, 0
    for n, shp in _SMALL:
        size = 1
        for s in shp:
            size *= s
        out[n] = flat[off:off + size].reshape(shp)
        off += size
    return out


def _by_chip(name, g):
    if name == "w_in":
        return _from_pad(g).reshape(D_MODEL, N_CHIP, SHARD_IN).transpose(1, 0, 2)
    if name == "conv_w":
        return g.reshape(4, N_CHIP, 3 * BRANCH_W // N_CHIP).transpose(1, 0, 2)
    if name == "w_up":
        return jnp.stack(g).reshape(4, BRANCH_W, N_CHIP, D_MODEL // N_CHIP).transpose(2, 0, 1, 3)
    return g.reshape(N_CHIP, D_MODEL // N_CHIP, D_MODEL)


def kernel(x, mem, norm_pre, norm_post, norm_mem, w_in, conv_w, a_log, dt_bias, dn_norm, gm_norm, spatial_w, spatial_b, sinks, w_mem_kv, w_up, w_out, loss_target, m_norm_pre, m_norm_post, m_norm_mem, m_w_in, m_conv_w, m_a_log, m_dt_bias, m_dn_norm, m_gm_norm, m_spatial_w, m_spatial_b, m_sinks, m_w_mem_kv, m_w_up, m_w_out, v_norm_pre, v_norm_post, v_norm_mem, v_w_in, v_conv_w, v_a_log, v_dt_bias, v_dn_norm, v_gm_norm, v_spatial_w, v_spatial_b, v_sinks, v_w_mem_kv, v_w_up, v_w_out):
    w = dict(norm_pre=norm_pre, norm_post=norm_post, norm_mem=norm_mem, w_in=w_in, conv_w=conv_w, a_log=a_log,
             dt_bias=dt_bias, dn_norm=dn_norm, gm_norm=gm_norm, spatial_w=spatial_w, spatial_b=spatial_b, sinks=sinks,
             w_mem_kv=w_mem_kv, w_up=w_up, w_out=w_out)
    m = dict(norm_pre=m_norm_pre, norm_post=m_norm_post, norm_mem=m_norm_mem, w_in=m_w_in, conv_w=m_conv_w,
             a_log=m_a_log, dt_bias=m_dt_bias, dn_norm=m_dn_norm, gm_norm=m_gm_norm, spatial_w=m_spatial_w,
             spatial_b=m_spatial_b, sinks=m_sinks, w_mem_kv=m_w_mem_kv, w_up=m_w_up, w_out=m_w_out)
    v = dict(norm_pre=v_norm_pre, norm_post=v_norm_post, norm_mem=v_norm_mem, w_in=v_w_in, conv_w=v_conv_w,
             a_log=v_a_log, dt_bias=v_dt_bias, dn_norm=v_dn_norm, gm_norm=v_gm_norm, spatial_w=v_spatial_w,
             spatial_b=v_spatial_b, sinks=v_sinks, w_mem_kv=v_w_mem_kv, w_up=v_w_up, w_out=v_w_out)

    g_in, g_conv, g_kv, g_up, g_out = _allgather_chips(
        [w_in.astype(BF16), conv_w, w_mem_kv.astype(BF16), w_up.astype(BF16), w_out.astype(BF16)])
    Ws = []
    for l in range(N_LAYER):
        Ws.append(_layer_weights(
            l, jnp.concatenate([g_in[s, l] for s in range(N_CHIP)], axis=1),
            jnp.concatenate([g_conv[s, l] for s in range(N_CHIP)], axis=1),
            g_kv[:, l].reshape(D_MODEL, D_MODEL),
            g_up[:, l].transpose(1, 2, 0, 3).reshape(4, BRANCH_W, D_MODEL),
            g_out[:, l].reshape(D_MODEL, D_MODEL), w))

    loss, dx, grads = _local_step(x[0], mem[0], loss_target[0], Ws)
    loss = lax.psum(loss, ("x", "y", "c"))

    c = lax.axis_index("c")
    full = [dict(w_in=g["w_pad"], conv_w=g["conv"].reshape(4, 3 * BRANCH_W), w_mem_kv=g["w_mem_kv"], w_up=g["w_up"],
                 w_out=g["w_out"]) for g in grads]
    big = [jnp.stack([_by_chip(n, full[l][n]) for l in range(N_LAYER)]) for n in _BIG]
    theirs = _pair_exchange(big)
    pair = []
    for n, g, p in zip(_BIG, big, theirs):
        mine = lax.dynamic_index_in_dim(g, c, 0, keepdims=False)
        k = mine.shape[-1]
        pair.append(_ew("pair_sum_" + n, lambda a, b: [a + b], [mine.reshape(-1, k), p.reshape(-1, k)], 1)[0]
                    .reshape(mine.shape))
    landed = _chip_scatter(pair)
    totals = []
    for n, r in zip(_BIG, landed):
        k = r.shape[-1]
        r3 = r.reshape(N_CHIP, -1, k)
        totals.append(_ew("chip_sum_" + n, lambda a, b, c_, d: [((a + b) + c_) + d], [(r3, s) for s in range(N_CHIP)], 1)[0]
                      .reshape(r.shape[1:]))
    gbig = dict(zip(_BIG, _pair_share(totals)))

    small_local = dict(
        norm_pre=jnp.stack([g["norm_pre"][0] for g in grads]), norm_post=jnp.stack([g["norm_post"][0] for g in grads]),
        norm_mem=jnp.stack([g["norm_mem"][0] for g in grads]), a_log=jnp.stack([g["a_vec"][0, 4:8] for g in grads]),
        dt_bias=jnp.stack([g["dt_vec"][0, 4:8] for g in grads]), dn_norm=jnp.stack([g["dn_norm"][0] for g in grads]),
        gm_norm=jnp.stack([g["gm_norm"][0] for g in grads]), spatial_w=jnp.stack([g["spatial_w"] for g in grads]),
        spatial_b=jnp.stack([g["spatial_b"][:, 0, :] for g in grads]),
        sinks=jnp.stack([g["sink_vec"][0, :8] for g in grads]))
    gsmall_packed = _allreduce_small(_pack_small(small_local))

    d_s, m_s, v_s = _ew("adamw_small", _adamw_fn, [_pack_small(w), gsmall_packed, _pack_small(m), _pack_small(v)], 3)
    gsmall, dsmall, msmall, vsmall = (_unpack_small(p) for p in (gsmall_packed, d_s, m_s, v_s))
    g_o, d_o, m_o, v_o = dict(gsmall), dict(dsmall), dict(msmall), dict(vsmall)
    for n in _BIG:
        g_o[n] = gbig[n]
        d_o[n], m_o[n], v_o[n] = _adamw("adamw_" + n, w[n], gbig[n], m[n], v[n])
    return (loss, dx[None], *[g_o[n] for n in _NAMES], *[d_o[n] for n in _NAMES], *[m_o[n] for n in _NAMES],
            *[v_o[n] for n in _NAMES])
```

```python
import collections
import functools

import jax
import jax.numpy as jnp
from jax import lax
from jax.experimental import pallas as pl
from jax.experimental.pallas import tpu as pltpu

F32 = jnp.float32
BF16 = jnp.bfloat16

D_MODEL = 1024
BRANCH_W = 512
MEM_LEN = 256
N_LAYER = 2
N_CHIP = 4
N_DEV = 8
EPS = 1e-6
NEG_INF = -1e30
DN_CHUNK = 64
LANES = 128
VMEM_LIMIT = 48 * 1024 * 1024

ADAM_LR, ADAM_B1, ADAM_B2, ADAM_EPS, ADAM_WD, ADAM_STEP = 0.001, 0.9, 0.999, 1e-08, 0.01, 10

N_PAD = 10240
O_GATE = 0
O_AQ, O_AK, O_AV, O_AZ = 4096, 4608, 5120, 5632
O_BUV, O_BZ = 6144, 7168
O_CKV, O_BA = 7680, 7936
O_CQ, O_CZ = 8192, 8704
O_MQ, O_MZ = 9216, 9728
O_MISC, W_MISC = O_CKV, 512
_PAD_SEGS = [(5896, 4096), (0, 512), (512, 512), (1024, 512), (1536, 512), (2056, 1024), (3080, 512),
             (4104, 128), (4232, 128), (2048, 8), (None, 120), (None, 128),
             (3592, 512), (4360, 512), (4872, 512), (5384, 512)]
D_IN = 9992
SHARD_IN = D_IN // N_CHIP


def _w_pad_from_slabs(slabs):
    parts = []
    for s, n in _PAD_SEGS:
        if s is None:
            parts.append(jnp.zeros((slabs[0].shape[0], n), slabs[0].dtype))
            continue
        a = s
        while a < s + n:
            chip = a // SHARD_IN
            b = min(s + n, (chip + 1) * SHARD_IN)
            parts.append(slabs[chip][:, a - chip * SHARD_IN:b - chip * SHARD_IN])
            a = b
    return jnp.concatenate(parts, axis=1)


def _slab_from_pad(dw, chip):
    segs, off = [], 0
    for s, n in _PAD_SEGS:
        if s is not None:
            segs.append((s, off, n))
        off += n
    lo, hi = chip * SHARD_IN, (chip + 1) * SHARD_IN
    parts = []
    for s, o, n in sorted(segs):
        a, b = max(s, lo), min(s + n, hi)
        if a < b:
            parts.append(dw[:, o + a - s:o + b - s])
    return jnp.concatenate(parts, axis=1)


def _dot(a, b, dims, prec):
    if prec == "bf16":
        return lax.dot_general(a.astype(BF16), b.astype(BF16), (dims, ((), ())), preferred_element_type=F32)
    return lax.dot_general(a, b, (dims, ((), ())), precision=lax.Precision.HIGHEST, preferred_element_type=F32)


_NN, _NT, _TN = ((1,), (0,)), ((1,), (1,)), ((0,), (0,))


def _make_mm(prec):
    @jax.custom_vjp
    def nn(a, b):
        return _dot(a, b, _NN, prec)

    @jax.custom_vjp
    def nt(a, b):
        return _dot(a, b, _NT, prec)

    @jax.custom_vjp
    def tn(a, b):
        return _dot(a, b, _TN, prec)

    nn.defvjp(lambda a, b: (nn(a, b), (a, b)), lambda r, g: (nt(g, r[1]), tn(r[0], g)))
    nt.defvjp(lambda a, b: (nt(a, b), (a, b)), lambda r, g: (nn(g, r[1]), tn(g, r[0])))
    tn.defvjp(lambda a, b: (tn(a, b), (a, b)), lambda r, g: (nt(r[1], g), nn(r[0], g)))
    return nn, nt, tn


_nn16, _nt16, _tn16 = _make_mm("bf16")
_nn32, _nt32, _tn32 = _make_mm("f32")


def _make_slice(axis):
    @functools.partial(jax.custom_vjp, nondiff_argnums=(1, 2, 3))
    def sl(x, a, b, n):
        return x[a:b] if axis == 0 else x[:, a:b]

    def fwd(x, a, b, n):
        return sl(x, a, b, n), None

    def bwd(a, b, n, _, g):
        parts = []
        if a > 0:
            parts.append(jnp.zeros((a, g.shape[1]) if axis == 0 else (g.shape[0], a), g.dtype))
        parts.append(g)
        if n - b > 0:
            parts.append(jnp.zeros((n - b, g.shape[1]) if axis == 0 else (g.shape[0], n - b), g.dtype))
        return (jnp.concatenate(parts, axis=axis),)

    sl.defvjp(fwd, bwd)
    return sl


_sl0, _sl1 = _make_slice(0), _make_slice(1)


def _rowsl(x, a, b):
    return _sl0(x, a, b, x.shape[0])


def _cols(x, a, b):
    return _sl1(x, a, b, x.shape[1])


@functools.partial(jax.custom_vjp, nondiff_argnums=(1,))
def _rollr(x, s):
    return pltpu.roll(x, s, 0)


_rollr.defvjp(lambda x, s: (_rollr(x, s), None),
              lambda s, _, g: (pltpu.roll(g, g.shape[0] - s, 0),))


def _iota(shape, axis):
    return lax.broadcasted_iota(jnp.int32, shape, axis)


def _sigmoid(x):
    return lax.logistic(x)


def _silu(x):
    return x * _sigmoid(x)


def _gelu(x):
    return 0.5 * x * (1.0 + jnp.tanh(0.7978845608028654 * (x + 0.044715 * (x * x * x))))


def _softplus(x):
    return jnp.maximum(x, 0.0) + jnp.log(1.0 + jnp.exp(-jnp.abs(x)))


def _rms(x, g):
    return x * lax.rsqrt(jnp.mean(x * x, axis=-1, keepdims=True) + EPS) * g


def _lane_pick(x, lane):
    return jnp.sum(x * (_iota((1, x.shape[1]), 1) == lane).astype(F32), axis=1, keepdims=True)


Row = collections.namedtuple("Row", "arr w cb hb grad", defaults=(0, True))


def _full_spec(shape):
    return pl.BlockSpec(shape, lambda i, _n=len(shape): (0,) * _n)


def _load_params(refs):
    return [[p[g].astype(F32) for g in range(p.shape[0])] if len(p.shape) == 3 else p[...].astype(F32)
            for p in refs]


def _params(**kw):
    return pltpu.CompilerParams(vmem_limit_bytes=VMEM_LIMIT, **kw)


def _rows_fwd(name, fn, rows, params, outs, tr, carry=None):
    T = rows[0].arr.shape[0]
    n = T // tr
    halos = [r for r in rows if r.hb]
    nr, nh, npar, no = len(rows), len(halos), len(params), len(outs)

    def body(*refs):
        row_refs, halo_refs = refs[:nr], refs[nr:nr + nh]
        par_refs = refs[nr + nh:nr + nh + npar]
        out_refs = refs[nr + nh + npar:nr + nh + npar + no]
        rest = refs[nr + nh + npar + no:]
        first = pl.program_id(0) == 0
        cvals = None
        if carry is not None:
            csave_ref, carry_ref = rest

            @pl.when(first)
            def _():
                carry_ref[...] = jnp.zeros_like(carry_ref)

            cvals = [carry_ref[g] for g in range(carry[0])]
            for g in range(carry[0]):
                csave_ref[0, g] = cvals[g]
        c_out, o = fn(first, cvals, [r[...].astype(F32) for r in row_refs],
                      [h[...].astype(F32) for h in halo_refs], _load_params(par_refs))
        for r, v in zip(out_refs, o):
            r[...] = v.astype(r.dtype)
        if carry is not None:
            for g in range(carry[0]):
                carry_ref[g] = c_out[g]

    in_specs = [pl.BlockSpec((tr, r.w), lambda i, c=r.cb: (i, c)) for r in rows]
    in_specs += [pl.BlockSpec((r.hb, r.w), lambda i, c=r.cb, q=tr // r.hb: (jnp.maximum(i * q - 1, 0), c))
                 for r in halos]
    in_specs += [_full_spec(p.shape) for p in params]
    out_shape = [jax.ShapeDtypeStruct((T, w), dt) for w, dt in outs]
    out_specs = [pl.BlockSpec((tr, w), lambda i: (i, 0)) for w, _ in outs]
    scratch = []
    if carry is not None:
        out_shape.append(jax.ShapeDtypeStruct((n,) + carry, F32))
        out_specs.append(pl.BlockSpec((1,) + carry, lambda i: (i, 0, 0, 0)))
        scratch.append(pltpu.VMEM(carry, F32))
    return pl.pallas_call(
        body, name=name, grid=(n,), in_specs=in_specs, out_specs=out_specs, out_shape=out_shape,
        scratch_shapes=scratch, compiler_params=_params(dimension_semantics=("arbitrary",)),
    )(*[r.arr for r in rows], *[r.arr for r in halos], *params)


def _rows_bwd(name, fn, rows, params, douts, tr, carry=None, csave=None, dcols=None):
    T = rows[0].arr.shape[0]
    n = T // tr
    halos = [r for r in rows if r.hb]
    grows = [r for r in rows if r.grad is True]
    crows = [r for r in rows if r.grad == "cols"]
    wcols = sum(r.w for r in crows)
    nr, nh, npar, nd, ng = len(rows), len(halos), len(params), len(douts), len(grows)
    nc = 0 if carry is None else 1
    ncol = 1 if crows else 0
    nalias = 1 if (crows and dcols is not None) else 0

    def body(*refs):
        row_refs, halo_refs = refs[:nr], refs[nr:nr + nh]
        par_refs = refs[nr + nh:nr + nh + npar]
        k = nr + nh + npar
        csave_ref = refs[k] if nc else None
        dout_refs = refs[k + nc:k + nc + nd]
        k = k + nc + nd + nalias
        drow_refs = refs[k:k + ng]
        dcols_ref = refs[k + ng] if ncol else None
        dpar_refs = refs[k + ng + ncol:k + ng + ncol + npar]
        k = k + ng + ncol + npar
        dcarry_ref = refs[k] if nc else None
        hgrad_refs = refs[k + nc:]
        i = pl.program_id(0)
        first_tile = i == n - 1

        @pl.when(i == 0)
        def _():
            for r in dpar_refs:
                r[...] = jnp.zeros_like(r)
            for r in hgrad_refs:
                r[...] = jnp.zeros_like(r)
            if nc:
                dcarry_ref[...] = jnp.zeros_like(dcarry_ref)

        rv = [r[...].astype(F32) for r in row_refs]
        hv = [h[...].astype(F32) for h in halo_refs]
        pv = _load_params(par_refs)
        dov = [d[...].astype(F32) for d in dout_refs]
        if nc:
            cv = [csave_ref[0, g] for g in range(carry[0])]
            _, vjp = jax.vjp(lambda c, r, h, p: fn(first_tile, c, r, h, p), cv, rv, hv, pv)
            dc, dr, dh, dp = vjp(([dcarry_ref[g] for g in range(carry[0])], dov))
            for g in range(carry[0]):
                dcarry_ref[g] = dc[g]
        else:
            _, vjp = jax.vjp(lambda r, h, p: fn(first_tile, None, r, h, p)[1], rv, hv, pv)
            dr, dh, dp = vjp(dov)
        gi = hi = 0
        pieces = []
        for kk, r in enumerate(rows):
            d = dr[kk]
            if r.hb:
                carried = hgrad_refs[hi][...]
                d = d + (carried if tr == r.hb else
                         jnp.concatenate([jnp.zeros((tr - r.hb, r.w), F32), carried], axis=0))
                hgrad_refs[hi][...] = dh[hi]
                hi += 1
            if r.grad is True:
                drow_refs[gi][...] = d.astype(drow_refs[gi].dtype)
                gi += 1
            elif r.grad == "cols":
                pieces.append(d.astype(BF16))
        if ncol:
            dcols_ref[...] = pieces[0] if len(pieces) == 1 else jnp.concatenate(pieces, axis=1)
        for r, d in zip(dpar_refs, dp):
            if len(r.shape) == 3:
                for g in range(r.shape[0]):
                    r[g] += d[g]
            else:
                r[...] += d

    rev = lambda i: n - 1 - i
    in_specs = [pl.BlockSpec((tr, r.w), lambda i, c=r.cb: (rev(i), c)) for r in rows]
    in_specs += [pl.BlockSpec((r.hb, r.w), lambda i, c=r.cb, q=tr // r.hb: (jnp.maximum(rev(i) * q - 1, 0), c))
                 for r in halos]
    in_specs += [_full_spec(p.shape) for p in params]
    args = [r.arr for r in rows] + [r.arr for r in halos] + list(params)
    scratch = []
    if nc:
        in_specs.append(pl.BlockSpec((1,) + carry, lambda i: (rev(i), 0, 0, 0)))
        args.append(csave)
        scratch.append(pltpu.VMEM(carry, F32))
    in_specs += [pl.BlockSpec((tr, d.shape[1]), lambda i: (rev(i), 0)) for d in douts]
    args += list(douts)
    aliases = {}
    if nalias:
        aliases = {len(args): ng}
        in_specs.append(pl.BlockSpec(memory_space=pl.ANY))
        args.append(dcols)
    scratch += [pltpu.VMEM((r.hb, r.w), F32) for r in halos]
    out_shape = [jax.ShapeDtypeStruct((T, r.w), F32) for r in grows]
    out_specs = [pl.BlockSpec((tr, r.w), lambda i: (rev(i), 0)) for r in grows]
    if ncol:
        off = crows[0].cb * crows[0].w
        assert off % wcols == 0 and all(a.cb * a.w + a.w == b.cb * b.w for a, b in zip(crows, crows[1:]))
        out_shape.append(jax.ShapeDtypeStruct((T, N_PAD), BF16))
        out_specs.append(pl.BlockSpec((tr, wcols), lambda i, c=off // wcols: (rev(i), c)))
    out_shape += [jax.ShapeDtypeStruct(p.shape, F32) for p in params]
    out_specs += [_full_spec(p.shape) for p in params]
    res = pl.pallas_call(
        body, name=name, grid=(n,), in_specs=in_specs, out_specs=out_specs, out_shape=out_shape,
        scratch_shapes=scratch, input_output_aliases=aliases,
        compiler_params=_params(dimension_semantics=("arbitrary",)),
    )(*args)
    return list(res[:ng]), list(res[ng + ncol:]), (res[ng] if ncol else dcols)


def _fill_misc(dcols, dkv, dba, tr):
    T = dkv.shape[0]

    def body(kv_ref, ba_ref, _, o_ref):
        o_ref[...] = jnp.concatenate([kv_ref[...], ba_ref[...]], axis=1).astype(BF16)

    return pl.pallas_call(
        body, name="misc_bwd", grid=(T // tr,),
        in_specs=[pl.BlockSpec((tr, 256), lambda i: (i, 0)), pl.BlockSpec((tr, 256), lambda i: (i, 0)),
                  pl.BlockSpec(memory_space=pl.ANY)],
        out_specs=pl.BlockSpec((tr, W_MISC), lambda i: (i, O_MISC // W_MISC)),
        out_shape=jax.ShapeDtypeStruct((T, N_PAD), BF16), input_output_aliases={2: 0},
        compiler_params=_params(dimension_semantics=("arbitrary",)),
    )(dkv, dba, dcols)


def _matmul(name, a, b, kind, out_dtype, tm, tn, tk):
    if kind == "tn":
        (K, M), N = a.shape, b.shape[1]
    else:
        (M, K), N = a.shape, (b.shape[0] if kind == "nt" else b.shape[1])
    tm, tn, tk = min(tm, M), min(tn, N), min(tk, K)
    nk = K // tk
    dims = {"nn": _NN, "nt": _NT, "tn": _TN}[kind]

    def body(a_ref, b_ref, o_ref, acc_ref):
        k = pl.program_id(2)

        @pl.when(k == 0)
        def _():
            acc_ref[...] = jnp.zeros_like(acc_ref)

        acc_ref[...] += lax.dot_general(a_ref[...], b_ref[...], (dims, ((), ())), preferred_element_type=F32)

        @pl.when(k == nk - 1)
        def _():
            o_ref[...] = acc_ref[...].astype(o_ref.dtype)

    a_spec = pl.BlockSpec((tk, tm), lambda i, j, k: (k, i)) if kind == "tn" else pl.BlockSpec((tm, tk), lambda i, j, k: (i, k))
    b_spec = pl.BlockSpec((tn, tk), lambda i, j, k: (j, k)) if kind == "nt" else pl.BlockSpec((tk, tn), lambda i, j, k: (k, j))
    return pl.pallas_call(
        body, name=name, grid=(M // tm, N // tn, nk), in_specs=[a_spec, b_spec],
        out_specs=pl.BlockSpec((tm, tn), lambda i, j, k: (i, j)),
        out_shape=jax.ShapeDtypeStruct((M, N), out_dtype),
        scratch_shapes=[pltpu.VMEM((tm, tn), F32)],
        compiler_params=_params(dimension_semantics=("arbitrary", "arbitrary", "arbitrary")),
    )(a, b)


def _pre_fn(first, _, rows, halos, params):
    return None, [_rms(rows[0], params[0])]


def _pre_fn_res(first, _, rows, halos, params):
    return None, [_rms(rows[0], params[0]), rows[0]]


def _memkv_fn(first, _, rows, halos, params):
    g, w = params
    return None, [_nn16(_rms(rows[0], g), w)]


def _conv_silu(x, halo, w4, keep_halo):
    tr = x.shape[0]
    halo = halo * keep_halo
    rid = _iota((tr, 1), 0)
    acc = w4[3] * x
    for s in (1, 2, 3):
        hs = jnp.concatenate([_rollr(halo, s), jnp.zeros((tr - halo.shape[0], x.shape[1]), F32)], axis=0)
        acc = acc + w4[3 - s] * jnp.where(rid < s, hs, _rollr(x, s))
    return _silu(acc)


def _dn_fn(first, S, rows, halos, params):
    qp, kp, vp, z, ba = rows
    conv, a_vec, dt_vec, dnorm = params
    ba = _cols(ba, 0, LANES)
    tr = qp.shape[0]
    keep = jnp.where(first, 0.0, 1.0)
    q = _conv_silu(qp, halos[0], [conv[3 * j + 0] for j in range(4)], keep)
    k = _conv_silu(kp, halos[1], [conv[3 * j + 1] for j in range(4)], keep)
    v = _conv_silu(vp, halos[2], [conv[3 * j + 2] for j in range(4)], keep)
    qh, kh, vh = [], [], []
    for h in range(4):
        a, b = h * LANES, (h + 1) * LANES
        xq, xk = _cols(q, a, b), _cols(k, a, b)
        qh.append(xq * lax.rsqrt(jnp.sum(xq * xq, axis=1, keepdims=True) + EPS) * (LANES ** -0.5))
        kh.append(xk * lax.rsqrt(jnp.sum(xk * xk, axis=1, keepdims=True) + EPS))
        vh.append(_cols(v, a, b))
    beta_all = _sigmoid(ba)
    g_all = -jnp.exp(a_vec) * _softplus(ba + dt_vec)
    C = DN_CHUNK
    ii, jj = _iota((C, C), 0), _iota((C, C), 1)
    strict, incl = ii > jj, ii >= jj
    eye = (ii == jj).astype(F32)
    last_row = (_iota((C, 1), 0) == C - 1).astype(F32)
    S = list(S)
    ychunks = []
    for c in range(tr // C):
        r0, r1 = c * C, (c + 1) * C
        gcs = _nn32(incl.astype(F32), _rowsl(g_all, r0, r1))
        bch = _rowsl(beta_all, r0, r1)
        zc = _rowsl(z, r0, r1)
        yh = []
        for h in range(4):
            qc, kc, vc = _rowsl(qh[h], r0, r1), _rowsl(kh[h], r0, r1), _rowsl(vh[h], r0, r1)
            beta = _lane_pick(bch, h)
            gc = _lane_pick(gcs, 4 + h)
            gc_row = jnp.sum(eye * gc, axis=0, keepdims=True)
            dec = jnp.exp(jnp.where(incl, gc - gc_row, 0.0))
            kb = kc * beta
            P = -jnp.where(strict, _nt16(kb, kc) * dec, 0.0)
            tinv = eye + P
            for _ in range(5):
                P = _nn16(P, P)
                tinv = tinv + _nn16(tinv, P)
            egc = jnp.exp(gc)
            u = _nn16(tinv, vc * beta)
            w = _nn16(tinv, kb * egc)
            aqk = jnp.where(incl, _nt16(qc, kc) * dec, 0.0)
            glast = jnp.sum(gc * last_row, axis=0, keepdims=True)
            vnew = u - _nn16(w, S[h])
            o = _nn16(qc * egc, S[h]) + _nn16(aqk, vnew)
            S[h] = S[h] * jnp.exp(glast) + _tn16(kc * jnp.exp(glast - gc), vnew)
            yh.append(_rms(o, dnorm) * _silu(_cols(zc, h * LANES, (h + 1) * LANES)))
        ychunks.append(jnp.concatenate(yh, axis=1))
    return S, [jnp.concatenate(ychunks, axis=0)]


def _gm_fn(first, _, rows, halos, params):
    uv, z = rows
    gnorm, ws, bs = params
    tr = uv.shape[0]
    guv = _gelu(uv)
    u = _cols(guv, 0, BRANCH_W)
    v = _rms(_cols(guv, BRANCH_W, 2 * BRANCH_W), gnorm)
    ii, jj = _iota((LANES, LANES), 0), _iota((LANES, LANES), 1)
    eye = (ii == jj).astype(F32)
    wsm = [jnp.where(ii >= jj, ws[g], 0.0) for g in range(4)]
    bcol = [jnp.sum(eye * bs[g], axis=1, keepdims=True) for g in range(4)]
    chunks = []
    for c in range(tr // LANES):
        vc = _rowsl(v, c * LANES, (c + 1) * LANES)
        chunks.append(jnp.concatenate(
            [_nn16(wsm[g], _cols(vc, g * LANES, (g + 1) * LANES)) + bcol[g] for g in range(4)], axis=1))
    return None, [u * jnp.concatenate(chunks, axis=0) * _silu(z)]


def _swa_fn(first, _, rows, halos, params):
    q, kvc, z = rows
    sink_vec = params[0]
    P = LANES
    kv = jnp.concatenate([halos[0], kvc], axis=0)
    k, v = _cols(kv, 0, P), _cols(kv, P, 2 * P)
    r, cc = _iota((P, P), 0), _iota((P, P), 1)
    lane = _iota((1, P), 1)
    dist = _iota((P, 2 * P), 0) + P - _iota((P, 2 * P), 1)
    kmin = jnp.where(first, P, 0)
    valid = (dist >= 0) & (dist < P) & (_iota((P, 2 * P), 1) >= kmin)
    blocks = [None] * 4
    for kh in range(2):
        dup = (r == kh * 64 + (cc & 63)).astype(F32)
        kk, vv = _nn16(k, dup), _nn16(v, dup)
        for g in range(4):
            h = kh * 4 + g
            half = ((lane >= 64) == (h % 2 == 1)).astype(F32)
            qb = _cols(q, (h // 2) * P, (h // 2 + 1) * P) * half
            s = jnp.where(valid, _nt16(qb, kk) * 0.125, NEG_INF)
            sink = _lane_pick(sink_vec, h)
            m = lax.stop_gradient(jnp.maximum(jnp.max(s, axis=1, keepdims=True), sink))
            e = jnp.exp(s - m)
            p = e / (jnp.sum(e, axis=1, keepdims=True) + jnp.exp(sink - m))
            o = _nn16(p, vv) * half
            blocks[h // 2] = o if blocks[h // 2] is None else blocks[h // 2] + o
    return None, [jnp.concatenate(blocks, axis=1) * _silu(z)]


def _mem_fn(first, _, rows, halos, params):
    q, z = rows
    mkv = params[0]
    outs = []
    for h in range(4):
        a, b = h * LANES, (h + 1) * LANES
        s = _nt16(_cols(q, a, b), _cols(mkv, a, b)) * (LANES ** -0.5)
        m = lax.stop_gradient(jnp.max(s, axis=1, keepdims=True))
        e = jnp.exp(s - m)
        p = e / jnp.sum(e, axis=1, keepdims=True)
        outs.append(_nn16(p, _cols(mkv, BRANCH_W + a, BRANCH_W + b)))
    return None, [jnp.concatenate(outs, axis=1) * _silu(z)]


def _up_fn(first, _, rows, halos, params):
    y, gl = rows
    return None, [_sigmoid(gl) * _nn16(y, params[0])]


def _out_fn(first, _, rows, halos, params):
    x, m0, m1, m2, m3 = rows
    w, g = params
    return None, [x + _rms(_nn16(m0 + m1 + m2 + m3, w), g)]


def _loss_fn(first, _, rows, halos, params):
    y, t = rows
    d = y - t
    lrow = 0.5 * jnp.mean(d * d, axis=1, keepdims=True)
    return None, [d * (1.0 / D_MODEL), jnp.broadcast_to(lrow, (y.shape[0], LANES))]


TR = 256
DN_TR = 256
CARRY = (4, LANES, LANES)


def _branch_rows(cols, g):
    a = [Row(cols, 512, O_AQ // 512, 8, g), Row(cols, 512, O_AK // 512, 8, g), Row(cols, 512, O_AV // 512, 8, g),
         Row(cols, 512, O_AZ // 512, 0, g), Row(cols, 256, O_BA // 256)]
    b = [Row(cols, 1024, O_BUV // 1024, 0, g), Row(cols, 512, O_BZ // 512, 0, g)]
    c = [Row(cols, 512, O_CQ // 512, 0, g), Row(cols, 256, O_CKV // 256, LANES), Row(cols, 512, O_CZ // 512, 0, g)]
    m = [Row(cols, 512, O_MQ // 512, 0, g), Row(cols, 512, O_MZ // 512, 0, g)]
    return a, b, c, m


def _layer_fwd(x, mem, W):
    h = _rows_fwd("prenorm_fwd", _pre_fn, [Row(x, D_MODEL, 0)], [W["norm_pre"]], [(D_MODEL, BF16)], TR)[0]
    cols = _matmul("in_proj_fwd", h, W["w_pad"], "nn", F32, 1024, 512, 1024)
    mem_kv = _rows_fwd("memkv_fwd", _memkv_fn, [Row(mem, D_MODEL, 0)], [W["norm_mem"], W["w_mem_kv"]],
                       [(D_MODEL, F32)], MEM_LEN)[0]
    ra, rb, rc, rm = _branch_rows(cols, True)
    y_a, csave = _rows_fwd("dn_fwd", _dn_fn, ra, [W["conv"], W["a_vec"], W["dt_vec"], W["dn_norm"]],
                           [(BRANCH_W, F32)], DN_TR, CARRY)
    y_b = _rows_fwd("gm_fwd", _gm_fn, rb, [W["gm_norm"], W["spatial_w"], W["spatial_b"]], [(BRANCH_W, F32)], TR)[0]
    y_c = _rows_fwd("swa_fwd", _swa_fn, rc, [W["sink_vec"]], [(BRANCH_W, F32)], LANES)[0]
    y_m = _rows_fwd("mem_fwd", _mem_fn, rm, [mem_kv], [(BRANCH_W, F32)], TR)[0]
    ys = [y_a, y_b, y_c, y_m]
    ms = [_rows_fwd("up_fwd", _up_fn, [Row(ys[n], BRANCH_W, 0), Row(cols, D_MODEL, n)], [W["w_up"][n]],
                    [(D_MODEL, F32)], TR)[0] for n in range(4)]
    x_new = _rows_fwd("out_fwd", _out_fn, [Row(x, D_MODEL, 0)] + [Row(m, D_MODEL, 0) for m in ms],
                      [W["w_out"], W["norm_post"]], [(D_MODEL, F32)], TR)[0]
    return x_new, dict(x=x, h=h, cols=cols, mem_kv=mem_kv, csave=csave, ys=ys, ms=ms)


def _layer_bwd(dxn, mem, W, sv):
    x, cols = sv["x"], sv["cols"]
    (dx_res, dm), (dw_out, dnorm_post), _ = _rows_bwd(
        "out_bwd", _out_fn, [Row(x, D_MODEL, 0), Row(sv["ms"][0], D_MODEL, 0)]
        + [Row(m, D_MODEL, 0, 0, False) for m in sv["ms"][1:]], [W["w_out"], W["norm_post"]], [dxn], TR)
    dys, dw_up, dcols = [], [], None
    for n in range(4):
        (dy,), (dwu,), dcols = _rows_bwd(
            "up_bwd", _up_fn, [Row(sv["ys"][n], BRANCH_W, 0), Row(cols, D_MODEL, n, 0, "cols")], [W["w_up"][n]],
            [dm], TR, dcols=dcols)
        dys.append(dy), dw_up.append(dwu)
    ra, rb, rc, rm = _branch_rows(cols, "cols")
    (dba,), (dconv, da_vec, ddt_vec, ddn_norm), dcols = _rows_bwd(
        "dn_bwd", _dn_fn, ra, [W["conv"], W["a_vec"], W["dt_vec"], W["dn_norm"]], [dys[0]], DN_TR, CARRY,
        sv["csave"], dcols=dcols)
    _, (dgm_norm, dws, dbs), dcols = _rows_bwd(
        "gm_bwd", _gm_fn, rb, [W["gm_norm"], W["spatial_w"], W["spatial_b"]], [dys[1]], TR, dcols=dcols)
    (dkv_c,), (dsink,), dcols = _rows_bwd("swa_bwd", _swa_fn, rc, [W["sink_vec"]], [dys[2]], LANES, dcols=dcols)
    _, (dmem_kv,), dcols = _rows_bwd("mem_bwd", _mem_fn, rm, [sv["mem_kv"]], [dys[3]], TR, dcols=dcols)
    dcols = _fill_misc(dcols, dkv_c, dba, TR)
    _, (dnorm_mem, dw_mem_kv), _ = _rows_bwd("memkv_bwd", _memkv_fn, [Row(mem, D_MODEL, 0, 0, False)],
                                             [W["norm_mem"], W["w_mem_kv"]], [dmem_kv], MEM_LEN)
    dw_pad = _matmul("in_proj_dw", sv["h"], dcols, "tn", F32, 1024, 1024, 1024)
    dh = _matmul("in_proj_dx", dcols, W["w_pad"], "nt", F32, 1024, 1024, 1024)
    (dx,), (dnorm_pre,), _ = _rows_bwd("prenorm_bwd", _pre_fn_res, [Row(x, D_MODEL, 0)], [W["norm_pre"]],
                                       [dh, dx_res], TR)
    grads = dict(norm_pre=dnorm_pre, norm_post=dnorm_post, norm_mem=dnorm_mem, w_pad=dw_pad, conv=dconv,
                 a_vec=da_vec, dt_vec=ddt_vec, dn_norm=ddn_norm, gm_norm=dgm_norm, spatial_w=dws, spatial_b=dbs,
                 sink_vec=dsink, w_mem_kv=dw_mem_kv, w_up=dw_up, w_out=dw_out)
    return dx, grads


def _lane_vec(v, off):
    return jnp.zeros((1, LANES), F32).at[0, off:off + v.shape[0]].set(v)


def _layer_weights(l, w_pad, conv_w, w_mem_kv, w_up, w_out, small):
    return dict(
        w_pad=w_pad, conv=conv_w.reshape(4, 3, BRANCH_W).reshape(12, 1, BRANCH_W),
        w_mem_kv=w_mem_kv, w_up=[w_up[n] for n in range(4)], w_out=w_out,
        norm_pre=small["norm_pre"][l][None], norm_post=small["norm_post"][l][None],
        norm_mem=small["norm_mem"][l][None],
        a_vec=_lane_vec(small["a_log"][l], 4), dt_vec=_lane_vec(small["dt_bias"][l], 4),
        dn_norm=small["dn_norm"][l][None], gm_norm=small["gm_norm"][l][None],
        spatial_w=small["spatial_w"][l], spatial_b=small["spatial_b"][l][:, None, :],
        sink_vec=_lane_vec(small["sinks"][l], 0))


def _local_step(x, mem, target, Ws):
    saved = []
    for W in Ws:
        x, sv = _layer_fwd(x, mem, W)
        saved.append(sv)
    dy, lrows = _rows_fwd("loss", _loss_fn, [Row(x, D_MODEL, 0), Row(target, D_MODEL, 0)], [],
                          [(D_MODEL, F32), (LANES, F32)], TR)
    loss = jnp.sum(lrows[:, 0])
    grads = []
    for W, sv in zip(reversed(Ws), reversed(saved)):
        dy, g = _layer_bwd(dy, mem, W, sv)
        grads.append(g)
    return loss, dy, grads[::-1]


_MESH = pl.DeviceIdType.MESH
_ANY = pl.BlockSpec(memory_space=pl.ANY)


def _position():
    return lax.axis_index("x"), lax.axis_index("y"), lax.axis_index("c")


def _remote(src, dst, send_sem, recv_sem, dev):
    return pltpu.make_async_remote_copy(src_ref=src, dst_ref=dst, send_sem=send_sem, recv_sem=recv_sem,
                                        device_id=dev, device_id_type=_MESH)


def _hbm_call(name, body, arrs, out_shapes, sems, aliases=None):
    return pl.pallas_call(
        body, name=name, in_specs=[_ANY] * len(arrs), out_specs=[_ANY] * len(out_shapes), out_shape=out_shapes,
        scratch_shapes=[pltpu.SemaphoreType.DMA((k,)) for k in sems], input_output_aliases=aliases or {},
        compiler_params=pltpu.CompilerParams(has_side_effects=True),
    )(*arrs)


def _other_chips(x, y):
    return [(1 - x, y), (x, 1 - y), (1 - x, 1 - y)]


def _gather_weights(arrs):
    n = len(arrs)

    def body(*refs):
        ins, outs = refs[:n], refs[n:2 * n]
        ici_send, ici_recv, d2d_send, d2d_recv = refs[2 * n:]
        x, y, c = _position()
        me = 2 * x + y
        chips = _other_chips(x, y)
        sends = []
        for a in range(n):
            for j, (px, py) in enumerate(chips):
                sends.append(_remote(ins[a].at[c], outs[a].at[c, me], ici_send.at[3 * a + j], ici_recv.at[3 * a + j],
                                     (px, py, c)))
                sends[-1].start()
        for a in range(n):
            for j, (px, py) in enumerate(chips):
                slab = outs[a].at[c, 2 * px + py]
                _remote(ins[a].at[c], slab, ici_send.at[3 * a + j], ici_recv.at[3 * a + j], (px, py, c)).wait_recv()
                sends.append(_remote(slab, slab, d2d_send.at[3 * a + j], d2d_recv.at[3 * a + j], (x, y, 1 - c)))
                sends[-1].start()
        for a in range(n):
            for j, (px, py) in enumerate(chips):
                slab = outs[a].at[1 - c, 2 * px + py]
                _remote(slab, slab, d2d_send.at[3 * a + j], d2d_recv.at[3 * a + j], (x, y, 1 - c)).wait_recv()
        for cp in sends:
            cp.wait_send()

    return _hbm_call("gather_weights", body, arrs,
                     [jax.ShapeDtypeStruct((N_LAYER, N_CHIP) + a.shape[1:], a.dtype) for a in arrs], [3 * n] * 4)


def _pair_exchange(arrs):
    n = len(arrs)

    def body(*refs):
        ins, outs = refs[:n], refs[n:2 * n]
        send_sems, recv_sems = refs[2 * n:]
        x, y, c = _position()
        cps = [_remote(ins[a].at[1 - c], outs[a], send_sems.at[a], recv_sems.at[a], (x, y, 1 - c)) for a in range(n)]
        for cp in cps:
            cp.start()
        for cp in cps:
            cp.wait_recv()
        for cp in cps:
            cp.wait_send()

    return _hbm_call("pair_exchange", body, arrs, [jax.ShapeDtypeStruct(a.shape[1:], a.dtype) for a in arrs], [n, n])


def _chip_scatter(arrs):
    n = len(arrs)

    def body(*refs):
        ins, outs = refs[:n], refs[n:2 * n]
        send_sems, recv_sems = refs[2 * n:]
        x, y, c = _position()
        me = 2 * x + y
        sends = []
        for a in range(n):
            for j, (px, py) in enumerate(_other_chips(x, y)):
                sends.append(_remote(ins[a].at[2 * px + py], outs[a].at[me], send_sems.at[3 * a + j],
                                     recv_sems.at[3 * a + j], (px, py, c)))
                sends[-1].start()
        for a in range(n):
            for j, (px, py) in enumerate(_other_chips(x, y)):
                _remote(ins[a].at[me], outs[a].at[2 * px + py], send_sems.at[3 * a + j], recv_sems.at[3 * a + j],
                        (px, py, c)).wait_recv()
        for cp in sends:
            cp.wait_send()

    return _hbm_call("chip_scatter", body, arrs, [jax.ShapeDtypeStruct(a.shape, a.dtype) for a in arrs],
                     [3 * n, 3 * n])


def _pair_share(arrs):
    n = len(arrs)

    def body(*refs):
        ins, outs = refs[:n], refs[n:2 * n]
        send_sems, recv_sems = refs[2 * n:]
        x, y, c = _position()
        cps = [_remote(ins[a].at[c], outs[a].at[c], send_sems.at[a], recv_sems.at[a], (x, y, 1 - c)) for a in range(n)]
        for cp in cps:
            cp.start()
        for a in range(n):
            _remote(ins[a].at[c], outs[a].at[1 - c], send_sems.at[a], recv_sems.at[a], (x, y, 1 - c)).wait_recv()
        for cp in cps:
            cp.wait_send()

    return _hbm_call("pair_share", body, arrs, [jax.ShapeDtypeStruct(a.shape, a.dtype) for a in arrs], [n, n],
                     {a: a for a in range(n)})


def _allreduce_small(g):
    def body(g_ref, o_ref, buf, send_sems, recv_sems):
        x, y, c = _position()
        me = 4 * x + 2 * y + c
        buf[me] = g_ref[...]
        peers = []
        for j in range(1, N_DEV):
            px = 1 - x if j & 4 else x
            py = 1 - y if j & 2 else y
            pc = 1 - c if j & 1 else c
            peers.append((px, py, pc))
        sends = [_remote(g_ref, buf.at[me], send_sems.at[j], recv_sems.at[j], p) for j, p in enumerate(peers)]
        for cp in sends:
            cp.start()
        for j, (px, py, pc) in enumerate(peers):
            _remote(g_ref, buf.at[4 * px + 2 * py + pc], send_sems.at[j], recv_sems.at[j], (px, py, pc)).wait_recv()
        for cp in sends:
            cp.wait_send()
        acc = buf[0]
        for s in range(1, N_DEV):
            acc = acc + buf[s]
        o_ref[...] = acc

    vmem = pl.BlockSpec(memory_space=pltpu.VMEM)
    return pl.pallas_call(
        body, name="allreduce_small", in_specs=[vmem], out_specs=vmem, out_shape=jax.ShapeDtypeStruct(g.shape, F32),
        scratch_shapes=[pltpu.VMEM((N_DEV,) + g.shape, F32), pltpu.SemaphoreType.DMA((N_DEV - 1,)),
                        pltpu.SemaphoreType.DMA((N_DEV - 1,))],
        compiler_params=_params(),
    )(g)


EW_ROWS = 128


def _ew(name, fn, ins, n_out, out_dtype=F32, out_core_slot=False):
    def dims(a):
        return a[0].shape[1:] if isinstance(a, tuple) else a.shape

    R, w = dims(ins[0])
    tr = EW_ROWS if R % EW_ROWS == 0 else R

    def body(c_ref, *refs):
        outs = fn(*[r[...] for r in refs[:len(ins)]])
        for r, v in zip(refs[len(ins):], outs):
            r[...] = v.astype(r.dtype)

    def lead_spec(l):
        if l == "c":
            return pl.BlockSpec((None, tr, w), lambda i, c_ref: (c_ref[0], i, 0))
        return pl.BlockSpec((None, tr, w), lambda i, c_ref, s=l: (s, i, 0))

    plain = pl.BlockSpec((tr, w), lambda i, c_ref: (i, 0))
    in_specs = [lead_spec(a[1]) if isinstance(a, tuple) else plain for a in ins]
    out_spec = lead_spec("c") if out_core_slot else plain
    out_shape = jax.ShapeDtypeStruct((2, R, w) if out_core_slot else (R, w), out_dtype)
    return pl.pallas_call(
        body, name=name,
        grid_spec=pltpu.PrefetchScalarGridSpec(num_scalar_prefetch=1, grid=(R // tr,), in_specs=in_specs,
                                               out_specs=[out_spec] * n_out),
        out_shape=[out_shape] * n_out, compiler_params=_params(dimension_semantics=("arbitrary",)),
    )(lax.axis_index("c").astype(jnp.int32).reshape(1), *[a[0] if isinstance(a, tuple) else a for a in ins])


def _adamw_fn(w, g, m, v):
    m = ADAM_B1 * m + (1.0 - ADAM_B1) * g
    v = ADAM_B2 * v + (1.0 - ADAM_B2) * (g * g)
    m_hat = m / (1.0 - ADAM_B1 ** ADAM_STEP)
    v_hat = v / (1.0 - ADAM_B2 ** ADAM_STEP)
    delta = -ADAM_LR * (m_hat / (jnp.sqrt(v_hat) + ADAM_EPS) + ADAM_WD * w)
    return delta, m, v


def _adamw(name, w, g, m, v):
    shape = w.shape
    two = lambda a: a.reshape(-1, shape[-1])
    return [o.reshape(shape) for o in _ew(name, _adamw_fn, [two(w), two(g), two(m), two(v)], 3)]


_SMALL = [("norm_pre", (2, 1024)), ("norm_post", (2, 1024)), ("norm_mem", (2, 1024)), ("a_log", (2, 4)),
          ("dt_bias", (2, 4)), ("dn_norm", (2, 128)), ("gm_norm", (2, 512)), ("spatial_w", (2, 4, 128, 128)),
          ("spatial_b", (2, 4, 128)), ("sinks", (2, 8))]
_SMALL_ROWS = 144
_BIG = ["w_in", "conv_w", "w_mem_kv", "w_up", "w_out"]
_NAMES = ["norm_pre", "norm_post", "norm_mem", "w_in", "conv_w", "a_log", "dt_bias", "dn_norm", "gm_norm",
          "spatial_w", "spatial_b", "sinks", "w_mem_kv", "w_up", "w_out"]


def _size(shape):
    n = 1
    for s in shape:
        n *= s
    return n


def _pack_small(d):
    rows = []
    for n, shp in _SMALL:
        flat = d[n].reshape(-1)
        rows.append(jnp.pad(flat, (0, -flat.shape[0] % 1024)).reshape(-1, 1024))
    used = sum(r.shape[0] for r in rows)
    return jnp.concatenate(rows + [jnp.zeros((_SMALL_ROWS - used, 1024), F32)], axis=0)


def _unpack_small(p):
    out, off = {}, 0
    for n, shp in _SMALL:
        k = -(-_size(shp) // 1024)
        out[n] = p[off:off + k].reshape(-1)[:_size(shp)].reshape(shp)
        off += k
    return out


def _by_chip(name, g):
    if name == "w_in":
        g = jnp.stack([_slab_from_pad(g, s) for s in range(N_CHIP)])
    elif name == "conv_w":
        g = g.reshape(4, N_CHIP, 3 * BRANCH_W // N_CHIP).transpose(1, 0, 2)
    elif name == "w_up":
        g = jnp.stack(g).reshape(4, BRANCH_W, N_CHIP, D_MODEL // N_CHIP).transpose(2, 0, 1, 3)
    else:
        g = g.reshape(N_CHIP, D_MODEL // N_CHIP, D_MODEL)
    return g.astype(BF16)


def _own_slot(buf, mine, me, axis):
    return lax.dynamic_update_index_in_dim(buf, mine.astype(buf.dtype), me, axis)


def kernel(x, mem, norm_pre, norm_post, norm_mem, w_in, conv_w, a_log, dt_bias, dn_norm, gm_norm, spatial_w, spatial_b, sinks, w_mem_kv, w_up, w_out, loss_target, m_norm_pre, m_norm_post, m_norm_mem, m_w_in, m_conv_w, m_a_log, m_dt_bias, m_dn_norm, m_gm_norm, m_spatial_w, m_spatial_b, m_sinks, m_w_mem_kv, m_w_up, m_w_out, v_norm_pre, v_norm_post, v_norm_mem, v_w_in, v_conv_w, v_a_log, v_dt_bias, v_dn_norm, v_gm_norm, v_spatial_w, v_spatial_b, v_sinks, v_w_mem_kv, v_w_up, v_w_out):
    w = dict(norm_pre=norm_pre, norm_post=norm_post, norm_mem=norm_mem, w_in=w_in, conv_w=conv_w, a_log=a_log,
             dt_bias=dt_bias, dn_norm=dn_norm, gm_norm=gm_norm, spatial_w=spatial_w, spatial_b=spatial_b, sinks=sinks,
             w_mem_kv=w_mem_kv, w_up=w_up, w_out=w_out)
    m = dict(norm_pre=m_norm_pre, norm_post=m_norm_post, norm_mem=m_norm_mem, w_in=m_w_in, conv_w=m_conv_w,
             a_log=m_a_log, dt_bias=m_dt_bias, dn_norm=m_dn_norm, gm_norm=m_gm_norm, spatial_w=m_spatial_w,
             spatial_b=m_spatial_b, sinks=m_sinks, w_mem_kv=m_w_mem_kv, w_up=m_w_up, w_out=m_w_out)
    v = dict(norm_pre=v_norm_pre, norm_post=v_norm_post, norm_mem=v_norm_mem, w_in=v_w_in, conv_w=v_conv_w,
             a_log=v_a_log, dt_bias=v_dt_bias, dn_norm=v_dn_norm, gm_norm=v_gm_norm, spatial_w=v_spatial_w,
             spatial_b=v_spatial_b, sinks=v_sinks, w_mem_kv=v_w_mem_kv, w_up=v_w_up, w_out=v_w_out)
    me = 2 * lax.axis_index("x") + lax.axis_index("y")

    local = [w_in.astype(BF16), conv_w, w_mem_kv.astype(BF16), w_up.astype(BF16), w_out.astype(BF16)]
    g_in, g_conv, g_kv, g_up, g_out = [_own_slot(g, loc[:, None], me, 1)
                                       for g, loc in zip(_gather_weights(local), local)]
    Ws = []
    for l in range(N_LAYER):
        Ws.append(_layer_weights(
            l, _w_pad_from_slabs([g_in[l, s] for s in range(N_CHIP)]),
            jnp.concatenate([g_conv[l, s] for s in range(N_CHIP)], axis=1),
            g_kv[l].reshape(D_MODEL, D_MODEL),
            g_up[l].transpose(1, 2, 0, 3).reshape(4, BRANCH_W, D_MODEL),
            g_out[l].reshape(D_MODEL, D_MODEL), w))

    loss, dx, grads = _local_step(x[0], mem[0], loss_target[0], Ws)
    loss = lax.psum(loss, ("x", "y", "c"))

    full = [dict(w_in=g["w_pad"], conv_w=g["conv"].reshape(4, 3 * BRANCH_W), w_mem_kv=g["w_mem_kv"], w_up=g["w_up"],
                 w_out=g["w_out"]) for g in grads]
    big = [jnp.stack([_by_chip(n, full[l][n]) for l in range(N_LAYER)]) for n in _BIG]
    theirs = _pair_exchange(big)
    add2 = lambda a, b: [a.astype(F32) + b.astype(F32)]
    pair = []
    for n, g, p in zip(_BIG, big, theirs):
        k = g.shape[-1]
        pair.append(_ew("pair_sum_" + n, add2, [(g.reshape(N_LAYER, -1, k), "c"), p.reshape(-1, k)], 1, BF16)[0]
                    .reshape(p.shape))
    landed = [_own_slot(r, lax.dynamic_index_in_dim(q, me, 0), me, 0) for r, q in zip(_chip_scatter(pair), pair)]
    add4 = lambda a, b, c_, d: [((a.astype(F32) + b.astype(F32)) + c_.astype(F32)) + d.astype(F32)]
    totals = []
    for n, r in zip(_BIG, landed):
        k = r.shape[-1]
        totals.append(_ew("chip_sum_" + n, add4, [(r.reshape(N_CHIP, -1, k), s) for s in range(N_CHIP)], 1,
                          out_core_slot=True)[0].reshape((N_LAYER,) + r.shape[1:]))
    gbig = dict(zip(_BIG, _pair_share(totals)))

    small_local = dict(
        norm_pre=jnp.stack([g["norm_pre"][0] for g in grads]), norm_post=jnp.stack([g["norm_post"][0] for g in grads]),
        norm_mem=jnp.stack([g["norm_mem"][0] for g in grads]), a_log=jnp.stack([g["a_vec"][0, 4:8] for g in grads]),
        dt_bias=jnp.stack([g["dt_vec"][0, 4:8] for g in grads]), dn_norm=jnp.stack([g["dn_norm"][0] for g in grads]),
        gm_norm=jnp.stack([g["gm_norm"][0] for g in grads]), spatial_w=jnp.stack([g["spatial_w"] for g in grads]),
        spatial_b=jnp.stack([g["spatial_b"][:, 0, :] for g in grads]),
        sinks=jnp.stack([g["sink_vec"][0, :8] for g in grads]))
    gsmall_packed = _allreduce_small(_pack_small(small_local))

    d_s, m_s, v_s = _ew("adamw_small", _adamw_fn, [_pack_small(w), gsmall_packed, _pack_small(m), _pack_small(v)], 3)
    gsmall, dsmall, msmall, vsmall = (_unpack_small(p) for p in (gsmall_packed, d_s, m_s, v_s))
    g_o, d_o, m_o, v_o = dict(gsmall), dict(dsmall), dict(msmall), dict(vsmall)
    for n in _BIG:
        g_o[n] = gbig[n]
        d_o[n], m_o[n], v_o[n] = _adamw("adamw_" + n, w[n], gbig[n], m[n], v[n])
    return (loss, dx[None], *[g_o[n] for n in _NAMES], *[d_o[n] for n in _NAMES], *[m_o[n] for n in _NAMES],
            *[v_o[n] for n in _NAMES])
```

```python
import collections
import functools

import jax
import jax.numpy as jnp
from jax import lax
from jax.experimental import pallas as pl
from jax.experimental.pallas import tpu as pltpu

F32 = jnp.float32
BF16 = jnp.bfloat16

D_MODEL = 1024
BRANCH_W = 512
MEM_LEN = 256
N_LAYER = 2
N_CHIP = 4
N_DEV = 8
EPS = 1e-6
NEG_INF = -1e30
DN_CHUNK = 64
LANES = 128
VMEM_LIMIT = 48 * 1024 * 1024

ADAM_LR, ADAM_B1, ADAM_B2, ADAM_EPS, ADAM_WD, ADAM_STEP = 0.001, 0.9, 0.999, 1e-08, 0.01, 10

N_PAD = 10240
O_GATE = 0
O_AQ, O_AK, O_AV, O_AZ = 4096, 4608, 5120, 5632
O_BUV, O_BZ = 6144, 7168
O_CKV, O_BA = 7680, 7936
O_CQ, O_CZ = 8192, 8704
O_MQ, O_MZ = 9216, 9728
O_MISC, W_MISC = O_CKV, 512
_PAD_SEGS = [(5896, 4096), (0, 512), (512, 512), (1024, 512), (1536, 512), (2056, 1024), (3080, 512),
             (4104, 128), (4232, 128), (2048, 8), (None, 120), (None, 128),
             (3592, 512), (4360, 512), (4872, 512), (5384, 512)]
D_IN = 9992
SHARD_IN = D_IN // N_CHIP


SHARD_PAD = 2560


def _w_pad_from_slabs(slabs):
    parts = []
    for s, n in _PAD_SEGS:
        if s is None:
            parts.append(jnp.zeros((n, slabs[0].shape[1]), slabs[0].dtype))
            continue
        a = s
        while a < s + n:
            chip = a // SHARD_IN
            b = min(s + n, (chip + 1) * SHARD_IN)
            parts.append(slabs[chip][a - chip * SHARD_IN:b - chip * SHARD_IN])
            a = b
    return jnp.concatenate(parts, axis=0)


def _slab_from_pad(dw, chip):
    segs, off = [], 0
    for s, n in _PAD_SEGS:
        if s is not None:
            segs.append((s, off, n))
        off += n
    lo, hi = chip * SHARD_IN, (chip + 1) * SHARD_IN
    parts = []
    for s, o, n in sorted(segs):
        a, b = max(s, lo), min(s + n, hi)
        if a < b:
            parts.append(dw[o + a - s:o + b - s])
    parts.append(jnp.zeros((SHARD_PAD - SHARD_IN, dw.shape[1]), dw.dtype))
    return jnp.concatenate(parts, axis=0)


def _dot(a, b, dims, prec):
    if prec == "bf16":
        return lax.dot_general(a.astype(BF16), b.astype(BF16), (dims, ((), ())), preferred_element_type=F32)
    return lax.dot_general(a, b, (dims, ((), ())), precision=lax.Precision.HIGHEST, preferred_element_type=F32)


_NN, _NT, _TN = ((1,), (0,)), ((1,), (1,)), ((0,), (0,))


def _make_mm(prec):
    @jax.custom_vjp
    def nn(a, b):
        return _dot(a, b, _NN, prec)

    @jax.custom_vjp
    def nt(a, b):
        return _dot(a, b, _NT, prec)

    @jax.custom_vjp
    def tn(a, b):
        return _dot(a, b, _TN, prec)

    nn.defvjp(lambda a, b: (nn(a, b), (a, b)), lambda r, g: (nt(g, r[1]), tn(r[0], g)))
    nt.defvjp(lambda a, b: (nt(a, b), (a, b)), lambda r, g: (nn(g, r[1]), tn(g, r[0])))
    tn.defvjp(lambda a, b: (tn(a, b), (a, b)), lambda r, g: (nt(r[1], g), nn(r[0], g)))
    return nn, nt, tn


_nn16, _nt16, _tn16 = _make_mm("bf16")
_nn32, _nt32, _tn32 = _make_mm("f32")


def _make_slice(axis):
    @functools.partial(jax.custom_vjp, nondiff_argnums=(1, 2, 3))
    def sl(x, a, b, n):
        return x[a:b] if axis == 0 else x[:, a:b]

    def fwd(x, a, b, n):
        return sl(x, a, b, n), None

    def bwd(a, b, n, _, g):
        parts = []
        if a > 0:
            parts.append(jnp.zeros((a, g.shape[1]) if axis == 0 else (g.shape[0], a), g.dtype))
        parts.append(g)
        if n - b > 0:
            parts.append(jnp.zeros((n - b, g.shape[1]) if axis == 0 else (g.shape[0], n - b), g.dtype))
        return (jnp.concatenate(parts, axis=axis),)

    sl.defvjp(fwd, bwd)
    return sl


_sl0, _sl1 = _make_slice(0), _make_slice(1)


def _rowsl(x, a, b):
    return _sl0(x, a, b, x.shape[0])


def _cols(x, a, b):
    return _sl1(x, a, b, x.shape[1])


@functools.partial(jax.custom_vjp, nondiff_argnums=(1,))
def _rollr(x, s):
    return pltpu.roll(x, s, 0)


_rollr.defvjp(lambda x, s: (_rollr(x, s), None),
              lambda s, _, g: (pltpu.roll(g, g.shape[0] - s, 0),))


def _iota(shape, axis):
    return lax.broadcasted_iota(jnp.int32, shape, axis)


def _sigmoid(x):
    return lax.logistic(x)


def _silu(x):
    return x * _sigmoid(x)


def _gelu(x):
    return 0.5 * x * (1.0 + jnp.tanh(0.7978845608028654 * (x + 0.044715 * (x * x * x))))


def _softplus(x):
    return jnp.maximum(x, 0.0) + jnp.log(1.0 + jnp.exp(-jnp.abs(x)))


def _rms(x, g):
    return x * lax.rsqrt(jnp.mean(x * x, axis=-1, keepdims=True) + EPS) * g


def _lane_pick(x, lane):
    return jnp.sum(x * (_iota((1, x.shape[1]), 1) == lane).astype(F32), axis=1, keepdims=True)


Row = collections.namedtuple("Row", "arr w cb hb grad", defaults=(0, True))


def _full_spec(shape):
    return pl.BlockSpec(shape, lambda i, _n=len(shape): (0,) * _n)


def _load_params(refs):
    return [[p[g].astype(F32) for g in range(p.shape[0])] if len(p.shape) == 3 else p[...].astype(F32)
            for p in refs]


def _params(**kw):
    return pltpu.CompilerParams(vmem_limit_bytes=VMEM_LIMIT, **kw)


def _rows_fwd(name, fn, rows, params, outs, tr, carry=None):
    T = rows[0].arr.shape[0]
    n = T // tr
    halos = [r for r in rows if r.hb]
    nr, nh, npar, no = len(rows), len(halos), len(params), len(outs)

    def body(*refs):
        row_refs, halo_refs = refs[:nr], refs[nr:nr + nh]
        par_refs = refs[nr + nh:nr + nh + npar]
        out_refs = refs[nr + nh + npar:nr + nh + npar + no]
        rest = refs[nr + nh + npar + no:]
        first = pl.program_id(0) == 0
        cvals = None
        if carry is not None:
            csave_ref, carry_ref = rest

            @pl.when(first)
            def _():
                carry_ref[...] = jnp.zeros_like(carry_ref)

            cvals = [carry_ref[g] for g in range(carry[0])]
            for g in range(carry[0]):
                csave_ref[0, g] = cvals[g]
        c_out, o = fn(first, cvals, [r[...].astype(F32) for r in row_refs],
                      [h[...].astype(F32) for h in halo_refs], _load_params(par_refs))
        for r, v in zip(out_refs, o):
            r[...] = v.astype(r.dtype)
        if carry is not None:
            for g in range(carry[0]):
                carry_ref[g] = c_out[g]

    in_specs = [pl.BlockSpec((tr, r.w), lambda i, c=r.cb: (i, c)) for r in rows]
    in_specs += [pl.BlockSpec((r.hb, r.w), lambda i, c=r.cb, q=tr // r.hb: (jnp.maximum(i * q - 1, 0), c))
                 for r in halos]
    in_specs += [_full_spec(p.shape) for p in params]
    out_shape = [jax.ShapeDtypeStruct((T, w), dt) for w, dt in outs]
    out_specs = [pl.BlockSpec((tr, w), lambda i: (i, 0)) for w, _ in outs]
    scratch = []
    if carry is not None:
        out_shape.append(jax.ShapeDtypeStruct((n,) + carry, F32))
        out_specs.append(pl.BlockSpec((1,) + carry, lambda i: (i, 0, 0, 0)))
        scratch.append(pltpu.VMEM(carry, F32))
    return pl.pallas_call(
        body, name=name, grid=(n,), in_specs=in_specs, out_specs=out_specs, out_shape=out_shape,
        scratch_shapes=scratch, compiler_params=_params(dimension_semantics=("arbitrary",)),
    )(*[r.arr for r in rows], *[r.arr for r in halos], *params)


def _rows_bwd(name, fn, rows, params, douts, tr, carry=None, csave=None, dcols=None):
    T = rows[0].arr.shape[0]
    n = T // tr
    halos = [r for r in rows if r.hb]
    grows = [r for r in rows if r.grad is True]
    crows = [r for r in rows if r.grad == "cols"]
    wcols = sum(r.w for r in crows)
    nr, nh, npar, nd, ng = len(rows), len(halos), len(params), len(douts), len(grows)
    nc = 0 if carry is None else 1
    ncol = 1 if crows else 0
    nalias = 1 if (crows and dcols is not None) else 0

    def body(*refs):
        row_refs, halo_refs = refs[:nr], refs[nr:nr + nh]
        par_refs = refs[nr + nh:nr + nh + npar]
        k = nr + nh + npar
        csave_ref = refs[k] if nc else None
        dout_refs = refs[k + nc:k + nc + nd]
        k = k + nc + nd + nalias
        drow_refs = refs[k:k + ng]
        dcols_ref = refs[k + ng] if ncol else None
        dpar_refs = refs[k + ng + ncol:k + ng + ncol + npar]
        k = k + ng + ncol + npar
        dcarry_ref = refs[k] if nc else None
        hgrad_refs = refs[k + nc:]
        i = pl.program_id(0)
        first_tile = i == n - 1

        @pl.when(i == 0)
        def _():
            for r in dpar_refs:
                r[...] = jnp.zeros_like(r)
            for r in hgrad_refs:
                r[...] = jnp.zeros_like(r)
            if nc:
                dcarry_ref[...] = jnp.zeros_like(dcarry_ref)

        rv = [r[...].astype(F32) for r in row_refs]
        hv = [h[...].astype(F32) for h in halo_refs]
        pv = _load_params(par_refs)
        dov = [d[...].astype(F32) for d in dout_refs]
        if nc:
            cv = [csave_ref[0, g] for g in range(carry[0])]
            _, vjp = jax.vjp(lambda c, r, h, p: fn(first_tile, c, r, h, p), cv, rv, hv, pv)
            dc, dr, dh, dp = vjp(([dcarry_ref[g] for g in range(carry[0])], dov))
            for g in range(carry[0]):
                dcarry_ref[g] = dc[g]
        else:
            _, vjp = jax.vjp(lambda r, h, p: fn(first_tile, None, r, h, p)[1], rv, hv, pv)
            dr, dh, dp = vjp(dov)
        gi = hi = 0
        pieces = []
        for kk, r in enumerate(rows):
            d = dr[kk]
            if r.hb:
                carried = hgrad_refs[hi][...]
                d = d + (carried if tr == r.hb else
                         jnp.concatenate([jnp.zeros((tr - r.hb, r.w), F32), carried], axis=0))
                hgrad_refs[hi][...] = dh[hi]
                hi += 1
            if r.grad is True:
                drow_refs[gi][...] = d.astype(drow_refs[gi].dtype)
                gi += 1
            elif r.grad == "cols":
                pieces.append(d.astype(BF16))
        if ncol:
            dcols_ref[...] = pieces[0] if len(pieces) == 1 else jnp.concatenate(pieces, axis=1)
        for r, d in zip(dpar_refs, dp):
            if len(r.shape) == 3:
                for g in range(r.shape[0]):
                    r[g] += d[g]
            else:
                r[...] += d

    rev = lambda i: n - 1 - i
    in_specs = [pl.BlockSpec((tr, r.w), lambda i, c=r.cb: (rev(i), c)) for r in rows]
    in_specs += [pl.BlockSpec((r.hb, r.w), lambda i, c=r.cb, q=tr // r.hb: (jnp.maximum(rev(i) * q - 1, 0), c))
                 for r in halos]
    in_specs += [_full_spec(p.shape) for p in params]
    args = [r.arr for r in rows] + [r.arr for r in halos] + list(params)
    scratch = []
    if nc:
        in_specs.append(pl.BlockSpec((1,) + carry, lambda i: (rev(i), 0, 0, 0)))
        args.append(csave)
        scratch.append(pltpu.VMEM(carry, F32))
    in_specs += [pl.BlockSpec((tr, d.shape[1]), lambda i: (rev(i), 0)) for d in douts]
    args += list(douts)
    aliases = {}
    if nalias:
        aliases = {len(args): ng}
        in_specs.append(pl.BlockSpec(memory_space=pl.ANY))
        args.append(dcols)
    scratch += [pltpu.VMEM((r.hb, r.w), F32) for r in halos]
    out_shape = [jax.ShapeDtypeStruct((T, r.w), F32) for r in grows]
    out_specs = [pl.BlockSpec((tr, r.w), lambda i: (rev(i), 0)) for r in grows]
    if ncol:
        off = crows[0].cb * crows[0].w
        assert off % wcols == 0 and all(a.cb * a.w + a.w == b.cb * b.w for a, b in zip(crows, crows[1:]))
        out_shape.append(jax.ShapeDtypeStruct((T, N_PAD), BF16))
        out_specs.append(pl.BlockSpec((tr, wcols), lambda i, c=off // wcols: (rev(i), c)))
    out_shape += [jax.ShapeDtypeStruct(p.shape, F32) for p in params]
    out_specs += [_full_spec(p.shape) for p in params]
    res = pl.pallas_call(
        body, name=name, grid=(n,), in_specs=in_specs, out_specs=out_specs, out_shape=out_shape,
        scratch_shapes=scratch, input_output_aliases=aliases,
        compiler_params=_params(dimension_semantics=("arbitrary",)),
    )(*args)
    return list(res[:ng]), list(res[ng + ncol:]), (res[ng] if ncol else dcols)


def _fill_misc(dcols, dkv, dba, tr):
    T = dkv.shape[0]

    def body(kv_ref, ba_ref, _, o_ref):
        o_ref[...] = jnp.concatenate([kv_ref[...], ba_ref[...]], axis=1).astype(BF16)

    return pl.pallas_call(
        body, name="misc_bwd", grid=(T // tr,),
        in_specs=[pl.BlockSpec((tr, 256), lambda i: (i, 0)), pl.BlockSpec((tr, 256), lambda i: (i, 0)),
                  pl.BlockSpec(memory_space=pl.ANY)],
        out_specs=pl.BlockSpec((tr, W_MISC), lambda i: (i, O_MISC // W_MISC)),
        out_shape=jax.ShapeDtypeStruct((T, N_PAD), BF16), input_output_aliases={2: 0},
        compiler_params=_params(dimension_semantics=("arbitrary",)),
    )(dkv, dba, dcols)


def _matmul(name, a, b, kind, out_dtype, tm, tn, tk):
    if kind == "tn":
        (K, M), N = a.shape, b.shape[1]
    else:
        (M, K), N = a.shape, (b.shape[0] if kind == "nt" else b.shape[1])
    tm, tn, tk = min(tm, M), min(tn, N), min(tk, K)
    nk = K // tk
    dims = {"nn": _NN, "nt": _NT, "tn": _TN}[kind]

    def body(a_ref, b_ref, o_ref, acc_ref):
        k = pl.program_id(2)

        @pl.when(k == 0)
        def _():
            acc_ref[...] = jnp.zeros_like(acc_ref)

        acc_ref[...] += lax.dot_general(a_ref[...], b_ref[...], (dims, ((), ())), preferred_element_type=F32)

        @pl.when(k == nk - 1)
        def _():
            o_ref[...] = acc_ref[...].astype(o_ref.dtype)

    a_spec = pl.BlockSpec((tk, tm), lambda i, j, k: (k, i)) if kind == "tn" else pl.BlockSpec((tm, tk), lambda i, j, k: (i, k))
    b_spec = pl.BlockSpec((tn, tk), lambda i, j, k: (j, k)) if kind == "nt" else pl.BlockSpec((tk, tn), lambda i, j, k: (k, j))
    return pl.pallas_call(
        body, name=name, grid=(M // tm, N // tn, nk), in_specs=[a_spec, b_spec],
        out_specs=pl.BlockSpec((tm, tn), lambda i, j, k: (i, j)),
        out_shape=jax.ShapeDtypeStruct((M, N), out_dtype),
        scratch_shapes=[pltpu.VMEM((tm, tn), F32)],
        compiler_params=_params(dimension_semantics=("arbitrary", "arbitrary", "arbitrary")),
    )(a, b)


def _pre_fn(first, _, rows, halos, params):
    return None, [_rms(rows[0], params[0])]


def _pre_fn_res(first, _, rows, halos, params):
    return None, [_rms(rows[0], params[0]), rows[0]]


def _memkv_fn(first, _, rows, halos, params):
    g, w = params
    return None, [_nn16(_rms(rows[0], g), w)]


def _conv_silu(x, halo, w4, keep_halo):
    tr = x.shape[0]
    halo = halo * keep_halo
    rid = _iota((tr, 1), 0)
    acc = w4[3] * x
    for s in (1, 2, 3):
        hs = jnp.concatenate([_rollr(halo, s), jnp.zeros((tr - halo.shape[0], x.shape[1]), F32)], axis=0)
        acc = acc + w4[3 - s] * jnp.where(rid < s, hs, _rollr(x, s))
    return _silu(acc)


def _dn_fn(first, S, rows, halos, params):
    qp, kp, vp, z, ba = rows
    conv, a_vec, dt_vec, dnorm = params
    ba = _cols(ba, 0, LANES)
    tr = qp.shape[0]
    keep = jnp.where(first, 0.0, 1.0)
    q = _conv_silu(qp, halos[0], [conv[3 * j + 0] for j in range(4)], keep)
    k = _conv_silu(kp, halos[1], [conv[3 * j + 1] for j in range(4)], keep)
    v = _conv_silu(vp, halos[2], [conv[3 * j + 2] for j in range(4)], keep)
    qh, kh, vh = [], [], []
    for h in range(4):
        a, b = h * LANES, (h + 1) * LANES
        xq, xk = _cols(q, a, b), _cols(k, a, b)
        qh.append(xq * lax.rsqrt(jnp.sum(xq * xq, axis=1, keepdims=True) + EPS) * (LANES ** -0.5))
        kh.append(xk * lax.rsqrt(jnp.sum(xk * xk, axis=1, keepdims=True) + EPS))
        vh.append(_cols(v, a, b))
    beta_all = _sigmoid(ba)
    g_all = -jnp.exp(a_vec) * _softplus(ba + dt_vec)
    C = DN_CHUNK
    ii, jj = _iota((C, C), 0), _iota((C, C), 1)
    strict, incl = ii > jj, ii >= jj
    eye = (ii == jj).astype(F32)
    last_row = (_iota((C, 1), 0) == C - 1).astype(F32)
    n_chunk = tr // C
    pairs = [(c, h) for c in range(n_chunk) for h in range(4)]
    rows_of = lambda a, c: _rowsl(a, c * C, (c + 1) * C)
    gcs = [_nn32(incl.astype(F32), rows_of(g_all, c)) for c in range(n_chunk)]
    qc = {(c, h): rows_of(qh[h], c) for c, h in pairs}
    kc = {(c, h): rows_of(kh[h], c) for c, h in pairs}
    beta = {(c, h): _lane_pick(rows_of(beta_all, c), h) for c, h in pairs}
    gc = {(c, h): _lane_pick(gcs[c], 4 + h) for c, h in pairs}
    dec = {p: jnp.exp(jnp.where(incl, gc[p] - jnp.sum(eye * gc[p], axis=0, keepdims=True), 0.0)) for p in pairs}
    egc = {p: jnp.exp(gc[p]) for p in pairs}
    kb = {p: kc[p] * beta[p] for p in pairs}
    kq = {p: _nt16(jnp.concatenate([kb[p], qc[p]], axis=0), kc[p]) for p in pairs}
    P = {p: -jnp.where(strict, _rowsl(kq[p], 0, C) * dec[p], 0.0) for p in pairs}
    aqk = {p: jnp.where(incl, _rowsl(kq[p], C, 2 * C) * dec[p], 0.0) for p in pairs}
    tinv = {p: eye + P[p] for p in pairs}
    P = {p: _nn16(P[p], P[p]) for p in pairs}
    for j in range(5):
        if j < 4:
            pt = {p: _nn16(jnp.concatenate([P[p], tinv[p]], axis=0), P[p]) for p in pairs}
            tinv = {p: tinv[p] + _rowsl(pt[p], C, 2 * C) for p in pairs}
            P = {p: _rowsl(pt[p], 0, C) for p in pairs}
        else:
            tinv = {p: tinv[p] + _nn16(tinv[p], P[p]) for p in pairs}
    uw = {(c, h): _nn16(tinv[c, h], jnp.concatenate([rows_of(vh[h], c) * beta[c, h], kb[c, h] * egc[c, h]], axis=1))
          for c, h in pairs}
    S = list(S)
    ychunks = []
    for c in range(n_chunk):
        zc = rows_of(z, c)
        hs = range(4)
        ws = [_nn16(jnp.concatenate([_cols(uw[c, h], LANES, 2 * LANES), qc[c, h] * egc[c, h]], axis=0), S[h]) for h in hs]
        vnew = [_cols(uw[c, h], 0, LANES) - _rowsl(ws[h], 0, C) for h in hs]
        o = [_rowsl(ws[h], C, 2 * C) + _nn16(aqk[c, h], vnew[h]) for h in hs]
        glast = [jnp.sum(gc[c, h] * last_row, axis=0, keepdims=True) for h in hs]
        S = [S[h] * jnp.exp(glast[h]) + _tn16(kc[c, h] * jnp.exp(glast[h] - gc[c, h]), vnew[h]) for h in hs]
        ychunks.append(jnp.concatenate(
            [_rms(o[h], dnorm) * _silu(_cols(zc, h * LANES, (h + 1) * LANES)) for h in hs], axis=1))
    return S, [jnp.concatenate(ychunks, axis=0)]


def _gm_fn(first, _, rows, halos, params):
    uv, z = rows
    gnorm, ws, bs = params
    tr = uv.shape[0]
    guv = _gelu(uv)
    u = _cols(guv, 0, BRANCH_W)
    v = _rms(_cols(guv, BRANCH_W, 2 * BRANCH_W), gnorm)
    ii, jj = _iota((LANES, LANES), 0), _iota((LANES, LANES), 1)
    eye = (ii == jj).astype(F32)
    wsm = [jnp.where(ii >= jj, ws[g], 0.0) for g in range(4)]
    bcol = [jnp.sum(eye * bs[g], axis=1, keepdims=True) for g in range(4)]
    chunks = []
    for c in range(tr // LANES):
        vc = _rowsl(v, c * LANES, (c + 1) * LANES)
        chunks.append(jnp.concatenate(
            [_nn16(wsm[g], _cols(vc, g * LANES, (g + 1) * LANES)) + bcol[g] for g in range(4)], axis=1))
    return None, [u * jnp.concatenate(chunks, axis=0) * _silu(z)]


def _swa_fn(first, _, rows, halos, params):
    q, kvc, z = rows
    sink_vec = params[0]
    P = LANES
    kv = jnp.concatenate([halos[0], kvc], axis=0)
    k, v = _cols(kv, 0, P), _cols(kv, P, 2 * P)
    r, cc = _iota((P, P), 0), _iota((P, P), 1)
    lane = _iota((1, P), 1)
    dist = _iota((P, 2 * P), 0) + P - _iota((P, 2 * P), 1)
    kmin = jnp.where(first, P, 0)
    valid = (dist >= 0) & (dist < P) & (_iota((P, 2 * P), 1) >= kmin)
    blocks = [None] * 4
    for kh in range(2):
        dup = (r == kh * 64 + (cc & 63)).astype(F32)
        kk, vv = _nn16(k, dup), _nn16(v, dup)
        for g in range(4):
            h = kh * 4 + g
            half = ((lane >= 64) == (h % 2 == 1)).astype(F32)
            qb = _cols(q, (h // 2) * P, (h // 2 + 1) * P) * half
            s = jnp.where(valid, _nt16(qb, kk) * 0.125, NEG_INF)
            sink = _lane_pick(sink_vec, h)
            m = lax.stop_gradient(jnp.maximum(jnp.max(s, axis=1, keepdims=True), sink))
            e = jnp.exp(s - m)
            p = e / (jnp.sum(e, axis=1, keepdims=True) + jnp.exp(sink - m))
            o = _nn16(p, vv) * half
            blocks[h // 2] = o if blocks[h // 2] is None else blocks[h // 2] + o
    return None, [jnp.concatenate(blocks, axis=1) * _silu(z)]


def _mem_fn(first, _, rows, halos, params):
    q, z = rows
    mkv = params[0]
    outs = []
    for h in range(4):
        a, b = h * LANES, (h + 1) * LANES
        s = _nt16(_cols(q, a, b), _cols(mkv, a, b)) * (LANES ** -0.5)
        m = lax.stop_gradient(jnp.max(s, axis=1, keepdims=True))
        e = jnp.exp(s - m)
        p = e / jnp.sum(e, axis=1, keepdims=True)
        outs.append(_nn16(p, _cols(mkv, BRANCH_W + a, BRANCH_W + b)))
    return None, [jnp.concatenate(outs, axis=1) * _silu(z)]


def _up_fn(first, _, rows, halos, params):
    y, gl = rows
    return None, [_sigmoid(gl) * _nn16(y, params[0])]


def _out_fn(first, _, rows, halos, params):
    x, m0, m1, m2, m3 = rows
    w, g = params
    return None, [x + _rms(_nn16(m0 + m1 + m2 + m3, w), g)]


def _loss_fn(first, _, rows, halos, params):
    y, t = rows
    d = y - t
    lrow = 0.5 * jnp.mean(d * d, axis=1, keepdims=True)
    return None, [d * (1.0 / D_MODEL), jnp.broadcast_to(lrow, (y.shape[0], LANES))]


TR = 256
DN_TR = 256
CARRY = (4, LANES, LANES)


def _branch_rows(cols, g):
    a = [Row(cols, 512, O_AQ // 512, 8, g), Row(cols, 512, O_AK // 512, 8, g), Row(cols, 512, O_AV // 512, 8, g),
         Row(cols, 512, O_AZ // 512, 0, g), Row(cols, 256, O_BA // 256)]
    b = [Row(cols, 1024, O_BUV // 1024, 0, g), Row(cols, 512, O_BZ // 512, 0, g)]
    c = [Row(cols, 512, O_CQ // 512, 0, g), Row(cols, 256, O_CKV // 256, LANES), Row(cols, 512, O_CZ // 512, 0, g)]
    m = [Row(cols, 512, O_MQ // 512, 0, g), Row(cols, 512, O_MZ // 512, 0, g)]
    return a, b, c, m


def _layer_fwd(x, mem, W):
    h = _rows_fwd("prenorm_fwd", _pre_fn, [Row(x, D_MODEL, 0)], [W["norm_pre"]], [(D_MODEL, BF16)], TR)[0]
    cols = _matmul("in_proj_fwd", h, W["w_pad"], "nt", F32, 1024, 512, 1024)
    mem_kv = _rows_fwd("memkv_fwd", _memkv_fn, [Row(mem, D_MODEL, 0)], [W["norm_mem"], W["w_mem_kv"]],
                       [(D_MODEL, F32)], MEM_LEN)[0]
    ra, rb, rc, rm = _branch_rows(cols, True)
    y_a, csave = _rows_fwd("dn_fwd", _dn_fn, ra, [W["conv"], W["a_vec"], W["dt_vec"], W["dn_norm"]],
                           [(BRANCH_W, F32)], DN_TR, CARRY)
    y_b = _rows_fwd("gm_fwd", _gm_fn, rb, [W["gm_norm"], W["spatial_w"], W["spatial_b"]], [(BRANCH_W, F32)], TR)[0]
    y_c = _rows_fwd("swa_fwd", _swa_fn, rc, [W["sink_vec"]], [(BRANCH_W, F32)], LANES)[0]
    y_m = _rows_fwd("mem_fwd", _mem_fn, rm, [mem_kv], [(BRANCH_W, F32)], TR)[0]
    ys = [y_a, y_b, y_c, y_m]
    ms = [_rows_fwd("up_fwd", _up_fn, [Row(ys[n], BRANCH_W, 0), Row(cols, D_MODEL, n)], [W["w_up"][n]],
                    [(D_MODEL, F32)], TR)[0] for n in range(4)]
    x_new = _rows_fwd("out_fwd", _out_fn, [Row(x, D_MODEL, 0)] + [Row(m, D_MODEL, 0) for m in ms],
                      [W["w_out"], W["norm_post"]], [(D_MODEL, F32)], TR)[0]
    return x_new, dict(x=x, h=h, cols=cols, mem_kv=mem_kv, csave=csave, ys=ys, ms=ms)


def _layer_bwd(dxn, mem, W, sv):
    x, cols = sv["x"], sv["cols"]
    (dx_res, dm), (dw_out, dnorm_post), _ = _rows_bwd(
        "out_bwd", _out_fn, [Row(x, D_MODEL, 0), Row(sv["ms"][0], D_MODEL, 0)]
        + [Row(m, D_MODEL, 0, 0, False) for m in sv["ms"][1:]], [W["w_out"], W["norm_post"]], [dxn], TR)
    dys, dw_up, dcols = [], [], None
    for n in range(4):
        (dy,), (dwu,), dcols = _rows_bwd(
            "up_bwd", _up_fn, [Row(sv["ys"][n], BRANCH_W, 0), Row(cols, D_MODEL, n, 0, "cols")], [W["w_up"][n]],
            [dm], TR, dcols=dcols)
        dys.append(dy), dw_up.append(dwu)
    ra, rb, rc, rm = _branch_rows(cols, "cols")
    (dba,), (dconv, da_vec, ddt_vec, ddn_norm), dcols = _rows_bwd(
        "dn_bwd", _dn_fn, ra, [W["conv"], W["a_vec"], W["dt_vec"], W["dn_norm"]], [dys[0]], DN_TR, CARRY,
        sv["csave"], dcols=dcols)
    _, (dgm_norm, dws, dbs), dcols = _rows_bwd(
        "gm_bwd", _gm_fn, rb, [W["gm_norm"], W["spatial_w"], W["spatial_b"]], [dys[1]], TR, dcols=dcols)
    (dkv_c,), (dsink,), dcols = _rows_bwd("swa_bwd", _swa_fn, rc, [W["sink_vec"]], [dys[2]], LANES, dcols=dcols)
    _, (dmem_kv,), dcols = _rows_bwd("mem_bwd", _mem_fn, rm, [sv["mem_kv"]], [dys[3]], TR, dcols=dcols)
    dcols = _fill_misc(dcols, dkv_c, dba, TR)
    _, (dnorm_mem, dw_mem_kv), _ = _rows_bwd("memkv_bwd", _memkv_fn, [Row(mem, D_MODEL, 0, 0, False)],
                                             [W["norm_mem"], W["w_mem_kv"]], [dmem_kv], MEM_LEN)
    dw_pad = _matmul("in_proj_dw", dcols, sv["h"], "tn", F32, 1024, 1024, 1024)
    dh = _matmul("in_proj_dx", dcols, W["w_pad"], "nn", F32, 1024, 1024, 1024)
    (dx,), (dnorm_pre,), _ = _rows_bwd("prenorm_bwd", _pre_fn_res, [Row(x, D_MODEL, 0)], [W["norm_pre"]],
                                       [dh, dx_res], TR)
    grads = dict(norm_pre=dnorm_pre, norm_post=dnorm_post, norm_mem=dnorm_mem, w_pad=dw_pad, conv=dconv,
                 a_vec=da_vec, dt_vec=ddt_vec, dn_norm=ddn_norm, gm_norm=dgm_norm, spatial_w=dws, spatial_b=dbs,
                 sink_vec=dsink, w_mem_kv=dw_mem_kv, w_up=dw_up, w_out=dw_out)
    return dx, grads


def _lane_vec(v, off):
    return jnp.zeros((1, LANES), F32).at[0, off:off + v.shape[0]].set(v)


def _layer_weights(l, w_pad, conv_w, w_mem_kv, w_up, w_out, small):
    return dict(
        w_pad=w_pad, conv=conv_w.reshape(4, 3, BRANCH_W).reshape(12, 1, BRANCH_W),
        w_mem_kv=w_mem_kv, w_up=[w_up[n] for n in range(4)], w_out=w_out,
        norm_pre=small["norm_pre"][l][None], norm_post=small["norm_post"][l][None],
        norm_mem=small["norm_mem"][l][None],
        a_vec=_lane_vec(small["a_log"][l], 4), dt_vec=_lane_vec(small["dt_bias"][l], 4),
        dn_norm=small["dn_norm"][l][None], gm_norm=small["gm_norm"][l][None],
        spatial_w=small["spatial_w"][l], spatial_b=small["spatial_b"][l][:, None, :],
        sink_vec=_lane_vec(small["sinks"][l], 0))


def _local_step(x, mem, target, Ws):
    saved = []
    for W in Ws:
        x, sv = _layer_fwd(x, mem, W)
        saved.append(sv)
    dy, lrows = _rows_fwd("loss", _loss_fn, [Row(x, D_MODEL, 0), Row(target, D_MODEL, 0)], [],
                          [(D_MODEL, F32), (LANES, F32)], TR)
    loss = jnp.sum(lrows[:, 0])
    grads = []
    for W, sv in zip(reversed(Ws), reversed(saved)):
        dy, g = _layer_bwd(dy, mem, W, sv)
        grads.append(g)
    return loss, dy, grads[::-1]


_MESH = pl.DeviceIdType.MESH
_ANY = pl.BlockSpec(memory_space=pl.ANY)


def _position():
    return lax.axis_index("x"), lax.axis_index("y"), lax.axis_index("c")


def _remote(src, dst, send_sem, recv_sem, dev):
    return pltpu.make_async_remote_copy(src_ref=src, dst_ref=dst, send_sem=send_sem, recv_sem=recv_sem,
                                        device_id=dev, device_id_type=_MESH)


def _hbm_call(name, body, arrs, out_shapes, sems, aliases=None):
    return pl.pallas_call(
        body, name=name, in_specs=[_ANY] * len(arrs), out_specs=[_ANY] * len(out_shapes), out_shape=out_shapes,
        scratch_shapes=[pltpu.SemaphoreType.DMA((k,)) for k in sems], input_output_aliases=aliases or {},
        compiler_params=pltpu.CompilerParams(has_side_effects=True),
    )(*arrs)


def _other_chips(x, y):
    return [(1 - x, y), (x, 1 - y), (1 - x, 1 - y)]


def _gather_weights(arrs):
    n = len(arrs)

    def body(*refs):
        ins, outs = refs[:n], refs[n:2 * n]
        ici_send, ici_recv, d2d_send, d2d_recv = refs[2 * n:]
        x, y, c = _position()
        me = 2 * x + y
        chips = _other_chips(x, y)
        sends = []
        for a in range(n):
            for j, (px, py) in enumerate(chips):
                sends.append(_remote(ins[a].at[c], outs[a].at[c, me], ici_send.at[3 * a + j], ici_recv.at[3 * a + j],
                                     (px, py, c)))
                sends[-1].start()
        for a in range(n):
            for j, (px, py) in enumerate(chips):
                slab = outs[a].at[c, 2 * px + py]
                _remote(ins[a].at[c], slab, ici_send.at[3 * a + j], ici_recv.at[3 * a + j], (px, py, c)).wait_recv()
                sends.append(_remote(slab, slab, d2d_send.at[3 * a + j], d2d_recv.at[3 * a + j], (x, y, 1 - c)))
                sends[-1].start()
        for a in range(n):
            for j, (px, py) in enumerate(chips):
                slab = outs[a].at[1 - c, 2 * px + py]
                _remote(slab, slab, d2d_send.at[3 * a + j], d2d_recv.at[3 * a + j], (x, y, 1 - c)).wait_recv()
        for cp in sends:
            cp.wait_send()

    return _hbm_call("gather_weights", body, arrs,
                     [jax.ShapeDtypeStruct((N_LAYER, N_CHIP) + a.shape[1:], a.dtype) for a in arrs], [3 * n] * 4)


def _pair_exchange(arrs):
    n = len(arrs)

    def body(*refs):
        ins, outs = refs[:n], refs[n:2 * n]
        send_sems, recv_sems = refs[2 * n:]
        x, y, c = _position()
        cps = [_remote(ins[a].at[1 - c], outs[a], send_sems.at[a], recv_sems.at[a], (x, y, 1 - c)) for a in range(n)]
        for cp in cps:
            cp.start()
        for cp in cps:
            cp.wait_recv()
        for cp in cps:
            cp.wait_send()

    return _hbm_call("pair_exchange", body, arrs, [jax.ShapeDtypeStruct(a.shape[1:], a.dtype) for a in arrs], [n, n])


def _chip_scatter(arrs):
    n = len(arrs)

    def body(*refs):
        ins, outs = refs[:n], refs[n:2 * n]
        send_sems, recv_sems = refs[2 * n:]
        x, y, c = _position()
        me = 2 * x + y
        sends = []
        for a in range(n):
            for j, (px, py) in enumerate(_other_chips(x, y)):
                sends.append(_remote(ins[a].at[2 * px + py], outs[a].at[me], send_sems.at[3 * a + j],
                                     recv_sems.at[3 * a + j], (px, py, c)))
                sends[-1].start()
        for a in range(n):
            for j, (px, py) in enumerate(_other_chips(x, y)):
                _remote(ins[a].at[me], outs[a].at[2 * px + py], send_sems.at[3 * a + j], recv_sems.at[3 * a + j],
                        (px, py, c)).wait_recv()
        for cp in sends:
            cp.wait_send()

    return _hbm_call("chip_scatter", body, arrs, [jax.ShapeDtypeStruct(a.shape, a.dtype) for a in arrs],
                     [3 * n, 3 * n])


def _pair_share(arrs):
    n = len(arrs)

    def body(*refs):
        ins, outs = refs[:n], refs[n:2 * n]
        send_sems, recv_sems = refs[2 * n:]
        x, y, c = _position()
        cps = [_remote(ins[a].at[c], outs[a].at[c], send_sems.at[a], recv_sems.at[a], (x, y, 1 - c)) for a in range(n)]
        for cp in cps:
            cp.start()
        for a in range(n):
            _remote(ins[a].at[c], outs[a].at[1 - c], send_sems.at[a], recv_sems.at[a], (x, y, 1 - c)).wait_recv()
        for cp in cps:
            cp.wait_send()

    return _hbm_call("pair_share", body, arrs, [jax.ShapeDtypeStruct(a.shape, a.dtype) for a in arrs], [n, n],
                     {a: a for a in range(n)})


def _allreduce_small(g):
    def body(g_ref, o_ref, buf, send_sems, recv_sems):
        x, y, c = _position()
        me = 4 * x + 2 * y + c
        buf[me] = g_ref[...]
        peers = []
        for j in range(1, N_DEV):
            px = 1 - x if j & 4 else x
            py = 1 - y if j & 2 else y
            pc = 1 - c if j & 1 else c
            peers.append((px, py, pc))
        sends = [_remote(g_ref, buf.at[me], send_sems.at[j], recv_sems.at[j], p) for j, p in enumerate(peers)]
        for cp in sends:
            cp.start()
        for j, (px, py, pc) in enumerate(peers):
            _remote(g_ref, buf.at[4 * px + 2 * py + pc], send_sems.at[j], recv_sems.at[j], (px, py, pc)).wait_recv()
        for cp in sends:
            cp.wait_send()
        acc = buf[0]
        for s in range(1, N_DEV):
            acc = acc + buf[s]
        o_ref[...] = acc

    vmem = pl.BlockSpec(memory_space=pltpu.VMEM)
    return pl.pallas_call(
        body, name="allreduce_small", in_specs=[vmem], out_specs=vmem, out_shape=jax.ShapeDtypeStruct(g.shape, F32),
        scratch_shapes=[pltpu.VMEM((N_DEV,) + g.shape, F32), pltpu.SemaphoreType.DMA((N_DEV - 1,)),
                        pltpu.SemaphoreType.DMA((N_DEV - 1,))],
        compiler_params=_params(),
    )(g)


EW_ROWS = 512


def _ew(name, fn, ins, n_out, out_dtype=F32, out_core_slot=False):
    def dims(a):
        return a[0].shape[1:] if isinstance(a, tuple) else a.shape

    R, w = dims(ins[0])
    tr = EW_ROWS if R % EW_ROWS == 0 else R

    def body(c_ref, *refs):
        outs = fn(*[r[...] for r in refs[:len(ins)]])
        for r, v in zip(refs[len(ins):], outs):
            r[...] = v.astype(r.dtype)

    def lead_spec(l):
        if l == "c":
            return pl.BlockSpec((None, tr, w), lambda i, c_ref: (c_ref[0], i, 0))
        return pl.BlockSpec((None, tr, w), lambda i, c_ref, s=l: (s, i, 0))

    plain = pl.BlockSpec((tr, w), lambda i, c_ref: (i, 0))
    in_specs = [lead_spec(a[1]) if isinstance(a, tuple) else plain for a in ins]
    out_spec = lead_spec("c") if out_core_slot else plain
    out_shape = jax.ShapeDtypeStruct((2, R, w) if out_core_slot else (R, w), out_dtype)
    return pl.pallas_call(
        body, name=name,
        grid_spec=pltpu.PrefetchScalarGridSpec(num_scalar_prefetch=1, grid=(R // tr,), in_specs=in_specs,
                                               out_specs=[out_spec] * n_out),
        out_shape=[out_shape] * n_out, compiler_params=_params(dimension_semantics=("arbitrary",)),
    )(lax.axis_index("c").astype(jnp.int32).reshape(1), *[a[0] if isinstance(a, tuple) else a for a in ins])


def _adamw_fn(w, g, m, v):
    m = ADAM_B1 * m + (1.0 - ADAM_B1) * g
    v = ADAM_B2 * v + (1.0 - ADAM_B2) * (g * g)
    m_hat = m / (1.0 - ADAM_B1 ** ADAM_STEP)
    v_hat = v / (1.0 - ADAM_B2 ** ADAM_STEP)
    delta = -ADAM_LR * (m_hat / (jnp.sqrt(v_hat) + ADAM_EPS) + ADAM_WD * w)
    return delta, m, v


def _adamw(name, w, g, m, v):
    shape = w.shape
    two = lambda a: a.reshape(-1, shape[-1])
    return [o.reshape(shape) for o in _ew(name, _adamw_fn, [two(w), two(g), two(m), two(v)], 3)]


_SMALL = [("norm_pre", (2, 1024)), ("norm_post", (2, 1024)), ("norm_mem", (2, 1024)), ("a_log", (2, 4)),
          ("dt_bias", (2, 4)), ("dn_norm", (2, 128)), ("gm_norm", (2, 512)), ("spatial_w", (2, 4, 128, 128)),
          ("spatial_b", (2, 4, 128)), ("sinks", (2, 8))]
_SMALL_ROWS = 144
_BIG = ["w_in", "conv_w", "w_mem_kv", "w_up", "w_out"]
_NAMES = ["norm_pre", "norm_post", "norm_mem", "w_in", "conv_w", "a_log", "dt_bias", "dn_norm", "gm_norm",
          "spatial_w", "spatial_b", "sinks", "w_mem_kv", "w_up", "w_out"]


def _size(shape):
    n = 1
    for s in shape:
        n *= s
    return n


def _pack_small(d):
    rows = []
    for n, shp in _SMALL:
        flat = d[n].reshape(-1)
        rows.append(jnp.pad(flat, (0, -flat.shape[0] % 1024)).reshape(-1, 1024))
    used = sum(r.shape[0] for r in rows)
    return jnp.concatenate(rows + [jnp.zeros((_SMALL_ROWS - used, 1024), F32)], axis=0)


def _unpack_small(p):
    out, off = {}, 0
    for n, shp in _SMALL:
        k = -(-_size(shp) // 1024)
        out[n] = p[off:off + k].reshape(-1)[:_size(shp)].reshape(shp)
        off += k
    return out


def _by_chip(name, g):
    if name == "w_in":
        g = jnp.stack([_slab_from_pad(g, s) for s in range(N_CHIP)])
    elif name == "conv_w":
        g = g.reshape(4, N_CHIP, 3 * BRANCH_W // N_CHIP).transpose(1, 0, 2)
    elif name == "w_up":
        g = jnp.stack(g).reshape(4, BRANCH_W, N_CHIP, D_MODEL // N_CHIP).transpose(2, 0, 1, 3)
    else:
        g = g.reshape(N_CHIP, D_MODEL // N_CHIP, D_MODEL)
    return g.astype(BF16)


def _own_slot(buf, mine, me, axis):
    return lax.dynamic_update_index_in_dim(buf, mine.astype(buf.dtype), me, axis)


def kernel(x, mem, norm_pre, norm_post, norm_mem, w_in, conv_w, a_log, dt_bias, dn_norm, gm_norm, spatial_w, spatial_b, sinks, w_mem_kv, w_up, w_out, loss_target, m_norm_pre, m_norm_post, m_norm_mem, m_w_in, m_conv_w, m_a_log, m_dt_bias, m_dn_norm, m_gm_norm, m_spatial_w, m_spatial_b, m_sinks, m_w_mem_kv, m_w_up, m_w_out, v_norm_pre, v_norm_post, v_norm_mem, v_w_in, v_conv_w, v_a_log, v_dt_bias, v_dn_norm, v_gm_norm, v_spatial_w, v_spatial_b, v_sinks, v_w_mem_kv, v_w_up, v_w_out):
    w = dict(norm_pre=norm_pre, norm_post=norm_post, norm_mem=norm_mem, w_in=w_in, conv_w=conv_w, a_log=a_log,
             dt_bias=dt_bias, dn_norm=dn_norm, gm_norm=gm_norm, spatial_w=spatial_w, spatial_b=spatial_b, sinks=sinks,
             w_mem_kv=w_mem_kv, w_up=w_up, w_out=w_out)
    m = dict(norm_pre=m_norm_pre, norm_post=m_norm_post, norm_mem=m_norm_mem, w_in=m_w_in, conv_w=m_conv_w,
             a_log=m_a_log, dt_bias=m_dt_bias, dn_norm=m_dn_norm, gm_norm=m_gm_norm, spatial_w=m_spatial_w,
             spatial_b=m_spatial_b, sinks=m_sinks, w_mem_kv=m_w_mem_kv, w_up=m_w_up, w_out=m_w_out)
    v = dict(norm_pre=v_norm_pre, norm_post=v_norm_post, norm_mem=v_norm_mem, w_in=v_w_in, conv_w=v_conv_w,
             a_log=v_a_log, dt_bias=v_dt_bias, dn_norm=v_dn_norm, gm_norm=v_gm_norm, spatial_w=v_spatial_w,
             spatial_b=v_spatial_b, sinks=v_sinks, w_mem_kv=v_w_mem_kv, w_up=v_w_up, w_out=v_w_out)
    me = 2 * lax.axis_index("x") + lax.axis_index("y")

    t_in = lambda a: jnp.pad(a.transpose(0, 2, 1), ((0, 0), (0, SHARD_PAD - SHARD_IN), (0, 0)))
    t_out = lambda a: a[:, :SHARD_IN].transpose(0, 2, 1)
    local = [t_in(w_in.astype(BF16)), conv_w, w_mem_kv.astype(BF16), w_up.astype(BF16), w_out.astype(BF16)]
    g_in, g_conv, g_kv, g_up, g_out = [_own_slot(g, loc[:, None], me, 1)
                                       for g, loc in zip(_gather_weights(local), local)]
    Ws = []
    for l in range(N_LAYER):
        Ws.append(_layer_weights(
            l, _w_pad_from_slabs([g_in[l, s] for s in range(N_CHIP)]),
            jnp.concatenate([g_conv[l, s] for s in range(N_CHIP)], axis=1),
            g_kv[l].reshape(D_MODEL, D_MODEL),
            g_up[l].transpose(1, 2, 0, 3).reshape(4, BRANCH_W, D_MODEL),
            g_out[l].reshape(D_MODEL, D_MODEL), w))

    loss, dx, grads = _local_step(x[0], mem[0], loss_target[0], Ws)
    loss = lax.psum(loss, ("x", "y", "c"))

    full = [dict(w_in=g["w_pad"], conv_w=g["conv"].reshape(4, 3 * BRANCH_W), w_mem_kv=g["w_mem_kv"], w_up=g["w_up"],
                 w_out=g["w_out"]) for g in grads]
    big = [jnp.stack([_by_chip(n, full[l][n]) for l in range(N_LAYER)]) for n in _BIG]
    theirs = _pair_exchange(big)
    add2 = lambda a, b: [a.astype(F32) + b.astype(F32)]
    pair = []
    for n, g, p in zip(_BIG, big, theirs):
        k = g.shape[-1]
        pair.append(_ew("pair_sum_" + n, add2, [(g.reshape(N_LAYER, -1, k), "c"), p.reshape(-1, k)], 1, BF16)[0]
                    .reshape(p.shape))
    landed = [_own_slot(r, lax.dynamic_index_in_dim(q, me, 0), me, 0) for r, q in zip(_chip_scatter(pair), pair)]
    add4 = lambda a, b, c_, d: [((a.astype(F32) + b.astype(F32)) + c_.astype(F32)) + d.astype(F32)]
    totals = []
    for n, r in zip(_BIG, landed):
        k = r.shape[-1]
        totals.append(_ew("chip_sum_" + n, add4, [(r.reshape(N_CHIP, -1, k), s) for s in range(N_CHIP)], 1,
                          out_core_slot=True)[0].reshape((N_LAYER,) + r.shape[1:]))
    gbig = dict(zip(_BIG, _pair_share(totals)))

    small_local = dict(
        norm_pre=jnp.stack([g["norm_pre"][0] for g in grads]), norm_post=jnp.stack([g["norm_post"][0] for g in grads]),
        norm_mem=jnp.stack([g["norm_mem"][0] for g in grads]), a_log=jnp.stack([g["a_vec"][0, 4:8] for g in grads]),
        dt_bias=jnp.stack([g["dt_vec"][0, 4:8] for g in grads]), dn_norm=jnp.stack([g["dn_norm"][0] for g in grads]),
        gm_norm=jnp.stack([g["gm_norm"][0] for g in grads]), spatial_w=jnp.stack([g["spatial_w"] for g in grads]),
        spatial_b=jnp.stack([g["spatial_b"][:, 0, :] for g in grads]),
        sinks=jnp.stack([g["sink_vec"][0, :8] for g in grads]))
    gsmall_packed = _allreduce_small(_pack_small(small_local))

    d_s, m_s, v_s = _ew("adamw_small", _adamw_fn, [_pack_small(w), gsmall_packed, _pack_small(m), _pack_small(v)], 3)
    gsmall, dsmall, msmall, vsmall = (_unpack_small(p) for p in (gsmall_packed, d_s, m_s, v_s))
    g_o, d_o, m_o, v_o = dict(gsmall), dict(dsmall), dict(msmall), dict(vsmall)
    for n in _BIG:
        if n == "w_in":
            outs = _adamw("adamw_" + n, t_in(w[n]), gbig[n], t_in(m[n]), t_in(v[n]))
            g_o[n], d_o[n], m_o[n], v_o[n] = [t_out(o) for o in [gbig[n]] + outs]
        else:
            g_o[n] = gbig[n]
            d_o[n], m_o[n], v_o[n] = _adamw("adamw_" + n, w[n], gbig[n], m[n], v[n])
    return (loss, dx[None], *[g_o[n] for n in _NAMES], *[d_o[n] for n in _NAMES], *[m_o[n] for n in _NAMES],
            *[v_o[n] for n in _NAMES])
```

```python
import collections
import functools

import jax
import jax.numpy as jnp
from jax import lax
from jax.experimental import pallas as pl
from jax.experimental.pallas import tpu as pltpu

F32 = jnp.float32
BF16 = jnp.bfloat16

D_MODEL = 1024
BRANCH_W = 512
MEM_LEN = 256
N_LAYER = 2
N_CHIP = 4
N_DEV = 8
EPS = 1e-6
NEG_INF = -1e30
DN_CHUNK = 64
LANES = 128
VMEM_LIMIT = 48 * 1024 * 1024

ADAM_LR, ADAM_B1, ADAM_B2, ADAM_EPS, ADAM_WD, ADAM_STEP = 0.001, 0.9, 0.999, 1e-08, 0.01, 10

N_PAD = 10240
O_GATE = 0
O_AQ, O_AK, O_AV, O_AZ = 4096, 4608, 5120, 5632
O_BUV, O_BZ = 6144, 7168
O_CKV, O_BA = 7680, 7936
O_CQ, O_CZ = 8192, 8704
O_MQ, O_MZ = 9216, 9728
O_MISC, W_MISC = O_CKV, 512
_PAD_SEGS = [(5896, 4096), (0, 512), (512, 512), (1024, 512), (1536, 512), (2056, 1024), (3080, 512),
             (4104, 128), (4232, 128), (2048, 8), (None, 120), (None, 128),
             (3592, 512), (4360, 512), (4872, 512), (5384, 512)]
D_IN = 9992
SHARD_IN = D_IN // N_CHIP


SHARD_PAD = 2560


def _pad_parts():
    parts, off = [], 0
    for s, n in _PAD_SEGS:
        a = s
        while s is not None and a < s + n:
            chip = a // SHARD_IN
            b = min(s + n, (chip + 1) * SHARD_IN)
            parts.append((chip, a - chip * SHARD_IN, off + a - s, b - a))
            a = b
        off += n
    return parts


PERM_ROWS = 512
PERM_SLACK = 32


def _permute_rows(name, src, parts, n_out, out_dtype, into=None, into_row=0, total_rows=None):
    B, Z = PERM_ROWS, PERM_ROWS + PERM_SLACK
    w = src.shape[1]
    plans = []
    for blk in range(n_out // B):
        o, runs = blk * B, []
        for s, d, n in parts:
            lo, hi = max(d, o), min(d + n, o + B)
            if lo < hi:
                s0 = s + lo - d
                wa = s0 // 16 * 16
                wb = min(-(-(s0 + hi - lo) // 16) * 16, src.shape[0])
                runs.append((wa, wb - wa, s0 - (lo - o) - wa, lo - o, hi - o))
        plans.append(runs)
    max_runs = max(len(r) for r in plans)
    nblk = len(plans)

    def body(*refs):
        src_ref, out_ref, inbuf, obuf, insem, outsem = (refs[0],) + refs[-5:]

        def in_copies(blk):
            return [pltpu.make_async_copy(src_ref.at[pl.ds(wa, ws)], inbuf.at[blk % 2, r, pl.ds(0, ws)],
                                          insem.at[blk % 2, r]) for r, (wa, ws, _, _, _) in enumerate(plans[blk])]

        def out_copy(blk):
            return pltpu.make_async_copy(obuf.at[blk % 2], out_ref.at[pl.ds(into_row + blk * B, B)], outsem.at[blk % 2])

        for cp in in_copies(0):
            cp.start()
        rid = _iota((B, 1), 0)
        for blk in range(nblk):
            if blk + 1 < nblk:
                for cp in in_copies(blk + 1):
                    cp.start()
            for cp in in_copies(blk):
                cp.wait()
            val = jnp.zeros((B, w), F32)
            for r, (wa, ws, t, l0, l1) in enumerate(plans[blk]):
                win = jnp.concatenate([inbuf[blk % 2, r, pl.ds(0, ws)].astype(F32), jnp.zeros((Z - ws, w), F32)], axis=0)
                moved = pltpu.roll(win, (-t) % Z, 0)[:B]
                val = jnp.where((rid >= l0) & (rid < l1), moved, val)
            if blk >= 2:
                out_copy(blk - 2).wait()
            obuf[blk % 2] = val.astype(out_dtype)
            out_copy(blk).start()
        for blk in range(max(nblk - 2, 0), nblk):
            out_copy(blk).wait()

    args, in_specs, aliases = [src], [_ANY], {}
    out_rows = total_rows or n_out
    if into is not None:
        args.append(into)
        in_specs.append(_ANY)
        aliases = {1: 0}
        out_rows = into.shape[0]
    return pl.pallas_call(
        body, name=name, in_specs=in_specs, out_specs=_ANY, out_shape=jax.ShapeDtypeStruct((out_rows, w), out_dtype),
        scratch_shapes=[pltpu.VMEM((2, max_runs, Z, w), src.dtype), pltpu.VMEM((2, B, w), out_dtype),
                        pltpu.SemaphoreType.DMA((2, max_runs)), pltpu.SemaphoreType.DMA((2,))],
        input_output_aliases=aliases, compiler_params=_params(),
    )(*args)


def _w_pad_from_slabs(slabs, layer):
    base = lambda chip: (layer * N_CHIP + chip) * SHARD_PAD
    return _permute_rows("w_pad_rows", slabs.reshape(-1, slabs.shape[-1]),
                         [(base(chip) + s, d, n) for chip, s, d, n in _pad_parts()], N_PAD, BF16)


def _slabs_from_pad(dw, layer, into):
    return _permute_rows("w_pad_grad_rows", dw, [(d, chip * SHARD_PAD + s, n) for chip, s, d, n in _pad_parts()],
                         N_CHIP * SHARD_PAD, BF16, into, layer * N_CHIP * SHARD_PAD, N_LAYER * N_CHIP * SHARD_PAD)


def _dot(a, b, dims, prec):
    if prec == "bf16":
        return lax.dot_general(a.astype(BF16), b.astype(BF16), (dims, ((), ())), preferred_element_type=F32)
    return lax.dot_general(a, b, (dims, ((), ())), precision=lax.Precision.HIGHEST, preferred_element_type=F32)


_NN, _NT, _TN = ((1,), (0,)), ((1,), (1,)), ((0,), (0,))


def _make_mm(prec):
    @jax.custom_vjp
    def nn(a, b):
        return _dot(a, b, _NN, prec)

    @jax.custom_vjp
    def nt(a, b):
        return _dot(a, b, _NT, prec)

    @jax.custom_vjp
    def tn(a, b):
        return _dot(a, b, _TN, prec)

    nn.defvjp(lambda a, b: (nn(a, b), (a, b)), lambda r, g: (nt(g, r[1]), tn(r[0], g)))
    nt.defvjp(lambda a, b: (nt(a, b), (a, b)), lambda r, g: (nn(g, r[1]), tn(g, r[0])))
    tn.defvjp(lambda a, b: (tn(a, b), (a, b)), lambda r, g: (nt(r[1], g), nn(r[0], g)))
    return nn, nt, tn


_nn16, _nt16, _tn16 = _make_mm("bf16")
_nn32, _nt32, _tn32 = _make_mm("f32")


def _make_slice(axis):
    @functools.partial(jax.custom_vjp, nondiff_argnums=(1, 2, 3))
    def sl(x, a, b, n):
        return x[a:b] if axis == 0 else x[:, a:b]

    def fwd(x, a, b, n):
        return sl(x, a, b, n), None

    def bwd(a, b, n, _, g):
        parts = []
        if a > 0:
            parts.append(jnp.zeros((a, g.shape[1]) if axis == 0 else (g.shape[0], a), g.dtype))
        parts.append(g)
        if n - b > 0:
            parts.append(jnp.zeros((n - b, g.shape[1]) if axis == 0 else (g.shape[0], n - b), g.dtype))
        return (jnp.concatenate(parts, axis=axis),)

    sl.defvjp(fwd, bwd)
    return sl


_sl0, _sl1 = _make_slice(0), _make_slice(1)


def _rowsl(x, a, b):
    return _sl0(x, a, b, x.shape[0])


def _cols(x, a, b):
    return _sl1(x, a, b, x.shape[1])


@functools.partial(jax.custom_vjp, nondiff_argnums=(1,))
def _rollr(x, s):
    return pltpu.roll(x, s, 0)


_rollr.defvjp(lambda x, s: (_rollr(x, s), None),
              lambda s, _, g: (pltpu.roll(g, g.shape[0] - s, 0),))


def _iota(shape, axis):
    return lax.broadcasted_iota(jnp.int32, shape, axis)


def _sigmoid(x):
    return lax.logistic(x)


def _silu(x):
    return x * _sigmoid(x)


def _gelu(x):
    return 0.5 * x * (1.0 + jnp.tanh(0.7978845608028654 * (x + 0.044715 * (x * x * x))))


def _softplus(x):
    return jnp.maximum(x, 0.0) + jnp.log(1.0 + jnp.exp(-jnp.abs(x)))


def _rms(x, g):
    return x * lax.rsqrt(jnp.mean(x * x, axis=-1, keepdims=True) + EPS) * g


def _lane_pick(x, lane):
    return jnp.sum(x * (_iota((1, x.shape[1]), 1) == lane).astype(F32), axis=1, keepdims=True)


Row = collections.namedtuple("Row", "arr w cb hb grad", defaults=(0, True))


def _full_spec(shape):
    return pl.BlockSpec(shape, lambda i, _n=len(shape): (0,) * _n)


def _load_params(refs):
    return [[p[g].astype(F32) for g in range(p.shape[0])] if len(p.shape) == 3 else p[...].astype(F32)
            for p in refs]


def _params(**kw):
    return pltpu.CompilerParams(vmem_limit_bytes=VMEM_LIMIT, **kw)


def _rows_fwd(name, fn, rows, params, outs, tr, carry=None):
    T = rows[0].arr.shape[0]
    n = T // tr
    halos = [r for r in rows if r.hb]
    nr, nh, npar, no = len(rows), len(halos), len(params), len(outs)

    def body(*refs):
        row_refs, halo_refs = refs[:nr], refs[nr:nr + nh]
        par_refs = refs[nr + nh:nr + nh + npar]
        out_refs = refs[nr + nh + npar:nr + nh + npar + no]
        rest = refs[nr + nh + npar + no:]
        first = pl.program_id(0) == 0
        cvals = None
        if carry is not None:
            csave_ref, carry_ref = rest

            @pl.when(first)
            def _():
                carry_ref[...] = jnp.zeros_like(carry_ref)

            cvals = [carry_ref[g] for g in range(carry[0])]
            for g in range(carry[0]):
                csave_ref[0, g] = cvals[g]
        c_out, o = fn(first, cvals, [r[...].astype(F32) for r in row_refs],
                      [h[...].astype(F32) for h in halo_refs], _load_params(par_refs))
        for r, v in zip(out_refs, o):
            r[...] = v.astype(r.dtype)
        if carry is not None:
            for g in range(carry[0]):
                carry_ref[g] = c_out[g]

    in_specs = [pl.BlockSpec((tr, r.w), lambda i, c=r.cb: (i, c)) for r in rows]
    in_specs += [pl.BlockSpec((r.hb, r.w), lambda i, c=r.cb, q=tr // r.hb: (jnp.maximum(i * q - 1, 0), c))
                 for r in halos]
    in_specs += [_full_spec(p.shape) for p in params]
    out_shape = [jax.ShapeDtypeStruct((T, w), dt) for w, dt in outs]
    out_specs = [pl.BlockSpec((tr, w), lambda i: (i, 0)) for w, _ in outs]
    scratch = []
    if carry is not None:
        out_shape.append(jax.ShapeDtypeStruct((n,) + carry, F32))
        out_specs.append(pl.BlockSpec((1,) + carry, lambda i: (i, 0, 0, 0)))
        scratch.append(pltpu.VMEM(carry, F32))
    return pl.pallas_call(
        body, name=name, grid=(n,), in_specs=in_specs, out_specs=out_specs, out_shape=out_shape,
        scratch_shapes=scratch, compiler_params=_params(dimension_semantics=("arbitrary",)),
    )(*[r.arr for r in rows], *[r.arr for r in halos], *params)


def _rows_bwd(name, fn, rows, params, douts, tr, carry=None, csave=None, dcols=None):
    T = rows[0].arr.shape[0]
    n = T // tr
    halos = [r for r in rows if r.hb]
    grows = [r for r in rows if r.grad is True]
    crows = [r for r in rows if r.grad == "cols"]
    wcols = sum(r.w for r in crows)
    nr, nh, npar, nd, ng = len(rows), len(halos), len(params), len(douts), len(grows)
    nc = 0 if carry is None else 1
    ncol = 1 if crows else 0
    nalias = 1 if (crows and dcols is not None) else 0

    def body(*refs):
        row_refs, halo_refs = refs[:nr], refs[nr:nr + nh]
        par_refs = refs[nr + nh:nr + nh + npar]
        k = nr + nh + npar
        csave_ref = refs[k] if nc else None
        dout_refs = refs[k + nc:k + nc + nd]
        k = k + nc + nd + nalias
        drow_refs = refs[k:k + ng]
        dcols_ref = refs[k + ng] if ncol else None
        dpar_refs = refs[k + ng + ncol:k + ng + ncol + npar]
        k = k + ng + ncol + npar
        dcarry_ref = refs[k] if nc else None
        hgrad_refs = refs[k + nc:]
        i = pl.program_id(0)
        first_tile = i == n - 1

        @pl.when(i == 0)
        def _():
            for r in dpar_refs:
                r[...] = jnp.zeros_like(r)
            for r in hgrad_refs:
                r[...] = jnp.zeros_like(r)
            if nc:
                dcarry_ref[...] = jnp.zeros_like(dcarry_ref)

        rv = [r[...].astype(F32) for r in row_refs]
        hv = [h[...].astype(F32) for h in halo_refs]
        pv = _load_params(par_refs)
        dov = [d[...].astype(F32) for d in dout_refs]
        if nc:
            cv = [csave_ref[0, g] for g in range(carry[0])]
            _, vjp = jax.vjp(lambda c, r, h, p: fn(first_tile, c, r, h, p), cv, rv, hv, pv)
            dc, dr, dh, dp = vjp(([dcarry_ref[g] for g in range(carry[0])], dov))
            for g in range(carry[0]):
                dcarry_ref[g] = dc[g]
        else:
            _, vjp = jax.vjp(lambda r, h, p: fn(first_tile, None, r, h, p)[1], rv, hv, pv)
            dr, dh, dp = vjp(dov)
        gi = hi = 0
        pieces = []
        for kk, r in enumerate(rows):
            d = dr[kk]
            if r.hb:
                carried = hgrad_refs[hi][...]
                d = d + (carried if tr == r.hb else
                         jnp.concatenate([jnp.zeros((tr - r.hb, r.w), F32), carried], axis=0))
                hgrad_refs[hi][...] = dh[hi]
                hi += 1
            if r.grad is True:
                drow_refs[gi][...] = d.astype(drow_refs[gi].dtype)
                gi += 1
            elif r.grad == "cols":
                pieces.append(d.astype(BF16))
        if ncol:
            dcols_ref[...] = pieces[0] if len(pieces) == 1 else jnp.concatenate(pieces, axis=1)
        for r, d in zip(dpar_refs, dp):
            if len(r.shape) == 3:
                for g in range(r.shape[0]):
                    r[g] += d[g]
            else:
                r[...] += d

    rev = lambda i: n - 1 - i
    in_specs = [pl.BlockSpec((tr, r.w), lambda i, c=r.cb: (rev(i), c)) for r in rows]
    in_specs += [pl.BlockSpec((r.hb, r.w), lambda i, c=r.cb, q=tr // r.hb: (jnp.maximum(rev(i) * q - 1, 0), c))
                 for r in halos]
    in_specs += [_full_spec(p.shape) for p in params]
    args = [r.arr for r in rows] + [r.arr for r in halos] + list(params)
    scratch = []
    if nc:
        in_specs.append(pl.BlockSpec((1,) + carry, lambda i: (rev(i), 0, 0, 0)))
        args.append(csave)
        scratch.append(pltpu.VMEM(carry, F32))
    in_specs += [pl.BlockSpec((tr, d.shape[1]), lambda i: (rev(i), 0)) for d in douts]
    args += list(douts)
    aliases = {}
    if nalias:
        aliases = {len(args): ng}
        in_specs.append(pl.BlockSpec(memory_space=pl.ANY))
        args.append(dcols)
    scratch += [pltpu.VMEM((r.hb, r.w), F32) for r in halos]
    out_shape = [jax.ShapeDtypeStruct((T, r.w), F32) for r in grows]
    out_specs = [pl.BlockSpec((tr, r.w), lambda i: (rev(i), 0)) for r in grows]
    if ncol:
        off = crows[0].cb * crows[0].w
        assert off % wcols == 0 and all(a.cb * a.w + a.w == b.cb * b.w for a, b in zip(crows, crows[1:]))
        out_shape.append(jax.ShapeDtypeStruct((T, N_PAD), BF16))
        out_specs.append(pl.BlockSpec((tr, wcols), lambda i, c=off // wcols: (rev(i), c)))
    out_shape += [jax.ShapeDtypeStruct(p.shape, F32) for p in params]
    out_specs += [_full_spec(p.shape) for p in params]
    res = pl.pallas_call(
        body, name=name, grid=(n,), in_specs=in_specs, out_specs=out_specs, out_shape=out_shape,
        scratch_shapes=scratch, input_output_aliases=aliases,
        compiler_params=_params(dimension_semantics=("arbitrary",)),
    )(*args)
    return list(res[:ng]), list(res[ng + ncol:]), (res[ng] if ncol else dcols)


def _fill_misc(dcols, dkv, dba, tr):
    T = dkv.shape[0]

    def body(kv_ref, ba_ref, _, o_ref):
        o_ref[...] = jnp.concatenate([kv_ref[...], ba_ref[...]], axis=1).astype(BF16)

    return pl.pallas_call(
        body, name="misc_bwd", grid=(T // tr,),
        in_specs=[pl.BlockSpec((tr, 256), lambda i: (i, 0)), pl.BlockSpec((tr, 256), lambda i: (i, 0)),
                  pl.BlockSpec(memory_space=pl.ANY)],
        out_specs=pl.BlockSpec((tr, W_MISC), lambda i: (i, O_MISC // W_MISC)),
        out_shape=jax.ShapeDtypeStruct((T, N_PAD), BF16), input_output_aliases={2: 0},
        compiler_params=_params(dimension_semantics=("arbitrary",)),
    )(dkv, dba, dcols)


def _matmul(name, a, b, kind, out_dtype, tm, tn, tk):
    if kind == "tn":
        (K, M), N = a.shape, b.shape[1]
    else:
        (M, K), N = a.shape, (b.shape[0] if kind == "nt" else b.shape[1])
    tm, tn, tk = min(tm, M), min(tn, N), min(tk, K)
    nk = K // tk
    dims = {"nn": _NN, "nt": _NT, "tn": _TN}[kind]

    def body(a_ref, b_ref, o_ref, acc_ref):
        k = pl.program_id(2)

        @pl.when(k == 0)
        def _():
            acc_ref[...] = jnp.zeros_like(acc_ref)

        acc_ref[...] += lax.dot_general(a_ref[...], b_ref[...], (dims, ((), ())), preferred_element_type=F32)

        @pl.when(k == nk - 1)
        def _():
            o_ref[...] = acc_ref[...].astype(o_ref.dtype)

    a_spec = pl.BlockSpec((tk, tm), lambda i, j, k: (k, i)) if kind == "tn" else pl.BlockSpec((tm, tk), lambda i, j, k: (i, k))
    b_spec = pl.BlockSpec((tn, tk), lambda i, j, k: (j, k)) if kind == "nt" else pl.BlockSpec((tk, tn), lambda i, j, k: (k, j))
    return pl.pallas_call(
        body, name=name, grid=(M // tm, N // tn, nk), in_specs=[a_spec, b_spec],
        out_specs=pl.BlockSpec((tm, tn), lambda i, j, k: (i, j)),
        out_shape=jax.ShapeDtypeStruct((M, N), out_dtype),
        scratch_shapes=[pltpu.VMEM((tm, tn), F32)],
        compiler_params=_params(dimension_semantics=("arbitrary", "arbitrary", "arbitrary")),
    )(a, b)


def _pre_fn(first, _, rows, halos, params):
    return None, [_rms(rows[0], params[0])]


def _pre_fn_res(first, _, rows, halos, params):
    return None, [_rms(rows[0], params[0]), rows[0]]


def _memkv_fn(first, _, rows, halos, params):
    g, w = params
    return None, [_nn16(_rms(rows[0], g), w)]


def _conv_silu(x, halo, w4, keep_halo):
    tr = x.shape[0]
    halo = halo * keep_halo
    rid = _iota((tr, 1), 0)
    acc = w4[3] * x
    for s in (1, 2, 3):
        hs = jnp.concatenate([_rollr(halo, s), jnp.zeros((tr - halo.shape[0], x.shape[1]), F32)], axis=0)
        acc = acc + w4[3 - s] * jnp.where(rid < s, hs, _rollr(x, s))
    return _silu(acc)


def _dn_fn(first, S, rows, halos, params):
    qp, kp, vp, z, ba = rows
    conv, a_vec, dt_vec, dnorm = params
    ba = _cols(ba, 0, LANES)
    tr = qp.shape[0]
    keep = jnp.where(first, 0.0, 1.0)
    q = _conv_silu(qp, halos[0], [conv[3 * j + 0] for j in range(4)], keep)
    k = _conv_silu(kp, halos[1], [conv[3 * j + 1] for j in range(4)], keep)
    v = _conv_silu(vp, halos[2], [conv[3 * j + 2] for j in range(4)], keep)
    qh, kh, vh = [], [], []
    for h in range(4):
        a, b = h * LANES, (h + 1) * LANES
        xq, xk = _cols(q, a, b), _cols(k, a, b)
        qh.append(xq * lax.rsqrt(jnp.sum(xq * xq, axis=1, keepdims=True) + EPS) * (LANES ** -0.5))
        kh.append(xk * lax.rsqrt(jnp.sum(xk * xk, axis=1, keepdims=True) + EPS))
        vh.append(_cols(v, a, b))
    beta_all = _sigmoid(ba)
    g_all = -jnp.exp(a_vec) * _softplus(ba + dt_vec)
    C = DN_CHUNK
    ii, jj = _iota((C, C), 0), _iota((C, C), 1)
    strict, incl = ii > jj, ii >= jj
    eye = (ii == jj).astype(F32)
    last_row = (_iota((C, 1), 0) == C - 1).astype(F32)
    n_chunk = tr // C
    pairs = [(c, h) for c in range(n_chunk) for h in range(4)]
    rows_of = lambda a, c: _rowsl(a, c * C, (c + 1) * C)
    gcs = [_nn32(incl.astype(F32), rows_of(g_all, c)) for c in range(n_chunk)]
    qc = {(c, h): rows_of(qh[h], c) for c, h in pairs}
    kc = {(c, h): rows_of(kh[h], c) for c, h in pairs}
    beta = {(c, h): _lane_pick(rows_of(beta_all, c), h) for c, h in pairs}
    gc = {(c, h): _lane_pick(gcs[c], 4 + h) for c, h in pairs}
    dec = {p: jnp.exp(jnp.where(incl, gc[p] - jnp.sum(eye * gc[p], axis=0, keepdims=True), 0.0)) for p in pairs}
    egc = {p: jnp.exp(gc[p]) for p in pairs}
    kb = {p: kc[p] * beta[p] for p in pairs}
    kq = {p: _nt16(jnp.concatenate([kb[p], qc[p]], axis=0), kc[p]) for p in pairs}
    P = {p: -jnp.where(strict, _rowsl(kq[p], 0, C) * dec[p], 0.0) for p in pairs}
    aqk = {p: jnp.where(incl, _rowsl(kq[p], C, 2 * C) * dec[p], 0.0) for p in pairs}
    tinv = {p: eye + P[p] for p in pairs}
    P = {p: _nn16(P[p], P[p]) for p in pairs}
    for j in range(5):
        if j < 4:
            pt = {p: _nn16(jnp.concatenate([P[p], tinv[p]], axis=0), P[p]) for p in pairs}
            tinv = {p: tinv[p] + _rowsl(pt[p], C, 2 * C) for p in pairs}
            P = {p: _rowsl(pt[p], 0, C) for p in pairs}
        else:
            tinv = {p: tinv[p] + _nn16(tinv[p], P[p]) for p in pairs}
    uw = {(c, h): _nn16(tinv[c, h], jnp.concatenate([rows_of(vh[h], c) * beta[c, h], kb[c, h] * egc[c, h]], axis=1))
          for c, h in pairs}
    S = list(S)
    ychunks = []
    for c in range(n_chunk):
        zc = rows_of(z, c)
        hs = range(4)
        ws = [_nn16(jnp.concatenate([_cols(uw[c, h], LANES, 2 * LANES), qc[c, h] * egc[c, h]], axis=0), S[h]) for h in hs]
        vnew = [_cols(uw[c, h], 0, LANES) - _rowsl(ws[h], 0, C) for h in hs]
        o = [_rowsl(ws[h], C, 2 * C) + _nn16(aqk[c, h], vnew[h]) for h in hs]
        glast = [jnp.sum(gc[c, h] * last_row, axis=0, keepdims=True) for h in hs]
        S = [S[h] * jnp.exp(glast[h]) + _tn16(kc[c, h] * jnp.exp(glast[h] - gc[c, h]), vnew[h]) for h in hs]
        ychunks.append(jnp.concatenate(
            [_rms(o[h], dnorm) * _silu(_cols(zc, h * LANES, (h + 1) * LANES)) for h in hs], axis=1))
    return S, [jnp.concatenate(ychunks, axis=0)]


def _gm_fn(first, _, rows, halos, params):
    uv, z = rows
    gnorm, ws, bs = params
    tr = uv.shape[0]
    guv = _gelu(uv)
    u = _cols(guv, 0, BRANCH_W)
    v = _rms(_cols(guv, BRANCH_W, 2 * BRANCH_W), gnorm)
    ii, jj = _iota((LANES, LANES), 0), _iota((LANES, LANES), 1)
    eye = (ii == jj).astype(F32)
    wsm = [jnp.where(ii >= jj, ws[g], 0.0) for g in range(4)]
    bcol = [jnp.sum(eye * bs[g], axis=1, keepdims=True) for g in range(4)]
    chunks = []
    for c in range(tr // LANES):
        vc = _rowsl(v, c * LANES, (c + 1) * LANES)
        chunks.append(jnp.concatenate(
            [_nn16(wsm[g], _cols(vc, g * LANES, (g + 1) * LANES)) + bcol[g] for g in range(4)], axis=1))
    return None, [u * jnp.concatenate(chunks, axis=0) * _silu(z)]


def _swa_fn(first, _, rows, halos, params):
    q, kvc, z = rows
    sink_vec = params[0]
    P = LANES
    kv = jnp.concatenate([halos[0], kvc], axis=0)
    k, v = _cols(kv, 0, P), _cols(kv, P, 2 * P)
    r, cc = _iota((P, P), 0), _iota((P, P), 1)
    lane = _iota((1, P), 1)
    dist = _iota((P, 2 * P), 0) + P - _iota((P, 2 * P), 1)
    kmin = jnp.where(first, P, 0)
    valid = (dist >= 0) & (dist < P) & (_iota((P, 2 * P), 1) >= kmin)
    blocks = [None] * 4
    for kh in range(2):
        dup = (r == kh * 64 + (cc & 63)).astype(F32)
        kk, vv = _nn16(k, dup), _nn16(v, dup)
        for g in range(4):
            h = kh * 4 + g
            half = ((lane >= 64) == (h % 2 == 1)).astype(F32)
            qb = _cols(q, (h // 2) * P, (h // 2 + 1) * P) * half
            s = jnp.where(valid, _nt16(qb, kk) * 0.125, NEG_INF)
            sink = _lane_pick(sink_vec, h)
            m = lax.stop_gradient(jnp.maximum(jnp.max(s, axis=1, keepdims=True), sink))
            e = jnp.exp(s - m)
            p = e / (jnp.sum(e, axis=1, keepdims=True) + jnp.exp(sink - m))
            o = _nn16(p, vv) * half
            blocks[h // 2] = o if blocks[h // 2] is None else blocks[h // 2] + o
    return None, [jnp.concatenate(blocks, axis=1) * _silu(z)]


def _mem_fn(first, _, rows, halos, params):
    q, z = rows
    mkv = params[0]
    outs = []
    for h in range(4):
        a, b = h * LANES, (h + 1) * LANES
        s = _nt16(_cols(q, a, b), _cols(mkv, a, b)) * (LANES ** -0.5)
        m = lax.stop_gradient(jnp.max(s, axis=1, keepdims=True))
        e = jnp.exp(s - m)
        p = e / jnp.sum(e, axis=1, keepdims=True)
        outs.append(_nn16(p, _cols(mkv, BRANCH_W + a, BRANCH_W + b)))
    return None, [jnp.concatenate(outs, axis=1) * _silu(z)]


def _up_fn(first, _, rows, halos, params):
    y, gl = rows
    return None, [_sigmoid(gl) * _nn16(y, params[0])]


def _out_fn(first, _, rows, halos, params):
    x, m0, m1, m2, m3 = rows
    w, g = params
    return None, [x + _rms(_nn16(m0 + m1 + m2 + m3, w), g)]


def _loss_fn(first, _, rows, halos, params):
    y, t = rows
    d = y - t
    lrow = 0.5 * jnp.mean(d * d, axis=1, keepdims=True)
    return None, [d * (1.0 / D_MODEL), jnp.broadcast_to(lrow, (y.shape[0], LANES))]


TR = 256
DN_TR = 256
CARRY = (4, LANES, LANES)


def _branch_rows(cols, g):
    a = [Row(cols, 512, O_AQ // 512, 8, g), Row(cols, 512, O_AK // 512, 8, g), Row(cols, 512, O_AV // 512, 8, g),
         Row(cols, 512, O_AZ // 512, 0, g), Row(cols, 256, O_BA // 256)]
    b = [Row(cols, 1024, O_BUV // 1024, 0, g), Row(cols, 512, O_BZ // 512, 0, g)]
    c = [Row(cols, 512, O_CQ // 512, 0, g), Row(cols, 256, O_CKV // 256, LANES), Row(cols, 512, O_CZ // 512, 0, g)]
    m = [Row(cols, 512, O_MQ // 512, 0, g), Row(cols, 512, O_MZ // 512, 0, g)]
    return a, b, c, m


def _layer_fwd(x, mem, W):
    h = _rows_fwd("prenorm_fwd", _pre_fn, [Row(x, D_MODEL, 0)], [W["norm_pre"]], [(D_MODEL, BF16)], TR)[0]
    cols = _matmul("in_proj_fwd", h, W["w_pad"], "nt", F32, 1024, 512, 1024)
    mem_kv = _rows_fwd("memkv_fwd", _memkv_fn, [Row(mem, D_MODEL, 0)], [W["norm_mem"], W["w_mem_kv"]],
                       [(D_MODEL, F32)], MEM_LEN)[0]
    ra, rb, rc, rm = _branch_rows(cols, True)
    y_a, csave = _rows_fwd("dn_fwd", _dn_fn, ra, [W["conv"], W["a_vec"], W["dt_vec"], W["dn_norm"]],
                           [(BRANCH_W, F32)], DN_TR, CARRY)
    y_b = _rows_fwd("gm_fwd", _gm_fn, rb, [W["gm_norm"], W["spatial_w"], W["spatial_b"]], [(BRANCH_W, F32)], TR)[0]
    y_c = _rows_fwd("swa_fwd", _swa_fn, rc, [W["sink_vec"]], [(BRANCH_W, F32)], LANES)[0]
    y_m = _rows_fwd("mem_fwd", _mem_fn, rm, [mem_kv], [(BRANCH_W, F32)], TR)[0]
    ys = [y_a, y_b, y_c, y_m]
    ms = [_rows_fwd("up_fwd", _up_fn, [Row(ys[n], BRANCH_W, 0), Row(cols, D_MODEL, n)], [W["w_up"][n]],
                    [(D_MODEL, F32)], TR)[0] for n in range(4)]
    x_new = _rows_fwd("out_fwd", _out_fn, [Row(x, D_MODEL, 0)] + [Row(m, D_MODEL, 0) for m in ms],
                      [W["w_out"], W["norm_post"]], [(D_MODEL, F32)], TR)[0]
    return x_new, dict(x=x, h=h, cols=cols, mem_kv=mem_kv, csave=csave, ys=ys, ms=ms)


def _layer_bwd(dxn, mem, W, sv):
    x, cols = sv["x"], sv["cols"]
    (dx_res, dm), (dw_out, dnorm_post), _ = _rows_bwd(
        "out_bwd", _out_fn, [Row(x, D_MODEL, 0), Row(sv["ms"][0], D_MODEL, 0)]
        + [Row(m, D_MODEL, 0, 0, False) for m in sv["ms"][1:]], [W["w_out"], W["norm_post"]], [dxn], TR)
    dys, dw_up, dcols = [], [], None
    for n in range(4):
        (dy,), (dwu,), dcols = _rows_bwd(
            "up_bwd", _up_fn, [Row(sv["ys"][n], BRANCH_W, 0), Row(cols, D_MODEL, n, 0, "cols")], [W["w_up"][n]],
            [dm], TR, dcols=dcols)
        dys.append(dy), dw_up.append(dwu)
    ra, rb, rc, rm = _branch_rows(cols, "cols")
    (dba,), (dconv, da_vec, ddt_vec, ddn_norm), dcols = _rows_bwd(
        "dn_bwd", _dn_fn, ra, [W["conv"], W["a_vec"], W["dt_vec"], W["dn_norm"]], [dys[0]], DN_TR, CARRY,
        sv["csave"], dcols=dcols)
    _, (dgm_norm, dws, dbs), dcols = _rows_bwd(
        "gm_bwd", _gm_fn, rb, [W["gm_norm"], W["spatial_w"], W["spatial_b"]], [dys[1]], TR, dcols=dcols)
    (dkv_c,), (dsink,), dcols = _rows_bwd("swa_bwd", _swa_fn, rc, [W["sink_vec"]], [dys[2]], LANES, dcols=dcols)
    _, (dmem_kv,), dcols = _rows_bwd("mem_bwd", _mem_fn, rm, [sv["mem_kv"]], [dys[3]], TR, dcols=dcols)
    dcols = _fill_misc(dcols, dkv_c, dba, TR)
    _, (dnorm_mem, dw_mem_kv), _ = _rows_bwd("memkv_bwd", _memkv_fn, [Row(mem, D_MODEL, 0, 0, False)],
                                             [W["norm_mem"], W["w_mem_kv"]], [dmem_kv], MEM_LEN)
    dw_pad = _matmul("in_proj_dw", dcols, sv["h"], "tn", F32, 1024, 1024, 1024)
    dh = _matmul("in_proj_dx", dcols, W["w_pad"], "nn", F32, 1024, 1024, 1024)
    (dx,), (dnorm_pre,), _ = _rows_bwd("prenorm_bwd", _pre_fn_res, [Row(x, D_MODEL, 0)], [W["norm_pre"]],
                                       [dh, dx_res], TR)
    grads = dict(norm_pre=dnorm_pre, norm_post=dnorm_post, norm_mem=dnorm_mem, w_pad=dw_pad, conv=dconv,
                 a_vec=da_vec, dt_vec=ddt_vec, dn_norm=ddn_norm, gm_norm=dgm_norm, spatial_w=dws, spatial_b=dbs,
                 sink_vec=dsink, w_mem_kv=dw_mem_kv, w_up=dw_up, w_out=dw_out)
    return dx, grads


def _lane_vec(v, off):
    return jnp.zeros((1, LANES), F32).at[0, off:off + v.shape[0]].set(v)


def _layer_weights(l, w_pad, conv_w, w_mem_kv, w_up, w_out, small):
    return dict(
        w_pad=w_pad, conv=conv_w.reshape(4, 3, BRANCH_W).reshape(12, 1, BRANCH_W),
        w_mem_kv=w_mem_kv, w_up=[w_up[n] for n in range(4)], w_out=w_out,
        norm_pre=small["norm_pre"][l][None], norm_post=small["norm_post"][l][None],
        norm_mem=small["norm_mem"][l][None],
        a_vec=_lane_vec(small["a_log"][l], 4), dt_vec=_lane_vec(small["dt_bias"][l], 4),
        dn_norm=small["dn_norm"][l][None], gm_norm=small["gm_norm"][l][None],
        spatial_w=small["spatial_w"][l], spatial_b=small["spatial_b"][l][:, None, :],
        sink_vec=_lane_vec(small["sinks"][l], 0))


def _local_step(x, mem, target, Ws):
    saved = []
    for W in Ws:
        x, sv = _layer_fwd(x, mem, W)
        saved.append(sv)
    dy, lrows = _rows_fwd("loss", _loss_fn, [Row(x, D_MODEL, 0), Row(target, D_MODEL, 0)], [],
                          [(D_MODEL, F32), (LANES, F32)], TR)
    loss = jnp.sum(lrows[:, 0])
    grads = []
    for W, sv in zip(reversed(Ws), reversed(saved)):
        dy, g = _layer_bwd(dy, mem, W, sv)
        grads.append(g)
    return loss, dy, grads[::-1]


_MESH = pl.DeviceIdType.MESH
_ANY = pl.BlockSpec(memory_space=pl.ANY)


def _position():
    return lax.axis_index("x"), lax.axis_index("y"), lax.axis_index("c")


def _remote(src, dst, send_sem, recv_sem, dev):
    return pltpu.make_async_remote_copy(src_ref=src, dst_ref=dst, send_sem=send_sem, recv_sem=recv_sem,
                                        device_id=dev, device_id_type=_MESH)


def _hbm_call(name, body, arrs, out_shapes, sems, aliases=None):
    return pl.pallas_call(
        body, name=name, in_specs=[_ANY] * len(arrs), out_specs=[_ANY] * len(out_shapes), out_shape=out_shapes,
        scratch_shapes=[pltpu.SemaphoreType.DMA((k,)) for k in sems], input_output_aliases=aliases or {},
        compiler_params=pltpu.CompilerParams(has_side_effects=True),
    )(*arrs)


def _other_chips(x, y):
    return [(1 - x, y), (x, 1 - y), (1 - x, 1 - y)]


def _gather_weights(arrs):
    n = len(arrs)

    def body(*refs):
        ins, outs = refs[:n], refs[n:2 * n]
        ici_send, ici_recv, d2d_send, d2d_recv = refs[2 * n:]
        x, y, c = _position()
        me = 2 * x + y
        chips = _other_chips(x, y)
        sends = []
        for a in range(n):
            for j, (px, py) in enumerate(chips):
                sends.append(_remote(ins[a].at[c], outs[a].at[c, me], ici_send.at[3 * a + j], ici_recv.at[3 * a + j],
                                     (px, py, c)))
                sends[-1].start()
        for a in range(n):
            for j, (px, py) in enumerate(chips):
                slab = outs[a].at[c, 2 * px + py]
                _remote(ins[a].at[c], slab, ici_send.at[3 * a + j], ici_recv.at[3 * a + j], (px, py, c)).wait_recv()
                sends.append(_remote(slab, slab, d2d_send.at[3 * a + j], d2d_recv.at[3 * a + j], (x, y, 1 - c)))
                sends[-1].start()
        for a in range(n):
            for j, (px, py) in enumerate(chips):
                slab = outs[a].at[1 - c, 2 * px + py]
                _remote(slab, slab, d2d_send.at[3 * a + j], d2d_recv.at[3 * a + j], (x, y, 1 - c)).wait_recv()
        for cp in sends:
            cp.wait_send()

    return _hbm_call("gather_weights", body, arrs,
                     [jax.ShapeDtypeStruct((N_LAYER, N_CHIP) + a.shape[1:], a.dtype) for a in arrs], [3 * n] * 4)


def _pair_exchange(arrs):
    n = len(arrs)

    def body(*refs):
        ins, outs = refs[:n], refs[n:2 * n]
        send_sems, recv_sems = refs[2 * n:]
        x, y, c = _position()
        cps = [_remote(ins[a].at[1 - c], outs[a], send_sems.at[a], recv_sems.at[a], (x, y, 1 - c)) for a in range(n)]
        for cp in cps:
            cp.start()
        for cp in cps:
            cp.wait_recv()
        for cp in cps:
            cp.wait_send()

    return _hbm_call("pair_exchange", body, arrs, [jax.ShapeDtypeStruct(a.shape[1:], a.dtype) for a in arrs], [n, n])


def _chip_scatter(arrs):
    n = len(arrs)

    def body(*refs):
        ins, outs = refs[:n], refs[n:2 * n]
        send_sems, recv_sems = refs[2 * n:]
        x, y, c = _position()
        me = 2 * x + y
        sends = []
        for a in range(n):
            for j, (px, py) in enumerate(_other_chips(x, y)):
                sends.append(_remote(ins[a].at[2 * px + py], outs[a].at[me], send_sems.at[3 * a + j],
                                     recv_sems.at[3 * a + j], (px, py, c)))
                sends[-1].start()
        for a in range(n):
            for j, (px, py) in enumerate(_other_chips(x, y)):
                _remote(ins[a].at[me], outs[a].at[2 * px + py], send_sems.at[3 * a + j], recv_sems.at[3 * a + j],
                        (px, py, c)).wait_recv()
        for cp in sends:
            cp.wait_send()

    return _hbm_call("chip_scatter", body, arrs, [jax.ShapeDtypeStruct(a.shape, a.dtype) for a in arrs],
                     [3 * n, 3 * n])


def _pair_share(arrs):
    n = len(arrs)

    def body(*refs):
        ins, outs = refs[:n], refs[n:2 * n]
        send_sems, recv_sems = refs[2 * n:]
        x, y, c = _position()
        cps = [_remote(ins[a].at[c], outs[a].at[c], send_sems.at[a], recv_sems.at[a], (x, y, 1 - c)) for a in range(n)]
        for cp in cps:
            cp.start()
        for a in range(n):
            _remote(ins[a].at[c], outs[a].at[1 - c], send_sems.at[a], recv_sems.at[a], (x, y, 1 - c)).wait_recv()
        for cp in cps:
            cp.wait_send()

    return _hbm_call("pair_share", body, arrs, [jax.ShapeDtypeStruct(a.shape, a.dtype) for a in arrs], [n, n],
                     {a: a for a in range(n)})


def _allreduce_small(g):
    def body(g_ref, o_ref, buf, send_sems, recv_sems):
        x, y, c = _position()
        me = 4 * x + 2 * y + c
        buf[me] = g_ref[...]
        peers = []
        for j in range(1, N_DEV):
            px = 1 - x if j & 4 else x
            py = 1 - y if j & 2 else y
            pc = 1 - c if j & 1 else c
            peers.append((px, py, pc))
        sends = [_remote(g_ref, buf.at[me], send_sems.at[j], recv_sems.at[j], p) for j, p in enumerate(peers)]
        for cp in sends:
            cp.start()
        for j, (px, py, pc) in enumerate(peers):
            _remote(g_ref, buf.at[4 * px + 2 * py + pc], send_sems.at[j], recv_sems.at[j], (px, py, pc)).wait_recv()
        for cp in sends:
            cp.wait_send()
        acc = buf[0]
        for s in range(1, N_DEV):
            acc = acc + buf[s]
        o_ref[...] = acc

    vmem = pl.BlockSpec(memory_space=pltpu.VMEM)
    return pl.pallas_call(
        body, name="allreduce_small", in_specs=[vmem], out_specs=vmem, out_shape=jax.ShapeDtypeStruct(g.shape, F32),
        scratch_shapes=[pltpu.VMEM((N_DEV,) + g.shape, F32), pltpu.SemaphoreType.DMA((N_DEV - 1,)),
                        pltpu.SemaphoreType.DMA((N_DEV - 1,))],
        compiler_params=_params(),
    )(g)


EW_ROWS = 512


def _ew(name, fn, ins, n_out, out_dtype=F32, out_core_slot=False):
    def dims(a):
        return a[0].shape[1:] if isinstance(a, tuple) else a.shape

    R, w = dims(ins[0])
    tr = EW_ROWS if R % EW_ROWS == 0 else R

    def body(c_ref, *refs):
        outs = fn(*[r[...] for r in refs[:len(ins)]])
        for r, v in zip(refs[len(ins):], outs):
            r[...] = v.astype(r.dtype)

    def lead_spec(l):
        if l == "c":
            return pl.BlockSpec((None, tr, w), lambda i, c_ref: (c_ref[0], i, 0))
        return pl.BlockSpec((None, tr, w), lambda i, c_ref, s=l: (s, i, 0))

    plain = pl.BlockSpec((tr, w), lambda i, c_ref: (i, 0))
    in_specs = [lead_spec(a[1]) if isinstance(a, tuple) else plain for a in ins]
    out_spec = lead_spec("c") if out_core_slot else plain
    out_shape = jax.ShapeDtypeStruct((2, R, w) if out_core_slot else (R, w), out_dtype)
    return pl.pallas_call(
        body, name=name,
        grid_spec=pltpu.PrefetchScalarGridSpec(num_scalar_prefetch=1, grid=(R // tr,), in_specs=in_specs,
                                               out_specs=[out_spec] * n_out),
        out_shape=[out_shape] * n_out, compiler_params=_params(dimension_semantics=("arbitrary",)),
    )(lax.axis_index("c").astype(jnp.int32).reshape(1), *[a[0] if isinstance(a, tuple) else a for a in ins])


def _adamw_fn(w, g, m, v):
    m = ADAM_B1 * m + (1.0 - ADAM_B1) * g
    v = ADAM_B2 * v + (1.0 - ADAM_B2) * (g * g)
    m_hat = m / (1.0 - ADAM_B1 ** ADAM_STEP)
    v_hat = v / (1.0 - ADAM_B2 ** ADAM_STEP)
    delta = -ADAM_LR * (m_hat / (jnp.sqrt(v_hat) + ADAM_EPS) + ADAM_WD * w)
    return delta, m, v


def _adamw(name, w, g, m, v):
    shape = w.shape
    two = lambda a: a.reshape(-1, shape[-1])
    return [o.reshape(shape) for o in _ew(name, _adamw_fn, [two(w), two(g), two(m), two(v)], 3)]


def _adamw_rows(name, w, g, m, v):
    L, R, k = w.shape

    def body(w_ref, g_ref, m_ref, v_ref, d_out, m_out, v_out, g_out):
        g_blk = g_ref[...]
        d_out[...], m_out[...], v_out[...] = _adamw_fn(w_ref[...], g_blk, m_ref[...], v_ref[...])
        g_out[...] = g_blk

    spec = pl.BlockSpec((None, EW_ROWS, k), lambda l, i: (l, i, 0))
    return pl.pallas_call(
        body, name=name, grid=(L, -(-R // EW_ROWS)), in_specs=[spec] * 4, out_specs=[spec] * 4,
        out_shape=[jax.ShapeDtypeStruct((L, R, k), F32)] * 4,
        compiler_params=_params(dimension_semantics=("arbitrary", "arbitrary")),
    )(w, g, m, v)


_SMALL = [("norm_pre", (2, 1024)), ("norm_post", (2, 1024)), ("norm_mem", (2, 1024)), ("a_log", (2, 4)),
          ("dt_bias", (2, 4)), ("dn_norm", (2, 128)), ("gm_norm", (2, 512)), ("spatial_w", (2, 4, 128, 128)),
          ("spatial_b", (2, 4, 128)), ("sinks", (2, 8))]
_SMALL_ROWS = 144
_BIG = ["w_in", "conv_w", "w_mem_kv", "w_up", "w_out"]
_NAMES = ["norm_pre", "norm_post", "norm_mem", "w_in", "conv_w", "a_log", "dt_bias", "dn_norm", "gm_norm",
          "spatial_w", "spatial_b", "sinks", "w_mem_kv", "w_up", "w_out"]


def _size(shape):
    n = 1
    for s in shape:
        n *= s
    return n


def _pack_small(d):
    rows = []
    for n, shp in _SMALL:
        flat = d[n].reshape(-1)
        rows.append(jnp.pad(flat, (0, -flat.shape[0] % 1024)).reshape(-1, 1024))
    used = sum(r.shape[0] for r in rows)
    return jnp.concatenate(rows + [jnp.zeros((_SMALL_ROWS - used, 1024), F32)], axis=0)


def _unpack_small(p):
    out, off = {}, 0
    for n, shp in _SMALL:
        k = -(-_size(shp) // 1024)
        out[n] = p[off:off + k].reshape(-1)[:_size(shp)].reshape(shp)
        off += k
    return out


def _by_chip(name, g):
    if name == "conv_w":
        g = g.reshape(4, N_CHIP, 3 * BRANCH_W // N_CHIP).transpose(1, 0, 2)
    elif name == "w_up":
        g = jnp.stack(g).reshape(4, BRANCH_W, N_CHIP, D_MODEL // N_CHIP).transpose(2, 0, 1, 3)
    else:
        g = g.reshape(N_CHIP, D_MODEL // N_CHIP, D_MODEL)
    return g.astype(BF16)


def _own_slot(buf, mine, me, axis):
    return lax.dynamic_update_index_in_dim(buf, mine.astype(buf.dtype), me, axis)


def kernel(x, mem, norm_pre, norm_post, norm_mem, w_in, conv_w, a_log, dt_bias, dn_norm, gm_norm, spatial_w, spatial_b, sinks, w_mem_kv, w_up, w_out, loss_target, m_norm_pre, m_norm_post, m_norm_mem, m_w_in, m_conv_w, m_a_log, m_dt_bias, m_dn_norm, m_gm_norm, m_spatial_w, m_spatial_b, m_sinks, m_w_mem_kv, m_w_up, m_w_out, v_norm_pre, v_norm_post, v_norm_mem, v_w_in, v_conv_w, v_a_log, v_dt_bias, v_dn_norm, v_gm_norm, v_spatial_w, v_spatial_b, v_sinks, v_w_mem_kv, v_w_up, v_w_out):
    w = dict(norm_pre=norm_pre, norm_post=norm_post, norm_mem=norm_mem, w_in=w_in, conv_w=conv_w, a_log=a_log,
             dt_bias=dt_bias, dn_norm=dn_norm, gm_norm=gm_norm, spatial_w=spatial_w, spatial_b=spatial_b, sinks=sinks,
             w_mem_kv=w_mem_kv, w_up=w_up, w_out=w_out)
    m = dict(norm_pre=m_norm_pre, norm_post=m_norm_post, norm_mem=m_norm_mem, w_in=m_w_in, conv_w=m_conv_w,
             a_log=m_a_log, dt_bias=m_dt_bias, dn_norm=m_dn_norm, gm_norm=m_gm_norm, spatial_w=m_spatial_w,
             spatial_b=m_spatial_b, sinks=m_sinks, w_mem_kv=m_w_mem_kv, w_up=m_w_up, w_out=m_w_out)
    v = dict(norm_pre=v_norm_pre, norm_post=v_norm_post, norm_mem=v_norm_mem, w_in=v_w_in, conv_w=v_conv_w,
             a_log=v_a_log, dt_bias=v_dt_bias, dn_norm=v_dn_norm, gm_norm=v_gm_norm, spatial_w=v_spatial_w,
             spatial_b=v_spatial_b, sinks=v_sinks, w_mem_kv=v_w_mem_kv, w_up=v_w_up, w_out=v_w_out)
    me = 2 * lax.axis_index("x") + lax.axis_index("y")

    w_in_t = jnp.pad(w_in.astype(BF16).transpose(0, 2, 1), ((0, 0), (0, SHARD_PAD - SHARD_IN), (0, 0)))
    local = [w_in_t, conv_w, w_mem_kv.astype(BF16), w_up.astype(BF16), w_out.astype(BF16)]
    g_in, g_conv, g_kv, g_up, g_out = [_own_slot(g, loc[:, None], me, 1)
                                       for g, loc in zip(_gather_weights(local), local)]
    Ws = []
    for l in range(N_LAYER):
        Ws.append(_layer_weights(
            l, _w_pad_from_slabs(g_in, l),
            jnp.concatenate([g_conv[l, s] for s in range(N_CHIP)], axis=1),
            g_kv[l].reshape(D_MODEL, D_MODEL),
            g_up[l].transpose(1, 2, 0, 3).reshape(4, BRANCH_W, D_MODEL),
            g_out[l].reshape(D_MODEL, D_MODEL), w))

    loss, dx, grads = _local_step(x[0], mem[0], loss_target[0], Ws)
    loss = lax.psum(loss, ("x", "y", "c"))

    full = [dict(conv_w=g["conv"].reshape(4, 3 * BRANCH_W), w_mem_kv=g["w_mem_kv"], w_up=g["w_up"], w_out=g["w_out"])
            for g in grads]
    slabs = None
    for l in range(N_LAYER):
        slabs = _slabs_from_pad(grads[l]["w_pad"], l, slabs)
    big = [slabs.reshape(N_LAYER, N_CHIP, SHARD_PAD, D_MODEL)]
    big += [jnp.stack([_by_chip(n, full[l][n]) for l in range(N_LAYER)]) for n in _BIG[1:]]
    theirs = _pair_exchange(big)
    add2 = lambda a, b: [a.astype(F32) + b.astype(F32)]
    pair = []
    for n, g, p in zip(_BIG, big, theirs):
        k = g.shape[-1]
        pair.append(_ew("pair_sum_" + n, add2, [(g.reshape(N_LAYER, -1, k), "c"), p.reshape(-1, k)], 1, BF16)[0]
                    .reshape(p.shape))
    landed = [_own_slot(r, lax.dynamic_index_in_dim(q, me, 0), me, 0) for r, q in zip(_chip_scatter(pair), pair)]
    add4 = lambda a, b, c_, d: [((a.astype(F32) + b.astype(F32)) + c_.astype(F32)) + d.astype(F32)]
    totals = []
    for n, r in zip(_BIG, landed):
        k = r.shape[-1]
        totals.append(_ew("chip_sum_" + n, add4, [(r.reshape(N_CHIP, -1, k), s) for s in range(N_CHIP)], 1,
                          out_core_slot=True)[0].reshape((N_LAYER,) + r.shape[1:]))
    gbig = dict(zip(_BIG, _pair_share(totals)))

    small_local = dict(
        norm_pre=jnp.stack([g["norm_pre"][0] for g in grads]), norm_post=jnp.stack([g["norm_post"][0] for g in grads]),
        norm_mem=jnp.stack([g["norm_mem"][0] for g in grads]), a_log=jnp.stack([g["a_vec"][0, 4:8] for g in grads]),
        dt_bias=jnp.stack([g["dt_vec"][0, 4:8] for g in grads]), dn_norm=jnp.stack([g["dn_norm"][0] for g in grads]),
        gm_norm=jnp.stack([g["gm_norm"][0] for g in grads]), spatial_w=jnp.stack([g["spatial_w"] for g in grads]),
        spatial_b=jnp.stack([g["spatial_b"][:, 0, :] for g in grads]),
        sinks=jnp.stack([g["sink_vec"][0, :8] for g in grads]))
    gsmall_packed = _allreduce_small(_pack_small(small_local))

    d_s, m_s, v_s = _ew("adamw_small", _adamw_fn, [_pack_small(w), gsmall_packed, _pack_small(m), _pack_small(v)], 3)
    gsmall, dsmall, msmall, vsmall = (_unpack_small(p) for p in (gsmall_packed, d_s, m_s, v_s))
    g_o, d_o, m_o, v_o = dict(gsmall), dict(dsmall), dict(msmall), dict(vsmall)
    for n in _BIG:
        if n == "w_in":
            tr = lambda a: a.transpose(0, 2, 1)
            d_o[n], m_o[n], v_o[n], g_o[n] = [tr(o) for o in _adamw_rows("adamw_" + n, tr(w[n]), gbig[n], tr(m[n]), tr(v[n]))]
        else:
            g_o[n] = gbig[n]
            d_o[n], m_o[n], v_o[n] = _adamw("adamw_" + n, w[n], gbig[n], m[n], v[n])
    return (loss, dx[None], *[g_o[n] for n in _NAMES], *[d_o[n] for n in _NAMES], *[m_o[n] for n in _NAMES],
            *[v_o[n] for n in _NAMES])
```

```python
import collections
import functools

import jax
import jax.numpy as jnp
from jax import lax
from jax.experimental import pallas as pl
from jax.experimental.pallas import tpu as pltpu

F32 = jnp.float32
BF16 = jnp.bfloat16

D_MODEL = 1024
BRANCH_W = 512
MEM_LEN = 256
N_LAYER = 2
N_CHIP = 4
N_DEV = 8
EPS = 1e-6
NEG_INF = -1e30
DN_CHUNK = 64
LANES = 128
VMEM_LIMIT = 48 * 1024 * 1024

ADAM_LR, ADAM_B1, ADAM_B2, ADAM_EPS, ADAM_WD, ADAM_STEP = 0.001, 0.9, 0.999, 1e-08, 0.01, 10

N_PAD = 10240
O_GATE = 0
O_AQ, O_AK, O_AV, O_AZ = 4096, 4608, 5120, 5632
O_BUV, O_BZ = 6144, 7168
O_CKV, O_BA = 7680, 7936
O_CQ, O_CZ = 8192, 8704
O_MQ, O_MZ = 9216, 9728
O_MISC, W_MISC = O_CKV, 512
_PAD_SEGS = [(5896, 4096), (0, 512), (512, 512), (1024, 512), (1536, 512), (2056, 1024), (3080, 512),
             (4104, 128), (4232, 128), (2048, 8), (None, 120), (None, 128),
             (3592, 512), (4360, 512), (4872, 512), (5384, 512)]
D_IN = 9992
SHARD_IN = D_IN // N_CHIP


SHARD_PAD = 2560


def _pad_parts():
    parts, off = [], 0
    for s, n in _PAD_SEGS:
        a = s
        while s is not None and a < s + n:
            chip = a // SHARD_IN
            b = min(s + n, (chip + 1) * SHARD_IN)
            parts.append((chip, a - chip * SHARD_IN, off + a - s, b - a))
            a = b
        off += n
    return parts


PERM_ROWS = 512
PERM_SLACK = 32


def _permute_rows(name, src, parts, n_out, out_dtype):
    B, Z = PERM_ROWS, PERM_ROWS + PERM_SLACK
    w = src.shape[1]
    plans = []
    for blk in range(n_out // B):
        o, runs = blk * B, []
        for s, d, n in parts:
            lo, hi = max(d, o), min(d + n, o + B)
            if lo < hi:
                s0 = s + lo - d
                wa = s0 // 16 * 16
                wb = min(-(-(s0 + hi - lo) // 16) * 16, src.shape[0])
                runs.append((wa, wb - wa, s0 - (lo - o) - wa, lo - o, hi - o))
        plans.append(runs)
    max_runs = max(len(r) for r in plans)
    nblk = len(plans)

    def body(*refs):
        src_ref, out_ref, inbuf, obuf, insem, outsem = (refs[0],) + refs[-5:]

        def in_copies(blk):
            return [pltpu.make_async_copy(src_ref.at[pl.ds(wa, ws)], inbuf.at[blk % 2, r, pl.ds(0, ws)],
                                          insem.at[blk % 2, r]) for r, (wa, ws, _, _, _) in enumerate(plans[blk])]

        def out_copy(blk):
            return pltpu.make_async_copy(obuf.at[blk % 2], out_ref.at[pl.ds(blk * B, B)], outsem.at[blk % 2])

        for cp in in_copies(0):
            cp.start()
        rid = _iota((B, 1), 0)
        for blk in range(nblk):
            if blk + 1 < nblk:
                for cp in in_copies(blk + 1):
                    cp.start()
            for cp in in_copies(blk):
                cp.wait()
            val = jnp.zeros((B, w), F32)
            for r, (wa, ws, t, l0, l1) in enumerate(plans[blk]):
                win = jnp.concatenate([inbuf[blk % 2, r, pl.ds(0, ws)].astype(F32), jnp.zeros((Z - ws, w), F32)], axis=0)
                moved = pltpu.roll(win, (-t) % Z, 0)[:B]
                val = jnp.where((rid >= l0) & (rid < l1), moved, val)
            if blk >= 2:
                out_copy(blk - 2).wait()
            obuf[blk % 2] = val.astype(out_dtype)
            out_copy(blk).start()
        for blk in range(max(nblk - 2, 0), nblk):
            out_copy(blk).wait()

    return pl.pallas_call(
        body, name=name, in_specs=[_ANY], out_specs=_ANY, out_shape=jax.ShapeDtypeStruct((n_out, w), out_dtype),
        scratch_shapes=[pltpu.VMEM((2, max_runs, Z, w), src.dtype), pltpu.VMEM((2, B, w), out_dtype),
                        pltpu.SemaphoreType.DMA((2, max_runs)), pltpu.SemaphoreType.DMA((2,))],
        compiler_params=_params(),
    )(src)


def _slab_parts():
    h, out = SHARD_PAD // 2, []
    for chip, s, d, n in _pad_parts():
        a = s
        while a < s + n:
            half = a // h
            b = min(s + n, (half + 1) * h)
            out.append(((half * N_CHIP + chip) * h + a - half * h, d + a - s, b - a))
            a = b
    return out


def _w_pad_from_slabs(slabs):
    return _permute_rows("w_pad_rows", slabs.reshape(-1, slabs.shape[-1]), _slab_parts(), N_PAD, BF16)


def _slabs_from_pad(dw):
    slabs = _permute_rows("w_pad_grad_rows", dw, [(d, s, n) for s, d, n in _slab_parts()], N_CHIP * SHARD_PAD, BF16)
    return slabs.reshape(2, N_CHIP, SHARD_PAD // 2, dw.shape[1])


def _dot(a, b, dims, prec):
    if prec == "bf16":
        return lax.dot_general(a.astype(BF16), b.astype(BF16), (dims, ((), ())), preferred_element_type=F32)
    return lax.dot_general(a, b, (dims, ((), ())), precision=lax.Precision.HIGHEST, preferred_element_type=F32)


_NN, _NT, _TN = ((1,), (0,)), ((1,), (1,)), ((0,), (0,))


def _make_mm(prec):
    @jax.custom_vjp
    def nn(a, b):
        return _dot(a, b, _NN, prec)

    @jax.custom_vjp
    def nt(a, b):
        return _dot(a, b, _NT, prec)

    @jax.custom_vjp
    def tn(a, b):
        return _dot(a, b, _TN, prec)

    nn.defvjp(lambda a, b: (nn(a, b), (a, b)), lambda r, g: (nt(g, r[1]), tn(r[0], g)))
    nt.defvjp(lambda a, b: (nt(a, b), (a, b)), lambda r, g: (nn(g, r[1]), tn(g, r[0])))
    tn.defvjp(lambda a, b: (tn(a, b), (a, b)), lambda r, g: (nt(r[1], g), nn(r[0], g)))
    return nn, nt, tn


_nn16, _nt16, _tn16 = _make_mm("bf16")
_nn32, _nt32, _tn32 = _make_mm("f32")


def _make_slice(axis):
    @functools.partial(jax.custom_vjp, nondiff_argnums=(1, 2, 3))
    def sl(x, a, b, n):
        return x[a:b] if axis == 0 else x[:, a:b]

    def fwd(x, a, b, n):
        return sl(x, a, b, n), None

    def bwd(a, b, n, _, g):
        parts = []
        if a > 0:
            parts.append(jnp.zeros((a, g.shape[1]) if axis == 0 else (g.shape[0], a), g.dtype))
        parts.append(g)
        if n - b > 0:
            parts.append(jnp.zeros((n - b, g.shape[1]) if axis == 0 else (g.shape[0], n - b), g.dtype))
        return (jnp.concatenate(parts, axis=axis),)

    sl.defvjp(fwd, bwd)
    return sl


_sl0, _sl1 = _make_slice(0), _make_slice(1)


def _rowsl(x, a, b):
    return _sl0(x, a, b, x.shape[0])


def _cols(x, a, b):
    return _sl1(x, a, b, x.shape[1])


@functools.partial(jax.custom_vjp, nondiff_argnums=(1,))
def _rollr(x, s):
    return pltpu.roll(x, s, 0)


_rollr.defvjp(lambda x, s: (_rollr(x, s), None),
              lambda s, _, g: (pltpu.roll(g, g.shape[0] - s, 0),))


def _iota(shape, axis):
    return lax.broadcasted_iota(jnp.int32, shape, axis)


def _sigmoid(x):
    return lax.logistic(x)


def _silu(x):
    return x * _sigmoid(x)


def _gelu(x):
    return 0.5 * x * (1.0 + jnp.tanh(0.7978845608028654 * (x + 0.044715 * (x * x * x))))


def _softplus(x):
    return jnp.maximum(x, 0.0) + jnp.log(1.0 + jnp.exp(-jnp.abs(x)))


def _rms(x, g):
    return x * lax.rsqrt(jnp.mean(x * x, axis=-1, keepdims=True) + EPS) * g


def _lane_pick(x, lane):
    return jnp.sum(x * (_iota((1, x.shape[1]), 1) == lane).astype(F32), axis=1, keepdims=True)


Row = collections.namedtuple("Row", "arr w cb hb grad", defaults=(0, True))


def _full_spec(shape):
    return pl.BlockSpec(shape, lambda i, _n=len(shape): (0,) * _n)


def _load_params(refs):
    return [[p[g].astype(F32) for g in range(p.shape[0])] if len(p.shape) == 3 else p[...].astype(F32)
            for p in refs]


def _params(**kw):
    return pltpu.CompilerParams(vmem_limit_bytes=VMEM_LIMIT, **kw)


def _rows_fwd(name, fn, rows, params, outs, tr, carry=None):
    T = rows[0].arr.shape[0]
    n = T // tr
    halos = [r for r in rows if r.hb]
    nr, nh, npar, no = len(rows), len(halos), len(params), len(outs)

    def body(*refs):
        row_refs, halo_refs = refs[:nr], refs[nr:nr + nh]
        par_refs = refs[nr + nh:nr + nh + npar]
        out_refs = refs[nr + nh + npar:nr + nh + npar + no]
        rest = refs[nr + nh + npar + no:]
        first = pl.program_id(0) == 0
        cvals = None
        if carry is not None:
            csave_ref, carry_ref = rest

            @pl.when(first)
            def _():
                carry_ref[...] = jnp.zeros_like(carry_ref)

            cvals = [carry_ref[g] for g in range(carry[0])]
            for g in range(carry[0]):
                csave_ref[0, g] = cvals[g]
        c_out, o = fn(first, cvals, [r[...].astype(F32) for r in row_refs],
                      [h[...].astype(F32) for h in halo_refs], _load_params(par_refs))
        for r, v in zip(out_refs, o):
            r[...] = v.astype(r.dtype)
        if carry is not None:
            for g in range(carry[0]):
                carry_ref[g] = c_out[g]

    in_specs = [pl.BlockSpec((tr, r.w), lambda i, c=r.cb: (i, c)) for r in rows]
    in_specs += [pl.BlockSpec((r.hb, r.w), lambda i, c=r.cb, q=tr // r.hb: (jnp.maximum(i * q - 1, 0), c))
                 for r in halos]
    in_specs += [_full_spec(p.shape) for p in params]
    out_shape = [jax.ShapeDtypeStruct((T, w), dt) for w, dt in outs]
    out_specs = [pl.BlockSpec((tr, w), lambda i: (i, 0)) for w, _ in outs]
    scratch = []
    if carry is not None:
        out_shape.append(jax.ShapeDtypeStruct((n,) + carry, F32))
        out_specs.append(pl.BlockSpec((1,) + carry, lambda i: (i, 0, 0, 0)))
        scratch.append(pltpu.VMEM(carry, F32))
    return pl.pallas_call(
        body, name=name, grid=(n,), in_specs=in_specs, out_specs=out_specs, out_shape=out_shape,
        scratch_shapes=scratch, compiler_params=_params(dimension_semantics=("arbitrary",)),
    )(*[r.arr for r in rows], *[r.arr for r in halos], *params)


def _rows_bwd(name, fn, rows, params, douts, tr, carry=None, csave=None, dcols=None):
    T = rows[0].arr.shape[0]
    n = T // tr
    halos = [r for r in rows if r.hb]
    grows = [r for r in rows if r.grad is True]
    crows = [r for r in rows if r.grad == "cols"]
    wcols = sum(r.w for r in crows)
    nr, nh, npar, nd, ng = len(rows), len(halos), len(params), len(douts), len(grows)
    nc = 0 if carry is None else 1
    ncol = 1 if crows else 0
    nalias = 1 if (crows and dcols is not None) else 0

    def body(*refs):
        row_refs, halo_refs = refs[:nr], refs[nr:nr + nh]
        par_refs = refs[nr + nh:nr + nh + npar]
        k = nr + nh + npar
        csave_ref = refs[k] if nc else None
        dout_refs = refs[k + nc:k + nc + nd]
        k = k + nc + nd + nalias
        drow_refs = refs[k:k + ng]
        dcols_ref = refs[k + ng] if ncol else None
        dpar_refs = refs[k + ng + ncol:k + ng + ncol + npar]
        k = k + ng + ncol + npar
        dcarry_ref = refs[k] if nc else None
        hgrad_refs = refs[k + nc:]
        i = pl.program_id(0)
        first_tile = i == n - 1

        @pl.when(i == 0)
        def _():
            for r in dpar_refs:
                r[...] = jnp.zeros_like(r)
            for r in hgrad_refs:
                r[...] = jnp.zeros_like(r)
            if nc:
                dcarry_ref[...] = jnp.zeros_like(dcarry_ref)

        rv = [r[...].astype(F32) for r in row_refs]
        hv = [h[...].astype(F32) for h in halo_refs]
        pv = _load_params(par_refs)
        dov = [d[...].astype(F32) for d in dout_refs]
        if nc:
            cv = [csave_ref[0, g] for g in range(carry[0])]
            _, vjp = jax.vjp(lambda c, r, h, p: fn(first_tile, c, r, h, p), cv, rv, hv, pv)
            dc, dr, dh, dp = vjp(([dcarry_ref[g] for g in range(carry[0])], dov))
            for g in range(carry[0]):
                dcarry_ref[g] = dc[g]
        else:
            _, vjp = jax.vjp(lambda r, h, p: fn(first_tile, None, r, h, p)[1], rv, hv, pv)
            dr, dh, dp = vjp(dov)
        gi = hi = 0
        pieces = []
        for kk, r in enumerate(rows):
            d = dr[kk]
            if r.hb:
                carried = hgrad_refs[hi][...]
                d = d + (carried if tr == r.hb else
                         jnp.concatenate([jnp.zeros((tr - r.hb, r.w), F32), carried], axis=0))
                hgrad_refs[hi][...] = dh[hi]
                hi += 1
            if r.grad is True:
                drow_refs[gi][...] = d.astype(drow_refs[gi].dtype)
                gi += 1
            elif r.grad == "cols":
                pieces.append(d.astype(BF16))
        if ncol:
            dcols_ref[...] = pieces[0] if len(pieces) == 1 else jnp.concatenate(pieces, axis=1)
        for r, d in zip(dpar_refs, dp):
            if len(r.shape) == 3:
                for g in range(r.shape[0]):
                    r[g] += d[g]
            else:
                r[...] += d

    rev = lambda i: n - 1 - i
    in_specs = [pl.BlockSpec((tr, r.w), lambda i, c=r.cb: (rev(i), c)) for r in rows]
    in_specs += [pl.BlockSpec((r.hb, r.w), lambda i, c=r.cb, q=tr // r.hb: (jnp.maximum(rev(i) * q - 1, 0), c))
                 for r in halos]
    in_specs += [_full_spec(p.shape) for p in params]
    args = [r.arr for r in rows] + [r.arr for r in halos] + list(params)
    scratch = []
    if nc:
        in_specs.append(pl.BlockSpec((1,) + carry, lambda i: (rev(i), 0, 0, 0)))
        args.append(csave)
        scratch.append(pltpu.VMEM(carry, F32))
    in_specs += [pl.BlockSpec((tr, d.shape[1]), lambda i: (rev(i), 0)) for d in douts]
    args += list(douts)
    aliases = {}
    if nalias:
        aliases = {len(args): ng}
        in_specs.append(pl.BlockSpec(memory_space=pl.ANY))
        args.append(dcols)
    scratch += [pltpu.VMEM((r.hb, r.w), F32) for r in halos]
    out_shape = [jax.ShapeDtypeStruct((T, r.w), F32) for r in grows]
    out_specs = [pl.BlockSpec((tr, r.w), lambda i: (rev(i), 0)) for r in grows]
    if ncol:
        off = crows[0].cb * crows[0].w
        assert off % wcols == 0 and all(a.cb * a.w + a.w == b.cb * b.w for a, b in zip(crows, crows[1:]))
        out_shape.append(jax.ShapeDtypeStruct((T, N_PAD), BF16))
        out_specs.append(pl.BlockSpec((tr, wcols), lambda i, c=off // wcols: (rev(i), c)))
    out_shape += [jax.ShapeDtypeStruct(p.shape, F32) for p in params]
    out_specs += [_full_spec(p.shape) for p in params]
    res = pl.pallas_call(
        body, name=name, grid=(n,), in_specs=in_specs, out_specs=out_specs, out_shape=out_shape,
        scratch_shapes=scratch, input_output_aliases=aliases,
        compiler_params=_params(dimension_semantics=("arbitrary",)),
    )(*args)
    return list(res[:ng]), list(res[ng + ncol:]), (res[ng] if ncol else dcols)


def _fill_misc(dcols, dkv, dba, tr):
    T = dkv.shape[0]

    def body(kv_ref, ba_ref, _, o_ref):
        o_ref[...] = jnp.concatenate([kv_ref[...], ba_ref[...]], axis=1).astype(BF16)

    return pl.pallas_call(
        body, name="misc_bwd", grid=(T // tr,),
        in_specs=[pl.BlockSpec((tr, 256), lambda i: (i, 0)), pl.BlockSpec((tr, 256), lambda i: (i, 0)),
                  pl.BlockSpec(memory_space=pl.ANY)],
        out_specs=pl.BlockSpec((tr, W_MISC), lambda i: (i, O_MISC // W_MISC)),
        out_shape=jax.ShapeDtypeStruct((T, N_PAD), BF16), input_output_aliases={2: 0},
        compiler_params=_params(dimension_semantics=("arbitrary",)),
    )(dkv, dba, dcols)


def _matmul(name, a, b, kind, out_dtype, tm, tn, tk):
    if kind == "tn":
        (K, M), N = a.shape, b.shape[1]
    else:
        (M, K), N = a.shape, (b.shape[0] if kind == "nt" else b.shape[1])
    tm, tn, tk = min(tm, M), min(tn, N), min(tk, K)
    nk = K // tk
    dims = {"nn": _NN, "nt": _NT, "tn": _TN}[kind]

    def body(a_ref, b_ref, o_ref, acc_ref):
        k = pl.program_id(2)

        @pl.when(k == 0)
        def _():
            acc_ref[...] = jnp.zeros_like(acc_ref)

        acc_ref[...] += lax.dot_general(a_ref[...], b_ref[...], (dims, ((), ())), preferred_element_type=F32)

        @pl.when(k == nk - 1)
        def _():
            o_ref[...] = acc_ref[...].astype(o_ref.dtype)

    a_spec = pl.BlockSpec((tk, tm), lambda i, j, k: (k, i)) if kind == "tn" else pl.BlockSpec((tm, tk), lambda i, j, k: (i, k))
    b_spec = pl.BlockSpec((tn, tk), lambda i, j, k: (j, k)) if kind == "nt" else pl.BlockSpec((tk, tn), lambda i, j, k: (k, j))
    return pl.pallas_call(
        body, name=name, grid=(M // tm, N // tn, nk), in_specs=[a_spec, b_spec],
        out_specs=pl.BlockSpec((tm, tn), lambda i, j, k: (i, j)),
        out_shape=jax.ShapeDtypeStruct((M, N), out_dtype),
        scratch_shapes=[pltpu.VMEM((tm, tn), F32)],
        compiler_params=_params(dimension_semantics=("arbitrary", "arbitrary", "arbitrary")),
    )(a, b)


def _pre_fn(first, _, rows, halos, params):
    return None, [_rms(rows[0], params[0])]


def _pre_fn_res(first, _, rows, halos, params):
    return None, [_rms(rows[0], params[0]), rows[0]]


def _memkv_fn(first, _, rows, halos, params):
    g, w = params
    return None, [_nn16(_rms(rows[0], g), w)]


def _conv_silu(x, halo, w4, keep_halo):
    tr = x.shape[0]
    halo = halo * keep_halo
    rid = _iota((tr, 1), 0)
    acc = w4[3] * x
    for s in (1, 2, 3):
        hs = jnp.concatenate([_rollr(halo, s), jnp.zeros((tr - halo.shape[0], x.shape[1]), F32)], axis=0)
        acc = acc + w4[3 - s] * jnp.where(rid < s, hs, _rollr(x, s))
    return _silu(acc)


def _dn_fn(first, S, rows, halos, params):
    qp, kp, vp, z, ba = rows
    conv, a_vec, dt_vec, dnorm = params
    ba = _cols(ba, 0, LANES)
    tr = qp.shape[0]
    keep = jnp.where(first, 0.0, 1.0)
    q = _conv_silu(qp, halos[0], [conv[3 * j + 0] for j in range(4)], keep)
    k = _conv_silu(kp, halos[1], [conv[3 * j + 1] for j in range(4)], keep)
    v = _conv_silu(vp, halos[2], [conv[3 * j + 2] for j in range(4)], keep)
    qh, kh, vh = [], [], []
    for h in range(4):
        a, b = h * LANES, (h + 1) * LANES
        xq, xk = _cols(q, a, b), _cols(k, a, b)
        qh.append(xq * lax.rsqrt(jnp.sum(xq * xq, axis=1, keepdims=True) + EPS) * (LANES ** -0.5))
        kh.append(xk * lax.rsqrt(jnp.sum(xk * xk, axis=1, keepdims=True) + EPS))
        vh.append(_cols(v, a, b))
    beta_all = _sigmoid(ba)
    g_all = -jnp.exp(a_vec) * _softplus(ba + dt_vec)
    C = DN_CHUNK
    ii, jj = _iota((C, C), 0), _iota((C, C), 1)
    strict, incl = ii > jj, ii >= jj
    eye = (ii == jj).astype(F32)
    last_row = (_iota((C, 1), 0) == C - 1).astype(F32)
    n_chunk = tr // C
    pairs = [(c, h) for c in range(n_chunk) for h in range(4)]
    rows_of = lambda a, c: _rowsl(a, c * C, (c + 1) * C)
    gcs = [_nn32(incl.astype(F32), rows_of(g_all, c)) for c in range(n_chunk)]
    qc = {(c, h): rows_of(qh[h], c) for c, h in pairs}
    kc = {(c, h): rows_of(kh[h], c) for c, h in pairs}
    beta = {(c, h): _lane_pick(rows_of(beta_all, c), h) for c, h in pairs}
    gc = {(c, h): _lane_pick(gcs[c], 4 + h) for c, h in pairs}
    dec = {p: jnp.exp(jnp.where(incl, gc[p] - jnp.sum(eye * gc[p], axis=0, keepdims=True), 0.0)) for p in pairs}
    egc = {p: jnp.exp(gc[p]) for p in pairs}
    kb = {p: kc[p] * beta[p] for p in pairs}
    kq = {p: _nt16(jnp.concatenate([kb[p], qc[p]], axis=0), kc[p]) for p in pairs}
    P = {p: -jnp.where(strict, _rowsl(kq[p], 0, C) * dec[p], 0.0) for p in pairs}
    aqk = {p: jnp.where(incl, _rowsl(kq[p], C, 2 * C) * dec[p], 0.0) for p in pairs}
    tinv = {p: eye + P[p] for p in pairs}
    P = {p: _nn16(P[p], P[p]) for p in pairs}
    for j in range(5):
        if j < 4:
            pt = {p: _nn16(jnp.concatenate([P[p], tinv[p]], axis=0), P[p]) for p in pairs}
            tinv = {p: tinv[p] + _rowsl(pt[p], C, 2 * C) for p in pairs}
            P = {p: _rowsl(pt[p], 0, C) for p in pairs}
        else:
            tinv = {p: tinv[p] + _nn16(tinv[p], P[p]) for p in pairs}
    uw = {(c, h): _nn16(tinv[c, h], jnp.concatenate([rows_of(vh[h], c) * beta[c, h], kb[c, h] * egc[c, h]], axis=1))
          for c, h in pairs}
    S = list(S)
    ychunks = []
    for c in range(n_chunk):
        zc = rows_of(z, c)
        hs = range(4)
        ws = [_nn16(jnp.concatenate([_cols(uw[c, h], LANES, 2 * LANES), qc[c, h] * egc[c, h]], axis=0), S[h]) for h in hs]
        vnew = [_cols(uw[c, h], 0, LANES) - _rowsl(ws[h], 0, C) for h in hs]
        o = [_rowsl(ws[h], C, 2 * C) + _nn16(aqk[c, h], vnew[h]) for h in hs]
        glast = [jnp.sum(gc[c, h] * last_row, axis=0, keepdims=True) for h in hs]
        S = [S[h] * jnp.exp(glast[h]) + _tn16(kc[c, h] * jnp.exp(glast[h] - gc[c, h]), vnew[h]) for h in hs]
        ychunks.append(jnp.concatenate(
            [_rms(o[h], dnorm) * _silu(_cols(zc, h * LANES, (h + 1) * LANES)) for h in hs], axis=1))
    return S, [jnp.concatenate(ychunks, axis=0)]


def _gm_fn(first, _, rows, halos, params):
    uv, z = rows
    gnorm, ws, bs = params
    tr = uv.shape[0]
    guv = _gelu(uv)
    u = _cols(guv, 0, BRANCH_W)
    v = _rms(_cols(guv, BRANCH_W, 2 * BRANCH_W), gnorm)
    ii, jj = _iota((LANES, LANES), 0), _iota((LANES, LANES), 1)
    eye = (ii == jj).astype(F32)
    wsm = [jnp.where(ii >= jj, ws[g], 0.0) for g in range(4)]
    bcol = [jnp.sum(eye * bs[g], axis=1, keepdims=True) for g in range(4)]
    chunks = []
    for c in range(tr // LANES):
        vc = _rowsl(v, c * LANES, (c + 1) * LANES)
        chunks.append(jnp.concatenate(
            [_nn16(wsm[g], _cols(vc, g * LANES, (g + 1) * LANES)) + bcol[g] for g in range(4)], axis=1))
    return None, [u * jnp.concatenate(chunks, axis=0) * _silu(z)]


def _swa_fn(first, _, rows, halos, params):
    q, kvc, z = rows
    sink_vec = params[0]
    P = LANES
    kv = jnp.concatenate([halos[0], kvc], axis=0)
    k, v = _cols(kv, 0, P), _cols(kv, P, 2 * P)
    r, cc = _iota((P, P), 0), _iota((P, P), 1)
    lane = _iota((1, P), 1)
    dist = _iota((P, 2 * P), 0) + P - _iota((P, 2 * P), 1)
    kmin = jnp.where(first, P, 0)
    valid = (dist >= 0) & (dist < P) & (_iota((P, 2 * P), 1) >= kmin)
    blocks = [None] * 4
    for kh in range(2):
        dup = (r == kh * 64 + (cc & 63)).astype(F32)
        kk, vv = _nn16(k, dup), _nn16(v, dup)
        for g in range(4):
            h = kh * 4 + g
            half = ((lane >= 64) == (h % 2 == 1)).astype(F32)
            qb = _cols(q, (h // 2) * P, (h // 2 + 1) * P) * half
            s = jnp.where(valid, _nt16(qb, kk) * 0.125, NEG_INF)
            sink = _lane_pick(sink_vec, h)
            m = lax.stop_gradient(jnp.maximum(jnp.max(s, axis=1, keepdims=True), sink))
            e = jnp.exp(s - m)
            p = e / (jnp.sum(e, axis=1, keepdims=True) + jnp.exp(sink - m))
            o = _nn16(p, vv) * half
            blocks[h // 2] = o if blocks[h // 2] is None else blocks[h // 2] + o
    return None, [jnp.concatenate(blocks, axis=1) * _silu(z)]


def _mem_fn(first, _, rows, halos, params):
    q, z = rows
    mkv = params[0]
    outs = []
    for h in range(4):
        a, b = h * LANES, (h + 1) * LANES
        s = _nt16(_cols(q, a, b), _cols(mkv, a, b)) * (LANES ** -0.5)
        m = lax.stop_gradient(jnp.max(s, axis=1, keepdims=True))
        e = jnp.exp(s - m)
        p = e / jnp.sum(e, axis=1, keepdims=True)
        outs.append(_nn16(p, _cols(mkv, BRANCH_W + a, BRANCH_W + b)))
    return None, [jnp.concatenate(outs, axis=1) * _silu(z)]


def _up_fn(first, _, rows, halos, params):
    y, gl = rows
    return None, [_sigmoid(gl) * _nn16(y, params[0])]


def _out_fn(first, _, rows, halos, params):
    x, m0, m1, m2, m3 = rows
    w, g = params
    return None, [x + _rms(_nn16(m0 + m1 + m2 + m3, w), g)]


def _loss_fn(first, _, rows, halos, params):
    y, t = rows
    d = y - t
    lrow = 0.5 * jnp.mean(d * d, axis=1, keepdims=True)
    return None, [d * (1.0 / D_MODEL), jnp.broadcast_to(lrow, (y.shape[0], LANES))]


TR = 256
DN_TR = 256
CARRY = (4, LANES, LANES)


def _branch_rows(cols, g):
    a = [Row(cols, 512, O_AQ // 512, 8, g), Row(cols, 512, O_AK // 512, 8, g), Row(cols, 512, O_AV // 512, 8, g),
         Row(cols, 512, O_AZ // 512, 0, g), Row(cols, 256, O_BA // 256)]
    b = [Row(cols, 1024, O_BUV // 1024, 0, g), Row(cols, 512, O_BZ // 512, 0, g)]
    c = [Row(cols, 512, O_CQ // 512, 0, g), Row(cols, 256, O_CKV // 256, LANES), Row(cols, 512, O_CZ // 512, 0, g)]
    m = [Row(cols, 512, O_MQ // 512, 0, g), Row(cols, 512, O_MZ // 512, 0, g)]
    return a, b, c, m


def _layer_fwd(x, mem, W):
    h = _rows_fwd("prenorm_fwd", _pre_fn, [Row(x, D_MODEL, 0)], [W["norm_pre"]], [(D_MODEL, BF16)], TR)[0]
    cols = _matmul("in_proj_fwd", h, W["w_pad"], "nt", F32, 1024, 512, 1024)
    mem_kv = _rows_fwd("memkv_fwd", _memkv_fn, [Row(mem, D_MODEL, 0)], [W["norm_mem"], W["w_mem_kv"]],
                       [(D_MODEL, F32)], MEM_LEN)[0]
    ra, rb, rc, rm = _branch_rows(cols, True)
    y_a, csave = _rows_fwd("dn_fwd", _dn_fn, ra, [W["conv"], W["a_vec"], W["dt_vec"], W["dn_norm"]],
                           [(BRANCH_W, F32)], DN_TR, CARRY)
    y_b = _rows_fwd("gm_fwd", _gm_fn, rb, [W["gm_norm"], W["spatial_w"], W["spatial_b"]], [(BRANCH_W, F32)], TR)[0]
    y_c = _rows_fwd("swa_fwd", _swa_fn, rc, [W["sink_vec"]], [(BRANCH_W, F32)], LANES)[0]
    y_m = _rows_fwd("mem_fwd", _mem_fn, rm, [mem_kv], [(BRANCH_W, F32)], TR)[0]
    ys = [y_a, y_b, y_c, y_m]
    ms = [_rows_fwd("up_fwd", _up_fn, [Row(ys[n], BRANCH_W, 0), Row(cols, D_MODEL, n)], [W["w_up"][n]],
                    [(D_MODEL, F32)], TR)[0] for n in range(4)]
    x_new = _rows_fwd("out_fwd", _out_fn, [Row(x, D_MODEL, 0)] + [Row(m, D_MODEL, 0) for m in ms],
                      [W["w_out"], W["norm_post"]], [(D_MODEL, F32)], TR)[0]
    return x_new, dict(x=x, h=h, cols=cols, mem_kv=mem_kv, csave=csave, ys=ys, ms=ms)


def _layer_bwd(dxn, mem, W, sv):
    x, cols = sv["x"], sv["cols"]
    (dx_res, dm), (dw_out, dnorm_post), _ = _rows_bwd(
        "out_bwd", _out_fn, [Row(x, D_MODEL, 0), Row(sv["ms"][0], D_MODEL, 0)]
        + [Row(m, D_MODEL, 0, 0, False) for m in sv["ms"][1:]], [W["w_out"], W["norm_post"]], [dxn], TR)
    dys, dw_up, dcols = [], [], None
    for n in range(4):
        (dy,), (dwu,), dcols = _rows_bwd(
            "up_bwd", _up_fn, [Row(sv["ys"][n], BRANCH_W, 0), Row(cols, D_MODEL, n, 0, "cols")], [W["w_up"][n]],
            [dm], TR, dcols=dcols)
        dys.append(dy), dw_up.append(dwu)
    ra, rb, rc, rm = _branch_rows(cols, "cols")
    (dba,), (dconv, da_vec, ddt_vec, ddn_norm), dcols = _rows_bwd(
        "dn_bwd", _dn_fn, ra, [W["conv"], W["a_vec"], W["dt_vec"], W["dn_norm"]], [dys[0]], DN_TR, CARRY,
        sv["csave"], dcols=dcols)
    _, (dgm_norm, dws, dbs), dcols = _rows_bwd(
        "gm_bwd", _gm_fn, rb, [W["gm_norm"], W["spatial_w"], W["spatial_b"]], [dys[1]], TR, dcols=dcols)
    (dkv_c,), (dsink,), dcols = _rows_bwd("swa_bwd", _swa_fn, rc, [W["sink_vec"]], [dys[2]], LANES, dcols=dcols)
    _, (dmem_kv,), dcols = _rows_bwd("mem_bwd", _mem_fn, rm, [sv["mem_kv"]], [dys[3]], TR, dcols=dcols)
    dcols = _fill_misc(dcols, dkv_c, dba, TR)
    _, (dnorm_mem, dw_mem_kv), _ = _rows_bwd("memkv_bwd", _memkv_fn, [Row(mem, D_MODEL, 0, 0, False)],
                                             [W["norm_mem"], W["w_mem_kv"]], [dmem_kv], MEM_LEN)
    dw_pad = _matmul("in_proj_dw", dcols, sv["h"], "tn", F32, 1024, 1024, 1024)
    dh = _matmul("in_proj_dx", dcols, W["w_pad"], "nn", F32, 1024, 1024, 1024)
    (dx,), (dnorm_pre,), _ = _rows_bwd("prenorm_bwd", _pre_fn_res, [Row(x, D_MODEL, 0)], [W["norm_pre"]],
                                       [dh, dx_res], TR)
    grads = dict(norm_pre=dnorm_pre, norm_post=dnorm_post, norm_mem=dnorm_mem, w_pad=dw_pad, conv=dconv,
                 a_vec=da_vec, dt_vec=ddt_vec, dn_norm=ddn_norm, gm_norm=dgm_norm, spatial_w=dws, spatial_b=dbs,
                 sink_vec=dsink, w_mem_kv=dw_mem_kv, w_up=dw_up, w_out=dw_out)
    return dx, grads


def _lane_vec(v, off):
    return jnp.zeros((1, LANES), F32).at[0, off:off + v.shape[0]].set(v)


def _layer_weights(l, w_pad, conv_w, w_mem_kv, w_up, w_out, small):
    return dict(
        w_pad=w_pad, conv=conv_w.reshape(4, 3, BRANCH_W).reshape(12, 1, BRANCH_W),
        w_mem_kv=w_mem_kv, w_up=[w_up[n] for n in range(4)], w_out=w_out,
        norm_pre=small["norm_pre"][l][None], norm_post=small["norm_post"][l][None],
        norm_mem=small["norm_mem"][l][None],
        a_vec=_lane_vec(small["a_log"][l], 4), dt_vec=_lane_vec(small["dt_bias"][l], 4),
        dn_norm=small["dn_norm"][l][None], gm_norm=small["gm_norm"][l][None],
        spatial_w=small["spatial_w"][l], spatial_b=small["spatial_b"][l][:, None, :],
        sink_vec=_lane_vec(small["sinks"][l], 0))


_MESH = pl.DeviceIdType.MESH
_ANY = pl.BlockSpec(memory_space=pl.ANY)


def _position():
    return lax.axis_index("x"), lax.axis_index("y"), lax.axis_index("c")


def _remote(src, dst, send_sem, recv_sem, dev):
    return pltpu.make_async_remote_copy(src_ref=src, dst_ref=dst, send_sem=send_sem, recv_sem=recv_sem,
                                        device_id=dev, device_id_type=_MESH)


def _hbm_call(name, body, arrs, out_shapes, sems, aliases=None):
    return pl.pallas_call(
        body, name=name, in_specs=[_ANY] * len(arrs), out_specs=[_ANY] * len(out_shapes), out_shape=out_shapes,
        scratch_shapes=[pltpu.SemaphoreType.DMA((k,)) for k in sems], input_output_aliases=aliases or {},
        compiler_params=pltpu.CompilerParams(has_side_effects=True),
    )(*arrs)


def _other_chips(x, y):
    return [(1 - x, y), (x, 1 - y), (1 - x, 1 - y)]


def _gather_weights(arrs):
    n = len(arrs)

    def body(*refs):
        ins, outs = refs[:n], refs[n:2 * n]
        ici_send, ici_recv, d2d_send, d2d_recv = refs[2 * n:]
        x, y, c = _position()
        me = 2 * x + y
        chips = _other_chips(x, y)
        sends = []
        for a in range(n):
            for j, (px, py) in enumerate(chips):
                sends.append(_remote(ins[a].at[c], outs[a].at[c, me], ici_send.at[3 * a + j], ici_recv.at[3 * a + j],
                                     (px, py, c)))
                sends[-1].start()
        for a in range(n):
            for j, (px, py) in enumerate(chips):
                slab = outs[a].at[c, 2 * px + py]
                _remote(ins[a].at[c], slab, ici_send.at[3 * a + j], ici_recv.at[3 * a + j], (px, py, c)).wait_recv()
                sends.append(_remote(slab, slab, d2d_send.at[3 * a + j], d2d_recv.at[3 * a + j], (x, y, 1 - c)))
                sends[-1].start()
        for a in range(n):
            for j, (px, py) in enumerate(chips):
                slab = outs[a].at[1 - c, 2 * px + py]
                _remote(slab, slab, d2d_send.at[3 * a + j], d2d_recv.at[3 * a + j], (x, y, 1 - c)).wait_recv()
        for cp in sends:
            cp.wait_send()

    return _hbm_call("gather_weights", body, arrs,
                     [jax.ShapeDtypeStruct((N_LAYER, N_CHIP) + a.shape[1:], a.dtype) for a in arrs], [3 * n] * 4)


def _pair_exchange(arrs):
    n = len(arrs)

    def body(*refs):
        ins, outs = refs[:n], refs[n:2 * n]
        send_sems, recv_sems = refs[2 * n:]
        x, y, c = _position()
        cps = [_remote(ins[a].at[1 - c], outs[a], send_sems.at[a], recv_sems.at[a], (x, y, 1 - c)) for a in range(n)]
        for cp in cps:
            cp.start()
        for cp in cps:
            cp.wait_recv()
        for cp in cps:
            cp.wait_send()

    return _hbm_call("pair_exchange", body, arrs, [jax.ShapeDtypeStruct(a.shape[1:], a.dtype) for a in arrs], [n, n])


def _chip_scatter(arrs):
    n = len(arrs)

    def body(*refs):
        ins, outs = refs[:n], refs[n:2 * n]
        send_sems, recv_sems = refs[2 * n:]
        x, y, c = _position()
        me = 2 * x + y
        sends = []
        for a in range(n):
            for j, (px, py) in enumerate(_other_chips(x, y)):
                sends.append(_remote(ins[a].at[2 * px + py], outs[a].at[me], send_sems.at[3 * a + j],
                                     recv_sems.at[3 * a + j], (px, py, c)))
                sends[-1].start()
        for a in range(n):
            for j, (px, py) in enumerate(_other_chips(x, y)):
                _remote(ins[a].at[me], outs[a].at[2 * px + py], send_sems.at[3 * a + j], recv_sems.at[3 * a + j],
                        (px, py, c)).wait_recv()
        for cp in sends:
            cp.wait_send()

    return _hbm_call("chip_scatter", body, arrs, [jax.ShapeDtypeStruct(a.shape, a.dtype) for a in arrs],
                     [3 * n, 3 * n])


def _pair_share(arrs):
    n = len(arrs)

    def body(*refs):
        ins, outs = refs[:n], refs[n:2 * n]
        send_sems, recv_sems = refs[2 * n:]
        x, y, c = _position()
        cps = [_remote(ins[a].at[c], outs[a].at[c], send_sems.at[a], recv_sems.at[a], (x, y, 1 - c)) for a in range(n)]
        for cp in cps:
            cp.start()
        for a in range(n):
            _remote(ins[a].at[c], outs[a].at[1 - c], send_sems.at[a], recv_sems.at[a], (x, y, 1 - c)).wait_recv()
        for cp in cps:
            cp.wait_send()

    return _hbm_call("pair_share", body, arrs, [jax.ShapeDtypeStruct(a.shape, a.dtype) for a in arrs], [n, n],
                     {a: a for a in range(n)})


def _pair_forward(arrs):
    n = len(arrs)

    def body(*refs):
        ins, outs = refs[:n], refs[n:2 * n]
        send_sems, recv_sems = refs[2 * n:]
        x, y, c = _position()
        sends = []
        for a in range(n):
            for j, (px, py) in enumerate(_other_chips(x, y)):
                sends.append(_remote(ins[a].at[c, 2 * px + py], outs[a].at[c, 2 * px + py], send_sems.at[3 * a + j],
                                     recv_sems.at[3 * a + j], (x, y, 1 - c)))
                sends[-1].start()
        for a in range(n):
            for j, (px, py) in enumerate(_other_chips(x, y)):
                slab = outs[a].at[1 - c, 2 * px + py]
                _remote(slab, slab, send_sems.at[3 * a + j], recv_sems.at[3 * a + j], (x, y, 1 - c)).wait_recv()
        for cp in sends:
            cp.wait_send()

    return _hbm_call("pair_forward", body, arrs, [jax.ShapeDtypeStruct(a.shape, a.dtype) for a in arrs],
                     [3 * n, 3 * n], {a: a for a in range(n)})


_HBM = pl.BlockSpec(memory_space=pltpu.HBM)
_SEM = pl.BlockSpec(memory_space=pltpu.SEMAPHORE)
_EFFECT = pltpu.SideEffectType.DATAFLOW_SIDE_EFFECTING


def _chip_copies(kind, srcs, lands, send_sems, recv_sems):
    x, y, c = _position()
    me = 2 * x + y
    sends, recvs = [], []
    for a in range(len(srcs)):
        for j, (px, py) in enumerate(_other_chips(x, y)):
            s, sems, dev = 2 * px + py, (send_sems.at[3 * a + j], recv_sems.at[3 * a + j]), (px, py, c)
            if kind == "gather":
                sends.append(_remote(srcs[a].at[c], lands[a].at[c, me], *sems, dev))
                recvs.append(_remote(srcs[a].at[c], lands[a].at[c, s], *sems, dev))
            else:
                sends.append(_remote(srcs[a].at[s], lands[a].at[me], *sems, dev))
                recvs.append(_remote(srcs[a].at[me], lands[a].at[s], *sems, dev))
    return sends, recvs


def _split_start(name, kind, srcs, land_shapes, after):
    n = len(srcs)

    def body(*refs):
        sends, _ = _chip_copies(kind, refs[:n], refs[n:2 * n], refs[2 * n + 1], refs[2 * n + 2])
        for cp in sends:
            cp.start()
        refs[-1][...] = jnp.zeros_like(refs[-1])

    hbm = lambda a: pltpu.with_memory_space_constraint(a, pltpu.HBM)
    lands = [lax.empty(s.shape, s.dtype) for s in land_shapes]
    outs = pl.pallas_call(
        body, name=name, in_specs=[_HBM] * (2 * n) + [_ANY],
        out_specs=[_SEM, _SEM] + [_HBM] * (2 * n) + [pl.BlockSpec(memory_space=pltpu.VMEM)],
        out_shape=[pltpu.SemaphoreType.DMA((3 * n,)), pltpu.SemaphoreType.DMA((3 * n,))]
        + [pltpu.HBM(a.shape, a.dtype) for a in list(srcs) + lands] + [jax.ShapeDtypeStruct((8, LANES), F32)],
        input_output_aliases={i: 2 + i for i in range(2 * n)},
        compiler_params=pltpu.CompilerParams(has_side_effects=_EFFECT),
    )(*[hbm(a) for a in srcs], *[hbm(a) for a in lands], after)
    return outs[0], outs[1], list(outs[2:2 + 2 * n]), outs[-1]


def _split_wait(name, kind, started, after):
    send_sems, recv_sems, thru, _ = started
    n = len(thru) // 2

    def body(*refs):
        sends, recvs = _chip_copies(kind, refs[:n], refs[n:2 * n], refs[2 * n], refs[2 * n + 1])
        for cp in sends:
            cp.wait_send()
        for cp in recvs:
            cp.wait_recv()

    outs = pl.pallas_call(
        body, name=name, in_specs=[_HBM] * (2 * n) + [_SEM, _SEM, _ANY], out_specs=[_HBM] * (2 * n),
        out_shape=[pltpu.HBM(a.shape, a.dtype) for a in thru], input_output_aliases={i: i for i in range(2 * n)},
        compiler_params=pltpu.CompilerParams(has_side_effects=_EFFECT),
    )(*thru, send_sems, recv_sems, after)
    return list(outs[:n]), list(outs[n:])


def _allreduce_small(g):
    def body(g_ref, o_ref, buf, send_sems, recv_sems):
        x, y, c = _position()
        me = 4 * x + 2 * y + c
        buf[me] = g_ref[...]
        peers = []
        for j in range(1, N_DEV):
            px = 1 - x if j & 4 else x
            py = 1 - y if j & 2 else y
            pc = 1 - c if j & 1 else c
            peers.append((px, py, pc))
        sends = [_remote(g_ref, buf.at[me], send_sems.at[j], recv_sems.at[j], p) for j, p in enumerate(peers)]
        for cp in sends:
            cp.start()
        for j, (px, py, pc) in enumerate(peers):
            _remote(g_ref, buf.at[4 * px + 2 * py + pc], send_sems.at[j], recv_sems.at[j], (px, py, pc)).wait_recv()
        for cp in sends:
            cp.wait_send()
        acc = buf[0]
        for s in range(1, N_DEV):
            acc = acc + buf[s]
        o_ref[...] = acc

    vmem = pl.BlockSpec(memory_space=pltpu.VMEM)
    return pl.pallas_call(
        body, name="allreduce_small", in_specs=[vmem], out_specs=vmem, out_shape=jax.ShapeDtypeStruct(g.shape, F32),
        scratch_shapes=[pltpu.VMEM((N_DEV,) + g.shape, F32), pltpu.SemaphoreType.DMA((N_DEV - 1,)),
                        pltpu.SemaphoreType.DMA((N_DEV - 1,))],
        compiler_params=_params(),
    )(g)


EW_ROWS = 512


def _ew(name, fn, ins, n_out, out_dtype=F32, out_core_slot=False):
    def dims(a):
        return a[0].shape[1:] if isinstance(a, tuple) else a.shape

    R, w = dims(ins[0])
    tr = EW_ROWS if R % EW_ROWS == 0 else R

    def body(c_ref, *refs):
        outs = fn(*[r[...] for r in refs[:len(ins)]])
        for r, v in zip(refs[len(ins):], outs):
            r[...] = v.astype(r.dtype)

    def lead_spec(l):
        if l == "c":
            return pl.BlockSpec((None, tr, w), lambda i, c_ref: (c_ref[0], i, 0))
        return pl.BlockSpec((None, tr, w), lambda i, c_ref, s=l: (s, i, 0))

    plain = pl.BlockSpec((tr, w), lambda i, c_ref: (i, 0))
    in_specs = [lead_spec(a[1]) if isinstance(a, tuple) else plain for a in ins]
    out_spec = lead_spec("c") if out_core_slot else plain
    out_shape = jax.ShapeDtypeStruct((2, R, w) if out_core_slot else (R, w), out_dtype)
    return pl.pallas_call(
        body, name=name,
        grid_spec=pltpu.PrefetchScalarGridSpec(num_scalar_prefetch=1, grid=(R // tr,), in_specs=in_specs,
                                               out_specs=[out_spec] * n_out),
        out_shape=[out_shape] * n_out, compiler_params=_params(dimension_semantics=("arbitrary",)),
    )(lax.axis_index("c").astype(jnp.int32).reshape(1), *[a[0] if isinstance(a, tuple) else a for a in ins])


def _adamw_fn(w, g, m, v):
    m = ADAM_B1 * m + (1.0 - ADAM_B1) * g
    v = ADAM_B2 * v + (1.0 - ADAM_B2) * (g * g)
    m_hat = m / (1.0 - ADAM_B1 ** ADAM_STEP)
    v_hat = v / (1.0 - ADAM_B2 ** ADAM_STEP)
    delta = -ADAM_LR * (m_hat / (jnp.sqrt(v_hat) + ADAM_EPS) + ADAM_WD * w)
    return delta, m, v


def _adamw(name, w, g, m, v):
    shape = w.shape
    two = lambda a: a.reshape(-1, shape[-1])
    return [o.reshape(shape) for o in _ew(name, _adamw_fn, [two(w), two(g), two(m), two(v)], 3)]


def _adamw_rows(name, w, g, m, v):
    L, R, k = w.shape

    def body(w_ref, g_ref, m_ref, v_ref, d_out, m_out, v_out, g_out):
        g_blk = g_ref[...]
        d_out[...], m_out[...], v_out[...] = _adamw_fn(w_ref[...], g_blk, m_ref[...], v_ref[...])
        g_out[...] = g_blk

    spec = pl.BlockSpec((None, EW_ROWS, k), lambda l, i: (l, i, 0))
    return pl.pallas_call(
        body, name=name, grid=(L, -(-R // EW_ROWS)), in_specs=[spec] * 4, out_specs=[spec] * 4,
        out_shape=[jax.ShapeDtypeStruct((L, R, k), F32)] * 4,
        compiler_params=_params(dimension_semantics=("arbitrary", "arbitrary")),
    )(w, g, m, v)


_SMALL = [("norm_pre", (2, 1024)), ("norm_post", (2, 1024)), ("norm_mem", (2, 1024)), ("a_log", (2, 4)),
          ("dt_bias", (2, 4)), ("dn_norm", (2, 128)), ("gm_norm", (2, 512)), ("spatial_w", (2, 4, 128, 128)),
          ("spatial_b", (2, 4, 128)), ("sinks", (2, 8))]
_SMALL_ROWS = 200
_BIG = ["w_in", "conv_w", "w_mem_kv", "w_up", "w_out"]
_NAMES = ["norm_pre", "norm_post", "norm_mem", "w_in", "conv_w", "a_log", "dt_bias", "dn_norm", "gm_norm",
          "spatial_w", "spatial_b", "sinks", "w_mem_kv", "w_up", "w_out"]


def _size(shape):
    n = 1
    for s in shape:
        n *= s
    return n


_PACK_UNIT = 8 * 1024


def _pack_small(d):
    rows = []
    for n, shp in _SMALL:
        flat = d[n].reshape(-1)
        rows.append(jnp.pad(flat, (0, -flat.shape[0] % _PACK_UNIT)).reshape(-1, 1024))
    assert sum(r.shape[0] for r in rows) == _SMALL_ROWS
    return jnp.concatenate(rows, axis=0)


def _unpack_small(p):
    out, off = {}, 0
    for n, shp in _SMALL:
        k = -(-_size(shp) // _PACK_UNIT) * 8
        out[n] = p[off:off + k].reshape(-1)[:_size(shp)].reshape(shp)
        off += k
    return out


_HALF_SHAPE = {"w_in": (SHARD_PAD // 2, D_MODEL), "conv_w": (2, 3 * BRANCH_W // N_CHIP), "w_mem_kv": (128, D_MODEL),
               "w_up": (2, BRANCH_W, D_MODEL // N_CHIP), "w_out": (128, D_MODEL)}


def _chip_major(g):
    g = jnp.swapaxes(g, 0, 1)
    return g.reshape((N_CHIP, 2 * g.shape[2]) + g.shape[3:])


def _half_major(g):
    g = g.reshape((N_CHIP, 2, g.shape[1] // 2) + g.shape[2:])
    return jnp.swapaxes(g, 0, 1).astype(BF16)


def _weight_views(l, gathered, small):
    g_in, g_conv, g_kv, g_up, g_out = gathered
    return _layer_weights(
        l, _w_pad_from_slabs(g_in), _chip_major(g_conv).transpose(1, 0, 2).reshape(4, 3 * BRANCH_W),
        _chip_major(g_kv).reshape(D_MODEL, D_MODEL),
        _chip_major(g_up).transpose(1, 2, 0, 3).reshape(4, BRANCH_W, D_MODEL),
        _chip_major(g_out).reshape(D_MODEL, D_MODEL), small)


def _pair_sums(g):
    big = [_slabs_from_pad(g["w_pad"]),
           _half_major(g["conv"].reshape(4, N_CHIP, 3 * BRANCH_W // N_CHIP).transpose(1, 0, 2)),
           _half_major(g["w_mem_kv"].reshape(N_CHIP, D_MODEL // N_CHIP, D_MODEL)),
           _half_major(jnp.stack(g["w_up"]).reshape(4, BRANCH_W, N_CHIP, D_MODEL // N_CHIP).transpose(2, 0, 1, 3)),
           _half_major(g["w_out"].reshape(N_CHIP, D_MODEL // N_CHIP, D_MODEL))]
    add2 = lambda a, b: [a.astype(F32) + b.astype(F32)]
    pair = []
    for n, b, p in zip(_BIG, big, _pair_exchange(big)):
        k = b.shape[-1]
        pair.append(_ew("pair_sum_" + n, add2, [(b.reshape(2, -1, k), "c"), p.reshape(-1, k)], 1, BF16)[0]
                    .reshape(p.shape))
    return pair


def _chip_sums(landed, pair, me):
    add4 = lambda a, b, c_, d: [((a.astype(F32) + b.astype(F32)) + c_.astype(F32)) + d.astype(F32)]
    totals = []
    for n, r, q in zip(_BIG, landed, pair):
        r = _own_slot(r, lax.dynamic_index_in_dim(q, me, 0), me, 0)
        k = r.shape[-1]
        totals.append(_ew("chip_sum_" + n, add4, [(r.reshape(N_CHIP, -1, k), s) for s in range(N_CHIP)], 1,
                          out_core_slot=True)[0].reshape((2,) + r.shape[1:]))
    return totals


def _own_slot(buf, mine, me, axis):
    return lax.dynamic_update_index_in_dim(buf, mine.astype(buf.dtype), me, axis)


def kernel(x, mem, norm_pre, norm_post, norm_mem, w_in, conv_w, a_log, dt_bias, dn_norm, gm_norm, spatial_w, spatial_b, sinks, w_mem_kv, w_up, w_out, loss_target, m_norm_pre, m_norm_post, m_norm_mem, m_w_in, m_conv_w, m_a_log, m_dt_bias, m_dn_norm, m_gm_norm, m_spatial_w, m_spatial_b, m_sinks, m_w_mem_kv, m_w_up, m_w_out, v_norm_pre, v_norm_post, v_norm_mem, v_w_in, v_conv_w, v_a_log, v_dt_bias, v_dn_norm, v_gm_norm, v_spatial_w, v_spatial_b, v_sinks, v_w_mem_kv, v_w_up, v_w_out):
    w = dict(norm_pre=norm_pre, norm_post=norm_post, norm_mem=norm_mem, w_in=w_in, conv_w=conv_w, a_log=a_log,
             dt_bias=dt_bias, dn_norm=dn_norm, gm_norm=gm_norm, spatial_w=spatial_w, spatial_b=spatial_b, sinks=sinks,
             w_mem_kv=w_mem_kv, w_up=w_up, w_out=w_out)
    m = dict(norm_pre=m_norm_pre, norm_post=m_norm_post, norm_mem=m_norm_mem, w_in=m_w_in, conv_w=m_conv_w,
             a_log=m_a_log, dt_bias=m_dt_bias, dn_norm=m_dn_norm, gm_norm=m_gm_norm, spatial_w=m_spatial_w,
             spatial_b=m_spatial_b, sinks=m_sinks, w_mem_kv=m_w_mem_kv, w_up=m_w_up, w_out=m_w_out)
    v = dict(norm_pre=v_norm_pre, norm_post=v_norm_post, norm_mem=v_norm_mem, w_in=v_w_in, conv_w=v_conv_w,
             a_log=v_a_log, dt_bias=v_dt_bias, dn_norm=v_dn_norm, gm_norm=v_gm_norm, spatial_w=v_spatial_w,
             spatial_b=v_spatial_b, sinks=v_sinks, w_mem_kv=v_w_mem_kv, w_up=v_w_up, w_out=v_w_out)
    me = 2 * lax.axis_index("x") + lax.axis_index("y")

    w_in_t = jnp.pad(w_in.astype(BF16).transpose(0, 2, 1), ((0, 0), (0, SHARD_PAD - SHARD_IN), (0, 0)))
    local = dict(w_in=w_in_t, conv_w=conv_w, w_mem_kv=w_mem_kv.astype(BF16), w_up=w_up.astype(BF16),
                 w_out=w_out.astype(BF16))
    halves = lambda l: [local[n][l].reshape((2,) + _HALF_SHAPE[n]) for n in _BIG]
    own = lambda gathered, mine: [_own_slot(g, h[:, None], me, 1) for g, h in zip(gathered, mine)]
    g0 = own(_gather_weights(halves(0)), halves(0))
    started = _split_start("gather_l1_start", "gather", halves(1),
                           [jax.ShapeDtypeStruct((2, N_CHIP) + _HALF_SHAPE[n], local[n].dtype) for n in _BIG], g0[1])

    xl, meml = x[0], mem[0]
    W0 = _weight_views(0, g0, w)
    W0["norm_pre"] = W0["norm_pre"] + started[3][0, 0]
    x1, sv0 = _layer_fwd(xl, meml, W0)
    mine1, landed1 = _split_wait("gather_l1_wait", "gather", started, x1)
    W1 = _weight_views(1, own(_pair_forward(landed1), mine1), w)
    x2, sv1 = _layer_fwd(x1, meml, W1)
    dy, lrows = _rows_fwd("loss", _loss_fn, [Row(x2, D_MODEL, 0), Row(loss_target[0], D_MODEL, 0)], [],
                          [(D_MODEL, F32), (LANES, F32)], TR)
    loss = lax.psum(jnp.sum(lrows[:, 0]), ("x", "y", "c"))

    dx1, grads1 = _layer_bwd(dy, meml, W1, sv1)
    pair1 = _pair_sums(grads1)
    scattering = _split_start("scatter_l1_start", "scatter", pair1,
                              [jax.ShapeDtypeStruct(p.shape, p.dtype) for p in pair1], pair1[1])
    W0["norm_post"] = W0["norm_post"] + scattering[3][0, 0]
    dx, grads0 = _layer_bwd(dx1, meml, W0, sv0)
    pair1, landed1 = _split_wait("scatter_l1_wait", "scatter", scattering, dx)
    pair0 = _pair_sums(grads0)
    totals = _pair_share(_chip_sums(_chip_scatter(pair0), pair0, me) + _chip_sums(landed1, pair1, me))
    shard = lambda t, n: t.reshape(local[n].shape[1:])
    gbig = {n: jnp.stack([shard(totals[i], n), shard(totals[len(_BIG) + i], n)]) for i, n in enumerate(_BIG)}
    grads = [grads0, grads1]

    small_local = dict(
        norm_pre=jnp.stack([g["norm_pre"][0] for g in grads]), norm_post=jnp.stack([g["norm_post"][0] for g in grads]),
        norm_mem=jnp.stack([g["norm_mem"][0] for g in grads]), a_log=jnp.stack([g["a_vec"][0, 4:8] for g in grads]),
        dt_bias=jnp.stack([g["dt_vec"][0, 4:8] for g in grads]), dn_norm=jnp.stack([g["dn_norm"][0] for g in grads]),
        gm_norm=jnp.stack([g["gm_norm"][0] for g in grads]), spatial_w=jnp.stack([g["spatial_w"] for g in grads]),
        spatial_b=jnp.stack([g["spatial_b"][:, 0, :] for g in grads]),
        sinks=jnp.stack([g["sink_vec"][0, :8] for g in grads]))
    gsmall_packed = _allreduce_small(_pack_small(small_local))

    d_s, m_s, v_s = _ew("adamw_small", _adamw_fn, [_pack_small(w), gsmall_packed, _pack_small(m), _pack_small(v)], 3)
    gsmall, dsmall, msmall, vsmall = (_unpack_small(p) for p in (gsmall_packed, d_s, m_s, v_s))
    g_o, d_o, m_o, v_o = dict(gsmall), dict(dsmall), dict(msmall), dict(vsmall)
    for n in _BIG:
        if n == "w_in":
            tr = lambda a: a.transpose(0, 2, 1)
            d_o[n], m_o[n], v_o[n], g_o[n] = [tr(o) for o in _adamw_rows("adamw_" + n, tr(w[n]), gbig[n], tr(m[n]), tr(v[n]))]
        else:
            g_o[n] = gbig[n]
            d_o[n], m_o[n], v_o[n] = _adamw("adamw_" + n, w[n], gbig[n], m[n], v[n])
    return (loss, dx[None], *[g_o[n] for n in _NAMES], *[d_o[n] for n in _NAMES], *[m_o[n] for n in _NAMES],
            *[v_o[n] for n in _NAMES])
```

```python
import collections
import functools

import jax
import jax.numpy as jnp
from jax import lax
from jax.experimental import pallas as pl
from jax.experimental.pallas import tpu as pltpu

F32 = jnp.float32
BF16 = jnp.bfloat16

D_MODEL = 1024
BRANCH_W = 512
MEM_LEN = 256
N_LAYER = 2
N_CHIP = 4
N_DEV = 8
EPS = 1e-6
NEG_INF = -1e30
DN_CHUNK = 64
LANES = 128
VMEM_LIMIT = 48 * 1024 * 1024

ADAM_LR, ADAM_B1, ADAM_B2, ADAM_EPS, ADAM_WD, ADAM_STEP = 0.001, 0.9, 0.999, 1e-08, 0.01, 10

N_PAD = 10240
O_GATE = 0
O_AQ, O_AK, O_AV, O_AZ = 4096, 4608, 5120, 5632
O_BUV, O_BZ = 6144, 7168
O_CKV, O_BA = 7680, 7936
O_CQ, O_CZ = 8192, 8704
O_MQ, O_MZ = 9216, 9728
O_MISC, W_MISC = O_CKV, 512
_PAD_SEGS = [(5896, 4096), (0, 512), (512, 512), (1024, 512), (1536, 512), (2056, 1024), (3080, 512),
             (4104, 128), (4232, 128), (2048, 8), (None, 120), (None, 128),
             (3592, 512), (4360, 512), (4872, 512), (5384, 512)]
D_IN = 9992
SHARD_IN = D_IN // N_CHIP


SHARD_PAD = 2560


def _pad_parts():
    parts, off = [], 0
    for s, n in _PAD_SEGS:
        a = s
        while s is not None and a < s + n:
            chip = a // SHARD_IN
            b = min(s + n, (chip + 1) * SHARD_IN)
            parts.append((chip, a - chip * SHARD_IN, off + a - s, b - a))
            a = b
        off += n
    return parts


PERM_ROWS = 512
PERM_SLACK = 32


def _permute_rows(name, src, parts, n_out, out_dtype):
    B, Z = PERM_ROWS, PERM_ROWS + PERM_SLACK
    w = src.shape[1]
    plans = []
    for blk in range(n_out // B):
        o, runs = blk * B, []
        for s, d, n in parts:
            lo, hi = max(d, o), min(d + n, o + B)
            if lo < hi:
                s0 = s + lo - d
                wa = s0 // 16 * 16
                wb = min(-(-(s0 + hi - lo) // 16) * 16, src.shape[0])
                runs.append((wa, wb - wa, s0 - (lo - o) - wa, lo - o, hi - o))
        plans.append(runs)
    max_runs = max(len(r) for r in plans)
    nblk = len(plans)

    def body(*refs):
        src_ref, out_ref, inbuf, obuf, insem, outsem = (refs[0],) + refs[-5:]

        def in_copies(blk):
            return [pltpu.make_async_copy(src_ref.at[pl.ds(wa, ws)], inbuf.at[blk % 2, r, pl.ds(0, ws)],
                                          insem.at[blk % 2, r]) for r, (wa, ws, _, _, _) in enumerate(plans[blk])]

        def out_copy(blk):
            return pltpu.make_async_copy(obuf.at[blk % 2], out_ref.at[pl.ds(blk * B, B)], outsem.at[blk % 2])

        for cp in in_copies(0):
            cp.start()
        rid = _iota((B, 1), 0)
        for blk in range(nblk):
            if blk + 1 < nblk:
                for cp in in_copies(blk + 1):
                    cp.start()
            for cp in in_copies(blk):
                cp.wait()
            val = jnp.zeros((B, w), F32)
            for r, (wa, ws, t, l0, l1) in enumerate(plans[blk]):
                win = jnp.concatenate([inbuf[blk % 2, r, pl.ds(0, ws)].astype(F32), jnp.zeros((Z - ws, w), F32)], axis=0)
                moved = pltpu.roll(win, (-t) % Z, 0)[:B]
                val = jnp.where((rid >= l0) & (rid < l1), moved, val)
            if blk >= 2:
                out_copy(blk - 2).wait()
            obuf[blk % 2] = val.astype(out_dtype)
            out_copy(blk).start()
        for blk in range(max(nblk - 2, 0), nblk):
            out_copy(blk).wait()

    return pl.pallas_call(
        body, name=name, in_specs=[_ANY], out_specs=_ANY, out_shape=jax.ShapeDtypeStruct((n_out, w), out_dtype),
        scratch_shapes=[pltpu.VMEM((2, max_runs, Z, w), src.dtype), pltpu.VMEM((2, B, w), out_dtype),
                        pltpu.SemaphoreType.DMA((2, max_runs)), pltpu.SemaphoreType.DMA((2,))],
        compiler_params=_params(),
    )(src)


def _slab_parts():
    h, out = SHARD_PAD // 2, []
    for chip, s, d, n in _pad_parts():
        a = s
        while a < s + n:
            half = a // h
            b = min(s + n, (half + 1) * h)
            out.append(((half * N_CHIP + chip) * h + a - half * h, d + a - s, b - a))
            a = b
    return out


def _w_pad_from_slabs(slabs):
    return _permute_rows("w_pad_rows", slabs.reshape(-1, slabs.shape[-1]), _slab_parts(), N_PAD, BF16)


def _slabs_from_pad(dw):
    slabs = _permute_rows("w_pad_grad_rows", dw, [(d, s, n) for s, d, n in _slab_parts()], N_CHIP * SHARD_PAD, BF16)
    return slabs.reshape(2, N_CHIP, SHARD_PAD // 2, dw.shape[1])


def _dot(a, b, dims, prec):
    if prec == "bf16":
        return lax.dot_general(a.astype(BF16), b.astype(BF16), (dims, ((), ())), preferred_element_type=F32)
    return lax.dot_general(a, b, (dims, ((), ())), precision=lax.Precision.HIGHEST, preferred_element_type=F32)


_NN, _NT, _TN = ((1,), (0,)), ((1,), (1,)), ((0,), (0,))


def _make_mm(prec):
    @jax.custom_vjp
    def nn(a, b):
        return _dot(a, b, _NN, prec)

    @jax.custom_vjp
    def nt(a, b):
        return _dot(a, b, _NT, prec)

    @jax.custom_vjp
    def tn(a, b):
        return _dot(a, b, _TN, prec)

    nn.defvjp(lambda a, b: (nn(a, b), (a, b)), lambda r, g: (nt(g, r[1]), tn(r[0], g)))
    nt.defvjp(lambda a, b: (nt(a, b), (a, b)), lambda r, g: (nn(g, r[1]), tn(g, r[0])))
    tn.defvjp(lambda a, b: (tn(a, b), (a, b)), lambda r, g: (nt(r[1], g), nn(r[0], g)))
    return nn, nt, tn


_nn16, _nt16, _tn16 = _make_mm("bf16")
_nn32, _nt32, _tn32 = _make_mm("f32")


def _make_slice(axis):
    @functools.partial(jax.custom_vjp, nondiff_argnums=(1, 2, 3))
    def sl(x, a, b, n):
        return x[a:b] if axis == 0 else x[:, a:b]

    def fwd(x, a, b, n):
        return sl(x, a, b, n), None

    def bwd(a, b, n, _, g):
        parts = []
        if a > 0:
            parts.append(jnp.zeros((a, g.shape[1]) if axis == 0 else (g.shape[0], a), g.dtype))
        parts.append(g)
        if n - b > 0:
            parts.append(jnp.zeros((n - b, g.shape[1]) if axis == 0 else (g.shape[0], n - b), g.dtype))
        return (jnp.concatenate(parts, axis=axis),)

    sl.defvjp(fwd, bwd)
    return sl


_sl0, _sl1 = _make_slice(0), _make_slice(1)


def _rowsl(x, a, b):
    return _sl0(x, a, b, x.shape[0])


def _cols(x, a, b):
    return _sl1(x, a, b, x.shape[1])


@functools.partial(jax.custom_vjp, nondiff_argnums=(1,))
def _rollr(x, s):
    return pltpu.roll(x, s, 0)


_rollr.defvjp(lambda x, s: (_rollr(x, s), None),
              lambda s, _, g: (pltpu.roll(g, g.shape[0] - s, 0),))


def _iota(shape, axis):
    return lax.broadcasted_iota(jnp.int32, shape, axis)


def _sigmoid(x):
    return lax.logistic(x)


def _silu(x):
    return x * _sigmoid(x)


def _gelu(x):
    return 0.5 * x * (1.0 + jnp.tanh(0.7978845608028654 * (x + 0.044715 * (x * x * x))))


def _softplus(x):
    return jnp.maximum(x, 0.0) + jnp.log(1.0 + jnp.exp(-jnp.abs(x)))


def _rms(x, g):
    return x * lax.rsqrt(jnp.mean(x * x, axis=-1, keepdims=True) + EPS) * g


def _lane_pick(x, lane):
    return jnp.sum(x * (_iota((1, x.shape[1]), 1) == lane).astype(F32), axis=1, keepdims=True)


Row = collections.namedtuple("Row", "arr w cb hb grad", defaults=(0, True))


def _full_spec(shape):
    return pl.BlockSpec(shape, lambda i, _n=len(shape): (0,) * _n)


def _load_params(refs):
    return [[p[g].astype(F32) for g in range(p.shape[0])] if len(p.shape) == 3 else p[...].astype(F32)
            for p in refs]


def _params(**kw):
    return pltpu.CompilerParams(vmem_limit_bytes=VMEM_LIMIT, **kw)


def _rows_fwd(name, fn, rows, params, outs, tr, carry=None):
    T = rows[0].arr.shape[0]
    n = T // tr
    halos = [r for r in rows if r.hb]
    nr, nh, npar, no = len(rows), len(halos), len(params), len(outs)

    def body(*refs):
        row_refs, halo_refs = refs[:nr], refs[nr:nr + nh]
        par_refs = refs[nr + nh:nr + nh + npar]
        out_refs = refs[nr + nh + npar:nr + nh + npar + no]
        rest = refs[nr + nh + npar + no:]
        first = pl.program_id(0) == 0
        cvals = None
        if carry is not None:
            csave_ref, carry_ref = rest

            @pl.when(first)
            def _():
                carry_ref[...] = jnp.zeros_like(carry_ref)

            cvals = [carry_ref[g] for g in range(carry[0])]
            for g in range(carry[0]):
                csave_ref[0, g] = cvals[g]
        c_out, o = fn(first, cvals, [r[...].astype(F32) for r in row_refs],
                      [h[...].astype(F32) for h in halo_refs], _load_params(par_refs))
        for r, v in zip(out_refs, o):
            r[...] = v.astype(r.dtype)
        if carry is not None:
            for g in range(carry[0]):
                carry_ref[g] = c_out[g]

    in_specs = [pl.BlockSpec((tr, r.w), lambda i, c=r.cb: (i, c)) for r in rows]
    in_specs += [pl.BlockSpec((r.hb, r.w), lambda i, c=r.cb, q=tr // r.hb: (jnp.maximum(i * q - 1, 0), c))
                 for r in halos]
    in_specs += [_full_spec(p.shape) for p in params]
    out_shape = [jax.ShapeDtypeStruct((T, w), dt) for w, dt in outs]
    out_specs = [pl.BlockSpec((tr, w), lambda i: (i, 0)) for w, _ in outs]
    scratch = []
    if carry is not None:
        out_shape.append(jax.ShapeDtypeStruct((n,) + carry, F32))
        out_specs.append(pl.BlockSpec((1,) + carry, lambda i: (i, 0, 0, 0)))
        scratch.append(pltpu.VMEM(carry, F32))
    return pl.pallas_call(
        body, name=name, grid=(n,), in_specs=in_specs, out_specs=out_specs, out_shape=out_shape,
        scratch_shapes=scratch, compiler_params=_params(dimension_semantics=("arbitrary",)),
    )(*[r.arr for r in rows], *[r.arr for r in halos], *params)


def _rows_bwd(name, fn, rows, params, douts, tr, carry=None, csave=None, dcols=None):
    T = rows[0].arr.shape[0]
    n = T // tr
    halos = [r for r in rows if r.hb]
    grows = [r for r in rows if r.grad is True]
    crows = [r for r in rows if r.grad == "cols"]
    wcols = sum(r.w for r in crows)
    nr, nh, npar, nd, ng = len(rows), len(halos), len(params), len(douts), len(grows)
    nc = 0 if carry is None else 1
    ncol = 1 if crows else 0
    nalias = 1 if (crows and dcols is not None) else 0

    def body(*refs):
        row_refs, halo_refs = refs[:nr], refs[nr:nr + nh]
        par_refs = refs[nr + nh:nr + nh + npar]
        k = nr + nh + npar
        csave_ref = refs[k] if nc else None
        dout_refs = refs[k + nc:k + nc + nd]
        k = k + nc + nd + nalias
        drow_refs = refs[k:k + ng]
        dcols_ref = refs[k + ng] if ncol else None
        dpar_refs = refs[k + ng + ncol:k + ng + ncol + npar]
        k = k + ng + ncol + npar
        dcarry_ref = refs[k] if nc else None
        hgrad_refs = refs[k + nc:]
        i = pl.program_id(0)
        first_tile = i == n - 1

        @pl.when(i == 0)
        def _():
            for r in dpar_refs:
                r[...] = jnp.zeros_like(r)
            for r in hgrad_refs:
                r[...] = jnp.zeros_like(r)
            if nc:
                dcarry_ref[...] = jnp.zeros_like(dcarry_ref)

        rv = [r[...].astype(F32) for r in row_refs]
        hv = [h[...].astype(F32) for h in halo_refs]
        pv = _load_params(par_refs)
        dov = [d[...].astype(F32) for d in dout_refs]
        if nc:
            cv = [csave_ref[0, g] for g in range(carry[0])]
            _, vjp = jax.vjp(lambda c, r, h, p: fn(first_tile, c, r, h, p), cv, rv, hv, pv)
            dc, dr, dh, dp = vjp(([dcarry_ref[g] for g in range(carry[0])], dov))
            for g in range(carry[0]):
                dcarry_ref[g] = dc[g]
        else:
            _, vjp = jax.vjp(lambda r, h, p: fn(first_tile, None, r, h, p)[1], rv, hv, pv)
            dr, dh, dp = vjp(dov)
        gi = hi = 0
        pieces = []
        for kk, r in enumerate(rows):
            d = dr[kk]
            if r.hb:
                carried = hgrad_refs[hi][...]
                d = d + (carried if tr == r.hb else
                         jnp.concatenate([jnp.zeros((tr - r.hb, r.w), F32), carried], axis=0))
                hgrad_refs[hi][...] = dh[hi]
                hi += 1
            if r.grad is True:
                drow_refs[gi][...] = d.astype(drow_refs[gi].dtype)
                gi += 1
            elif r.grad == "cols":
                pieces.append(d.astype(BF16))
        if ncol:
            dcols_ref[...] = pieces[0] if len(pieces) == 1 else jnp.concatenate(pieces, axis=1)
        for r, d in zip(dpar_refs, dp):
            if len(r.shape) == 3:
                for g in range(r.shape[0]):
                    r[g] += d[g]
            else:
                r[...] += d

    rev = lambda i: n - 1 - i
    in_specs = [pl.BlockSpec((tr, r.w), lambda i, c=r.cb: (rev(i), c)) for r in rows]
    in_specs += [pl.BlockSpec((r.hb, r.w), lambda i, c=r.cb, q=tr // r.hb: (jnp.maximum(rev(i) * q - 1, 0), c))
                 for r in halos]
    in_specs += [_full_spec(p.shape) for p in params]
    args = [r.arr for r in rows] + [r.arr for r in halos] + list(params)
    scratch = []
    if nc:
        in_specs.append(pl.BlockSpec((1,) + carry, lambda i: (rev(i), 0, 0, 0)))
        args.append(csave)
        scratch.append(pltpu.VMEM(carry, F32))
    in_specs += [pl.BlockSpec((tr, d.shape[1]), lambda i: (rev(i), 0)) for d in douts]
    args += list(douts)
    aliases = {}
    if nalias:
        aliases = {len(args): ng}
        in_specs.append(pl.BlockSpec(memory_space=pl.ANY))
        args.append(dcols)
    scratch += [pltpu.VMEM((r.hb, r.w), F32) for r in halos]
    out_shape = [jax.ShapeDtypeStruct((T, r.w), F32) for r in grows]
    out_specs = [pl.BlockSpec((tr, r.w), lambda i: (rev(i), 0)) for r in grows]
    if ncol:
        off = crows[0].cb * crows[0].w
        assert off % wcols == 0 and all(a.cb * a.w + a.w == b.cb * b.w for a, b in zip(crows, crows[1:]))
        out_shape.append(jax.ShapeDtypeStruct((T, N_PAD), BF16))
        out_specs.append(pl.BlockSpec((tr, wcols), lambda i, c=off // wcols: (rev(i), c)))
    out_shape += [jax.ShapeDtypeStruct(p.shape, F32) for p in params]
    out_specs += [_full_spec(p.shape) for p in params]
    res = pl.pallas_call(
        body, name=name, grid=(n,), in_specs=in_specs, out_specs=out_specs, out_shape=out_shape,
        scratch_shapes=scratch, input_output_aliases=aliases,
        compiler_params=_params(dimension_semantics=("arbitrary",)),
    )(*args)
    return list(res[:ng]), list(res[ng + ncol:]), (res[ng] if ncol else dcols)


def _fill_misc(dcols, dkv, dba, tr):
    T = dkv.shape[0]

    def body(kv_ref, ba_ref, _, o_ref):
        o_ref[...] = jnp.concatenate([kv_ref[...], ba_ref[...]], axis=1).astype(BF16)

    return pl.pallas_call(
        body, name="misc_bwd", grid=(T // tr,),
        in_specs=[pl.BlockSpec((tr, 256), lambda i: (i, 0)), pl.BlockSpec((tr, 256), lambda i: (i, 0)),
                  pl.BlockSpec(memory_space=pl.ANY)],
        out_specs=pl.BlockSpec((tr, W_MISC), lambda i: (i, O_MISC // W_MISC)),
        out_shape=jax.ShapeDtypeStruct((T, N_PAD), BF16), input_output_aliases={2: 0},
        compiler_params=_params(dimension_semantics=("arbitrary",)),
    )(dkv, dba, dcols)


def _matmul(name, a, b, kind, out_dtype, tm, tn, tk):
    if kind == "tn":
        (K, M), N = a.shape, b.shape[1]
    else:
        (M, K), N = a.shape, (b.shape[0] if kind == "nt" else b.shape[1])
    tm, tn, tk = min(tm, M), min(tn, N), min(tk, K)
    nk = K // tk
    dims = {"nn": _NN, "nt": _NT, "tn": _TN}[kind]

    def body(a_ref, b_ref, o_ref, acc_ref):
        k = pl.program_id(2)

        @pl.when(k == 0)
        def _():
            acc_ref[...] = jnp.zeros_like(acc_ref)

        acc_ref[...] += lax.dot_general(a_ref[...], b_ref[...], (dims, ((), ())), preferred_element_type=F32)

        @pl.when(k == nk - 1)
        def _():
            o_ref[...] = acc_ref[...].astype(o_ref.dtype)

    a_spec = pl.BlockSpec((tk, tm), lambda i, j, k: (k, i)) if kind == "tn" else pl.BlockSpec((tm, tk), lambda i, j, k: (i, k))
    b_spec = pl.BlockSpec((tn, tk), lambda i, j, k: (j, k)) if kind == "nt" else pl.BlockSpec((tk, tn), lambda i, j, k: (k, j))
    return pl.pallas_call(
        body, name=name, grid=(M // tm, N // tn, nk), in_specs=[a_spec, b_spec],
        out_specs=pl.BlockSpec((tm, tn), lambda i, j, k: (i, j)),
        out_shape=jax.ShapeDtypeStruct((M, N), out_dtype),
        scratch_shapes=[pltpu.VMEM((tm, tn), F32)],
        compiler_params=_params(dimension_semantics=("arbitrary", "arbitrary", "arbitrary")),
    )(a, b)


def _pre_fn(first, _, rows, halos, params):
    return None, [_rms(rows[0], params[0])]


def _pre_fn_res(first, _, rows, halos, params):
    return None, [_rms(rows[0], params[0]), rows[0]]


def _memkv_fn(first, _, rows, halos, params):
    g, w = params
    return None, [_nn16(_rms(rows[0], g), w)]


def _conv_silu(x, halo, w4, keep_halo):
    tr = x.shape[0]
    halo = halo * keep_halo
    rid = _iota((tr, 1), 0)
    acc = w4[3] * x
    for s in (1, 2, 3):
        hs = jnp.concatenate([_rollr(halo, s), jnp.zeros((tr - halo.shape[0], x.shape[1]), F32)], axis=0)
        acc = acc + w4[3 - s] * jnp.where(rid < s, hs, _rollr(x, s))
    return _silu(acc)


def _dn_fn(first, S, rows, halos, params):
    qp, kp, vp, z, ba = rows
    conv, a_vec, dt_vec, dnorm = params
    ba = _cols(ba, 0, LANES)
    tr = qp.shape[0]
    keep = jnp.where(first, 0.0, 1.0)
    q = _conv_silu(qp, halos[0], [conv[3 * j + 0] for j in range(4)], keep)
    k = _conv_silu(kp, halos[1], [conv[3 * j + 1] for j in range(4)], keep)
    v = _conv_silu(vp, halos[2], [conv[3 * j + 2] for j in range(4)], keep)
    qh, kh, vh = [], [], []
    for h in range(4):
        a, b = h * LANES, (h + 1) * LANES
        xq, xk = _cols(q, a, b), _cols(k, a, b)
        qh.append(xq * lax.rsqrt(jnp.sum(xq * xq, axis=1, keepdims=True) + EPS) * (LANES ** -0.5))
        kh.append(xk * lax.rsqrt(jnp.sum(xk * xk, axis=1, keepdims=True) + EPS))
        vh.append(_cols(v, a, b))
    beta_all = _sigmoid(ba)
    g_all = -jnp.exp(a_vec) * _softplus(ba + dt_vec)
    C = DN_CHUNK
    ii, jj = _iota((C, C), 0), _iota((C, C), 1)
    strict, incl = ii > jj, ii >= jj
    eye = (ii == jj).astype(F32)
    last_row = (_iota((C, 1), 0) == C - 1).astype(F32)
    n_chunk = tr // C
    pairs = [(c, h) for c in range(n_chunk) for h in range(4)]
    rows_of = lambda a, c: _rowsl(a, c * C, (c + 1) * C)
    gcs = [_nn32(incl.astype(F32), rows_of(g_all, c)) for c in range(n_chunk)]
    qc = {(c, h): rows_of(qh[h], c) for c, h in pairs}
    kc = {(c, h): rows_of(kh[h], c) for c, h in pairs}
    beta = {(c, h): _lane_pick(rows_of(beta_all, c), h) for c, h in pairs}
    gc = {(c, h): _lane_pick(gcs[c], 4 + h) for c, h in pairs}
    dec = {p: jnp.exp(jnp.where(incl, gc[p] - jnp.sum(eye * gc[p], axis=0, keepdims=True), 0.0)) for p in pairs}
    egc = {p: jnp.exp(gc[p]) for p in pairs}
    kb = {p: kc[p] * beta[p] for p in pairs}
    kq = {p: _nt16(jnp.concatenate([kb[p], qc[p]], axis=0), kc[p]) for p in pairs}
    P = {p: -jnp.where(strict, _rowsl(kq[p], 0, C) * dec[p], 0.0) for p in pairs}
    aqk = {p: jnp.where(incl, _rowsl(kq[p], C, 2 * C) * dec[p], 0.0) for p in pairs}
    tinv = {p: eye + P[p] for p in pairs}
    P = {p: _nn16(P[p], P[p]) for p in pairs}
    for j in range(5):
        if j < 4:
            pt = {p: _nn16(jnp.concatenate([P[p], tinv[p]], axis=0), P[p]) for p in pairs}
            tinv = {p: tinv[p] + _rowsl(pt[p], C, 2 * C) for p in pairs}
            P = {p: _rowsl(pt[p], 0, C) for p in pairs}
        else:
            tinv = {p: tinv[p] + _nn16(tinv[p], P[p]) for p in pairs}
    uw = {(c, h): _nn16(tinv[c, h], jnp.concatenate([rows_of(vh[h], c) * beta[c, h], kb[c, h] * egc[c, h]], axis=1))
          for c, h in pairs}
    S = list(S)
    ychunks = []
    for c in range(n_chunk):
        zc = rows_of(z, c)
        hs = range(4)
        ws = [_nn16(jnp.concatenate([_cols(uw[c, h], LANES, 2 * LANES), qc[c, h] * egc[c, h]], axis=0), S[h]) for h in hs]
        vnew = [_cols(uw[c, h], 0, LANES) - _rowsl(ws[h], 0, C) for h in hs]
        o = [_rowsl(ws[h], C, 2 * C) + _nn16(aqk[c, h], vnew[h]) for h in hs]
        glast = [jnp.sum(gc[c, h] * last_row, axis=0, keepdims=True) for h in hs]
        S = [S[h] * jnp.exp(glast[h]) + _tn16(kc[c, h] * jnp.exp(glast[h] - gc[c, h]), vnew[h]) for h in hs]
        ychunks.append(jnp.concatenate(
            [_rms(o[h], dnorm) * _silu(_cols(zc, h * LANES, (h + 1) * LANES)) for h in hs], axis=1))
    return S, [jnp.concatenate(ychunks, axis=0)]


def _gm_fn(first, _, rows, halos, params):
    uv, z = rows
    gnorm, ws, bs = params
    tr = uv.shape[0]
    guv = _gelu(uv)
    u = _cols(guv, 0, BRANCH_W)
    v = _rms(_cols(guv, BRANCH_W, 2 * BRANCH_W), gnorm)
    ii, jj = _iota((LANES, LANES), 0), _iota((LANES, LANES), 1)
    eye = (ii == jj).astype(F32)
    wsm = [jnp.where(ii >= jj, ws[g], 0.0) for g in range(4)]
    bcol = [jnp.sum(eye * bs[g], axis=1, keepdims=True) for g in range(4)]
    chunks = []
    for c in range(tr // LANES):
        vc = _rowsl(v, c * LANES, (c + 1) * LANES)
        chunks.append(jnp.concatenate(
            [_nn16(wsm[g], _cols(vc, g * LANES, (g + 1) * LANES)) + bcol[g] for g in range(4)], axis=1))
    return None, [u * jnp.concatenate(chunks, axis=0) * _silu(z)]


def _swa_fn(first, _, rows, halos, params):
    q, kvc, z = rows
    sink_vec = params[0]
    P = LANES
    kv = jnp.concatenate([halos[0], kvc], axis=0)
    k, v = _cols(kv, 0, P), _cols(kv, P, 2 * P)
    r, cc = _iota((P, P), 0), _iota((P, P), 1)
    lane = _iota((1, P), 1)
    dist = _iota((P, 2 * P), 0) + P - _iota((P, 2 * P), 1)
    kmin = jnp.where(first, P, 0)
    valid = (dist >= 0) & (dist < P) & (_iota((P, 2 * P), 1) >= kmin)
    blocks = [None] * 4
    for kh in range(2):
        dup = (r == kh * 64 + (cc & 63)).astype(F32)
        kk, vv = _nn16(k, dup), _nn16(v, dup)
        for g in range(4):
            h = kh * 4 + g
            half = ((lane >= 64) == (h % 2 == 1)).astype(F32)
            qb = _cols(q, (h // 2) * P, (h // 2 + 1) * P) * half
            s = jnp.where(valid, _nt16(qb, kk) * 0.125, NEG_INF)
            sink = _lane_pick(sink_vec, h)
            m = lax.stop_gradient(jnp.maximum(jnp.max(s, axis=1, keepdims=True), sink))
            e = jnp.exp(s - m)
            p = e / (jnp.sum(e, axis=1, keepdims=True) + jnp.exp(sink - m))
            o = _nn16(p, vv) * half
            blocks[h // 2] = o if blocks[h // 2] is None else blocks[h // 2] + o
    return None, [jnp.concatenate(blocks, axis=1) * _silu(z)]


def _mem_fn(first, _, rows, halos, params):
    q, z = rows
    mkv = params[0]
    outs = []
    for h in range(4):
        a, b = h * LANES, (h + 1) * LANES
        s = _nt16(_cols(q, a, b), _cols(mkv, a, b)) * (LANES ** -0.5)
        m = lax.stop_gradient(jnp.max(s, axis=1, keepdims=True))
        e = jnp.exp(s - m)
        p = e / jnp.sum(e, axis=1, keepdims=True)
        outs.append(_nn16(p, _cols(mkv, BRANCH_W + a, BRANCH_W + b)))
    return None, [jnp.concatenate(outs, axis=1) * _silu(z)]


def _up_fn(first, _, rows, halos, params):
    ys, gl, w_up = rows[:4], rows[4], params[0]
    merged = None
    for n in range(4):
        term = _sigmoid(_cols(gl, n * D_MODEL, (n + 1) * D_MODEL)) * _nn16(ys[n], w_up[n])
        merged = term if merged is None else merged + term
    return None, [merged]


def _out_fn(first, _, rows, halos, params):
    x, merged = rows
    w, g = params
    return None, [x + _rms(_nn16(merged, w), g)]


def _loss_fn(first, _, rows, halos, params):
    y, t = rows
    d = y - t
    lrow = 0.5 * jnp.mean(d * d, axis=1, keepdims=True)
    return None, [d * (1.0 / D_MODEL), jnp.broadcast_to(lrow, (y.shape[0], LANES))]


TR = 256
DN_TR = 256
UP_TR = 128
CONV_HALO = 16
CARRY = (4, LANES, LANES)


def _branch_rows(cols, g):
    hb = CONV_HALO
    a = [Row(cols, 512, O_AQ // 512, hb, g), Row(cols, 512, O_AK // 512, hb, g), Row(cols, 512, O_AV // 512, hb, g),
         Row(cols, 512, O_AZ // 512, 0, g), Row(cols, 256, O_BA // 256)]
    b = [Row(cols, 1024, O_BUV // 1024, 0, g), Row(cols, 512, O_BZ // 512, 0, g)]
    c = [Row(cols, 512, O_CQ // 512, 0, g), Row(cols, 256, O_CKV // 256, LANES), Row(cols, 512, O_CZ // 512, 0, g)]
    m = [Row(cols, 512, O_MQ // 512, 0, g), Row(cols, 512, O_MZ // 512, 0, g)]
    return a, b, c, m


def _layer_fwd(x, mem, W):
    h = _rows_fwd("prenorm_fwd", _pre_fn, [Row(x, D_MODEL, 0)], [W["norm_pre"]], [(D_MODEL, BF16)], TR)[0]
    cols = _matmul("in_proj_fwd", h, W["w_pad"], "nt", BF16, 1024, 1024, 1024)
    mem_kv = _rows_fwd("memkv_fwd", _memkv_fn, [Row(mem, D_MODEL, 0)], [W["norm_mem"], W["w_mem_kv"]],
                       [(D_MODEL, F32)], MEM_LEN)[0]
    ra, rb, rc, rm = _branch_rows(cols, True)
    y_a, csave = _rows_fwd("dn_fwd", _dn_fn, ra, [W["conv"], W["a_vec"], W["dt_vec"], W["dn_norm"]],
                           [(BRANCH_W, BF16)], DN_TR, CARRY)
    y_b = _rows_fwd("gm_fwd", _gm_fn, rb, [W["gm_norm"], W["spatial_w"], W["spatial_b"]], [(BRANCH_W, BF16)], TR)[0]
    y_c = _rows_fwd("swa_fwd", _swa_fn, rc, [W["sink_vec"]], [(BRANCH_W, BF16)], LANES)[0]
    y_m = _rows_fwd("mem_fwd", _mem_fn, rm, [mem_kv], [(BRANCH_W, BF16)], TR)[0]
    ys = [y_a, y_b, y_c, y_m]
    merged = _rows_fwd("up_fwd", _up_fn, [Row(y, BRANCH_W, 0) for y in ys] + [Row(cols, 4 * D_MODEL, 0)],
                       [W["w_up"]], [(D_MODEL, BF16)], UP_TR)[0]
    x_new = _rows_fwd("out_fwd", _out_fn, [Row(x, D_MODEL, 0), Row(merged, D_MODEL, 0)],
                      [W["w_out"], W["norm_post"]], [(D_MODEL, F32)], TR)[0]
    return x_new, dict(x=x, h=h, cols=cols, mem_kv=mem_kv, csave=csave, ys=ys, merged=merged)


def _layer_bwd(dxn, mem, W, sv):
    x, cols = sv["x"], sv["cols"]
    (dx_res, dm), (dw_out, dnorm_post), _ = _rows_bwd(
        "out_bwd", _out_fn, [Row(x, D_MODEL, 0), Row(sv["merged"], D_MODEL, 0)], [W["w_out"], W["norm_post"]],
        [dxn], TR)
    dys, (dw_up,), dcols = _rows_bwd(
        "up_bwd", _up_fn, [Row(y, BRANCH_W, 0) for y in sv["ys"]] + [Row(cols, 4 * D_MODEL, 0, 0, "cols")],
        [W["w_up"]], [dm], UP_TR)
    ra, rb, rc, rm = _branch_rows(cols, "cols")
    (dba,), (dconv, da_vec, ddt_vec, ddn_norm), dcols = _rows_bwd(
        "dn_bwd", _dn_fn, ra, [W["conv"], W["a_vec"], W["dt_vec"], W["dn_norm"]], [dys[0]], DN_TR, CARRY,
        sv["csave"], dcols=dcols)
    _, (dgm_norm, dws, dbs), dcols = _rows_bwd(
        "gm_bwd", _gm_fn, rb, [W["gm_norm"], W["spatial_w"], W["spatial_b"]], [dys[1]], TR, dcols=dcols)
    (dkv_c,), (dsink,), dcols = _rows_bwd("swa_bwd", _swa_fn, rc, [W["sink_vec"]], [dys[2]], LANES, dcols=dcols)
    _, (dmem_kv,), dcols = _rows_bwd("mem_bwd", _mem_fn, rm, [sv["mem_kv"]], [dys[3]], TR, dcols=dcols)
    dcols = _fill_misc(dcols, dkv_c, dba, TR)
    _, (dnorm_mem, dw_mem_kv), _ = _rows_bwd("memkv_bwd", _memkv_fn, [Row(mem, D_MODEL, 0, 0, False)],
                                             [W["norm_mem"], W["w_mem_kv"]], [dmem_kv], MEM_LEN)
    dw_pad = _matmul("in_proj_dw", dcols, sv["h"], "tn", F32, 1024, 1024, 1024)
    dh = _matmul("in_proj_dx", dcols, W["w_pad"], "nn", F32, 1024, 1024, 1024)
    (dx,), (dnorm_pre,), _ = _rows_bwd("prenorm_bwd", _pre_fn_res, [Row(x, D_MODEL, 0)], [W["norm_pre"]],
                                       [dh, dx_res], TR)
    grads = dict(norm_pre=dnorm_pre, norm_post=dnorm_post, norm_mem=dnorm_mem, w_pad=dw_pad, conv=dconv,
                 a_vec=da_vec, dt_vec=ddt_vec, dn_norm=ddn_norm, gm_norm=dgm_norm, spatial_w=dws, spatial_b=dbs,
                 sink_vec=dsink, w_mem_kv=dw_mem_kv, w_up=dw_up, w_out=dw_out)
    return dx, grads


def _lane_vec(v, off):
    return jnp.zeros((1, LANES), F32).at[0, off:off + v.shape[0]].set(v)


def _layer_weights(l, w_pad, conv_w, w_mem_kv, w_up, w_out, small):
    return dict(
        w_pad=w_pad, conv=conv_w.reshape(4, 3, BRANCH_W).reshape(12, 1, BRANCH_W),
        w_mem_kv=w_mem_kv, w_up=w_up, w_out=w_out,
        norm_pre=small["norm_pre"][l][None], norm_post=small["norm_post"][l][None],
        norm_mem=small["norm_mem"][l][None],
        a_vec=_lane_vec(small["a_log"][l], 4), dt_vec=_lane_vec(small["dt_bias"][l], 4),
        dn_norm=small["dn_norm"][l][None], gm_norm=small["gm_norm"][l][None],
        spatial_w=small["spatial_w"][l], spatial_b=small["spatial_b"][l][:, None, :],
        sink_vec=_lane_vec(small["sinks"][l], 0))


_MESH = pl.DeviceIdType.MESH
_ANY = pl.BlockSpec(memory_space=pl.ANY)


def _position():
    return lax.axis_index("x"), lax.axis_index("y"), lax.axis_index("c")


def _remote(src, dst, send_sem, recv_sem, dev):
    return pltpu.make_async_remote_copy(src_ref=src, dst_ref=dst, send_sem=send_sem, recv_sem=recv_sem,
                                        device_id=dev, device_id_type=_MESH)


def _hbm_call(name, body, arrs, out_shapes, sems, aliases=None):
    return pl.pallas_call(
        body, name=name, in_specs=[_ANY] * len(arrs), out_specs=[_ANY] * len(out_shapes), out_shape=out_shapes,
        scratch_shapes=[pltpu.SemaphoreType.DMA((k,)) for k in sems], input_output_aliases=aliases or {},
        compiler_params=pltpu.CompilerParams(has_side_effects=True),
    )(*arrs)


def _other_chips(x, y):
    return [(1 - x, y), (x, 1 - y), (1 - x, 1 - y)]


def _gather_weights(arrs):
    n = len(arrs)

    def body(*refs):
        ins, outs = refs[:n], refs[n:2 * n]
        ici_send, ici_recv, d2d_send, d2d_recv = refs[2 * n:]
        x, y, c = _position()
        me = 2 * x + y
        chips = _other_chips(x, y)
        sends = []
        for a in range(n):
            for j, (px, py) in enumerate(chips):
                sends.append(_remote(ins[a].at[c], outs[a].at[c, me], ici_send.at[3 * a + j], ici_recv.at[3 * a + j],
                                     (px, py, c)))
                sends[-1].start()
        for a in range(n):
            for j, (px, py) in enumerate(chips):
                slab = outs[a].at[c, 2 * px + py]
                _remote(ins[a].at[c], slab, ici_send.at[3 * a + j], ici_recv.at[3 * a + j], (px, py, c)).wait_recv()
                sends.append(_remote(slab, slab, d2d_send.at[3 * a + j], d2d_recv.at[3 * a + j], (x, y, 1 - c)))
                sends[-1].start()
        for a in range(n):
            for j, (px, py) in enumerate(chips):
                slab = outs[a].at[1 - c, 2 * px + py]
                _remote(slab, slab, d2d_send.at[3 * a + j], d2d_recv.at[3 * a + j], (x, y, 1 - c)).wait_recv()
        for cp in sends:
            cp.wait_send()

    return _hbm_call("gather_weights", body, arrs,
                     [jax.ShapeDtypeStruct((N_LAYER, N_CHIP) + a.shape[1:], a.dtype) for a in arrs], [3 * n] * 4)


def _pair_exchange(arrs):
    n = len(arrs)

    def body(*refs):
        ins, outs = refs[:n], refs[n:2 * n]
        send_sems, recv_sems = refs[2 * n:]
        x, y, c = _position()
        cps = [_remote(ins[a].at[1 - c], outs[a], send_sems.at[a], recv_sems.at[a], (x, y, 1 - c)) for a in range(n)]
        for cp in cps:
            cp.start()
        for cp in cps:
            cp.wait_recv()
        for cp in cps:
            cp.wait_send()

    return _hbm_call("pair_exchange", body, arrs, [jax.ShapeDtypeStruct(a.shape[1:], a.dtype) for a in arrs], [n, n])


def _chip_scatter(arrs):
    n = len(arrs)

    def body(*refs):
        ins, outs = refs[:n], refs[n:2 * n]
        send_sems, recv_sems = refs[2 * n:]
        x, y, c = _position()
        me = 2 * x + y
        sends = []
        for a in range(n):
            for j, (px, py) in enumerate(_other_chips(x, y)):
                sends.append(_remote(ins[a].at[2 * px + py], outs[a].at[me], send_sems.at[3 * a + j],
                                     recv_sems.at[3 * a + j], (px, py, c)))
                sends[-1].start()
        for a in range(n):
            for j, (px, py) in enumerate(_other_chips(x, y)):
                _remote(ins[a].at[me], outs[a].at[2 * px + py], send_sems.at[3 * a + j], recv_sems.at[3 * a + j],
                        (px, py, c)).wait_recv()
        for cp in sends:
            cp.wait_send()

    return _hbm_call("chip_scatter", body, arrs, [jax.ShapeDtypeStruct(a.shape, a.dtype) for a in arrs],
                     [3 * n, 3 * n])


def _pair_share(arrs):
    n = len(arrs)

    def body(*refs):
        ins, outs = refs[:n], refs[n:2 * n]
        send_sems, recv_sems = refs[2 * n:]
        x, y, c = _position()
        cps = [_remote(ins[a].at[c], outs[a].at[c], send_sems.at[a], recv_sems.at[a], (x, y, 1 - c)) for a in range(n)]
        for cp in cps:
            cp.start()
        for a in range(n):
            _remote(ins[a].at[c], outs[a].at[1 - c], send_sems.at[a], recv_sems.at[a], (x, y, 1 - c)).wait_recv()
        for cp in cps:
            cp.wait_send()

    return _hbm_call("pair_share", body, arrs, [jax.ShapeDtypeStruct(a.shape, a.dtype) for a in arrs], [n, n],
                     {a: a for a in range(n)})


def _pair_forward(arrs):
    n = len(arrs)

    def body(*refs):
        ins, outs = refs[:n], refs[n:2 * n]
        send_sems, recv_sems = refs[2 * n:]
        x, y, c = _position()
        sends = []
        for a in range(n):
            for j, (px, py) in enumerate(_other_chips(x, y)):
                sends.append(_remote(ins[a].at[c, 2 * px + py], outs[a].at[c, 2 * px + py], send_sems.at[3 * a + j],
                                     recv_sems.at[3 * a + j], (x, y, 1 - c)))
                sends[-1].start()
        for a in range(n):
            for j, (px, py) in enumerate(_other_chips(x, y)):
                slab = outs[a].at[1 - c, 2 * px + py]
                _remote(slab, slab, send_sems.at[3 * a + j], recv_sems.at[3 * a + j], (x, y, 1 - c)).wait_recv()
        for cp in sends:
            cp.wait_send()

    return _hbm_call("pair_forward", body, arrs, [jax.ShapeDtypeStruct(a.shape, a.dtype) for a in arrs],
                     [3 * n, 3 * n], {a: a for a in range(n)})


_HBM = pl.BlockSpec(memory_space=pltpu.HBM)
_SEM = pl.BlockSpec(memory_space=pltpu.SEMAPHORE)
_EFFECT = pltpu.SideEffectType.DATAFLOW_SIDE_EFFECTING


def _chip_copies(kind, srcs, lands, send_sems, recv_sems):
    x, y, c = _position()
    me = 2 * x + y
    sends, recvs = [], []
    for a in range(len(srcs)):
        for j, (px, py) in enumerate(_other_chips(x, y)):
            s, sems, dev = 2 * px + py, (send_sems.at[3 * a + j], recv_sems.at[3 * a + j]), (px, py, c)
            if kind == "gather":
                sends.append(_remote(srcs[a].at[c], lands[a].at[c, me], *sems, dev))
                recvs.append(_remote(srcs[a].at[c], lands[a].at[c, s], *sems, dev))
            else:
                sends.append(_remote(srcs[a].at[s], lands[a].at[me], *sems, dev))
                recvs.append(_remote(srcs[a].at[me], lands[a].at[s], *sems, dev))
    return sends, recvs


def _split_start(name, kind, srcs, land_shapes, after):
    n = len(srcs)

    def body(*refs):
        sends, _ = _chip_copies(kind, refs[:n], refs[n:2 * n], refs[2 * n + 1], refs[2 * n + 2])
        for cp in sends:
            cp.start()
        refs[-1][...] = jnp.zeros_like(refs[-1])

    hbm = lambda a: pltpu.with_memory_space_constraint(a, pltpu.HBM)
    lands = [lax.empty(s.shape, s.dtype) for s in land_shapes]
    outs = pl.pallas_call(
        body, name=name, in_specs=[_HBM] * (2 * n) + [_ANY],
        out_specs=[_SEM, _SEM] + [_HBM] * (2 * n) + [pl.BlockSpec(memory_space=pltpu.VMEM)],
        out_shape=[pltpu.SemaphoreType.DMA((3 * n,)), pltpu.SemaphoreType.DMA((3 * n,))]
        + [pltpu.HBM(a.shape, a.dtype) for a in list(srcs) + lands] + [jax.ShapeDtypeStruct((8, LANES), F32)],
        input_output_aliases={i: 2 + i for i in range(2 * n)},
        compiler_params=pltpu.CompilerParams(has_side_effects=_EFFECT),
    )(*[hbm(a) for a in srcs], *[hbm(a) for a in lands], after)
    return outs[0], outs[1], list(outs[2:2 + 2 * n]), outs[-1]


def _split_wait(name, kind, started, after):
    send_sems, recv_sems, thru, _ = started
    n = len(thru) // 2

    def body(*refs):
        sends, recvs = _chip_copies(kind, refs[:n], refs[n:2 * n], refs[2 * n], refs[2 * n + 1])
        for cp in sends:
            cp.wait_send()
        for cp in recvs:
            cp.wait_recv()

    outs = pl.pallas_call(
        body, name=name, in_specs=[_HBM] * (2 * n) + [_SEM, _SEM, _ANY], out_specs=[_HBM] * (2 * n),
        out_shape=[pltpu.HBM(a.shape, a.dtype) for a in thru], input_output_aliases={i: i for i in range(2 * n)},
        compiler_params=pltpu.CompilerParams(has_side_effects=_EFFECT),
    )(*thru, send_sems, recv_sems, after)
    return list(outs[:n]), list(outs[n:])


def _allreduce_small(g):
    def body(g_ref, o_ref, buf, send_sems, recv_sems):
        x, y, c = _position()
        me = 4 * x + 2 * y + c
        buf[me] = g_ref[...]
        peers = []
        for j in range(1, N_DEV):
            px = 1 - x if j & 4 else x
            py = 1 - y if j & 2 else y
            pc = 1 - c if j & 1 else c
            peers.append((px, py, pc))
        sends = [_remote(g_ref, buf.at[me], send_sems.at[j], recv_sems.at[j], p) for j, p in enumerate(peers)]
        for cp in sends:
            cp.start()
        for j, (px, py, pc) in enumerate(peers):
            _remote(g_ref, buf.at[4 * px + 2 * py + pc], send_sems.at[j], recv_sems.at[j], (px, py, pc)).wait_recv()
        for cp in sends:
            cp.wait_send()
        acc = buf[0]
        for s in range(1, N_DEV):
            acc = acc + buf[s]
        o_ref[...] = acc

    vmem = pl.BlockSpec(memory_space=pltpu.VMEM)
    return pl.pallas_call(
        body, name="allreduce_small", in_specs=[vmem], out_specs=vmem, out_shape=jax.ShapeDtypeStruct(g.shape, F32),
        scratch_shapes=[pltpu.VMEM((N_DEV,) + g.shape, F32), pltpu.SemaphoreType.DMA((N_DEV - 1,)),
                        pltpu.SemaphoreType.DMA((N_DEV - 1,))],
        compiler_params=_params(),
    )(g)


EW_ROWS = 512


def _ew(name, fn, ins, n_out, out_dtype=F32, out_core_slot=False):
    def dims(a):
        return a[0].shape[1:] if isinstance(a, tuple) else a.shape

    R, w = dims(ins[0])
    tr = EW_ROWS if R % EW_ROWS == 0 else R

    def body(c_ref, *refs):
        outs = fn(*[r[...] for r in refs[:len(ins)]])
        for r, v in zip(refs[len(ins):], outs):
            r[...] = v.astype(r.dtype)

    def lead_spec(l):
        if l == "c":
            return pl.BlockSpec((None, tr, w), lambda i, c_ref: (c_ref[0], i, 0))
        return pl.BlockSpec((None, tr, w), lambda i, c_ref, s=l: (s, i, 0))

    plain = pl.BlockSpec((tr, w), lambda i, c_ref: (i, 0))
    in_specs = [lead_spec(a[1]) if isinstance(a, tuple) else plain for a in ins]
    out_spec = lead_spec("c") if out_core_slot else plain
    out_shape = jax.ShapeDtypeStruct((2, R, w) if out_core_slot else (R, w), out_dtype)
    return pl.pallas_call(
        body, name=name,
        grid_spec=pltpu.PrefetchScalarGridSpec(num_scalar_prefetch=1, grid=(R // tr,), in_specs=in_specs,
                                               out_specs=[out_spec] * n_out),
        out_shape=[out_shape] * n_out, compiler_params=_params(dimension_semantics=("arbitrary",)),
    )(lax.axis_index("c").astype(jnp.int32).reshape(1), *[a[0] if isinstance(a, tuple) else a for a in ins])


def _adamw_fn(w, g, m, v):
    m = ADAM_B1 * m + (1.0 - ADAM_B1) * g
    v = ADAM_B2 * v + (1.0 - ADAM_B2) * (g * g)
    m_hat = m / (1.0 - ADAM_B1 ** ADAM_STEP)
    v_hat = v / (1.0 - ADAM_B2 ** ADAM_STEP)
    delta = -ADAM_LR * (m_hat / (jnp.sqrt(v_hat) + ADAM_EPS) + ADAM_WD * w)
    return delta, m, v


def _adamw(name, w, g, m, v):
    shape = w.shape
    two = lambda a: a.reshape(-1, shape[-1])
    return [o.reshape(shape) for o in _ew(name, _adamw_fn, [two(w), two(g), two(m), two(v)], 3)]


def _adamw_rows(name, w, g, m, v):
    L, R, k = w.shape

    def body(w_ref, g_ref, m_ref, v_ref, d_out, m_out, v_out, g_out):
        g_blk = g_ref[...]
        d_out[...], m_out[...], v_out[...] = _adamw_fn(w_ref[...], g_blk, m_ref[...], v_ref[...])
        g_out[...] = g_blk

    spec = pl.BlockSpec((None, EW_ROWS, k), lambda l, i: (l, i, 0))
    return pl.pallas_call(
        body, name=name, grid=(L, -(-R // EW_ROWS)), in_specs=[spec] * 4, out_specs=[spec] * 4,
        out_shape=[jax.ShapeDtypeStruct((L, R, k), F32)] * 4,
        compiler_params=_params(dimension_semantics=("arbitrary", "arbitrary")),
    )(w, g, m, v)


_SMALL = [("norm_pre", (2, 1024)), ("norm_post", (2, 1024)), ("norm_mem", (2, 1024)), ("a_log", (2, 4)),
          ("dt_bias", (2, 4)), ("dn_norm", (2, 128)), ("gm_norm", (2, 512)), ("spatial_w", (2, 4, 128, 128)),
          ("spatial_b", (2, 4, 128)), ("sinks", (2, 8))]
_SMALL_ROWS = 200
_BIG = ["w_in", "conv_w", "w_mem_kv", "w_up", "w_out"]
_NAMES = ["norm_pre", "norm_post", "norm_mem", "w_in", "conv_w", "a_log", "dt_bias", "dn_norm", "gm_norm",
          "spatial_w", "spatial_b", "sinks", "w_mem_kv", "w_up", "w_out"]


def _size(shape):
    n = 1
    for s in shape:
        n *= s
    return n


_PACK_UNIT = 8 * 1024


def _pack_small(d):
    rows = []
    for n, shp in _SMALL:
        flat = d[n].reshape(-1)
        rows.append(jnp.pad(flat, (0, -flat.shape[0] % _PACK_UNIT)).reshape(-1, 1024))
    assert sum(r.shape[0] for r in rows) == _SMALL_ROWS
    return jnp.concatenate(rows, axis=0)


def _unpack_small(p):
    out, off = {}, 0
    for n, shp in _SMALL:
        k = -(-_size(shp) // _PACK_UNIT) * 8
        out[n] = p[off:off + k].reshape(-1)[:_size(shp)].reshape(shp)
        off += k
    return out


_HALF_SHAPE = {"w_in": (SHARD_PAD // 2, D_MODEL), "conv_w": (2, 3 * BRANCH_W // N_CHIP), "w_mem_kv": (128, D_MODEL),
               "w_up": (2, BRANCH_W, D_MODEL // N_CHIP), "w_out": (128, D_MODEL)}


def _chip_major(g):
    g = jnp.swapaxes(g, 0, 1)
    return g.reshape((N_CHIP, 2 * g.shape[2]) + g.shape[3:])


def _half_major(g):
    g = g.reshape((N_CHIP, 2, g.shape[1] // 2) + g.shape[2:])
    return jnp.swapaxes(g, 0, 1).astype(BF16)


def _weight_views(l, gathered, small):
    g_in, g_conv, g_kv, g_up, g_out = gathered
    return _layer_weights(
        l, _w_pad_from_slabs(g_in), _chip_major(g_conv).transpose(1, 0, 2).reshape(4, 3 * BRANCH_W),
        _chip_major(g_kv).reshape(D_MODEL, D_MODEL),
        _chip_major(g_up).transpose(1, 2, 0, 3).reshape(4, BRANCH_W, D_MODEL),
        _chip_major(g_out).reshape(D_MODEL, D_MODEL), small)


def _pair_sums(g):
    big = [_slabs_from_pad(g["w_pad"]),
           _half_major(g["conv"].reshape(4, N_CHIP, 3 * BRANCH_W // N_CHIP).transpose(1, 0, 2)),
           _half_major(g["w_mem_kv"].reshape(N_CHIP, D_MODEL // N_CHIP, D_MODEL)),
           _half_major(g["w_up"].reshape(4, BRANCH_W, N_CHIP, D_MODEL // N_CHIP).transpose(2, 0, 1, 3)),
           _half_major(g["w_out"].reshape(N_CHIP, D_MODEL // N_CHIP, D_MODEL))]
    add2 = lambda a, b: [a.astype(F32) + b.astype(F32)]
    pair = []
    for n, b, p in zip(_BIG, big, _pair_exchange(big)):
        k = b.shape[-1]
        pair.append(_ew("pair_sum_" + n, add2, [(b.reshape(2, -1, k), "c"), p.reshape(-1, k)], 1, BF16)[0]
                    .reshape(p.shape))
    return pair


def _chip_sums(landed, pair, me):
    add4 = lambda a, b, c_, d: [((a.astype(F32) + b.astype(F32)) + c_.astype(F32)) + d.astype(F32)]
    totals = []
    for n, r, q in zip(_BIG, landed, pair):
        r = _own_slot(r, lax.dynamic_index_in_dim(q, me, 0), me, 0)
        k = r.shape[-1]
        totals.append(_ew("chip_sum_" + n, add4, [(r.reshape(N_CHIP, -1, k), s) for s in range(N_CHIP)], 1,
                          out_core_slot=True)[0].reshape((2,) + r.shape[1:]))
    return totals


def _own_slot(buf, mine, me, axis):
    return lax.dynamic_update_index_in_dim(buf, mine.astype(buf.dtype), me, axis)


def kernel(x, mem, norm_pre, norm_post, norm_mem, w_in, conv_w, a_log, dt_bias, dn_norm, gm_norm, spatial_w, spatial_b, sinks, w_mem_kv, w_up, w_out, loss_target, m_norm_pre, m_norm_post, m_norm_mem, m_w_in, m_conv_w, m_a_log, m_dt_bias, m_dn_norm, m_gm_norm, m_spatial_w, m_spatial_b, m_sinks, m_w_mem_kv, m_w_up, m_w_out, v_norm_pre, v_norm_post, v_norm_mem, v_w_in, v_conv_w, v_a_log, v_dt_bias, v_dn_norm, v_gm_norm, v_spatial_w, v_spatial_b, v_sinks, v_w_mem_kv, v_w_up, v_w_out):
    w = dict(norm_pre=norm_pre, norm_post=norm_post, norm_mem=norm_mem, w_in=w_in, conv_w=conv_w, a_log=a_log,
             dt_bias=dt_bias, dn_norm=dn_norm, gm_norm=gm_norm, spatial_w=spatial_w, spatial_b=spatial_b, sinks=sinks,
             w_mem_kv=w_mem_kv, w_up=w_up, w_out=w_out)
    m = dict(norm_pre=m_norm_pre, norm_post=m_norm_post, norm_mem=m_norm_mem, w_in=m_w_in, conv_w=m_conv_w,
             a_log=m_a_log, dt_bias=m_dt_bias, dn_norm=m_dn_norm, gm_norm=m_gm_norm, spatial_w=m_spatial_w,
             spatial_b=m_spatial_b, sinks=m_sinks, w_mem_kv=m_w_mem_kv, w_up=m_w_up, w_out=m_w_out)
    v = dict(norm_pre=v_norm_pre, norm_post=v_norm_post, norm_mem=v_norm_mem, w_in=v_w_in, conv_w=v_conv_w,
             a_log=v_a_log, dt_bias=v_dt_bias, dn_norm=v_dn_norm, gm_norm=v_gm_norm, spatial_w=v_spatial_w,
             spatial_b=v_spatial_b, sinks=v_sinks, w_mem_kv=v_w_mem_kv, w_up=v_w_up, w_out=v_w_out)
    me = 2 * lax.axis_index("x") + lax.axis_index("y")

    w_in_t = jnp.pad(w_in.astype(BF16).transpose(0, 2, 1), ((0, 0), (0, SHARD_PAD - SHARD_IN), (0, 0)))
    local = dict(w_in=w_in_t, conv_w=conv_w, w_mem_kv=w_mem_kv.astype(BF16), w_up=w_up.astype(BF16),
                 w_out=w_out.astype(BF16))
    halves = lambda l: [local[n][l].reshape((2,) + _HALF_SHAPE[n]) for n in _BIG]
    own = lambda gathered, mine: [_own_slot(g, h[:, None], me, 1) for g, h in zip(gathered, mine)]
    g0 = own(_gather_weights(halves(0)), halves(0))
    started = _split_start("gather_l1_start", "gather", halves(1),
                           [jax.ShapeDtypeStruct((2, N_CHIP) + _HALF_SHAPE[n], local[n].dtype) for n in _BIG], g0[1])

    xl, meml = x[0], mem[0]
    W0 = _weight_views(0, g0, w)
    W0["norm_pre"] = W0["norm_pre"] + started[3][0, 0]
    x1, sv0 = _layer_fwd(xl, meml, W0)
    mine1, landed1 = _split_wait("gather_l1_wait", "gather", started, x1)
    W1 = _weight_views(1, own(_pair_forward(landed1), mine1), w)
    x2, sv1 = _layer_fwd(x1, meml, W1)
    dy, lrows = _rows_fwd("loss", _loss_fn, [Row(x2, D_MODEL, 0), Row(loss_target[0], D_MODEL, 0)], [],
                          [(D_MODEL, F32), (LANES, F32)], TR)
    loss = lax.psum(jnp.sum(lrows[:, 0]), ("x", "y", "c"))

    dx1, grads1 = _layer_bwd(dy, meml, W1, sv1)
    pair1 = _pair_sums(grads1)
    scattering = _split_start("scatter_l1_start", "scatter", pair1,
                              [jax.ShapeDtypeStruct(p.shape, p.dtype) for p in pair1], pair1[1])
    W0["norm_post"] = W0["norm_post"] + scattering[3][0, 0]
    dx, grads0 = _layer_bwd(dx1, meml, W0, sv0)
    pair1, landed1 = _split_wait("scatter_l1_wait", "scatter", scattering, dx)
    pair0 = _pair_sums(grads0)
    totals = _pair_share(_chip_sums(_chip_scatter(pair0), pair0, me) + _chip_sums(landed1, pair1, me))
    shard = lambda t, n: t.reshape(local[n].shape[1:])
    gbig = {n: jnp.stack([shard(totals[i], n), shard(totals[len(_BIG) + i], n)]) for i, n in enumerate(_BIG)}
    grads = [grads0, grads1]

    small_local = dict(
        norm_pre=jnp.stack([g["norm_pre"][0] for g in grads]), norm_post=jnp.stack([g["norm_post"][0] for g in grads]),
        norm_mem=jnp.stack([g["norm_mem"][0] for g in grads]), a_log=jnp.stack([g["a_vec"][0, 4:8] for g in grads]),
        dt_bias=jnp.stack([g["dt_vec"][0, 4:8] for g in grads]), dn_norm=jnp.stack([g["dn_norm"][0] for g in grads]),
        gm_norm=jnp.stack([g["gm_norm"][0] for g in grads]), spatial_w=jnp.stack([g["spatial_w"] for g in grads]),
        spatial_b=jnp.stack([g["spatial_b"][:, 0, :] for g in grads]),
        sinks=jnp.stack([g["sink_vec"][0, :8] for g in grads]))
    gsmall_packed = _allreduce_small(_pack_small(small_local))

    d_s, m_s, v_s = _ew("adamw_small", _adamw_fn, [_pack_small(w), gsmall_packed, _pack_small(m), _pack_small(v)], 3)
    gsmall, dsmall, msmall, vsmall = (_unpack_small(p) for p in (gsmall_packed, d_s, m_s, v_s))
    g_o, d_o, m_o, v_o = dict(gsmall), dict(dsmall), dict(msmall), dict(vsmall)
    for n in _BIG:
        if n == "w_in":
            tr = lambda a: a.transpose(0, 2, 1)
            d_o[n], m_o[n], v_o[n], g_o[n] = [tr(o) for o in _adamw_rows("adamw_" + n, tr(w[n]), gbig[n], tr(m[n]), tr(v[n]))]
        else:
            g_o[n] = gbig[n]
            d_o[n], m_o[n], v_o[n] = _adamw("adamw_" + n, w[n], gbig[n], m[n], v[n])
    return (loss, dx[None], *[g_o[n] for n in _NAMES], *[d_o[n] for n in _NAMES], *[m_o[n] for n in _NAMES],
            *[v_o[n] for n in _NAMES])
```

```python
import collections
import functools

import jax
import jax.numpy as jnp
from jax import lax
from jax.experimental import pallas as pl
from jax.experimental.pallas import tpu as pltpu

F32 = jnp.float32
BF16 = jnp.bfloat16

D_MODEL = 1024
BRANCH_W = 512
MEM_LEN = 256
N_LAYER = 2
N_CHIP = 4
N_DEV = 8
EPS = 1e-6
NEG_INF = -1e30
DN_CHUNK = 64
LANES = 128
VMEM_LIMIT = 48 * 1024 * 1024

ADAM_LR, ADAM_B1, ADAM_B2, ADAM_EPS, ADAM_WD, ADAM_STEP = 0.001, 0.9, 0.999, 1e-08, 0.01, 10

N_PAD = 10240
O_GATE = 0
O_AQ, O_AK, O_AV, O_AZ = 4096, 4608, 5120, 5632
O_BUV, O_BZ = 6144, 7168
O_CKV, O_BA = 7680, 7936
O_CQ, O_CZ = 8192, 8704
O_MQ, O_MZ = 9216, 9728
O_MISC, W_MISC = O_CKV, 512
_PAD_SEGS = [(5896, 4096), (0, 512), (512, 512), (1024, 512), (1536, 512), (2056, 1024), (3080, 512),
             (4104, 128), (4232, 128), (2048, 8), (None, 120), (None, 128),
             (3592, 512), (4360, 512), (4872, 512), (5384, 512)]
D_IN = 9992
SHARD_IN = D_IN // N_CHIP


SHARD_PAD = 2560


def _pad_parts():
    parts, off = [], 0
    for s, n in _PAD_SEGS:
        a = s
        while s is not None and a < s + n:
            chip = a // SHARD_IN
            b = min(s + n, (chip + 1) * SHARD_IN)
            parts.append((chip, a - chip * SHARD_IN, off + a - s, b - a))
            a = b
        off += n
    return parts


PERM_ROWS = 512
PERM_SLACK = 32


def _permute_rows(name, src, parts, n_out, out_dtype):
    B, Z = PERM_ROWS, PERM_ROWS + PERM_SLACK
    w = src.shape[1]
    plans = []
    for blk in range(n_out // B):
        o, runs = blk * B, []
        for s, d, n in parts:
            lo, hi = max(d, o), min(d + n, o + B)
            if lo < hi:
                s0 = s + lo - d
                wa = s0 // 16 * 16
                wb = min(-(-(s0 + hi - lo) // 16) * 16, src.shape[0])
                runs.append((wa, wb - wa, s0 - (lo - o) - wa, lo - o, hi - o))
        plans.append(runs)
    max_runs = max(len(r) for r in plans)
    nblk = len(plans)

    def body(*refs):
        src_ref, out_ref, inbuf, obuf, insem, outsem = (refs[0],) + refs[-5:]

        def in_copies(blk):
            return [pltpu.make_async_copy(src_ref.at[pl.ds(wa, ws)], inbuf.at[blk % 2, r, pl.ds(0, ws)],
                                          insem.at[blk % 2, r]) for r, (wa, ws, _, _, _) in enumerate(plans[blk])]

        def out_copy(blk):
            return pltpu.make_async_copy(obuf.at[blk % 2], out_ref.at[pl.ds(blk * B, B)], outsem.at[blk % 2])

        for cp in in_copies(0):
            cp.start()
        rid = _iota((B, 1), 0)
        for blk in range(nblk):
            if blk + 1 < nblk:
                for cp in in_copies(blk + 1):
                    cp.start()
            for cp in in_copies(blk):
                cp.wait()
            val = jnp.zeros((B, w), F32)
            for r, (wa, ws, t, l0, l1) in enumerate(plans[blk]):
                win = jnp.concatenate([inbuf[blk % 2, r, pl.ds(0, ws)].astype(F32), jnp.zeros((Z - ws, w), F32)], axis=0)
                moved = pltpu.roll(win, (-t) % Z, 0)[:B]
                val = jnp.where((rid >= l0) & (rid < l1), moved, val)
            if blk >= 2:
                out_copy(blk - 2).wait()
            obuf[blk % 2] = val.astype(out_dtype)
            out_copy(blk).start()
        for blk in range(max(nblk - 2, 0), nblk):
            out_copy(blk).wait()

    return pl.pallas_call(
        body, name=name, in_specs=[_ANY], out_specs=_ANY, out_shape=jax.ShapeDtypeStruct((n_out, w), out_dtype),
        scratch_shapes=[pltpu.VMEM((2, max_runs, Z, w), src.dtype), pltpu.VMEM((2, B, w), out_dtype),
                        pltpu.SemaphoreType.DMA((2, max_runs)), pltpu.SemaphoreType.DMA((2,))],
        compiler_params=_params(),
    )(src)


def _slab_parts():
    h, out = SHARD_PAD // 2, []
    for chip, s, d, n in _pad_parts():
        a = s
        while a < s + n:
            half = a // h
            b = min(s + n, (half + 1) * h)
            out.append(((half * N_CHIP + chip) * h + a - half * h, d + a - s, b - a))
            a = b
    return out


def _w_pad_from_slabs(slabs):
    return _permute_rows("w_pad_rows", slabs.reshape(-1, slabs.shape[-1]), _slab_parts(), N_PAD, BF16)


def _slabs_from_pad(dw):
    slabs = _permute_rows("w_pad_grad_rows", dw, [(d, s, n) for s, d, n in _slab_parts()], N_CHIP * SHARD_PAD, BF16)
    return slabs.reshape(2, N_CHIP, SHARD_PAD // 2, dw.shape[1])


def _dot(a, b, dims, prec):
    if prec == "bf16":
        return lax.dot_general(a.astype(BF16), b.astype(BF16), (dims, ((), ())), preferred_element_type=F32)
    return lax.dot_general(a, b, (dims, ((), ())), precision=lax.Precision.HIGHEST, preferred_element_type=F32)


_NN, _NT, _TN = ((1,), (0,)), ((1,), (1,)), ((0,), (0,))


def _make_mm(prec):
    @jax.custom_vjp
    def nn(a, b):
        return _dot(a, b, _NN, prec)

    @jax.custom_vjp
    def nt(a, b):
        return _dot(a, b, _NT, prec)

    @jax.custom_vjp
    def tn(a, b):
        return _dot(a, b, _TN, prec)

    nn.defvjp(lambda a, b: (nn(a, b), (a, b)), lambda r, g: (nt(g, r[1]), tn(r[0], g)))
    nt.defvjp(lambda a, b: (nt(a, b), (a, b)), lambda r, g: (nn(g, r[1]), tn(g, r[0])))
    tn.defvjp(lambda a, b: (tn(a, b), (a, b)), lambda r, g: (nt(r[1], g), nn(r[0], g)))
    return nn, nt, tn


_nn16, _nt16, _tn16 = _make_mm("bf16")
_nn32, _nt32, _tn32 = _make_mm("f32")


def _make_slice(axis):
    @functools.partial(jax.custom_vjp, nondiff_argnums=(1, 2, 3))
    def sl(x, a, b, n):
        return x[a:b] if axis == 0 else x[:, a:b]

    def fwd(x, a, b, n):
        return sl(x, a, b, n), None

    def bwd(a, b, n, _, g):
        parts = []
        if a > 0:
            parts.append(jnp.zeros((a, g.shape[1]) if axis == 0 else (g.shape[0], a), g.dtype))
        parts.append(g)
        if n - b > 0:
            parts.append(jnp.zeros((n - b, g.shape[1]) if axis == 0 else (g.shape[0], n - b), g.dtype))
        return (jnp.concatenate(parts, axis=axis),)

    sl.defvjp(fwd, bwd)
    return sl


_sl0, _sl1 = _make_slice(0), _make_slice(1)


def _rowsl(x, a, b):
    return _sl0(x, a, b, x.shape[0])


def _cols(x, a, b):
    return _sl1(x, a, b, x.shape[1])


@functools.partial(jax.custom_vjp, nondiff_argnums=(1,))
def _rollr(x, s):
    return pltpu.roll(x, s, 0)


_rollr.defvjp(lambda x, s: (_rollr(x, s), None),
              lambda s, _, g: (pltpu.roll(g, g.shape[0] - s, 0),))


def _iota(shape, axis):
    return lax.broadcasted_iota(jnp.int32, shape, axis)


def _sigmoid(x):
    return lax.logistic(x)


def _silu(x):
    return x * _sigmoid(x)


def _gelu(x):
    return 0.5 * x * (1.0 + jnp.tanh(0.7978845608028654 * (x + 0.044715 * (x * x * x))))


def _softplus(x):
    return jnp.maximum(x, 0.0) + jnp.log(1.0 + jnp.exp(-jnp.abs(x)))


def _rms(x, g):
    return x * lax.rsqrt(jnp.mean(x * x, axis=-1, keepdims=True) + EPS) * g


def _lane_pick(x, lane):
    return jnp.sum(x * (_iota((1, x.shape[1]), 1) == lane).astype(F32), axis=1, keepdims=True)


Row = collections.namedtuple("Row", "arr w cb hb grad", defaults=(0, True))


def _full_spec(shape):
    return pl.BlockSpec(shape, lambda i, _n=len(shape): (0,) * _n)


def _load_params(refs):
    return [[p[g].astype(F32) for g in range(p.shape[0])] if len(p.shape) == 3 else p[...].astype(F32)
            for p in refs]


def _params(**kw):
    return pltpu.CompilerParams(vmem_limit_bytes=VMEM_LIMIT, **kw)


def _rows_fwd(name, fn, rows, params, outs, tr, carry=None):
    T = rows[0].arr.shape[0]
    n = T // tr
    halos = [r for r in rows if r.hb]
    nr, nh, npar, no = len(rows), len(halos), len(params), len(outs)

    def body(*refs):
        row_refs, halo_refs = refs[:nr], refs[nr:nr + nh]
        par_refs = refs[nr + nh:nr + nh + npar]
        out_refs = refs[nr + nh + npar:nr + nh + npar + no]
        rest = refs[nr + nh + npar + no:]
        first = pl.program_id(0) == 0
        cvals = None
        if carry is not None:
            csave_ref, carry_ref = rest

            @pl.when(first)
            def _():
                carry_ref[...] = jnp.zeros_like(carry_ref)

            cvals = [carry_ref[g] for g in range(carry[0])]
            for g in range(carry[0]):
                csave_ref[0, g] = cvals[g]
        c_out, o = fn(first, cvals, [r[...].astype(F32) for r in row_refs],
                      [h[...].astype(F32) for h in halo_refs], _load_params(par_refs))
        for r, v in zip(out_refs, o):
            r[...] = v.astype(r.dtype)
        if carry is not None:
            for g in range(carry[0]):
                carry_ref[g] = c_out[g]

    in_specs = [pl.BlockSpec((tr, r.w), lambda i, c=r.cb: (i, c)) for r in rows]
    in_specs += [pl.BlockSpec((r.hb, r.w), lambda i, c=r.cb, q=tr // r.hb: (jnp.maximum(i * q - 1, 0), c))
                 for r in halos]
    in_specs += [_full_spec(p.shape) for p in params]
    out_shape = [jax.ShapeDtypeStruct((T, w), dt) for w, dt in outs]
    out_specs = [pl.BlockSpec((tr, w), lambda i: (i, 0)) for w, _ in outs]
    scratch = []
    if carry is not None:
        out_shape.append(jax.ShapeDtypeStruct((n,) + carry, F32))
        out_specs.append(pl.BlockSpec((1,) + carry, lambda i: (i, 0, 0, 0)))
        scratch.append(pltpu.VMEM(carry, F32))
    return pl.pallas_call(
        body, name=name, grid=(n,), in_specs=in_specs, out_specs=out_specs, out_shape=out_shape,
        scratch_shapes=scratch, compiler_params=_params(dimension_semantics=("arbitrary",)),
    )(*[r.arr for r in rows], *[r.arr for r in halos], *params)


def _rows_bwd(name, fn, rows, params, douts, tr, carry=None, csave=None, dcols=None):
    T = rows[0].arr.shape[0]
    n = T // tr
    halos = [r for r in rows if r.hb]
    grows = [r for r in rows if r.grad is True]
    crows = [r for r in rows if r.grad == "cols"]
    wcols = sum(r.w for r in crows)
    nr, nh, npar, nd, ng = len(rows), len(halos), len(params), len(douts), len(grows)
    nc = 0 if carry is None else 1
    ncol = 1 if crows else 0
    nalias = 1 if (crows and dcols is not None) else 0

    def body(*refs):
        row_refs, halo_refs = refs[:nr], refs[nr:nr + nh]
        par_refs = refs[nr + nh:nr + nh + npar]
        k = nr + nh + npar
        csave_ref = refs[k] if nc else None
        dout_refs = refs[k + nc:k + nc + nd]
        k = k + nc + nd + nalias
        drow_refs = refs[k:k + ng]
        dcols_ref = refs[k + ng] if ncol else None
        dpar_refs = refs[k + ng + ncol:k + ng + ncol + npar]
        k = k + ng + ncol + npar
        dcarry_ref = refs[k] if nc else None
        hgrad_refs = refs[k + nc:]
        i = pl.program_id(0)
        first_tile = i == n - 1

        @pl.when(i == 0)
        def _():
            for r in dpar_refs:
                r[...] = jnp.zeros_like(r)
            for r in hgrad_refs:
                r[...] = jnp.zeros_like(r)
            if nc:
                dcarry_ref[...] = jnp.zeros_like(dcarry_ref)

        rv = [r[...].astype(F32) for r in row_refs]
        hv = [h[...].astype(F32) for h in halo_refs]
        pv = _load_params(par_refs)
        dov = [d[...].astype(F32) for d in dout_refs]
        if nc:
            cv = [csave_ref[0, g] for g in range(carry[0])]
            _, vjp = jax.vjp(lambda c, r, h, p: fn(first_tile, c, r, h, p), cv, rv, hv, pv)
            dc, dr, dh, dp = vjp(([dcarry_ref[g] for g in range(carry[0])], dov))
            for g in range(carry[0]):
                dcarry_ref[g] = dc[g]
        else:
            _, vjp = jax.vjp(lambda r, h, p: fn(first_tile, None, r, h, p)[1], rv, hv, pv)
            dr, dh, dp = vjp(dov)
        gi = hi = 0
        pieces = []
        for kk, r in enumerate(rows):
            d = dr[kk]
            if r.hb:
                carried = hgrad_refs[hi][...]
                d = d + (carried if tr == r.hb else
                         jnp.concatenate([jnp.zeros((tr - r.hb, r.w), F32), carried], axis=0))
                hgrad_refs[hi][...] = dh[hi]
                hi += 1
            if r.grad is True:
                drow_refs[gi][...] = d.astype(drow_refs[gi].dtype)
                gi += 1
            elif r.grad == "cols":
                pieces.append(d.astype(BF16))
        if ncol:
            dcols_ref[...] = pieces[0] if len(pieces) == 1 else jnp.concatenate(pieces, axis=1)
        for r, d in zip(dpar_refs, dp):
            if len(r.shape) == 3:
                for g in range(r.shape[0]):
                    r[g] += d[g]
            else:
                r[...] += d

    rev = lambda i: n - 1 - i
    in_specs = [pl.BlockSpec((tr, r.w), lambda i, c=r.cb: (rev(i), c)) for r in rows]
    in_specs += [pl.BlockSpec((r.hb, r.w), lambda i, c=r.cb, q=tr // r.hb: (jnp.maximum(rev(i) * q - 1, 0), c))
                 for r in halos]
    in_specs += [_full_spec(p.shape) for p in params]
    args = [r.arr for r in rows] + [r.arr for r in halos] + list(params)
    scratch = []
    if nc:
        in_specs.append(pl.BlockSpec((1,) + carry, lambda i: (rev(i), 0, 0, 0)))
        args.append(csave)
        scratch.append(pltpu.VMEM(carry, F32))
    in_specs += [pl.BlockSpec((tr, d.shape[1]), lambda i: (rev(i), 0)) for d in douts]
    args += list(douts)
    aliases = {}
    if nalias:
        aliases = {len(args): ng}
        in_specs.append(pl.BlockSpec(memory_space=pl.ANY))
        args.append(dcols)
    scratch += [pltpu.VMEM((r.hb, r.w), F32) for r in halos]
    out_shape = [jax.ShapeDtypeStruct((T, r.w), F32) for r in grows]
    out_specs = [pl.BlockSpec((tr, r.w), lambda i: (rev(i), 0)) for r in grows]
    if ncol:
        off = crows[0].cb * crows[0].w
        assert off % wcols == 0 and all(a.cb * a.w + a.w == b.cb * b.w for a, b in zip(crows, crows[1:]))
        out_shape.append(jax.ShapeDtypeStruct((T, N_PAD), BF16))
        out_specs.append(pl.BlockSpec((tr, wcols), lambda i, c=off // wcols: (rev(i), c)))
    out_shape += [jax.ShapeDtypeStruct(p.shape, F32) for p in params]
    out_specs += [_full_spec(p.shape) for p in params]
    res = pl.pallas_call(
        body, name=name, grid=(n,), in_specs=in_specs, out_specs=out_specs, out_shape=out_shape,
        scratch_shapes=scratch, input_output_aliases=aliases,
        compiler_params=_params(dimension_semantics=("arbitrary",)),
    )(*args)
    return list(res[:ng]), list(res[ng + ncol:]), (res[ng] if ncol else dcols)


def _fill_misc(dcols, dkv, dba, tr):
    T = dkv.shape[0]

    def body(kv_ref, ba_ref, _, o_ref):
        o_ref[...] = jnp.concatenate([kv_ref[...], ba_ref[...]], axis=1).astype(BF16)

    return pl.pallas_call(
        body, name="misc_bwd", grid=(T // tr,),
        in_specs=[pl.BlockSpec((tr, 256), lambda i: (i, 0)), pl.BlockSpec((tr, 256), lambda i: (i, 0)),
                  pl.BlockSpec(memory_space=pl.ANY)],
        out_specs=pl.BlockSpec((tr, W_MISC), lambda i: (i, O_MISC // W_MISC)),
        out_shape=jax.ShapeDtypeStruct((T, N_PAD), BF16), input_output_aliases={2: 0},
        compiler_params=_params(dimension_semantics=("arbitrary",)),
    )(dkv, dba, dcols)


def _matmul(name, a, b, kind, out_dtype, tm, tn, tk):
    if kind == "tn":
        (K, M), N = a.shape, b.shape[1]
    else:
        (M, K), N = a.shape, (b.shape[0] if kind == "nt" else b.shape[1])
    tm, tn, tk = min(tm, M), min(tn, N), min(tk, K)
    nk = K // tk
    dims = {"nn": _NN, "nt": _NT, "tn": _TN}[kind]

    def body(a_ref, b_ref, o_ref, acc_ref):
        k = pl.program_id(2)

        @pl.when(k == 0)
        def _():
            acc_ref[...] = jnp.zeros_like(acc_ref)

        acc_ref[...] += lax.dot_general(a_ref[...], b_ref[...], (dims, ((), ())), preferred_element_type=F32)

        @pl.when(k == nk - 1)
        def _():
            o_ref[...] = acc_ref[...].astype(o_ref.dtype)

    a_spec = pl.BlockSpec((tk, tm), lambda i, j, k: (k, i)) if kind == "tn" else pl.BlockSpec((tm, tk), lambda i, j, k: (i, k))
    b_spec = pl.BlockSpec((tn, tk), lambda i, j, k: (j, k)) if kind == "nt" else pl.BlockSpec((tk, tn), lambda i, j, k: (k, j))
    return pl.pallas_call(
        body, name=name, grid=(M // tm, N // tn, nk), in_specs=[a_spec, b_spec],
        out_specs=pl.BlockSpec((tm, tn), lambda i, j, k: (i, j)),
        out_shape=jax.ShapeDtypeStruct((M, N), out_dtype),
        scratch_shapes=[pltpu.VMEM((tm, tn), F32)],
        compiler_params=_params(dimension_semantics=("arbitrary", "arbitrary", "arbitrary")),
    )(a, b)


def _pre_fn(first, _, rows, halos, params):
    return None, [_rms(rows[0], params[0])]


def _pre_fn_res(first, _, rows, halos, params):
    return None, [_rms(rows[0], params[0]), rows[0]]


def _memkv_fn(first, _, rows, halos, params):
    g, w = params
    return None, [_nn16(_rms(rows[0], g), w)]


def _conv_silu(x, halo, w4, keep_halo):
    tr = x.shape[0]
    halo = halo * keep_halo
    rid = _iota((tr, 1), 0)
    acc = w4[3] * x
    for s in (1, 2, 3):
        hs = jnp.concatenate([_rollr(halo, s), jnp.zeros((tr - halo.shape[0], x.shape[1]), F32)], axis=0)
        acc = acc + w4[3 - s] * jnp.where(rid < s, hs, _rollr(x, s))
    return _silu(acc)


def _dn_fn(first, S, rows, halos, params):
    qp, kp, vp, z, ba = rows
    conv, a_vec, dt_vec, dnorm = params
    ba = _cols(ba, 0, LANES)
    tr = qp.shape[0]
    keep = jnp.where(first, 0.0, 1.0)
    q = _conv_silu(qp, halos[0], [conv[3 * j + 0] for j in range(4)], keep)
    k = _conv_silu(kp, halos[1], [conv[3 * j + 1] for j in range(4)], keep)
    v = _conv_silu(vp, halos[2], [conv[3 * j + 2] for j in range(4)], keep)
    qh, kh, vh = [], [], []
    for h in range(4):
        a, b = h * LANES, (h + 1) * LANES
        xq, xk = _cols(q, a, b), _cols(k, a, b)
        qh.append(xq * lax.rsqrt(jnp.sum(xq * xq, axis=1, keepdims=True) + EPS) * (LANES ** -0.5))
        kh.append(xk * lax.rsqrt(jnp.sum(xk * xk, axis=1, keepdims=True) + EPS))
        vh.append(_cols(v, a, b))
    beta_all = _sigmoid(ba)
    g_all = -jnp.exp(a_vec) * _softplus(ba + dt_vec)
    C = DN_CHUNK
    ii, jj = _iota((C, C), 0), _iota((C, C), 1)
    strict, incl = ii > jj, ii >= jj
    eye = (ii == jj).astype(F32)
    last_row = (_iota((C, 1), 0) == C - 1).astype(F32)
    n_chunk = tr // C
    pairs = [(c, h) for c in range(n_chunk) for h in range(4)]
    rows_of = lambda a, c: _rowsl(a, c * C, (c + 1) * C)
    gcs = [_nn32(incl.astype(F32), rows_of(g_all, c)) for c in range(n_chunk)]
    qc = {(c, h): rows_of(qh[h], c) for c, h in pairs}
    kc = {(c, h): rows_of(kh[h], c) for c, h in pairs}
    beta = {(c, h): _lane_pick(rows_of(beta_all, c), h) for c, h in pairs}
    gc = {(c, h): _lane_pick(gcs[c], 4 + h) for c, h in pairs}
    dec = {p: jnp.exp(jnp.where(incl, gc[p] - jnp.sum(eye * gc[p], axis=0, keepdims=True), 0.0)) for p in pairs}
    egc = {p: jnp.exp(gc[p]) for p in pairs}
    kb = {p: kc[p] * beta[p] for p in pairs}
    kq = {p: _nt16(jnp.concatenate([kb[p], qc[p]], axis=0), kc[p]) for p in pairs}
    P = {p: -jnp.where(strict, _rowsl(kq[p], 0, C) * dec[p], 0.0) for p in pairs}
    aqk = {p: jnp.where(incl, _rowsl(kq[p], C, 2 * C) * dec[p], 0.0) for p in pairs}
    tinv = {p: eye + P[p] for p in pairs}
    P = {p: _nn16(P[p], P[p]) for p in pairs}
    for j in range(5):
        if j < 4:
            pt = {p: _nn16(jnp.concatenate([P[p], tinv[p]], axis=0), P[p]) for p in pairs}
            tinv = {p: tinv[p] + _rowsl(pt[p], C, 2 * C) for p in pairs}
            P = {p: _rowsl(pt[p], 0, C) for p in pairs}
        else:
            tinv = {p: tinv[p] + _nn16(tinv[p], P[p]) for p in pairs}
    uw = {(c, h): _nn16(tinv[c, h], jnp.concatenate([rows_of(vh[h], c) * beta[c, h], kb[c, h] * egc[c, h]], axis=1))
          for c, h in pairs}
    S = list(S)
    ychunks = []
    for c in range(n_chunk):
        zc = rows_of(z, c)
        hs = range(4)
        ws = [_nn16(jnp.concatenate([_cols(uw[c, h], LANES, 2 * LANES), qc[c, h] * egc[c, h]], axis=0), S[h]) for h in hs]
        vnew = [_cols(uw[c, h], 0, LANES) - _rowsl(ws[h], 0, C) for h in hs]
        o = [_rowsl(ws[h], C, 2 * C) + _nn16(aqk[c, h], vnew[h]) for h in hs]
        glast = [jnp.sum(gc[c, h] * last_row, axis=0, keepdims=True) for h in hs]
        S = [S[h] * jnp.exp(glast[h]) + _tn16(kc[c, h] * jnp.exp(glast[h] - gc[c, h]), vnew[h]) for h in hs]
        ychunks.append(jnp.concatenate(
            [_rms(o[h], dnorm) * _silu(_cols(zc, h * LANES, (h + 1) * LANES)) for h in hs], axis=1))
    return S, [jnp.concatenate(ychunks, axis=0)]


def _gm_fn(first, _, rows, halos, params):
    uv, z = rows
    gnorm, ws, bs = params
    tr = uv.shape[0]
    guv = _gelu(uv)
    u = _cols(guv, 0, BRANCH_W)
    v = _rms(_cols(guv, BRANCH_W, 2 * BRANCH_W), gnorm)
    ii, jj = _iota((LANES, LANES), 0), _iota((LANES, LANES), 1)
    eye = (ii == jj).astype(F32)
    wsm = [jnp.where(ii >= jj, ws[g], 0.0) for g in range(4)]
    bcol = [jnp.sum(eye * bs[g], axis=1, keepdims=True) for g in range(4)]
    chunks = []
    for c in range(tr // LANES):
        vc = _rowsl(v, c * LANES, (c + 1) * LANES)
        chunks.append(jnp.concatenate(
            [_nn16(wsm[g], _cols(vc, g * LANES, (g + 1) * LANES)) + bcol[g] for g in range(4)], axis=1))
    return None, [u * jnp.concatenate(chunks, axis=0) * _silu(z)]


def _swa_fn(first, _, rows, halos, params):
    q, kvc, z = rows
    sink_vec = params[0]
    P = LANES
    kv = jnp.concatenate([halos[0], kvc], axis=0)
    k, v = _cols(kv, 0, P), _cols(kv, P, 2 * P)
    r, cc = _iota((P, P), 0), _iota((P, P), 1)
    lane = _iota((1, P), 1)
    dist = _iota((P, 2 * P), 0) + P - _iota((P, 2 * P), 1)
    kmin = jnp.where(first, P, 0)
    valid = (dist >= 0) & (dist < P) & (_iota((P, 2 * P), 1) >= kmin)
    blocks = [None] * 4
    for kh in range(2):
        dup = (r == kh * 64 + (cc & 63)).astype(F32)
        kk, vv = _nn16(k, dup), _nn16(v, dup)
        for g in range(4):
            h = kh * 4 + g
            half = ((lane >= 64) == (h % 2 == 1)).astype(F32)
            qb = _cols(q, (h // 2) * P, (h // 2 + 1) * P) * half
            s = jnp.where(valid, _nt16(qb, kk) * 0.125, NEG_INF)
            sink = _lane_pick(sink_vec, h)
            m = lax.stop_gradient(jnp.maximum(jnp.max(s, axis=1, keepdims=True), sink))
            e = jnp.exp(s - m)
            p = e / (jnp.sum(e, axis=1, keepdims=True) + jnp.exp(sink - m))
            o = _nn16(p, vv) * half
            blocks[h // 2] = o if blocks[h // 2] is None else blocks[h // 2] + o
    return None, [jnp.concatenate(blocks, axis=1) * _silu(z)]


def _mem_fn(first, _, rows, halos, params):
    q, z = rows
    mkv = params[0]
    outs = []
    for h in range(4):
        a, b = h * LANES, (h + 1) * LANES
        s = _nt16(_cols(q, a, b), _cols(mkv, a, b)) * (LANES ** -0.5)
        m = lax.stop_gradient(jnp.max(s, axis=1, keepdims=True))
        e = jnp.exp(s - m)
        p = e / jnp.sum(e, axis=1, keepdims=True)
        outs.append(_nn16(p, _cols(mkv, BRANCH_W + a, BRANCH_W + b)))
    return None, [jnp.concatenate(outs, axis=1) * _silu(z)]


def _up_fn(first, _, rows, halos, params):
    ys, gl, w_up = rows[:4], rows[4], params[0]
    merged = None
    for n in range(4):
        term = _sigmoid(_cols(gl, n * D_MODEL, (n + 1) * D_MODEL)) * _nn16(ys[n], w_up[n])
        merged = term if merged is None else merged + term
    return None, [merged]


def _out_fn(first, _, rows, halos, params):
    x, merged = rows
    w, g = params
    return None, [x + _rms(_nn16(merged, w), g)]


def _loss_fn(first, _, rows, halos, params):
    y, t = rows
    d = y - t
    lrow = 0.5 * jnp.mean(d * d, axis=1, keepdims=True)
    return None, [d * (1.0 / D_MODEL), jnp.broadcast_to(lrow, (y.shape[0], LANES))]


TR = 256
DN_TR = 256
UP_TR = 128
CONV_HALO = 16
CARRY = (4, LANES, LANES)


def _branch_rows(cols, g):
    hb = CONV_HALO
    a = [Row(cols, 512, O_AQ // 512, hb, g), Row(cols, 512, O_AK // 512, hb, g), Row(cols, 512, O_AV // 512, hb, g),
         Row(cols, 512, O_AZ // 512, 0, g), Row(cols, 256, O_BA // 256)]
    b = [Row(cols, 1024, O_BUV // 1024, 0, g), Row(cols, 512, O_BZ // 512, 0, g)]
    c = [Row(cols, 512, O_CQ // 512, 0, g), Row(cols, 256, O_CKV // 256, LANES), Row(cols, 512, O_CZ // 512, 0, g)]
    m = [Row(cols, 512, O_MQ // 512, 0, g), Row(cols, 512, O_MZ // 512, 0, g)]
    return a, b, c, m


def _layer_fwd(x, mem, W):
    h = _rows_fwd("prenorm_fwd", _pre_fn, [Row(x, D_MODEL, 0)], [W["norm_pre"]], [(D_MODEL, BF16)], TR)[0]
    cols = _matmul("in_proj_fwd", h, W["w_pad"], "nt", BF16, 1024, 1024, 1024)
    mem_kv = _rows_fwd("memkv_fwd", _memkv_fn, [Row(mem, D_MODEL, 0)], [W["norm_mem"], W["w_mem_kv"]],
                       [(D_MODEL, F32)], MEM_LEN)[0]
    ra, rb, rc, rm = _branch_rows(cols, True)
    y_a, csave = _rows_fwd("dn_fwd", _dn_fn, ra, [W["conv"], W["a_vec"], W["dt_vec"], W["dn_norm"]],
                           [(BRANCH_W, BF16)], DN_TR, CARRY)
    y_b = _rows_fwd("gm_fwd", _gm_fn, rb, [W["gm_norm"], W["spatial_w"], W["spatial_b"]], [(BRANCH_W, BF16)], TR)[0]
    y_c = _rows_fwd("swa_fwd", _swa_fn, rc, [W["sink_vec"]], [(BRANCH_W, BF16)], LANES)[0]
    y_m = _rows_fwd("mem_fwd", _mem_fn, rm, [mem_kv], [(BRANCH_W, BF16)], TR)[0]
    ys = [y_a, y_b, y_c, y_m]
    merged = _rows_fwd("up_fwd", _up_fn, [Row(y, BRANCH_W, 0) for y in ys] + [Row(cols, 4 * D_MODEL, 0)],
                       [W["w_up"]], [(D_MODEL, BF16)], UP_TR)[0]
    x_new = _rows_fwd("out_fwd", _out_fn, [Row(x, D_MODEL, 0), Row(merged, D_MODEL, 0)],
                      [W["w_out"], W["norm_post"]], [(D_MODEL, F32)], TR)[0]
    return x_new, dict(x=x, h=h, cols=cols, mem_kv=mem_kv, csave=csave, ys=ys, merged=merged)


def _layer_bwd(dxn, mem, W, sv):
    x, cols = sv["x"], sv["cols"]
    (dx_res, dm), (dw_out, dnorm_post), _ = _rows_bwd(
        "out_bwd", _out_fn, [Row(x, D_MODEL, 0), Row(sv["merged"], D_MODEL, 0)], [W["w_out"], W["norm_post"]],
        [dxn], TR)
    dys, (dw_up,), dcols = _rows_bwd(
        "up_bwd", _up_fn, [Row(y, BRANCH_W, 0) for y in sv["ys"]] + [Row(cols, 4 * D_MODEL, 0, 0, "cols")],
        [W["w_up"]], [dm], UP_TR)
    ra, rb, rc, rm = _branch_rows(cols, "cols")
    (dba,), (dconv, da_vec, ddt_vec, ddn_norm), dcols = _rows_bwd(
        "dn_bwd", _dn_fn, ra, [W["conv"], W["a_vec"], W["dt_vec"], W["dn_norm"]], [dys[0]], DN_TR, CARRY,
        sv["csave"], dcols=dcols)
    _, (dgm_norm, dws, dbs), dcols = _rows_bwd(
        "gm_bwd", _gm_fn, rb, [W["gm_norm"], W["spatial_w"], W["spatial_b"]], [dys[1]], TR, dcols=dcols)
    (dkv_c,), (dsink,), dcols = _rows_bwd("swa_bwd", _swa_fn, rc, [W["sink_vec"]], [dys[2]], LANES, dcols=dcols)
    _, (dmem_kv,), dcols = _rows_bwd("mem_bwd", _mem_fn, rm, [sv["mem_kv"]], [dys[3]], TR, dcols=dcols)
    dcols = _fill_misc(dcols, dkv_c, dba, TR)
    _, (dnorm_mem, dw_mem_kv), _ = _rows_bwd("memkv_bwd", _memkv_fn, [Row(mem, D_MODEL, 0, 0, False)],
                                             [W["norm_mem"], W["w_mem_kv"]], [dmem_kv], MEM_LEN)
    dw_pad = _matmul("in_proj_dw", dcols, sv["h"], "tn", F32, 1024, 1024, 1024)
    dh = _matmul("in_proj_dx", dcols, W["w_pad"], "nn", F32, 1024, 1024, 1024)
    (dx,), (dnorm_pre,), _ = _rows_bwd("prenorm_bwd", _pre_fn_res, [Row(x, D_MODEL, 0)], [W["norm_pre"]],
                                       [dh, dx_res], TR)
    grads = dict(norm_pre=dnorm_pre, norm_post=dnorm_post, norm_mem=dnorm_mem, w_pad=dw_pad, conv=dconv,
                 a_vec=da_vec, dt_vec=ddt_vec, dn_norm=ddn_norm, gm_norm=dgm_norm, spatial_w=dws, spatial_b=dbs,
                 sink_vec=dsink, w_mem_kv=dw_mem_kv, w_up=dw_up, w_out=dw_out)
    return dx, grads


def _lane_vec(v, off):
    return jnp.zeros((1, LANES), F32).at[0, off:off + v.shape[0]].set(v)


def _layer_weights(l, w_pad, conv_w, w_mem_kv, w_up, w_out, small):
    return dict(
        w_pad=w_pad, conv=conv_w.reshape(4, 3, BRANCH_W).reshape(12, 1, BRANCH_W),
        w_mem_kv=w_mem_kv, w_up=w_up, w_out=w_out,
        norm_pre=small["norm_pre"][l][None], norm_post=small["norm_post"][l][None],
        norm_mem=small["norm_mem"][l][None],
        a_vec=_lane_vec(small["a_log"][l], 4), dt_vec=_lane_vec(small["dt_bias"][l], 4),
        dn_norm=small["dn_norm"][l][None], gm_norm=small["gm_norm"][l][None],
        spatial_w=small["spatial_w"][l], spatial_b=small["spatial_b"][l][:, None, :],
        sink_vec=_lane_vec(small["sinks"][l], 0))


_MESH = pl.DeviceIdType.MESH
_ANY = pl.BlockSpec(memory_space=pl.ANY)


def _position():
    return lax.axis_index("x"), lax.axis_index("y"), lax.axis_index("c")


def _remote(src, dst, send_sem, recv_sem, dev):
    return pltpu.make_async_remote_copy(src_ref=src, dst_ref=dst, send_sem=send_sem, recv_sem=recv_sem,
                                        device_id=dev, device_id_type=_MESH)


def _hbm_call(name, body, arrs, out_shapes, sems, aliases=None):
    return pl.pallas_call(
        body, name=name, in_specs=[_ANY] * len(arrs), out_specs=[_ANY] * len(out_shapes), out_shape=out_shapes,
        scratch_shapes=[pltpu.SemaphoreType.DMA((k,)) for k in sems], input_output_aliases=aliases or {},
        compiler_params=pltpu.CompilerParams(has_side_effects=True),
    )(*arrs)


def _other_chips(x, y):
    return [(1 - x, y), (x, 1 - y), (1 - x, 1 - y)]


def _gather_weights(arrs):
    n = len(arrs)

    def body(*refs):
        ins, outs = refs[:n], refs[n:2 * n]
        ici_send, ici_recv, d2d_send, d2d_recv = refs[2 * n:]
        x, y, c = _position()
        me = 2 * x + y
        chips = _other_chips(x, y)
        sends = []
        for a in range(n):
            for j, (px, py) in enumerate(chips):
                sends.append(_remote(ins[a].at[c], outs[a].at[c, me], ici_send.at[3 * a + j], ici_recv.at[3 * a + j],
                                     (px, py, c)))
                sends[-1].start()
        for a in range(n):
            for j, (px, py) in enumerate(chips):
                slab = outs[a].at[c, 2 * px + py]
                _remote(ins[a].at[c], slab, ici_send.at[3 * a + j], ici_recv.at[3 * a + j], (px, py, c)).wait_recv()
                sends.append(_remote(slab, slab, d2d_send.at[3 * a + j], d2d_recv.at[3 * a + j], (x, y, 1 - c)))
                sends[-1].start()
        for a in range(n):
            for j, (px, py) in enumerate(chips):
                slab = outs[a].at[1 - c, 2 * px + py]
                _remote(slab, slab, d2d_send.at[3 * a + j], d2d_recv.at[3 * a + j], (x, y, 1 - c)).wait_recv()
        for cp in sends:
            cp.wait_send()

    return _hbm_call("gather_weights", body, arrs,
                     [jax.ShapeDtypeStruct((N_LAYER, N_CHIP) + a.shape[1:], a.dtype) for a in arrs], [3 * n] * 4)


def _pair_exchange(arrs):
    n = len(arrs)

    def body(*refs):
        ins, outs = refs[:n], refs[n:2 * n]
        send_sems, recv_sems = refs[2 * n:]
        x, y, c = _position()
        cps = [_remote(ins[a].at[1 - c], outs[a], send_sems.at[a], recv_sems.at[a], (x, y, 1 - c)) for a in range(n)]
        for cp in cps:
            cp.start()
        for cp in cps:
            cp.wait_recv()
        for cp in cps:
            cp.wait_send()

    return _hbm_call("pair_exchange", body, arrs, [jax.ShapeDtypeStruct(a.shape[1:], a.dtype) for a in arrs], [n, n])


def _chip_scatter(arrs):
    n = len(arrs)

    def body(*refs):
        ins, outs = refs[:n], refs[n:2 * n]
        send_sems, recv_sems = refs[2 * n:]
        x, y, c = _position()
        me = 2 * x + y
        sends = []
        for a in range(n):
            for j, (px, py) in enumerate(_other_chips(x, y)):
                sends.append(_remote(ins[a].at[2 * px + py], outs[a].at[me], send_sems.at[3 * a + j],
                                     recv_sems.at[3 * a + j], (px, py, c)))
                sends[-1].start()
        for a in range(n):
            for j, (px, py) in enumerate(_other_chips(x, y)):
                _remote(ins[a].at[me], outs[a].at[2 * px + py], send_sems.at[3 * a + j], recv_sems.at[3 * a + j],
                        (px, py, c)).wait_recv()
        for cp in sends:
            cp.wait_send()

    return _hbm_call("chip_scatter", body, arrs, [jax.ShapeDtypeStruct(a.shape, a.dtype) for a in arrs],
                     [3 * n, 3 * n])


def _pair_share(arrs):
    n = len(arrs)

    def body(*refs):
        ins, outs = refs[:n], refs[n:2 * n]
        send_sems, recv_sems = refs[2 * n:]
        x, y, c = _position()
        cps = [_remote(ins[a].at[c], outs[a].at[c], send_sems.at[a], recv_sems.at[a], (x, y, 1 - c)) for a in range(n)]
        for cp in cps:
            cp.start()
        for a in range(n):
            _remote(ins[a].at[c], outs[a].at[1 - c], send_sems.at[a], recv_sems.at[a], (x, y, 1 - c)).wait_recv()
        for cp in cps:
            cp.wait_send()

    return _hbm_call("pair_share", body, arrs, [jax.ShapeDtypeStruct(a.shape, a.dtype) for a in arrs], [n, n],
                     {a: a for a in range(n)})


def _pair_forward(arrs):
    n = len(arrs)

    def body(*refs):
        ins, outs = refs[:n], refs[n:2 * n]
        send_sems, recv_sems = refs[2 * n:]
        x, y, c = _position()
        sends = []
        for a in range(n):
            for j, (px, py) in enumerate(_other_chips(x, y)):
                sends.append(_remote(ins[a].at[c, 2 * px + py], outs[a].at[c, 2 * px + py], send_sems.at[3 * a + j],
                                     recv_sems.at[3 * a + j], (x, y, 1 - c)))
                sends[-1].start()
        for a in range(n):
            for j, (px, py) in enumerate(_other_chips(x, y)):
                slab = outs[a].at[1 - c, 2 * px + py]
                _remote(slab, slab, send_sems.at[3 * a + j], recv_sems.at[3 * a + j], (x, y, 1 - c)).wait_recv()
        for cp in sends:
            cp.wait_send()

    return _hbm_call("pair_forward", body, arrs, [jax.ShapeDtypeStruct(a.shape, a.dtype) for a in arrs],
                     [3 * n, 3 * n], {a: a for a in range(n)})


_HBM = pl.BlockSpec(memory_space=pltpu.HBM)
_SEM = pl.BlockSpec(memory_space=pltpu.SEMAPHORE)
_EFFECT = pltpu.SideEffectType.DATAFLOW_SIDE_EFFECTING


def _chip_copies(kind, srcs, lands, send_sems, recv_sems):
    x, y, c = _position()
    me = 2 * x + y
    sends, recvs = [], []
    for a in range(len(srcs)):
        for j, (px, py) in enumerate(_other_chips(x, y)):
            s, sems, dev = 2 * px + py, (send_sems.at[3 * a + j], recv_sems.at[3 * a + j]), (px, py, c)
            if kind == "gather":
                sends.append(_remote(srcs[a].at[c], lands[a].at[c, me], *sems, dev))
                recvs.append(_remote(srcs[a].at[c], lands[a].at[c, s], *sems, dev))
            else:
                sends.append(_remote(srcs[a].at[s], lands[a].at[me], *sems, dev))
                recvs.append(_remote(srcs[a].at[me], lands[a].at[s], *sems, dev))
    return sends, recvs


def _split_start(name, kind, srcs, land_shapes, after):
    n = len(srcs)

    def body(*refs):
        sends, _ = _chip_copies(kind, refs[:n], refs[n:2 * n], refs[2 * n + 1], refs[2 * n + 2])
        for cp in sends:
            cp.start()
        refs[-1][...] = jnp.zeros_like(refs[-1])

    hbm = lambda a: pltpu.with_memory_space_constraint(a, pltpu.HBM)
    lands = [lax.empty(s.shape, s.dtype) for s in land_shapes]
    outs = pl.pallas_call(
        body, name=name, in_specs=[_HBM] * (2 * n) + [_ANY],
        out_specs=[_SEM, _SEM] + [_HBM] * (2 * n) + [pl.BlockSpec(memory_space=pltpu.VMEM)],
        out_shape=[pltpu.SemaphoreType.DMA((3 * n,)), pltpu.SemaphoreType.DMA((3 * n,))]
        + [pltpu.HBM(a.shape, a.dtype) for a in list(srcs) + lands] + [jax.ShapeDtypeStruct((8, LANES), F32)],
        input_output_aliases={i: 2 + i for i in range(2 * n)},
        compiler_params=pltpu.CompilerParams(has_side_effects=_EFFECT),
    )(*[hbm(a) for a in srcs], *[hbm(a) for a in lands], after)
    return outs[0], outs[1], list(outs[2:2 + 2 * n]), outs[-1]


def _split_wait(name, kind, started, after):
    send_sems, recv_sems, thru, _ = started
    n = len(thru) // 2

    def body(*refs):
        sends, recvs = _chip_copies(kind, refs[:n], refs[n:2 * n], refs[2 * n], refs[2 * n + 1])
        for cp in sends:
            cp.wait_send()
        for cp in recvs:
            cp.wait_recv()

    outs = pl.pallas_call(
        body, name=name, in_specs=[_HBM] * (2 * n) + [_SEM, _SEM, _ANY], out_specs=[_HBM] * (2 * n),
        out_shape=[pltpu.HBM(a.shape, a.dtype) for a in thru], input_output_aliases={i: i for i in range(2 * n)},
        compiler_params=pltpu.CompilerParams(has_side_effects=_EFFECT),
    )(*thru, send_sems, recv_sems, after)
    return list(outs[:n]), list(outs[n:])


def _allreduce_small(g):
    def body(g_ref, o_ref, buf, send_sems, recv_sems):
        x, y, c = _position()
        me = 4 * x + 2 * y + c
        buf[me] = g_ref[...]
        peers = []
        for j in range(1, N_DEV):
            px = 1 - x if j & 4 else x
            py = 1 - y if j & 2 else y
            pc = 1 - c if j & 1 else c
            peers.append((px, py, pc))
        sends = [_remote(g_ref, buf.at[me], send_sems.at[j], recv_sems.at[j], p) for j, p in enumerate(peers)]
        for cp in sends:
            cp.start()
        for j, (px, py, pc) in enumerate(peers):
            _remote(g_ref, buf.at[4 * px + 2 * py + pc], send_sems.at[j], recv_sems.at[j], (px, py, pc)).wait_recv()
        for cp in sends:
            cp.wait_send()
        acc = buf[0]
        for s in range(1, N_DEV):
            acc = acc + buf[s]
        o_ref[...] = acc

    vmem = pl.BlockSpec(memory_space=pltpu.VMEM)
    return pl.pallas_call(
        body, name="allreduce_small", in_specs=[vmem], out_specs=vmem, out_shape=jax.ShapeDtypeStruct(g.shape, F32),
        scratch_shapes=[pltpu.VMEM((N_DEV,) + g.shape, F32), pltpu.SemaphoreType.DMA((N_DEV - 1,)),
                        pltpu.SemaphoreType.DMA((N_DEV - 1,))],
        compiler_params=_params(),
    )(g)


EW_ROWS = 512


def _ew(name, fn, ins, n_out, out_dtype=F32, out_slot=None, into=None):
    def dims(a):
        return a[0].shape[1:] if isinstance(a, tuple) else a.shape

    R, w = dims(ins[0])
    tr = EW_ROWS if R % EW_ROWS == 0 else R
    n_into = len(into) if into else 0

    def body(c_ref, *refs):
        outs = fn(*[r[...] for r in refs[:len(ins)]])
        for r, v in zip(refs[len(ins) + n_into:], outs):
            r[...] = v.astype(r.dtype)

    def lead_spec(l):
        if l == "c":
            return pl.BlockSpec((None, tr, w), lambda i, c_ref: (c_ref[0], i, 0))
        return pl.BlockSpec((None, tr, w), lambda i, c_ref, s=l: (s, i, 0))

    plain = pl.BlockSpec((tr, w), lambda i, c_ref: (i, 0))
    in_specs = [lead_spec(a[1]) if isinstance(a, tuple) else plain for a in ins] + [_ANY] * n_into
    out_spec = plain if out_slot is None else lead_spec(out_slot)
    out_shape = jax.ShapeDtypeStruct((R, w) if out_slot is None else (2, R, w), out_dtype)
    return pl.pallas_call(
        body, name=name,
        grid_spec=pltpu.PrefetchScalarGridSpec(num_scalar_prefetch=1, grid=(R // tr,), in_specs=in_specs,
                                               out_specs=[out_spec] * n_out),
        out_shape=[out_shape] * n_out, input_output_aliases={1 + len(ins) + j: j for j in range(n_into)},
        compiler_params=_params(dimension_semantics=("arbitrary",)),
    )(lax.axis_index("c").astype(jnp.int32).reshape(1), *[a[0] if isinstance(a, tuple) else a for a in ins],
      *(into or []))


def _adamw_fn(w, g, m, v):
    m = ADAM_B1 * m + (1.0 - ADAM_B1) * g
    v = ADAM_B2 * v + (1.0 - ADAM_B2) * (g * g)
    m_hat = m / (1.0 - ADAM_B1 ** ADAM_STEP)
    v_hat = v / (1.0 - ADAM_B2 ** ADAM_STEP)
    delta = -ADAM_LR * (m_hat / (jnp.sqrt(v_hat) + ADAM_EPS) + ADAM_WD * w)
    return delta, m, v


def _adamw(name, w, g, m, v):
    shape = w.shape
    two = lambda a: a.reshape(-1, shape[-1])
    return [o.reshape(shape) for o in _ew(name, _adamw_fn, [two(w), two(g), two(m), two(v)], 3)]


def _adamw_layer(name, l, w, g, m, v, into):
    k = w.shape[-1]
    three = lambda a: (a.reshape(N_LAYER, -1, k), l)
    fn = lambda w_, g_, m_, v_: _adamw_fn(w_, g_, m_, v_) + (g_,)
    outs = _ew(name, fn, [three(w), g.reshape(-1, k), three(m), three(v)], 4, out_slot=l,
               into=None if into is None else [a.reshape(N_LAYER, -1, k) for a in into])
    return [o.reshape(w.shape) for o in outs]


def _adamw_rows(name, l, w, g, m, v, into):
    _, R, k = w.shape
    n_into = len(into) if into else 0

    def body(*refs):
        w_ref, g_ref, m_ref, v_ref = refs[:4]
        d_out, m_out, v_out, g_out = refs[4 + n_into:]
        g_blk = g_ref[...]
        d_out[...], m_out[...], v_out[...] = _adamw_fn(w_ref[...], g_blk, m_ref[...], v_ref[...])
        g_out[...] = g_blk

    spec = pl.BlockSpec((None, EW_ROWS, k), lambda i: (l, i, 0))
    return pl.pallas_call(
        body, name=name, grid=(-(-R // EW_ROWS),),
        in_specs=[spec, pl.BlockSpec((EW_ROWS, k), lambda i: (i, 0)), spec, spec] + [_ANY] * n_into,
        out_specs=[spec] * 4, out_shape=[jax.ShapeDtypeStruct((N_LAYER, R, k), F32)] * 4,
        input_output_aliases={4 + j: j for j in range(n_into)},
        compiler_params=_params(dimension_semantics=("arbitrary",)),
    )(w, g, m, v, *(into or []))


_SMALL = [("norm_pre", (2, 1024)), ("norm_post", (2, 1024)), ("norm_mem", (2, 1024)), ("a_log", (2, 4)),
          ("dt_bias", (2, 4)), ("dn_norm", (2, 128)), ("gm_norm", (2, 512)), ("spatial_w", (2, 4, 128, 128)),
          ("spatial_b", (2, 4, 128)), ("sinks", (2, 8))]
_SMALL_ROWS = 200
_BIG = ["w_in", "conv_w", "w_mem_kv", "w_up", "w_out"]
_NAMES = ["norm_pre", "norm_post", "norm_mem", "w_in", "conv_w", "a_log", "dt_bias", "dn_norm", "gm_norm",
          "spatial_w", "spatial_b", "sinks", "w_mem_kv", "w_up", "w_out"]


def _size(shape):
    n = 1
    for s in shape:
        n *= s
    return n


_PACK_UNIT = 8 * 1024


def _pack_small(d):
    rows = []
    for n, shp in _SMALL:
        flat = d[n].reshape(-1)
        rows.append(jnp.pad(flat, (0, -flat.shape[0] % _PACK_UNIT)).reshape(-1, 1024))
    assert sum(r.shape[0] for r in rows) == _SMALL_ROWS
    return jnp.concatenate(rows, axis=0)


def _unpack_small(p):
    out, off = {}, 0
    for n, shp in _SMALL:
        k = -(-_size(shp) // _PACK_UNIT) * 8
        out[n] = p[off:off + k].reshape(-1)[:_size(shp)].reshape(shp)
        off += k
    return out


_HALF_SHAPE = {"w_in": (SHARD_PAD // 2, D_MODEL), "conv_w": (2, 3 * BRANCH_W // N_CHIP), "w_mem_kv": (128, D_MODEL),
               "w_up": (2, BRANCH_W, D_MODEL // N_CHIP), "w_out": (128, D_MODEL)}


def _chip_major(g):
    g = jnp.swapaxes(g, 0, 1)
    return g.reshape((N_CHIP, 2 * g.shape[2]) + g.shape[3:])


def _half_major(g):
    g = g.reshape((N_CHIP, 2, g.shape[1] // 2) + g.shape[2:])
    return jnp.swapaxes(g, 0, 1).astype(BF16)


def _weight_views(l, gathered, small):
    g_in, g_conv, g_kv, g_up, g_out = gathered
    return _layer_weights(
        l, _w_pad_from_slabs(g_in), _chip_major(g_conv).transpose(1, 0, 2).reshape(4, 3 * BRANCH_W),
        _chip_major(g_kv).reshape(D_MODEL, D_MODEL),
        _chip_major(g_up).transpose(1, 2, 0, 3).reshape(4, BRANCH_W, D_MODEL),
        _chip_major(g_out).reshape(D_MODEL, D_MODEL), small)


def _pair_sums(g):
    big = [_slabs_from_pad(g["w_pad"]),
           _half_major(g["conv"].reshape(4, N_CHIP, 3 * BRANCH_W // N_CHIP).transpose(1, 0, 2)),
           _half_major(g["w_mem_kv"].reshape(N_CHIP, D_MODEL // N_CHIP, D_MODEL)),
           _half_major(g["w_up"].reshape(4, BRANCH_W, N_CHIP, D_MODEL // N_CHIP).transpose(2, 0, 1, 3)),
           _half_major(g["w_out"].reshape(N_CHIP, D_MODEL // N_CHIP, D_MODEL))]
    add2 = lambda a, b: [a.astype(F32) + b.astype(F32)]
    pair = []
    for n, b, p in zip(_BIG, big, _pair_exchange(big)):
        k = b.shape[-1]
        pair.append(_ew("pair_sum_" + n, add2, [(b.reshape(2, -1, k), "c"), p.reshape(-1, k)], 1, BF16)[0]
                    .reshape(p.shape))
    return pair


def _chip_sums(landed, pair, me):
    add4 = lambda a, b, c_, d: [((a.astype(F32) + b.astype(F32)) + c_.astype(F32)) + d.astype(F32)]
    totals = []
    for n, r, q in zip(_BIG, landed, pair):
        r = _own_slot(r, lax.dynamic_index_in_dim(q, me, 0), me, 0)
        k = r.shape[-1]
        totals.append(_ew("chip_sum_" + n, add4, [(r.reshape(N_CHIP, -1, k), s) for s in range(N_CHIP)], 1,
                          out_slot="c")[0].reshape((2,) + r.shape[1:]))
    return totals


def _own_slot(buf, mine, me, axis):
    return lax.dynamic_update_index_in_dim(buf, mine.astype(buf.dtype), me, axis)


def kernel(x, mem, norm_pre, norm_post, norm_mem, w_in, conv_w, a_log, dt_bias, dn_norm, gm_norm, spatial_w, spatial_b, sinks, w_mem_kv, w_up, w_out, loss_target, m_norm_pre, m_norm_post, m_norm_mem, m_w_in, m_conv_w, m_a_log, m_dt_bias, m_dn_norm, m_gm_norm, m_spatial_w, m_spatial_b, m_sinks, m_w_mem_kv, m_w_up, m_w_out, v_norm_pre, v_norm_post, v_norm_mem, v_w_in, v_conv_w, v_a_log, v_dt_bias, v_dn_norm, v_gm_norm, v_spatial_w, v_spatial_b, v_sinks, v_w_mem_kv, v_w_up, v_w_out):
    w = dict(norm_pre=norm_pre, norm_post=norm_post, norm_mem=norm_mem, w_in=w_in, conv_w=conv_w, a_log=a_log,
             dt_bias=dt_bias, dn_norm=dn_norm, gm_norm=gm_norm, spatial_w=spatial_w, spatial_b=spatial_b, sinks=sinks,
             w_mem_kv=w_mem_kv, w_up=w_up, w_out=w_out)
    m = dict(norm_pre=m_norm_pre, norm_post=m_norm_post, norm_mem=m_norm_mem, w_in=m_w_in, conv_w=m_conv_w,
             a_log=m_a_log, dt_bias=m_dt_bias, dn_norm=m_dn_norm, gm_norm=m_gm_norm, spatial_w=m_spatial_w,
             spatial_b=m_spatial_b, sinks=m_sinks, w_mem_kv=m_w_mem_kv, w_up=m_w_up, w_out=m_w_out)
    v = dict(norm_pre=v_norm_pre, norm_post=v_norm_post, norm_mem=v_norm_mem, w_in=v_w_in, conv_w=v_conv_w,
             a_log=v_a_log, dt_bias=v_dt_bias, dn_norm=v_dn_norm, gm_norm=v_gm_norm, spatial_w=v_spatial_w,
             spatial_b=v_spatial_b, sinks=v_sinks, w_mem_kv=v_w_mem_kv, w_up=v_w_up, w_out=v_w_out)
    me = 2 * lax.axis_index("x") + lax.axis_index("y")

    w_in_t = jnp.pad(w_in.astype(BF16).transpose(0, 2, 1), ((0, 0), (0, SHARD_PAD - SHARD_IN), (0, 0)))
    local = dict(w_in=w_in_t, conv_w=conv_w, w_mem_kv=w_mem_kv.astype(BF16), w_up=w_up.astype(BF16),
                 w_out=w_out.astype(BF16))
    halves = lambda l: [local[n][l].reshape((2,) + _HALF_SHAPE[n]) for n in _BIG]
    own = lambda gathered, mine: [_own_slot(g, h[:, None], me, 1) for g, h in zip(gathered, mine)]
    g0 = own(_gather_weights(halves(0)), halves(0))
    started = _split_start("gather_l1_start", "gather", halves(1),
                           [jax.ShapeDtypeStruct((2, N_CHIP) + _HALF_SHAPE[n], local[n].dtype) for n in _BIG], g0[1])

    xl, meml = x[0], mem[0]
    W0 = _weight_views(0, g0, w)
    W0["norm_pre"] = W0["norm_pre"] + started[3][0, 0]
    x1, sv0 = _layer_fwd(xl, meml, W0)
    mine1, landed1 = _split_wait("gather_l1_wait", "gather", started, x1)
    W1 = _weight_views(1, own(_pair_forward(landed1), mine1), w)
    x2, sv1 = _layer_fwd(x1, meml, W1)
    dy, lrows = _rows_fwd("loss", _loss_fn, [Row(x2, D_MODEL, 0), Row(loss_target[0], D_MODEL, 0)], [],
                          [(D_MODEL, F32), (LANES, F32)], TR)
    loss = lax.psum(jnp.sum(lrows[:, 0]), ("x", "y", "c"))

    dx1, grads1 = _layer_bwd(dy, meml, W1, sv1)
    pair1 = _pair_sums(grads1)
    scattering = _split_start("scatter_l1_start", "scatter", pair1,
                              [jax.ShapeDtypeStruct(p.shape, p.dtype) for p in pair1], pair1[1])
    W0["norm_post"] = W0["norm_post"] + scattering[3][0, 0]
    dx, grads0 = _layer_bwd(dx1, meml, W0, sv0)
    pair1, landed1 = _split_wait("scatter_l1_wait", "scatter", scattering, dx)
    pair0 = _pair_sums(grads0)
    scattering = _split_start("scatter_l0_start", "scatter", pair0,
                              [jax.ShapeDtypeStruct(p.shape, p.dtype) for p in pair0], pair0[1])
    after_start = scattering[3][0, 0]
    grads = [grads0, grads1]

    small_local = dict(
        norm_pre=jnp.stack([g["norm_pre"][0] for g in grads]), norm_post=jnp.stack([g["norm_post"][0] for g in grads]),
        norm_mem=jnp.stack([g["norm_mem"][0] for g in grads]), a_log=jnp.stack([g["a_vec"][0, 4:8] for g in grads]),
        dt_bias=jnp.stack([g["dt_vec"][0, 4:8] for g in grads]), dn_norm=jnp.stack([g["dn_norm"][0] for g in grads]),
        gm_norm=jnp.stack([g["gm_norm"][0] for g in grads]), spatial_w=jnp.stack([g["spatial_w"] for g in grads]),
        spatial_b=jnp.stack([g["spatial_b"][:, 0, :] for g in grads]),
        sinks=jnp.stack([g["sink_vec"][0, :8] for g in grads]))
    gsmall_packed = _allreduce_small(_pack_small(small_local) + after_start)

    d_s, m_s, v_s = _ew("adamw_small", _adamw_fn, [_pack_small(w), gsmall_packed, _pack_small(m), _pack_small(v)], 3)
    gsmall, dsmall, msmall, vsmall = (_unpack_small(p) for p in (gsmall_packed, d_s, m_s, v_s))
    g_o, d_o, m_o, v_o = dict(gsmall), dict(dsmall), dict(msmall), dict(vsmall)
    tr = lambda a: a.transpose(0, 2, 1)
    w_t, m_t, v_t = tr(w["w_in"]), tr(m["w_in"]), tr(v["w_in"])

    def update(l, totals, into):
        outs = {}
        for n, t in zip(_BIG, totals):
            g_l = t.reshape(local[n].shape[1:])
            if n == "w_in":
                outs[n] = _adamw_rows("adamw_" + n, l, w_t, g_l, m_t, v_t, into and into[n])
            else:
                outs[n] = _adamw_layer("adamw_" + n, l, w[n], g_l, m[n], v[n], into and into[n])
        return outs

    landed1[1] = landed1[1] + after_start.astype(landed1[1].dtype)
    outs1 = update(1, _pair_share(_chip_sums(landed1, pair1, me)), None)
    pair0, landed0 = _split_wait("scatter_l0_wait", "scatter", scattering, outs1["w_in"][0])
    outs = update(0, _pair_share(_chip_sums(landed0, pair0, me)), outs1)
    for n in _BIG:
        d_o[n], m_o[n], v_o[n], g_o[n] = [tr(o) for o in outs[n]] if n == "w_in" else outs[n]
    return (loss, dx[None], *[g_o[n] for n in _NAMES], *[d_o[n] for n in _NAMES], *[m_o[n] for n in _NAMES],
            *[v_o[n] for n in _NAMES])
```

```python
import collections
import functools

import jax
import jax.numpy as jnp
from jax import lax
from jax.experimental import pallas as pl
from jax.experimental.pallas import tpu as pltpu

F32 = jnp.float32
BF16 = jnp.bfloat16

D_MODEL = 1024
BRANCH_W = 512
MEM_LEN = 256
N_LAYER = 2
N_CHIP = 4
N_DEV = 8
EPS = 1e-6
NEG_INF = -1e30
DN_CHUNK = 64
LANES = 128
VMEM_LIMIT = 48 * 1024 * 1024

ADAM_LR, ADAM_B1, ADAM_B2, ADAM_EPS, ADAM_WD, ADAM_STEP = 0.001, 0.9, 0.999, 1e-08, 0.01, 10

N_PAD = 10240
O_GATE = 0
O_AQ, O_AK, O_AV, O_AZ = 4096, 4608, 5120, 5632
O_BUV, O_BZ = 6144, 7168
O_CKV, O_BA = 7680, 7936
O_CQ, O_CZ = 8192, 8704
O_MQ, O_MZ = 9216, 9728
O_MISC, W_MISC = O_CKV, 512
_PAD_SEGS = [(5896, 4096), (0, 512), (512, 512), (1024, 512), (1536, 512), (2056, 1024), (3080, 512),
             (4104, 128), (4232, 128), (2048, 8), (None, 120), (None, 128),
             (3592, 512), (4360, 512), (4872, 512), (5384, 512)]
D_IN = 9992
SHARD_IN = D_IN // N_CHIP


SHARD_PAD = 2560


def _pad_parts():
    parts, off = [], 0
    for s, n in _PAD_SEGS:
        a = s
        while s is not None and a < s + n:
            chip = a // SHARD_IN
            b = min(s + n, (chip + 1) * SHARD_IN)
            parts.append((chip, a - chip * SHARD_IN, off + a - s, b - a))
            a = b
        off += n
    return parts


PERM_ROWS = 512
PERM_SLACK = 32


def _permute_rows(name, src, parts, n_out, out_dtype):
    B, Z = PERM_ROWS, PERM_ROWS + PERM_SLACK
    w = src.shape[1]
    plans = []
    for blk in range(n_out // B):
        o, runs = blk * B, []
        for s, d, n in parts:
            lo, hi = max(d, o), min(d + n, o + B)
            if lo < hi:
                s0 = s + lo - d
                wa = s0 // 16 * 16
                wb = min(-(-(s0 + hi - lo) // 16) * 16, src.shape[0])
                runs.append((wa, wb - wa, s0 - (lo - o) - wa, lo - o, hi - o))
        plans.append(runs)
    max_runs = max(len(r) for r in plans)
    nblk = len(plans)

    def body(*refs):
        src_ref, out_ref, inbuf, obuf, insem, outsem = (refs[0],) + refs[-5:]

        def in_copies(blk):
            return [pltpu.make_async_copy(src_ref.at[pl.ds(wa, ws)], inbuf.at[blk % 2, r, pl.ds(0, ws)],
                                          insem.at[blk % 2, r]) for r, (wa, ws, _, _, _) in enumerate(plans[blk])]

        def out_copy(blk):
            return pltpu.make_async_copy(obuf.at[blk % 2], out_ref.at[pl.ds(blk * B, B)], outsem.at[blk % 2])

        for cp in in_copies(0):
            cp.start()
        rid = _iota((B, 1), 0)
        for blk in range(nblk):
            if blk + 1 < nblk:
                for cp in in_copies(blk + 1):
                    cp.start()
            for cp in in_copies(blk):
                cp.wait()
            val = jnp.zeros((B, w), F32)
            for r, (wa, ws, t, l0, l1) in enumerate(plans[blk]):
                win = jnp.concatenate([inbuf[blk % 2, r, pl.ds(0, ws)].astype(F32), jnp.zeros((Z - ws, w), F32)], axis=0)
                moved = pltpu.roll(win, (-t) % Z, 0)[:B]
                val = jnp.where((rid >= l0) & (rid < l1), moved, val)
            if blk >= 2:
                out_copy(blk - 2).wait()
            obuf[blk % 2] = val.astype(out_dtype)
            out_copy(blk).start()
        for blk in range(max(nblk - 2, 0), nblk):
            out_copy(blk).wait()

    return pl.pallas_call(
        body, name=name, in_specs=[_ANY], out_specs=_ANY, out_shape=jax.ShapeDtypeStruct((n_out, w), out_dtype),
        scratch_shapes=[pltpu.VMEM((2, max_runs, Z, w), src.dtype), pltpu.VMEM((2, B, w), out_dtype),
                        pltpu.SemaphoreType.DMA((2, max_runs)), pltpu.SemaphoreType.DMA((2,))],
        compiler_params=_params(),
    )(src)


def _slab_parts():
    h, out = SHARD_PAD // 2, []
    for chip, s, d, n in _pad_parts():
        a = s
        while a < s + n:
            half = a // h
            b = min(s + n, (half + 1) * h)
            out.append(((half * N_CHIP + chip) * h + a - half * h, d + a - s, b - a))
            a = b
    return out


def _w_pad_from_slabs(slabs):
    return _permute_rows("w_pad_rows", slabs.reshape(-1, slabs.shape[-1]), _slab_parts(), N_PAD, BF16)


def _slabs_from_pad(dw):
    slabs = _permute_rows("w_pad_grad_rows", dw, [(d, s, n) for s, d, n in _slab_parts()], N_CHIP * SHARD_PAD, BF16)
    return slabs.reshape(2, N_CHIP, SHARD_PAD // 2, dw.shape[1])


def _dot(a, b, dims, prec):
    if prec == "bf16":
        return lax.dot_general(a.astype(BF16), b.astype(BF16), (dims, ((), ())), preferred_element_type=F32)
    return lax.dot_general(a, b, (dims, ((), ())), precision=lax.Precision.HIGHEST, preferred_element_type=F32)


_NN, _NT, _TN = ((1,), (0,)), ((1,), (1,)), ((0,), (0,))


def _make_mm(prec):
    @jax.custom_vjp
    def nn(a, b):
        return _dot(a, b, _NN, prec)

    @jax.custom_vjp
    def nt(a, b):
        return _dot(a, b, _NT, prec)

    @jax.custom_vjp
    def tn(a, b):
        return _dot(a, b, _TN, prec)

    nn.defvjp(lambda a, b: (nn(a, b), (a, b)), lambda r, g: (nt(g, r[1]), tn(r[0], g)))
    nt.defvjp(lambda a, b: (nt(a, b), (a, b)), lambda r, g: (nn(g, r[1]), tn(g, r[0])))
    tn.defvjp(lambda a, b: (tn(a, b), (a, b)), lambda r, g: (nt(r[1], g), nn(r[0], g)))
    return nn, nt, tn


_nn16, _nt16, _tn16 = _make_mm("bf16")
_nn32, _nt32, _tn32 = _make_mm("f32")


def _make_slice(axis):
    @functools.partial(jax.custom_vjp, nondiff_argnums=(1, 2, 3))
    def sl(x, a, b, n):
        return x[a:b] if axis == 0 else x[:, a:b]

    def fwd(x, a, b, n):
        return sl(x, a, b, n), None

    def bwd(a, b, n, _, g):
        parts = []
        if a > 0:
            parts.append(jnp.zeros((a, g.shape[1]) if axis == 0 else (g.shape[0], a), g.dtype))
        parts.append(g)
        if n - b > 0:
            parts.append(jnp.zeros((n - b, g.shape[1]) if axis == 0 else (g.shape[0], n - b), g.dtype))
        return (jnp.concatenate(parts, axis=axis),)

    sl.defvjp(fwd, bwd)
    return sl


_sl0, _sl1 = _make_slice(0), _make_slice(1)


def _rowsl(x, a, b):
    return _sl0(x, a, b, x.shape[0])


def _cols(x, a, b):
    return _sl1(x, a, b, x.shape[1])


@functools.partial(jax.custom_vjp, nondiff_argnums=(1,))
def _rollr(x, s):
    return pltpu.roll(x, s, 0)


_rollr.defvjp(lambda x, s: (_rollr(x, s), None),
              lambda s, _, g: (pltpu.roll(g, g.shape[0] - s, 0),))


def _iota(shape, axis):
    return lax.broadcasted_iota(jnp.int32, shape, axis)


def _sigmoid(x):
    return lax.logistic(x)


def _silu(x):
    return x * _sigmoid(x)


def _gelu(x):
    return 0.5 * x * (1.0 + jnp.tanh(0.7978845608028654 * (x + 0.044715 * (x * x * x))))


def _softplus(x):
    return jnp.maximum(x, 0.0) + jnp.log(1.0 + jnp.exp(-jnp.abs(x)))


def _rms(x, g):
    return x * lax.rsqrt(jnp.mean(x * x, axis=-1, keepdims=True) + EPS) * g


def _lane_pick(x, lane):
    return jnp.sum(x * (_iota((1, x.shape[1]), 1) == lane).astype(F32), axis=1, keepdims=True)


Row = collections.namedtuple("Row", "arr w cb hb grad", defaults=(0, True))


def _full_spec(shape):
    return pl.BlockSpec(shape, lambda i, _n=len(shape): (0,) * _n)


def _load_params(refs):
    return [[p[g].astype(F32) for g in range(p.shape[0])] if len(p.shape) == 3 else p[...].astype(F32)
            for p in refs]


def _params(**kw):
    return pltpu.CompilerParams(vmem_limit_bytes=VMEM_LIMIT, **kw)


def _rows_fwd(name, fn, rows, params, outs, tr, carry=None):
    T = rows[0].arr.shape[0]
    n = T // tr
    halos = [r for r in rows if r.hb]
    nr, nh, npar, no = len(rows), len(halos), len(params), len(outs)

    def body(*refs):
        row_refs, halo_refs = refs[:nr], refs[nr:nr + nh]
        par_refs = refs[nr + nh:nr + nh + npar]
        out_refs = refs[nr + nh + npar:nr + nh + npar + no]
        rest = refs[nr + nh + npar + no:]
        first = pl.program_id(0) == 0
        cvals = None
        if carry is not None:
            csave_ref, carry_ref = rest

            @pl.when(first)
            def _():
                carry_ref[...] = jnp.zeros_like(carry_ref)

            cvals = [carry_ref[g] for g in range(carry[0])]
            for g in range(carry[0]):
                csave_ref[0, g] = cvals[g]
        c_out, o = fn(first, cvals, [r[...].astype(F32) for r in row_refs],
                      [h[...].astype(F32) for h in halo_refs], _load_params(par_refs))
        for r, v in zip(out_refs, o):
            r[...] = v.astype(r.dtype)
        if carry is not None:
            for g in range(carry[0]):
                carry_ref[g] = c_out[g]

    in_specs = [pl.BlockSpec((tr, r.w), lambda i, c=r.cb: (i, c)) for r in rows]
    in_specs += [pl.BlockSpec((r.hb, r.w), lambda i, c=r.cb, q=tr // r.hb: (jnp.maximum(i * q - 1, 0), c))
                 for r in halos]
    in_specs += [_full_spec(p.shape) for p in params]
    out_shape = [jax.ShapeDtypeStruct((T, w), dt) for w, dt in outs]
    out_specs = [pl.BlockSpec((tr, w), lambda i: (i, 0)) for w, _ in outs]
    scratch = []
    if carry is not None:
        out_shape.append(jax.ShapeDtypeStruct((n,) + carry, F32))
        out_specs.append(pl.BlockSpec((1,) + carry, lambda i: (i, 0, 0, 0)))
        scratch.append(pltpu.VMEM(carry, F32))
    return pl.pallas_call(
        body, name=name, grid=(n,), in_specs=in_specs, out_specs=out_specs, out_shape=out_shape,
        scratch_shapes=scratch, compiler_params=_params(dimension_semantics=("arbitrary",)),
    )(*[r.arr for r in rows], *[r.arr for r in halos], *params)


def _rows_bwd(name, fn, rows, params, douts, tr, carry=None, csave=None, dcols=None):
    T = rows[0].arr.shape[0]
    n = T // tr
    halos = [r for r in rows if r.hb]
    grows = [r for r in rows if r.grad is True]
    crows = [r for r in rows if r.grad == "cols"]
    wcols = sum(r.w for r in crows)
    nr, nh, npar, nd, ng = len(rows), len(halos), len(params), len(douts), len(grows)
    nc = 0 if carry is None else 1
    ncol = 1 if crows else 0
    nalias = 1 if (crows and dcols is not None) else 0

    def body(*refs):
        row_refs, halo_refs = refs[:nr], refs[nr:nr + nh]
        par_refs = refs[nr + nh:nr + nh + npar]
        k = nr + nh + npar
        csave_ref = refs[k] if nc else None
        dout_refs = refs[k + nc:k + nc + nd]
        k = k + nc + nd + nalias
        drow_refs = refs[k:k + ng]
        dcols_ref = refs[k + ng] if ncol else None
        dpar_refs = refs[k + ng + ncol:k + ng + ncol + npar]
        k = k + ng + ncol + npar
        dcarry_ref = refs[k] if nc else None
        hgrad_refs = refs[k + nc:]
        i = pl.program_id(0)
        first_tile = i == n - 1

        @pl.when(i == 0)
        def _():
            for r in dpar_refs:
                r[...] = jnp.zeros_like(r)
            for r in hgrad_refs:
                r[...] = jnp.zeros_like(r)
            if nc:
                dcarry_ref[...] = jnp.zeros_like(dcarry_ref)

        rv = [r[...].astype(F32) for r in row_refs]
        hv = [h[...].astype(F32) for h in halo_refs]
        pv = _load_params(par_refs)
        dov = [d[...].astype(F32) for d in dout_refs]
        if nc:
            cv = [csave_ref[0, g] for g in range(carry[0])]
            _, vjp = jax.vjp(lambda c, r, h, p: fn(first_tile, c, r, h, p), cv, rv, hv, pv)
            dc, dr, dh, dp = vjp(([dcarry_ref[g] for g in range(carry[0])], dov))
            for g in range(carry[0]):
                dcarry_ref[g] = dc[g]
        else:
            _, vjp = jax.vjp(lambda r, h, p: fn(first_tile, None, r, h, p)[1], rv, hv, pv)
            dr, dh, dp = vjp(dov)
        gi = hi = 0
        pieces = []
        for kk, r in enumerate(rows):
            d = dr[kk]
            if r.hb:
                carried = hgrad_refs[hi][...]
                d = d + (carried if tr == r.hb else
                         jnp.concatenate([jnp.zeros((tr - r.hb, r.w), F32), carried], axis=0))
                hgrad_refs[hi][...] = dh[hi]
                hi += 1
            if r.grad is True:
                drow_refs[gi][...] = d.astype(drow_refs[gi].dtype)
                gi += 1
            elif r.grad == "cols":
                pieces.append(d.astype(BF16))
        if ncol:
            dcols_ref[...] = pieces[0] if len(pieces) == 1 else jnp.concatenate(pieces, axis=1)
        for r, d in zip(dpar_refs, dp):
            if len(r.shape) == 3:
                for g in range(r.shape[0]):
                    r[g] += d[g]
            else:
                r[...] += d

    rev = lambda i: n - 1 - i
    in_specs = [pl.BlockSpec((tr, r.w), lambda i, c=r.cb: (rev(i), c)) for r in rows]
    in_specs += [pl.BlockSpec((r.hb, r.w), lambda i, c=r.cb, q=tr // r.hb: (jnp.maximum(rev(i) * q - 1, 0), c))
                 for r in halos]
    in_specs += [_full_spec(p.shape) for p in params]
    args = [r.arr for r in rows] + [r.arr for r in halos] + list(params)
    scratch = []
    if nc:
        in_specs.append(pl.BlockSpec((1,) + carry, lambda i: (rev(i), 0, 0, 0)))
        args.append(csave)
        scratch.append(pltpu.VMEM(carry, F32))
    douts = [d if isinstance(d, Row) else Row(d, d.shape[1], 0) for d in douts]
    in_specs += [pl.BlockSpec((tr, d.w), lambda i, c=d.cb: (rev(i), c)) for d in douts]
    args += [d.arr for d in douts]
    aliases = {}
    if nalias:
        aliases = {len(args): ng}
        in_specs.append(pl.BlockSpec(memory_space=pl.ANY))
        args.append(dcols)
    scratch += [pltpu.VMEM((r.hb, r.w), F32) for r in halos]
    out_shape = [jax.ShapeDtypeStruct((T, r.w), F32) for r in grows]
    out_specs = [pl.BlockSpec((tr, r.w), lambda i: (rev(i), 0)) for r in grows]
    if ncol:
        off = crows[0].cb * crows[0].w
        assert off % wcols == 0 and all(a.cb * a.w + a.w == b.cb * b.w for a, b in zip(crows, crows[1:]))
        out_shape.append(jax.ShapeDtypeStruct((T, N_PAD), BF16))
        out_specs.append(pl.BlockSpec((tr, wcols), lambda i, c=off // wcols: (rev(i), c)))
    out_shape += [jax.ShapeDtypeStruct(p.shape, F32) for p in params]
    out_specs += [_full_spec(p.shape) for p in params]
    res = pl.pallas_call(
        body, name=name, grid=(n,), in_specs=in_specs, out_specs=out_specs, out_shape=out_shape,
        scratch_shapes=scratch, input_output_aliases=aliases,
        compiler_params=_params(dimension_semantics=("arbitrary",)),
    )(*args)
    return list(res[:ng]), list(res[ng + ncol:]), (res[ng] if ncol else dcols)


def _fill_misc(dcols, dkv, dba, tr):
    T = dkv.shape[0]

    def body(kv_ref, ba_ref, _, o_ref):
        o_ref[...] = jnp.concatenate([kv_ref[...], ba_ref[...]], axis=1).astype(BF16)

    return pl.pallas_call(
        body, name="misc_bwd", grid=(T // tr,),
        in_specs=[pl.BlockSpec((tr, 256), lambda i: (i, 0)), pl.BlockSpec((tr, 256), lambda i: (i, 0)),
                  pl.BlockSpec(memory_space=pl.ANY)],
        out_specs=pl.BlockSpec((tr, W_MISC), lambda i: (i, O_MISC // W_MISC)),
        out_shape=jax.ShapeDtypeStruct((T, N_PAD), BF16), input_output_aliases={2: 0},
        compiler_params=_params(dimension_semantics=("arbitrary",)),
    )(dkv, dba, dcols)


def _up_bwd(ys, cols, dm, w_up):
    T, tr = dm.shape[0], UPB_TR

    def body(y_ref, gl_ref, dm_ref, w_ref, dy_ref, dgl_ref, dw_ref):
        @pl.when(pl.program_id(1) == 0)
        def _():
            dw_ref[...] = jnp.zeros_like(dw_ref)

        _, vjp = jax.vjp(lambda y, gl, w: _sigmoid(gl) * _nn16(y, w),
                         y_ref[...].astype(F32), gl_ref[...].astype(F32), w_ref[...].astype(F32))
        dy, dgl, dw = vjp(dm_ref[...])
        dy_ref[...] = dy
        dgl_ref[...] = dgl.astype(BF16)
        dw_ref[...] += dw

    branch_rows = lambda w: pl.BlockSpec((tr, w), lambda n, i: (i, n))
    weight = pl.BlockSpec((None, BRANCH_W, D_MODEL), lambda n, i: (n, 0, 0))
    return pl.pallas_call(
        body, name="up_bwd", grid=(4, T // tr),
        in_specs=[branch_rows(BRANCH_W), branch_rows(D_MODEL), pl.BlockSpec((tr, D_MODEL), lambda n, i: (i, 0)), weight],
        out_specs=[branch_rows(BRANCH_W), branch_rows(D_MODEL), weight],
        out_shape=[jax.ShapeDtypeStruct((T, 4 * BRANCH_W), F32), jax.ShapeDtypeStruct((T, N_PAD), BF16),
                   jax.ShapeDtypeStruct(w_up.shape, F32)],
        compiler_params=_params(dimension_semantics=("arbitrary", "arbitrary")),
    )(ys, cols, dm, w_up)


def _matmul(name, a, b, kind, out_dtype, tm, tn, tk):
    if kind == "tn":
        (K, M), N = a.shape, b.shape[1]
    else:
        (M, K), N = a.shape, (b.shape[0] if kind == "nt" else b.shape[1])
    tm, tn, tk = min(tm, M), min(tn, N), min(tk, K)
    nk = K // tk
    dims = {"nn": _NN, "nt": _NT, "tn": _TN}[kind]

    def body(a_ref, b_ref, o_ref, acc_ref):
        k = pl.program_id(2)

        @pl.when(k == 0)
        def _():
            acc_ref[...] = jnp.zeros_like(acc_ref)

        acc_ref[...] += lax.dot_general(a_ref[...], b_ref[...], (dims, ((), ())), preferred_element_type=F32)

        @pl.when(k == nk - 1)
        def _():
            o_ref[...] = acc_ref[...].astype(o_ref.dtype)

    a_spec = pl.BlockSpec((tk, tm), lambda i, j, k: (k, i)) if kind == "tn" else pl.BlockSpec((tm, tk), lambda i, j, k: (i, k))
    b_spec = pl.BlockSpec((tn, tk), lambda i, j, k: (j, k)) if kind == "nt" else pl.BlockSpec((tk, tn), lambda i, j, k: (k, j))
    return pl.pallas_call(
        body, name=name, grid=(M // tm, N // tn, nk), in_specs=[a_spec, b_spec],
        out_specs=pl.BlockSpec((tm, tn), lambda i, j, k: (i, j)),
        out_shape=jax.ShapeDtypeStruct((M, N), out_dtype),
        scratch_shapes=[pltpu.VMEM((tm, tn), F32)],
        compiler_params=_params(dimension_semantics=("arbitrary", "arbitrary", "arbitrary")),
    )(a, b)


def _pre_fn(first, _, rows, halos, params):
    return None, [_rms(rows[0], params[0])]


def _pre_fn_res(first, _, rows, halos, params):
    return None, [_rms(rows[0], params[0]), rows[0]]


def _memkv_fn(first, _, rows, halos, params):
    g, w = params
    return None, [_nn16(_rms(rows[0], g), w)]


def _conv_silu(x, halo, w4, keep_halo):
    tr = x.shape[0]
    halo = halo * keep_halo
    rid = _iota((tr, 1), 0)
    acc = w4[3] * x
    for s in (1, 2, 3):
        hs = jnp.concatenate([_rollr(halo, s), jnp.zeros((tr - halo.shape[0], x.shape[1]), F32)], axis=0)
        acc = acc + w4[3 - s] * jnp.where(rid < s, hs, _rollr(x, s))
    return _silu(acc)


def _dn_fn(first, S, rows, halos, params):
    qp, kp, vp, z, ba = rows
    conv, a_vec, dt_vec, dnorm = params
    ba = _cols(ba, 0, LANES)
    tr = qp.shape[0]
    keep = jnp.where(first, 0.0, 1.0)
    q = _conv_silu(qp, halos[0], [conv[3 * j + 0] for j in range(4)], keep)
    k = _conv_silu(kp, halos[1], [conv[3 * j + 1] for j in range(4)], keep)
    v = _conv_silu(vp, halos[2], [conv[3 * j + 2] for j in range(4)], keep)
    qh, kh, vh = [], [], []
    for h in range(4):
        a, b = h * LANES, (h + 1) * LANES
        xq, xk = _cols(q, a, b), _cols(k, a, b)
        qh.append(xq * lax.rsqrt(jnp.sum(xq * xq, axis=1, keepdims=True) + EPS) * (LANES ** -0.5))
        kh.append(xk * lax.rsqrt(jnp.sum(xk * xk, axis=1, keepdims=True) + EPS))
        vh.append(_cols(v, a, b))
    beta_all = _sigmoid(ba)
    g_all = -jnp.exp(a_vec) * _softplus(ba + dt_vec)
    C = DN_CHUNK
    ii, jj = _iota((C, C), 0), _iota((C, C), 1)
    strict, incl = ii > jj, ii >= jj
    eye = (ii == jj).astype(F32)
    last_row = (_iota((C, 1), 0) == C - 1).astype(F32)
    n_chunk = tr // C
    pairs = [(c, h) for c in range(n_chunk) for h in range(4)]
    rows_of = lambda a, c: _rowsl(a, c * C, (c + 1) * C)
    gcs = [_nn32(incl.astype(F32), rows_of(g_all, c)) for c in range(n_chunk)]
    qc = {(c, h): rows_of(qh[h], c) for c, h in pairs}
    kc = {(c, h): rows_of(kh[h], c) for c, h in pairs}
    beta = {(c, h): _lane_pick(rows_of(beta_all, c), h) for c, h in pairs}
    gc = {(c, h): _lane_pick(gcs[c], 4 + h) for c, h in pairs}
    dec = {p: jnp.exp(jnp.where(incl, gc[p] - jnp.sum(eye * gc[p], axis=0, keepdims=True), 0.0)) for p in pairs}
    egc = {p: jnp.exp(gc[p]) for p in pairs}
    kb = {p: kc[p] * beta[p] for p in pairs}
    kq = {p: _nt16(jnp.concatenate([kb[p], qc[p]], axis=0), kc[p]) for p in pairs}
    P = {p: -jnp.where(strict, _rowsl(kq[p], 0, C) * dec[p], 0.0) for p in pairs}
    aqk = {p: jnp.where(incl, _rowsl(kq[p], C, 2 * C) * dec[p], 0.0) for p in pairs}
    tinv = {p: eye + P[p] for p in pairs}
    P = {p: _nn16(P[p], P[p]) for p in pairs}
    for j in range(5):
        if j < 4:
            pt = {p: _nn16(jnp.concatenate([P[p], tinv[p]], axis=0), P[p]) for p in pairs}
            tinv = {p: tinv[p] + _rowsl(pt[p], C, 2 * C) for p in pairs}
            P = {p: _rowsl(pt[p], 0, C) for p in pairs}
        else:
            tinv = {p: tinv[p] + _nn16(tinv[p], P[p]) for p in pairs}
    uw = {(c, h): _nn16(tinv[c, h], jnp.concatenate([rows_of(vh[h], c) * beta[c, h], kb[c, h] * egc[c, h]], axis=1))
          for c, h in pairs}
    S = list(S)
    ychunks = []
    for c in range(n_chunk):
        zc = rows_of(z, c)
        hs = range(4)
        ws = [_nn16(jnp.concatenate([_cols(uw[c, h], LANES, 2 * LANES), qc[c, h] * egc[c, h]], axis=0), S[h]) for h in hs]
        vnew = [_cols(uw[c, h], 0, LANES) - _rowsl(ws[h], 0, C) for h in hs]
        o = [_rowsl(ws[h], C, 2 * C) + _nn16(aqk[c, h], vnew[h]) for h in hs]
        glast = [jnp.sum(gc[c, h] * last_row, axis=0, keepdims=True) for h in hs]
        S = [S[h] * jnp.exp(glast[h]) + _tn16(kc[c, h] * jnp.exp(glast[h] - gc[c, h]), vnew[h]) for h in hs]
        ychunks.append(jnp.concatenate(
            [_rms(o[h], dnorm) * _silu(_cols(zc, h * LANES, (h + 1) * LANES)) for h in hs], axis=1))
    return S, [jnp.concatenate(ychunks, axis=0)]


def _gm_fn(first, _, rows, halos, params):
    uv, z = rows
    gnorm, ws, bs = params
    tr = uv.shape[0]
    guv = _gelu(uv)
    u = _cols(guv, 0, BRANCH_W)
    v = _rms(_cols(guv, BRANCH_W, 2 * BRANCH_W), gnorm)
    ii, jj = _iota((LANES, LANES), 0), _iota((LANES, LANES), 1)
    eye = (ii == jj).astype(F32)
    wsm = [jnp.where(ii >= jj, ws[g], 0.0) for g in range(4)]
    bcol = [jnp.sum(eye * bs[g], axis=1, keepdims=True) for g in range(4)]
    chunks = []
    for c in range(tr // LANES):
        vc = _rowsl(v, c * LANES, (c + 1) * LANES)
        chunks.append(jnp.concatenate(
            [_nn16(wsm[g], _cols(vc, g * LANES, (g + 1) * LANES)) + bcol[g] for g in range(4)], axis=1))
    return None, [u * jnp.concatenate(chunks, axis=0) * _silu(z)]


def _swa_fn(first, _, rows, halos, params):
    q, kvc, z = rows
    sink_vec = params[0]
    P = LANES
    kv = jnp.concatenate([halos[0], kvc], axis=0)
    k, v = _cols(kv, 0, P), _cols(kv, P, 2 * P)
    r, cc = _iota((P, P), 0), _iota((P, P), 1)
    lane = _iota((1, P), 1)
    dist = _iota((P, 2 * P), 0) + P - _iota((P, 2 * P), 1)
    kmin = jnp.where(first, P, 0)
    valid = (dist >= 0) & (dist < P) & (_iota((P, 2 * P), 1) >= kmin)
    blocks = [None] * 4
    for kh in range(2):
        dup = (r == kh * 64 + (cc & 63)).astype(F32)
        kk, vv = _nn16(k, dup), _nn16(v, dup)
        for g in range(4):
            h = kh * 4 + g
            half = ((lane >= 64) == (h % 2 == 1)).astype(F32)
            qb = _cols(q, (h // 2) * P, (h // 2 + 1) * P) * half
            s = jnp.where(valid, _nt16(qb, kk) * 0.125, NEG_INF)
            sink = _lane_pick(sink_vec, h)
            m = lax.stop_gradient(jnp.maximum(jnp.max(s, axis=1, keepdims=True), sink))
            e = jnp.exp(s - m)
            p = e / (jnp.sum(e, axis=1, keepdims=True) + jnp.exp(sink - m))
            o = _nn16(p, vv) * half
            blocks[h // 2] = o if blocks[h // 2] is None else blocks[h // 2] + o
    return None, [jnp.concatenate(blocks, axis=1) * _silu(z)]


def _mem_fn(first, _, rows, halos, params):
    q, z = rows
    mkv = params[0]
    outs = []
    for h in range(4):
        a, b = h * LANES, (h + 1) * LANES
        s = _nt16(_cols(q, a, b), _cols(mkv, a, b)) * (LANES ** -0.5)
        m = lax.stop_gradient(jnp.max(s, axis=1, keepdims=True))
        e = jnp.exp(s - m)
        p = e / jnp.sum(e, axis=1, keepdims=True)
        outs.append(_nn16(p, _cols(mkv, BRANCH_W + a, BRANCH_W + b)))
    return None, [jnp.concatenate(outs, axis=1) * _silu(z)]


def _up_fn(first, _, rows, halos, params):
    ys, gl, w_up = rows[:4], rows[4], params[0]
    merged = None
    for n in range(4):
        term = _sigmoid(_cols(gl, n * D_MODEL, (n + 1) * D_MODEL)) * _nn16(ys[n], w_up[n])
        merged = term if merged is None else merged + term
    return None, [merged]


def _out_fn(first, _, rows, halos, params):
    x, merged = rows
    w, g = params
    return None, [x + _rms(_nn16(merged, w), g)]


def _loss_fn(first, _, rows, halos, params):
    y, t = rows
    d = y - t
    lrow = 0.5 * jnp.mean(d * d, axis=1, keepdims=True)
    return None, [d * (1.0 / D_MODEL), jnp.broadcast_to(lrow, (y.shape[0], LANES))]


TR = 256
DN_TR = 256
UP_TR = 256
UPB_TR = 512
CONV_HALO = 16
CARRY = (4, LANES, LANES)


def _branch_rows(cols, g):
    hb = CONV_HALO
    a = [Row(cols, 512, O_AQ // 512, hb, g), Row(cols, 512, O_AK // 512, hb, g), Row(cols, 512, O_AV // 512, hb, g),
         Row(cols, 512, O_AZ // 512, 0, g), Row(cols, 256, O_BA // 256)]
    b = [Row(cols, 1024, O_BUV // 1024, 0, g), Row(cols, 512, O_BZ // 512, 0, g)]
    c = [Row(cols, 512, O_CQ // 512, 0, g), Row(cols, 256, O_CKV // 256, LANES), Row(cols, 512, O_CZ // 512, 0, g)]
    m = [Row(cols, 512, O_MQ // 512, 0, g), Row(cols, 512, O_MZ // 512, 0, g)]
    return a, b, c, m


def _layer_fwd(x, mem, W):
    h = _rows_fwd("prenorm_fwd", _pre_fn, [Row(x, D_MODEL, 0)], [W["norm_pre"]], [(D_MODEL, BF16)], TR)[0]
    cols = _matmul("in_proj_fwd", h, W["w_pad"], "nt", BF16, 1024, 1024, 1024)
    mem_kv = _rows_fwd("memkv_fwd", _memkv_fn, [Row(mem, D_MODEL, 0)], [W["norm_mem"], W["w_mem_kv"]],
                       [(D_MODEL, F32)], MEM_LEN)[0]
    ra, rb, rc, rm = _branch_rows(cols, True)
    y_a, csave = _rows_fwd("dn_fwd", _dn_fn, ra, [W["conv"], W["a_vec"], W["dt_vec"], W["dn_norm"]],
                           [(BRANCH_W, BF16)], DN_TR, CARRY)
    y_b = _rows_fwd("gm_fwd", _gm_fn, rb, [W["gm_norm"], W["spatial_w"], W["spatial_b"]], [(BRANCH_W, BF16)], TR)[0]
    y_c = _rows_fwd("swa_fwd", _swa_fn, rc, [W["sink_vec"]], [(BRANCH_W, BF16)], LANES)[0]
    y_m = _rows_fwd("mem_fwd", _mem_fn, rm, [mem_kv], [(BRANCH_W, BF16)], TR)[0]
    ys = [y_a, y_b, y_c, y_m]
    merged = _rows_fwd("up_fwd", _up_fn, [Row(y, BRANCH_W, 0) for y in ys] + [Row(cols, 4 * D_MODEL, 0)],
                       [W["w_up"]], [(D_MODEL, BF16)], UP_TR)[0]
    x_new = _rows_fwd("out_fwd", _out_fn, [Row(x, D_MODEL, 0), Row(merged, D_MODEL, 0)],
                      [W["w_out"], W["norm_post"]], [(D_MODEL, F32)], TR)[0]
    return x_new, dict(x=x, h=h, cols=cols, mem_kv=mem_kv, csave=csave, ys=ys, merged=merged)


def _layer_bwd(dxn, mem, W, sv):
    x, cols = sv["x"], sv["cols"]
    (dx_res, dm), (dw_out, dnorm_post), _ = _rows_bwd(
        "out_bwd", _out_fn, [Row(x, D_MODEL, 0), Row(sv["merged"], D_MODEL, 0)], [W["w_out"], W["norm_post"]],
        [dxn], TR)
    dys, dcols, dw_up = _up_bwd(jnp.concatenate(sv["ys"], axis=1), cols, dm, W["w_up"])
    dys = [Row(dys, BRANCH_W, n) for n in range(4)]
    ra, rb, rc, rm = _branch_rows(cols, "cols")
    (dba,), (dconv, da_vec, ddt_vec, ddn_norm), dcols = _rows_bwd(
        "dn_bwd", _dn_fn, ra, [W["conv"], W["a_vec"], W["dt_vec"], W["dn_norm"]], [dys[0]], DN_TR, CARRY,
        sv["csave"], dcols=dcols)
    _, (dgm_norm, dws, dbs), dcols = _rows_bwd(
        "gm_bwd", _gm_fn, rb, [W["gm_norm"], W["spatial_w"], W["spatial_b"]], [dys[1]], TR, dcols=dcols)
    (dkv_c,), (dsink,), dcols = _rows_bwd("swa_bwd", _swa_fn, rc, [W["sink_vec"]], [dys[2]], LANES, dcols=dcols)
    _, (dmem_kv,), dcols = _rows_bwd("mem_bwd", _mem_fn, rm, [sv["mem_kv"]], [dys[3]], TR, dcols=dcols)
    dcols = _fill_misc(dcols, dkv_c, dba, TR)
    _, (dnorm_mem, dw_mem_kv), _ = _rows_bwd("memkv_bwd", _memkv_fn, [Row(mem, D_MODEL, 0, 0, False)],
                                             [W["norm_mem"], W["w_mem_kv"]], [dmem_kv], MEM_LEN)
    dw_pad = _matmul("in_proj_dw", dcols, sv["h"], "tn", F32, 1024, 1024, 1024)
    dh = _matmul("in_proj_dx", dcols, W["w_pad"], "nn", F32, 1024, 1024, 1024)
    (dx,), (dnorm_pre,), _ = _rows_bwd("prenorm_bwd", _pre_fn_res, [Row(x, D_MODEL, 0)], [W["norm_pre"]],
                                       [dh, dx_res], TR)
    grads = dict(norm_pre=dnorm_pre, norm_post=dnorm_post, norm_mem=dnorm_mem, w_pad=dw_pad, conv=dconv,
                 a_vec=da_vec, dt_vec=ddt_vec, dn_norm=ddn_norm, gm_norm=dgm_norm, spatial_w=dws, spatial_b=dbs,
                 sink_vec=dsink, w_mem_kv=dw_mem_kv, w_up=dw_up, w_out=dw_out)
    return dx, grads


def _lane_vec(v, off):
    return jnp.zeros((1, LANES), F32).at[0, off:off + v.shape[0]].set(v)


def _layer_weights(l, w_pad, conv_w, w_mem_kv, w_up, w_out, small):
    return dict(
        w_pad=w_pad, conv=conv_w.reshape(4, 3, BRANCH_W).reshape(12, 1, BRANCH_W),
        w_mem_kv=w_mem_kv, w_up=w_up, w_out=w_out,
        norm_pre=small["norm_pre"][l][None], norm_post=small["norm_post"][l][None],
        norm_mem=small["norm_mem"][l][None],
        a_vec=_lane_vec(small["a_log"][l], 4), dt_vec=_lane_vec(small["dt_bias"][l], 4),
        dn_norm=small["dn_norm"][l][None], gm_norm=small["gm_norm"][l][None],
        spatial_w=small["spatial_w"][l], spatial_b=small["spatial_b"][l][:, None, :],
        sink_vec=_lane_vec(small["sinks"][l], 0))


_MESH = pl.DeviceIdType.MESH
_ANY = pl.BlockSpec(memory_space=pl.ANY)


def _position():
    return lax.axis_index("x"), lax.axis_index("y"), lax.axis_index("c")


def _remote(src, dst, send_sem, recv_sem, dev):
    return pltpu.make_async_remote_copy(src_ref=src, dst_ref=dst, send_sem=send_sem, recv_sem=recv_sem,
                                        device_id=dev, device_id_type=_MESH)


def _hbm_call(name, body, arrs, out_shapes, sems, aliases=None):
    return pl.pallas_call(
        body, name=name, in_specs=[_ANY] * len(arrs), out_specs=[_ANY] * len(out_shapes), out_shape=out_shapes,
        scratch_shapes=[pltpu.SemaphoreType.DMA((k,)) for k in sems], input_output_aliases=aliases or {},
        compiler_params=pltpu.CompilerParams(has_side_effects=True),
    )(*arrs)


def _other_chips(x, y):
    return [(1 - x, y), (x, 1 - y), (1 - x, 1 - y)]


def _gather_weights(arrs):
    n = len(arrs)

    def body(*refs):
        ins, outs = refs[:n], refs[n:2 * n]
        ici_send, ici_recv, d2d_send, d2d_recv = refs[2 * n:]
        x, y, c = _position()
        me = 2 * x + y
        chips = _other_chips(x, y)
        sends = []
        for a in range(n):
            for j, (px, py) in enumerate(chips):
                sends.append(_remote(ins[a].at[c], outs[a].at[c, me], ici_send.at[3 * a + j], ici_recv.at[3 * a + j],
                                     (px, py, c)))
                sends[-1].start()
        for a in range(n):
            for j, (px, py) in enumerate(chips):
                slab = outs[a].at[c, 2 * px + py]
                _remote(ins[a].at[c], slab, ici_send.at[3 * a + j], ici_recv.at[3 * a + j], (px, py, c)).wait_recv()
                sends.append(_remote(slab, slab, d2d_send.at[3 * a + j], d2d_recv.at[3 * a + j], (x, y, 1 - c)))
                sends[-1].start()
        for a in range(n):
            for j, (px, py) in enumerate(chips):
                slab = outs[a].at[1 - c, 2 * px + py]
                _remote(slab, slab, d2d_send.at[3 * a + j], d2d_recv.at[3 * a + j], (x, y, 1 - c)).wait_recv()
        for cp in sends:
            cp.wait_send()

    return _hbm_call("gather_weights", body, arrs,
                     [jax.ShapeDtypeStruct((N_LAYER, N_CHIP) + a.shape[1:], a.dtype) for a in arrs], [3 * n] * 4)


def _pair_exchange(arrs):
    n = len(arrs)

    def body(*refs):
        ins, outs = refs[:n], refs[n:2 * n]
        send_sems, recv_sems = refs[2 * n:]
        x, y, c = _position()
        cps = [_remote(ins[a].at[1 - c], outs[a], send_sems.at[a], recv_sems.at[a], (x, y, 1 - c)) for a in range(n)]
        for cp in cps:
            cp.start()
        for cp in cps:
            cp.wait_recv()
        for cp in cps:
            cp.wait_send()

    return _hbm_call("pair_exchange", body, arrs, [jax.ShapeDtypeStruct(a.shape[1:], a.dtype) for a in arrs], [n, n])


def _chip_scatter(arrs):
    n = len(arrs)

    def body(*refs):
        ins, outs = refs[:n], refs[n:2 * n]
        send_sems, recv_sems = refs[2 * n:]
        x, y, c = _position()
        me = 2 * x + y
        sends = []
        for a in range(n):
            for j, (px, py) in enumerate(_other_chips(x, y)):
                sends.append(_remote(ins[a].at[2 * px + py], outs[a].at[me], send_sems.at[3 * a + j],
                                     recv_sems.at[3 * a + j], (px, py, c)))
                sends[-1].start()
        for a in range(n):
            for j, (px, py) in enumerate(_other_chips(x, y)):
                _remote(ins[a].at[me], outs[a].at[2 * px + py], send_sems.at[3 * a + j], recv_sems.at[3 * a + j],
                        (px, py, c)).wait_recv()
        for cp in sends:
            cp.wait_send()

    return _hbm_call("chip_scatter", body, arrs, [jax.ShapeDtypeStruct(a.shape, a.dtype) for a in arrs],
                     [3 * n, 3 * n])


def _pair_share(arrs):
    n = len(arrs)

    def body(*refs):
        ins, outs = refs[:n], refs[n:2 * n]
        send_sems, recv_sems = refs[2 * n:]
        x, y, c = _position()
        cps = [_remote(ins[a].at[c], outs[a].at[c], send_sems.at[a], recv_sems.at[a], (x, y, 1 - c)) for a in range(n)]
        for cp in cps:
            cp.start()
        for a in range(n):
            _remote(ins[a].at[c], outs[a].at[1 - c], send_sems.at[a], recv_sems.at[a], (x, y, 1 - c)).wait_recv()
        for cp in cps:
            cp.wait_send()

    return _hbm_call("pair_share", body, arrs, [jax.ShapeDtypeStruct(a.shape, a.dtype) for a in arrs], [n, n],
                     {a: a for a in range(n)})


def _pair_forward(arrs):
    n = len(arrs)

    def body(*refs):
        ins, outs = refs[:n], refs[n:2 * n]
        send_sems, recv_sems = refs[2 * n:]
        x, y, c = _position()
        sends = []
        for a in range(n):
            for j, (px, py) in enumerate(_other_chips(x, y)):
                sends.append(_remote(ins[a].at[c, 2 * px + py], outs[a].at[c, 2 * px + py], send_sems.at[3 * a + j],
                                     recv_sems.at[3 * a + j], (x, y, 1 - c)))
                sends[-1].start()
        for a in range(n):
            for j, (px, py) in enumerate(_other_chips(x, y)):
                slab = outs[a].at[1 - c, 2 * px + py]
                _remote(slab, slab, send_sems.at[3 * a + j], recv_sems.at[3 * a + j], (x, y, 1 - c)).wait_recv()
        for cp in sends:
            cp.wait_send()

    return _hbm_call("pair_forward", body, arrs, [jax.ShapeDtypeStruct(a.shape, a.dtype) for a in arrs],
                     [3 * n, 3 * n], {a: a for a in range(n)})


_HBM = pl.BlockSpec(memory_space=pltpu.HBM)
_SEM = pl.BlockSpec(memory_space=pltpu.SEMAPHORE)
_EFFECT = pltpu.SideEffectType.DATAFLOW_SIDE_EFFECTING


def _chip_copies(kind, srcs, lands, send_sems, recv_sems):
    x, y, c = _position()
    me = 2 * x + y
    sends, recvs = [], []
    for a in range(len(srcs)):
        for j, (px, py) in enumerate(_other_chips(x, y)):
            s, sems, dev = 2 * px + py, (send_sems.at[3 * a + j], recv_sems.at[3 * a + j]), (px, py, c)
            if kind == "gather":
                sends.append(_remote(srcs[a].at[c], lands[a].at[c, me], *sems, dev))
                recvs.append(_remote(srcs[a].at[c], lands[a].at[c, s], *sems, dev))
            else:
                sends.append(_remote(srcs[a].at[s], lands[a].at[me], *sems, dev))
                recvs.append(_remote(srcs[a].at[me], lands[a].at[s], *sems, dev))
    return sends, recvs


def _split_start(name, kind, srcs, land_shapes, after):
    n = len(srcs)

    def body(*refs):
        sends, _ = _chip_copies(kind, refs[:n], refs[n:2 * n], refs[2 * n + 1], refs[2 * n + 2])
        for cp in sends:
            cp.start()
        refs[-1][...] = jnp.zeros_like(refs[-1])

    hbm = lambda a: pltpu.with_memory_space_constraint(a, pltpu.HBM)
    lands = [lax.empty(s.shape, s.dtype) for s in land_shapes]
    outs = pl.pallas_call(
        body, name=name, in_specs=[_HBM] * (2 * n) + [_ANY],
        out_specs=[_SEM, _SEM] + [_HBM] * (2 * n) + [pl.BlockSpec(memory_space=pltpu.VMEM)],
        out_shape=[pltpu.SemaphoreType.DMA((3 * n,)), pltpu.SemaphoreType.DMA((3 * n,))]
        + [pltpu.HBM(a.shape, a.dtype) for a in list(srcs) + lands] + [jax.ShapeDtypeStruct((8, LANES), F32)],
        input_output_aliases={i: 2 + i for i in range(2 * n)},
        compiler_params=pltpu.CompilerParams(has_side_effects=_EFFECT),
    )(*[hbm(a) for a in srcs], *[hbm(a) for a in lands], after)
    return outs[0], outs[1], list(outs[2:2 + 2 * n]), outs[-1]


def _split_wait(name, kind, started, after):
    send_sems, recv_sems, thru, _ = started
    n = len(thru) // 2

    def body(*refs):
        sends, recvs = _chip_copies(kind, refs[:n], refs[n:2 * n], refs[2 * n], refs[2 * n + 1])
        for cp in sends:
            cp.wait_send()
        for cp in recvs:
            cp.wait_recv()

    outs = pl.pallas_call(
        body, name=name, in_specs=[_HBM] * (2 * n) + [_SEM, _SEM, _ANY], out_specs=[_HBM] * (2 * n),
        out_shape=[pltpu.HBM(a.shape, a.dtype) for a in thru], input_output_aliases={i: i for i in range(2 * n)},
        compiler_params=pltpu.CompilerParams(has_side_effects=_EFFECT),
    )(*thru, send_sems, recv_sems, after)
    return list(outs[:n]), list(outs[n:])


def _allreduce_small(g):
    def body(g_ref, o_ref, pair_buf, chip_buf, send_sems, recv_sems):
        x, y, c = _position()
        me = 2 * x + y
        sib = (x, y, 1 - c)
        to_sib = _remote(g_ref.at[1 - c], pair_buf, send_sems.at[0], recv_sems.at[0], sib)
        to_sib.start()
        to_sib.wait_recv()
        chip_buf[me] = g_ref[c] + pair_buf[...]
        sends = [to_sib]
        chips = _other_chips(x, y)
        for j, (px, py) in enumerate(chips):
            sends.append(_remote(chip_buf.at[me], chip_buf.at[me], send_sems.at[1 + j], recv_sems.at[1 + j], (px, py, c)))
            sends[-1].start()
        for j, (px, py) in enumerate(chips):
            _remote(chip_buf.at[me], chip_buf.at[2 * px + py], send_sems.at[1 + j], recv_sems.at[1 + j],
                    (px, py, c)).wait_recv()
        o_ref[c] = ((chip_buf[0] + chip_buf[1]) + chip_buf[2]) + chip_buf[3]
        sends.append(_remote(o_ref.at[c], o_ref.at[c], send_sems.at[4], recv_sems.at[4], sib))
        sends[-1].start()
        _remote(o_ref.at[c], o_ref.at[1 - c], send_sems.at[4], recv_sems.at[4], sib).wait_recv()
        for cp in sends:
            cp.wait_send()

    vmem = pl.BlockSpec(memory_space=pltpu.VMEM)
    return pl.pallas_call(
        body, name="allreduce_small", in_specs=[vmem], out_specs=vmem, out_shape=jax.ShapeDtypeStruct(g.shape, F32),
        scratch_shapes=[pltpu.VMEM(g.shape[1:], F32), pltpu.VMEM((N_CHIP,) + g.shape[1:], F32),
                        pltpu.SemaphoreType.DMA((5,)), pltpu.SemaphoreType.DMA((5,))],
        compiler_params=_params(),
    )(g)


EW_ROWS = 512


def _ew(name, fn, ins, n_out, out_dtype=F32, out_slot=None, into=None):
    def dims(a):
        return a[0].shape[1:] if isinstance(a, tuple) else a.shape

    R, w = dims(ins[0])
    tr = EW_ROWS if R % EW_ROWS == 0 else R
    n_into = len(into) if into else 0

    def body(c_ref, *refs):
        outs = fn(*[r[...] for r in refs[:len(ins)]])
        for r, v in zip(refs[len(ins) + n_into:], outs):
            r[...] = v.astype(r.dtype)

    def lead_spec(l):
        if l == "c":
            return pl.BlockSpec((None, tr, w), lambda i, c_ref: (c_ref[0], i, 0))
        return pl.BlockSpec((None, tr, w), lambda i, c_ref, s=l: (s, i, 0))

    plain = pl.BlockSpec((tr, w), lambda i, c_ref: (i, 0))
    in_specs = [lead_spec(a[1]) if isinstance(a, tuple) else plain for a in ins] + [_ANY] * n_into
    out_spec = plain if out_slot is None else lead_spec(out_slot)
    out_shape = jax.ShapeDtypeStruct((R, w) if out_slot is None else (2, R, w), out_dtype)
    return pl.pallas_call(
        body, name=name,
        grid_spec=pltpu.PrefetchScalarGridSpec(num_scalar_prefetch=1, grid=(R // tr,), in_specs=in_specs,
                                               out_specs=[out_spec] * n_out),
        out_shape=[out_shape] * n_out, input_output_aliases={1 + len(ins) + j: j for j in range(n_into)},
        compiler_params=_params(dimension_semantics=("arbitrary",)),
    )(lax.axis_index("c").astype(jnp.int32).reshape(1), *[a[0] if isinstance(a, tuple) else a for a in ins],
      *(into or []))


def _adamw_fn(w, g, m, v):
    m = ADAM_B1 * m + (1.0 - ADAM_B1) * g
    v = ADAM_B2 * v + (1.0 - ADAM_B2) * (g * g)
    m_hat = m / (1.0 - ADAM_B1 ** ADAM_STEP)
    v_hat = v / (1.0 - ADAM_B2 ** ADAM_STEP)
    delta = -ADAM_LR * (m_hat / (jnp.sqrt(v_hat) + ADAM_EPS) + ADAM_WD * w)
    return delta, m, v


def _adamw(name, w, g, m, v):
    shape = w.shape
    two = lambda a: a.reshape(-1, shape[-1])
    return [o.reshape(shape) for o in _ew(name, _adamw_fn, [two(w), two(g), two(m), two(v)], 3)]


def _adamw_layer(name, l, w, g, m, v, into):
    k = w.shape[-1]
    three = lambda a: (a.reshape(N_LAYER, -1, k), l)
    fn = lambda w_, g_, m_, v_: _adamw_fn(w_, g_, m_, v_) + (g_,)
    outs = _ew(name, fn, [three(w), g.reshape(-1, k), three(m), three(v)], 4, out_slot=l,
               into=None if into is None else [a.reshape(N_LAYER, -1, k) for a in into])
    return [o.reshape(w.shape) for o in outs]


def _adamw_rows(name, l, w, g, m, v, into):
    _, R, k = w.shape
    n_into = len(into) if into else 0

    def body(*refs):
        w_ref, g_ref, m_ref, v_ref = refs[:4]
        d_out, m_out, v_out, g_out = refs[4 + n_into:]
        g_blk = g_ref[...]
        d_out[...], m_out[...], v_out[...] = _adamw_fn(w_ref[...], g_blk, m_ref[...], v_ref[...])
        g_out[...] = g_blk

    spec = pl.BlockSpec((None, EW_ROWS, k), lambda i: (l, i, 0))
    return pl.pallas_call(
        body, name=name, grid=(-(-R // EW_ROWS),),
        in_specs=[spec, pl.BlockSpec((EW_ROWS, k), lambda i: (i, 0)), spec, spec] + [_ANY] * n_into,
        out_specs=[spec] * 4, out_shape=[jax.ShapeDtypeStruct((N_LAYER, R, k), F32)] * 4,
        input_output_aliases={4 + j: j for j in range(n_into)},
        compiler_params=_params(dimension_semantics=("arbitrary",)),
    )(w, g, m, v, *(into or []))


_SMALL = [("norm_pre", (2, 1024)), ("norm_post", (2, 1024)), ("norm_mem", (2, 1024)), ("a_log", (2, 4)),
          ("dt_bias", (2, 4)), ("dn_norm", (2, 128)), ("gm_norm", (2, 512)), ("spatial_w", (2, 4, 128, 128)),
          ("spatial_b", (2, 4, 128)), ("sinks", (2, 8))]
_SMALL_ROWS = 200
_BIG = ["w_in", "conv_w", "w_mem_kv", "w_up", "w_out"]
_NAMES = ["norm_pre", "norm_post", "norm_mem", "w_in", "conv_w", "a_log", "dt_bias", "dn_norm", "gm_norm",
          "spatial_w", "spatial_b", "sinks", "w_mem_kv", "w_up", "w_out"]


def _size(shape):
    n = 1
    for s in shape:
        n *= s
    return n


_PACK_UNIT = 8 * 1024


def _pack_small(d):
    rows = []
    for n, shp in _SMALL:
        flat = d[n].reshape(-1)
        rows.append(jnp.pad(flat, (0, -flat.shape[0] % _PACK_UNIT)).reshape(-1, 1024))
    assert sum(r.shape[0] for r in rows) == _SMALL_ROWS
    return jnp.concatenate(rows, axis=0)


def _unpack_small(p):
    out, off = {}, 0
    for n, shp in _SMALL:
        k = -(-_size(shp) // _PACK_UNIT) * 8
        out[n] = p[off:off + k].reshape(-1)[:_size(shp)].reshape(shp)
        off += k
    return out


_HALF_SHAPE = {"w_in": (SHARD_PAD // 2, D_MODEL), "conv_w": (2, 3 * BRANCH_W // N_CHIP), "w_mem_kv": (128, D_MODEL),
               "w_up": (2, BRANCH_W, D_MODEL // N_CHIP), "w_out": (128, D_MODEL)}


def _chip_major(g):
    g = jnp.swapaxes(g, 0, 1)
    return g.reshape((N_CHIP, 2 * g.shape[2]) + g.shape[3:])


def _half_major(g):
    g = g.reshape((N_CHIP, 2, g.shape[1] // 2) + g.shape[2:])
    return jnp.swapaxes(g, 0, 1).astype(BF16)


def _weight_views(l, gathered, small):
    g_in, g_conv, g_kv, g_up, g_out = gathered
    return _layer_weights(
        l, _w_pad_from_slabs(g_in), _chip_major(g_conv).transpose(1, 0, 2).reshape(4, 3 * BRANCH_W),
        _chip_major(g_kv).reshape(D_MODEL, D_MODEL),
        _chip_major(g_up).transpose(1, 2, 0, 3).reshape(4, BRANCH_W, D_MODEL),
        _chip_major(g_out).reshape(D_MODEL, D_MODEL), small)


def _pair_sums(g):
    big = [_slabs_from_pad(g["w_pad"]),
           _half_major(g["conv"].reshape(4, N_CHIP, 3 * BRANCH_W // N_CHIP).transpose(1, 0, 2)),
           _half_major(g["w_mem_kv"].reshape(N_CHIP, D_MODEL // N_CHIP, D_MODEL)),
           _half_major(g["w_up"].reshape(4, BRANCH_W, N_CHIP, D_MODEL // N_CHIP).transpose(2, 0, 1, 3)),
           _half_major(g["w_out"].reshape(N_CHIP, D_MODEL // N_CHIP, D_MODEL))]
    add2 = lambda a, b: [a.astype(F32) + b.astype(F32)]
    pair = []
    for n, b, p in zip(_BIG, big, _pair_exchange(big)):
        k = b.shape[-1]
        pair.append(_ew("pair_sum_" + n, add2, [(b.reshape(2, -1, k), "c"), p.reshape(-1, k)], 1, BF16)[0]
                    .reshape(p.shape))
    return pair


def _chip_sums(landed, pair, me):
    add4 = lambda a, b, c_, d: [((a.astype(F32) + b.astype(F32)) + c_.astype(F32)) + d.astype(F32)]
    totals = []
    for n, r, q in zip(_BIG, landed, pair):
        r = _own_slot(r, lax.dynamic_index_in_dim(q, me, 0), me, 0)
        k = r.shape[-1]
        totals.append(_ew("chip_sum_" + n, add4, [(r.reshape(N_CHIP, -1, k), s) for s in range(N_CHIP)], 1,
                          out_slot="c")[0].reshape((2,) + r.shape[1:]))
    return totals


def _own_slot(buf, mine, me, axis):
    return lax.dynamic_update_index_in_dim(buf, mine.astype(buf.dtype), me, axis)


def kernel(x, mem, norm_pre, norm_post, norm_mem, w_in, conv_w, a_log, dt_bias, dn_norm, gm_norm, spatial_w, spatial_b, sinks, w_mem_kv, w_up, w_out, loss_target, m_norm_pre, m_norm_post, m_norm_mem, m_w_in, m_conv_w, m_a_log, m_dt_bias, m_dn_norm, m_gm_norm, m_spatial_w, m_spatial_b, m_sinks, m_w_mem_kv, m_w_up, m_w_out, v_norm_pre, v_norm_post, v_norm_mem, v_w_in, v_conv_w, v_a_log, v_dt_bias, v_dn_norm, v_gm_norm, v_spatial_w, v_spatial_b, v_sinks, v_w_mem_kv, v_w_up, v_w_out):
    w = dict(norm_pre=norm_pre, norm_post=norm_post, norm_mem=norm_mem, w_in=w_in, conv_w=conv_w, a_log=a_log,
             dt_bias=dt_bias, dn_norm=dn_norm, gm_norm=gm_norm, spatial_w=spatial_w, spatial_b=spatial_b, sinks=sinks,
             w_mem_kv=w_mem_kv, w_up=w_up, w_out=w_out)
    m = dict(norm_pre=m_norm_pre, norm_post=m_norm_post, norm_mem=m_norm_mem, w_in=m_w_in, conv_w=m_conv_w,
             a_log=m_a_log, dt_bias=m_dt_bias, dn_norm=m_dn_norm, gm_norm=m_gm_norm, spatial_w=m_spatial_w,
             spatial_b=m_spatial_b, sinks=m_sinks, w_mem_kv=m_w_mem_kv, w_up=m_w_up, w_out=m_w_out)
    v = dict(norm_pre=v_norm_pre, norm_post=v_norm_post, norm_mem=v_norm_mem, w_in=v_w_in, conv_w=v_conv_w,
             a_log=v_a_log, dt_bias=v_dt_bias, dn_norm=v_dn_norm, gm_norm=v_gm_norm, spatial_w=v_spatial_w,
             spatial_b=v_spatial_b, sinks=v_sinks, w_mem_kv=v_w_mem_kv, w_up=v_w_up, w_out=v_w_out)
    me = 2 * lax.axis_index("x") + lax.axis_index("y")

    w_in_t = jnp.pad(w_in.astype(BF16).transpose(0, 2, 1), ((0, 0), (0, SHARD_PAD - SHARD_IN), (0, 0)))
    local = dict(w_in=w_in_t, conv_w=conv_w, w_mem_kv=w_mem_kv.astype(BF16), w_up=w_up.astype(BF16),
                 w_out=w_out.astype(BF16))
    halves = lambda l: [local[n][l].reshape((2,) + _HALF_SHAPE[n]) for n in _BIG]
    own = lambda gathered, mine: [_own_slot(g, h[:, None], me, 1) for g, h in zip(gathered, mine)]
    g0 = own(_gather_weights(halves(0)), halves(0))
    started = _split_start("gather_l1_start", "gather", halves(1),
                           [jax.ShapeDtypeStruct((2, N_CHIP) + _HALF_SHAPE[n], local[n].dtype) for n in _BIG], g0[1])

    xl, meml = x[0], mem[0]
    W0 = _weight_views(0, g0, w)
    W0["norm_pre"] = W0["norm_pre"] + started[3][0, 0]
    x1, sv0 = _layer_fwd(xl, meml, W0)
    mine1, landed1 = _split_wait("gather_l1_wait", "gather", started, x1)
    W1 = _weight_views(1, own(_pair_forward(landed1), mine1), w)
    x2, sv1 = _layer_fwd(x1, meml, W1)
    dy, lrows = _rows_fwd("loss", _loss_fn, [Row(x2, D_MODEL, 0), Row(loss_target[0], D_MODEL, 0)], [],
                          [(D_MODEL, F32), (LANES, F32)], TR)
    loss = lax.psum(jnp.sum(lrows[:, 0]), ("x", "y", "c"))

    dx1, grads1 = _layer_bwd(dy, meml, W1, sv1)
    pair1 = _pair_sums(grads1)
    scattering = _split_start("scatter_l1_start", "scatter", pair1,
                              [jax.ShapeDtypeStruct(p.shape, p.dtype) for p in pair1], pair1[1])
    W0["norm_post"] = W0["norm_post"] + scattering[3][0, 0]
    dx, grads0 = _layer_bwd(dx1, meml, W0, sv0)
    pair1, landed1 = _split_wait("scatter_l1_wait", "scatter", scattering, dx)
    pair0 = _pair_sums(grads0)
    scattering = _split_start("scatter_l0_start", "scatter", pair0,
                              [jax.ShapeDtypeStruct(p.shape, p.dtype) for p in pair0], pair0[1])
    after_start = scattering[3][0, 0]
    grads = [grads0, grads1]

    small_local = dict(
        norm_pre=jnp.stack([g["norm_pre"][0] for g in grads]), norm_post=jnp.stack([g["norm_post"][0] for g in grads]),
        norm_mem=jnp.stack([g["norm_mem"][0] for g in grads]), a_log=jnp.stack([g["a_vec"][0, 4:8] for g in grads]),
        dt_bias=jnp.stack([g["dt_vec"][0, 4:8] for g in grads]), dn_norm=jnp.stack([g["dn_norm"][0] for g in grads]),
        gm_norm=jnp.stack([g["gm_norm"][0] for g in grads]), spatial_w=jnp.stack([g["spatial_w"] for g in grads]),
        spatial_b=jnp.stack([g["spatial_b"][:, 0, :] for g in grads]),
        sinks=jnp.stack([g["sink_vec"][0, :8] for g in grads]))
    packed = jnp.pad(_pack_small(small_local) + after_start, ((0, 8), (0, 0)))
    gsmall_packed = _allreduce_small(packed.reshape(2, -1, 1024)).reshape(-1, 1024)[:_SMALL_ROWS]

    d_s, m_s, v_s = _ew("adamw_small", _adamw_fn, [_pack_small(w), gsmall_packed, _pack_small(m), _pack_small(v)], 3)
    gsmall, dsmall, msmall, vsmall = (_unpack_small(p) for p in (gsmall_packed, d_s, m_s, v_s))
    g_o, d_o, m_o, v_o = dict(gsmall), dict(dsmall), dict(msmall), dict(vsmall)
    tr = lambda a: a.transpose(0, 2, 1)
    w_t, m_t, v_t = tr(w["w_in"]), tr(m["w_in"]), tr(v["w_in"])

    def update(l, totals, into):
        outs = {}
        for n, t in zip(_BIG, totals):
            g_l = t.reshape(local[n].shape[1:])
            if n == "w_in":
                outs[n] = _adamw_rows("adamw_" + n, l, w_t, g_l, m_t, v_t, into and into[n])
            else:
                outs[n] = _adamw_layer("adamw_" + n, l, w[n], g_l, m[n], v[n], into and into[n])
        return outs

    landed1[1] = landed1[1] + after_start.astype(landed1[1].dtype)
    outs1 = update(1, _pair_share(_chip_sums(landed1, pair1, me)), None)
    pair0, landed0 = _split_wait("scatter_l0_wait", "scatter", scattering, outs1["w_in"][0])
    outs = update(0, _pair_share(_chip_sums(landed0, pair0, me)), outs1)
    for n in _BIG:
        d_o[n], m_o[n], v_o[n], g_o[n] = [tr(o) for o in outs[n]] if n == "w_in" else outs[n]
    return (loss, dx[None], *[g_o[n] for n in _NAMES], *[d_o[n] for n in _NAMES], *[m_o[n] for n in _NAMES],
            *[v_o[n] for n in _NAMES])
```

```python
import collections
import functools

import jax
import jax.numpy as jnp
from jax import lax
from jax.experimental import pallas as pl
from jax.experimental.pallas import tpu as pltpu

F32 = jnp.float32
BF16 = jnp.bfloat16

D_MODEL = 1024
BRANCH_W = 512
MEM_LEN = 256
N_LAYER = 2
N_CHIP = 4
N_DEV = 8
EPS = 1e-6
NEG_INF = -1e30
DN_CHUNK = 64
LANES = 128
VMEM_LIMIT = 48 * 1024 * 1024

ADAM_LR, ADAM_B1, ADAM_B2, ADAM_EPS, ADAM_WD, ADAM_STEP = 0.001, 0.9, 0.999, 1e-08, 0.01, 10

N_PAD = 10240
O_GATE = 0
O_AQ, O_AK, O_AV, O_AZ = 4096, 4608, 5120, 5632
O_BUV, O_BZ = 6144, 7168
O_CKV, O_BA = 7680, 7936
O_CQ, O_CZ = 8192, 8704
O_MQ, O_MZ = 9216, 9728
O_MISC, W_MISC = O_CKV, 512
_PAD_SEGS = [(5896, 4096), (0, 512), (512, 512), (1024, 512), (1536, 512), (2056, 1024), (3080, 512),
             (4104, 128), (4232, 128), (2048, 8), (None, 120), (None, 128),
             (3592, 512), (4360, 512), (4872, 512), (5384, 512)]
D_IN = 9992
SHARD_IN = D_IN // N_CHIP


SHARD_PAD = 2560


def _pad_parts():
    parts, off = [], 0
    for s, n in _PAD_SEGS:
        a = s
        while s is not None and a < s + n:
            chip = a // SHARD_IN
            b = min(s + n, (chip + 1) * SHARD_IN)
            parts.append((chip, a - chip * SHARD_IN, off + a - s, b - a))
            a = b
        off += n
    return parts


PERM_ROWS = 512
PERM_SLACK = 32


def _permute_rows(name, src, parts, n_out, out_dtype):
    B, Z = PERM_ROWS, PERM_ROWS + PERM_SLACK
    w = src.shape[1]
    plans = []
    for blk in range(n_out // B):
        o, runs = blk * B, []
        for s, d, n in parts:
            lo, hi = max(d, o), min(d + n, o + B)
            if lo < hi:
                s0 = s + lo - d
                wa = s0 // 16 * 16
                wb = min(-(-(s0 + hi - lo) // 16) * 16, src.shape[0])
                runs.append((wa, wb - wa, s0 - (lo - o) - wa, lo - o, hi - o))
        plans.append(runs)
    max_runs = max(len(r) for r in plans)
    nblk = len(plans)

    def body(*refs):
        src_ref, out_ref, inbuf, obuf, insem, outsem = (refs[0],) + refs[-5:]

        def in_copies(blk):
            return [pltpu.make_async_copy(src_ref.at[pl.ds(wa, ws)], inbuf.at[blk % 2, r, pl.ds(0, ws)],
                                          insem.at[blk % 2, r]) for r, (wa, ws, _, _, _) in enumerate(plans[blk])]

        def out_copy(blk):
            return pltpu.make_async_copy(obuf.at[blk % 2], out_ref.at[pl.ds(blk * B, B)], outsem.at[blk % 2])

        for cp in in_copies(0):
            cp.start()
        rid = _iota((B, 1), 0)
        for blk in range(nblk):
            if blk + 1 < nblk:
                for cp in in_copies(blk + 1):
                    cp.start()
            for cp in in_copies(blk):
                cp.wait()
            val = jnp.zeros((B, w), F32)
            for r, (wa, ws, t, l0, l1) in enumerate(plans[blk]):
                win = jnp.concatenate([inbuf[blk % 2, r, pl.ds(0, ws)].astype(F32), jnp.zeros((Z - ws, w), F32)], axis=0)
                moved = pltpu.roll(win, (-t) % Z, 0)[:B]
                val = jnp.where((rid >= l0) & (rid < l1), moved, val)
            if blk >= 2:
                out_copy(blk - 2).wait()
            obuf[blk % 2] = val.astype(out_dtype)
            out_copy(blk).start()
        for blk in range(max(nblk - 2, 0), nblk):
            out_copy(blk).wait()

    return pl.pallas_call(
        body, name=name, in_specs=[_ANY], out_specs=_ANY, out_shape=jax.ShapeDtypeStruct((n_out, w), out_dtype),
        scratch_shapes=[pltpu.VMEM((2, max_runs, Z, w), src.dtype), pltpu.VMEM((2, B, w), out_dtype),
                        pltpu.SemaphoreType.DMA((2, max_runs)), pltpu.SemaphoreType.DMA((2,))],
        compiler_params=_params(),
    )(src)


def _slab_parts():
    h, out = SHARD_PAD // 2, []
    for chip, s, d, n in _pad_parts():
        a = s
        while a < s + n:
            half = a // h
            b = min(s + n, (half + 1) * h)
            out.append(((half * N_CHIP + chip) * h + a - half * h, d + a - s, b - a))
            a = b
    return out


def _w_pad_from_slabs(slabs):
    return _permute_rows("w_pad_rows", slabs.reshape(-1, slabs.shape[-1]), _slab_parts(), N_PAD, BF16)


def _slabs_from_pad(dw):
    slabs = _permute_rows("w_pad_grad_rows", dw, [(d, s, n) for s, d, n in _slab_parts()], N_CHIP * SHARD_PAD, BF16)
    return slabs.reshape(2, N_CHIP, SHARD_PAD // 2, dw.shape[1])


def _dot(a, b, dims, prec):
    if prec == "bf16":
        return lax.dot_general(a.astype(BF16), b.astype(BF16), (dims, ((), ())), preferred_element_type=F32)
    return lax.dot_general(a, b, (dims, ((), ())), precision=lax.Precision.HIGHEST, preferred_element_type=F32)


_NN, _NT, _TN = ((1,), (0,)), ((1,), (1,)), ((0,), (0,))


def _make_mm(prec):
    @jax.custom_vjp
    def nn(a, b):
        return _dot(a, b, _NN, prec)

    @jax.custom_vjp
    def nt(a, b):
        return _dot(a, b, _NT, prec)

    @jax.custom_vjp
    def tn(a, b):
        return _dot(a, b, _TN, prec)

    nn.defvjp(lambda a, b: (nn(a, b), (a, b)), lambda r, g: (nt(g, r[1]), tn(r[0], g)))
    nt.defvjp(lambda a, b: (nt(a, b), (a, b)), lambda r, g: (nn(g, r[1]), tn(g, r[0])))
    tn.defvjp(lambda a, b: (tn(a, b), (a, b)), lambda r, g: (nt(r[1], g), nn(r[0], g)))
    return nn, nt, tn


_nn16, _nt16, _tn16 = _make_mm("bf16")
_nn32, _nt32, _tn32 = _make_mm("f32")


def _make_slice(axis):
    @functools.partial(jax.custom_vjp, nondiff_argnums=(1, 2, 3))
    def sl(x, a, b, n):
        return x[a:b] if axis == 0 else x[:, a:b]

    def fwd(x, a, b, n):
        return sl(x, a, b, n), None

    def bwd(a, b, n, _, g):
        parts = []
        if a > 0:
            parts.append(jnp.zeros((a, g.shape[1]) if axis == 0 else (g.shape[0], a), g.dtype))
        parts.append(g)
        if n - b > 0:
            parts.append(jnp.zeros((n - b, g.shape[1]) if axis == 0 else (g.shape[0], n - b), g.dtype))
        return (jnp.concatenate(parts, axis=axis),)

    sl.defvjp(fwd, bwd)
    return sl


_sl0, _sl1 = _make_slice(0), _make_slice(1)


def _rowsl(x, a, b):
    return _sl0(x, a, b, x.shape[0])


def _cols(x, a, b):
    return _sl1(x, a, b, x.shape[1])


@functools.partial(jax.custom_vjp, nondiff_argnums=(1,))
def _rollr(x, s):
    return pltpu.roll(x, s, 0)


_rollr.defvjp(lambda x, s: (_rollr(x, s), None),
              lambda s, _, g: (pltpu.roll(g, g.shape[0] - s, 0),))


def _iota(shape, axis):
    return lax.broadcasted_iota(jnp.int32, shape, axis)


def _sigmoid(x):
    return lax.logistic(x)


def _silu(x):
    return x * _sigmoid(x)


def _gelu(x):
    return 0.5 * x * (1.0 + jnp.tanh(0.7978845608028654 * (x + 0.044715 * (x * x * x))))


def _softplus(x):
    return jnp.maximum(x, 0.0) + jnp.log(1.0 + jnp.exp(-jnp.abs(x)))


def _rms(x, g):
    return x * lax.rsqrt(jnp.mean(x * x, axis=-1, keepdims=True) + EPS) * g


def _lane_pick(x, lane):
    return jnp.sum(x * (_iota((1, x.shape[1]), 1) == lane).astype(F32), axis=1, keepdims=True)


Row = collections.namedtuple("Row", "arr w cb hb grad", defaults=(0, True))


def _full_spec(shape):
    return pl.BlockSpec(shape, lambda i, _n=len(shape): (0,) * _n)


def _load_params(refs):
    return [[p[g].astype(F32) for g in range(p.shape[0])] if len(p.shape) == 3 else p[...].astype(F32)
            for p in refs]


def _params(**kw):
    return pltpu.CompilerParams(vmem_limit_bytes=VMEM_LIMIT, **kw)


def _rows_fwd(name, fn, rows, params, outs, tr, carry=None):
    T = rows[0].arr.shape[0]
    n = T // tr
    halos = [r for r in rows if r.hb]
    nr, nh, npar, no = len(rows), len(halos), len(params), len(outs)

    def body(*refs):
        row_refs, halo_refs = refs[:nr], refs[nr:nr + nh]
        par_refs = refs[nr + nh:nr + nh + npar]
        out_refs = refs[nr + nh + npar:nr + nh + npar + no]
        rest = refs[nr + nh + npar + no:]
        first = pl.program_id(0) == 0
        cvals = None
        if carry is not None:
            csave_ref, carry_ref = rest

            @pl.when(first)
            def _():
                carry_ref[...] = jnp.zeros_like(carry_ref)

            cvals = [carry_ref[g] for g in range(carry[0])]
            for g in range(carry[0]):
                csave_ref[0, g] = cvals[g]
        c_out, o = fn(first, cvals, [r[...].astype(F32) for r in row_refs],
                      [h[...].astype(F32) for h in halo_refs], _load_params(par_refs))
        for r, v in zip(out_refs, o):
            r[...] = v.astype(r.dtype)
        if carry is not None:
            for g in range(carry[0]):
                carry_ref[g] = c_out[g]

    in_specs = [pl.BlockSpec((tr, r.w), lambda i, c=r.cb: (i, c)) for r in rows]
    in_specs += [pl.BlockSpec((r.hb, r.w), lambda i, c=r.cb, q=tr // r.hb: (jnp.maximum(i * q - 1, 0), c))
                 for r in halos]
    in_specs += [_full_spec(p.shape) for p in params]
    out_shape = [jax.ShapeDtypeStruct((T, w), dt) for w, dt in outs]
    out_specs = [pl.BlockSpec((tr, w), lambda i: (i, 0)) for w, _ in outs]
    scratch = []
    if carry is not None:
        out_shape.append(jax.ShapeDtypeStruct((n,) + carry, F32))
        out_specs.append(pl.BlockSpec((1,) + carry, lambda i: (i, 0, 0, 0)))
        scratch.append(pltpu.VMEM(carry, F32))
    return pl.pallas_call(
        body, name=name, grid=(n,), in_specs=in_specs, out_specs=out_specs, out_shape=out_shape,
        scratch_shapes=scratch, compiler_params=_params(dimension_semantics=("arbitrary",)),
    )(*[r.arr for r in rows], *[r.arr for r in halos], *params)


def _rows_bwd(name, fn, rows, params, douts, tr, carry=None, csave=None, dcols=None):
    T = rows[0].arr.shape[0]
    n = T // tr
    halos = [r for r in rows if r.hb]
    grows = [r for r in rows if r.grad is True]
    crows = [r for r in rows if r.grad == "cols"]
    wcols = sum(r.w for r in crows)
    nr, nh, npar, nd, ng = len(rows), len(halos), len(params), len(douts), len(grows)
    nc = 0 if carry is None else 1
    ncol = 1 if crows else 0
    nalias = 1 if (crows and dcols is not None) else 0

    def body(*refs):
        row_refs, halo_refs = refs[:nr], refs[nr:nr + nh]
        par_refs = refs[nr + nh:nr + nh + npar]
        k = nr + nh + npar
        csave_ref = refs[k] if nc else None
        dout_refs = refs[k + nc:k + nc + nd]
        k = k + nc + nd + nalias
        drow_refs = refs[k:k + ng]
        dcols_ref = refs[k + ng] if ncol else None
        dpar_refs = refs[k + ng + ncol:k + ng + ncol + npar]
        k = k + ng + ncol + npar
        dcarry_ref = refs[k] if nc else None
        hgrad_refs = refs[k + nc:]
        i = pl.program_id(0)
        first_tile = i == n - 1

        @pl.when(i == 0)
        def _():
            for r in dpar_refs:
                r[...] = jnp.zeros_like(r)
            for r in hgrad_refs:
                r[...] = jnp.zeros_like(r)
            if nc:
                dcarry_ref[...] = jnp.zeros_like(dcarry_ref)

        rv = [r[...].astype(F32) for r in row_refs]
        hv = [h[...].astype(F32) for h in halo_refs]
        pv = _load_params(par_refs)
        dov = [d[...].astype(F32) for d in dout_refs]
        if nc:
            cv = [csave_ref[0, g] for g in range(carry[0])]
            _, vjp = jax.vjp(lambda c, r, h, p: fn(first_tile, c, r, h, p), cv, rv, hv, pv)
            dc, dr, dh, dp = vjp(([dcarry_ref[g] for g in range(carry[0])], dov))
            for g in range(carry[0]):
                dcarry_ref[g] = dc[g]
        else:
            _, vjp = jax.vjp(lambda r, h, p: fn(first_tile, None, r, h, p)[1], rv, hv, pv)
            dr, dh, dp = vjp(dov)
        gi = hi = 0
        pieces = []
        for kk, r in enumerate(rows):
            d = dr[kk]
            if r.hb:
                carried = hgrad_refs[hi][...]
                d = d + (carried if tr == r.hb else
                         jnp.concatenate([jnp.zeros((tr - r.hb, r.w), F32), carried], axis=0))
                hgrad_refs[hi][...] = dh[hi]
                hi += 1
            if r.grad is True:
                drow_refs[gi][...] = d.astype(drow_refs[gi].dtype)
                gi += 1
            elif r.grad == "cols":
                pieces.append(d.astype(BF16))
        if ncol:
            dcols_ref[...] = pieces[0] if len(pieces) == 1 else jnp.concatenate(pieces, axis=1)
        for r, d in zip(dpar_refs, dp):
            if len(r.shape) == 3:
                for g in range(r.shape[0]):
                    r[g] += d[g]
            else:
                r[...] += d

    rev = lambda i: n - 1 - i
    in_specs = [pl.BlockSpec((tr, r.w), lambda i, c=r.cb: (rev(i), c)) for r in rows]
    in_specs += [pl.BlockSpec((r.hb, r.w), lambda i, c=r.cb, q=tr // r.hb: (jnp.maximum(rev(i) * q - 1, 0), c))
                 for r in halos]
    in_specs += [_full_spec(p.shape) for p in params]
    args = [r.arr for r in rows] + [r.arr for r in halos] + list(params)
    scratch = []
    if nc:
        in_specs.append(pl.BlockSpec((1,) + carry, lambda i: (rev(i), 0, 0, 0)))
        args.append(csave)
        scratch.append(pltpu.VMEM(carry, F32))
    douts = [d if isinstance(d, Row) else Row(d, d.shape[1], 0) for d in douts]
    in_specs += [pl.BlockSpec((tr, d.w), lambda i, c=d.cb: (rev(i), c)) for d in douts]
    args += [d.arr for d in douts]
    aliases = {}
    if nalias:
        aliases = {len(args): ng}
        in_specs.append(pl.BlockSpec(memory_space=pl.ANY))
        args.append(dcols)
    scratch += [pltpu.VMEM((r.hb, r.w), F32) for r in halos]
    out_shape = [jax.ShapeDtypeStruct((T, r.w), F32) for r in grows]
    out_specs = [pl.BlockSpec((tr, r.w), lambda i: (rev(i), 0)) for r in grows]
    if ncol:
        off = crows[0].cb * crows[0].w
        assert off % wcols == 0 and all(a.cb * a.w + a.w == b.cb * b.w for a, b in zip(crows, crows[1:]))
        out_shape.append(jax.ShapeDtypeStruct((T, N_PAD), BF16))
        out_specs.append(pl.BlockSpec((tr, wcols), lambda i, c=off // wcols: (rev(i), c)))
    out_shape += [jax.ShapeDtypeStruct(p.shape, F32) for p in params]
    out_specs += [_full_spec(p.shape) for p in params]
    res = pl.pallas_call(
        body, name=name, grid=(n,), in_specs=in_specs, out_specs=out_specs, out_shape=out_shape,
        scratch_shapes=scratch, input_output_aliases=aliases,
        compiler_params=_params(dimension_semantics=("arbitrary",)),
    )(*args)
    return list(res[:ng]), list(res[ng + ncol:]), (res[ng] if ncol else dcols)


def _fill_misc(dcols, dkv, dba, tr):
    T = dkv.shape[0]

    def body(kv_ref, ba_ref, _, o_ref):
        o_ref[...] = jnp.concatenate([kv_ref[...], ba_ref[...]], axis=1).astype(BF16)

    return pl.pallas_call(
        body, name="misc_bwd", grid=(T // tr,),
        in_specs=[pl.BlockSpec((tr, 256), lambda i: (i, 0)), pl.BlockSpec((tr, 256), lambda i: (i, 0)),
                  pl.BlockSpec(memory_space=pl.ANY)],
        out_specs=pl.BlockSpec((tr, W_MISC), lambda i: (i, O_MISC // W_MISC)),
        out_shape=jax.ShapeDtypeStruct((T, N_PAD), BF16), input_output_aliases={2: 0},
        compiler_params=_params(dimension_semantics=("arbitrary",)),
    )(dkv, dba, dcols)


def _up_bwd(ys, cols, dm, w_up):
    T, tr = dm.shape[0], UPB_TR

    def body(y_ref, gl_ref, dm_ref, w_ref, dy_ref, dgl_ref, dw_ref):
        @pl.when(pl.program_id(1) == 0)
        def _():
            dw_ref[...] = jnp.zeros_like(dw_ref)

        _, vjp = jax.vjp(lambda y, gl, w: _sigmoid(gl) * _nn16(y, w),
                         y_ref[...].astype(F32), gl_ref[...].astype(F32), w_ref[...].astype(F32))
        dy, dgl, dw = vjp(dm_ref[...])
        dy_ref[...] = dy
        dgl_ref[...] = dgl.astype(BF16)
        dw_ref[...] += dw

    branch_rows = lambda w: pl.BlockSpec((tr, w), lambda n, i: (i, n))
    weight = pl.BlockSpec((None, BRANCH_W, D_MODEL), lambda n, i: (n, 0, 0))
    return pl.pallas_call(
        body, name="up_bwd", grid=(4, T // tr),
        in_specs=[branch_rows(BRANCH_W), branch_rows(D_MODEL), pl.BlockSpec((tr, D_MODEL), lambda n, i: (i, 0)), weight],
        out_specs=[branch_rows(BRANCH_W), branch_rows(D_MODEL), weight],
        out_shape=[jax.ShapeDtypeStruct((T, 4 * BRANCH_W), F32), jax.ShapeDtypeStruct((T, N_PAD), BF16),
                   jax.ShapeDtypeStruct(w_up.shape, F32)],
        compiler_params=_params(dimension_semantics=("arbitrary", "arbitrary")),
    )(ys, cols, dm, w_up)


def _matmul(name, a, b, kind, out_dtype, tm, tn, tk):
    if kind == "tn":
        (K, M), N = a.shape, b.shape[1]
    else:
        (M, K), N = a.shape, (b.shape[0] if kind == "nt" else b.shape[1])
    tm, tn, tk = min(tm, M), min(tn, N), min(tk, K)
    nk = K // tk
    dims = {"nn": _NN, "nt": _NT, "tn": _TN}[kind]

    def body(a_ref, b_ref, o_ref, *acc):
        part = lax.dot_general(a_ref[...], b_ref[...], (dims, ((), ())), preferred_element_type=F32)
        if nk == 1:
            o_ref[...] = part.astype(o_ref.dtype)
            return
        acc_ref = acc[0] if acc else o_ref
        k = pl.program_id(2)

        @pl.when(k == 0)
        def _():
            acc_ref[...] = part

        @pl.when(k > 0)
        def _():
            acc_ref[...] += part

        if acc:
            @pl.when(k == nk - 1)
            def _():
                o_ref[...] = acc_ref[...].astype(o_ref.dtype)

    a_spec = pl.BlockSpec((tk, tm), lambda i, j, k: (k, i)) if kind == "tn" else pl.BlockSpec((tm, tk), lambda i, j, k: (i, k))
    b_spec = pl.BlockSpec((tn, tk), lambda i, j, k: (j, k)) if kind == "nt" else pl.BlockSpec((tk, tn), lambda i, j, k: (k, j))
    return pl.pallas_call(
        body, name=name, grid=(M // tm, N // tn, nk), in_specs=[a_spec, b_spec],
        out_specs=pl.BlockSpec((tm, tn), lambda i, j, k: (i, j)),
        out_shape=jax.ShapeDtypeStruct((M, N), out_dtype),
        scratch_shapes=[pltpu.VMEM((tm, tn), F32)] if nk > 1 and out_dtype != F32 else [],
        compiler_params=_params(dimension_semantics=("arbitrary", "arbitrary", "arbitrary")),
    )(a, b)


def _pre_fn(first, _, rows, halos, params):
    return None, [_rms(rows[0], params[0])]


def _pre_fn_res(first, _, rows, halos, params):
    return None, [_rms(rows[0], params[0]), rows[0]]


def _memkv_fn(first, _, rows, halos, params):
    g, w = params
    return None, [_nn16(_rms(rows[0], g), w)]


def _conv_silu(x, halo, w4, keep_halo):
    tr = x.shape[0]
    halo = halo * keep_halo
    rid = _iota((tr, 1), 0)
    acc = w4[3] * x
    for s in (1, 2, 3):
        hs = jnp.concatenate([_rollr(halo, s), jnp.zeros((tr - halo.shape[0], x.shape[1]), F32)], axis=0)
        acc = acc + w4[3 - s] * jnp.where(rid < s, hs, _rollr(x, s))
    return _silu(acc)


def _dn_fn(first, S, rows, halos, params):
    qp, kp, vp, z, ba = rows
    conv, a_vec, dt_vec, dnorm = params
    ba = _cols(ba, 0, LANES)
    tr = qp.shape[0]
    keep = jnp.where(first, 0.0, 1.0)
    q = _conv_silu(qp, halos[0], [conv[3 * j + 0] for j in range(4)], keep)
    k = _conv_silu(kp, halos[1], [conv[3 * j + 1] for j in range(4)], keep)
    v = _conv_silu(vp, halos[2], [conv[3 * j + 2] for j in range(4)], keep)
    qh, kh, vh = [], [], []
    for h in range(4):
        a, b = h * LANES, (h + 1) * LANES
        xq, xk = _cols(q, a, b), _cols(k, a, b)
        qh.append(xq * lax.rsqrt(jnp.sum(xq * xq, axis=1, keepdims=True) + EPS) * (LANES ** -0.5))
        kh.append(xk * lax.rsqrt(jnp.sum(xk * xk, axis=1, keepdims=True) + EPS))
        vh.append(_cols(v, a, b))
    beta_all = _sigmoid(ba)
    g_all = -jnp.exp(a_vec) * _softplus(ba + dt_vec)
    C = DN_CHUNK
    ii, jj = _iota((C, C), 0), _iota((C, C), 1)
    strict, incl = ii > jj, ii >= jj
    eye = (ii == jj).astype(F32)
    last_row = (_iota((C, 1), 0) == C - 1).astype(F32)
    n_chunk = tr // C
    pairs = [(c, h) for c in range(n_chunk) for h in range(4)]
    rows_of = lambda a, c: _rowsl(a, c * C, (c + 1) * C)
    gcs = [_nn32(incl.astype(F32), rows_of(g_all, c)) for c in range(n_chunk)]
    qc = {(c, h): rows_of(qh[h], c) for c, h in pairs}
    kc = {(c, h): rows_of(kh[h], c) for c, h in pairs}
    beta = {(c, h): _lane_pick(rows_of(beta_all, c), h) for c, h in pairs}
    gc = {(c, h): _lane_pick(gcs[c], 4 + h) for c, h in pairs}
    dec = {p: jnp.exp(jnp.where(incl, gc[p] - jnp.sum(eye * gc[p], axis=0, keepdims=True), 0.0)) for p in pairs}
    egc = {p: jnp.exp(gc[p]) for p in pairs}
    kb = {p: kc[p] * beta[p] for p in pairs}
    kq = {p: _nt16(jnp.concatenate([kb[p], qc[p]], axis=0), kc[p]) for p in pairs}
    P = {p: -jnp.where(strict, _rowsl(kq[p], 0, C) * dec[p], 0.0) for p in pairs}
    aqk = {p: jnp.where(incl, _rowsl(kq[p], C, 2 * C) * dec[p], 0.0) for p in pairs}
    tinv = {p: eye + P[p] for p in pairs}
    P = {p: _nn16(P[p], P[p]) for p in pairs}
    for j in range(5):
        if j < 4:
            pt = {p: _nn16(jnp.concatenate([P[p], tinv[p]], axis=0), P[p]) for p in pairs}
            tinv = {p: tinv[p] + _rowsl(pt[p], C, 2 * C) for p in pairs}
            P = {p: _rowsl(pt[p], 0, C) for p in pairs}
        else:
            tinv = {p: tinv[p] + _nn16(tinv[p], P[p]) for p in pairs}
    uw = {(c, h): _nn16(tinv[c, h], jnp.concatenate([rows_of(vh[h], c) * beta[c, h], kb[c, h] * egc[c, h]], axis=1))
          for c, h in pairs}
    S = list(S)
    ychunks = []
    for c in range(n_chunk):
        zc = rows_of(z, c)
        hs = range(4)
        ws = [_nn16(jnp.concatenate([_cols(uw[c, h], LANES, 2 * LANES), qc[c, h] * egc[c, h]], axis=0), S[h]) for h in hs]
        vnew = [_cols(uw[c, h], 0, LANES) - _rowsl(ws[h], 0, C) for h in hs]
        o = [_rowsl(ws[h], C, 2 * C) + _nn16(aqk[c, h], vnew[h]) for h in hs]
        glast = [jnp.sum(gc[c, h] * last_row, axis=0, keepdims=True) for h in hs]
        S = [S[h] * jnp.exp(glast[h]) + _tn16(kc[c, h] * jnp.exp(glast[h] - gc[c, h]), vnew[h]) for h in hs]
        ychunks.append(jnp.concatenate(
            [_rms(o[h], dnorm) * _silu(_cols(zc, h * LANES, (h + 1) * LANES)) for h in hs], axis=1))
    return S, [jnp.concatenate(ychunks, axis=0)]


def _gm_fn(first, _, rows, halos, params):
    uv, z = rows
    gnorm, ws, bs = params
    tr = uv.shape[0]
    guv = _gelu(uv)
    u = _cols(guv, 0, BRANCH_W)
    v = _rms(_cols(guv, BRANCH_W, 2 * BRANCH_W), gnorm)
    ii, jj = _iota((LANES, LANES), 0), _iota((LANES, LANES), 1)
    eye = (ii == jj).astype(F32)
    wsm = [jnp.where(ii >= jj, ws[g], 0.0) for g in range(4)]
    bcol = [jnp.sum(eye * bs[g], axis=1, keepdims=True) for g in range(4)]
    chunks = []
    for c in range(tr // LANES):
        vc = _rowsl(v, c * LANES, (c + 1) * LANES)
        chunks.append(jnp.concatenate(
            [_nn16(wsm[g], _cols(vc, g * LANES, (g + 1) * LANES)) + bcol[g] for g in range(4)], axis=1))
    return None, [u * jnp.concatenate(chunks, axis=0) * _silu(z)]


def _swa_fn(first, _, rows, halos, params):
    q, kvc, z = rows
    sink_vec = params[0]
    P = LANES
    kv = jnp.concatenate([halos[0], kvc], axis=0)
    k, v = _cols(kv, 0, P), _cols(kv, P, 2 * P)
    r, cc = _iota((P, P), 0), _iota((P, P), 1)
    lane = _iota((1, P), 1)
    dist = _iota((P, 2 * P), 0) + P - _iota((P, 2 * P), 1)
    kmin = jnp.where(first, P, 0)
    valid = (dist >= 0) & (dist < P) & (_iota((P, 2 * P), 1) >= kmin)
    halves = [(lane < 64).astype(F32), (lane >= 64).astype(F32)]
    k_v = jnp.concatenate([k, v], axis=0)
    kkvv = [_nn16(k_v, (r == kh * 64 + (cc & 63)).astype(F32)) for kh in range(2)]
    scores = [_nt16(jnp.concatenate([_cols(q, (2 * kh + g // 2) * P, (2 * kh + g // 2 + 1) * P) * halves[g % 2]
                                     for g in range(4)], axis=0), _rowsl(kkvv[kh], 0, 2 * P)) for kh in range(2)]
    probs = []
    for kh in range(2):
        ps = []
        for g in range(4):
            s = jnp.where(valid, _rowsl(scores[kh], g * P, (g + 1) * P) * 0.125, NEG_INF)
            sink = _lane_pick(sink_vec, kh * 4 + g)
            m = lax.stop_gradient(jnp.maximum(jnp.max(s, axis=1, keepdims=True), sink))
            e = jnp.exp(s - m)
            ps.append(e / (jnp.sum(e, axis=1, keepdims=True) + jnp.exp(sink - m)))
        probs.append(jnp.concatenate(ps, axis=0))
    outs = [_nn16(probs[kh], _rowsl(kkvv[kh], 2 * P, 4 * P)) for kh in range(2)]
    blocks = [_rowsl(outs[j // 2], (2 * (j % 2)) * P, (2 * (j % 2) + 1) * P) * halves[0]
              + _rowsl(outs[j // 2], (2 * (j % 2) + 1) * P, (2 * (j % 2) + 2) * P) * halves[1] for j in range(4)]
    return None, [jnp.concatenate(blocks, axis=1) * _silu(z)]


def _mem_fn(first, _, rows, halos, params):
    q, z = rows
    mkv = params[0]
    heads = [(h * LANES, (h + 1) * LANES) for h in range(4)]
    scores = [_nt16(_cols(q, a, b), _cols(mkv, a, b)) * (LANES ** -0.5) for a, b in heads]
    probs = []
    for s in scores:
        e = jnp.exp(s - lax.stop_gradient(jnp.max(s, axis=1, keepdims=True)))
        probs.append(e / jnp.sum(e, axis=1, keepdims=True))
    outs = [_nn16(p, _cols(mkv, BRANCH_W + a, BRANCH_W + b)) for p, (a, b) in zip(probs, heads)]
    return None, [jnp.concatenate(outs, axis=1) * _silu(z)]


def _up_fn(first, _, rows, halos, params):
    ys, gl, w_up = rows[:4], rows[4], params[0]
    merged = None
    for n in range(4):
        term = _sigmoid(_cols(gl, n * D_MODEL, (n + 1) * D_MODEL)) * _nn16(ys[n], w_up[n])
        merged = term if merged is None else merged + term
    return None, [merged]


def _out_fn(first, _, rows, halos, params):
    x, merged = rows
    w, g = params
    return None, [x + _rms(_nn16(merged, w), g)]


def _loss_fn(first, _, rows, halos, params):
    y, t = rows
    d = y - t
    lrow = 0.5 * jnp.mean(d * d, axis=1, keepdims=True)
    return None, [d * (1.0 / D_MODEL), jnp.broadcast_to(lrow, (y.shape[0], LANES))]


TR = 256
DN_TR = 256
UP_TR = 256
UPB_TR = 512
CONV_HALO = 16
CARRY = (4, LANES, LANES)


def _branch_rows(cols, g):
    hb = CONV_HALO
    a = [Row(cols, 512, O_AQ // 512, hb, g), Row(cols, 512, O_AK // 512, hb, g), Row(cols, 512, O_AV // 512, hb, g),
         Row(cols, 512, O_AZ // 512, 0, g), Row(cols, 256, O_BA // 256)]
    b = [Row(cols, 1024, O_BUV // 1024, 0, g), Row(cols, 512, O_BZ // 512, 0, g)]
    c = [Row(cols, 512, O_CQ // 512, 0, g), Row(cols, 256, O_CKV // 256, LANES), Row(cols, 512, O_CZ // 512, 0, g)]
    m = [Row(cols, 512, O_MQ // 512, 0, g), Row(cols, 512, O_MZ // 512, 0, g)]
    return a, b, c, m


def _layer_fwd(x, mem, W):
    h = _rows_fwd("prenorm_fwd", _pre_fn, [Row(x, D_MODEL, 0)], [W["norm_pre"]], [(D_MODEL, BF16)], TR)[0]
    cols = _matmul("in_proj_fwd", h, W["w_pad"], "nt", BF16, 1024, 1024, 1024)
    mem_kv = _rows_fwd("memkv_fwd", _memkv_fn, [Row(mem, D_MODEL, 0)], [W["norm_mem"], W["w_mem_kv"]],
                       [(D_MODEL, F32)], MEM_LEN)[0]
    ra, rb, rc, rm = _branch_rows(cols, True)
    y_a, csave = _rows_fwd("dn_fwd", _dn_fn, ra, [W["conv"], W["a_vec"], W["dt_vec"], W["dn_norm"]],
                           [(BRANCH_W, BF16)], DN_TR, CARRY)
    y_b = _rows_fwd("gm_fwd", _gm_fn, rb, [W["gm_norm"], W["spatial_w"], W["spatial_b"]], [(BRANCH_W, BF16)], TR)[0]
    y_c = _rows_fwd("swa_fwd", _swa_fn, rc, [W["sink_vec"]], [(BRANCH_W, BF16)], LANES)[0]
    y_m = _rows_fwd("mem_fwd", _mem_fn, rm, [mem_kv], [(BRANCH_W, BF16)], TR)[0]
    ys = [y_a, y_b, y_c, y_m]
    merged = _rows_fwd("up_fwd", _up_fn, [Row(y, BRANCH_W, 0) for y in ys] + [Row(cols, 4 * D_MODEL, 0)],
                       [W["w_up"]], [(D_MODEL, BF16)], UP_TR)[0]
    x_new = _rows_fwd("out_fwd", _out_fn, [Row(x, D_MODEL, 0), Row(merged, D_MODEL, 0)],
                      [W["w_out"], W["norm_post"]], [(D_MODEL, F32)], TR)[0]
    return x_new, dict(x=x, h=h, cols=cols, mem_kv=mem_kv, csave=csave, ys=ys, merged=merged)


def _layer_bwd(dxn, mem, W, sv):
    x, cols = sv["x"], sv["cols"]
    (dx_res, dm), (dw_out, dnorm_post), _ = _rows_bwd(
        "out_bwd", _out_fn, [Row(x, D_MODEL, 0), Row(sv["merged"], D_MODEL, 0)], [W["w_out"], W["norm_post"]],
        [dxn], TR)
    dys, dcols, dw_up = _up_bwd(jnp.concatenate(sv["ys"], axis=1), cols, dm, W["w_up"])
    dys = [Row(dys, BRANCH_W, n) for n in range(4)]
    ra, rb, rc, rm = _branch_rows(cols, "cols")
    (dba,), (dconv, da_vec, ddt_vec, ddn_norm), dcols = _rows_bwd(
        "dn_bwd", _dn_fn, ra, [W["conv"], W["a_vec"], W["dt_vec"], W["dn_norm"]], [dys[0]], DN_TR, CARRY,
        sv["csave"], dcols=dcols)
    _, (dgm_norm, dws, dbs), dcols = _rows_bwd(
        "gm_bwd", _gm_fn, rb, [W["gm_norm"], W["spatial_w"], W["spatial_b"]], [dys[1]], TR, dcols=dcols)
    (dkv_c,), (dsink,), dcols = _rows_bwd("swa_bwd", _swa_fn, rc, [W["sink_vec"]], [dys[2]], LANES, dcols=dcols)
    _, (dmem_kv,), dcols = _rows_bwd("mem_bwd", _mem_fn, rm, [sv["mem_kv"]], [dys[3]], TR, dcols=dcols)
    dcols = _fill_misc(dcols, dkv_c, dba, TR)
    _, (dnorm_mem, dw_mem_kv), _ = _rows_bwd("memkv_bwd", _memkv_fn, [Row(mem, D_MODEL, 0, 0, False)],
                                             [W["norm_mem"], W["w_mem_kv"]], [dmem_kv], MEM_LEN)
    dw_pad = _matmul("in_proj_dw", dcols, sv["h"], "tn", F32, 1024, 1024, 1024)
    dh = _matmul("in_proj_dx", dcols, W["w_pad"], "nn", F32, 1024, 1024, 1024)
    (dx,), (dnorm_pre,), _ = _rows_bwd("prenorm_bwd", _pre_fn_res, [Row(x, D_MODEL, 0)], [W["norm_pre"]],
                                       [dh, dx_res], TR)
    grads = dict(norm_pre=dnorm_pre, norm_post=dnorm_post, norm_mem=dnorm_mem, w_pad=dw_pad, conv=dconv,
                 a_vec=da_vec, dt_vec=ddt_vec, dn_norm=ddn_norm, gm_norm=dgm_norm, spatial_w=dws, spatial_b=dbs,
                 sink_vec=dsink, w_mem_kv=dw_mem_kv, w_up=dw_up, w_out=dw_out)
    return dx, grads


def _lane_vec(v, off):
    return jnp.zeros((1, LANES), F32).at[0, off:off + v.shape[0]].set(v)


def _layer_weights(l, w_pad, conv_w, w_mem_kv, w_up, w_out, small):
    return dict(
        w_pad=w_pad, conv=conv_w.reshape(4, 3, BRANCH_W).reshape(12, 1, BRANCH_W),
        w_mem_kv=w_mem_kv, w_up=w_up, w_out=w_out,
        norm_pre=small["norm_pre"][l][None], norm_post=small["norm_post"][l][None],
        norm_mem=small["norm_mem"][l][None],
        a_vec=_lane_vec(small["a_log"][l], 4), dt_vec=_lane_vec(small["dt_bias"][l], 4),
        dn_norm=small["dn_norm"][l][None], gm_norm=small["gm_norm"][l][None],
        spatial_w=small["spatial_w"][l], spatial_b=small["spatial_b"][l][:, None, :],
        sink_vec=_lane_vec(small["sinks"][l], 0))


_MESH = pl.DeviceIdType.MESH
_ANY = pl.BlockSpec(memory_space=pl.ANY)


def _position():
    return lax.axis_index("x"), lax.axis_index("y"), lax.axis_index("c")


def _remote(src, dst, send_sem, recv_sem, dev):
    return pltpu.make_async_remote_copy(src_ref=src, dst_ref=dst, send_sem=send_sem, recv_sem=recv_sem,
                                        device_id=dev, device_id_type=_MESH)


def _hbm_call(name, body, arrs, out_shapes, sems, aliases=None):
    return pl.pallas_call(
        body, name=name, in_specs=[_ANY] * len(arrs), out_specs=[_ANY] * len(out_shapes), out_shape=out_shapes,
        scratch_shapes=[pltpu.SemaphoreType.DMA((k,)) for k in sems], input_output_aliases=aliases or {},
        compiler_params=pltpu.CompilerParams(has_side_effects=True),
    )(*arrs)


def _other_chips(x, y):
    return [(1 - x, y), (x, 1 - y), (1 - x, 1 - y)]


def _gather_weights(arrs):
    n = len(arrs)

    def body(*refs):
        ins, outs = refs[:n], refs[n:2 * n]
        ici_send, ici_recv, d2d_send, d2d_recv = refs[2 * n:]
        x, y, c = _position()
        me = 2 * x + y
        chips = _other_chips(x, y)
        sends = []
        for a in range(n):
            for j, (px, py) in enumerate(chips):
                sends.append(_remote(ins[a].at[c], outs[a].at[c, me], ici_send.at[3 * a + j], ici_recv.at[3 * a + j],
                                     (px, py, c)))
                sends[-1].start()
        for a in range(n):
            for j, (px, py) in enumerate(chips):
                slab = outs[a].at[c, 2 * px + py]
                _remote(ins[a].at[c], slab, ici_send.at[3 * a + j], ici_recv.at[3 * a + j], (px, py, c)).wait_recv()
                sends.append(_remote(slab, slab, d2d_send.at[3 * a + j], d2d_recv.at[3 * a + j], (x, y, 1 - c)))
                sends[-1].start()
        for a in range(n):
            for j, (px, py) in enumerate(chips):
                slab = outs[a].at[1 - c, 2 * px + py]
                _remote(slab, slab, d2d_send.at[3 * a + j], d2d_recv.at[3 * a + j], (x, y, 1 - c)).wait_recv()
        for cp in sends:
            cp.wait_send()

    return _hbm_call("gather_weights", body, arrs,
                     [jax.ShapeDtypeStruct((N_LAYER, N_CHIP) + a.shape[1:], a.dtype) for a in arrs], [3 * n] * 4)


def _pair_exchange(arrs):
    n = len(arrs)

    def body(*refs):
        ins, outs = refs[:n], refs[n:2 * n]
        send_sems, recv_sems = refs[2 * n:]
        x, y, c = _position()
        cps = [_remote(ins[a].at[1 - c], outs[a], send_sems.at[a], recv_sems.at[a], (x, y, 1 - c)) for a in range(n)]
        for cp in cps:
            cp.start()
        for cp in cps:
            cp.wait_recv()
        for cp in cps:
            cp.wait_send()

    return _hbm_call("pair_exchange", body, arrs, [jax.ShapeDtypeStruct(a.shape[1:], a.dtype) for a in arrs], [n, n])


def _chip_scatter(arrs):
    n = len(arrs)

    def body(*refs):
        ins, outs = refs[:n], refs[n:2 * n]
        send_sems, recv_sems = refs[2 * n:]
        x, y, c = _position()
        me = 2 * x + y
        sends = []
        for a in range(n):
            for j, (px, py) in enumerate(_other_chips(x, y)):
                sends.append(_remote(ins[a].at[2 * px + py], outs[a].at[me], send_sems.at[3 * a + j],
                                     recv_sems.at[3 * a + j], (px, py, c)))
                sends[-1].start()
        for a in range(n):
            for j, (px, py) in enumerate(_other_chips(x, y)):
                _remote(ins[a].at[me], outs[a].at[2 * px + py], send_sems.at[3 * a + j], recv_sems.at[3 * a + j],
                        (px, py, c)).wait_recv()
        for cp in sends:
            cp.wait_send()

    return _hbm_call("chip_scatter", body, arrs, [jax.ShapeDtypeStruct(a.shape, a.dtype) for a in arrs],
                     [3 * n, 3 * n])


def _pair_share(arrs):
    n = len(arrs)

    def body(*refs):
        ins, outs = refs[:n], refs[n:2 * n]
        send_sems, recv_sems = refs[2 * n:]
        x, y, c = _position()
        cps = [_remote(ins[a].at[c], outs[a].at[c], send_sems.at[a], recv_sems.at[a], (x, y, 1 - c)) for a in range(n)]
        for cp in cps:
            cp.start()
        for a in range(n):
            _remote(ins[a].at[c], outs[a].at[1 - c], send_sems.at[a], recv_sems.at[a], (x, y, 1 - c)).wait_recv()
        for cp in cps:
            cp.wait_send()

    return _hbm_call("pair_share", body, arrs, [jax.ShapeDtypeStruct(a.shape, a.dtype) for a in arrs], [n, n],
                     {a: a for a in range(n)})


def _pair_forward(arrs):
    n = len(arrs)

    def body(*refs):
        ins, outs = refs[:n], refs[n:2 * n]
        send_sems, recv_sems = refs[2 * n:]
        x, y, c = _position()
        sends = []
        for a in range(n):
            for j, (px, py) in enumerate(_other_chips(x, y)):
                sends.append(_remote(ins[a].at[c, 2 * px + py], outs[a].at[c, 2 * px + py], send_sems.at[3 * a + j],
                                     recv_sems.at[3 * a + j], (x, y, 1 - c)))
                sends[-1].start()
        for a in range(n):
            for j, (px, py) in enumerate(_other_chips(x, y)):
                slab = outs[a].at[1 - c, 2 * px + py]
                _remote(slab, slab, send_sems.at[3 * a + j], recv_sems.at[3 * a + j], (x, y, 1 - c)).wait_recv()
        for cp in sends:
            cp.wait_send()

    return _hbm_call("pair_forward", body, arrs, [jax.ShapeDtypeStruct(a.shape, a.dtype) for a in arrs],
                     [3 * n, 3 * n], {a: a for a in range(n)})


_HBM = pl.BlockSpec(memory_space=pltpu.HBM)
_SEM = pl.BlockSpec(memory_space=pltpu.SEMAPHORE)
_EFFECT = pltpu.SideEffectType.DATAFLOW_SIDE_EFFECTING


def _chip_copies(kind, srcs, lands, send_sems, recv_sems):
    x, y, c = _position()
    me = 2 * x + y
    sends, recvs = [], []
    for a in range(len(srcs)):
        for j, (px, py) in enumerate(_other_chips(x, y)):
            s, sems, dev = 2 * px + py, (send_sems.at[3 * a + j], recv_sems.at[3 * a + j]), (px, py, c)
            if kind == "gather":
                sends.append(_remote(srcs[a].at[c], lands[a].at[c, me], *sems, dev))
                recvs.append(_remote(srcs[a].at[c], lands[a].at[c, s], *sems, dev))
            else:
                sends.append(_remote(srcs[a].at[s], lands[a].at[me], *sems, dev))
                recvs.append(_remote(srcs[a].at[me], lands[a].at[s], *sems, dev))
    return sends, recvs


def _split_start(name, kind, srcs, land_shapes, after):
    n = len(srcs)

    def body(*refs):
        sends, _ = _chip_copies(kind, refs[:n], refs[n:2 * n], refs[2 * n + 1], refs[2 * n + 2])
        for cp in sends:
            cp.start()
        refs[-1][...] = jnp.zeros_like(refs[-1])

    hbm = lambda a: pltpu.with_memory_space_constraint(a, pltpu.HBM)
    lands = [lax.empty(s.shape, s.dtype) for s in land_shapes]
    outs = pl.pallas_call(
        body, name=name, in_specs=[_HBM] * (2 * n) + [_ANY],
        out_specs=[_SEM, _SEM] + [_HBM] * (2 * n) + [pl.BlockSpec(memory_space=pltpu.VMEM)],
        out_shape=[pltpu.SemaphoreType.DMA((3 * n,)), pltpu.SemaphoreType.DMA((3 * n,))]
        + [pltpu.HBM(a.shape, a.dtype) for a in list(srcs) + lands] + [jax.ShapeDtypeStruct((8, LANES), F32)],
        input_output_aliases={i: 2 + i for i in range(2 * n)},
        compiler_params=pltpu.CompilerParams(has_side_effects=_EFFECT),
    )(*[hbm(a) for a in srcs], *[hbm(a) for a in lands], after)
    return outs[0], outs[1], list(outs[2:2 + 2 * n]), outs[-1]


def _split_wait(name, kind, started, after):
    send_sems, recv_sems, thru, _ = started
    n = len(thru) // 2

    def body(*refs):
        sends, recvs = _chip_copies(kind, refs[:n], refs[n:2 * n], refs[2 * n], refs[2 * n + 1])
        for cp in sends:
            cp.wait_send()
        for cp in recvs:
            cp.wait_recv()

    outs = pl.pallas_call(
        body, name=name, in_specs=[_HBM] * (2 * n) + [_SEM, _SEM, _ANY], out_specs=[_HBM] * (2 * n),
        out_shape=[pltpu.HBM(a.shape, a.dtype) for a in thru], input_output_aliases={i: i for i in range(2 * n)},
        compiler_params=pltpu.CompilerParams(has_side_effects=_EFFECT),
    )(*thru, send_sems, recv_sems, after)
    return list(outs[:n]), list(outs[n:])


def _allreduce_small(g):
    def body(g_ref, o_ref, pair_buf, chip_buf, send_sems, recv_sems):
        x, y, c = _position()
        me = 2 * x + y
        sib = (x, y, 1 - c)
        to_sib = _remote(g_ref.at[1 - c], pair_buf, send_sems.at[0], recv_sems.at[0], sib)
        to_sib.start()
        to_sib.wait_recv()
        chip_buf[me] = g_ref[c] + pair_buf[...]
        sends = [to_sib]
        chips = _other_chips(x, y)
        for j, (px, py) in enumerate(chips):
            sends.append(_remote(chip_buf.at[me], chip_buf.at[me], send_sems.at[1 + j], recv_sems.at[1 + j], (px, py, c)))
            sends[-1].start()
        for j, (px, py) in enumerate(chips):
            _remote(chip_buf.at[me], chip_buf.at[2 * px + py], send_sems.at[1 + j], recv_sems.at[1 + j],
                    (px, py, c)).wait_recv()
        o_ref[c] = ((chip_buf[0] + chip_buf[1]) + chip_buf[2]) + chip_buf[3]
        sends.append(_remote(o_ref.at[c], o_ref.at[c], send_sems.at[4], recv_sems.at[4], sib))
        sends[-1].start()
        _remote(o_ref.at[c], o_ref.at[1 - c], send_sems.at[4], recv_sems.at[4], sib).wait_recv()
        for cp in sends:
            cp.wait_send()

    vmem = pl.BlockSpec(memory_space=pltpu.VMEM)
    return pl.pallas_call(
        body, name="allreduce_small", in_specs=[vmem], out_specs=vmem, out_shape=jax.ShapeDtypeStruct(g.shape, F32),
        scratch_shapes=[pltpu.VMEM(g.shape[1:], F32), pltpu.VMEM((N_CHIP,) + g.shape[1:], F32),
                        pltpu.SemaphoreType.DMA((5,)), pltpu.SemaphoreType.DMA((5,))],
        compiler_params=_params(),
    )(g)


EW_ROWS = 512


def _ew(name, fn, ins, n_out, out_dtype=F32, out_slot=None, into=None):
    def dims(a):
        return a[0].shape[1:] if isinstance(a, tuple) else a.shape

    R, w = dims(ins[0])
    tr = EW_ROWS if R % EW_ROWS == 0 else R
    n_into = len(into) if into else 0

    def body(c_ref, *refs):
        outs = fn(*[r[...] for r in refs[:len(ins)]])
        for r, v in zip(refs[len(ins) + n_into:], outs):
            r[...] = v.astype(r.dtype)

    def lead_spec(l):
        if l == "c":
            return pl.BlockSpec((None, tr, w), lambda i, c_ref: (c_ref[0], i, 0))
        return pl.BlockSpec((None, tr, w), lambda i, c_ref, s=l: (s, i, 0))

    plain = pl.BlockSpec((tr, w), lambda i, c_ref: (i, 0))
    in_specs = [lead_spec(a[1]) if isinstance(a, tuple) else plain for a in ins] + [_ANY] * n_into
    out_spec = plain if out_slot is None else lead_spec(out_slot)
    out_shape = jax.ShapeDtypeStruct((R, w) if out_slot is None else (2, R, w), out_dtype)
    return pl.pallas_call(
        body, name=name,
        grid_spec=pltpu.PrefetchScalarGridSpec(num_scalar_prefetch=1, grid=(R // tr,), in_specs=in_specs,
                                               out_specs=[out_spec] * n_out),
        out_shape=[out_shape] * n_out, input_output_aliases={1 + len(ins) + j: j for j in range(n_into)},
        compiler_params=_params(dimension_semantics=("arbitrary",)),
    )(lax.axis_index("c").astype(jnp.int32).reshape(1), *[a[0] if isinstance(a, tuple) else a for a in ins],
      *(into or []))


def _adamw_fn(w, g, m, v):
    m = ADAM_B1 * m + (1.0 - ADAM_B1) * g
    v = ADAM_B2 * v + (1.0 - ADAM_B2) * (g * g)
    m_hat = m / (1.0 - ADAM_B1 ** ADAM_STEP)
    v_hat = v / (1.0 - ADAM_B2 ** ADAM_STEP)
    delta = -ADAM_LR * (m_hat / (jnp.sqrt(v_hat) + ADAM_EPS) + ADAM_WD * w)
    return delta, m, v


def _adamw(name, w, g, m, v):
    shape = w.shape
    two = lambda a: a.reshape(-1, shape[-1])
    return [o.reshape(shape) for o in _ew(name, _adamw_fn, [two(w), two(g), two(m), two(v)], 3)]


def _adamw_layer(name, l, w, g, m, v, into):
    k = w.shape[-1]
    three = lambda a: (a.reshape(N_LAYER, -1, k), l)
    fn = lambda w_, g_, m_, v_: _adamw_fn(w_, g_, m_, v_) + (g_,)
    outs = _ew(name, fn, [three(w), g.reshape(-1, k), three(m), three(v)], 4, out_slot=l,
               into=None if into is None else [a.reshape(N_LAYER, -1, k) for a in into])
    return [o.reshape(w.shape) for o in outs]


def _adamw_rows(name, l, w, g, m, v, into):
    _, R, k = w.shape
    n_into = len(into) if into else 0

    def body(*refs):
        w_ref, g_ref, m_ref, v_ref = refs[:4]
        d_out, m_out, v_out, g_out = refs[4 + n_into:]
        g_blk = g_ref[...]
        d_out[...], m_out[...], v_out[...] = _adamw_fn(w_ref[...], g_blk, m_ref[...], v_ref[...])
        g_out[...] = g_blk

    spec = pl.BlockSpec((None, EW_ROWS, k), lambda i: (l, i, 0))
    return pl.pallas_call(
        body, name=name, grid=(-(-R // EW_ROWS),),
        in_specs=[spec, pl.BlockSpec((EW_ROWS, k), lambda i: (i, 0)), spec, spec] + [_ANY] * n_into,
        out_specs=[spec] * 4, out_shape=[jax.ShapeDtypeStruct((N_LAYER, R, k), F32)] * 4,
        input_output_aliases={4 + j: j for j in range(n_into)},
        compiler_params=_params(dimension_semantics=("arbitrary",)),
    )(w, g, m, v, *(into or []))


_SMALL = [("norm_pre", (2, 1024)), ("norm_post", (2, 1024)), ("norm_mem", (2, 1024)), ("a_log", (2, 4)),
          ("dt_bias", (2, 4)), ("dn_norm", (2, 128)), ("gm_norm", (2, 512)), ("spatial_w", (2, 4, 128, 128)),
          ("spatial_b", (2, 4, 128)), ("sinks", (2, 8))]
_SMALL_ROWS = 200
_BIG = ["w_in", "conv_w", "w_mem_kv", "w_up", "w_out"]
_NAMES = ["norm_pre", "norm_post", "norm_mem", "w_in", "conv_w", "a_log", "dt_bias", "dn_norm", "gm_norm",
          "spatial_w", "spatial_b", "sinks", "w_mem_kv", "w_up", "w_out"]


def _size(shape):
    n = 1
    for s in shape:
        n *= s
    return n


_PACK_UNIT = 8 * 1024


def _pack_small(d):
    rows = []
    for n, shp in _SMALL:
        flat = d[n].reshape(-1)
        rows.append(jnp.pad(flat, (0, -flat.shape[0] % _PACK_UNIT)).reshape(-1, 1024))
    assert sum(r.shape[0] for r in rows) == _SMALL_ROWS
    return jnp.concatenate(rows, axis=0)


def _unpack_small(p):
    out, off = {}, 0
    for n, shp in _SMALL:
        k = -(-_size(shp) // _PACK_UNIT) * 8
        out[n] = p[off:off + k].reshape(-1)[:_size(shp)].reshape(shp)
        off += k
    return out


_HALF_SHAPE = {"w_in": (SHARD_PAD // 2, D_MODEL), "conv_w": (2, 3 * BRANCH_W // N_CHIP), "w_mem_kv": (128, D_MODEL),
               "w_up": (2, BRANCH_W, D_MODEL // N_CHIP), "w_out": (128, D_MODEL)}


def _chip_major(g):
    g = jnp.swapaxes(g, 0, 1)
    return g.reshape((N_CHIP, 2 * g.shape[2]) + g.shape[3:])


def _half_major(g):
    g = g.reshape((N_CHIP, 2, g.shape[1] // 2) + g.shape[2:])
    return jnp.swapaxes(g, 0, 1).astype(BF16)


def _weight_views(l, gathered, small):
    g_in, g_conv, g_kv, g_up, g_out = gathered
    return _layer_weights(
        l, _w_pad_from_slabs(g_in), _chip_major(g_conv).transpose(1, 0, 2).reshape(4, 3 * BRANCH_W),
        _chip_major(g_kv).reshape(D_MODEL, D_MODEL),
        _chip_major(g_up).transpose(1, 2, 0, 3).reshape(4, BRANCH_W, D_MODEL),
        _chip_major(g_out).reshape(D_MODEL, D_MODEL), small)


def _pair_sums(g):
    big = [_slabs_from_pad(g["w_pad"]),
           _half_major(g["conv"].reshape(4, N_CHIP, 3 * BRANCH_W // N_CHIP).transpose(1, 0, 2)),
           _half_major(g["w_mem_kv"].reshape(N_CHIP, D_MODEL // N_CHIP, D_MODEL)),
           _half_major(g["w_up"].reshape(4, BRANCH_W, N_CHIP, D_MODEL // N_CHIP).transpose(2, 0, 1, 3)),
           _half_major(g["w_out"].reshape(N_CHIP, D_MODEL // N_CHIP, D_MODEL))]
    add2 = lambda a, b: [a.astype(F32) + b.astype(F32)]
    pair = []
    for n, b, p in zip(_BIG, big, _pair_exchange(big)):
        k = b.shape[-1]
        pair.append(_ew("pair_sum_" + n, add2, [(b.reshape(2, -1, k), "c"), p.reshape(-1, k)], 1, BF16)[0]
                    .reshape(p.shape))
    return pair


def _chip_sums(landed, pair, me):
    add4 = lambda a, b, c_, d: [((a.astype(F32) + b.astype(F32)) + c_.astype(F32)) + d.astype(F32)]
    totals = []
    for n, r, q in zip(_BIG, landed, pair):
        r = _own_slot(r, lax.dynamic_index_in_dim(q, me, 0), me, 0)
        k = r.shape[-1]
        totals.append(_ew("chip_sum_" + n, add4, [(r.reshape(N_CHIP, -1, k), s) for s in range(N_CHIP)], 1,
                          out_slot="c")[0].reshape((2,) + r.shape[1:]))
    return totals


def _own_slot(buf, mine, me, axis):
    return lax.dynamic_update_index_in_dim(buf, mine.astype(buf.dtype), me, axis)


def kernel(x, mem, norm_pre, norm_post, norm_mem, w_in, conv_w, a_log, dt_bias, dn_norm, gm_norm, spatial_w, spatial_b, sinks, w_mem_kv, w_up, w_out, loss_target, m_norm_pre, m_norm_post, m_norm_mem, m_w_in, m_conv_w, m_a_log, m_dt_bias, m_dn_norm, m_gm_norm, m_spatial_w, m_spatial_b, m_sinks, m_w_mem_kv, m_w_up, m_w_out, v_norm_pre, v_norm_post, v_norm_mem, v_w_in, v_conv_w, v_a_log, v_dt_bias, v_dn_norm, v_gm_norm, v_spatial_w, v_spatial_b, v_sinks, v_w_mem_kv, v_w_up, v_w_out):
    w = dict(norm_pre=norm_pre, norm_post=norm_post, norm_mem=norm_mem, w_in=w_in, conv_w=conv_w, a_log=a_log,
             dt_bias=dt_bias, dn_norm=dn_norm, gm_norm=gm_norm, spatial_w=spatial_w, spatial_b=spatial_b, sinks=sinks,
             w_mem_kv=w_mem_kv, w_up=w_up, w_out=w_out)
    m = dict(norm_pre=m_norm_pre, norm_post=m_norm_post, norm_mem=m_norm_mem, w_in=m_w_in, conv_w=m_conv_w,
             a_log=m_a_log, dt_bias=m_dt_bias, dn_norm=m_dn_norm, gm_norm=m_gm_norm, spatial_w=m_spatial_w,
             spatial_b=m_spatial_b, sinks=m_sinks, w_mem_kv=m_w_mem_kv, w_up=m_w_up, w_out=m_w_out)
    v = dict(norm_pre=v_norm_pre, norm_post=v_norm_post, norm_mem=v_norm_mem, w_in=v_w_in, conv_w=v_conv_w,
             a_log=v_a_log, dt_bias=v_dt_bias, dn_norm=v_dn_norm, gm_norm=v_gm_norm, spatial_w=v_spatial_w,
             spatial_b=v_spatial_b, sinks=v_sinks, w_mem_kv=v_w_mem_kv, w_up=v_w_up, w_out=v_w_out)
    me = 2 * lax.axis_index("x") + lax.axis_index("y")

    w_in_t = jnp.pad(w_in.astype(BF16).transpose(0, 2, 1), ((0, 0), (0, SHARD_PAD - SHARD_IN), (0, 0)))
    local = dict(w_in=w_in_t, conv_w=conv_w, w_mem_kv=w_mem_kv.astype(BF16), w_up=w_up.astype(BF16),
                 w_out=w_out.astype(BF16))
    halves = lambda l: [local[n][l].reshape((2,) + _HALF_SHAPE[n]) for n in _BIG]
    own = lambda gathered, mine: [_own_slot(g, h[:, None], me, 1) for g, h in zip(gathered, mine)]
    g0 = own(_gather_weights(halves(0)), halves(0))
    started = _split_start("gather_l1_start", "gather", halves(1),
                           [jax.ShapeDtypeStruct((2, N_CHIP) + _HALF_SHAPE[n], local[n].dtype) for n in _BIG], g0[1])

    xl, meml = x[0], mem[0]
    W0 = _weight_views(0, g0, w)
    W0["norm_pre"] = W0["norm_pre"] + started[3][0, 0]
    x1, sv0 = _layer_fwd(xl, meml, W0)
    mine1, landed1 = _split_wait("gather_l1_wait", "gather", started, x1)
    W1 = _weight_views(1, own(_pair_forward(landed1), mine1), w)
    x2, sv1 = _layer_fwd(x1, meml, W1)
    dy, lrows = _rows_fwd("loss", _loss_fn, [Row(x2, D_MODEL, 0), Row(loss_target[0], D_MODEL, 0)], [],
                          [(D_MODEL, F32), (LANES, F32)], TR)
    loss = lax.psum(jnp.sum(lrows[:, 0]), ("x", "y", "c"))

    dx1, grads1 = _layer_bwd(dy, meml, W1, sv1)
    pair1 = _pair_sums(grads1)
    scattering = _split_start("scatter_l1_start", "scatter", pair1,
                              [jax.ShapeDtypeStruct(p.shape, p.dtype) for p in pair1], pair1[1])
    W0["norm_post"] = W0["norm_post"] + scattering[3][0, 0]
    dx, grads0 = _layer_bwd(dx1, meml, W0, sv0)
    pair1, landed1 = _split_wait("scatter_l1_wait", "scatter", scattering, dx)
    pair0 = _pair_sums(grads0)
    scattering = _split_start("scatter_l0_start", "scatter", pair0,
                              [jax.ShapeDtypeStruct(p.shape, p.dtype) for p in pair0], pair0[1])
    after_start = scattering[3][0, 0]
    grads = [grads0, grads1]

    small_local = dict(
        norm_pre=jnp.stack([g["norm_pre"][0] for g in grads]), norm_post=jnp.stack([g["norm_post"][0] for g in grads]),
        norm_mem=jnp.stack([g["norm_mem"][0] for g in grads]), a_log=jnp.stack([g["a_vec"][0, 4:8] for g in grads]),
        dt_bias=jnp.stack([g["dt_vec"][0, 4:8] for g in grads]), dn_norm=jnp.stack([g["dn_norm"][0] for g in grads]),
        gm_norm=jnp.stack([g["gm_norm"][0] for g in grads]), spatial_w=jnp.stack([g["spatial_w"] for g in grads]),
        spatial_b=jnp.stack([g["spatial_b"][:, 0, :] for g in grads]),
        sinks=jnp.stack([g["sink_vec"][0, :8] for g in grads]))
    packed = jnp.pad(_pack_small(small_local) + after_start, ((0, 8), (0, 0)))
    gsmall_packed = _allreduce_small(packed.reshape(2, -1, 1024)).reshape(-1, 1024)[:_SMALL_ROWS]

    d_s, m_s, v_s = _ew("adamw_small", _adamw_fn, [_pack_small(w), gsmall_packed, _pack_small(m), _pack_small(v)], 3)
    gsmall, dsmall, msmall, vsmall = (_unpack_small(p) for p in (gsmall_packed, d_s, m_s, v_s))
    g_o, d_o, m_o, v_o = dict(gsmall), dict(dsmall), dict(msmall), dict(vsmall)
    tr = lambda a: a.transpose(0, 2, 1)
    w_t, m_t, v_t = tr(w["w_in"]), tr(m["w_in"]), tr(v["w_in"])

    def update(l, totals, into):
        outs = {}
        for n, t in zip(_BIG, totals):
            g_l = t.reshape(local[n].shape[1:])
            if n == "w_in":
                outs[n] = _adamw_rows("adamw_" + n, l, w_t, g_l, m_t, v_t, into and into[n])
            else:
                outs[n] = _adamw_layer("adamw_" + n, l, w[n], g_l, m[n], v[n], into and into[n])
        return outs

    landed1[1] = landed1[1] + after_start.astype(landed1[1].dtype)
    outs1 = update(1, _pair_share(_chip_sums(landed1, pair1, me)), None)
    pair0, landed0 = _split_wait("scatter_l0_wait", "scatter", scattering, outs1["w_in"][0])
    outs = update(0, _pair_share(_chip_sums(landed0, pair0, me)), outs1)
    for n in _BIG:
        d_o[n], m_o[n], v_o[n], g_o[n] = [tr(o) for o in outs[n]] if n == "w_in" else outs[n]
    return (loss, dx[None], *[g_o[n] for n in _NAMES], *[d_o[n] for n in _NAMES], *[m_o[n] for n in _NAMES],
            *[v_o[n] for n in _NAMES])
```

```python
import collections
import functools

import jax
import jax.numpy as jnp
from jax import lax
from jax.experimental import pallas as pl
from jax.experimental.pallas import tpu as pltpu

F32 = jnp.float32
BF16 = jnp.bfloat16

D_MODEL = 1024
BRANCH_W = 512
MEM_LEN = 256
N_LAYER = 2
N_CHIP = 4
N_DEV = 8
EPS = 1e-6
NEG_INF = -1e30
DN_CHUNK = 64
LANES = 128
VMEM_LIMIT = 48 * 1024 * 1024

ADAM_LR, ADAM_B1, ADAM_B2, ADAM_EPS, ADAM_WD, ADAM_STEP = 0.001, 0.9, 0.999, 1e-08, 0.01, 10

N_PAD = 10240
O_GATE = 0
O_AQ, O_AK, O_AV, O_AZ = 4096, 4608, 5120, 5632
O_BUV, O_BZ = 6144, 7168
O_CKV, O_BA = 7680, 7936
O_CQ, O_CZ = 8192, 8704
O_MQ, O_MZ = 9216, 9728
O_MISC, W_MISC = O_CKV, 512
_PAD_SEGS = [(5896, 4096), (0, 512), (512, 512), (1024, 512), (1536, 512), (2056, 1024), (3080, 512),
             (4104, 128), (4232, 128), (2048, 8), (None, 120), (None, 128),
             (3592, 512), (4360, 512), (4872, 512), (5384, 512)]
D_IN = 9992
SHARD_IN = D_IN // N_CHIP


SHARD_PAD = 2560


def _pad_parts():
    parts, off = [], 0
    for s, n in _PAD_SEGS:
        a = s
        while s is not None and a < s + n:
            chip = a // SHARD_IN
            b = min(s + n, (chip + 1) * SHARD_IN)
            parts.append((chip, a - chip * SHARD_IN, off + a - s, b - a))
            a = b
        off += n
    return parts


PERM_ROWS = 512
PERM_SLACK = 32


def _permute_rows(name, src, parts, n_out, out_dtype):
    B, Z = PERM_ROWS, PERM_ROWS + PERM_SLACK
    w = src.shape[1]
    plans = []
    for blk in range(n_out // B):
        o, runs = blk * B, []
        for s, d, n in parts:
            lo, hi = max(d, o), min(d + n, o + B)
            if lo < hi:
                s0 = s + lo - d
                wa = s0 // 16 * 16
                wb = min(-(-(s0 + hi - lo) // 16) * 16, src.shape[0])
                runs.append((wa, wb - wa, s0 - (lo - o) - wa, lo - o, hi - o))
        plans.append(runs)
    max_runs = max(len(r) for r in plans)
    nblk = len(plans)

    def body(*refs):
        src_ref, out_ref, inbuf, obuf, insem, outsem = (refs[0],) + refs[-5:]

        def in_copies(blk):
            return [pltpu.make_async_copy(src_ref.at[pl.ds(wa, ws)], inbuf.at[blk % 2, r, pl.ds(0, ws)],
                                          insem.at[blk % 2, r]) for r, (wa, ws, _, _, _) in enumerate(plans[blk])]

        def out_copy(blk):
            return pltpu.make_async_copy(obuf.at[blk % 2], out_ref.at[pl.ds(blk * B, B)], outsem.at[blk % 2])

        for cp in in_copies(0):
            cp.start()
        rid = _iota((B, 1), 0)
        for blk in range(nblk):
            if blk + 1 < nblk:
                for cp in in_copies(blk + 1):
                    cp.start()
            for cp in in_copies(blk):
                cp.wait()
            val = jnp.zeros((B, w), F32)
            for r, (wa, ws, t, l0, l1) in enumerate(plans[blk]):
                win = jnp.concatenate([inbuf[blk % 2, r, pl.ds(0, ws)].astype(F32), jnp.zeros((Z - ws, w), F32)], axis=0)
                moved = pltpu.roll(win, (-t) % Z, 0)[:B]
                val = jnp.where((rid >= l0) & (rid < l1), moved, val)
            if blk >= 2:
                out_copy(blk - 2).wait()
            obuf[blk % 2] = val.astype(out_dtype)
            out_copy(blk).start()
        for blk in range(max(nblk - 2, 0), nblk):
            out_copy(blk).wait()

    return pl.pallas_call(
        body, name=name, in_specs=[_ANY], out_specs=_ANY, out_shape=jax.ShapeDtypeStruct((n_out, w), out_dtype),
        scratch_shapes=[pltpu.VMEM((2, max_runs, Z, w), src.dtype), pltpu.VMEM((2, B, w), out_dtype),
                        pltpu.SemaphoreType.DMA((2, max_runs)), pltpu.SemaphoreType.DMA((2,))],
        compiler_params=_params(),
    )(src)


def _slab_parts():
    h, out = SHARD_PAD // 2, []
    for chip, s, d, n in _pad_parts():
        a = s
        while a < s + n:
            half = a // h
            b = min(s + n, (half + 1) * h)
            out.append(((half * N_CHIP + chip) * h + a - half * h, d + a - s, b - a))
            a = b
    return out


def _w_pad_from_slabs(slabs):
    return _permute_rows("w_pad_rows", slabs.reshape(-1, slabs.shape[-1]), _slab_parts(), N_PAD, BF16)


def _slabs_from_pad(dw):
    slabs = _permute_rows("w_pad_grad_rows", dw, [(d, s, n) for s, d, n in _slab_parts()], N_CHIP * SHARD_PAD, BF16)
    return slabs.reshape(2, N_CHIP, SHARD_PAD // 2, dw.shape[1])


def _dot(a, b, dims, prec):
    if prec == "bf16":
        return lax.dot_general(a.astype(BF16), b.astype(BF16), (dims, ((), ())), preferred_element_type=F32)
    return lax.dot_general(a, b, (dims, ((), ())), precision=lax.Precision.HIGHEST, preferred_element_type=F32)


_NN, _NT, _TN = ((1,), (0,)), ((1,), (1,)), ((0,), (0,))


def _make_mm(prec):
    @jax.custom_vjp
    def nn(a, b):
        return _dot(a, b, _NN, prec)

    @jax.custom_vjp
    def nt(a, b):
        return _dot(a, b, _NT, prec)

    @jax.custom_vjp
    def tn(a, b):
        return _dot(a, b, _TN, prec)

    nn.defvjp(lambda a, b: (nn(a, b), (a, b)), lambda r, g: (nt(g, r[1]), tn(r[0], g)))
    nt.defvjp(lambda a, b: (nt(a, b), (a, b)), lambda r, g: (nn(g, r[1]), tn(g, r[0])))
    tn.defvjp(lambda a, b: (tn(a, b), (a, b)), lambda r, g: (nt(r[1], g), nn(r[0], g)))
    return nn, nt, tn


_nn16, _nt16, _tn16 = _make_mm("bf16")
_nn32, _nt32, _tn32 = _make_mm("f32")


def _make_slice(axis):
    @functools.partial(jax.custom_vjp, nondiff_argnums=(1, 2, 3))
    def sl(x, a, b, n):
        return x[a:b] if axis == 0 else x[:, a:b]

    def fwd(x, a, b, n):
        return sl(x, a, b, n), None

    def bwd(a, b, n, _, g):
        parts = []
        if a > 0:
            parts.append(jnp.zeros((a, g.shape[1]) if axis == 0 else (g.shape[0], a), g.dtype))
        parts.append(g)
        if n - b > 0:
            parts.append(jnp.zeros((n - b, g.shape[1]) if axis == 0 else (g.shape[0], n - b), g.dtype))
        return (jnp.concatenate(parts, axis=axis),)

    sl.defvjp(fwd, bwd)
    return sl


_sl0, _sl1 = _make_slice(0), _make_slice(1)


def _rowsl(x, a, b):
    return _sl0(x, a, b, x.shape[0])


def _cols(x, a, b):
    return _sl1(x, a, b, x.shape[1])


@functools.partial(jax.custom_vjp, nondiff_argnums=(1,))
def _rollr(x, s):
    return pltpu.roll(x, s, 0)


_rollr.defvjp(lambda x, s: (_rollr(x, s), None),
              lambda s, _, g: (pltpu.roll(g, g.shape[0] - s, 0),))


def _iota(shape, axis):
    return lax.broadcasted_iota(jnp.int32, shape, axis)


def _sigmoid(x):
    return lax.logistic(x)


def _silu(x):
    return x * _sigmoid(x)


def _gelu(x):
    return 0.5 * x * (1.0 + jnp.tanh(0.7978845608028654 * (x + 0.044715 * (x * x * x))))


def _softplus(x):
    return jnp.maximum(x, 0.0) + jnp.log(1.0 + jnp.exp(-jnp.abs(x)))


def _rms(x, g):
    return x * lax.rsqrt(jnp.mean(x * x, axis=-1, keepdims=True) + EPS) * g


def _lane_pick(x, lane):
    return jnp.sum(x * (_iota((1, x.shape[1]), 1) == lane).astype(F32), axis=1, keepdims=True)


Row = collections.namedtuple("Row", "arr w cb hb grad", defaults=(0, True))


def _full_spec(shape):
    return pl.BlockSpec(shape, lambda i, _n=len(shape): (0,) * _n)


def _load_params(refs):
    return [[p[g].astype(F32) for g in range(p.shape[0])] if len(p.shape) == 3 else p[...].astype(F32)
            for p in refs]


def _params(**kw):
    return pltpu.CompilerParams(vmem_limit_bytes=VMEM_LIMIT, **kw)


def _rows_fwd(name, fn, rows, params, outs, tr, carry=None):
    T = rows[0].arr.shape[0]
    n = T // tr
    halos = [r for r in rows if r.hb]
    nr, nh, npar, no = len(rows), len(halos), len(params), len(outs)

    def body(*refs):
        row_refs, halo_refs = refs[:nr], refs[nr:nr + nh]
        par_refs = refs[nr + nh:nr + nh + npar]
        out_refs = refs[nr + nh + npar:nr + nh + npar + no]
        rest = refs[nr + nh + npar + no:]
        first = pl.program_id(0) == 0
        cvals = None
        if carry is not None:
            csave_ref, carry_ref = rest

            @pl.when(first)
            def _():
                carry_ref[...] = jnp.zeros_like(carry_ref)

            cvals = [carry_ref[g] for g in range(carry[0])]
            for g in range(carry[0]):
                csave_ref[0, g] = cvals[g]
        c_out, o = fn(first, cvals, [r[...].astype(F32) for r in row_refs],
                      [h[...].astype(F32) for h in halo_refs], _load_params(par_refs))
        for r, v in zip(out_refs, o):
            r[...] = v.astype(r.dtype)
        if carry is not None:
            for g in range(carry[0]):
                carry_ref[g] = c_out[g]

    in_specs = [pl.BlockSpec((tr, r.w), lambda i, c=r.cb: (i, c)) for r in rows]
    in_specs += [pl.BlockSpec((r.hb, r.w), lambda i, c=r.cb, q=tr // r.hb: (jnp.maximum(i * q - 1, 0), c))
                 for r in halos]
    in_specs += [_full_spec(p.shape) for p in params]
    out_shape = [jax.ShapeDtypeStruct((T, w), dt) for w, dt in outs]
    out_specs = [pl.BlockSpec((tr, w), lambda i: (i, 0)) for w, _ in outs]
    scratch = []
    if carry is not None:
        out_shape.append(jax.ShapeDtypeStruct((n,) + carry, F32))
        out_specs.append(pl.BlockSpec((1,) + carry, lambda i: (i, 0, 0, 0)))
        scratch.append(pltpu.VMEM(carry, F32))
    return pl.pallas_call(
        body, name=name, grid=(n,), in_specs=in_specs, out_specs=out_specs, out_shape=out_shape,
        scratch_shapes=scratch, compiler_params=_params(dimension_semantics=("arbitrary",)),
    )(*[r.arr for r in rows], *[r.arr for r in halos], *params)


def _rows_bwd(name, fn, rows, params, douts, tr, carry=None, csave=None, dcols=None):
    T = rows[0].arr.shape[0]
    n = T // tr
    halos = [r for r in rows if r.hb]
    grows = [r for r in rows if r.grad is True]
    crows = [r for r in rows if r.grad == "cols"]
    wcols = sum(r.w for r in crows)
    nr, nh, npar, nd, ng = len(rows), len(halos), len(params), len(douts), len(grows)
    nc = 0 if carry is None else 1
    ncol = 1 if crows else 0
    nalias = 1 if (crows and dcols is not None) else 0

    def body(*refs):
        row_refs, halo_refs = refs[:nr], refs[nr:nr + nh]
        par_refs = refs[nr + nh:nr + nh + npar]
        k = nr + nh + npar
        csave_ref = refs[k] if nc else None
        dout_refs = refs[k + nc:k + nc + nd]
        k = k + nc + nd + nalias
        drow_refs = refs[k:k + ng]
        dcols_ref = refs[k + ng] if ncol else None
        dpar_refs = refs[k + ng + ncol:k + ng + ncol + npar]
        k = k + ng + ncol + npar
        dcarry_ref = refs[k] if nc else None
        hgrad_refs = refs[k + nc:]
        i = pl.program_id(0)
        first_tile = i == n - 1

        @pl.when(i == 0)
        def _():
            for r in dpar_refs:
                r[...] = jnp.zeros_like(r)
            for r in hgrad_refs:
                r[...] = jnp.zeros_like(r)
            if nc:
                dcarry_ref[...] = jnp.zeros_like(dcarry_ref)

        rv = [r[...].astype(F32) for r in row_refs]
        hv = [h[...].astype(F32) for h in halo_refs]
        pv = _load_params(par_refs)
        dov = [d[...].astype(F32) for d in dout_refs]
        if nc:
            cv = [csave_ref[0, g] for g in range(carry[0])]
            _, vjp = jax.vjp(lambda c, r, h, p: fn(first_tile, c, r, h, p), cv, rv, hv, pv)
            dc, dr, dh, dp = vjp(([dcarry_ref[g] for g in range(carry[0])], dov))
            for g in range(carry[0]):
                dcarry_ref[g] = dc[g]
        else:
            _, vjp = jax.vjp(lambda r, h, p: fn(first_tile, None, r, h, p)[1], rv, hv, pv)
            dr, dh, dp = vjp(dov)
        gi = hi = 0
        pieces = []
        for kk, r in enumerate(rows):
            d = dr[kk]
            if r.hb:
                carried = hgrad_refs[hi][...]
                d = d + (carried if tr == r.hb else
                         jnp.concatenate([jnp.zeros((tr - r.hb, r.w), F32), carried], axis=0))
                hgrad_refs[hi][...] = dh[hi]
                hi += 1
            if r.grad is True:
                drow_refs[gi][...] = d.astype(drow_refs[gi].dtype)
                gi += 1
            elif r.grad == "cols":
                pieces.append(d.astype(BF16))
        if ncol:
            dcols_ref[...] = pieces[0] if len(pieces) == 1 else jnp.concatenate(pieces, axis=1)
        for r, d in zip(dpar_refs, dp):
            if len(r.shape) == 3:
                for g in range(r.shape[0]):
                    r[g] += d[g]
            else:
                r[...] += d

    rev = lambda i: n - 1 - i
    in_specs = [pl.BlockSpec((tr, r.w), lambda i, c=r.cb: (rev(i), c)) for r in rows]
    in_specs += [pl.BlockSpec((r.hb, r.w), lambda i, c=r.cb, q=tr // r.hb: (jnp.maximum(rev(i) * q - 1, 0), c))
                 for r in halos]
    in_specs += [_full_spec(p.shape) for p in params]
    args = [r.arr for r in rows] + [r.arr for r in halos] + list(params)
    scratch = []
    if nc:
        in_specs.append(pl.BlockSpec((1,) + carry, lambda i: (rev(i), 0, 0, 0)))
        args.append(csave)
        scratch.append(pltpu.VMEM(carry, F32))
    douts = [d if isinstance(d, Row) else Row(d, d.shape[1], 0) for d in douts]
    in_specs += [pl.BlockSpec((tr, d.w), lambda i, c=d.cb: (rev(i), c)) for d in douts]
    args += [d.arr for d in douts]
    aliases = {}
    if nalias:
        aliases = {len(args): ng}
        in_specs.append(pl.BlockSpec(memory_space=pl.ANY))
        args.append(dcols)
    scratch += [pltpu.VMEM((r.hb, r.w), F32) for r in halos]
    out_shape = [jax.ShapeDtypeStruct((T, r.w), F32) for r in grows]
    out_specs = [pl.BlockSpec((tr, r.w), lambda i: (rev(i), 0)) for r in grows]
    if ncol:
        off = crows[0].cb * crows[0].w
        assert off % wcols == 0 and all(a.cb * a.w + a.w == b.cb * b.w for a, b in zip(crows, crows[1:]))
        out_shape.append(jax.ShapeDtypeStruct((T, N_PAD), BF16))
        out_specs.append(pl.BlockSpec((tr, wcols), lambda i, c=off // wcols: (rev(i), c)))
    out_shape += [jax.ShapeDtypeStruct(p.shape, F32) for p in params]
    out_specs += [_full_spec(p.shape) for p in params]
    res = pl.pallas_call(
        body, name=name, grid=(n,), in_specs=in_specs, out_specs=out_specs, out_shape=out_shape,
        scratch_shapes=scratch, input_output_aliases=aliases,
        compiler_params=_params(dimension_semantics=("arbitrary",)),
    )(*args)
    return list(res[:ng]), list(res[ng + ncol:]), (res[ng] if ncol else dcols)


def _fill_misc(dcols, dkv, dba, tr):
    T = dkv.shape[0]

    def body(kv_ref, ba_ref, _, o_ref):
        o_ref[...] = jnp.concatenate([kv_ref[...], ba_ref[...]], axis=1).astype(BF16)

    return pl.pallas_call(
        body, name="misc_bwd", grid=(T // tr,),
        in_specs=[pl.BlockSpec((tr, 256), lambda i: (i, 0)), pl.BlockSpec((tr, 256), lambda i: (i, 0)),
                  pl.BlockSpec(memory_space=pl.ANY)],
        out_specs=pl.BlockSpec((tr, W_MISC), lambda i: (i, O_MISC // W_MISC)),
        out_shape=jax.ShapeDtypeStruct((T, N_PAD), BF16), input_output_aliases={2: 0},
        compiler_params=_params(dimension_semantics=("arbitrary",)),
    )(dkv, dba, dcols)


def _up_bwd(ys, cols, dm, w_up):
    T, tr = dm.shape[0], UPB_TR

    def body(y_ref, gl_ref, dm_ref, w_ref, dy_ref, dgl_ref, dw_ref):
        @pl.when(pl.program_id(1) == 0)
        def _():
            dw_ref[...] = jnp.zeros_like(dw_ref)

        _, vjp = jax.vjp(lambda y, gl, w: _sigmoid(gl) * _nn16(y, w),
                         y_ref[...].astype(F32), gl_ref[...].astype(F32), w_ref[...].astype(F32))
        dy, dgl, dw = vjp(dm_ref[...])
        dy_ref[...] = dy
        dgl_ref[...] = dgl.astype(BF16)
        dw_ref[...] += dw

    branch_rows = lambda w: pl.BlockSpec((tr, w), lambda n, i: (i, n))
    weight = pl.BlockSpec((None, BRANCH_W, D_MODEL), lambda n, i: (n, 0, 0))
    return pl.pallas_call(
        body, name="up_bwd", grid=(4, T // tr),
        in_specs=[branch_rows(BRANCH_W), branch_rows(D_MODEL), pl.BlockSpec((tr, D_MODEL), lambda n, i: (i, 0)), weight],
        out_specs=[branch_rows(BRANCH_W), branch_rows(D_MODEL), weight],
        out_shape=[jax.ShapeDtypeStruct((T, 4 * BRANCH_W), F32), jax.ShapeDtypeStruct((T, N_PAD), BF16),
                   jax.ShapeDtypeStruct(w_up.shape, F32)],
        compiler_params=_params(dimension_semantics=("arbitrary", "arbitrary")),
    )(ys, cols, dm, w_up)


def _matmul(name, a, b, kind, out_dtype, tm, tn, tk):
    if kind == "tn":
        (K, M), N = a.shape, b.shape[1]
    else:
        (M, K), N = a.shape, (b.shape[0] if kind == "nt" else b.shape[1])
    tm, tn, tk = min(tm, M), min(tn, N), min(tk, K)
    nk = K // tk
    dims = {"nn": _NN, "nt": _NT, "tn": _TN}[kind]

    def body(a_ref, b_ref, o_ref, *acc):
        part = lax.dot_general(a_ref[...], b_ref[...], (dims, ((), ())), preferred_element_type=F32)
        if nk == 1:
            o_ref[...] = part.astype(o_ref.dtype)
            return
        acc_ref = acc[0] if acc else o_ref
        k = pl.program_id(2)

        @pl.when(k == 0)
        def _():
            acc_ref[...] = part

        @pl.when(k > 0)
        def _():
            acc_ref[...] += part

        if acc:
            @pl.when(k == nk - 1)
            def _():
                o_ref[...] = acc_ref[...].astype(o_ref.dtype)

    a_spec = pl.BlockSpec((tk, tm), lambda i, j, k: (k, i)) if kind == "tn" else pl.BlockSpec((tm, tk), lambda i, j, k: (i, k))
    b_spec = pl.BlockSpec((tn, tk), lambda i, j, k: (j, k)) if kind == "nt" else pl.BlockSpec((tk, tn), lambda i, j, k: (k, j))
    return pl.pallas_call(
        body, name=name, grid=(M // tm, N // tn, nk), in_specs=[a_spec, b_spec],
        out_specs=pl.BlockSpec((tm, tn), lambda i, j, k: (i, j)),
        out_shape=jax.ShapeDtypeStruct((M, N), out_dtype),
        scratch_shapes=[pltpu.VMEM((tm, tn), F32)] if nk > 1 and out_dtype != F32 else [],
        compiler_params=_params(dimension_semantics=("arbitrary", "arbitrary", "arbitrary")),
    )(a, b)


def _pre_fn(first, _, rows, halos, params):
    return None, [_rms(rows[0], params[0])]


def _pre_fn_res(first, _, rows, halos, params):
    return None, [_rms(rows[0], params[0]), rows[0]]


def _memkv_fn(first, _, rows, halos, params):
    g, w = params
    return None, [_nn16(_rms(rows[0], g), w)]


def _conv_silu(x, halo, w4, keep_halo):
    tr = x.shape[0]
    halo = halo * keep_halo
    rid = _iota((tr, 1), 0)
    acc = w4[3] * x
    for s in (1, 2, 3):
        hs = jnp.concatenate([_rollr(halo, s), jnp.zeros((tr - halo.shape[0], x.shape[1]), F32)], axis=0)
        acc = acc + w4[3 - s] * jnp.where(rid < s, hs, _rollr(x, s))
    return _silu(acc)


def _dn_fn(first, S, rows, halos, params):
    qp, kp, vp, z, ba = rows
    conv, a_vec, dt_vec, dnorm = params
    ba = _cols(ba, 0, LANES)
    tr = qp.shape[0]
    keep = jnp.where(first, 0.0, 1.0)
    q = _conv_silu(qp, halos[0], [conv[3 * j + 0] for j in range(4)], keep)
    k = _conv_silu(kp, halos[1], [conv[3 * j + 1] for j in range(4)], keep)
    v = _conv_silu(vp, halos[2], [conv[3 * j + 2] for j in range(4)], keep)
    qh, kh, vh = [], [], []
    for h in range(4):
        a, b = h * LANES, (h + 1) * LANES
        xq, xk = _cols(q, a, b), _cols(k, a, b)
        qh.append(xq * lax.rsqrt(jnp.sum(xq * xq, axis=1, keepdims=True) + EPS) * (LANES ** -0.5))
        kh.append(xk * lax.rsqrt(jnp.sum(xk * xk, axis=1, keepdims=True) + EPS))
        vh.append(_cols(v, a, b))
    beta_all = _sigmoid(ba)
    g_all = -jnp.exp(a_vec) * _softplus(ba + dt_vec)
    C = DN_CHUNK
    ii, jj = _iota((C, C), 0), _iota((C, C), 1)
    strict, incl = ii > jj, ii >= jj
    eye = (ii == jj).astype(F32)
    last_row = (_iota((C, 1), 0) == C - 1).astype(F32)
    n_chunk = tr // C
    pairs = [(c, h) for c in range(n_chunk) for h in range(4)]
    rows_of = lambda a, c: _rowsl(a, c * C, (c + 1) * C)
    gcs = [_nn32(incl.astype(F32), rows_of(g_all, c)) for c in range(n_chunk)]
    qc = {(c, h): rows_of(qh[h], c) for c, h in pairs}
    kc = {(c, h): rows_of(kh[h], c) for c, h in pairs}
    beta = {(c, h): _lane_pick(rows_of(beta_all, c), h) for c, h in pairs}
    gc = {(c, h): _lane_pick(gcs[c], 4 + h) for c, h in pairs}
    dec = {p: jnp.exp(jnp.where(incl, gc[p] - jnp.sum(eye * gc[p], axis=0, keepdims=True), 0.0)) for p in pairs}
    egc = {p: jnp.exp(gc[p]) for p in pairs}
    kb = {p: kc[p] * beta[p] for p in pairs}
    kq = {p: _nt16(jnp.concatenate([kb[p], qc[p]], axis=0), kc[p]) for p in pairs}
    P = {p: -jnp.where(strict, _rowsl(kq[p], 0, C) * dec[p], 0.0) for p in pairs}
    aqk = {p: jnp.where(incl, _rowsl(kq[p], C, 2 * C) * dec[p], 0.0) for p in pairs}
    tinv = {p: eye + P[p] for p in pairs}
    P = {p: _nn16(P[p], P[p]) for p in pairs}
    for j in range(5):
        if j < 4:
            pt = {p: _nn16(jnp.concatenate([P[p], tinv[p]], axis=0), P[p]) for p in pairs}
            tinv = {p: tinv[p] + _rowsl(pt[p], C, 2 * C) for p in pairs}
            P = {p: _rowsl(pt[p], 0, C) for p in pairs}
        else:
            tinv = {p: tinv[p] + _nn16(tinv[p], P[p]) for p in pairs}
    uw = {(c, h): _nn16(tinv[c, h], jnp.concatenate([rows_of(vh[h], c) * beta[c, h], kb[c, h] * egc[c, h]], axis=1))
          for c, h in pairs}
    S = list(S)
    ychunks = []
    for c in range(n_chunk):
        zc = rows_of(z, c)
        hs = range(4)
        ws = [_nn16(jnp.concatenate([_cols(uw[c, h], LANES, 2 * LANES), qc[c, h] * egc[c, h]], axis=0), S[h]) for h in hs]
        vnew = [_cols(uw[c, h], 0, LANES) - _rowsl(ws[h], 0, C) for h in hs]
        o = [_rowsl(ws[h], C, 2 * C) + _nn16(aqk[c, h], vnew[h]) for h in hs]
        glast = [jnp.sum(gc[c, h] * last_row, axis=0, keepdims=True) for h in hs]
        S = [S[h] * jnp.exp(glast[h]) + _tn16(kc[c, h] * jnp.exp(glast[h] - gc[c, h]), vnew[h]) for h in hs]
        ychunks.append(jnp.concatenate(
            [_rms(o[h], dnorm) * _silu(_cols(zc, h * LANES, (h + 1) * LANES)) for h in hs], axis=1))
    return S, [jnp.concatenate(ychunks, axis=0)]


def _gm_fn(first, _, rows, halos, params):
    uv, z = rows
    gnorm, ws, bs = params
    tr = uv.shape[0]
    guv = _gelu(uv)
    u = _cols(guv, 0, BRANCH_W)
    v = _rms(_cols(guv, BRANCH_W, 2 * BRANCH_W), gnorm)
    ii, jj = _iota((LANES, LANES), 0), _iota((LANES, LANES), 1)
    eye = (ii == jj).astype(F32)
    wsm = [jnp.where(ii >= jj, ws[g], 0.0) for g in range(4)]
    bcol = [jnp.sum(eye * bs[g], axis=1, keepdims=True) for g in range(4)]
    chunks = []
    for c in range(tr // LANES):
        vc = _rowsl(v, c * LANES, (c + 1) * LANES)
        chunks.append(jnp.concatenate(
            [_nn16(wsm[g], _cols(vc, g * LANES, (g + 1) * LANES)) + bcol[g] for g in range(4)], axis=1))
    return None, [u * jnp.concatenate(chunks, axis=0) * _silu(z)]


def _swa_fn(first, _, rows, halos, params):
    q, kvc, z = rows
    sink_vec = params[0]
    P = LANES
    kv = jnp.concatenate([halos[0], kvc], axis=0)
    k, v = _cols(kv, 0, P), _cols(kv, P, 2 * P)
    r, cc = _iota((P, P), 0), _iota((P, P), 1)
    lane = _iota((1, P), 1)
    dist = _iota((P, 2 * P), 0) + P - _iota((P, 2 * P), 1)
    kmin = jnp.where(first, P, 0)
    valid = (dist >= 0) & (dist < P) & (_iota((P, 2 * P), 1) >= kmin)
    halves = [(lane < 64).astype(F32), (lane >= 64).astype(F32)]
    k_v = jnp.concatenate([k, v], axis=0)
    kkvv = [_nn16(k_v, (r == kh * 64 + (cc & 63)).astype(F32)) for kh in range(2)]
    scores = [_nt16(jnp.concatenate([_cols(q, (2 * kh + g // 2) * P, (2 * kh + g // 2 + 1) * P) * halves[g % 2]
                                     for g in range(4)], axis=0), _rowsl(kkvv[kh], 0, 2 * P)) for kh in range(2)]
    probs = []
    for kh in range(2):
        ps = []
        for g in range(4):
            s = jnp.where(valid, _rowsl(scores[kh], g * P, (g + 1) * P) * 0.125, NEG_INF)
            sink = _lane_pick(sink_vec, kh * 4 + g)
            m = lax.stop_gradient(jnp.maximum(jnp.max(s, axis=1, keepdims=True), sink))
            e = jnp.exp(s - m)
            ps.append(e / (jnp.sum(e, axis=1, keepdims=True) + jnp.exp(sink - m)))
        probs.append(jnp.concatenate(ps, axis=0))
    outs = [_nn16(probs[kh], _rowsl(kkvv[kh], 2 * P, 4 * P)) for kh in range(2)]
    blocks = [_rowsl(outs[j // 2], (2 * (j % 2)) * P, (2 * (j % 2) + 1) * P) * halves[0]
              + _rowsl(outs[j // 2], (2 * (j % 2) + 1) * P, (2 * (j % 2) + 2) * P) * halves[1] for j in range(4)]
    return None, [jnp.concatenate(blocks, axis=1) * _silu(z)]


def _mem_fn(first, _, rows, halos, params):
    q, z = rows
    mkv = params[0]
    heads = [(h * LANES, (h + 1) * LANES) for h in range(4)]
    scores = [_nt16(_cols(q, a, b), _cols(mkv, a, b)) * (LANES ** -0.5) for a, b in heads]
    probs = []
    for s in scores:
        e = jnp.exp(s - lax.stop_gradient(jnp.max(s, axis=1, keepdims=True)))
        probs.append(e / jnp.sum(e, axis=1, keepdims=True))
    outs = [_nn16(p, _cols(mkv, BRANCH_W + a, BRANCH_W + b)) for p, (a, b) in zip(probs, heads)]
    return None, [jnp.concatenate(outs, axis=1) * _silu(z)]


def _up_fn(first, _, rows, halos, params):
    ys, gl, w_up = rows[:4], rows[4], params[0]
    merged = None
    for n in range(4):
        term = _sigmoid(_cols(gl, n * D_MODEL, (n + 1) * D_MODEL)) * _nn16(ys[n], w_up[n])
        merged = term if merged is None else merged + term
    return None, [merged]


def _out_fn(first, _, rows, halos, params):
    x, merged = rows
    w, g = params
    return None, [x + _rms(_nn16(merged, w), g)]


def _loss_fn(first, _, rows, halos, params):
    y, t = rows
    d = y - t
    lrow = 0.5 * jnp.mean(d * d, axis=1, keepdims=True)
    return None, [d * (1.0 / D_MODEL), jnp.broadcast_to(lrow, (y.shape[0], LANES))]


TR = 256
DN_TR = 256
UP_TR = 256
UPB_TR = 512
CONV_HALO = 16
CARRY = (4, LANES, LANES)


def _branch_rows(cols, g):
    hb = CONV_HALO
    a = [Row(cols, 512, O_AQ // 512, hb, g), Row(cols, 512, O_AK // 512, hb, g), Row(cols, 512, O_AV // 512, hb, g),
         Row(cols, 512, O_AZ // 512, 0, g), Row(cols, 256, O_BA // 256)]
    b = [Row(cols, 1024, O_BUV // 1024, 0, g), Row(cols, 512, O_BZ // 512, 0, g)]
    c = [Row(cols, 512, O_CQ // 512, 0, g), Row(cols, 256, O_CKV // 256, LANES), Row(cols, 512, O_CZ // 512, 0, g)]
    m = [Row(cols, 512, O_MQ // 512, 0, g), Row(cols, 512, O_MZ // 512, 0, g)]
    return a, b, c, m


def _layer_fwd(x, mem, W, late_weights=None):
    h = _rows_fwd("prenorm_fwd", _pre_fn, [Row(x, D_MODEL, 0)], [W["norm_pre"]], [(D_MODEL, BF16)], TR)[0]
    cols = _matmul("in_proj_fwd", h, W["w_pad"], "nt", BF16, 2048, 1024, 1024)
    if late_weights is not None:
        W = dict(W, **late_weights(cols))
    mem_kv = _rows_fwd("memkv_fwd", _memkv_fn, [Row(mem, D_MODEL, 0)], [W["norm_mem"], W["w_mem_kv"]],
                       [(D_MODEL, F32)], MEM_LEN)[0]
    ra, rb, rc, rm = _branch_rows(cols, True)
    y_a, csave = _rows_fwd("dn_fwd", _dn_fn, ra, [W["conv"], W["a_vec"], W["dt_vec"], W["dn_norm"]],
                           [(BRANCH_W, BF16)], DN_TR, CARRY)
    y_b = _rows_fwd("gm_fwd", _gm_fn, rb, [W["gm_norm"], W["spatial_w"], W["spatial_b"]], [(BRANCH_W, BF16)], TR)[0]
    y_c = _rows_fwd("swa_fwd", _swa_fn, rc, [W["sink_vec"]], [(BRANCH_W, BF16)], LANES)[0]
    y_m = _rows_fwd("mem_fwd", _mem_fn, rm, [mem_kv], [(BRANCH_W, BF16)], TR)[0]
    ys = [y_a, y_b, y_c, y_m]
    merged = _rows_fwd("up_fwd", _up_fn, [Row(y, BRANCH_W, 0) for y in ys] + [Row(cols, 4 * D_MODEL, 0)],
                       [W["w_up"]], [(D_MODEL, BF16)], UP_TR)[0]
    x_new = _rows_fwd("out_fwd", _out_fn, [Row(x, D_MODEL, 0), Row(merged, D_MODEL, 0)],
                      [W["w_out"], W["norm_post"]], [(D_MODEL, F32)], TR)[0]
    return x_new, dict(x=x, h=h, cols=cols, mem_kv=mem_kv, csave=csave, ys=ys, merged=merged), W


def _layer_bwd(dxn, mem, W, sv):
    x, cols = sv["x"], sv["cols"]
    (dx_res, dm), (dw_out, dnorm_post), _ = _rows_bwd(
        "out_bwd", _out_fn, [Row(x, D_MODEL, 0), Row(sv["merged"], D_MODEL, 0)], [W["w_out"], W["norm_post"]],
        [dxn], TR)
    dys, dcols, dw_up = _up_bwd(jnp.concatenate(sv["ys"], axis=1), cols, dm, W["w_up"])
    dys = [Row(dys, BRANCH_W, n) for n in range(4)]
    ra, rb, rc, rm = _branch_rows(cols, "cols")
    (dba,), (dconv, da_vec, ddt_vec, ddn_norm), dcols = _rows_bwd(
        "dn_bwd", _dn_fn, ra, [W["conv"], W["a_vec"], W["dt_vec"], W["dn_norm"]], [dys[0]], DN_TR, CARRY,
        sv["csave"], dcols=dcols)
    _, (dgm_norm, dws, dbs), dcols = _rows_bwd(
        "gm_bwd", _gm_fn, rb, [W["gm_norm"], W["spatial_w"], W["spatial_b"]], [dys[1]], TR, dcols=dcols)
    (dkv_c,), (dsink,), dcols = _rows_bwd("swa_bwd", _swa_fn, rc, [W["sink_vec"]], [dys[2]], LANES, dcols=dcols)
    _, (dmem_kv,), dcols = _rows_bwd("mem_bwd", _mem_fn, rm, [sv["mem_kv"]], [dys[3]], TR, dcols=dcols)
    dcols = _fill_misc(dcols, dkv_c, dba, TR)
    _, (dnorm_mem, dw_mem_kv), _ = _rows_bwd("memkv_bwd", _memkv_fn, [Row(mem, D_MODEL, 0, 0, False)],
                                             [W["norm_mem"], W["w_mem_kv"]], [dmem_kv], MEM_LEN)
    dw_pad = _matmul("in_proj_dw", dcols, sv["h"], "tn", F32, 1024, 1024, 2048)
    dh = _matmul("in_proj_dx", dcols, W["w_pad"], "nn", F32, 2048, 1024, 1024)
    (dx,), (dnorm_pre,), _ = _rows_bwd("prenorm_bwd", _pre_fn_res, [Row(x, D_MODEL, 0)], [W["norm_pre"]],
                                       [dh, dx_res], TR)
    grads = dict(norm_pre=dnorm_pre, norm_post=dnorm_post, norm_mem=dnorm_mem, w_pad=dw_pad, conv=dconv,
                 a_vec=da_vec, dt_vec=ddt_vec, dn_norm=ddn_norm, gm_norm=dgm_norm, spatial_w=dws, spatial_b=dbs,
                 sink_vec=dsink, w_mem_kv=dw_mem_kv, w_up=dw_up, w_out=dw_out)
    return dx, grads


def _lane_vec(v, off):
    return jnp.zeros((1, LANES), F32).at[0, off:off + v.shape[0]].set(v)


def _layer_weights(l, w_pad, conv_w, small, **late):
    return dict(
        late, w_pad=w_pad, conv=conv_w.reshape(4, 3, BRANCH_W).reshape(12, 1, BRANCH_W),
        norm_pre=small["norm_pre"][l][None], norm_post=small["norm_post"][l][None],
        norm_mem=small["norm_mem"][l][None],
        a_vec=_lane_vec(small["a_log"][l], 4), dt_vec=_lane_vec(small["dt_bias"][l], 4),
        dn_norm=small["dn_norm"][l][None], gm_norm=small["gm_norm"][l][None],
        spatial_w=small["spatial_w"][l], spatial_b=small["spatial_b"][l][:, None, :],
        sink_vec=_lane_vec(small["sinks"][l], 0))


_MESH = pl.DeviceIdType.MESH
_ANY = pl.BlockSpec(memory_space=pl.ANY)


def _position():
    return lax.axis_index("x"), lax.axis_index("y"), lax.axis_index("c")


def _remote(src, dst, send_sem, recv_sem, dev):
    return pltpu.make_async_remote_copy(src_ref=src, dst_ref=dst, send_sem=send_sem, recv_sem=recv_sem,
                                        device_id=dev, device_id_type=_MESH)


def _hbm_call(name, body, arrs, out_shapes, sems, aliases=None):
    return pl.pallas_call(
        body, name=name, in_specs=[_ANY] * len(arrs), out_specs=[_ANY] * len(out_shapes), out_shape=out_shapes,
        scratch_shapes=[pltpu.SemaphoreType.DMA((k,)) for k in sems], input_output_aliases=aliases or {},
        compiler_params=pltpu.CompilerParams(has_side_effects=True),
    )(*arrs)


def _other_chips(x, y):
    return [(1 - x, y), (x, 1 - y), (1 - x, 1 - y)]


def _gather_weights(arrs):
    n = len(arrs)

    def body(*refs):
        ins, outs = refs[:n], refs[n:2 * n]
        ici_send, ici_recv, d2d_send, d2d_recv = refs[2 * n:]
        x, y, c = _position()
        me = 2 * x + y
        chips = _other_chips(x, y)
        sends = []
        for a in range(n):
            for j, (px, py) in enumerate(chips):
                sends.append(_remote(ins[a].at[c], outs[a].at[c, me], ici_send.at[3 * a + j], ici_recv.at[3 * a + j],
                                     (px, py, c)))
                sends[-1].start()
        for a in range(n):
            for j, (px, py) in enumerate(chips):
                slab = outs[a].at[c, 2 * px + py]
                _remote(ins[a].at[c], slab, ici_send.at[3 * a + j], ici_recv.at[3 * a + j], (px, py, c)).wait_recv()
                sends.append(_remote(slab, slab, d2d_send.at[3 * a + j], d2d_recv.at[3 * a + j], (x, y, 1 - c)))
                sends[-1].start()
        for a in range(n):
            for j, (px, py) in enumerate(chips):
                slab = outs[a].at[1 - c, 2 * px + py]
                _remote(slab, slab, d2d_send.at[3 * a + j], d2d_recv.at[3 * a + j], (x, y, 1 - c)).wait_recv()
        for cp in sends:
            cp.wait_send()

    return _hbm_call("gather_weights", body, arrs,
                     [jax.ShapeDtypeStruct((N_LAYER, N_CHIP) + a.shape[1:], a.dtype) for a in arrs], [3 * n] * 4)


def _pair_exchange(arrs):
    n = len(arrs)

    def body(*refs):
        ins, outs = refs[:n], refs[n:2 * n]
        send_sems, recv_sems = refs[2 * n:]
        x, y, c = _position()
        cps = [_remote(ins[a].at[1 - c], outs[a], send_sems.at[a], recv_sems.at[a], (x, y, 1 - c)) for a in range(n)]
        for cp in cps:
            cp.start()
        for cp in cps:
            cp.wait_recv()
        for cp in cps:
            cp.wait_send()

    return _hbm_call("pair_exchange", body, arrs, [jax.ShapeDtypeStruct(a.shape[1:], a.dtype) for a in arrs], [n, n])


def _chip_scatter(arrs):
    n = len(arrs)

    def body(*refs):
        ins, outs = refs[:n], refs[n:2 * n]
        send_sems, recv_sems = refs[2 * n:]
        x, y, c = _position()
        me = 2 * x + y
        sends = []
        for a in range(n):
            for j, (px, py) in enumerate(_other_chips(x, y)):
                sends.append(_remote(ins[a].at[2 * px + py], outs[a].at[me], send_sems.at[3 * a + j],
                                     recv_sems.at[3 * a + j], (px, py, c)))
                sends[-1].start()
        for a in range(n):
            for j, (px, py) in enumerate(_other_chips(x, y)):
                _remote(ins[a].at[me], outs[a].at[2 * px + py], send_sems.at[3 * a + j], recv_sems.at[3 * a + j],
                        (px, py, c)).wait_recv()
        for cp in sends:
            cp.wait_send()

    return _hbm_call("chip_scatter", body, arrs, [jax.ShapeDtypeStruct(a.shape, a.dtype) for a in arrs],
                     [3 * n, 3 * n])


def _pair_share(arrs):
    n = len(arrs)

    def body(*refs):
        ins, outs = refs[:n], refs[n:2 * n]
        send_sems, recv_sems = refs[2 * n:]
        x, y, c = _position()
        cps = [_remote(ins[a].at[c], outs[a].at[c], send_sems.at[a], recv_sems.at[a], (x, y, 1 - c)) for a in range(n)]
        for cp in cps:
            cp.start()
        for a in range(n):
            _remote(ins[a].at[c], outs[a].at[1 - c], send_sems.at[a], recv_sems.at[a], (x, y, 1 - c)).wait_recv()
        for cp in cps:
            cp.wait_send()

    return _hbm_call("pair_share", body, arrs, [jax.ShapeDtypeStruct(a.shape, a.dtype) for a in arrs], [n, n],
                     {a: a for a in range(n)})


def _pair_forward(arrs):
    n = len(arrs)

    def body(*refs):
        ins, outs = refs[:n], refs[n:2 * n]
        send_sems, recv_sems = refs[2 * n:]
        x, y, c = _position()
        sends = []
        for a in range(n):
            for j, (px, py) in enumerate(_other_chips(x, y)):
                sends.append(_remote(ins[a].at[c, 2 * px + py], outs[a].at[c, 2 * px + py], send_sems.at[3 * a + j],
                                     recv_sems.at[3 * a + j], (x, y, 1 - c)))
                sends[-1].start()
        for a in range(n):
            for j, (px, py) in enumerate(_other_chips(x, y)):
                slab = outs[a].at[1 - c, 2 * px + py]
                _remote(slab, slab, send_sems.at[3 * a + j], recv_sems.at[3 * a + j], (x, y, 1 - c)).wait_recv()
        for cp in sends:
            cp.wait_send()

    return _hbm_call("pair_forward", body, arrs, [jax.ShapeDtypeStruct(a.shape, a.dtype) for a in arrs],
                     [3 * n, 3 * n], {a: a for a in range(n)})


_HBM = pl.BlockSpec(memory_space=pltpu.HBM)
_SEM = pl.BlockSpec(memory_space=pltpu.SEMAPHORE)
_EFFECT = pltpu.SideEffectType.DATAFLOW_SIDE_EFFECTING


def _chip_copies(kind, srcs, lands, send_sems, recv_sems):
    x, y, c = _position()
    me = 2 * x + y
    sends, recvs = [], []
    for a in range(len(srcs)):
        for j, (px, py) in enumerate(_other_chips(x, y)):
            s, sems, dev = 2 * px + py, (send_sems.at[3 * a + j], recv_sems.at[3 * a + j]), (px, py, c)
            if kind == "gather":
                sends.append(_remote(srcs[a].at[c], lands[a].at[c, me], *sems, dev))
                recvs.append(_remote(srcs[a].at[c], lands[a].at[c, s], *sems, dev))
            else:
                sends.append(_remote(srcs[a].at[s], lands[a].at[me], *sems, dev))
                recvs.append(_remote(srcs[a].at[me], lands[a].at[s], *sems, dev))
    return sends, recvs


def _split_start(name, kind, srcs, land_shapes, after):
    n = len(srcs)

    def body(*refs):
        sends, _ = _chip_copies(kind, refs[:n], refs[n:2 * n], refs[2 * n + 1], refs[2 * n + 2])
        for cp in sends:
            cp.start()
        refs[-1][...] = jnp.zeros_like(refs[-1])

    hbm = lambda a: pltpu.with_memory_space_constraint(a, pltpu.HBM)
    lands = [lax.empty(s.shape, s.dtype) for s in land_shapes]
    outs = pl.pallas_call(
        body, name=name, in_specs=[_HBM] * (2 * n) + [_ANY],
        out_specs=[_SEM, _SEM] + [_HBM] * (2 * n) + [pl.BlockSpec(memory_space=pltpu.VMEM)],
        out_shape=[pltpu.SemaphoreType.DMA((3 * n,)), pltpu.SemaphoreType.DMA((3 * n,))]
        + [pltpu.HBM(a.shape, a.dtype) for a in list(srcs) + lands] + [jax.ShapeDtypeStruct((8, LANES), F32)],
        input_output_aliases={i: 2 + i for i in range(2 * n)},
        compiler_params=pltpu.CompilerParams(has_side_effects=_EFFECT),
    )(*[hbm(a) for a in srcs], *[hbm(a) for a in lands], after)
    return outs[0], outs[1], list(outs[2:2 + 2 * n]), outs[-1]


def _split_wait(name, kind, started, after):
    send_sems, recv_sems, thru, _ = started
    n = len(thru) // 2

    def body(*refs):
        sends, recvs = _chip_copies(kind, refs[:n], refs[n:2 * n], refs[2 * n], refs[2 * n + 1])
        for cp in sends:
            cp.wait_send()
        for cp in recvs:
            cp.wait_recv()

    outs = pl.pallas_call(
        body, name=name, in_specs=[_HBM] * (2 * n) + [_SEM, _SEM, _ANY], out_specs=[_HBM] * (2 * n),
        out_shape=[pltpu.HBM(a.shape, a.dtype) for a in thru], input_output_aliases={i: i for i in range(2 * n)},
        compiler_params=pltpu.CompilerParams(has_side_effects=_EFFECT),
    )(*thru, send_sems, recv_sems, after)
    return list(outs[:n]), list(outs[n:])


def _allreduce_small(g):
    def body(g_ref, o_ref, pair_buf, chip_buf, send_sems, recv_sems):
        x, y, c = _position()
        me = 2 * x + y
        sib = (x, y, 1 - c)
        to_sib = _remote(g_ref.at[1 - c], pair_buf, send_sems.at[0], recv_sems.at[0], sib)
        to_sib.start()
        to_sib.wait_recv()
        chip_buf[me] = g_ref[c] + pair_buf[...]
        sends = [to_sib]
        chips = _other_chips(x, y)
        for j, (px, py) in enumerate(chips):
            sends.append(_remote(chip_buf.at[me], chip_buf.at[me], send_sems.at[1 + j], recv_sems.at[1 + j], (px, py, c)))
            sends[-1].start()
        for j, (px, py) in enumerate(chips):
            _remote(chip_buf.at[me], chip_buf.at[2 * px + py], send_sems.at[1 + j], recv_sems.at[1 + j],
                    (px, py, c)).wait_recv()
        o_ref[c] = ((chip_buf[0] + chip_buf[1]) + chip_buf[2]) + chip_buf[3]
        sends.append(_remote(o_ref.at[c], o_ref.at[c], send_sems.at[4], recv_sems.at[4], sib))
        sends[-1].start()
        _remote(o_ref.at[c], o_ref.at[1 - c], send_sems.at[4], recv_sems.at[4], sib).wait_recv()
        for cp in sends:
            cp.wait_send()

    vmem = pl.BlockSpec(memory_space=pltpu.VMEM)
    return pl.pallas_call(
        body, name="allreduce_small", in_specs=[vmem], out_specs=vmem, out_shape=jax.ShapeDtypeStruct(g.shape, F32),
        scratch_shapes=[pltpu.VMEM(g.shape[1:], F32), pltpu.VMEM((N_CHIP,) + g.shape[1:], F32),
                        pltpu.SemaphoreType.DMA((5,)), pltpu.SemaphoreType.DMA((5,))],
        compiler_params=_params(),
    )(g)


EW_ROWS = 512


def _ew(name, fn, ins, n_out, out_dtype=F32, out_slot=None, into=None):
    def dims(a):
        return a[0].shape[1:] if isinstance(a, tuple) else a.shape

    R, w = dims(ins[0])
    tr = EW_ROWS if R % EW_ROWS == 0 else R
    n_into = len(into) if into else 0

    def body(c_ref, *refs):
        outs = fn(*[r[...] for r in refs[:len(ins)]])
        for r, v in zip(refs[len(ins) + n_into:], outs):
            r[...] = v.astype(r.dtype)

    def lead_spec(l):
        if l == "c":
            return pl.BlockSpec((None, tr, w), lambda i, c_ref: (c_ref[0], i, 0))
        return pl.BlockSpec((None, tr, w), lambda i, c_ref, s=l: (s, i, 0))

    plain = pl.BlockSpec((tr, w), lambda i, c_ref: (i, 0))
    in_specs = [lead_spec(a[1]) if isinstance(a, tuple) else plain for a in ins] + [_ANY] * n_into
    out_spec = plain if out_slot is None else lead_spec(out_slot)
    out_shape = jax.ShapeDtypeStruct((R, w) if out_slot is None else (2, R, w), out_dtype)
    return pl.pallas_call(
        body, name=name,
        grid_spec=pltpu.PrefetchScalarGridSpec(num_scalar_prefetch=1, grid=(R // tr,), in_specs=in_specs,
                                               out_specs=[out_spec] * n_out),
        out_shape=[out_shape] * n_out, input_output_aliases={1 + len(ins) + j: j for j in range(n_into)},
        compiler_params=_params(dimension_semantics=("arbitrary",)),
    )(lax.axis_index("c").astype(jnp.int32).reshape(1), *[a[0] if isinstance(a, tuple) else a for a in ins],
      *(into or []))


def _adamw_fn(w, g, m, v):
    m = ADAM_B1 * m + (1.0 - ADAM_B1) * g
    v = ADAM_B2 * v + (1.0 - ADAM_B2) * (g * g)
    m_hat = m / (1.0 - ADAM_B1 ** ADAM_STEP)
    v_hat = v / (1.0 - ADAM_B2 ** ADAM_STEP)
    delta = -ADAM_LR * (m_hat / (jnp.sqrt(v_hat) + ADAM_EPS) + ADAM_WD * w)
    return delta, m, v


def _adamw(name, w, g, m, v):
    shape = w.shape
    two = lambda a: a.reshape(-1, shape[-1])
    return [o.reshape(shape) for o in _ew(name, _adamw_fn, [two(w), two(g), two(m), two(v)], 3)]


def _adamw_layer(name, l, w, g, m, v, into):
    k = w.shape[-1]
    three = lambda a: (a.reshape(N_LAYER, -1, k), l)
    fn = lambda w_, g_, m_, v_: _adamw_fn(w_, g_, m_, v_) + (g_,)
    outs = _ew(name, fn, [three(w), g.reshape(-1, k), three(m), three(v)], 4, out_slot=l,
               into=None if into is None else [a.reshape(N_LAYER, -1, k) for a in into])
    return [o.reshape(w.shape) for o in outs]


def _adamw_rows(name, l, w, g, m, v, into):
    _, R, k = w.shape
    n_into = len(into) if into else 0

    def body(*refs):
        w_ref, g_ref, m_ref, v_ref = refs[:4]
        d_out, m_out, v_out, g_out = refs[4 + n_into:]
        g_blk = g_ref[...]
        d_out[...], m_out[...], v_out[...] = _adamw_fn(w_ref[...], g_blk, m_ref[...], v_ref[...])
        g_out[...] = g_blk

    spec = pl.BlockSpec((None, EW_ROWS, k), lambda i: (l, i, 0))
    return pl.pallas_call(
        body, name=name, grid=(-(-R // EW_ROWS),),
        in_specs=[spec, pl.BlockSpec((EW_ROWS, k), lambda i: (i, 0)), spec, spec] + [_ANY] * n_into,
        out_specs=[spec] * 4, out_shape=[jax.ShapeDtypeStruct((N_LAYER, R, k), F32)] * 4,
        input_output_aliases={4 + j: j for j in range(n_into)},
        compiler_params=_params(dimension_semantics=("arbitrary",)),
    )(w, g, m, v, *(into or []))


_SMALL = [("norm_pre", (2, 1024)), ("norm_post", (2, 1024)), ("norm_mem", (2, 1024)), ("a_log", (2, 4)),
          ("dt_bias", (2, 4)), ("dn_norm", (2, 128)), ("gm_norm", (2, 512)), ("spatial_w", (2, 4, 128, 128)),
          ("spatial_b", (2, 4, 128)), ("sinks", (2, 8))]
_SMALL_ROWS = 200
_BIG = ["w_in", "conv_w", "w_mem_kv", "w_up", "w_out"]
_NAMES = ["norm_pre", "norm_post", "norm_mem", "w_in", "conv_w", "a_log", "dt_bias", "dn_norm", "gm_norm",
          "spatial_w", "spatial_b", "sinks", "w_mem_kv", "w_up", "w_out"]


def _size(shape):
    n = 1
    for s in shape:
        n *= s
    return n


_PACK_UNIT = 8 * 1024


def _pack_small(d):
    rows = []
    for n, shp in _SMALL:
        flat = d[n].reshape(-1)
        rows.append(jnp.pad(flat, (0, -flat.shape[0] % _PACK_UNIT)).reshape(-1, 1024))
    assert sum(r.shape[0] for r in rows) == _SMALL_ROWS
    return jnp.concatenate(rows, axis=0)


def _unpack_small(p):
    out, off = {}, 0
    for n, shp in _SMALL:
        k = -(-_size(shp) // _PACK_UNIT) * 8
        out[n] = p[off:off + k].reshape(-1)[:_size(shp)].reshape(shp)
        off += k
    return out


_HALF_SHAPE = {"w_in": (SHARD_PAD // 2, D_MODEL), "conv_w": (2, 3 * BRANCH_W // N_CHIP), "w_mem_kv": (128, D_MODEL),
               "w_up": (2, BRANCH_W, D_MODEL // N_CHIP), "w_out": (128, D_MODEL)}


def _chip_major(g):
    g = jnp.swapaxes(g, 0, 1)
    return g.reshape((N_CHIP, 2 * g.shape[2]) + g.shape[3:])


def _half_major(g):
    g = g.reshape((N_CHIP, 2, g.shape[1] // 2) + g.shape[2:])
    return jnp.swapaxes(g, 0, 1).astype(BF16)


N_EARLY = 2


def _early_views(l, g_in, g_conv, small):
    return _layer_weights(l, _w_pad_from_slabs(g_in),
                          _chip_major(g_conv).transpose(1, 0, 2).reshape(4, 3 * BRANCH_W), small)


def _late_views(g_kv, g_up, g_out):
    return dict(w_mem_kv=_chip_major(g_kv).reshape(D_MODEL, D_MODEL),
                w_up=_chip_major(g_up).transpose(1, 2, 0, 3).reshape(4, BRANCH_W, D_MODEL),
                w_out=_chip_major(g_out).reshape(D_MODEL, D_MODEL))


def _pair_sums(g):
    big = [_slabs_from_pad(g["w_pad"]),
           _half_major(g["conv"].reshape(4, N_CHIP, 3 * BRANCH_W // N_CHIP).transpose(1, 0, 2)),
           _half_major(g["w_mem_kv"].reshape(N_CHIP, D_MODEL // N_CHIP, D_MODEL)),
           _half_major(g["w_up"].reshape(4, BRANCH_W, N_CHIP, D_MODEL // N_CHIP).transpose(2, 0, 1, 3)),
           _half_major(g["w_out"].reshape(N_CHIP, D_MODEL // N_CHIP, D_MODEL))]
    add2 = lambda a, b: [a.astype(F32) + b.astype(F32)]
    pair = []
    for n, b, p in zip(_BIG, big, _pair_exchange(big)):
        k = b.shape[-1]
        pair.append(_ew("pair_sum_" + n, add2, [(b.reshape(2, -1, k), "c"), p.reshape(-1, k)], 1, BF16)[0]
                    .reshape(p.shape))
    return pair


def _chip_sums(landed, pair, me):
    add4 = lambda a, b, c_, d: [((a.astype(F32) + b.astype(F32)) + c_.astype(F32)) + d.astype(F32)]
    totals = []
    for n, r, q in zip(_BIG, landed, pair):
        r = _own_slot(r, lax.dynamic_index_in_dim(q, me, 0), me, 0)
        k = r.shape[-1]
        totals.append(_ew("chip_sum_" + n, add4, [(r.reshape(N_CHIP, -1, k), s) for s in range(N_CHIP)], 1,
                          out_slot="c")[0].reshape((2,) + r.shape[1:]))
    return totals


def _own_slot(buf, mine, me, axis):
    return lax.dynamic_update_index_in_dim(buf, mine.astype(buf.dtype), me, axis)


def kernel(x, mem, norm_pre, norm_post, norm_mem, w_in, conv_w, a_log, dt_bias, dn_norm, gm_norm, spatial_w, spatial_b, sinks, w_mem_kv, w_up, w_out, loss_target, m_norm_pre, m_norm_post, m_norm_mem, m_w_in, m_conv_w, m_a_log, m_dt_bias, m_dn_norm, m_gm_norm, m_spatial_w, m_spatial_b, m_sinks, m_w_mem_kv, m_w_up, m_w_out, v_norm_pre, v_norm_post, v_norm_mem, v_w_in, v_conv_w, v_a_log, v_dt_bias, v_dn_norm, v_gm_norm, v_spatial_w, v_spatial_b, v_sinks, v_w_mem_kv, v_w_up, v_w_out):
    w = dict(norm_pre=norm_pre, norm_post=norm_post, norm_mem=norm_mem, w_in=w_in, conv_w=conv_w, a_log=a_log,
             dt_bias=dt_bias, dn_norm=dn_norm, gm_norm=gm_norm, spatial_w=spatial_w, spatial_b=spatial_b, sinks=sinks,
             w_mem_kv=w_mem_kv, w_up=w_up, w_out=w_out)
    m = dict(norm_pre=m_norm_pre, norm_post=m_norm_post, norm_mem=m_norm_mem, w_in=m_w_in, conv_w=m_conv_w,
             a_log=m_a_log, dt_bias=m_dt_bias, dn_norm=m_dn_norm, gm_norm=m_gm_norm, spatial_w=m_spatial_w,
             spatial_b=m_spatial_b, sinks=m_sinks, w_mem_kv=m_w_mem_kv, w_up=m_w_up, w_out=m_w_out)
    v = dict(norm_pre=v_norm_pre, norm_post=v_norm_post, norm_mem=v_norm_mem, w_in=v_w_in, conv_w=v_conv_w,
             a_log=v_a_log, dt_bias=v_dt_bias, dn_norm=v_dn_norm, gm_norm=v_gm_norm, spatial_w=v_spatial_w,
             spatial_b=v_spatial_b, sinks=v_sinks, w_mem_kv=v_w_mem_kv, w_up=v_w_up, w_out=v_w_out)
    me = 2 * lax.axis_index("x") + lax.axis_index("y")

    w_in_t = jnp.pad(w_in.astype(BF16).transpose(0, 2, 1), ((0, 0), (0, SHARD_PAD - SHARD_IN), (0, 0)))
    local = dict(w_in=w_in_t, conv_w=conv_w, w_mem_kv=w_mem_kv.astype(BF16), w_up=w_up.astype(BF16),
                 w_out=w_out.astype(BF16))
    halves = lambda l: [local[n][l].reshape((2,) + _HALF_SHAPE[n]) for n in _BIG]
    own = lambda gathered, mine: [_own_slot(g, h[:, None], me, 1) for g, h in zip(gathered, mine)]
    lands = [jax.ShapeDtypeStruct((2, N_CHIP) + _HALF_SHAPE[n], local[n].dtype) for n in _BIG]
    h0 = halves(0)
    g0 = own(_gather_weights(h0[:N_EARLY]), h0[:N_EARLY])
    rest0 = _split_start("gather_l0_rest_start", "gather", h0[N_EARLY:], lands[N_EARLY:], g0[1])
    started = _split_start("gather_l1_start", "gather", halves(1), lands, rest0[3])

    xl, meml = x[0], mem[0]
    W0 = _early_views(0, g0[0], g0[1], w)
    W0["norm_pre"] = W0["norm_pre"] + started[3][0, 0]

    def late0(cols):
        mine, landed = _split_wait("gather_l0_rest_wait", "gather", rest0, cols)
        return _late_views(*own(_pair_forward(landed), mine))

    x1, sv0, W0 = _layer_fwd(xl, meml, W0, late0)
    mine1, landed1 = _split_wait("gather_l1_wait", "gather", started, x1)
    g1 = own(_pair_forward(landed1), mine1)
    W1 = dict(_early_views(1, g1[0], g1[1], w), **_late_views(*g1[N_EARLY:]))
    x2, sv1, _ = _layer_fwd(x1, meml, W1)
    dy, lrows = _rows_fwd("loss", _loss_fn, [Row(x2, D_MODEL, 0), Row(loss_target[0], D_MODEL, 0)], [],
                          [(D_MODEL, F32), (LANES, F32)], TR)
    loss = lax.psum(jnp.sum(lrows[:, 0]), ("x", "y", "c"))

    dx1, grads1 = _layer_bwd(dy, meml, W1, sv1)
    pair1 = _pair_sums(grads1)
    scattering = _split_start("scatter_l1_start", "scatter", pair1,
                              [jax.ShapeDtypeStruct(p.shape, p.dtype) for p in pair1], pair1[1])
    W0["norm_post"] = W0["norm_post"] + scattering[3][0, 0]
    dx, grads0 = _layer_bwd(dx1, meml, W0, sv0)
    pair1, landed1 = _split_wait("scatter_l1_wait", "scatter", scattering, dx)
    pair0 = _pair_sums(grads0)
    scattering = _split_start("scatter_l0_start", "scatter", pair0,
                              [jax.ShapeDtypeStruct(p.shape, p.dtype) for p in pair0], pair0[1])
    after_start = scattering[3][0, 0]
    grads = [grads0, grads1]

    small_local = dict(
        norm_pre=jnp.stack([g["norm_pre"][0] for g in grads]), norm_post=jnp.stack([g["norm_post"][0] for g in grads]),
        norm_mem=jnp.stack([g["norm_mem"][0] for g in grads]), a_log=jnp.stack([g["a_vec"][0, 4:8] for g in grads]),
        dt_bias=jnp.stack([g["dt_vec"][0, 4:8] for g in grads]), dn_norm=jnp.stack([g["dn_norm"][0] for g in grads]),
        gm_norm=jnp.stack([g["gm_norm"][0] for g in grads]), spatial_w=jnp.stack([g["spatial_w"] for g in grads]),
        spatial_b=jnp.stack([g["spatial_b"][:, 0, :] for g in grads]),
        sinks=jnp.stack([g["sink_vec"][0, :8] for g in grads]))
    packed = jnp.pad(_pack_small(small_local) + after_start, ((0, 8), (0, 0)))
    gsmall_packed = _allreduce_small(packed.reshape(2, -1, 1024)).reshape(-1, 1024)[:_SMALL_ROWS]

    d_s, m_s, v_s = _ew("adamw_small", _adamw_fn, [_pack_small(w), gsmall_packed, _pack_small(m), _pack_small(v)], 3)
    gsmall, dsmall, msmall, vsmall = (_unpack_small(p) for p in (gsmall_packed, d_s, m_s, v_s))
    g_o, d_o, m_o, v_o = dict(gsmall), dict(dsmall), dict(msmall), dict(vsmall)
    tr = lambda a: a.transpose(0, 2, 1)
    w_t, m_t, v_t = tr(w["w_in"]), tr(m["w_in"]), tr(v["w_in"])

    def update(l, totals, into):
        outs = {}
        for n, t in zip(_BIG, totals):
            g_l = t.reshape(local[n].shape[1:])
            if n == "w_in":
                outs[n] = _adamw_rows("adamw_" + n, l, w_t, g_l, m_t, v_t, into and into[n])
            else:
                outs[n] = _adamw_layer("adamw_" + n, l, w[n], g_l, m[n], v[n], into and into[n])
        return outs

    landed1[1] = landed1[1] + after_start.astype(landed1[1].dtype)
    outs1 = update(1, _pair_share(_chip_sums(landed1, pair1, me)), None)
    pair0, landed0 = _split_wait("scatter_l0_wait", "scatter", scattering, outs1["w_in"][0])
    outs = update(0, _pair_share(_chip_sums(landed0, pair0, me)), outs1)
    for n in _BIG:
        d_o[n], m_o[n], v_o[n], g_o[n] = [tr(o) for o in outs[n]] if n == "w_in" else outs[n]
    return (loss, dx[None], *[g_o[n] for n in _NAMES], *[d_o[n] for n in _NAMES], *[m_o[n] for n in _NAMES],
            *[v_o[n] for n in _NAMES])
```

```python
import collections
import functools

import jax
import jax.numpy as jnp
from jax import lax
from jax.experimental import pallas as pl
from jax.experimental.pallas import tpu as pltpu

F32 = jnp.float32
BF16 = jnp.bfloat16

D_MODEL = 1024
BRANCH_W = 512
MEM_LEN = 256
N_LAYER = 2
N_CHIP = 4
N_DEV = 8
EPS = 1e-6
NEG_INF = -1e30
DN_CHUNK = 64
LANES = 128
VMEM_LIMIT = 48 * 1024 * 1024

ADAM_LR, ADAM_B1, ADAM_B2, ADAM_EPS, ADAM_WD, ADAM_STEP = 0.001, 0.9, 0.999, 1e-08, 0.01, 10

N_PAD = 10240
O_GATE = 0
O_AQ, O_AK, O_AV, O_AZ = 4096, 4608, 5120, 5632
O_BUV, O_BZ = 6144, 7168
O_CKV, O_BA = 7680, 7936
O_CQ, O_CZ = 8192, 8704
O_MQ, O_MZ = 9216, 9728
O_MISC, W_MISC = O_CKV, 512
_PAD_SEGS = [(5896, 4096), (0, 512), (512, 512), (1024, 512), (1536, 512), (2056, 1024), (3080, 512),
             (4104, 128), (4232, 128), (2048, 8), (None, 120), (None, 128),
             (3592, 512), (4360, 512), (4872, 512), (5384, 512)]
D_IN = 9992
SHARD_IN = D_IN // N_CHIP


SHARD_PAD = 2560


def _pad_parts():
    parts, off = [], 0
    for s, n in _PAD_SEGS:
        a = s
        while s is not None and a < s + n:
            chip = a // SHARD_IN
            b = min(s + n, (chip + 1) * SHARD_IN)
            parts.append((chip, a - chip * SHARD_IN, off + a - s, b - a))
            a = b
        off += n
    return parts


PERM_ROWS = 512
PERM_SLACK = 32


def _permute_rows(name, src, parts, n_out, out_dtype):
    B, Z = PERM_ROWS, PERM_ROWS + PERM_SLACK
    w = src.shape[1]
    plans = []
    for blk in range(n_out // B):
        o, runs = blk * B, []
        for s, d, n in parts:
            lo, hi = max(d, o), min(d + n, o + B)
            if lo < hi:
                s0 = s + lo - d
                wa = s0 // 16 * 16
                wb = min(-(-(s0 + hi - lo) // 16) * 16, src.shape[0])
                runs.append((wa, wb - wa, s0 - (lo - o) - wa, lo - o, hi - o))
        plans.append(runs)
    max_runs = max(len(r) for r in plans)
    nblk = len(plans)

    def body(*refs):
        src_ref, out_ref, inbuf, obuf, insem, outsem = (refs[0],) + refs[-5:]

        def in_copies(blk):
            return [pltpu.make_async_copy(src_ref.at[pl.ds(wa, ws)], inbuf.at[blk % 2, r, pl.ds(0, ws)],
                                          insem.at[blk % 2, r]) for r, (wa, ws, _, _, _) in enumerate(plans[blk])]

        def out_copy(blk):
            return pltpu.make_async_copy(obuf.at[blk % 2], out_ref.at[pl.ds(blk * B, B)], outsem.at[blk % 2])

        for cp in in_copies(0):
            cp.start()
        rid = _iota((B, 1), 0)
        for blk in range(nblk):
            if blk + 1 < nblk:
                for cp in in_copies(blk + 1):
                    cp.start()
            for cp in in_copies(blk):
                cp.wait()
            val = jnp.zeros((B, w), F32)
            for r, (wa, ws, t, l0, l1) in enumerate(plans[blk]):
                win = jnp.concatenate([inbuf[blk % 2, r, pl.ds(0, ws)].astype(F32), jnp.zeros((Z - ws, w), F32)], axis=0)
                moved = pltpu.roll(win, (-t) % Z, 0)[:B]
                val = jnp.where((rid >= l0) & (rid < l1), moved, val)
            if blk >= 2:
                out_copy(blk - 2).wait()
            obuf[blk % 2] = val.astype(out_dtype)
            out_copy(blk).start()
        for blk in range(max(nblk - 2, 0), nblk):
            out_copy(blk).wait()

    return pl.pallas_call(
        body, name=name, in_specs=[_ANY], out_specs=_ANY, out_shape=jax.ShapeDtypeStruct((n_out, w), out_dtype),
        scratch_shapes=[pltpu.VMEM((2, max_runs, Z, w), src.dtype), pltpu.VMEM((2, B, w), out_dtype),
                        pltpu.SemaphoreType.DMA((2, max_runs)), pltpu.SemaphoreType.DMA((2,))],
        compiler_params=_params(),
    )(src)


def _slab_parts():
    h, out = SHARD_PAD // 2, []
    for chip, s, d, n in _pad_parts():
        a = s
        while a < s + n:
            half = a // h
            b = min(s + n, (half + 1) * h)
            out.append(((half * N_CHIP + chip) * h + a - half * h, d + a - s, b - a))
            a = b
    return out


def _w_pad_from_slabs(slabs):
    return _permute_rows("w_pad_rows", slabs.reshape(-1, slabs.shape[-1]), _slab_parts(), N_PAD, BF16)


def _slabs_from_pad(dw):
    slabs = _permute_rows("w_pad_grad_rows", dw, [(d, s, n) for s, d, n in _slab_parts()], N_CHIP * SHARD_PAD, BF16)
    return slabs.reshape(2, N_CHIP, SHARD_PAD // 2, dw.shape[1])


def _dot(a, b, dims, prec):
    if prec == "bf16":
        return lax.dot_general(a.astype(BF16), b.astype(BF16), (dims, ((), ())), preferred_element_type=F32)
    return lax.dot_general(a, b, (dims, ((), ())), precision=lax.Precision.HIGHEST, preferred_element_type=F32)


_NN, _NT, _TN = ((1,), (0,)), ((1,), (1,)), ((0,), (0,))


def _make_mm(prec):
    @jax.custom_vjp
    def nn(a, b):
        return _dot(a, b, _NN, prec)

    @jax.custom_vjp
    def nt(a, b):
        return _dot(a, b, _NT, prec)

    @jax.custom_vjp
    def tn(a, b):
        return _dot(a, b, _TN, prec)

    nn.defvjp(lambda a, b: (nn(a, b), (a, b)), lambda r, g: (nt(g, r[1]), tn(r[0], g)))
    nt.defvjp(lambda a, b: (nt(a, b), (a, b)), lambda r, g: (nn(g, r[1]), tn(g, r[0])))
    tn.defvjp(lambda a, b: (tn(a, b), (a, b)), lambda r, g: (nt(r[1], g), nn(r[0], g)))
    return nn, nt, tn


_nn16, _nt16, _tn16 = _make_mm("bf16")
_nn32, _nt32, _tn32 = _make_mm("f32")


def _make_slice(axis):
    @functools.partial(jax.custom_vjp, nondiff_argnums=(1, 2, 3))
    def sl(x, a, b, n):
        return x[a:b] if axis == 0 else x[:, a:b]

    def fwd(x, a, b, n):
        return sl(x, a, b, n), None

    def bwd(a, b, n, _, g):
        parts = []
        if a > 0:
            parts.append(jnp.zeros((a, g.shape[1]) if axis == 0 else (g.shape[0], a), g.dtype))
        parts.append(g)
        if n - b > 0:
            parts.append(jnp.zeros((n - b, g.shape[1]) if axis == 0 else (g.shape[0], n - b), g.dtype))
        return (jnp.concatenate(parts, axis=axis),)

    sl.defvjp(fwd, bwd)
    return sl


_sl0, _sl1 = _make_slice(0), _make_slice(1)


def _rowsl(x, a, b):
    return _sl0(x, a, b, x.shape[0])


def _cols(x, a, b):
    return _sl1(x, a, b, x.shape[1])


@functools.partial(jax.custom_vjp, nondiff_argnums=(1,))
def _rollr(x, s):
    return pltpu.roll(x, s, 0)


_rollr.defvjp(lambda x, s: (_rollr(x, s), None),
              lambda s, _, g: (pltpu.roll(g, g.shape[0] - s, 0),))


def _iota(shape, axis):
    return lax.broadcasted_iota(jnp.int32, shape, axis)


def _sigmoid(x):
    return lax.logistic(x)


def _silu(x):
    return x * _sigmoid(x)


def _gelu(x):
    return 0.5 * x * (1.0 + jnp.tanh(0.7978845608028654 * (x + 0.044715 * (x * x * x))))


def _softplus(x):
    return jnp.maximum(x, 0.0) + jnp.log(1.0 + jnp.exp(-jnp.abs(x)))


def _rms(x, g):
    return x * lax.rsqrt(jnp.mean(x * x, axis=-1, keepdims=True) + EPS) * g


def _lane_pick(x, lane):
    return jnp.sum(x * (_iota((1, x.shape[1]), 1) == lane).astype(F32), axis=1, keepdims=True)


Row = collections.namedtuple("Row", "arr w cb hb grad", defaults=(0, True))


def _full_spec(shape):
    return pl.BlockSpec(shape, lambda i, _n=len(shape): (0,) * _n)


def _load_params(refs):
    return [[p[g].astype(F32) for g in range(p.shape[0])] if len(p.shape) == 3 else p[...].astype(F32)
            for p in refs]


def _params(**kw):
    return pltpu.CompilerParams(vmem_limit_bytes=VMEM_LIMIT, **kw)


def _rows_fwd(name, fn, rows, params, outs, tr, carry=None):
    T = rows[0].arr.shape[0]
    n = T // tr
    halos = [r for r in rows if r.hb]
    nr, nh, npar, no = len(rows), len(halos), len(params), len(outs)

    def body(*refs):
        row_refs, halo_refs = refs[:nr], refs[nr:nr + nh]
        par_refs = refs[nr + nh:nr + nh + npar]
        out_refs = refs[nr + nh + npar:nr + nh + npar + no]
        rest = refs[nr + nh + npar + no:]
        first = pl.program_id(0) == 0
        cvals = None
        if carry is not None:
            csave_ref, carry_ref = rest

            @pl.when(first)
            def _():
                carry_ref[...] = jnp.zeros_like(carry_ref)

            cvals = [carry_ref[g] for g in range(carry[0])]
            for g in range(carry[0]):
                csave_ref[0, g] = cvals[g]
        c_out, o = fn(first, cvals, [r[...].astype(F32) for r in row_refs],
                      [h[...].astype(F32) for h in halo_refs], _load_params(par_refs))
        for r, v in zip(out_refs, o):
            r[...] = v.astype(r.dtype)
        if carry is not None:
            for g in range(carry[0]):
                carry_ref[g] = c_out[g]

    in_specs = [pl.BlockSpec((tr, r.w), lambda i, c=r.cb: (i, c)) for r in rows]
    in_specs += [pl.BlockSpec((r.hb, r.w), lambda i, c=r.cb, q=tr // r.hb: (jnp.maximum(i * q - 1, 0), c))
                 for r in halos]
    in_specs += [_full_spec(p.shape) for p in params]
    out_shape = [jax.ShapeDtypeStruct((T, w), dt) for w, dt in outs]
    out_specs = [pl.BlockSpec((tr, w), lambda i: (i, 0)) for w, _ in outs]
    scratch = []
    if carry is not None:
        out_shape.append(jax.ShapeDtypeStruct((n,) + carry, F32))
        out_specs.append(pl.BlockSpec((1,) + carry, lambda i: (i, 0, 0, 0)))
        scratch.append(pltpu.VMEM(carry, F32))
    return pl.pallas_call(
        body, name=name, grid=(n,), in_specs=in_specs, out_specs=out_specs, out_shape=out_shape,
        scratch_shapes=scratch, compiler_params=_params(dimension_semantics=("arbitrary",)),
    )(*[r.arr for r in rows], *[r.arr for r in halos], *params)


def _rows_bwd(name, fn, rows, params, douts, tr, carry=None, csave=None, dcols=None):
    T = rows[0].arr.shape[0]
    n = T // tr
    halos = [r for r in rows if r.hb]
    grows = [r for r in rows if r.grad is True]
    crows = [r for r in rows if r.grad == "cols"]
    wcols = sum(r.w for r in crows)
    nr, nh, npar, nd, ng = len(rows), len(halos), len(params), len(douts), len(grows)
    nc = 0 if carry is None else 1
    ncol = 1 if crows else 0
    nalias = 1 if (crows and dcols is not None) else 0

    def body(*refs):
        row_refs, halo_refs = refs[:nr], refs[nr:nr + nh]
        par_refs = refs[nr + nh:nr + nh + npar]
        k = nr + nh + npar
        csave_ref = refs[k] if nc else None
        dout_refs = refs[k + nc:k + nc + nd]
        k = k + nc + nd + nalias
        drow_refs = refs[k:k + ng]
        dcols_ref = refs[k + ng] if ncol else None
        dpar_refs = refs[k + ng + ncol:k + ng + ncol + npar]
        k = k + ng + ncol + npar
        dcarry_ref = refs[k] if nc else None
        hgrad_refs = refs[k + nc:]
        i = pl.program_id(0)
        first_tile = i == n - 1

        @pl.when(i == 0)
        def _():
            for r in dpar_refs:
                r[...] = jnp.zeros_like(r)
            for r in hgrad_refs:
                r[...] = jnp.zeros_like(r)
            if nc:
                dcarry_ref[...] = jnp.zeros_like(dcarry_ref)

        rv = [r[...].astype(F32) for r in row_refs]
        hv = [h[...].astype(F32) for h in halo_refs]
        pv = _load_params(par_refs)
        dov = [d[...].astype(F32) for d in dout_refs]
        if nc:
            cv = [csave_ref[0, g] for g in range(carry[0])]
            _, vjp = jax.vjp(lambda c, r, h, p: fn(first_tile, c, r, h, p), cv, rv, hv, pv)
            dc, dr, dh, dp = vjp(([dcarry_ref[g] for g in range(carry[0])], dov))
            for g in range(carry[0]):
                dcarry_ref[g] = dc[g]
        else:
            _, vjp = jax.vjp(lambda r, h, p: fn(first_tile, None, r, h, p)[1], rv, hv, pv)
            dr, dh, dp = vjp(dov)
        gi = hi = 0
        pieces = []
        for kk, r in enumerate(rows):
            d = dr[kk]
            if r.hb:
                carried = hgrad_refs[hi][...]
                d = d + (carried if tr == r.hb else
                         jnp.concatenate([jnp.zeros((tr - r.hb, r.w), F32), carried], axis=0))
                hgrad_refs[hi][...] = dh[hi]
                hi += 1
            if r.grad is True:
                drow_refs[gi][...] = d.astype(drow_refs[gi].dtype)
                gi += 1
            elif r.grad == "cols":
                pieces.append(d.astype(BF16))
        if ncol:
            dcols_ref[...] = pieces[0] if len(pieces) == 1 else jnp.concatenate(pieces, axis=1)
        for r, d in zip(dpar_refs, dp):
            if len(r.shape) == 3:
                for g in range(r.shape[0]):
                    r[g] += d[g]
            else:
                r[...] += d

    rev = lambda i: n - 1 - i
    in_specs = [pl.BlockSpec((tr, r.w), lambda i, c=r.cb: (rev(i), c)) for r in rows]
    in_specs += [pl.BlockSpec((r.hb, r.w), lambda i, c=r.cb, q=tr // r.hb: (jnp.maximum(rev(i) * q - 1, 0), c))
                 for r in halos]
    in_specs += [_full_spec(p.shape) for p in params]
    args = [r.arr for r in rows] + [r.arr for r in halos] + list(params)
    scratch = []
    if nc:
        in_specs.append(pl.BlockSpec((1,) + carry, lambda i: (rev(i), 0, 0, 0)))
        args.append(csave)
        scratch.append(pltpu.VMEM(carry, F32))
    douts = [d if isinstance(d, Row) else Row(d, d.shape[1], 0) for d in douts]
    in_specs += [pl.BlockSpec((tr, d.w), lambda i, c=d.cb: (rev(i), c)) for d in douts]
    args += [d.arr for d in douts]
    aliases = {}
    if nalias:
        aliases = {len(args): ng}
        in_specs.append(pl.BlockSpec(memory_space=pl.ANY))
        args.append(dcols)
    scratch += [pltpu.VMEM((r.hb, r.w), F32) for r in halos]
    out_shape = [jax.ShapeDtypeStruct((T, r.w), F32) for r in grows]
    out_specs = [pl.BlockSpec((tr, r.w), lambda i: (rev(i), 0)) for r in grows]
    if ncol:
        off = crows[0].cb * crows[0].w
        assert off % wcols == 0 and all(a.cb * a.w + a.w == b.cb * b.w for a, b in zip(crows, crows[1:]))
        out_shape.append(jax.ShapeDtypeStruct((T, N_PAD), BF16))
        out_specs.append(pl.BlockSpec((tr, wcols), lambda i, c=off // wcols: (rev(i), c)))
    out_shape += [jax.ShapeDtypeStruct(p.shape, F32) for p in params]
    out_specs += [_full_spec(p.shape) for p in params]
    res = pl.pallas_call(
        body, name=name, grid=(n,), in_specs=in_specs, out_specs=out_specs, out_shape=out_shape,
        scratch_shapes=scratch, input_output_aliases=aliases,
        compiler_params=_params(dimension_semantics=("arbitrary",)),
    )(*args)
    return list(res[:ng]), list(res[ng + ncol:]), (res[ng] if ncol else dcols)


def _fill_misc(dcols, dkv, dba, tr):
    T = dkv.shape[0]

    def body(kv_ref, ba_ref, _, o_ref):
        o_ref[...] = jnp.concatenate([kv_ref[...], ba_ref[...]], axis=1).astype(BF16)

    return pl.pallas_call(
        body, name="misc_bwd", grid=(T // tr,),
        in_specs=[pl.BlockSpec((tr, 256), lambda i: (i, 0)), pl.BlockSpec((tr, 256), lambda i: (i, 0)),
                  pl.BlockSpec(memory_space=pl.ANY)],
        out_specs=pl.BlockSpec((tr, W_MISC), lambda i: (i, O_MISC // W_MISC)),
        out_shape=jax.ShapeDtypeStruct((T, N_PAD), BF16), input_output_aliases={2: 0},
        compiler_params=_params(dimension_semantics=("arbitrary",)),
    )(dkv, dba, dcols)


def _up_bwd(ys, cols, dm, w_up):
    T, tr = dm.shape[0], UPB_TR

    def body(y_ref, gl_ref, dm_ref, w_ref, dy_ref, dgl_ref, dw_ref):
        @pl.when(pl.program_id(1) == 0)
        def _():
            dw_ref[...] = jnp.zeros_like(dw_ref)

        _, vjp = jax.vjp(lambda y, gl, w: _sigmoid(gl) * _nn16(y, w),
                         y_ref[...].astype(F32), gl_ref[...].astype(F32), w_ref[...].astype(F32))
        dy, dgl, dw = vjp(dm_ref[...])
        dy_ref[...] = dy
        dgl_ref[...] = dgl.astype(BF16)
        dw_ref[...] += dw

    branch_rows = lambda w: pl.BlockSpec((tr, w), lambda n, i: (i, n))
    weight = pl.BlockSpec((None, BRANCH_W, D_MODEL), lambda n, i: (n, 0, 0))
    return pl.pallas_call(
        body, name="up_bwd", grid=(4, T // tr),
        in_specs=[branch_rows(BRANCH_W), branch_rows(D_MODEL), pl.BlockSpec((tr, D_MODEL), lambda n, i: (i, 0)), weight],
        out_specs=[branch_rows(BRANCH_W), branch_rows(D_MODEL), weight],
        out_shape=[jax.ShapeDtypeStruct((T, 4 * BRANCH_W), F32), jax.ShapeDtypeStruct((T, N_PAD), BF16),
                   jax.ShapeDtypeStruct(w_up.shape, F32)],
        compiler_params=_params(dimension_semantics=("arbitrary", "arbitrary")),
    )(ys, cols, dm, w_up)


def _matmul(name, a, b, kind, out_dtype, tm, tn, tk, after=None):
    if kind == "tn":
        (K, M), N = a.shape, b.shape[1]
    else:
        (M, K), N = a.shape, (b.shape[0] if kind == "nt" else b.shape[1])
    tm, tn, tk = min(tm, M), min(tn, N), min(tk, K)
    nk = K // tk
    dims = {"nn": _NN, "nt": _NT, "tn": _TN}[kind]

    n_after = 0 if after is None else 1

    def body(*refs):
        a_ref, b_ref, o_ref, acc = refs[0], refs[1], refs[2 + n_after], refs[3 + n_after:]
        part = lax.dot_general(a_ref[...], b_ref[...], (dims, ((), ())), preferred_element_type=F32)
        if nk == 1:
            o_ref[...] = part.astype(o_ref.dtype)
            return
        acc_ref = acc[0] if acc else o_ref
        k = pl.program_id(2)

        @pl.when(k == 0)
        def _():
            acc_ref[...] = part

        @pl.when(k > 0)
        def _():
            acc_ref[...] += part

        if acc:
            @pl.when(k == nk - 1)
            def _():
                o_ref[...] = acc_ref[...].astype(o_ref.dtype)

    a_spec = pl.BlockSpec((tk, tm), lambda i, j, k: (k, i)) if kind == "tn" else pl.BlockSpec((tm, tk), lambda i, j, k: (i, k))
    b_spec = pl.BlockSpec((tn, tk), lambda i, j, k: (j, k)) if kind == "nt" else pl.BlockSpec((tk, tn), lambda i, j, k: (k, j))
    return pl.pallas_call(
        body, name=name, grid=(M // tm, N // tn, nk), in_specs=[a_spec, b_spec] + [_ANY] * n_after,
        out_specs=pl.BlockSpec((tm, tn), lambda i, j, k: (i, j)),
        out_shape=jax.ShapeDtypeStruct((M, N), out_dtype),
        scratch_shapes=[pltpu.VMEM((tm, tn), F32)] if nk > 1 and out_dtype != F32 else [],
        compiler_params=_params(dimension_semantics=("arbitrary", "arbitrary", "arbitrary")),
    )(a, b, *([] if after is None else [after]))


def _pre_fn(first, _, rows, halos, params):
    return None, [_rms(rows[0], params[0])]


def _pre_fn_res(first, _, rows, halos, params):
    return None, [_rms(rows[0], params[0]), rows[0]]


def _memkv_fn(first, _, rows, halos, params):
    g, w = params
    return None, [_nn16(_rms(rows[0], g), w)]


def _conv_silu(x, halo, w4, keep_halo):
    tr = x.shape[0]
    halo = halo * keep_halo
    rid = _iota((tr, 1), 0)
    acc = w4[3] * x
    for s in (1, 2, 3):
        hs = jnp.concatenate([_rollr(halo, s), jnp.zeros((tr - halo.shape[0], x.shape[1]), F32)], axis=0)
        acc = acc + w4[3 - s] * jnp.where(rid < s, hs, _rollr(x, s))
    return _silu(acc)


def _dn_fn(first, S, rows, halos, params):
    qp, kp, vp, z, ba = rows
    conv, a_vec, dt_vec, dnorm = params
    ba = _cols(ba, 0, LANES)
    tr = qp.shape[0]
    keep = jnp.where(first, 0.0, 1.0)
    q = _conv_silu(qp, halos[0], [conv[3 * j + 0] for j in range(4)], keep)
    k = _conv_silu(kp, halos[1], [conv[3 * j + 1] for j in range(4)], keep)
    v = _conv_silu(vp, halos[2], [conv[3 * j + 2] for j in range(4)], keep)
    qh, kh, vh = [], [], []
    for h in range(4):
        a, b = h * LANES, (h + 1) * LANES
        xq, xk = _cols(q, a, b), _cols(k, a, b)
        qh.append(xq * lax.rsqrt(jnp.sum(xq * xq, axis=1, keepdims=True) + EPS) * (LANES ** -0.5))
        kh.append(xk * lax.rsqrt(jnp.sum(xk * xk, axis=1, keepdims=True) + EPS))
        vh.append(_cols(v, a, b))
    beta_all = _sigmoid(ba)
    g_all = -jnp.exp(a_vec) * _softplus(ba + dt_vec)
    C = DN_CHUNK
    ii, jj = _iota((C, C), 0), _iota((C, C), 1)
    strict, incl = ii > jj, ii >= jj
    eye = (ii == jj).astype(F32)
    last_row = (_iota((C, 1), 0) == C - 1).astype(F32)
    n_chunk = tr // C
    pairs = [(c, h) for c in range(n_chunk) for h in range(4)]
    rows_of = lambda a, c: _rowsl(a, c * C, (c + 1) * C)
    gcs = [_nn32(incl.astype(F32), rows_of(g_all, c)) for c in range(n_chunk)]
    qc = {(c, h): rows_of(qh[h], c) for c, h in pairs}
    kc = {(c, h): rows_of(kh[h], c) for c, h in pairs}
    beta = {(c, h): _lane_pick(rows_of(beta_all, c), h) for c, h in pairs}
    gc = {(c, h): _lane_pick(gcs[c], 4 + h) for c, h in pairs}
    dec = {p: jnp.exp(jnp.where(incl, gc[p] - jnp.sum(eye * gc[p], axis=0, keepdims=True), 0.0)) for p in pairs}
    egc = {p: jnp.exp(gc[p]) for p in pairs}
    kb = {p: kc[p] * beta[p] for p in pairs}
    kq = {p: _nt16(jnp.concatenate([kb[p], qc[p]], axis=0), kc[p]) for p in pairs}
    P = {p: -jnp.where(strict, _rowsl(kq[p], 0, C) * dec[p], 0.0) for p in pairs}
    aqk = {p: jnp.where(incl, _rowsl(kq[p], C, 2 * C) * dec[p], 0.0) for p in pairs}
    tinv = {p: eye + P[p] for p in pairs}
    P = {p: _nn16(P[p], P[p]) for p in pairs}
    for j in range(5):
        if j < 4:
            pt = {p: _nn16(jnp.concatenate([P[p], tinv[p]], axis=0), P[p]) for p in pairs}
            tinv = {p: tinv[p] + _rowsl(pt[p], C, 2 * C) for p in pairs}
            P = {p: _rowsl(pt[p], 0, C) for p in pairs}
        else:
            tinv = {p: tinv[p] + _nn16(tinv[p], P[p]) for p in pairs}
    uw = {(c, h): _nn16(tinv[c, h], jnp.concatenate([rows_of(vh[h], c) * beta[c, h], kb[c, h] * egc[c, h]], axis=1))
          for c, h in pairs}
    S = list(S)
    ychunks = []
    for c in range(n_chunk):
        zc = rows_of(z, c)
        hs = range(4)
        ws = [_nn16(jnp.concatenate([_cols(uw[c, h], LANES, 2 * LANES), qc[c, h] * egc[c, h]], axis=0), S[h]) for h in hs]
        vnew = [_cols(uw[c, h], 0, LANES) - _rowsl(ws[h], 0, C) for h in hs]
        o = [_rowsl(ws[h], C, 2 * C) + _nn16(aqk[c, h], vnew[h]) for h in hs]
        glast = [jnp.sum(gc[c, h] * last_row, axis=0, keepdims=True) for h in hs]
        S = [S[h] * jnp.exp(glast[h]) + _tn16(kc[c, h] * jnp.exp(glast[h] - gc[c, h]), vnew[h]) for h in hs]
        ychunks.append(jnp.concatenate(
            [_rms(o[h], dnorm) * _silu(_cols(zc, h * LANES, (h + 1) * LANES)) for h in hs], axis=1))
    return S, [jnp.concatenate(ychunks, axis=0)]


def _gm_fn(first, _, rows, halos, params):
    uv, z = rows
    gnorm, ws, bs = params
    tr = uv.shape[0]
    guv = _gelu(uv)
    u = _cols(guv, 0, BRANCH_W)
    v = _rms(_cols(guv, BRANCH_W, 2 * BRANCH_W), gnorm)
    ii, jj = _iota((LANES, LANES), 0), _iota((LANES, LANES), 1)
    eye = (ii == jj).astype(F32)
    wsm = [jnp.where(ii >= jj, ws[g], 0.0) for g in range(4)]
    bcol = [jnp.sum(eye * bs[g], axis=1, keepdims=True) for g in range(4)]
    chunks = []
    for c in range(tr // LANES):
        vc = _rowsl(v, c * LANES, (c + 1) * LANES)
        chunks.append(jnp.concatenate(
            [_nn16(wsm[g], _cols(vc, g * LANES, (g + 1) * LANES)) + bcol[g] for g in range(4)], axis=1))
    return None, [u * jnp.concatenate(chunks, axis=0) * _silu(z)]


def _swa_fn(first, _, rows, halos, params):
    q, kvc, z = rows
    sink_vec = params[0]
    P = LANES
    kv = jnp.concatenate([halos[0], kvc], axis=0)
    k, v = _cols(kv, 0, P), _cols(kv, P, 2 * P)
    r, cc = _iota((P, P), 0), _iota((P, P), 1)
    lane = _iota((1, P), 1)
    dist = _iota((P, 2 * P), 0) + P - _iota((P, 2 * P), 1)
    kmin = jnp.where(first, P, 0)
    valid = (dist >= 0) & (dist < P) & (_iota((P, 2 * P), 1) >= kmin)
    halves = [(lane < 64).astype(F32), (lane >= 64).astype(F32)]
    k_v = jnp.concatenate([k, v], axis=0)
    kkvv = [_nn16(k_v, (r == kh * 64 + (cc & 63)).astype(F32)) for kh in range(2)]
    scores = [_nt16(jnp.concatenate([_cols(q, (2 * kh + g // 2) * P, (2 * kh + g // 2 + 1) * P) * halves[g % 2]
                                     for g in range(4)], axis=0), _rowsl(kkvv[kh], 0, 2 * P)) for kh in range(2)]
    probs = []
    for kh in range(2):
        ps = []
        for g in range(4):
            s = jnp.where(valid, _rowsl(scores[kh], g * P, (g + 1) * P) * 0.125, NEG_INF)
            sink = _lane_pick(sink_vec, kh * 4 + g)
            m = lax.stop_gradient(jnp.maximum(jnp.max(s, axis=1, keepdims=True), sink))
            e = jnp.exp(s - m)
            ps.append(e / (jnp.sum(e, axis=1, keepdims=True) + jnp.exp(sink - m)))
        probs.append(jnp.concatenate(ps, axis=0))
    outs = [_nn16(probs[kh], _rowsl(kkvv[kh], 2 * P, 4 * P)) for kh in range(2)]
    blocks = [_rowsl(outs[j // 2], (2 * (j % 2)) * P, (2 * (j % 2) + 1) * P) * halves[0]
              + _rowsl(outs[j // 2], (2 * (j % 2) + 1) * P, (2 * (j % 2) + 2) * P) * halves[1] for j in range(4)]
    return None, [jnp.concatenate(blocks, axis=1) * _silu(z)]


def _mem_fn(first, _, rows, halos, params):
    q, z = rows
    mkv = params[0]
    heads = [(h * LANES, (h + 1) * LANES) for h in range(4)]
    scores = [_nt16(_cols(q, a, b), _cols(mkv, a, b)) * (LANES ** -0.5) for a, b in heads]
    probs = []
    for s in scores:
        e = jnp.exp(s - lax.stop_gradient(jnp.max(s, axis=1, keepdims=True)))
        probs.append(e / jnp.sum(e, axis=1, keepdims=True))
    outs = [_nn16(p, _cols(mkv, BRANCH_W + a, BRANCH_W + b)) for p, (a, b) in zip(probs, heads)]
    return None, [jnp.concatenate(outs, axis=1) * _silu(z)]


def _up_fn(first, _, rows, halos, params):
    ys, gl, w_up = rows[:4], rows[4], params[0]
    merged = None
    for n in range(4):
        term = _sigmoid(_cols(gl, n * D_MODEL, (n + 1) * D_MODEL)) * _nn16(ys[n], w_up[n])
        merged = term if merged is None else merged + term
    return None, [merged]


def _out_fn(first, _, rows, halos, params):
    x, merged = rows
    w, g = params
    return None, [x + _rms(_nn16(merged, w), g)]


def _loss_fn(first, _, rows, halos, params):
    y, t = rows
    d = y - t
    lrow = 0.5 * jnp.mean(d * d, axis=1, keepdims=True)
    return None, [d * (1.0 / D_MODEL), jnp.broadcast_to(lrow, (y.shape[0], LANES))]


TR = 256
DN_TR = 256
UP_TR = 256
UPB_TR = 512
CONV_HALO = 16
CARRY = (4, LANES, LANES)


def _branch_rows(cols, g):
    hb = CONV_HALO
    a = [Row(cols, 512, O_AQ // 512, hb, g), Row(cols, 512, O_AK // 512, hb, g), Row(cols, 512, O_AV // 512, hb, g),
         Row(cols, 512, O_AZ // 512, 0, g), Row(cols, 256, O_BA // 256)]
    b = [Row(cols, 1024, O_BUV // 1024, 0, g), Row(cols, 512, O_BZ // 512, 0, g)]
    c = [Row(cols, 512, O_CQ // 512, 0, g), Row(cols, 256, O_CKV // 256, LANES), Row(cols, 512, O_CZ // 512, 0, g)]
    m = [Row(cols, 512, O_MQ // 512, 0, g), Row(cols, 512, O_MZ // 512, 0, g)]
    return a, b, c, m


def _layer_fwd(x, mem, W, late_weights=None):
    h = _rows_fwd("prenorm_fwd", _pre_fn, [Row(x, D_MODEL, 0)], [W["norm_pre"]], [(D_MODEL, BF16)], TR)[0]
    cols = _matmul("in_proj_fwd", h, W["w_pad"], "nt", BF16, 2048, 1024, 1024)
    if late_weights is not None:
        W = dict(W, **late_weights(cols))
    mem_kv = _rows_fwd("memkv_fwd", _memkv_fn, [Row(mem, D_MODEL, 0)], [W["norm_mem"], W["w_mem_kv"]],
                       [(D_MODEL, F32)], MEM_LEN)[0]
    ra, rb, rc, rm = _branch_rows(cols, True)
    y_a, csave = _rows_fwd("dn_fwd", _dn_fn, ra, [W["conv"], W["a_vec"], W["dt_vec"], W["dn_norm"]],
                           [(BRANCH_W, BF16)], DN_TR, CARRY)
    y_b = _rows_fwd("gm_fwd", _gm_fn, rb, [W["gm_norm"], W["spatial_w"], W["spatial_b"]], [(BRANCH_W, BF16)], TR)[0]
    y_c = _rows_fwd("swa_fwd", _swa_fn, rc, [W["sink_vec"]], [(BRANCH_W, BF16)], LANES)[0]
    y_m = _rows_fwd("mem_fwd", _mem_fn, rm, [mem_kv], [(BRANCH_W, BF16)], TR)[0]
    ys = [y_a, y_b, y_c, y_m]
    merged = _rows_fwd("up_fwd", _up_fn, [Row(y, BRANCH_W, 0) for y in ys] + [Row(cols, 4 * D_MODEL, 0)],
                       [W["w_up"]], [(D_MODEL, BF16)], UP_TR)[0]
    x_new = _rows_fwd("out_fwd", _out_fn, [Row(x, D_MODEL, 0), Row(merged, D_MODEL, 0)],
                      [W["w_out"], W["norm_post"]], [(D_MODEL, F32)], TR)[0]
    return x_new, dict(x=x, h=h, cols=cols, mem_kv=mem_kv, csave=csave, ys=ys, merged=merged), W


def _layer_bwd(dxn, mem, W, sv, on_weight_grads=None):
    x, cols = sv["x"], sv["cols"]
    (dx_res, dm), (dw_out, dnorm_post), _ = _rows_bwd(
        "out_bwd", _out_fn, [Row(x, D_MODEL, 0), Row(sv["merged"], D_MODEL, 0)], [W["w_out"], W["norm_post"]],
        [dxn], TR)
    dys, dcols, dw_up = _up_bwd(jnp.concatenate(sv["ys"], axis=1), cols, dm, W["w_up"])
    dys = [Row(dys, BRANCH_W, n) for n in range(4)]
    ra, rb, rc, rm = _branch_rows(cols, "cols")
    (dba,), (dconv, da_vec, ddt_vec, ddn_norm), dcols = _rows_bwd(
        "dn_bwd", _dn_fn, ra, [W["conv"], W["a_vec"], W["dt_vec"], W["dn_norm"]], [dys[0]], DN_TR, CARRY,
        sv["csave"], dcols=dcols)
    _, (dgm_norm, dws, dbs), dcols = _rows_bwd(
        "gm_bwd", _gm_fn, rb, [W["gm_norm"], W["spatial_w"], W["spatial_b"]], [dys[1]], TR, dcols=dcols)
    (dkv_c,), (dsink,), dcols = _rows_bwd("swa_bwd", _swa_fn, rc, [W["sink_vec"]], [dys[2]], LANES, dcols=dcols)
    _, (dmem_kv,), dcols = _rows_bwd("mem_bwd", _mem_fn, rm, [sv["mem_kv"]], [dys[3]], TR, dcols=dcols)
    dcols = _fill_misc(dcols, dkv_c, dba, TR)
    _, (dnorm_mem, dw_mem_kv), _ = _rows_bwd("memkv_bwd", _memkv_fn, [Row(mem, D_MODEL, 0, 0, False)],
                                             [W["norm_mem"], W["w_mem_kv"]], [dmem_kv], MEM_LEN)
    dw_pad = _matmul("in_proj_dw", dcols, sv["h"], "tn", F32, 1024, 1024, 2048)
    grads = dict(norm_post=dnorm_post, norm_mem=dnorm_mem, w_pad=dw_pad, conv=dconv,
                 a_vec=da_vec, dt_vec=ddt_vec, dn_norm=ddn_norm, gm_norm=dgm_norm, spatial_w=dws, spatial_b=dbs,
                 sink_vec=dsink, w_mem_kv=dw_mem_kv, w_up=dw_up, w_out=dw_out)
    started = None if on_weight_grads is None else on_weight_grads(grads)
    dh = _matmul("in_proj_dx", dcols, W["w_pad"], "nn", F32, 2048, 1024, 1024, after=started)
    (dx,), (grads["norm_pre"],), _ = _rows_bwd("prenorm_bwd", _pre_fn_res, [Row(x, D_MODEL, 0)], [W["norm_pre"]],
                                               [dh, dx_res], TR)
    return dx, grads


def _lane_vec(v, off):
    return jnp.zeros((1, LANES), F32).at[0, off:off + v.shape[0]].set(v)


def _layer_weights(l, w_pad, conv_w, small, **late):
    return dict(
        late, w_pad=w_pad, conv=conv_w.reshape(4, 3, BRANCH_W).reshape(12, 1, BRANCH_W),
        norm_pre=small["norm_pre"][l][None], norm_post=small["norm_post"][l][None],
        norm_mem=small["norm_mem"][l][None],
        a_vec=_lane_vec(small["a_log"][l], 4), dt_vec=_lane_vec(small["dt_bias"][l], 4),
        dn_norm=small["dn_norm"][l][None], gm_norm=small["gm_norm"][l][None],
        spatial_w=small["spatial_w"][l], spatial_b=small["spatial_b"][l][:, None, :],
        sink_vec=_lane_vec(small["sinks"][l], 0))


_MESH = pl.DeviceIdType.MESH
_ANY = pl.BlockSpec(memory_space=pl.ANY)


def _position():
    return lax.axis_index("x"), lax.axis_index("y"), lax.axis_index("c")


def _remote(src, dst, send_sem, recv_sem, dev):
    return pltpu.make_async_remote_copy(src_ref=src, dst_ref=dst, send_sem=send_sem, recv_sem=recv_sem,
                                        device_id=dev, device_id_type=_MESH)


def _hbm_call(name, body, arrs, out_shapes, sems, aliases=None):
    return pl.pallas_call(
        body, name=name, in_specs=[_ANY] * len(arrs), out_specs=[_ANY] * len(out_shapes), out_shape=out_shapes,
        scratch_shapes=[pltpu.SemaphoreType.DMA((k,)) for k in sems], input_output_aliases=aliases or {},
        compiler_params=pltpu.CompilerParams(has_side_effects=True),
    )(*arrs)


def _other_chips(x, y):
    return [(1 - x, y), (x, 1 - y), (1 - x, 1 - y)]


def _gather_weights(arrs):
    n = len(arrs)

    def body(*refs):
        ins, outs = refs[:n], refs[n:2 * n]
        ici_send, ici_recv, d2d_send, d2d_recv = refs[2 * n:]
        x, y, c = _position()
        me = 2 * x + y
        chips = _other_chips(x, y)
        sends = []
        for a in range(n):
            for j, (px, py) in enumerate(chips):
                sends.append(_remote(ins[a].at[c], outs[a].at[c, me], ici_send.at[3 * a + j], ici_recv.at[3 * a + j],
                                     (px, py, c)))
                sends[-1].start()
        for a in range(n):
            for j, (px, py) in enumerate(chips):
                slab = outs[a].at[c, 2 * px + py]
                _remote(ins[a].at[c], slab, ici_send.at[3 * a + j], ici_recv.at[3 * a + j], (px, py, c)).wait_recv()
                sends.append(_remote(slab, slab, d2d_send.at[3 * a + j], d2d_recv.at[3 * a + j], (x, y, 1 - c)))
                sends[-1].start()
        for a in range(n):
            for j, (px, py) in enumerate(chips):
                slab = outs[a].at[1 - c, 2 * px + py]
                _remote(slab, slab, d2d_send.at[3 * a + j], d2d_recv.at[3 * a + j], (x, y, 1 - c)).wait_recv()
        for cp in sends:
            cp.wait_send()

    return _hbm_call("gather_weights", body, arrs,
                     [jax.ShapeDtypeStruct((N_LAYER, N_CHIP) + a.shape[1:], a.dtype) for a in arrs], [3 * n] * 4)


def _pair_exchange(arrs):
    n = len(arrs)

    def body(*refs):
        ins, outs = refs[:n], refs[n:2 * n]
        send_sems, recv_sems = refs[2 * n:]
        x, y, c = _position()
        cps = [_remote(ins[a].at[1 - c], outs[a], send_sems.at[a], recv_sems.at[a], (x, y, 1 - c)) for a in range(n)]
        for cp in cps:
            cp.start()
        for cp in cps:
            cp.wait_recv()
        for cp in cps:
            cp.wait_send()

    return _hbm_call("pair_exchange", body, arrs, [jax.ShapeDtypeStruct(a.shape[1:], a.dtype) for a in arrs], [n, n])


def _chip_scatter(arrs):
    n = len(arrs)

    def body(*refs):
        ins, outs = refs[:n], refs[n:2 * n]
        send_sems, recv_sems = refs[2 * n:]
        x, y, c = _position()
        me = 2 * x + y
        sends = []
        for a in range(n):
            for j, (px, py) in enumerate(_other_chips(x, y)):
                sends.append(_remote(ins[a].at[2 * px + py], outs[a].at[me], send_sems.at[3 * a + j],
                                     recv_sems.at[3 * a + j], (px, py, c)))
                sends[-1].start()
        for a in range(n):
            for j, (px, py) in enumerate(_other_chips(x, y)):
                _remote(ins[a].at[me], outs[a].at[2 * px + py], send_sems.at[3 * a + j], recv_sems.at[3 * a + j],
                        (px, py, c)).wait_recv()
        for cp in sends:
            cp.wait_send()

    return _hbm_call("chip_scatter", body, arrs, [jax.ShapeDtypeStruct(a.shape, a.dtype) for a in arrs],
                     [3 * n, 3 * n])


def _pair_share(arrs):
    n = len(arrs)

    def body(*refs):
        ins, outs = refs[:n], refs[n:2 * n]
        send_sems, recv_sems = refs[2 * n:]
        x, y, c = _position()
        cps = [_remote(ins[a].at[c], outs[a].at[c], send_sems.at[a], recv_sems.at[a], (x, y, 1 - c)) for a in range(n)]
        for cp in cps:
            cp.start()
        for a in range(n):
            _remote(ins[a].at[c], outs[a].at[1 - c], send_sems.at[a], recv_sems.at[a], (x, y, 1 - c)).wait_recv()
        for cp in cps:
            cp.wait_send()

    return _hbm_call("pair_share", body, arrs, [jax.ShapeDtypeStruct(a.shape, a.dtype) for a in arrs], [n, n],
                     {a: a for a in range(n)})


def _pair_forward(arrs):
    n = len(arrs)

    def body(*refs):
        ins, outs = refs[:n], refs[n:2 * n]
        send_sems, recv_sems = refs[2 * n:]
        x, y, c = _position()
        sends = []
        for a in range(n):
            for j, (px, py) in enumerate(_other_chips(x, y)):
                sends.append(_remote(ins[a].at[c, 2 * px + py], outs[a].at[c, 2 * px + py], send_sems.at[3 * a + j],
                                     recv_sems.at[3 * a + j], (x, y, 1 - c)))
                sends[-1].start()
        for a in range(n):
            for j, (px, py) in enumerate(_other_chips(x, y)):
                slab = outs[a].at[1 - c, 2 * px + py]
                _remote(slab, slab, send_sems.at[3 * a + j], recv_sems.at[3 * a + j], (x, y, 1 - c)).wait_recv()
        for cp in sends:
            cp.wait_send()

    return _hbm_call("pair_forward", body, arrs, [jax.ShapeDtypeStruct(a.shape, a.dtype) for a in arrs],
                     [3 * n, 3 * n], {a: a for a in range(n)})


_HBM = pl.BlockSpec(memory_space=pltpu.HBM)
_SEM = pl.BlockSpec(memory_space=pltpu.SEMAPHORE)
_EFFECT = pltpu.SideEffectType.DATAFLOW_SIDE_EFFECTING


def _chip_copies(kind, srcs, lands, send_sems, recv_sems):
    x, y, c = _position()
    me = 2 * x + y
    sends, recvs = [], []
    for a in range(len(srcs)):
        for j, (px, py) in enumerate(_other_chips(x, y)):
            s, sems, dev = 2 * px + py, (send_sems.at[3 * a + j], recv_sems.at[3 * a + j]), (px, py, c)
            if kind == "gather":
                sends.append(_remote(srcs[a].at[c], lands[a].at[c, me], *sems, dev))
                recvs.append(_remote(srcs[a].at[c], lands[a].at[c, s], *sems, dev))
            else:
                sends.append(_remote(srcs[a].at[s], lands[a].at[me], *sems, dev))
                recvs.append(_remote(srcs[a].at[me], lands[a].at[s], *sems, dev))
    return sends, recvs


def _split_start(name, kind, srcs, land_shapes, after):
    n = len(srcs)

    def body(*refs):
        sends, _ = _chip_copies(kind, refs[:n], refs[n:2 * n], refs[2 * n + 1], refs[2 * n + 2])
        for cp in sends:
            cp.start()
        refs[-1][...] = jnp.zeros_like(refs[-1])

    hbm = lambda a: pltpu.with_memory_space_constraint(a, pltpu.HBM)
    lands = [lax.empty(s.shape, s.dtype) for s in land_shapes]
    outs = pl.pallas_call(
        body, name=name, in_specs=[_HBM] * (2 * n) + [_ANY],
        out_specs=[_SEM, _SEM] + [_HBM] * (2 * n) + [pl.BlockSpec(memory_space=pltpu.VMEM)],
        out_shape=[pltpu.SemaphoreType.DMA((3 * n,)), pltpu.SemaphoreType.DMA((3 * n,))]
        + [pltpu.HBM(a.shape, a.dtype) for a in list(srcs) + lands] + [jax.ShapeDtypeStruct((8, LANES), F32)],
        input_output_aliases={i: 2 + i for i in range(2 * n)},
        compiler_params=pltpu.CompilerParams(has_side_effects=_EFFECT),
    )(*[hbm(a) for a in srcs], *[hbm(a) for a in lands], after)
    return outs[0], outs[1], list(outs[2:2 + 2 * n]), outs[-1]


def _split_wait(name, kind, started, after):
    send_sems, recv_sems, thru, _ = started
    n = len(thru) // 2

    def body(*refs):
        sends, recvs = _chip_copies(kind, refs[:n], refs[n:2 * n], refs[2 * n], refs[2 * n + 1])
        for cp in sends:
            cp.wait_send()
        for cp in recvs:
            cp.wait_recv()

    outs = pl.pallas_call(
        body, name=name, in_specs=[_HBM] * (2 * n) + [_SEM, _SEM, _ANY], out_specs=[_HBM] * (2 * n),
        out_shape=[pltpu.HBM(a.shape, a.dtype) for a in thru], input_output_aliases={i: i for i in range(2 * n)},
        compiler_params=pltpu.CompilerParams(has_side_effects=_EFFECT),
    )(*thru, send_sems, recv_sems, after)
    return list(outs[:n]), list(outs[n:])


def _allreduce_small(g):
    def body(g_ref, o_ref, pair_buf, chip_buf, send_sems, recv_sems):
        x, y, c = _position()
        me = 2 * x + y
        sib = (x, y, 1 - c)
        to_sib = _remote(g_ref.at[1 - c], pair_buf, send_sems.at[0], recv_sems.at[0], sib)
        to_sib.start()
        to_sib.wait_recv()
        chip_buf[me] = g_ref[c] + pair_buf[...]
        sends = [to_sib]
        chips = _other_chips(x, y)
        for j, (px, py) in enumerate(chips):
            sends.append(_remote(chip_buf.at[me], chip_buf.at[me], send_sems.at[1 + j], recv_sems.at[1 + j], (px, py, c)))
            sends[-1].start()
        for j, (px, py) in enumerate(chips):
            _remote(chip_buf.at[me], chip_buf.at[2 * px + py], send_sems.at[1 + j], recv_sems.at[1 + j],
                    (px, py, c)).wait_recv()
        o_ref[c] = ((chip_buf[0] + chip_buf[1]) + chip_buf[2]) + chip_buf[3]
        sends.append(_remote(o_ref.at[c], o_ref.at[c], send_sems.at[4], recv_sems.at[4], sib))
        sends[-1].start()
        _remote(o_ref.at[c], o_ref.at[1 - c], send_sems.at[4], recv_sems.at[4], sib).wait_recv()
        for cp in sends:
            cp.wait_send()

    vmem = pl.BlockSpec(memory_space=pltpu.VMEM)
    return pl.pallas_call(
        body, name="allreduce_small", in_specs=[vmem], out_specs=vmem, out_shape=jax.ShapeDtypeStruct(g.shape, F32),
        scratch_shapes=[pltpu.VMEM(g.shape[1:], F32), pltpu.VMEM((N_CHIP,) + g.shape[1:], F32),
                        pltpu.SemaphoreType.DMA((5,)), pltpu.SemaphoreType.DMA((5,))],
        compiler_params=_params(),
    )(g)


EW_ROWS = 512


def _ew(name, fn, ins, n_out, out_dtype=F32, out_slot=None, into=None):
    def dims(a):
        return a[0].shape[1:] if isinstance(a, tuple) else a.shape

    R, w = dims(ins[0])
    tr = EW_ROWS if R % EW_ROWS == 0 else R
    n_into = len(into) if into else 0

    def body(c_ref, *refs):
        outs = fn(*[r[...] for r in refs[:len(ins)]])
        for r, v in zip(refs[len(ins) + n_into:], outs):
            r[...] = v.astype(r.dtype)

    def lead_spec(l):
        if l == "c":
            return pl.BlockSpec((None, tr, w), lambda i, c_ref: (c_ref[0], i, 0))
        return pl.BlockSpec((None, tr, w), lambda i, c_ref, s=l: (s, i, 0))

    plain = pl.BlockSpec((tr, w), lambda i, c_ref: (i, 0))
    in_specs = [lead_spec(a[1]) if isinstance(a, tuple) else plain for a in ins] + [_ANY] * n_into
    out_spec = plain if out_slot is None else lead_spec(out_slot)
    out_shape = jax.ShapeDtypeStruct((R, w) if out_slot is None else (2, R, w), out_dtype)
    return pl.pallas_call(
        body, name=name,
        grid_spec=pltpu.PrefetchScalarGridSpec(num_scalar_prefetch=1, grid=(R // tr,), in_specs=in_specs,
                                               out_specs=[out_spec] * n_out),
        out_shape=[out_shape] * n_out, input_output_aliases={1 + len(ins) + j: j for j in range(n_into)},
        compiler_params=_params(dimension_semantics=("arbitrary",)),
    )(lax.axis_index("c").astype(jnp.int32).reshape(1), *[a[0] if isinstance(a, tuple) else a for a in ins],
      *(into or []))


def _adamw_fn(w, g, m, v):
    m = ADAM_B1 * m + (1.0 - ADAM_B1) * g
    v = ADAM_B2 * v + (1.0 - ADAM_B2) * (g * g)
    m_hat = m / (1.0 - ADAM_B1 ** ADAM_STEP)
    v_hat = v / (1.0 - ADAM_B2 ** ADAM_STEP)
    delta = -ADAM_LR * (m_hat / (jnp.sqrt(v_hat) + ADAM_EPS) + ADAM_WD * w)
    return delta, m, v


def _adamw(name, w, g, m, v):
    shape = w.shape
    two = lambda a: a.reshape(-1, shape[-1])
    return [o.reshape(shape) for o in _ew(name, _adamw_fn, [two(w), two(g), two(m), two(v)], 3)]


def _adamw_layer(name, l, w, g, m, v, into):
    k = w.shape[-1]
    three = lambda a: (a.reshape(N_LAYER, -1, k), l)
    fn = lambda w_, g_, m_, v_: _adamw_fn(w_, g_, m_, v_) + (g_,)
    outs = _ew(name, fn, [three(w), g.reshape(-1, k), three(m), three(v)], 4, out_slot=l,
               into=None if into is None else [a.reshape(N_LAYER, -1, k) for a in into])
    return [o.reshape(w.shape) for o in outs]


def _adamw_rows(name, l, w, g, m, v, into):
    _, R, k = w.shape
    n_into = len(into) if into else 0

    def body(*refs):
        w_ref, g_ref, m_ref, v_ref = refs[:4]
        d_out, m_out, v_out, g_out = refs[4 + n_into:]
        g_blk = g_ref[...]
        d_out[...], m_out[...], v_out[...] = _adamw_fn(w_ref[...], g_blk, m_ref[...], v_ref[...])
        g_out[...] = g_blk

    spec = pl.BlockSpec((None, EW_ROWS, k), lambda i: (l, i, 0))
    return pl.pallas_call(
        body, name=name, grid=(-(-R // EW_ROWS),),
        in_specs=[spec, pl.BlockSpec((EW_ROWS, k), lambda i: (i, 0)), spec, spec] + [_ANY] * n_into,
        out_specs=[spec] * 4, out_shape=[jax.ShapeDtypeStruct((N_LAYER, R, k), F32)] * 4,
        input_output_aliases={4 + j: j for j in range(n_into)},
        compiler_params=_params(dimension_semantics=("arbitrary",)),
    )(w, g, m, v, *(into or []))


_SMALL = [("norm_pre", (2, 1024)), ("norm_post", (2, 1024)), ("norm_mem", (2, 1024)), ("a_log", (2, 4)),
          ("dt_bias", (2, 4)), ("dn_norm", (2, 128)), ("gm_norm", (2, 512)), ("spatial_w", (2, 4, 128, 128)),
          ("spatial_b", (2, 4, 128)), ("sinks", (2, 8))]
_SMALL_ROWS = 200
_BIG = ["w_in", "conv_w", "w_mem_kv", "w_up", "w_out"]
_NAMES = ["norm_pre", "norm_post", "norm_mem", "w_in", "conv_w", "a_log", "dt_bias", "dn_norm", "gm_norm",
          "spatial_w", "spatial_b", "sinks", "w_mem_kv", "w_up", "w_out"]


def _size(shape):
    n = 1
    for s in shape:
        n *= s
    return n


_PACK_UNIT = 8 * 1024


def _pack_small(d):
    rows = []
    for n, shp in _SMALL:
        flat = d[n].reshape(-1)
        rows.append(jnp.pad(flat, (0, -flat.shape[0] % _PACK_UNIT)).reshape(-1, 1024))
    assert sum(r.shape[0] for r in rows) == _SMALL_ROWS
    return jnp.concatenate(rows, axis=0)


def _unpack_small(p):
    out, off = {}, 0
    for n, shp in _SMALL:
        k = -(-_size(shp) // _PACK_UNIT) * 8
        out[n] = p[off:off + k].reshape(-1)[:_size(shp)].reshape(shp)
        off += k
    return out


_HALF_SHAPE = {"w_in": (SHARD_PAD // 2, D_MODEL), "conv_w": (2, 3 * BRANCH_W // N_CHIP), "w_mem_kv": (128, D_MODEL),
               "w_up": (2, BRANCH_W, D_MODEL // N_CHIP), "w_out": (128, D_MODEL)}


def _chip_major(g):
    g = jnp.swapaxes(g, 0, 1)
    return g.reshape((N_CHIP, 2 * g.shape[2]) + g.shape[3:])


def _half_major(g):
    g = g.reshape((N_CHIP, 2, g.shape[1] // 2) + g.shape[2:])
    return jnp.swapaxes(g, 0, 1).astype(BF16)


N_EARLY = 2


def _early_views(l, g_in, g_conv, small):
    return _layer_weights(l, _w_pad_from_slabs(g_in),
                          _chip_major(g_conv).transpose(1, 0, 2).reshape(4, 3 * BRANCH_W), small)


def _late_views(g_kv, g_up, g_out):
    return dict(w_mem_kv=_chip_major(g_kv).reshape(D_MODEL, D_MODEL),
                w_up=_chip_major(g_up).transpose(1, 2, 0, 3).reshape(4, BRANCH_W, D_MODEL),
                w_out=_chip_major(g_out).reshape(D_MODEL, D_MODEL))


def _pair_sums(g):
    big = [_slabs_from_pad(g["w_pad"]),
           _half_major(g["conv"].reshape(4, N_CHIP, 3 * BRANCH_W // N_CHIP).transpose(1, 0, 2)),
           _half_major(g["w_mem_kv"].reshape(N_CHIP, D_MODEL // N_CHIP, D_MODEL)),
           _half_major(g["w_up"].reshape(4, BRANCH_W, N_CHIP, D_MODEL // N_CHIP).transpose(2, 0, 1, 3)),
           _half_major(g["w_out"].reshape(N_CHIP, D_MODEL // N_CHIP, D_MODEL))]
    add2 = lambda a, b: [a.astype(F32) + b.astype(F32)]
    pair = []
    for n, b, p in zip(_BIG, big, _pair_exchange(big)):
        k = b.shape[-1]
        pair.append(_ew("pair_sum_" + n, add2, [(b.reshape(2, -1, k), "c"), p.reshape(-1, k)], 1, BF16)[0]
                    .reshape(p.shape))
    return pair


def _chip_sums(landed, pair, me):
    add4 = lambda a, b, c_, d: [((a.astype(F32) + b.astype(F32)) + c_.astype(F32)) + d.astype(F32)]
    totals = []
    for n, r, q in zip(_BIG, landed, pair):
        r = _own_slot(r, lax.dynamic_index_in_dim(q, me, 0), me, 0)
        k = r.shape[-1]
        totals.append(_ew("chip_sum_" + n, add4, [(r.reshape(N_CHIP, -1, k), s) for s in range(N_CHIP)], 1,
                          out_slot="c")[0].reshape((2,) + r.shape[1:]))
    return totals


def _own_slot(buf, mine, me, axis):
    return lax.dynamic_update_index_in_dim(buf, mine.astype(buf.dtype), me, axis)


def kernel(x, mem, norm_pre, norm_post, norm_mem, w_in, conv_w, a_log, dt_bias, dn_norm, gm_norm, spatial_w, spatial_b, sinks, w_mem_kv, w_up, w_out, loss_target, m_norm_pre, m_norm_post, m_norm_mem, m_w_in, m_conv_w, m_a_log, m_dt_bias, m_dn_norm, m_gm_norm, m_spatial_w, m_spatial_b, m_sinks, m_w_mem_kv, m_w_up, m_w_out, v_norm_pre, v_norm_post, v_norm_mem, v_w_in, v_conv_w, v_a_log, v_dt_bias, v_dn_norm, v_gm_norm, v_spatial_w, v_spatial_b, v_sinks, v_w_mem_kv, v_w_up, v_w_out):
    w = dict(norm_pre=norm_pre, norm_post=norm_post, norm_mem=norm_mem, w_in=w_in, conv_w=conv_w, a_log=a_log,
             dt_bias=dt_bias, dn_norm=dn_norm, gm_norm=gm_norm, spatial_w=spatial_w, spatial_b=spatial_b, sinks=sinks,
             w_mem_kv=w_mem_kv, w_up=w_up, w_out=w_out)
    m = dict(norm_pre=m_norm_pre, norm_post=m_norm_post, norm_mem=m_norm_mem, w_in=m_w_in, conv_w=m_conv_w,
             a_log=m_a_log, dt_bias=m_dt_bias, dn_norm=m_dn_norm, gm_norm=m_gm_norm, spatial_w=m_spatial_w,
             spatial_b=m_spatial_b, sinks=m_sinks, w_mem_kv=m_w_mem_kv, w_up=m_w_up, w_out=m_w_out)
    v = dict(norm_pre=v_norm_pre, norm_post=v_norm_post, norm_mem=v_norm_mem, w_in=v_w_in, conv_w=v_conv_w,
             a_log=v_a_log, dt_bias=v_dt_bias, dn_norm=v_dn_norm, gm_norm=v_gm_norm, spatial_w=v_spatial_w,
             spatial_b=v_spatial_b, sinks=v_sinks, w_mem_kv=v_w_mem_kv, w_up=v_w_up, w_out=v_w_out)
    me = 2 * lax.axis_index("x") + lax.axis_index("y")

    w_in_t = jnp.pad(w_in.astype(BF16).transpose(0, 2, 1), ((0, 0), (0, SHARD_PAD - SHARD_IN), (0, 0)))
    local = dict(w_in=w_in_t, conv_w=conv_w, w_mem_kv=w_mem_kv.astype(BF16), w_up=w_up.astype(BF16),
                 w_out=w_out.astype(BF16))
    halves = lambda l: [local[n][l].reshape((2,) + _HALF_SHAPE[n]) for n in _BIG]
    own = lambda gathered, mine: [_own_slot(g, h[:, None], me, 1) for g, h in zip(gathered, mine)]
    lands = [jax.ShapeDtypeStruct((2, N_CHIP) + _HALF_SHAPE[n], local[n].dtype) for n in _BIG]
    h0 = halves(0)
    g0 = own(_gather_weights(h0[:N_EARLY]), h0[:N_EARLY])
    rest0 = _split_start("gather_l0_rest_start", "gather", h0[N_EARLY:], lands[N_EARLY:], g0[1])
    started = _split_start("gather_l1_start", "gather", halves(1), lands, rest0[3])

    xl, meml = x[0], mem[0]
    W0 = _early_views(0, g0[0], g0[1], w)
    W0["norm_pre"] = W0["norm_pre"] + started[3][0, 0]

    def late0(cols):
        mine, landed = _split_wait("gather_l0_rest_wait", "gather", rest0, cols)
        return _late_views(*own(_pair_forward(landed), mine))

    x1, sv0, W0 = _layer_fwd(xl, meml, W0, late0)
    mine1, landed1 = _split_wait("gather_l1_wait", "gather", started, x1)
    g1 = own(_pair_forward(landed1), mine1)
    W1 = dict(_early_views(1, g1[0], g1[1], w), **_late_views(*g1[N_EARLY:]))
    x2, sv1, _ = _layer_fwd(x1, meml, W1)
    dy, lrows = _rows_fwd("loss", _loss_fn, [Row(x2, D_MODEL, 0), Row(loss_target[0], D_MODEL, 0)], [],
                          [(D_MODEL, F32), (LANES, F32)], TR)
    loss = lax.psum(jnp.sum(lrows[:, 0]), ("x", "y", "c"))

    scattering = {}

    def start_scatter(l):
        def on_weight_grads(g):
            pair = _pair_sums(g)
            scattering[l] = _split_start("scatter_l%d_start" % l, "scatter", pair,
                                         [jax.ShapeDtypeStruct(p.shape, p.dtype) for p in pair], pair[1])
            return scattering[l][3]
        return on_weight_grads

    dx1, grads1 = _layer_bwd(dy, meml, W1, sv1, start_scatter(1))
    dx, grads0 = _layer_bwd(dx1, meml, W0, sv0, start_scatter(0))
    pair1, landed1 = _split_wait("scatter_l1_wait", "scatter", scattering[1], dx)
    after_start = scattering[0][3][0, 0]
    grads = [grads0, grads1]

    small_local = dict(
        norm_pre=jnp.stack([g["norm_pre"][0] for g in grads]), norm_post=jnp.stack([g["norm_post"][0] for g in grads]),
        norm_mem=jnp.stack([g["norm_mem"][0] for g in grads]), a_log=jnp.stack([g["a_vec"][0, 4:8] for g in grads]),
        dt_bias=jnp.stack([g["dt_vec"][0, 4:8] for g in grads]), dn_norm=jnp.stack([g["dn_norm"][0] for g in grads]),
        gm_norm=jnp.stack([g["gm_norm"][0] for g in grads]), spatial_w=jnp.stack([g["spatial_w"] for g in grads]),
        spatial_b=jnp.stack([g["spatial_b"][:, 0, :] for g in grads]),
        sinks=jnp.stack([g["sink_vec"][0, :8] for g in grads]))
    packed = jnp.pad(_pack_small(small_local) + after_start, ((0, 8), (0, 0)))
    gsmall_packed = _allreduce_small(packed.reshape(2, -1, 1024)).reshape(-1, 1024)[:_SMALL_ROWS]

    d_s, m_s, v_s = _ew("adamw_small", _adamw_fn, [_pack_small(w), gsmall_packed, _pack_small(m), _pack_small(v)], 3)
    gsmall, dsmall, msmall, vsmall = (_unpack_small(p) for p in (gsmall_packed, d_s, m_s, v_s))
    g_o, d_o, m_o, v_o = dict(gsmall), dict(dsmall), dict(msmall), dict(vsmall)
    tr = lambda a: a.transpose(0, 2, 1)
    w_t, m_t, v_t = tr(w["w_in"]), tr(m["w_in"]), tr(v["w_in"])

    def update(l, totals, into):
        outs = {}
        for n, t in zip(_BIG, totals):
            g_l = t.reshape(local[n].shape[1:])
            if n == "w_in":
                outs[n] = _adamw_rows("adamw_" + n, l, w_t, g_l, m_t, v_t, into and into[n])
            else:
                outs[n] = _adamw_layer("adamw_" + n, l, w[n], g_l, m[n], v[n], into and into[n])
        return outs

    landed1[1] = landed1[1] + after_start.astype(landed1[1].dtype)
    outs1 = update(1, _pair_share(_chip_sums(landed1, pair1, me)), None)
    pair0, landed0 = _split_wait("scatter_l0_wait", "scatter", scattering[0], outs1["w_in"][0])
    outs = update(0, _pair_share(_chip_sums(landed0, pair0, me)), outs1)
    for n in _BIG:
        d_o[n], m_o[n], v_o[n], g_o[n] = [tr(o) for o in outs[n]] if n == "w_in" else outs[n]
    return (loss, dx[None], *[g_o[n] for n in _NAMES], *[d_o[n] for n in _NAMES], *[m_o[n] for n in _NAMES],
            *[v_o[n] for n in _NAMES])
```

```python
import collections
import functools

import jax
import jax.numpy as jnp
from jax import lax
from jax.experimental import pallas as pl
from jax.experimental.pallas import tpu as pltpu

F32 = jnp.float32
BF16 = jnp.bfloat16

D_MODEL = 1024
BRANCH_W = 512
MEM_LEN = 256
N_LAYER = 2
N_CHIP = 4
N_DEV = 8
EPS = 1e-6
NEG_INF = -1e30
DN_CHUNK = 64
LANES = 128
VMEM_LIMIT = 48 * 1024 * 1024

ADAM_LR, ADAM_B1, ADAM_B2, ADAM_EPS, ADAM_WD, ADAM_STEP = 0.001, 0.9, 0.999, 1e-08, 0.01, 10

N_PAD = 10240
O_GATE = 0
O_AQ, O_AK, O_AV, O_AZ = 4096, 4608, 5120, 5632
O_BUV, O_BZ = 6144, 7168
O_CKV, O_BA = 7680, 7936
O_CQ, O_CZ = 8192, 8704
O_MQ, O_MZ = 9216, 9728
O_MISC, W_MISC = O_CKV, 512
_PAD_SEGS = [(5896, 4096), (0, 512), (512, 512), (1024, 512), (1536, 512), (2056, 1024), (3080, 512),
             (4104, 128), (4232, 128), (2048, 8), (None, 120), (None, 128),
             (3592, 512), (4360, 512), (4872, 512), (5384, 512)]
D_IN = 9992
SHARD_IN = D_IN // N_CHIP


SHARD_PAD = 2560


def _pad_parts():
    parts, off = [], 0
    for s, n in _PAD_SEGS:
        a = s
        while s is not None and a < s + n:
            chip = a // SHARD_IN
            b = min(s + n, (chip + 1) * SHARD_IN)
            parts.append((chip, a - chip * SHARD_IN, off + a - s, b - a))
            a = b
        off += n
    return parts


PERM_ROWS = 512
PERM_SLACK = 32


def _permute_rows(name, src, parts, n_out, out_dtype):
    B, Z = PERM_ROWS, PERM_ROWS + PERM_SLACK
    w = src.shape[1]
    plans = []
    for blk in range(n_out // B):
        o, runs = blk * B, []
        for s, d, n in parts:
            lo, hi = max(d, o), min(d + n, o + B)
            if lo < hi:
                s0 = s + lo - d
                wa = s0 // 16 * 16
                wb = min(-(-(s0 + hi - lo) // 16) * 16, src.shape[0])
                runs.append((wa, wb - wa, s0 - (lo - o) - wa, lo - o, hi - o))
        plans.append(runs)
    max_runs = max(len(r) for r in plans)
    nblk = len(plans)

    def body(*refs):
        src_ref, out_ref, inbuf, obuf, insem, outsem = (refs[0],) + refs[-5:]

        def in_copies(blk):
            return [pltpu.make_async_copy(src_ref.at[pl.ds(wa, ws)], inbuf.at[blk % 2, r, pl.ds(0, ws)],
                                          insem.at[blk % 2, r]) for r, (wa, ws, _, _, _) in enumerate(plans[blk])]

        def out_copy(blk):
            return pltpu.make_async_copy(obuf.at[blk % 2], out_ref.at[pl.ds(blk * B, B)], outsem.at[blk % 2])

        for cp in in_copies(0):
            cp.start()
        rid = _iota((B, 1), 0)
        for blk in range(nblk):
            if blk + 1 < nblk:
                for cp in in_copies(blk + 1):
                    cp.start()
            for cp in in_copies(blk):
                cp.wait()
            val = jnp.zeros((B, w), F32)
            for r, (wa, ws, t, l0, l1) in enumerate(plans[blk]):
                win = jnp.concatenate([inbuf[blk % 2, r, pl.ds(0, ws)].astype(F32), jnp.zeros((Z - ws, w), F32)], axis=0)
                moved = pltpu.roll(win, (-t) % Z, 0)[:B]
                val = jnp.where((rid >= l0) & (rid < l1), moved, val)
            if blk >= 2:
                out_copy(blk - 2).wait()
            obuf[blk % 2] = val.astype(out_dtype)
            out_copy(blk).start()
        for blk in range(max(nblk - 2, 0), nblk):
            out_copy(blk).wait()

    return pl.pallas_call(
        body, name=name, in_specs=[_ANY], out_specs=_ANY, out_shape=jax.ShapeDtypeStruct((n_out, w), out_dtype),
        scratch_shapes=[pltpu.VMEM((2, max_runs, Z, w), src.dtype), pltpu.VMEM((2, B, w), out_dtype),
                        pltpu.SemaphoreType.DMA((2, max_runs)), pltpu.SemaphoreType.DMA((2,))],
        compiler_params=_params(),
    )(src)


def _slab_parts():
    h, out = SHARD_PAD // 2, []
    for chip, s, d, n in _pad_parts():
        a = s
        while a < s + n:
            half = a // h
            b = min(s + n, (half + 1) * h)
            out.append(((half * N_CHIP + chip) * h + a - half * h, d + a - s, b - a))
            a = b
    return out


def _w_pad_from_slabs(slabs):
    return _permute_rows("w_pad_rows", slabs.reshape(-1, slabs.shape[-1]), _slab_parts(), N_PAD, BF16)


def _slabs_from_pad(dw):
    slabs = _permute_rows("w_pad_grad_rows", dw, [(d, s, n) for s, d, n in _slab_parts()], N_CHIP * SHARD_PAD, BF16)
    return slabs.reshape(2, N_CHIP, SHARD_PAD // 2, dw.shape[1])


def _dot(a, b, dims, prec):
    if prec == "bf16":
        return lax.dot_general(a.astype(BF16), b.astype(BF16), (dims, ((), ())), preferred_element_type=F32)
    return lax.dot_general(a, b, (dims, ((), ())), precision=lax.Precision.HIGHEST, preferred_element_type=F32)


_NN, _NT, _TN = ((1,), (0,)), ((1,), (1,)), ((0,), (0,))


def _make_mm(prec):
    @jax.custom_vjp
    def nn(a, b):
        return _dot(a, b, _NN, prec)

    @jax.custom_vjp
    def nt(a, b):
        return _dot(a, b, _NT, prec)

    @jax.custom_vjp
    def tn(a, b):
        return _dot(a, b, _TN, prec)

    nn.defvjp(lambda a, b: (nn(a, b), (a, b)), lambda r, g: (nt(g, r[1]), tn(r[0], g)))
    nt.defvjp(lambda a, b: (nt(a, b), (a, b)), lambda r, g: (nn(g, r[1]), tn(g, r[0])))
    tn.defvjp(lambda a, b: (tn(a, b), (a, b)), lambda r, g: (nt(r[1], g), nn(r[0], g)))
    return nn, nt, tn


_nn16, _nt16, _tn16 = _make_mm("bf16")
_nn32, _nt32, _tn32 = _make_mm("f32")


def _make_slice(axis):
    @functools.partial(jax.custom_vjp, nondiff_argnums=(1, 2, 3))
    def sl(x, a, b, n):
        return x[a:b] if axis == 0 else x[:, a:b]

    def fwd(x, a, b, n):
        return sl(x, a, b, n), None

    def bwd(a, b, n, _, g):
        parts = []
        if a > 0:
            parts.append(jnp.zeros((a, g.shape[1]) if axis == 0 else (g.shape[0], a), g.dtype))
        parts.append(g)
        if n - b > 0:
            parts.append(jnp.zeros((n - b, g.shape[1]) if axis == 0 else (g.shape[0], n - b), g.dtype))
        return (jnp.concatenate(parts, axis=axis),)

    sl.defvjp(fwd, bwd)
    return sl


_sl0, _sl1 = _make_slice(0), _make_slice(1)


def _rowsl(x, a, b):
    return _sl0(x, a, b, x.shape[0])


def _cols(x, a, b):
    return _sl1(x, a, b, x.shape[1])


@functools.partial(jax.custom_vjp, nondiff_argnums=(1,))
def _rollr(x, s):
    return pltpu.roll(x, s, 0)


_rollr.defvjp(lambda x, s: (_rollr(x, s), None),
              lambda s, _, g: (pltpu.roll(g, g.shape[0] - s, 0),))


def _iota(shape, axis):
    return lax.broadcasted_iota(jnp.int32, shape, axis)


def _sigmoid(x):
    return lax.logistic(x)


def _silu(x):
    return x * _sigmoid(x)


def _gelu(x):
    return 0.5 * x * (1.0 + jnp.tanh(0.7978845608028654 * (x + 0.044715 * (x * x * x))))


def _softplus(x):
    return jnp.maximum(x, 0.0) + jnp.log(1.0 + jnp.exp(-jnp.abs(x)))


def _rms(x, g):
    return x * lax.rsqrt(jnp.mean(x * x, axis=-1, keepdims=True) + EPS) * g


def _lane_pick(x, lane):
    return jnp.sum(x * (_iota((1, x.shape[1]), 1) == lane).astype(F32), axis=1, keepdims=True)


Row = collections.namedtuple("Row", "arr w cb hb grad", defaults=(0, True))


def _full_spec(shape):
    return pl.BlockSpec(shape, lambda i, _n=len(shape): (0,) * _n)


def _load_params(refs):
    return [[p[g].astype(F32) for g in range(p.shape[0])] if len(p.shape) == 3 else p[...].astype(F32)
            for p in refs]


def _params(**kw):
    return pltpu.CompilerParams(vmem_limit_bytes=VMEM_LIMIT, **kw)


def _rows_fwd(name, fn, rows, params, outs, tr, carry=None):
    T = rows[0].arr.shape[0]
    n = T // tr
    halos = [r for r in rows if r.hb]
    nr, nh, npar, no = len(rows), len(halos), len(params), len(outs)

    def body(*refs):
        row_refs, halo_refs = refs[:nr], refs[nr:nr + nh]
        par_refs = refs[nr + nh:nr + nh + npar]
        out_refs = refs[nr + nh + npar:nr + nh + npar + no]
        rest = refs[nr + nh + npar + no:]
        first = pl.program_id(0) == 0
        cvals = None
        if carry is not None:
            csave_ref, carry_ref = rest

            @pl.when(first)
            def _():
                carry_ref[...] = jnp.zeros_like(carry_ref)

            cvals = [carry_ref[g] for g in range(carry[0])]
            for g in range(carry[0]):
                csave_ref[0, g] = cvals[g]
        c_out, o = fn(first, cvals, [r[...].astype(F32) for r in row_refs],
                      [h[...].astype(F32) for h in halo_refs], _load_params(par_refs))
        for r, v in zip(out_refs, o):
            r[...] = v.astype(r.dtype)
        if carry is not None:
            for g in range(carry[0]):
                carry_ref[g] = c_out[g]

    in_specs = [pl.BlockSpec((tr, r.w), lambda i, c=r.cb: (i, c)) for r in rows]
    in_specs += [pl.BlockSpec((r.hb, r.w), lambda i, c=r.cb, q=tr // r.hb: (jnp.maximum(i * q - 1, 0), c))
                 for r in halos]
    in_specs += [_full_spec(p.shape) for p in params]
    out_shape = [jax.ShapeDtypeStruct((T, w), dt) for w, dt in outs]
    out_specs = [pl.BlockSpec((tr, w), lambda i: (i, 0)) for w, _ in outs]
    scratch = []
    if carry is not None:
        out_shape.append(jax.ShapeDtypeStruct((n,) + carry, F32))
        out_specs.append(pl.BlockSpec((1,) + carry, lambda i: (i, 0, 0, 0)))
        scratch.append(pltpu.VMEM(carry, F32))
    return pl.pallas_call(
        body, name=name, grid=(n,), in_specs=in_specs, out_specs=out_specs, out_shape=out_shape,
        scratch_shapes=scratch, compiler_params=_params(dimension_semantics=("arbitrary",)),
    )(*[r.arr for r in rows], *[r.arr for r in halos], *params)


def _rows_bwd(name, fn, rows, params, douts, tr, carry=None, csave=None, dcols=None):
    T = rows[0].arr.shape[0]
    n = T // tr
    halos = [r for r in rows if r.hb]
    grows = [r for r in rows if r.grad is True]
    crows = [r for r in rows if r.grad == "cols"]
    wcols = sum(r.w for r in crows)
    nr, nh, npar, nd, ng = len(rows), len(halos), len(params), len(douts), len(grows)
    nc = 0 if carry is None else 1
    ncol = 1 if crows else 0
    nalias = 1 if (crows and dcols is not None) else 0

    def body(*refs):
        row_refs, halo_refs = refs[:nr], refs[nr:nr + nh]
        par_refs = refs[nr + nh:nr + nh + npar]
        k = nr + nh + npar
        csave_ref = refs[k] if nc else None
        dout_refs = refs[k + nc:k + nc + nd]
        k = k + nc + nd + nalias
        drow_refs = refs[k:k + ng]
        dcols_ref = refs[k + ng] if ncol else None
        dpar_refs = refs[k + ng + ncol:k + ng + ncol + npar]
        k = k + ng + ncol + npar
        dcarry_ref = refs[k] if nc else None
        hgrad_refs = refs[k + nc:]
        i = pl.program_id(0)
        first_tile = i == n - 1

        @pl.when(i == 0)
        def _():
            for r in dpar_refs:
                r[...] = jnp.zeros_like(r)
            for r in hgrad_refs:
                r[...] = jnp.zeros_like(r)
            if nc:
                dcarry_ref[...] = jnp.zeros_like(dcarry_ref)

        rv = [r[...].astype(F32) for r in row_refs]
        hv = [h[...].astype(F32) for h in halo_refs]
        pv = _load_params(par_refs)
        dov = [d[...].astype(F32) for d in dout_refs]
        if nc:
            cv = [csave_ref[0, g] for g in range(carry[0])]
            _, vjp = jax.vjp(lambda c, r, h, p: fn(first_tile, c, r, h, p), cv, rv, hv, pv)
            dc, dr, dh, dp = vjp(([dcarry_ref[g] for g in range(carry[0])], dov))
            for g in range(carry[0]):
                dcarry_ref[g] = dc[g]
        else:
            _, vjp = jax.vjp(lambda r, h, p: fn(first_tile, None, r, h, p)[1], rv, hv, pv)
            dr, dh, dp = vjp(dov)
        gi = hi = 0
        pieces = []
        for kk, r in enumerate(rows):
            d = dr[kk]
            if r.hb:
                carried = hgrad_refs[hi][...]
                d = d + (carried if tr == r.hb else
                         jnp.concatenate([jnp.zeros((tr - r.hb, r.w), F32), carried], axis=0))
                hgrad_refs[hi][...] = dh[hi]
                hi += 1
            if r.grad is True:
                drow_refs[gi][...] = d.astype(drow_refs[gi].dtype)
                gi += 1
            elif r.grad == "cols":
                pieces.append(d.astype(BF16))
        if ncol:
            dcols_ref[...] = pieces[0] if len(pieces) == 1 else jnp.concatenate(pieces, axis=1)
        for r, d in zip(dpar_refs, dp):
            if len(r.shape) == 3:
                for g in range(r.shape[0]):
                    r[g] += d[g]
            else:
                r[...] += d

    rev = lambda i: n - 1 - i
    in_specs = [pl.BlockSpec((tr, r.w), lambda i, c=r.cb: (rev(i), c)) for r in rows]
    in_specs += [pl.BlockSpec((r.hb, r.w), lambda i, c=r.cb, q=tr // r.hb: (jnp.maximum(rev(i) * q - 1, 0), c))
                 for r in halos]
    in_specs += [_full_spec(p.shape) for p in params]
    args = [r.arr for r in rows] + [r.arr for r in halos] + list(params)
    scratch = []
    if nc:
        in_specs.append(pl.BlockSpec((1,) + carry, lambda i: (rev(i), 0, 0, 0)))
        args.append(csave)
        scratch.append(pltpu.VMEM(carry, F32))
    douts = [d if isinstance(d, Row) else Row(d, d.shape[1], 0) for d in douts]
    in_specs += [pl.BlockSpec((tr, d.w), lambda i, c=d.cb: (rev(i), c)) for d in douts]
    args += [d.arr for d in douts]
    aliases = {}
    if nalias:
        aliases = {len(args): ng}
        in_specs.append(pl.BlockSpec(memory_space=pl.ANY))
        args.append(dcols)
    scratch += [pltpu.VMEM((r.hb, r.w), F32) for r in halos]
    out_shape = [jax.ShapeDtypeStruct((T, r.w), F32) for r in grows]
    out_specs = [pl.BlockSpec((tr, r.w), lambda i: (rev(i), 0)) for r in grows]
    if ncol:
        off = crows[0].cb * crows[0].w
        assert off % wcols == 0 and all(a.cb * a.w + a.w == b.cb * b.w for a, b in zip(crows, crows[1:]))
        out_shape.append(jax.ShapeDtypeStruct((T, N_PAD), BF16))
        out_specs.append(pl.BlockSpec((tr, wcols), lambda i, c=off // wcols: (rev(i), c)))
    out_shape += [jax.ShapeDtypeStruct(p.shape, F32) for p in params]
    out_specs += [_full_spec(p.shape) for p in params]
    res = pl.pallas_call(
        body, name=name, grid=(n,), in_specs=in_specs, out_specs=out_specs, out_shape=out_shape,
        scratch_shapes=scratch, input_output_aliases=aliases,
        compiler_params=_params(dimension_semantics=("arbitrary",)),
    )(*args)
    return list(res[:ng]), list(res[ng + ncol:]), (res[ng] if ncol else dcols)


def _fill_misc(dcols, dkv, dba, tr):
    T = dkv.shape[0]

    def body(kv_ref, ba_ref, _, o_ref):
        o_ref[...] = jnp.concatenate([kv_ref[...], ba_ref[...]], axis=1).astype(BF16)

    return pl.pallas_call(
        body, name="misc_bwd", grid=(T // tr,),
        in_specs=[pl.BlockSpec((tr, 256), lambda i: (i, 0)), pl.BlockSpec((tr, 256), lambda i: (i, 0)),
                  pl.BlockSpec(memory_space=pl.ANY)],
        out_specs=pl.BlockSpec((tr, W_MISC), lambda i: (i, O_MISC // W_MISC)),
        out_shape=jax.ShapeDtypeStruct((T, N_PAD), BF16), input_output_aliases={2: 0},
        compiler_params=_params(dimension_semantics=("arbitrary",)),
    )(dkv, dba, dcols)


def _up_bwd(ys, cols, dm, w_up):
    T, tr = dm.shape[0], UPB_TR

    def body(y_ref, gl_ref, dm_ref, w_ref, dy_ref, dgl_ref, dw_ref):
        @pl.when(pl.program_id(1) == 0)
        def _():
            dw_ref[...] = jnp.zeros_like(dw_ref)

        _, vjp = jax.vjp(lambda y, gl, w: _sigmoid(gl) * _nn16(y, w),
                         y_ref[...].astype(F32), gl_ref[...].astype(F32), w_ref[...].astype(F32))
        dy, dgl, dw = vjp(dm_ref[...])
        dy_ref[...] = dy
        dgl_ref[...] = dgl.astype(BF16)
        dw_ref[...] += dw

    branch_rows = lambda w: pl.BlockSpec((tr, w), lambda n, i: (i, n))
    weight = pl.BlockSpec((None, BRANCH_W, D_MODEL), lambda n, i: (n, 0, 0))
    return pl.pallas_call(
        body, name="up_bwd", grid=(4, T // tr),
        in_specs=[branch_rows(BRANCH_W), branch_rows(D_MODEL), pl.BlockSpec((tr, D_MODEL), lambda n, i: (i, 0)), weight],
        out_specs=[branch_rows(BRANCH_W), branch_rows(D_MODEL), weight],
        out_shape=[jax.ShapeDtypeStruct((T, 4 * BRANCH_W), F32), jax.ShapeDtypeStruct((T, N_PAD), BF16),
                   jax.ShapeDtypeStruct(w_up.shape, F32)],
        compiler_params=_params(dimension_semantics=("arbitrary", "arbitrary")),
    )(ys, cols, dm, w_up)


def _matmul(name, a, b, kind, out_dtype, tm, tn, tk, after=None):
    if kind == "tn":
        (K, M), N = a.shape, b.shape[1]
    else:
        (M, K), N = a.shape, (b.shape[0] if kind == "nt" else b.shape[1])
    tm, tn, tk = min(tm, M), min(tn, N), min(tk, K)
    nk = K // tk
    dims = {"nn": _NN, "nt": _NT, "tn": _TN}[kind]

    n_after = 0 if after is None else 1

    def body(*refs):
        a_ref, b_ref, o_ref, acc = refs[0], refs[1], refs[2 + n_after], refs[3 + n_after:]
        part = lax.dot_general(a_ref[...], b_ref[...], (dims, ((), ())), preferred_element_type=F32)
        if nk == 1:
            o_ref[...] = part.astype(o_ref.dtype)
            return
        acc_ref = acc[0] if acc else o_ref
        k = pl.program_id(2)

        @pl.when(k == 0)
        def _():
            acc_ref[...] = part

        @pl.when(k > 0)
        def _():
            acc_ref[...] += part

        if acc:
            @pl.when(k == nk - 1)
            def _():
                o_ref[...] = acc_ref[...].astype(o_ref.dtype)

    a_spec = pl.BlockSpec((tk, tm), lambda i, j, k: (k, i)) if kind == "tn" else pl.BlockSpec((tm, tk), lambda i, j, k: (i, k))
    b_spec = pl.BlockSpec((tn, tk), lambda i, j, k: (j, k)) if kind == "nt" else pl.BlockSpec((tk, tn), lambda i, j, k: (k, j))
    return pl.pallas_call(
        body, name=name, grid=(M // tm, N // tn, nk), in_specs=[a_spec, b_spec] + [_ANY] * n_after,
        out_specs=pl.BlockSpec((tm, tn), lambda i, j, k: (i, j)),
        out_shape=jax.ShapeDtypeStruct((M, N), out_dtype),
        scratch_shapes=[pltpu.VMEM((tm, tn), F32)] if nk > 1 and out_dtype != F32 else [],
        compiler_params=_params(dimension_semantics=("arbitrary", "arbitrary", "arbitrary")),
    )(a, b, *([] if after is None else [after]))


def _pre_fn(first, _, rows, halos, params):
    return None, [_rms(rows[0], params[0])]


def _pre_fn_res(first, _, rows, halos, params):
    return None, [_rms(rows[0], params[0]), rows[0]]


def _memkv_fn(first, _, rows, halos, params):
    g, w = params
    return None, [_nn16(_rms(rows[0], g), w)]


def _conv_silu(x, halo, w4, keep_halo):
    tr = x.shape[0]
    halo = halo * keep_halo
    rid = _iota((tr, 1), 0)
    acc = w4[3] * x
    for s in (1, 2, 3):
        hs = jnp.concatenate([_rollr(halo, s), jnp.zeros((tr - halo.shape[0], x.shape[1]), F32)], axis=0)
        acc = acc + w4[3 - s] * jnp.where(rid < s, hs, _rollr(x, s))
    return _silu(acc)


def _dn_fn(first, S, rows, halos, params):
    qp, kp, vp, z, ba = rows
    conv, a_vec, dt_vec, dnorm = params
    ba = _cols(ba, 0, LANES)
    tr = qp.shape[0]
    keep = jnp.where(first, 0.0, 1.0)
    q = _conv_silu(qp, halos[0], [conv[3 * j + 0] for j in range(4)], keep)
    k = _conv_silu(kp, halos[1], [conv[3 * j + 1] for j in range(4)], keep)
    v = _conv_silu(vp, halos[2], [conv[3 * j + 2] for j in range(4)], keep)
    qh, kh, vh = [], [], []
    for h in range(4):
        a, b = h * LANES, (h + 1) * LANES
        xq, xk = _cols(q, a, b), _cols(k, a, b)
        qh.append(xq * lax.rsqrt(jnp.sum(xq * xq, axis=1, keepdims=True) + EPS) * (LANES ** -0.5))
        kh.append(xk * lax.rsqrt(jnp.sum(xk * xk, axis=1, keepdims=True) + EPS))
        vh.append(_cols(v, a, b))
    beta_all = _sigmoid(ba)
    g_all = -jnp.exp(a_vec) * _softplus(ba + dt_vec)
    C = DN_CHUNK
    ii, jj = _iota((C, C), 0), _iota((C, C), 1)
    strict, incl = ii > jj, ii >= jj
    eye = (ii == jj).astype(F32)
    last_row = (_iota((C, 1), 0) == C - 1).astype(F32)
    n_chunk = tr // C
    pairs = [(c, h) for c in range(n_chunk) for h in range(4)]
    rows_of = lambda a, c: _rowsl(a, c * C, (c + 1) * C)
    gcs = [_nn32(incl.astype(F32), rows_of(g_all, c)) for c in range(n_chunk)]
    qc = {(c, h): rows_of(qh[h], c) for c, h in pairs}
    kc = {(c, h): rows_of(kh[h], c) for c, h in pairs}
    beta = {(c, h): _lane_pick(rows_of(beta_all, c), h) for c, h in pairs}
    gc = {(c, h): _lane_pick(gcs[c], 4 + h) for c, h in pairs}
    dec = {p: jnp.exp(jnp.where(incl, gc[p] - jnp.sum(eye * gc[p], axis=0, keepdims=True), 0.0)) for p in pairs}
    egc = {p: jnp.exp(gc[p]) for p in pairs}
    kb = {p: kc[p] * beta[p] for p in pairs}
    kq = {p: _nt16(jnp.concatenate([kb[p], qc[p]], axis=0), kc[p]) for p in pairs}
    P = {p: -jnp.where(strict, _rowsl(kq[p], 0, C) * dec[p], 0.0) for p in pairs}
    aqk = {p: jnp.where(incl, _rowsl(kq[p], C, 2 * C) * dec[p], 0.0) for p in pairs}
    tinv = {p: eye + P[p] for p in pairs}
    P = {p: _nn16(P[p], P[p]) for p in pairs}
    for j in range(5):
        if j < 4:
            pt = {p: _nn16(jnp.concatenate([P[p], tinv[p]], axis=0), P[p]) for p in pairs}
            tinv = {p: tinv[p] + _rowsl(pt[p], C, 2 * C) for p in pairs}
            P = {p: _rowsl(pt[p], 0, C) for p in pairs}
        else:
            tinv = {p: tinv[p] + _nn16(tinv[p], P[p]) for p in pairs}
    uw = {(c, h): _nn16(tinv[c, h], jnp.concatenate([rows_of(vh[h], c) * beta[c, h], kb[c, h] * egc[c, h]], axis=1))
          for c, h in pairs}
    S = list(S)
    ychunks = []
    for c in range(n_chunk):
        zc = rows_of(z, c)
        hs = range(4)
        ws = [_nn16(jnp.concatenate([_cols(uw[c, h], LANES, 2 * LANES), qc[c, h] * egc[c, h]], axis=0), S[h]) for h in hs]
        vnew = [_cols(uw[c, h], 0, LANES) - _rowsl(ws[h], 0, C) for h in hs]
        o = [_rowsl(ws[h], C, 2 * C) + _nn16(aqk[c, h], vnew[h]) for h in hs]
        glast = [jnp.sum(gc[c, h] * last_row, axis=0, keepdims=True) for h in hs]
        S = [S[h] * jnp.exp(glast[h]) + _tn16(kc[c, h] * jnp.exp(glast[h] - gc[c, h]), vnew[h]) for h in hs]
        ychunks.append(jnp.concatenate(
            [_rms(o[h], dnorm) * _silu(_cols(zc, h * LANES, (h + 1) * LANES)) for h in hs], axis=1))
    return S, [jnp.concatenate(ychunks, axis=0)]


def _gm_fn(first, _, rows, halos, params):
    uv, z = rows
    gnorm, ws, bs = params
    tr = uv.shape[0]
    guv = _gelu(uv)
    u = _cols(guv, 0, BRANCH_W)
    v = _rms(_cols(guv, BRANCH_W, 2 * BRANCH_W), gnorm)
    ii, jj = _iota((LANES, LANES), 0), _iota((LANES, LANES), 1)
    eye = (ii == jj).astype(F32)
    wsm = [jnp.where(ii >= jj, ws[g], 0.0) for g in range(4)]
    bcol = [jnp.sum(eye * bs[g], axis=1, keepdims=True) for g in range(4)]
    chunks = []
    for c in range(tr // LANES):
        vc = _rowsl(v, c * LANES, (c + 1) * LANES)
        chunks.append(jnp.concatenate(
            [_nn16(wsm[g], _cols(vc, g * LANES, (g + 1) * LANES)) + bcol[g] for g in range(4)], axis=1))
    return None, [u * jnp.concatenate(chunks, axis=0) * _silu(z)]


def _swa_fn(first, _, rows, halos, params):
    q, kvc, z = rows
    sink_vec = params[0]
    P = LANES
    kv = jnp.concatenate([halos[0], kvc], axis=0)
    k, v = _cols(kv, 0, P), _cols(kv, P, 2 * P)
    r, cc = _iota((P, P), 0), _iota((P, P), 1)
    lane = _iota((1, P), 1)
    dist = _iota((P, 2 * P), 0) + P - _iota((P, 2 * P), 1)
    kmin = jnp.where(first, P, 0)
    valid = (dist >= 0) & (dist < P) & (_iota((P, 2 * P), 1) >= kmin)
    halves = [(lane < 64).astype(F32), (lane >= 64).astype(F32)]
    k_v = jnp.concatenate([k, v], axis=0)
    kkvv = [_nn16(k_v, (r == kh * 64 + (cc & 63)).astype(F32)) for kh in range(2)]
    scores = [_nt16(jnp.concatenate([_cols(q, (2 * kh + g // 2) * P, (2 * kh + g // 2 + 1) * P) * halves[g % 2]
                                     for g in range(4)], axis=0), _rowsl(kkvv[kh], 0, 2 * P)) for kh in range(2)]
    probs = []
    for kh in range(2):
        ps = []
        for g in range(4):
            s = jnp.where(valid, _rowsl(scores[kh], g * P, (g + 1) * P) * 0.125, NEG_INF)
            sink = _lane_pick(sink_vec, kh * 4 + g)
            m = lax.stop_gradient(jnp.maximum(jnp.max(s, axis=1, keepdims=True), sink))
            e = jnp.exp(s - m)
            ps.append(e / (jnp.sum(e, axis=1, keepdims=True) + jnp.exp(sink - m)))
        probs.append(jnp.concatenate(ps, axis=0))
    outs = [_nn16(probs[kh], _rowsl(kkvv[kh], 2 * P, 4 * P)) for kh in range(2)]
    blocks = [_rowsl(outs[j // 2], (2 * (j % 2)) * P, (2 * (j % 2) + 1) * P) * halves[0]
              + _rowsl(outs[j // 2], (2 * (j % 2) + 1) * P, (2 * (j % 2) + 2) * P) * halves[1] for j in range(4)]
    return None, [jnp.concatenate(blocks, axis=1) * _silu(z)]


def _mem_fn(first, _, rows, halos, params):
    q, z = rows
    mkv = params[0]
    heads = [(h * LANES, (h + 1) * LANES) for h in range(4)]
    scores = [_nt16(_cols(q, a, b), _cols(mkv, a, b)) * (LANES ** -0.5) for a, b in heads]
    probs = []
    for s in scores:
        e = jnp.exp(s - lax.stop_gradient(jnp.max(s, axis=1, keepdims=True)))
        probs.append(e / jnp.sum(e, axis=1, keepdims=True))
    outs = [_nn16(p, _cols(mkv, BRANCH_W + a, BRANCH_W + b)) for p, (a, b) in zip(probs, heads)]
    return None, [jnp.concatenate(outs, axis=1) * _silu(z)]


def _up_fn(first, _, rows, halos, params):
    ys, gl, w_up = rows[:4], rows[4], params[0]
    merged = None
    for n in range(4):
        term = _sigmoid(_cols(gl, n * D_MODEL, (n + 1) * D_MODEL)) * _nn16(ys[n], w_up[n])
        merged = term if merged is None else merged + term
    return None, [merged]


def _out_fn(first, _, rows, halos, params):
    x, merged = rows
    w, g = params
    return None, [x + _rms(_nn16(merged, w), g)]


def _loss_fn(first, _, rows, halos, params):
    y, t = rows
    d = y - t
    lrow = 0.5 * jnp.mean(d * d, axis=1, keepdims=True)
    return None, [d * (1.0 / D_MODEL), jnp.broadcast_to(lrow, (y.shape[0], LANES))]


TR = 256
DN_TR = 256
UP_TR = 256
UPB_TR = 512
CONV_HALO = 16
CARRY = (4, LANES, LANES)


def _branch_rows(cols, g):
    hb = CONV_HALO
    a = [Row(cols, 512, O_AQ // 512, hb, g), Row(cols, 512, O_AK // 512, hb, g), Row(cols, 512, O_AV // 512, hb, g),
         Row(cols, 512, O_AZ // 512, 0, g), Row(cols, 256, O_BA // 256)]
    b = [Row(cols, 1024, O_BUV // 1024, 0, g), Row(cols, 512, O_BZ // 512, 0, g)]
    c = [Row(cols, 512, O_CQ // 512, 0, g), Row(cols, 256, O_CKV // 256, LANES), Row(cols, 512, O_CZ // 512, 0, g)]
    m = [Row(cols, 512, O_MQ // 512, 0, g), Row(cols, 512, O_MZ // 512, 0, g)]
    return a, b, c, m


def _layer_fwd(x, mem, W, late_weights=None):
    h = _rows_fwd("prenorm_fwd", _pre_fn, [Row(x, D_MODEL, 0)], [W["norm_pre"]], [(D_MODEL, BF16)], TR)[0]
    cols = _matmul("in_proj_fwd", h, W["w_pad"], "nt", BF16, 2048, 1024, 1024)
    if late_weights is not None:
        W = dict(W, **late_weights(cols))
    mem_kv = _rows_fwd("memkv_fwd", _memkv_fn, [Row(mem, D_MODEL, 0)], [W["norm_mem"], W["w_mem_kv"]],
                       [(D_MODEL, F32)], MEM_LEN)[0]
    ra, rb, rc, rm = _branch_rows(cols, True)
    y_a, csave = _rows_fwd("dn_fwd", _dn_fn, ra, [W["conv"], W["a_vec"], W["dt_vec"], W["dn_norm"]],
                           [(BRANCH_W, BF16)], DN_TR, CARRY)
    y_b = _rows_fwd("gm_fwd", _gm_fn, rb, [W["gm_norm"], W["spatial_w"], W["spatial_b"]], [(BRANCH_W, BF16)], TR)[0]
    y_c = _rows_fwd("swa_fwd", _swa_fn, rc, [W["sink_vec"]], [(BRANCH_W, BF16)], LANES)[0]
    y_m = _rows_fwd("mem_fwd", _mem_fn, rm, [mem_kv], [(BRANCH_W, BF16)], TR)[0]
    ys = [y_a, y_b, y_c, y_m]
    merged = _rows_fwd("up_fwd", _up_fn, [Row(y, BRANCH_W, 0) for y in ys] + [Row(cols, 4 * D_MODEL, 0)],
                       [W["w_up"]], [(D_MODEL, BF16)], UP_TR)[0]
    x_new = _rows_fwd("out_fwd", _out_fn, [Row(x, D_MODEL, 0), Row(merged, D_MODEL, 0)],
                      [W["w_out"], W["norm_post"]], [(D_MODEL, F32)], TR)[0]
    return x_new, dict(x=x, h=h, cols=cols, mem_kv=mem_kv, csave=csave, ys=ys, merged=merged), W


def _layer_bwd(dxn, mem, W, sv, on_weight_grads=None):
    x, cols = sv["x"], sv["cols"]
    (dx_res, dm), (dw_out, dnorm_post), _ = _rows_bwd(
        "out_bwd", _out_fn, [Row(x, D_MODEL, 0), Row(sv["merged"], D_MODEL, 0)], [W["w_out"], W["norm_post"]],
        [dxn], TR)
    dys, dcols, dw_up = _up_bwd(jnp.concatenate(sv["ys"], axis=1), cols, dm, W["w_up"])
    dys = [Row(dys, BRANCH_W, n) for n in range(4)]
    ra, rb, rc, rm = _branch_rows(cols, "cols")
    (dba,), (dconv, da_vec, ddt_vec, ddn_norm), dcols = _rows_bwd(
        "dn_bwd", _dn_fn, ra, [W["conv"], W["a_vec"], W["dt_vec"], W["dn_norm"]], [dys[0]], DN_TR, CARRY,
        sv["csave"], dcols=dcols)
    _, (dgm_norm, dws, dbs), dcols = _rows_bwd(
        "gm_bwd", _gm_fn, rb, [W["gm_norm"], W["spatial_w"], W["spatial_b"]], [dys[1]], TR, dcols=dcols)
    (dkv_c,), (dsink,), dcols = _rows_bwd("swa_bwd", _swa_fn, rc, [W["sink_vec"]], [dys[2]], LANES, dcols=dcols)
    _, (dmem_kv,), dcols = _rows_bwd("mem_bwd", _mem_fn, rm, [sv["mem_kv"]], [dys[3]], TR, dcols=dcols)
    dcols = _fill_misc(dcols, dkv_c, dba, TR)
    _, (dnorm_mem, dw_mem_kv), _ = _rows_bwd("memkv_bwd", _memkv_fn, [Row(mem, D_MODEL, 0, 0, False)],
                                             [W["norm_mem"], W["w_mem_kv"]], [dmem_kv], MEM_LEN)
    dw_pad = _matmul("in_proj_dw", dcols, sv["h"], "tn", F32, 1024, 1024, 2048)
    grads = dict(norm_post=dnorm_post, norm_mem=dnorm_mem, w_pad=dw_pad, conv=dconv,
                 a_vec=da_vec, dt_vec=ddt_vec, dn_norm=ddn_norm, gm_norm=dgm_norm, spatial_w=dws, spatial_b=dbs,
                 sink_vec=dsink, w_mem_kv=dw_mem_kv, w_up=dw_up, w_out=dw_out)
    started = None if on_weight_grads is None else on_weight_grads(grads)
    dh = _matmul("in_proj_dx", dcols, W["w_pad"], "nn", F32, 1024, 1024, 1024, after=started)
    (dx,), (grads["norm_pre"],), _ = _rows_bwd("prenorm_bwd", _pre_fn_res, [Row(x, D_MODEL, 0)], [W["norm_pre"]],
                                               [dh, dx_res], TR)
    return dx, grads


def _lane_vec(v, off):
    return jnp.zeros((1, LANES), F32).at[0, off:off + v.shape[0]].set(v)


def _layer_weights(l, w_pad, conv_w, small, **late):
    return dict(
        late, w_pad=w_pad, conv=conv_w.reshape(4, 3, BRANCH_W).reshape(12, 1, BRANCH_W),
        norm_pre=small["norm_pre"][l][None], norm_post=small["norm_post"][l][None],
        norm_mem=small["norm_mem"][l][None],
        a_vec=_lane_vec(small["a_log"][l], 4), dt_vec=_lane_vec(small["dt_bias"][l], 4),
        dn_norm=small["dn_norm"][l][None], gm_norm=small["gm_norm"][l][None],
        spatial_w=small["spatial_w"][l], spatial_b=small["spatial_b"][l][:, None, :],
        sink_vec=_lane_vec(small["sinks"][l], 0))


_MESH = pl.DeviceIdType.MESH
_ANY = pl.BlockSpec(memory_space=pl.ANY)


def _position():
    return lax.axis_index("x"), lax.axis_index("y"), lax.axis_index("c")


def _remote(src, dst, send_sem, recv_sem, dev):
    return pltpu.make_async_remote_copy(src_ref=src, dst_ref=dst, send_sem=send_sem, recv_sem=recv_sem,
                                        device_id=dev, device_id_type=_MESH)


def _hbm_call(name, body, arrs, out_shapes, sems, aliases=None):
    return pl.pallas_call(
        body, name=name, in_specs=[_ANY] * len(arrs), out_specs=[_ANY] * len(out_shapes), out_shape=out_shapes,
        scratch_shapes=[pltpu.SemaphoreType.DMA((k,)) for k in sems], input_output_aliases=aliases or {},
        compiler_params=pltpu.CompilerParams(has_side_effects=True),
    )(*arrs)


def _other_chips(x, y):
    return [(1 - x, y), (x, 1 - y), (1 - x, 1 - y)]


def _gather_weights(arrs, relayed):
    n = len(arrs)

    def body(*refs):
        ins, outs = refs[:n], refs[n:2 * n]
        ici_send, ici_recv, d2d_send, d2d_recv = refs[2 * n:]
        x, y, c = _position()
        me = 2 * x + y
        xn, yn, dg = _other_chips(x, y)
        chip = lambda p: 2 * p[0] + p[1]
        sends = []

        def go(cp):
            cp.start()
            sends.append(cp)

        def ici(a, j, src, dst, to):
            return _remote(src, dst, ici_send.at[4 * a + j], ici_recv.at[4 * a + j], (*to, c))

        for a in range(n):
            go(ici(a, 0, ins[a].at[c], outs[a].at[c, me], xn))
            go(ici(a, 1, ins[a].at[c], outs[a].at[c, me], yn))
            if not relayed[a]:
                go(ici(a, 2, ins[a].at[c], outs[a].at[c, me], dg))
        for a in range(n):
            h = arrs[a].shape[1] // 2
            from_x, from_y = outs[a].at[c, chip(xn)], outs[a].at[c, chip(yn)]
            ici(a, 0, ins[a].at[c], from_x, xn).wait_recv()
            if relayed[a]:
                go(ici(a, 2, from_x.at[pl.ds(0, h)], from_x.at[pl.ds(0, h)], yn))
            ici(a, 1, ins[a].at[c], from_y, yn).wait_recv()
            if relayed[a]:
                go(ici(a, 3, from_y.at[pl.ds(h, h)], from_y.at[pl.ds(h, h)], xn))
            for j, slab in enumerate((from_x, from_y)):
                go(_remote(slab, slab, d2d_send.at[3 * a + j], d2d_recv.at[3 * a + j], (x, y, 1 - c)))
        for a in range(n):
            h = arrs[a].shape[1] // 2
            from_d = outs[a].at[c, chip(dg)]
            if relayed[a]:
                ici(a, 2, from_d.at[pl.ds(0, h)], from_d.at[pl.ds(0, h)], yn).wait_recv()
                ici(a, 3, from_d.at[pl.ds(h, h)], from_d.at[pl.ds(h, h)], xn).wait_recv()
            else:
                ici(a, 2, ins[a].at[c], from_d, dg).wait_recv()
            go(_remote(from_d, from_d, d2d_send.at[3 * a + 2], d2d_recv.at[3 * a + 2], (x, y, 1 - c)))
        for a in range(n):
            for j, p in enumerate((xn, yn, dg)):
                slab = outs[a].at[1 - c, chip(p)]
                _remote(slab, slab, d2d_send.at[3 * a + j], d2d_recv.at[3 * a + j], (x, y, 1 - c)).wait_recv()
        for cp in sends:
            cp.wait_send()

    return _hbm_call("gather_weights", body, arrs,
                     [jax.ShapeDtypeStruct((N_LAYER, N_CHIP) + a.shape[1:], a.dtype) for a in arrs],
                     [4 * n, 4 * n, 3 * n, 3 * n])


def _pair_exchange(arrs):
    n = len(arrs)

    def body(*refs):
        ins, outs = refs[:n], refs[n:2 * n]
        send_sems, recv_sems = refs[2 * n:]
        x, y, c = _position()
        cps = [_remote(ins[a].at[1 - c], outs[a], send_sems.at[a], recv_sems.at[a], (x, y, 1 - c)) for a in range(n)]
        for cp in cps:
            cp.start()
        for cp in cps:
            cp.wait_recv()
        for cp in cps:
            cp.wait_send()

    return _hbm_call("pair_exchange", body, arrs, [jax.ShapeDtypeStruct(a.shape[1:], a.dtype) for a in arrs], [n, n])


def _chip_scatter(arrs):
    n = len(arrs)

    def body(*refs):
        ins, outs = refs[:n], refs[n:2 * n]
        send_sems, recv_sems = refs[2 * n:]
        x, y, c = _position()
        me = 2 * x + y
        sends = []
        for a in range(n):
            for j, (px, py) in enumerate(_other_chips(x, y)):
                sends.append(_remote(ins[a].at[2 * px + py], outs[a].at[me], send_sems.at[3 * a + j],
                                     recv_sems.at[3 * a + j], (px, py, c)))
                sends[-1].start()
        for a in range(n):
            for j, (px, py) in enumerate(_other_chips(x, y)):
                _remote(ins[a].at[me], outs[a].at[2 * px + py], send_sems.at[3 * a + j], recv_sems.at[3 * a + j],
                        (px, py, c)).wait_recv()
        for cp in sends:
            cp.wait_send()

    return _hbm_call("chip_scatter", body, arrs, [jax.ShapeDtypeStruct(a.shape, a.dtype) for a in arrs],
                     [3 * n, 3 * n])


def _pair_share(arrs):
    n = len(arrs)

    def body(*refs):
        ins, outs = refs[:n], refs[n:2 * n]
        send_sems, recv_sems = refs[2 * n:]
        x, y, c = _position()
        cps = [_remote(ins[a].at[c], outs[a].at[c], send_sems.at[a], recv_sems.at[a], (x, y, 1 - c)) for a in range(n)]
        for cp in cps:
            cp.start()
        for a in range(n):
            _remote(ins[a].at[c], outs[a].at[1 - c], send_sems.at[a], recv_sems.at[a], (x, y, 1 - c)).wait_recv()
        for cp in cps:
            cp.wait_send()

    return _hbm_call("pair_share", body, arrs, [jax.ShapeDtypeStruct(a.shape, a.dtype) for a in arrs], [n, n],
                     {a: a for a in range(n)})


def _pair_forward(arrs):
    n = len(arrs)

    def body(*refs):
        ins, outs = refs[:n], refs[n:2 * n]
        send_sems, recv_sems = refs[2 * n:]
        x, y, c = _position()
        sends = []
        for a in range(n):
            for j, (px, py) in enumerate(_other_chips(x, y)):
                sends.append(_remote(ins[a].at[c, 2 * px + py], outs[a].at[c, 2 * px + py], send_sems.at[3 * a + j],
                                     recv_sems.at[3 * a + j], (x, y, 1 - c)))
                sends[-1].start()
        for a in range(n):
            for j, (px, py) in enumerate(_other_chips(x, y)):
                slab = outs[a].at[1 - c, 2 * px + py]
                _remote(slab, slab, send_sems.at[3 * a + j], recv_sems.at[3 * a + j], (x, y, 1 - c)).wait_recv()
        for cp in sends:
            cp.wait_send()

    return _hbm_call("pair_forward", body, arrs, [jax.ShapeDtypeStruct(a.shape, a.dtype) for a in arrs],
                     [3 * n, 3 * n], {a: a for a in range(n)})


_HBM = pl.BlockSpec(memory_space=pltpu.HBM)
_SEM = pl.BlockSpec(memory_space=pltpu.SEMAPHORE)
_EFFECT = pltpu.SideEffectType.DATAFLOW_SIDE_EFFECTING


def _chip_copies(kind, srcs, lands, send_sems, recv_sems):
    x, y, c = _position()
    me = 2 * x + y
    sends, recvs = [], []
    for a in range(len(srcs)):
        for j, (px, py) in enumerate(_other_chips(x, y)):
            s, sems, dev = 2 * px + py, (send_sems.at[3 * a + j], recv_sems.at[3 * a + j]), (px, py, c)
            if kind == "gather":
                sends.append(_remote(srcs[a].at[c], lands[a].at[c, me], *sems, dev))
                recvs.append(_remote(srcs[a].at[c], lands[a].at[c, s], *sems, dev))
            else:
                sends.append(_remote(srcs[a].at[s], lands[a].at[me], *sems, dev))
                recvs.append(_remote(srcs[a].at[me], lands[a].at[s], *sems, dev))
    return sends, recvs


def _split_start(name, kind, srcs, land_shapes, after):
    n = len(srcs)

    def body(*refs):
        sends, _ = _chip_copies(kind, refs[:n], refs[n:2 * n], refs[2 * n + 1], refs[2 * n + 2])
        for cp in sends:
            cp.start()
        refs[-1][...] = jnp.zeros_like(refs[-1])

    hbm = lambda a: pltpu.with_memory_space_constraint(a, pltpu.HBM)
    lands = [lax.empty(s.shape, s.dtype) for s in land_shapes]
    outs = pl.pallas_call(
        body, name=name, in_specs=[_HBM] * (2 * n) + [_ANY],
        out_specs=[_SEM, _SEM] + [_HBM] * (2 * n) + [pl.BlockSpec(memory_space=pltpu.VMEM)],
        out_shape=[pltpu.SemaphoreType.DMA((3 * n,)), pltpu.SemaphoreType.DMA((3 * n,))]
        + [pltpu.HBM(a.shape, a.dtype) for a in list(srcs) + lands] + [jax.ShapeDtypeStruct((8, LANES), F32)],
        input_output_aliases={i: 2 + i for i in range(2 * n)},
        compiler_params=pltpu.CompilerParams(has_side_effects=_EFFECT),
    )(*[hbm(a) for a in srcs], *[hbm(a) for a in lands], after)
    return outs[0], outs[1], list(outs[2:2 + 2 * n]), outs[-1]


def _split_wait(name, kind, started, after):
    send_sems, recv_sems, thru, _ = started
    n = len(thru) // 2

    def body(*refs):
        sends, recvs = _chip_copies(kind, refs[:n], refs[n:2 * n], refs[2 * n], refs[2 * n + 1])
        for cp in sends:
            cp.wait_send()
        for cp in recvs:
            cp.wait_recv()

    outs = pl.pallas_call(
        body, name=name, in_specs=[_HBM] * (2 * n) + [_SEM, _SEM, _ANY], out_specs=[_HBM] * (2 * n),
        out_shape=[pltpu.HBM(a.shape, a.dtype) for a in thru], input_output_aliases={i: i for i in range(2 * n)},
        compiler_params=pltpu.CompilerParams(has_side_effects=_EFFECT),
    )(*thru, send_sems, recv_sems, after)
    return list(outs[:n]), list(outs[n:])


def _allreduce_small(g):
    def body(g_ref, o_ref, pair_buf, chip_buf, send_sems, recv_sems):
        x, y, c = _position()
        me = 2 * x + y
        sib = (x, y, 1 - c)
        to_sib = _remote(g_ref.at[1 - c], pair_buf, send_sems.at[0], recv_sems.at[0], sib)
        to_sib.start()
        to_sib.wait_recv()
        chip_buf[me] = g_ref[c] + pair_buf[...]
        sends = [to_sib]
        chips = _other_chips(x, y)
        for j, (px, py) in enumerate(chips):
            sends.append(_remote(chip_buf.at[me], chip_buf.at[me], send_sems.at[1 + j], recv_sems.at[1 + j], (px, py, c)))
            sends[-1].start()
        for j, (px, py) in enumerate(chips):
            _remote(chip_buf.at[me], chip_buf.at[2 * px + py], send_sems.at[1 + j], recv_sems.at[1 + j],
                    (px, py, c)).wait_recv()
        o_ref[c] = ((chip_buf[0] + chip_buf[1]) + chip_buf[2]) + chip_buf[3]
        sends.append(_remote(o_ref.at[c], o_ref.at[c], send_sems.at[4], recv_sems.at[4], sib))
        sends[-1].start()
        _remote(o_ref.at[c], o_ref.at[1 - c], send_sems.at[4], recv_sems.at[4], sib).wait_recv()
        for cp in sends:
            cp.wait_send()

    vmem = pl.BlockSpec(memory_space=pltpu.VMEM)
    return pl.pallas_call(
        body, name="allreduce_small", in_specs=[vmem], out_specs=vmem, out_shape=jax.ShapeDtypeStruct(g.shape, F32),
        scratch_shapes=[pltpu.VMEM(g.shape[1:], F32), pltpu.VMEM((N_CHIP,) + g.shape[1:], F32),
                        pltpu.SemaphoreType.DMA((5,)), pltpu.SemaphoreType.DMA((5,))],
        compiler_params=_params(),
    )(g)


EW_ROWS = 512


def _ew(name, fn, ins, n_out, out_dtype=F32, out_slot=None, into=None):
    def dims(a):
        return a[0].shape[1:] if isinstance(a, tuple) else a.shape

    R, w = dims(ins[0])
    tr = EW_ROWS if R % EW_ROWS == 0 else R
    n_into = len(into) if into else 0

    def body(c_ref, *refs):
        outs = fn(*[r[...] for r in refs[:len(ins)]])
        for r, v in zip(refs[len(ins) + n_into:], outs):
            r[...] = v.astype(r.dtype)

    def lead_spec(l):
        if l == "c":
            return pl.BlockSpec((None, tr, w), lambda i, c_ref: (c_ref[0], i, 0))
        return pl.BlockSpec((None, tr, w), lambda i, c_ref, s=l: (s, i, 0))

    plain = pl.BlockSpec((tr, w), lambda i, c_ref: (i, 0))
    in_specs = [lead_spec(a[1]) if isinstance(a, tuple) else plain for a in ins] + [_ANY] * n_into
    out_spec = plain if out_slot is None else lead_spec(out_slot)
    out_shape = jax.ShapeDtypeStruct((R, w) if out_slot is None else (2, R, w), out_dtype)
    return pl.pallas_call(
        body, name=name,
        grid_spec=pltpu.PrefetchScalarGridSpec(num_scalar_prefetch=1, grid=(R // tr,), in_specs=in_specs,
                                               out_specs=[out_spec] * n_out),
        out_shape=[out_shape] * n_out, input_output_aliases={1 + len(ins) + j: j for j in range(n_into)},
        compiler_params=_params(dimension_semantics=("arbitrary",)),
    )(lax.axis_index("c").astype(jnp.int32).reshape(1), *[a[0] if isinstance(a, tuple) else a for a in ins],
      *(into or []))


def _adamw_fn(w, g, m, v):
    m = ADAM_B1 * m + (1.0 - ADAM_B1) * g
    v = ADAM_B2 * v + (1.0 - ADAM_B2) * (g * g)
    m_hat = m / (1.0 - ADAM_B1 ** ADAM_STEP)
    v_hat = v / (1.0 - ADAM_B2 ** ADAM_STEP)
    delta = -ADAM_LR * (m_hat / (jnp.sqrt(v_hat) + ADAM_EPS) + ADAM_WD * w)
    return delta, m, v


def _adamw(name, w, g, m, v):
    shape = w.shape
    two = lambda a: a.reshape(-1, shape[-1])
    return [o.reshape(shape) for o in _ew(name, _adamw_fn, [two(w), two(g), two(m), two(v)], 3)]


def _adamw_layer(name, l, w, g, m, v, into):
    k = w.shape[-1]
    three = lambda a: (a.reshape(N_LAYER, -1, k), l)
    fn = lambda w_, g_, m_, v_: _adamw_fn(w_, g_, m_, v_) + (g_,)
    outs = _ew(name, fn, [three(w), g.reshape(-1, k), three(m), three(v)], 4, out_slot=l,
               into=None if into is None else [a.reshape(N_LAYER, -1, k) for a in into])
    return [o.reshape(w.shape) for o in outs]


def _adamw_rows(name, l, w, g, m, v, into):
    _, R, k = w.shape
    n_into = len(into) if into else 0

    def body(*refs):
        w_ref, g_ref, m_ref, v_ref = refs[:4]
        d_out, m_out, v_out, g_out = refs[4 + n_into:]
        g_blk = g_ref[...]
        d_out[...], m_out[...], v_out[...] = _adamw_fn(w_ref[...], g_blk, m_ref[...], v_ref[...])
        g_out[...] = g_blk

    spec = pl.BlockSpec((None, EW_ROWS, k), lambda i: (l, i, 0))
    return pl.pallas_call(
        body, name=name, grid=(-(-R // EW_ROWS),),
        in_specs=[spec, pl.BlockSpec((EW_ROWS, k), lambda i: (i, 0)), spec, spec] + [_ANY] * n_into,
        out_specs=[spec] * 4, out_shape=[jax.ShapeDtypeStruct((N_LAYER, R, k), F32)] * 4,
        input_output_aliases={4 + j: j for j in range(n_into)},
        compiler_params=_params(dimension_semantics=("arbitrary",)),
    )(w, g, m, v, *(into or []))


_SMALL = [("norm_pre", (2, 1024)), ("norm_post", (2, 1024)), ("norm_mem", (2, 1024)), ("a_log", (2, 4)),
          ("dt_bias", (2, 4)), ("dn_norm", (2, 128)), ("gm_norm", (2, 512)), ("spatial_w", (2, 4, 128, 128)),
          ("spatial_b", (2, 4, 128)), ("sinks", (2, 8))]
_SMALL_ROWS = 200
_BIG = ["w_in", "conv_w", "w_mem_kv", "w_up", "w_out"]
_NAMES = ["norm_pre", "norm_post", "norm_mem", "w_in", "conv_w", "a_log", "dt_bias", "dn_norm", "gm_norm",
          "spatial_w", "spatial_b", "sinks", "w_mem_kv", "w_up", "w_out"]


def _size(shape):
    n = 1
    for s in shape:
        n *= s
    return n


_PACK_UNIT = 8 * 1024


def _pack_small(d):
    rows = []
    for n, shp in _SMALL:
        flat = d[n].reshape(-1)
        rows.append(jnp.pad(flat, (0, -flat.shape[0] % _PACK_UNIT)).reshape(-1, 1024))
    assert sum(r.shape[0] for r in rows) == _SMALL_ROWS
    return jnp.concatenate(rows, axis=0)


def _unpack_small(p):
    out, off = {}, 0
    for n, shp in _SMALL:
        k = -(-_size(shp) // _PACK_UNIT) * 8
        out[n] = p[off:off + k].reshape(-1)[:_size(shp)].reshape(shp)
        off += k
    return out


_HALF_SHAPE = {"w_in": (SHARD_PAD // 2, D_MODEL), "conv_w": (2, 3 * BRANCH_W // N_CHIP), "w_mem_kv": (128, D_MODEL),
               "w_up": (2, BRANCH_W, D_MODEL // N_CHIP), "w_out": (128, D_MODEL)}


def _chip_major(g):
    g = jnp.swapaxes(g, 0, 1)
    return g.reshape((N_CHIP, 2 * g.shape[2]) + g.shape[3:])


def _half_major(g):
    g = g.reshape((N_CHIP, 2, g.shape[1] // 2) + g.shape[2:])
    return jnp.swapaxes(g, 0, 1).astype(BF16)


N_EARLY = 2


def _early_views(l, g_in, g_conv, small):
    return _layer_weights(l, _w_pad_from_slabs(g_in),
                          _chip_major(g_conv).transpose(1, 0, 2).reshape(4, 3 * BRANCH_W), small)


def _late_views(g_kv, g_up, g_out):
    return dict(w_mem_kv=_chip_major(g_kv).reshape(D_MODEL, D_MODEL),
                w_up=_chip_major(g_up).transpose(1, 2, 0, 3).reshape(4, BRANCH_W, D_MODEL),
                w_out=_chip_major(g_out).reshape(D_MODEL, D_MODEL))


def _pair_sums(g):
    big = [_slabs_from_pad(g["w_pad"]),
           _half_major(g["conv"].reshape(4, N_CHIP, 3 * BRANCH_W // N_CHIP).transpose(1, 0, 2)),
           _half_major(g["w_mem_kv"].reshape(N_CHIP, D_MODEL // N_CHIP, D_MODEL)),
           _half_major(g["w_up"].reshape(4, BRANCH_W, N_CHIP, D_MODEL // N_CHIP).transpose(2, 0, 1, 3)),
           _half_major(g["w_out"].reshape(N_CHIP, D_MODEL // N_CHIP, D_MODEL))]
    add2 = lambda a, b: [a.astype(F32) + b.astype(F32)]
    pair = []
    for n, b, p in zip(_BIG, big, _pair_exchange(big)):
        k = b.shape[-1]
        pair.append(_ew("pair_sum_" + n, add2, [(b.reshape(2, -1, k), "c"), p.reshape(-1, k)], 1, BF16)[0]
                    .reshape(p.shape))
    return pair


def _chip_sums(landed, pair, me):
    add4 = lambda a, b, c_, d: [((a.astype(F32) + b.astype(F32)) + c_.astype(F32)) + d.astype(F32)]
    totals = []
    for n, r, q in zip(_BIG, landed, pair):
        r = _own_slot(r, lax.dynamic_index_in_dim(q, me, 0), me, 0)
        k = r.shape[-1]
        totals.append(_ew("chip_sum_" + n, add4, [(r.reshape(N_CHIP, -1, k), s) for s in range(N_CHIP)], 1,
                          out_slot="c")[0].reshape((2,) + r.shape[1:]))
    return totals


def _own_slot(buf, mine, me, axis):
    return lax.dynamic_update_index_in_dim(buf, mine.astype(buf.dtype), me, axis)


def kernel(x, mem, norm_pre, norm_post, norm_mem, w_in, conv_w, a_log, dt_bias, dn_norm, gm_norm, spatial_w, spatial_b, sinks, w_mem_kv, w_up, w_out, loss_target, m_norm_pre, m_norm_post, m_norm_mem, m_w_in, m_conv_w, m_a_log, m_dt_bias, m_dn_norm, m_gm_norm, m_spatial_w, m_spatial_b, m_sinks, m_w_mem_kv, m_w_up, m_w_out, v_norm_pre, v_norm_post, v_norm_mem, v_w_in, v_conv_w, v_a_log, v_dt_bias, v_dn_norm, v_gm_norm, v_spatial_w, v_spatial_b, v_sinks, v_w_mem_kv, v_w_up, v_w_out):
    w = dict(norm_pre=norm_pre, norm_post=norm_post, norm_mem=norm_mem, w_in=w_in, conv_w=conv_w, a_log=a_log,
             dt_bias=dt_bias, dn_norm=dn_norm, gm_norm=gm_norm, spatial_w=spatial_w, spatial_b=spatial_b, sinks=sinks,
             w_mem_kv=w_mem_kv, w_up=w_up, w_out=w_out)
    m = dict(norm_pre=m_norm_pre, norm_post=m_norm_post, norm_mem=m_norm_mem, w_in=m_w_in, conv_w=m_conv_w,
             a_log=m_a_log, dt_bias=m_dt_bias, dn_norm=m_dn_norm, gm_norm=m_gm_norm, spatial_w=m_spatial_w,
             spatial_b=m_spatial_b, sinks=m_sinks, w_mem_kv=m_w_mem_kv, w_up=m_w_up, w_out=m_w_out)
    v = dict(norm_pre=v_norm_pre, norm_post=v_norm_post, norm_mem=v_norm_mem, w_in=v_w_in, conv_w=v_conv_w,
             a_log=v_a_log, dt_bias=v_dt_bias, dn_norm=v_dn_norm, gm_norm=v_gm_norm, spatial_w=v_spatial_w,
             spatial_b=v_spatial_b, sinks=v_sinks, w_mem_kv=v_w_mem_kv, w_up=v_w_up, w_out=v_w_out)
    me = 2 * lax.axis_index("x") + lax.axis_index("y")

    w_in_t = jnp.pad(w_in.astype(BF16).transpose(0, 2, 1), ((0, 0), (0, SHARD_PAD - SHARD_IN), (0, 0)))
    local = dict(w_in=w_in_t, conv_w=conv_w, w_mem_kv=w_mem_kv.astype(BF16), w_up=w_up.astype(BF16),
                 w_out=w_out.astype(BF16))
    halves = lambda l: [local[n][l].reshape((2,) + _HALF_SHAPE[n]) for n in _BIG]
    own = lambda gathered, mine: [_own_slot(g, h[:, None], me, 1) for g, h in zip(gathered, mine)]
    lands = [jax.ShapeDtypeStruct((2, N_CHIP) + _HALF_SHAPE[n], local[n].dtype) for n in _BIG]
    h0 = halves(0)
    g0 = own(_gather_weights(h0[:N_EARLY], [True, False]), h0[:N_EARLY])
    rest0 = _split_start("gather_l0_rest_start", "gather", h0[N_EARLY:], lands[N_EARLY:], g0[1])
    started = _split_start("gather_l1_start", "gather", halves(1), lands, rest0[3])

    xl, meml = x[0], mem[0]
    W0 = _early_views(0, g0[0], g0[1], w)
    W0["norm_pre"] = W0["norm_pre"] + started[3][0, 0]

    def late0(cols):
        mine, landed = _split_wait("gather_l0_rest_wait", "gather", rest0, cols)
        return _late_views(*own(_pair_forward(landed), mine))

    x1, sv0, W0 = _layer_fwd(xl, meml, W0, late0)
    mine1, landed1 = _split_wait("gather_l1_wait", "gather", started, x1)
    g1 = own(_pair_forward(landed1), mine1)
    W1 = dict(_early_views(1, g1[0], g1[1], w), **_late_views(*g1[N_EARLY:]))
    x2, sv1, _ = _layer_fwd(x1, meml, W1)
    dy, lrows = _rows_fwd("loss", _loss_fn, [Row(x2, D_MODEL, 0), Row(loss_target[0], D_MODEL, 0)], [],
                          [(D_MODEL, F32), (LANES, F32)], TR)
    loss = lax.psum(jnp.sum(lrows[:, 0]), ("x", "y", "c"))

    scattering = {}

    def start_scatter(l):
        def on_weight_grads(g):
            pair = _pair_sums(g)
            scattering[l] = _split_start("scatter_l%d_start" % l, "scatter", pair,
                                         [jax.ShapeDtypeStruct(p.shape, p.dtype) for p in pair], pair[1])
            return scattering[l][3]
        return on_weight_grads

    dx1, grads1 = _layer_bwd(dy, meml, W1, sv1, start_scatter(1))
    dx, grads0 = _layer_bwd(dx1, meml, W0, sv0, start_scatter(0))
    pair1, landed1 = _split_wait("scatter_l1_wait", "scatter", scattering[1], dx)
    after_start = scattering[0][3][0, 0]
    grads = [grads0, grads1]

    small_local = dict(
        norm_pre=jnp.stack([g["norm_pre"][0] for g in grads]), norm_post=jnp.stack([g["norm_post"][0] for g in grads]),
        norm_mem=jnp.stack([g["norm_mem"][0] for g in grads]), a_log=jnp.stack([g["a_vec"][0, 4:8] for g in grads]),
        dt_bias=jnp.stack([g["dt_vec"][0, 4:8] for g in grads]), dn_norm=jnp.stack([g["dn_norm"][0] for g in grads]),
        gm_norm=jnp.stack([g["gm_norm"][0] for g in grads]), spatial_w=jnp.stack([g["spatial_w"] for g in grads]),
        spatial_b=jnp.stack([g["spatial_b"][:, 0, :] for g in grads]),
        sinks=jnp.stack([g["sink_vec"][0, :8] for g in grads]))
    packed = jnp.pad(_pack_small(small_local) + after_start, ((0, 8), (0, 0)))
    gsmall_packed = _allreduce_small(packed.reshape(2, -1, 1024)).reshape(-1, 1024)[:_SMALL_ROWS]

    d_s, m_s, v_s = _ew("adamw_small", _adamw_fn, [_pack_small(w), gsmall_packed, _pack_small(m), _pack_small(v)], 3)
    gsmall, dsmall, msmall, vsmall = (_unpack_small(p) for p in (gsmall_packed, d_s, m_s, v_s))
    g_o, d_o, m_o, v_o = dict(gsmall), dict(dsmall), dict(msmall), dict(vsmall)
    tr = lambda a: a.transpose(0, 2, 1)
    w_t, m_t, v_t = tr(w["w_in"]), tr(m["w_in"]), tr(v["w_in"])

    def update(l, totals, into):
        outs = {}
        for n, t in zip(_BIG, totals):
            g_l = t.reshape(local[n].shape[1:])
            if n == "w_in":
                outs[n] = _adamw_rows("adamw_" + n, l, w_t, g_l, m_t, v_t, into and into[n])
            else:
                outs[n] = _adamw_layer("adamw_" + n, l, w[n], g_l, m[n], v[n], into and into[n])
        return outs

    landed1[1] = landed1[1] + after_start.astype(landed1[1].dtype)
    outs1 = update(1, _pair_share(_chip_sums(landed1, pair1, me)), None)
    pair0, landed0 = _split_wait("scatter_l0_wait", "scatter", scattering[0], outs1["w_in"][0])
    outs = update(0, _pair_share(_chip_sums(landed0, pair0, me)), outs1)
    for n in _BIG:
        d_o[n], m_o[n], v_o[n], g_o[n] = [tr(o) for o in outs[n]] if n == "w_in" else outs[n]
    return (loss, dx[None], *[g_o[n] for n in _NAMES], *[d_o[n] for n in _NAMES], *[m_o[n] for n in _NAMES],
            *[v_o[n] for n in _NAMES])
```

```python
import collections
import functools

import jax
import jax.numpy as jnp
from jax import lax
from jax.experimental import pallas as pl
from jax.experimental.pallas import tpu as pltpu

F32 = jnp.float32
BF16 = jnp.bfloat16

D_MODEL = 1024
BRANCH_W = 512
MEM_LEN = 256
N_LAYER = 2
N_CHIP = 4
N_DEV = 8
EPS = 1e-6
NEG_INF = -1e30
DN_CHUNK = 64
LANES = 128
VMEM_LIMIT = 48 * 1024 * 1024

ADAM_LR, ADAM_B1, ADAM_B2, ADAM_EPS, ADAM_WD, ADAM_STEP = 0.001, 0.9, 0.999, 1e-08, 0.01, 10

N_PAD = 10240
O_GATE = 0
O_AQ, O_AK, O_AV, O_AZ = 4096, 4608, 5120, 5632
O_BUV, O_BZ = 6144, 7168
O_CKV, O_BA = 7680, 7936
O_CQ, O_CZ = 8192, 8704
O_MQ, O_MZ = 9216, 9728
O_MISC, W_MISC = O_CKV, 512
_PAD_SEGS = [(5896, 4096), (0, 512), (512, 512), (1024, 512), (1536, 512), (2056, 1024), (3080, 512),
             (4104, 128), (4232, 128), (2048, 8), (None, 120), (None, 128),
             (3592, 512), (4360, 512), (4872, 512), (5384, 512)]
D_IN = 9992
SHARD_IN = D_IN // N_CHIP


SHARD_PAD = 2560


def _pad_parts():
    parts, off = [], 0
    for s, n in _PAD_SEGS:
        a = s
        while s is not None and a < s + n:
            chip = a // SHARD_IN
            b = min(s + n, (chip + 1) * SHARD_IN)
            parts.append((chip, a - chip * SHARD_IN, off + a - s, b - a))
            a = b
        off += n
    return parts


PERM_ROWS = 512
PERM_SLACK = 32


def _permute_rows(name, src, parts, n_out, out_dtype):
    B, Z = PERM_ROWS, PERM_ROWS + PERM_SLACK
    w = src.shape[1]
    plans = []
    for blk in range(n_out // B):
        o, runs = blk * B, []
        for s, d, n in parts:
            lo, hi = max(d, o), min(d + n, o + B)
            if lo < hi:
                s0 = s + lo - d
                wa = s0 // 16 * 16
                wb = min(-(-(s0 + hi - lo) // 16) * 16, src.shape[0])
                runs.append((wa, wb - wa, s0 - (lo - o) - wa, lo - o, hi - o))
        plans.append(runs)
    max_runs = max(len(r) for r in plans)
    nblk = len(plans)

    def body(*refs):
        src_ref, out_ref, inbuf, obuf, insem, outsem = (refs[0],) + refs[-5:]

        def in_copies(blk):
            return [pltpu.make_async_copy(src_ref.at[pl.ds(wa, ws)], inbuf.at[blk % 2, r, pl.ds(0, ws)],
                                          insem.at[blk % 2, r]) for r, (wa, ws, _, _, _) in enumerate(plans[blk])]

        def out_copy(blk):
            return pltpu.make_async_copy(obuf.at[blk % 2], out_ref.at[pl.ds(blk * B, B)], outsem.at[blk % 2])

        for cp in in_copies(0):
            cp.start()
        rid = _iota((B, 1), 0)
        for blk in range(nblk):
            if blk + 1 < nblk:
                for cp in in_copies(blk + 1):
                    cp.start()
            for cp in in_copies(blk):
                cp.wait()
            val = jnp.zeros((B, w), F32)
            for r, (wa, ws, t, l0, l1) in enumerate(plans[blk]):
                win = jnp.concatenate([inbuf[blk % 2, r, pl.ds(0, ws)].astype(F32), jnp.zeros((Z - ws, w), F32)], axis=0)
                moved = pltpu.roll(win, (-t) % Z, 0)[:B]
                val = jnp.where((rid >= l0) & (rid < l1), moved, val)
            if blk >= 2:
                out_copy(blk - 2).wait()
            obuf[blk % 2] = val.astype(out_dtype)
            out_copy(blk).start()
        for blk in range(max(nblk - 2, 0), nblk):
            out_copy(blk).wait()

    return pl.pallas_call(
        body, name=name, in_specs=[_ANY], out_specs=_ANY, out_shape=jax.ShapeDtypeStruct((n_out, w), out_dtype),
        scratch_shapes=[pltpu.VMEM((2, max_runs, Z, w), src.dtype), pltpu.VMEM((2, B, w), out_dtype),
                        pltpu.SemaphoreType.DMA((2, max_runs)), pltpu.SemaphoreType.DMA((2,))],
        compiler_params=_params(),
    )(src)


def _slab_parts():
    h, out = SHARD_PAD // 2, []
    for chip, s, d, n in _pad_parts():
        a = s
        while a < s + n:
            half = a // h
            b = min(s + n, (half + 1) * h)
            out.append(((half * N_CHIP + chip) * h + a - half * h, d + a - s, b - a))
            a = b
    return out


def _w_pad_from_slabs(slabs):
    return _permute_rows("w_pad_rows", slabs.reshape(-1, slabs.shape[-1]), _slab_parts(), N_PAD, BF16)


def _slabs_from_pad(dw):
    slabs = _permute_rows("w_pad_grad_rows", dw, [(d, s, n) for s, d, n in _slab_parts()], N_CHIP * SHARD_PAD, BF16)
    return slabs.reshape(2, N_CHIP, SHARD_PAD // 2, dw.shape[1])


def _dot(a, b, dims, prec):
    if prec == "bf16":
        return lax.dot_general(a.astype(BF16), b.astype(BF16), (dims, ((), ())), preferred_element_type=F32)
    return lax.dot_general(a, b, (dims, ((), ())), precision=lax.Precision.HIGHEST, preferred_element_type=F32)


_NN, _NT, _TN = ((1,), (0,)), ((1,), (1,)), ((0,), (0,))


def _make_mm(prec):
    @jax.custom_vjp
    def nn(a, b):
        return _dot(a, b, _NN, prec)

    @jax.custom_vjp
    def nt(a, b):
        return _dot(a, b, _NT, prec)

    @jax.custom_vjp
    def tn(a, b):
        return _dot(a, b, _TN, prec)

    nn.defvjp(lambda a, b: (nn(a, b), (a, b)), lambda r, g: (nt(g, r[1]), tn(r[0], g)))
    nt.defvjp(lambda a, b: (nt(a, b), (a, b)), lambda r, g: (nn(g, r[1]), tn(g, r[0])))
    tn.defvjp(lambda a, b: (tn(a, b), (a, b)), lambda r, g: (nt(r[1], g), nn(r[0], g)))
    return nn, nt, tn


_nn16, _nt16, _tn16 = _make_mm("bf16")
_nn32, _nt32, _tn32 = _make_mm("f32")


def _make_slice(axis):
    @functools.partial(jax.custom_vjp, nondiff_argnums=(1, 2, 3))
    def sl(x, a, b, n):
        return x[a:b] if axis == 0 else x[:, a:b]

    def fwd(x, a, b, n):
        return sl(x, a, b, n), None

    def bwd(a, b, n, _, g):
        parts = []
        if a > 0:
            parts.append(jnp.zeros((a, g.shape[1]) if axis == 0 else (g.shape[0], a), g.dtype))
        parts.append(g)
        if n - b > 0:
            parts.append(jnp.zeros((n - b, g.shape[1]) if axis == 0 else (g.shape[0], n - b), g.dtype))
        return (jnp.concatenate(parts, axis=axis),)

    sl.defvjp(fwd, bwd)
    return sl


_sl0, _sl1 = _make_slice(0), _make_slice(1)


def _rowsl(x, a, b):
    return _sl0(x, a, b, x.shape[0])


def _cols(x, a, b):
    return _sl1(x, a, b, x.shape[1])


@functools.partial(jax.custom_vjp, nondiff_argnums=(1,))
def _rollr(x, s):
    return pltpu.roll(x, s, 0)


_rollr.defvjp(lambda x, s: (_rollr(x, s), None),
              lambda s, _, g: (pltpu.roll(g, g.shape[0] - s, 0),))


def _iota(shape, axis):
    return lax.broadcasted_iota(jnp.int32, shape, axis)


def _sigmoid(x):
    return lax.logistic(x)


def _silu(x):
    return x * _sigmoid(x)


def _gelu(x):
    return 0.5 * x * (1.0 + jnp.tanh(0.7978845608028654 * (x + 0.044715 * (x * x * x))))


def _softplus(x):
    return jnp.maximum(x, 0.0) + jnp.log(1.0 + jnp.exp(-jnp.abs(x)))


def _rms(x, g):
    return x * lax.rsqrt(jnp.mean(x * x, axis=-1, keepdims=True) + EPS) * g


def _lane_pick(x, lane):
    return jnp.sum(x * (_iota((1, x.shape[1]), 1) == lane).astype(F32), axis=1, keepdims=True)


Row = collections.namedtuple("Row", "arr w cb hb grad", defaults=(0, True))


def _full_spec(shape):
    return pl.BlockSpec(shape, lambda i, _n=len(shape): (0,) * _n)


def _load_params(refs):
    return [[p[g].astype(F32) for g in range(p.shape[0])] if len(p.shape) == 3 else p[...].astype(F32)
            for p in refs]


def _params(**kw):
    return pltpu.CompilerParams(vmem_limit_bytes=VMEM_LIMIT, **kw)


def _rows_fwd(name, fn, rows, params, outs, tr, carry=None):
    T = rows[0].arr.shape[0]
    n = T // tr
    halos = [r for r in rows if r.hb]
    nr, nh, npar, no = len(rows), len(halos), len(params), len(outs)

    def body(*refs):
        row_refs, halo_refs = refs[:nr], refs[nr:nr + nh]
        par_refs = refs[nr + nh:nr + nh + npar]
        out_refs = refs[nr + nh + npar:nr + nh + npar + no]
        rest = refs[nr + nh + npar + no:]
        first = pl.program_id(0) == 0
        cvals = None
        if carry is not None:
            csave_ref, carry_ref = rest

            @pl.when(first)
            def _():
                carry_ref[...] = jnp.zeros_like(carry_ref)

            cvals = [carry_ref[g] for g in range(carry[0])]
            for g in range(carry[0]):
                csave_ref[0, g] = cvals[g]
        c_out, o = fn(first, cvals, [r[...].astype(F32) for r in row_refs],
                      [h[...].astype(F32) for h in halo_refs], _load_params(par_refs))
        for r, v in zip(out_refs, o):
            r[...] = v.astype(r.dtype)
        if carry is not None:
            for g in range(carry[0]):
                carry_ref[g] = c_out[g]

    in_specs = [pl.BlockSpec((tr, r.w), lambda i, c=r.cb: (i, c)) for r in rows]
    in_specs += [pl.BlockSpec((r.hb, r.w), lambda i, c=r.cb, q=tr // r.hb: (jnp.maximum(i * q - 1, 0), c))
                 for r in halos]
    in_specs += [_full_spec(p.shape) for p in params]
    out_shape = [jax.ShapeDtypeStruct((T, w), dt) for w, dt in outs]
    out_specs = [pl.BlockSpec((tr, w), lambda i: (i, 0)) for w, _ in outs]
    scratch = []
    if carry is not None:
        out_shape.append(jax.ShapeDtypeStruct((n,) + carry, F32))
        out_specs.append(pl.BlockSpec((1,) + carry, lambda i: (i, 0, 0, 0)))
        scratch.append(pltpu.VMEM(carry, F32))
    return pl.pallas_call(
        body, name=name, grid=(n,), in_specs=in_specs, out_specs=out_specs, out_shape=out_shape,
        scratch_shapes=scratch, compiler_params=_params(dimension_semantics=("arbitrary",)),
    )(*[r.arr for r in rows], *[r.arr for r in halos], *params)


def _rows_bwd(name, fn, rows, params, douts, tr, carry=None, csave=None, dcols=None):
    T = rows[0].arr.shape[0]
    n = T // tr
    halos = [r for r in rows if r.hb]
    grows = [r for r in rows if r.grad is True]
    crows = [r for r in rows if r.grad == "cols"]
    wcols = sum(r.w for r in crows)
    nr, nh, npar, nd, ng = len(rows), len(halos), len(params), len(douts), len(grows)
    nc = 0 if carry is None else 1
    ncol = 1 if crows else 0
    nalias = 1 if (crows and dcols is not None) else 0

    def body(*refs):
        row_refs, halo_refs = refs[:nr], refs[nr:nr + nh]
        par_refs = refs[nr + nh:nr + nh + npar]
        k = nr + nh + npar
        csave_ref = refs[k] if nc else None
        dout_refs = refs[k + nc:k + nc + nd]
        k = k + nc + nd + nalias
        drow_refs = refs[k:k + ng]
        dcols_ref = refs[k + ng] if ncol else None
        dpar_refs = refs[k + ng + ncol:k + ng + ncol + npar]
        k = k + ng + ncol + npar
        dcarry_ref = refs[k] if nc else None
        hgrad_refs = refs[k + nc:]
        i = pl.program_id(0)
        first_tile = i == n - 1

        @pl.when(i == 0)
        def _():
            for r in dpar_refs:
                r[...] = jnp.zeros_like(r)
            for r in hgrad_refs:
                r[...] = jnp.zeros_like(r)
            if nc:
                dcarry_ref[...] = jnp.zeros_like(dcarry_ref)

        rv = [r[...].astype(F32) for r in row_refs]
        hv = [h[...].astype(F32) for h in halo_refs]
        pv = _load_params(par_refs)
        dov = [d[...].astype(F32) for d in dout_refs]
        if nc:
            cv = [csave_ref[0, g] for g in range(carry[0])]
            _, vjp = jax.vjp(lambda c, r, h, p: fn(first_tile, c, r, h, p), cv, rv, hv, pv)
            dc, dr, dh, dp = vjp(([dcarry_ref[g] for g in range(carry[0])], dov))
            for g in range(carry[0]):
                dcarry_ref[g] = dc[g]
        else:
            _, vjp = jax.vjp(lambda r, h, p: fn(first_tile, None, r, h, p)[1], rv, hv, pv)
            dr, dh, dp = vjp(dov)
        gi = hi = 0
        pieces = []
        for kk, r in enumerate(rows):
            d = dr[kk]
            if r.hb:
                carried = hgrad_refs[hi][...]
                d = d + (carried if tr == r.hb else
                         jnp.concatenate([jnp.zeros((tr - r.hb, r.w), F32), carried], axis=0))
                hgrad_refs[hi][...] = dh[hi]
                hi += 1
            if r.grad is True:
                drow_refs[gi][...] = d.astype(drow_refs[gi].dtype)
                gi += 1
            elif r.grad == "cols":
                pieces.append(d.astype(BF16))
        if ncol:
            dcols_ref[...] = pieces[0] if len(pieces) == 1 else jnp.concatenate(pieces, axis=1)
        for r, d in zip(dpar_refs, dp):
            if len(r.shape) == 3:
                for g in range(r.shape[0]):
                    r[g] += d[g]
            else:
                r[...] += d

    rev = lambda i: n - 1 - i
    in_specs = [pl.BlockSpec((tr, r.w), lambda i, c=r.cb: (rev(i), c)) for r in rows]
    in_specs += [pl.BlockSpec((r.hb, r.w), lambda i, c=r.cb, q=tr // r.hb: (jnp.maximum(rev(i) * q - 1, 0), c))
                 for r in halos]
    in_specs += [_full_spec(p.shape) for p in params]
    args = [r.arr for r in rows] + [r.arr for r in halos] + list(params)
    scratch = []
    if nc:
        in_specs.append(pl.BlockSpec((1,) + carry, lambda i: (rev(i), 0, 0, 0)))
        args.append(csave)
        scratch.append(pltpu.VMEM(carry, F32))
    douts = [d if isinstance(d, Row) else Row(d, d.shape[1], 0) for d in douts]
    in_specs += [pl.BlockSpec((tr, d.w), lambda i, c=d.cb: (rev(i), c)) for d in douts]
    args += [d.arr for d in douts]
    aliases = {}
    if nalias:
        aliases = {len(args): ng}
        in_specs.append(pl.BlockSpec(memory_space=pl.ANY))
        args.append(dcols)
    scratch += [pltpu.VMEM((r.hb, r.w), F32) for r in halos]
    out_shape = [jax.ShapeDtypeStruct((T, r.w), F32) for r in grows]
    out_specs = [pl.BlockSpec((tr, r.w), lambda i: (rev(i), 0)) for r in grows]
    if ncol:
        off = crows[0].cb * crows[0].w
        assert off % wcols == 0 and all(a.cb * a.w + a.w == b.cb * b.w for a, b in zip(crows, crows[1:]))
        out_shape.append(jax.ShapeDtypeStruct((T, N_PAD), BF16))
        out_specs.append(pl.BlockSpec((tr, wcols), lambda i, c=off // wcols: (rev(i), c)))
    out_shape += [jax.ShapeDtypeStruct(p.shape, F32) for p in params]
    out_specs += [_full_spec(p.shape) for p in params]
    res = pl.pallas_call(
        body, name=name, grid=(n,), in_specs=in_specs, out_specs=out_specs, out_shape=out_shape,
        scratch_shapes=scratch, input_output_aliases=aliases,
        compiler_params=_params(dimension_semantics=("arbitrary",)),
    )(*args)
    return list(res[:ng]), list(res[ng + ncol:]), (res[ng] if ncol else dcols)


def _fill_misc(dcols, dkv, dba, tr):
    T = dkv.shape[0]

    def body(kv_ref, ba_ref, _, o_ref):
        o_ref[...] = jnp.concatenate([kv_ref[...], ba_ref[...]], axis=1).astype(BF16)

    return pl.pallas_call(
        body, name="misc_bwd", grid=(T // tr,),
        in_specs=[pl.BlockSpec((tr, 256), lambda i: (i, 0)), pl.BlockSpec((tr, 256), lambda i: (i, 0)),
                  pl.BlockSpec(memory_space=pl.ANY)],
        out_specs=pl.BlockSpec((tr, W_MISC), lambda i: (i, O_MISC // W_MISC)),
        out_shape=jax.ShapeDtypeStruct((T, N_PAD), BF16), input_output_aliases={2: 0},
        compiler_params=_params(dimension_semantics=("arbitrary",)),
    )(dkv, dba, dcols)


def _up_bwd(ys, cols, dm, w_up):
    T, tr = dm.shape[0], UPB_TR

    def body(y_ref, gl_ref, dm_ref, w_ref, dy_ref, dgl_ref, dw_ref):
        @pl.when(pl.program_id(1) == 0)
        def _():
            dw_ref[...] = jnp.zeros_like(dw_ref)

        _, vjp = jax.vjp(lambda y, gl, w: _sigmoid(gl) * _nn16(y, w),
                         y_ref[...].astype(F32), gl_ref[...].astype(F32), w_ref[...].astype(F32))
        dy, dgl, dw = vjp(dm_ref[...])
        dy_ref[...] = dy
        dgl_ref[...] = dgl.astype(BF16)
        dw_ref[...] += dw

    branch_rows = lambda w: pl.BlockSpec((tr, w), lambda n, i: (i, n))
    weight = pl.BlockSpec((None, BRANCH_W, D_MODEL), lambda n, i: (n, 0, 0))
    return pl.pallas_call(
        body, name="up_bwd", grid=(4, T // tr),
        in_specs=[branch_rows(BRANCH_W), branch_rows(D_MODEL), pl.BlockSpec((tr, D_MODEL), lambda n, i: (i, 0)), weight],
        out_specs=[branch_rows(BRANCH_W), branch_rows(D_MODEL), weight],
        out_shape=[jax.ShapeDtypeStruct((T, 4 * BRANCH_W), F32), jax.ShapeDtypeStruct((T, N_PAD), BF16),
                   jax.ShapeDtypeStruct(w_up.shape, F32)],
        compiler_params=_params(dimension_semantics=("arbitrary", "arbitrary")),
    )(ys, cols, dm, w_up)


def _matmul(name, a, b, kind, out_dtype, tm, tn, tk, after=None):
    if kind == "tn":
        (K, M), N = a.shape, b.shape[1]
    else:
        (M, K), N = a.shape, (b.shape[0] if kind == "nt" else b.shape[1])
    tm, tn, tk = min(tm, M), min(tn, N), min(tk, K)
    nk = K // tk
    dims = {"nn": _NN, "nt": _NT, "tn": _TN}[kind]

    n_after = 0 if after is None else 1

    def body(*refs):
        a_ref, b_ref, o_ref, acc = refs[0], refs[1], refs[2 + n_after], refs[3 + n_after:]
        part = lax.dot_general(a_ref[...], b_ref[...], (dims, ((), ())), preferred_element_type=F32)
        if nk == 1:
            o_ref[...] = part.astype(o_ref.dtype)
            return
        acc_ref = acc[0] if acc else o_ref
        k = pl.program_id(2)

        @pl.when(k == 0)
        def _():
            acc_ref[...] = part

        @pl.when(k > 0)
        def _():
            acc_ref[...] += part

        if acc:
            @pl.when(k == nk - 1)
            def _():
                o_ref[...] = acc_ref[...].astype(o_ref.dtype)

    a_spec = pl.BlockSpec((tk, tm), lambda i, j, k: (k, i)) if kind == "tn" else pl.BlockSpec((tm, tk), lambda i, j, k: (i, k))
    b_spec = pl.BlockSpec((tn, tk), lambda i, j, k: (j, k)) if kind == "nt" else pl.BlockSpec((tk, tn), lambda i, j, k: (k, j))
    return pl.pallas_call(
        body, name=name, grid=(M // tm, N // tn, nk), in_specs=[a_spec, b_spec] + [_ANY] * n_after,
        out_specs=pl.BlockSpec((tm, tn), lambda i, j, k: (i, j)),
        out_shape=jax.ShapeDtypeStruct((M, N), out_dtype),
        scratch_shapes=[pltpu.VMEM((tm, tn), F32)] if nk > 1 and out_dtype != F32 else [],
        compiler_params=_params(dimension_semantics=("arbitrary", "arbitrary", "arbitrary")),
    )(a, b, *([] if after is None else [after]))


def _pre_fn(first, _, rows, halos, params):
    return None, [_rms(rows[0], params[0])]


def _pre_fn_res(first, _, rows, halos, params):
    return None, [_rms(rows[0], params[0]), rows[0]]


def _memkv_fn(first, _, rows, halos, params):
    g, w = params
    return None, [_nn16(_rms(rows[0], g), w)]


def _conv_silu(x, halo, w4, keep_halo):
    tr = x.shape[0]
    halo = halo * keep_halo
    rid = _iota((tr, 1), 0)
    acc = w4[3] * x
    for s in (1, 2, 3):
        hs = jnp.concatenate([_rollr(halo, s), jnp.zeros((tr - halo.shape[0], x.shape[1]), F32)], axis=0)
        acc = acc + w4[3 - s] * jnp.where(rid < s, hs, _rollr(x, s))
    return _silu(acc)


def _dn_fn(first, S, rows, halos, params):
    qp, kp, vp, z, ba = rows
    conv, a_vec, dt_vec, dnorm = params
    ba = _cols(ba, 0, LANES)
    tr = qp.shape[0]
    keep = jnp.where(first, 0.0, 1.0)
    q = _conv_silu(qp, halos[0], [conv[3 * j + 0] for j in range(4)], keep)
    k = _conv_silu(kp, halos[1], [conv[3 * j + 1] for j in range(4)], keep)
    v = _conv_silu(vp, halos[2], [conv[3 * j + 2] for j in range(4)], keep)
    qh, kh, vh = [], [], []
    for h in range(4):
        a, b = h * LANES, (h + 1) * LANES
        xq, xk = _cols(q, a, b), _cols(k, a, b)
        qh.append(xq * lax.rsqrt(jnp.sum(xq * xq, axis=1, keepdims=True) + EPS) * (LANES ** -0.5))
        kh.append(xk * lax.rsqrt(jnp.sum(xk * xk, axis=1, keepdims=True) + EPS))
        vh.append(_cols(v, a, b))
    beta_all = _sigmoid(ba)
    g_all = -jnp.exp(a_vec) * _softplus(ba + dt_vec)
    C = DN_CHUNK
    ii, jj = _iota((C, C), 0), _iota((C, C), 1)
    strict, incl = ii > jj, ii >= jj
    eye = (ii == jj).astype(F32)
    last_row = (_iota((C, 1), 0) == C - 1).astype(F32)
    n_chunk = tr // C
    pairs = [(c, h) for c in range(n_chunk) for h in range(4)]
    rows_of = lambda a, c: _rowsl(a, c * C, (c + 1) * C)
    gcs = [_nn32(incl.astype(F32), rows_of(g_all, c)) for c in range(n_chunk)]
    qc = {(c, h): rows_of(qh[h], c) for c, h in pairs}
    kc = {(c, h): rows_of(kh[h], c) for c, h in pairs}
    beta = {(c, h): _lane_pick(rows_of(beta_all, c), h) for c, h in pairs}
    gc = {(c, h): _lane_pick(gcs[c], 4 + h) for c, h in pairs}
    dec = {p: jnp.exp(jnp.where(incl, gc[p] - jnp.sum(eye * gc[p], axis=0, keepdims=True), 0.0)) for p in pairs}
    egc = {p: jnp.exp(gc[p]) for p in pairs}
    kb = {p: kc[p] * beta[p] for p in pairs}
    kq = {p: _nt16(jnp.concatenate([kb[p], qc[p]], axis=0), kc[p]) for p in pairs}
    P = {p: -jnp.where(strict, _rowsl(kq[p], 0, C) * dec[p], 0.0) for p in pairs}
    aqk = {p: jnp.where(incl, _rowsl(kq[p], C, 2 * C) * dec[p], 0.0) for p in pairs}
    tinv = {p: eye + P[p] for p in pairs}
    P = {p: _nn16(P[p], P[p]) for p in pairs}
    for j in range(5):
        if j < 4:
            pt = {p: _nn16(jnp.concatenate([P[p], tinv[p]], axis=0), P[p]) for p in pairs}
            tinv = {p: tinv[p] + _rowsl(pt[p], C, 2 * C) for p in pairs}
            P = {p: _rowsl(pt[p], 0, C) for p in pairs}
        else:
            tinv = {p: tinv[p] + _nn16(tinv[p], P[p]) for p in pairs}
    uw = {(c, h): _nn16(tinv[c, h], jnp.concatenate([rows_of(vh[h], c) * beta[c, h], kb[c, h] * egc[c, h]], axis=1))
          for c, h in pairs}
    S = list(S)
    ychunks = []
    for c in range(n_chunk):
        zc = rows_of(z, c)
        hs = range(4)
        ws = [_nn16(jnp.concatenate([_cols(uw[c, h], LANES, 2 * LANES), qc[c, h] * egc[c, h]], axis=0), S[h]) for h in hs]
        vnew = [_cols(uw[c, h], 0, LANES) - _rowsl(ws[h], 0, C) for h in hs]
        o = [_rowsl(ws[h], C, 2 * C) + _nn16(aqk[c, h], vnew[h]) for h in hs]
        glast = [jnp.sum(gc[c, h] * last_row, axis=0, keepdims=True) for h in hs]
        S = [S[h] * jnp.exp(glast[h]) + _tn16(kc[c, h] * jnp.exp(glast[h] - gc[c, h]), vnew[h]) for h in hs]
        ychunks.append(jnp.concatenate(
            [_rms(o[h], dnorm) * _silu(_cols(zc, h * LANES, (h + 1) * LANES)) for h in hs], axis=1))
    return S, [jnp.concatenate(ychunks, axis=0)]


def _gm_fn(first, _, rows, halos, params):
    uv, z = rows
    gnorm, ws, bs = params
    tr = uv.shape[0]
    guv = _gelu(uv)
    u = _cols(guv, 0, BRANCH_W)
    v = _rms(_cols(guv, BRANCH_W, 2 * BRANCH_W), gnorm)
    ii, jj = _iota((LANES, LANES), 0), _iota((LANES, LANES), 1)
    eye = (ii == jj).astype(F32)
    wsm = [jnp.where(ii >= jj, ws[g], 0.0) for g in range(4)]
    bcol = [jnp.sum(eye * bs[g], axis=1, keepdims=True) for g in range(4)]
    chunks = []
    for c in range(tr // LANES):
        vc = _rowsl(v, c * LANES, (c + 1) * LANES)
        chunks.append(jnp.concatenate(
            [_nn16(wsm[g], _cols(vc, g * LANES, (g + 1) * LANES)) + bcol[g] for g in range(4)], axis=1))
    return None, [u * jnp.concatenate(chunks, axis=0) * _silu(z)]


def _swa_fn(first, _, rows, halos, params):
    q, kvc, z = rows
    sink_vec = params[0]
    P = LANES
    n_blk = q.shape[0] // P
    r, cc = _iota((P, P), 0), _iota((P, P), 1)
    lane = _iota((1, P), 1)
    key = _iota((P, 2 * P), 1)
    dist = _iota((P, 2 * P), 0) + P - key
    in_window = (dist >= 0) & (dist < P)
    valid = [in_window & (key >= jnp.where(first, P, 0))] + [in_window] * (n_blk - 1)
    halves = [(lane < 64).astype(F32), (lane >= 64).astype(F32)]
    dup = [(r == kh * 64 + (cc & 63)).astype(F32) for kh in range(2)]
    kv_blk = [halos[0]] + [_rowsl(kvc, b * P, (b + 1) * P) for b in range(n_blk)]
    pairs = [(b, kh) for b in range(n_blk) for kh in range(2)]
    kkvv = {}
    for b in range(n_blk):
        kv = jnp.concatenate([kv_blk[b], kv_blk[b + 1]], axis=0)
        k_v = jnp.concatenate([_cols(kv, 0, P), _cols(kv, P, 2 * P)], axis=0)
        for kh in range(2):
            kkvv[b, kh] = _nn16(k_v, dup[kh])
    scores = {}
    for b, kh in pairs:
        q_b = _rowsl(q, b * P, (b + 1) * P)
        stacked = jnp.concatenate([_cols(q_b, (2 * kh + g // 2) * P, (2 * kh + g // 2 + 1) * P) * halves[g % 2]
                                   for g in range(4)], axis=0)
        scores[b, kh] = _nt16(stacked, _rowsl(kkvv[b, kh], 0, 2 * P))
    probs = {}
    for b, kh in pairs:
        ps = []
        for g in range(4):
            s = jnp.where(valid[b], _rowsl(scores[b, kh], g * P, (g + 1) * P) * 0.125, NEG_INF)
            sink = _lane_pick(sink_vec, kh * 4 + g)
            m = lax.stop_gradient(jnp.maximum(jnp.max(s, axis=1, keepdims=True), sink))
            e = jnp.exp(s - m)
            ps.append(e / (jnp.sum(e, axis=1, keepdims=True) + jnp.exp(sink - m)))
        probs[b, kh] = jnp.concatenate(ps, axis=0)
    outs = {p: _nn16(probs[p], _rowsl(kkvv[p], 2 * P, 4 * P)) for p in pairs}
    tile = [jnp.concatenate([_rowsl(outs[b, j // 2], (2 * (j % 2)) * P, (2 * (j % 2) + 1) * P) * halves[0]
                             + _rowsl(outs[b, j // 2], (2 * (j % 2) + 1) * P, (2 * (j % 2) + 2) * P) * halves[1]
                             for j in range(4)], axis=1) for b in range(n_blk)]
    return None, [jnp.concatenate(tile, axis=0) * _silu(z)]


def _mem_fn(first, _, rows, halos, params):
    q, z = rows
    mkv = params[0]
    heads = [(h * LANES, (h + 1) * LANES) for h in range(4)]
    scores = [_nt16(_cols(q, a, b), _cols(mkv, a, b)) * (LANES ** -0.5) for a, b in heads]
    probs = []
    for s in scores:
        e = jnp.exp(s - lax.stop_gradient(jnp.max(s, axis=1, keepdims=True)))
        probs.append(e / jnp.sum(e, axis=1, keepdims=True))
    outs = [_nn16(p, _cols(mkv, BRANCH_W + a, BRANCH_W + b)) for p, (a, b) in zip(probs, heads)]
    return None, [jnp.concatenate(outs, axis=1) * _silu(z)]


def _up_fn(first, _, rows, halos, params):
    ys, gl, w_up = rows[:4], rows[4], params[0]
    merged = None
    for n in range(4):
        term = _sigmoid(_cols(gl, n * D_MODEL, (n + 1) * D_MODEL)) * _nn16(ys[n], w_up[n])
        merged = term if merged is None else merged + term
    return None, [merged]


def _out_fn(first, _, rows, halos, params):
    x, merged = rows
    w, g = params
    return None, [x + _rms(_nn16(merged, w), g)]


def _loss_fn(first, _, rows, halos, params):
    y, t = rows
    d = y - t
    lrow = 0.5 * jnp.mean(d * d, axis=1, keepdims=True)
    return None, [d * (1.0 / D_MODEL), jnp.broadcast_to(lrow, (y.shape[0], LANES))]


TR = 256
BIG_TR = 512
SWA_TR = 256
DN_TR = 256
UP_TR = 256
UPB_TR = 512
CONV_HALO = 16
CARRY = (4, LANES, LANES)


def _branch_rows(cols, g):
    hb = CONV_HALO
    a = [Row(cols, 512, O_AQ // 512, hb, g), Row(cols, 512, O_AK // 512, hb, g), Row(cols, 512, O_AV // 512, hb, g),
         Row(cols, 512, O_AZ // 512, 0, g), Row(cols, 256, O_BA // 256)]
    b = [Row(cols, 1024, O_BUV // 1024, 0, g), Row(cols, 512, O_BZ // 512, 0, g)]
    c = [Row(cols, 512, O_CQ // 512, 0, g), Row(cols, 256, O_CKV // 256, LANES), Row(cols, 512, O_CZ // 512, 0, g)]
    m = [Row(cols, 512, O_MQ // 512, 0, g), Row(cols, 512, O_MZ // 512, 0, g)]
    return a, b, c, m


def _layer_fwd(x, mem, W, late_weights=None):
    h = _rows_fwd("prenorm_fwd", _pre_fn, [Row(x, D_MODEL, 0)], [W["norm_pre"]], [(D_MODEL, BF16)], BIG_TR)[0]
    cols = _matmul("in_proj_fwd", h, W["w_pad"], "nt", BF16, 2048, 1024, 1024)
    if late_weights is not None:
        W = dict(W, **late_weights(cols))
    mem_kv = _rows_fwd("memkv_fwd", _memkv_fn, [Row(mem, D_MODEL, 0)], [W["norm_mem"], W["w_mem_kv"]],
                       [(D_MODEL, F32)], MEM_LEN)[0]
    ra, rb, rc, rm = _branch_rows(cols, True)
    y_a, csave = _rows_fwd("dn_fwd", _dn_fn, ra, [W["conv"], W["a_vec"], W["dt_vec"], W["dn_norm"]],
                           [(BRANCH_W, BF16)], DN_TR, CARRY)
    y_b = _rows_fwd("gm_fwd", _gm_fn, rb, [W["gm_norm"], W["spatial_w"], W["spatial_b"]], [(BRANCH_W, BF16)], BIG_TR)[0]
    y_c = _rows_fwd("swa_fwd", _swa_fn, rc, [W["sink_vec"]], [(BRANCH_W, BF16)], SWA_TR)[0]
    y_m = _rows_fwd("mem_fwd", _mem_fn, rm, [mem_kv], [(BRANCH_W, BF16)], BIG_TR)[0]
    ys = [y_a, y_b, y_c, y_m]
    merged = _rows_fwd("up_fwd", _up_fn, [Row(y, BRANCH_W, 0) for y in ys] + [Row(cols, 4 * D_MODEL, 0)],
                       [W["w_up"]], [(D_MODEL, BF16)], UP_TR)[0]
    x_new = _rows_fwd("out_fwd", _out_fn, [Row(x, D_MODEL, 0), Row(merged, D_MODEL, 0)],
                      [W["w_out"], W["norm_post"]], [(D_MODEL, F32)], TR)[0]
    return x_new, dict(x=x, h=h, cols=cols, mem_kv=mem_kv, csave=csave, ys=ys, merged=merged), W


def _layer_bwd(dxn, mem, W, sv, on_weight_grads=None):
    x, cols = sv["x"], sv["cols"]
    (dx_res, dm), (dw_out, dnorm_post), _ = _rows_bwd(
        "out_bwd", _out_fn, [Row(x, D_MODEL, 0), Row(sv["merged"], D_MODEL, 0)], [W["w_out"], W["norm_post"]],
        [dxn], TR)
    dys, dcols, dw_up = _up_bwd(jnp.concatenate(sv["ys"], axis=1), cols, dm, W["w_up"])
    dys = [Row(dys, BRANCH_W, n) for n in range(4)]
    ra, rb, rc, rm = _branch_rows(cols, "cols")
    (dba,), (dconv, da_vec, ddt_vec, ddn_norm), dcols = _rows_bwd(
        "dn_bwd", _dn_fn, ra, [W["conv"], W["a_vec"], W["dt_vec"], W["dn_norm"]], [dys[0]], DN_TR, CARRY,
        sv["csave"], dcols=dcols)
    _, (dgm_norm, dws, dbs), dcols = _rows_bwd(
        "gm_bwd", _gm_fn, rb, [W["gm_norm"], W["spatial_w"], W["spatial_b"]], [dys[1]], BIG_TR, dcols=dcols)
    (dkv_c,), (dsink,), dcols = _rows_bwd("swa_bwd", _swa_fn, rc, [W["sink_vec"]], [dys[2]], SWA_TR, dcols=dcols)
    _, (dmem_kv,), dcols = _rows_bwd("mem_bwd", _mem_fn, rm, [sv["mem_kv"]], [dys[3]], BIG_TR, dcols=dcols)
    dcols = _fill_misc(dcols, dkv_c, dba, BIG_TR)
    _, (dnorm_mem, dw_mem_kv), _ = _rows_bwd("memkv_bwd", _memkv_fn, [Row(mem, D_MODEL, 0, 0, False)],
                                             [W["norm_mem"], W["w_mem_kv"]], [dmem_kv], MEM_LEN)
    dw_pad = _matmul("in_proj_dw", dcols, sv["h"], "tn", BF16, 1024, 1024, 2048)
    grads = dict(norm_post=dnorm_post, norm_mem=dnorm_mem, w_pad=dw_pad, conv=dconv,
                 a_vec=da_vec, dt_vec=ddt_vec, dn_norm=ddn_norm, gm_norm=dgm_norm, spatial_w=dws, spatial_b=dbs,
                 sink_vec=dsink, w_mem_kv=dw_mem_kv, w_up=dw_up, w_out=dw_out)
    started = None if on_weight_grads is None else on_weight_grads(grads)
    dh = _matmul("in_proj_dx", dcols, W["w_pad"], "nn", F32, 1024, 1024, 1024, after=started)
    (dx,), (grads["norm_pre"],), _ = _rows_bwd("prenorm_bwd", _pre_fn_res, [Row(x, D_MODEL, 0)], [W["norm_pre"]],
                                               [dh, dx_res], BIG_TR)
    return dx, grads


def _lane_vec(v, off):
    return jnp.zeros((1, LANES), F32).at[0, off:off + v.shape[0]].set(v)


def _layer_weights(l, w_pad, conv_w, small, **late):
    return dict(
        late, w_pad=w_pad, conv=conv_w.reshape(4, 3, BRANCH_W).reshape(12, 1, BRANCH_W),
        norm_pre=small["norm_pre"][l][None], norm_post=small["norm_post"][l][None],
        norm_mem=small["norm_mem"][l][None],
        a_vec=_lane_vec(small["a_log"][l], 4), dt_vec=_lane_vec(small["dt_bias"][l], 4),
        dn_norm=small["dn_norm"][l][None], gm_norm=small["gm_norm"][l][None],
        spatial_w=small["spatial_w"][l], spatial_b=small["spatial_b"][l][:, None, :],
        sink_vec=_lane_vec(small["sinks"][l], 0))


_MESH = pl.DeviceIdType.MESH
_ANY = pl.BlockSpec(memory_space=pl.ANY)


def _position():
    return lax.axis_index("x"), lax.axis_index("y"), lax.axis_index("c")


def _remote(src, dst, send_sem, recv_sem, dev):
    return pltpu.make_async_remote_copy(src_ref=src, dst_ref=dst, send_sem=send_sem, recv_sem=recv_sem,
                                        device_id=dev, device_id_type=_MESH)


def _hbm_call(name, body, arrs, out_shapes, sems, aliases=None):
    return pl.pallas_call(
        body, name=name, in_specs=[_ANY] * len(arrs), out_specs=[_ANY] * len(out_shapes), out_shape=out_shapes,
        scratch_shapes=[pltpu.SemaphoreType.DMA((k,)) for k in sems], input_output_aliases=aliases or {},
        compiler_params=pltpu.CompilerParams(has_side_effects=True),
    )(*arrs)


def _other_chips(x, y):
    return [(1 - x, y), (x, 1 - y), (1 - x, 1 - y)]


def _gather_weights(arrs, relayed):
    n = len(arrs)

    def body(*refs):
        ins, outs = refs[:n], refs[n:2 * n]
        ici_send, ici_recv, d2d_send, d2d_recv = refs[2 * n:]
        x, y, c = _position()
        me = 2 * x + y
        xn, yn, dg = _other_chips(x, y)
        chip = lambda p: 2 * p[0] + p[1]
        sends = []

        def go(cp):
            cp.start()
            sends.append(cp)

        def ici(a, j, src, dst, to):
            return _remote(src, dst, ici_send.at[4 * a + j], ici_recv.at[4 * a + j], (*to, c))

        for a in range(n):
            go(ici(a, 0, ins[a].at[c], outs[a].at[c, me], xn))
            go(ici(a, 1, ins[a].at[c], outs[a].at[c, me], yn))
            if not relayed[a]:
                go(ici(a, 2, ins[a].at[c], outs[a].at[c, me], dg))
        for a in range(n):
            h = arrs[a].shape[1] // 2
            from_x, from_y = outs[a].at[c, chip(xn)], outs[a].at[c, chip(yn)]
            ici(a, 0, ins[a].at[c], from_x, xn).wait_recv()
            if relayed[a]:
                go(ici(a, 2, from_x.at[pl.ds(0, h)], from_x.at[pl.ds(0, h)], yn))
            ici(a, 1, ins[a].at[c], from_y, yn).wait_recv()
            if relayed[a]:
                go(ici(a, 3, from_y.at[pl.ds(h, h)], from_y.at[pl.ds(h, h)], xn))
            for j, slab in enumerate((from_x, from_y)):
                go(_remote(slab, slab, d2d_send.at[3 * a + j], d2d_recv.at[3 * a + j], (x, y, 1 - c)))
        for a in range(n):
            h = arrs[a].shape[1] // 2
            from_d = outs[a].at[c, chip(dg)]
            if relayed[a]:
                ici(a, 2, from_d.at[pl.ds(0, h)], from_d.at[pl.ds(0, h)], yn).wait_recv()
                ici(a, 3, from_d.at[pl.ds(h, h)], from_d.at[pl.ds(h, h)], xn).wait_recv()
            else:
                ici(a, 2, ins[a].at[c], from_d, dg).wait_recv()
            go(_remote(from_d, from_d, d2d_send.at[3 * a + 2], d2d_recv.at[3 * a + 2], (x, y, 1 - c)))
        for a in range(n):
            for j, p in enumerate((xn, yn, dg)):
                slab = outs[a].at[1 - c, chip(p)]
                _remote(slab, slab, d2d_send.at[3 * a + j], d2d_recv.at[3 * a + j], (x, y, 1 - c)).wait_recv()
        for cp in sends:
            cp.wait_send()

    return _hbm_call("gather_weights", body, arrs,
                     [jax.ShapeDtypeStruct((N_LAYER, N_CHIP) + a.shape[1:], a.dtype) for a in arrs],
                     [4 * n, 4 * n, 3 * n, 3 * n])


def _pair_exchange(arrs):
    n = len(arrs)

    def body(*refs):
        ins, outs = refs[:n], refs[n:2 * n]
        send_sems, recv_sems = refs[2 * n:]
        x, y, c = _position()
        cps = [_remote(ins[a].at[1 - c], outs[a], send_sems.at[a], recv_sems.at[a], (x, y, 1 - c)) for a in range(n)]
        for cp in cps:
            cp.start()
        for cp in cps:
            cp.wait_recv()
        for cp in cps:
            cp.wait_send()

    return _hbm_call("pair_exchange", body, arrs, [jax.ShapeDtypeStruct(a.shape[1:], a.dtype) for a in arrs], [n, n])


def _chip_scatter(arrs):
    n = len(arrs)

    def body(*refs):
        ins, outs = refs[:n], refs[n:2 * n]
        send_sems, recv_sems = refs[2 * n:]
        x, y, c = _position()
        me = 2 * x + y
        sends = []
        for a in range(n):
            for j, (px, py) in enumerate(_other_chips(x, y)):
                sends.append(_remote(ins[a].at[2 * px + py], outs[a].at[me], send_sems.at[3 * a + j],
                                     recv_sems.at[3 * a + j], (px, py, c)))
                sends[-1].start()
        for a in range(n):
            for j, (px, py) in enumerate(_other_chips(x, y)):
                _remote(ins[a].at[me], outs[a].at[2 * px + py], send_sems.at[3 * a + j], recv_sems.at[3 * a + j],
                        (px, py, c)).wait_recv()
        for cp in sends:
            cp.wait_send()

    return _hbm_call("chip_scatter", body, arrs, [jax.ShapeDtypeStruct(a.shape, a.dtype) for a in arrs],
                     [3 * n, 3 * n])


def _pair_share(arrs):
    n = len(arrs)

    def body(*refs):
        ins, outs = refs[:n], refs[n:2 * n]
        send_sems, recv_sems = refs[2 * n:]
        x, y, c = _position()
        cps = [_remote(ins[a].at[c], outs[a].at[c], send_sems.at[a], recv_sems.at[a], (x, y, 1 - c)) for a in range(n)]
        for cp in cps:
            cp.start()
        for a in range(n):
            _remote(ins[a].at[c], outs[a].at[1 - c], send_sems.at[a], recv_sems.at[a], (x, y, 1 - c)).wait_recv()
        for cp in cps:
            cp.wait_send()

    return _hbm_call("pair_share", body, arrs, [jax.ShapeDtypeStruct(a.shape, a.dtype) for a in arrs], [n, n],
                     {a: a for a in range(n)})


def _pair_forward(arrs):
    n = len(arrs)

    def body(*refs):
        ins, outs = refs[:n], refs[n:2 * n]
        send_sems, recv_sems = refs[2 * n:]
        x, y, c = _position()
        sends = []
        for a in range(n):
            for j, (px, py) in enumerate(_other_chips(x, y)):
                sends.append(_remote(ins[a].at[c, 2 * px + py], outs[a].at[c, 2 * px + py], send_sems.at[3 * a + j],
                                     recv_sems.at[3 * a + j], (x, y, 1 - c)))
                sends[-1].start()
        for a in range(n):
            for j, (px, py) in enumerate(_other_chips(x, y)):
                slab = outs[a].at[1 - c, 2 * px + py]
                _remote(slab, slab, send_sems.at[3 * a + j], recv_sems.at[3 * a + j], (x, y, 1 - c)).wait_recv()
        for cp in sends:
            cp.wait_send()

    return _hbm_call("pair_forward", body, arrs, [jax.ShapeDtypeStruct(a.shape, a.dtype) for a in arrs],
                     [3 * n, 3 * n], {a: a for a in range(n)})


_HBM = pl.BlockSpec(memory_space=pltpu.HBM)
_SEM = pl.BlockSpec(memory_space=pltpu.SEMAPHORE)
_EFFECT = pltpu.SideEffectType.DATAFLOW_SIDE_EFFECTING


def _chip_copies(kind, srcs, lands, send_sems, recv_sems):
    x, y, c = _position()
    me = 2 * x + y
    sends, recvs = [], []
    for a in range(len(srcs)):
        for j, (px, py) in enumerate(_other_chips(x, y)):
            s, sems, dev = 2 * px + py, (send_sems.at[3 * a + j], recv_sems.at[3 * a + j]), (px, py, c)
            if kind == "gather":
                sends.append(_remote(srcs[a].at[c], lands[a].at[c, me], *sems, dev))
                recvs.append(_remote(srcs[a].at[c], lands[a].at[c, s], *sems, dev))
            else:
                sends.append(_remote(srcs[a].at[s], lands[a].at[me], *sems, dev))
                recvs.append(_remote(srcs[a].at[me], lands[a].at[s], *sems, dev))
    return sends, recvs


def _split_start(name, kind, srcs, land_shapes, after):
    n = len(srcs)

    def body(*refs):
        sends, _ = _chip_copies(kind, refs[:n], refs[n:2 * n], refs[2 * n + 1], refs[2 * n + 2])
        for cp in sends:
            cp.start()
        refs[-1][...] = jnp.zeros_like(refs[-1])

    hbm = lambda a: pltpu.with_memory_space_constraint(a, pltpu.HBM)
    lands = [lax.empty(s.shape, s.dtype) for s in land_shapes]
    outs = pl.pallas_call(
        body, name=name, in_specs=[_HBM] * (2 * n) + [_ANY],
        out_specs=[_SEM, _SEM] + [_HBM] * (2 * n) + [pl.BlockSpec(memory_space=pltpu.VMEM)],
        out_shape=[pltpu.SemaphoreType.DMA((3 * n,)), pltpu.SemaphoreType.DMA((3 * n,))]
        + [pltpu.HBM(a.shape, a.dtype) for a in list(srcs) + lands] + [jax.ShapeDtypeStruct((8, LANES), F32)],
        input_output_aliases={i: 2 + i for i in range(2 * n)},
        compiler_params=pltpu.CompilerParams(has_side_effects=_EFFECT),
    )(*[hbm(a) for a in srcs], *[hbm(a) for a in lands], after)
    return outs[0], outs[1], list(outs[2:2 + 2 * n]), outs[-1]


def _split_wait(name, kind, started, after):
    send_sems, recv_sems, thru, _ = started
    n = len(thru) // 2

    def body(*refs):
        sends, recvs = _chip_copies(kind, refs[:n], refs[n:2 * n], refs[2 * n], refs[2 * n + 1])
        for cp in sends:
            cp.wait_send()
        for cp in recvs:
            cp.wait_recv()

    outs = pl.pallas_call(
        body, name=name, in_specs=[_HBM] * (2 * n) + [_SEM, _SEM, _ANY], out_specs=[_HBM] * (2 * n),
        out_shape=[pltpu.HBM(a.shape, a.dtype) for a in thru], input_output_aliases={i: i for i in range(2 * n)},
        compiler_params=pltpu.CompilerParams(has_side_effects=_EFFECT),
    )(*thru, send_sems, recv_sems, after)
    return list(outs[:n]), list(outs[n:])


def _allreduce_small(g):
    def body(g_ref, o_ref, pair_buf, chip_buf, send_sems, recv_sems):
        x, y, c = _position()
        me = 2 * x + y
        sib = (x, y, 1 - c)
        to_sib = _remote(g_ref.at[1 - c], pair_buf, send_sems.at[0], recv_sems.at[0], sib)
        to_sib.start()
        to_sib.wait_recv()
        chip_buf[me] = g_ref[c] + pair_buf[...]
        sends = [to_sib]
        chips = _other_chips(x, y)
        for j, (px, py) in enumerate(chips):
            sends.append(_remote(chip_buf.at[me], chip_buf.at[me], send_sems.at[1 + j], recv_sems.at[1 + j], (px, py, c)))
            sends[-1].start()
        for j, (px, py) in enumerate(chips):
            _remote(chip_buf.at[me], chip_buf.at[2 * px + py], send_sems.at[1 + j], recv_sems.at[1 + j],
                    (px, py, c)).wait_recv()
        o_ref[c] = ((chip_buf[0] + chip_buf[1]) + chip_buf[2]) + chip_buf[3]
        sends.append(_remote(o_ref.at[c], o_ref.at[c], send_sems.at[4], recv_sems.at[4], sib))
        sends[-1].start()
        _remote(o_ref.at[c], o_ref.at[1 - c], send_sems.at[4], recv_sems.at[4], sib).wait_recv()
        for cp in sends:
            cp.wait_send()

    vmem = pl.BlockSpec(memory_space=pltpu.VMEM)
    return pl.pallas_call(
        body, name="allreduce_small", in_specs=[vmem], out_specs=vmem, out_shape=jax.ShapeDtypeStruct(g.shape, F32),
        scratch_shapes=[pltpu.VMEM(g.shape[1:], F32), pltpu.VMEM((N_CHIP,) + g.shape[1:], F32),
                        pltpu.SemaphoreType.DMA((5,)), pltpu.SemaphoreType.DMA((5,))],
        compiler_params=_params(),
    )(g)


EW_ROWS = 512


def _ew(name, fn, ins, n_out, out_dtype=F32, out_slot=None, into=None):
    def dims(a):
        return a[0].shape[1:] if isinstance(a, tuple) else a.shape

    R, w = dims(ins[0])
    tr = EW_ROWS if R % EW_ROWS == 0 else R
    n_into = len(into) if into else 0

    def body(c_ref, *refs):
        outs = fn(*[r[...] for r in refs[:len(ins)]])
        for r, v in zip(refs[len(ins) + n_into:], outs):
            r[...] = v.astype(r.dtype)

    def lead_spec(l):
        if l == "c":
            return pl.BlockSpec((None, tr, w), lambda i, c_ref: (c_ref[0], i, 0))
        return pl.BlockSpec((None, tr, w), lambda i, c_ref, s=l: (s, i, 0))

    plain = pl.BlockSpec((tr, w), lambda i, c_ref: (i, 0))
    in_specs = [lead_spec(a[1]) if isinstance(a, tuple) else plain for a in ins] + [_ANY] * n_into
    out_spec = plain if out_slot is None else lead_spec(out_slot)
    out_shape = jax.ShapeDtypeStruct((R, w) if out_slot is None else (2, R, w), out_dtype)
    return pl.pallas_call(
        body, name=name,
        grid_spec=pltpu.PrefetchScalarGridSpec(num_scalar_prefetch=1, grid=(R // tr,), in_specs=in_specs,
                                               out_specs=[out_spec] * n_out),
        out_shape=[out_shape] * n_out, input_output_aliases={1 + len(ins) + j: j for j in range(n_into)},
        compiler_params=_params(dimension_semantics=("arbitrary",)),
    )(lax.axis_index("c").astype(jnp.int32).reshape(1), *[a[0] if isinstance(a, tuple) else a for a in ins],
      *(into or []))


def _adamw_fn(w, g, m, v):
    m = ADAM_B1 * m + (1.0 - ADAM_B1) * g
    v = ADAM_B2 * v + (1.0 - ADAM_B2) * (g * g)
    m_hat = m / (1.0 - ADAM_B1 ** ADAM_STEP)
    v_hat = v / (1.0 - ADAM_B2 ** ADAM_STEP)
    delta = -ADAM_LR * (m_hat / (jnp.sqrt(v_hat) + ADAM_EPS) + ADAM_WD * w)
    return delta, m, v


def _adamw(name, w, g, m, v):
    shape = w.shape
    two = lambda a: a.reshape(-1, shape[-1])
    return [o.reshape(shape) for o in _ew(name, _adamw_fn, [two(w), two(g), two(m), two(v)], 3)]


def _adamw_layer(name, l, w, g, m, v, into):
    k = w.shape[-1]
    three = lambda a: (a.reshape(N_LAYER, -1, k), l)
    fn = lambda w_, g_, m_, v_: _adamw_fn(w_, g_, m_, v_) + (g_,)
    outs = _ew(name, fn, [three(w), g.reshape(-1, k), three(m), three(v)], 4, out_slot=l,
               into=None if into is None else [a.reshape(N_LAYER, -1, k) for a in into])
    return [o.reshape(w.shape) for o in outs]


def _adamw_rows(name, l, w, g, m, v, into):
    _, R, k = w.shape
    n_into = len(into) if into else 0

    def body(*refs):
        w_ref, g_ref, m_ref, v_ref = refs[:4]
        d_out, m_out, v_out, g_out = refs[4 + n_into:]
        g_blk = g_ref[...]
        d_out[...], m_out[...], v_out[...] = _adamw_fn(w_ref[...], g_blk, m_ref[...], v_ref[...])
        g_out[...] = g_blk

    spec = pl.BlockSpec((None, EW_ROWS, k), lambda i: (l, i, 0))
    return pl.pallas_call(
        body, name=name, grid=(-(-R // EW_ROWS),),
        in_specs=[spec, pl.BlockSpec((EW_ROWS, k), lambda i: (i, 0)), spec, spec] + [_ANY] * n_into,
        out_specs=[spec] * 4, out_shape=[jax.ShapeDtypeStruct((N_LAYER, R, k), F32)] * 4,
        input_output_aliases={4 + j: j for j in range(n_into)},
        compiler_params=_params(dimension_semantics=("arbitrary",)),
    )(w, g, m, v, *(into or []))


_SMALL = [("norm_pre", (2, 1024)), ("norm_post", (2, 1024)), ("norm_mem", (2, 1024)), ("a_log", (2, 4)),
          ("dt_bias", (2, 4)), ("dn_norm", (2, 128)), ("gm_norm", (2, 512)), ("spatial_w", (2, 4, 128, 128)),
          ("spatial_b", (2, 4, 128)), ("sinks", (2, 8))]
_SMALL_ROWS = 200
_BIG = ["w_in", "conv_w", "w_mem_kv", "w_up", "w_out"]
_NAMES = ["norm_pre", "norm_post", "norm_mem", "w_in", "conv_w", "a_log", "dt_bias", "dn_norm", "gm_norm",
          "spatial_w", "spatial_b", "sinks", "w_mem_kv", "w_up", "w_out"]


def _size(shape):
    n = 1
    for s in shape:
        n *= s
    return n


_PACK_UNIT = 8 * 1024


def _pack_small(d):
    rows = []
    for n, shp in _SMALL:
        flat = d[n].reshape(-1)
        rows.append(jnp.pad(flat, (0, -flat.shape[0] % _PACK_UNIT)).reshape(-1, 1024))
    assert sum(r.shape[0] for r in rows) == _SMALL_ROWS
    return jnp.concatenate(rows, axis=0)


def _unpack_small(p):
    out, off = {}, 0
    for n, shp in _SMALL:
        k = -(-_size(shp) // _PACK_UNIT) * 8
        out[n] = p[off:off + k].reshape(-1)[:_size(shp)].reshape(shp)
        off += k
    return out


_HALF_SHAPE = {"w_in": (SHARD_PAD // 2, D_MODEL), "conv_w": (2, 3 * BRANCH_W // N_CHIP), "w_mem_kv": (128, D_MODEL),
               "w_up": (2, BRANCH_W, D_MODEL // N_CHIP), "w_out": (128, D_MODEL)}


def _chip_major(g):
    g = jnp.swapaxes(g, 0, 1)
    return g.reshape((N_CHIP, 2 * g.shape[2]) + g.shape[3:])


def _half_major(g):
    g = g.reshape((N_CHIP, 2, g.shape[1] // 2) + g.shape[2:])
    return jnp.swapaxes(g, 0, 1).astype(BF16)


N_EARLY = 2


def _early_views(l, g_in, g_conv, small):
    return _layer_weights(l, _w_pad_from_slabs(g_in),
                          _chip_major(g_conv).transpose(1, 0, 2).reshape(4, 3 * BRANCH_W), small)


def _late_views(g_kv, g_up, g_out):
    return dict(w_mem_kv=_chip_major(g_kv).reshape(D_MODEL, D_MODEL),
                w_up=_chip_major(g_up).transpose(1, 2, 0, 3).reshape(4, BRANCH_W, D_MODEL),
                w_out=_chip_major(g_out).reshape(D_MODEL, D_MODEL))


def _pair_sums(g):
    big = [_slabs_from_pad(g["w_pad"]),
           _half_major(g["conv"].reshape(4, N_CHIP, 3 * BRANCH_W // N_CHIP).transpose(1, 0, 2)),
           _half_major(g["w_mem_kv"].reshape(N_CHIP, D_MODEL // N_CHIP, D_MODEL)),
           _half_major(g["w_up"].reshape(4, BRANCH_W, N_CHIP, D_MODEL // N_CHIP).transpose(2, 0, 1, 3)),
           _half_major(g["w_out"].reshape(N_CHIP, D_MODEL // N_CHIP, D_MODEL))]
    add2 = lambda a, b: [a.astype(F32) + b.astype(F32)]
    pair = []
    for n, b, p in zip(_BIG, big, _pair_exchange(big)):
        k = b.shape[-1]
        pair.append(_ew("pair_sum_" + n, add2, [(b.reshape(2, -1, k), "c"), p.reshape(-1, k)], 1, BF16)[0]
                    .reshape(p.shape))
    return pair


def _chip_sums(landed, pair, me):
    add4 = lambda a, b, c_, d: [((a.astype(F32) + b.astype(F32)) + c_.astype(F32)) + d.astype(F32)]
    totals = []
    for n, r, q in zip(_BIG, landed, pair):
        r = _own_slot(r, lax.dynamic_index_in_dim(q, me, 0), me, 0)
        k = r.shape[-1]
        totals.append(_ew("chip_sum_" + n, add4, [(r.reshape(N_CHIP, -1, k), s) for s in range(N_CHIP)], 1,
                          out_slot="c")[0].reshape((2,) + r.shape[1:]))
    return totals


def _own_slot(buf, mine, me, axis):
    return lax.dynamic_update_index_in_dim(buf, mine.astype(buf.dtype), me, axis)


def kernel(x, mem, norm_pre, norm_post, norm_mem, w_in, conv_w, a_log, dt_bias, dn_norm, gm_norm, spatial_w, spatial_b, sinks, w_mem_kv, w_up, w_out, loss_target, m_norm_pre, m_norm_post, m_norm_mem, m_w_in, m_conv_w, m_a_log, m_dt_bias, m_dn_norm, m_gm_norm, m_spatial_w, m_spatial_b, m_sinks, m_w_mem_kv, m_w_up, m_w_out, v_norm_pre, v_norm_post, v_norm_mem, v_w_in, v_conv_w, v_a_log, v_dt_bias, v_dn_norm, v_gm_norm, v_spatial_w, v_spatial_b, v_sinks, v_w_mem_kv, v_w_up, v_w_out):
    w = dict(norm_pre=norm_pre, norm_post=norm_post, norm_mem=norm_mem, w_in=w_in, conv_w=conv_w, a_log=a_log,
             dt_bias=dt_bias, dn_norm=dn_norm, gm_norm=gm_norm, spatial_w=spatial_w, spatial_b=spatial_b, sinks=sinks,
             w_mem_kv=w_mem_kv, w_up=w_up, w_out=w_out)
    m = dict(norm_pre=m_norm_pre, norm_post=m_norm_post, norm_mem=m_norm_mem, w_in=m_w_in, conv_w=m_conv_w,
             a_log=m_a_log, dt_bias=m_dt_bias, dn_norm=m_dn_norm, gm_norm=m_gm_norm, spatial_w=m_spatial_w,
             spatial_b=m_spatial_b, sinks=m_sinks, w_mem_kv=m_w_mem_kv, w_up=m_w_up, w_out=m_w_out)
    v = dict(norm_pre=v_norm_pre, norm_post=v_norm_post, norm_mem=v_norm_mem, w_in=v_w_in, conv_w=v_conv_w,
             a_log=v_a_log, dt_bias=v_dt_bias, dn_norm=v_dn_norm, gm_norm=v_gm_norm, spatial_w=v_spatial_w,
             spatial_b=v_spatial_b, sinks=v_sinks, w_mem_kv=v_w_mem_kv, w_up=v_w_up, w_out=v_w_out)
    me = 2 * lax.axis_index("x") + lax.axis_index("y")

    w_in_t = jnp.pad(w_in.astype(BF16).transpose(0, 2, 1), ((0, 0), (0, SHARD_PAD - SHARD_IN), (0, 0)))
    local = dict(w_in=w_in_t, conv_w=conv_w, w_mem_kv=w_mem_kv.astype(BF16), w_up=w_up.astype(BF16),
                 w_out=w_out.astype(BF16))
    halves = lambda l: [local[n][l].reshape((2,) + _HALF_SHAPE[n]) for n in _BIG]
    own = lambda gathered, mine: [_own_slot(g, h[:, None], me, 1) for g, h in zip(gathered, mine)]
    lands = [jax.ShapeDtypeStruct((2, N_CHIP) + _HALF_SHAPE[n], local[n].dtype) for n in _BIG]
    h0 = halves(0)
    g0 = own(_gather_weights(h0[:N_EARLY], [True, False]), h0[:N_EARLY])
    rest0 = _split_start("gather_l0_rest_start", "gather", h0[N_EARLY:], lands[N_EARLY:], g0[1])
    started = _split_start("gather_l1_start", "gather", halves(1), lands, rest0[3])

    xl, meml = x[0], mem[0]
    W0 = _early_views(0, g0[0], g0[1], w)
    W0["norm_pre"] = W0["norm_pre"] + started[3][0, 0]

    def late0(cols):
        mine, landed = _split_wait("gather_l0_rest_wait", "gather", rest0, cols)
        return _late_views(*own(_pair_forward(landed), mine))

    x1, sv0, W0 = _layer_fwd(xl, meml, W0, late0)
    mine1, landed1 = _split_wait("gather_l1_wait", "gather", started, x1)
    g1 = own(_pair_forward(landed1), mine1)
    W1 = dict(_early_views(1, g1[0], g1[1], w), **_late_views(*g1[N_EARLY:]))
    x2, sv1, _ = _layer_fwd(x1, meml, W1)
    dy, lrows = _rows_fwd("loss", _loss_fn, [Row(x2, D_MODEL, 0), Row(loss_target[0], D_MODEL, 0)], [],
                          [(D_MODEL, F32), (LANES, F32)], BIG_TR)
    loss = lax.psum(jnp.sum(lrows[:, 0]), ("x", "y", "c"))

    scattering = {}

    def start_scatter(l):
        def on_weight_grads(g):
            pair = _pair_sums(g)
            scattering[l] = _split_start("scatter_l%d_start" % l, "scatter", pair,
                                         [jax.ShapeDtypeStruct(p.shape, p.dtype) for p in pair], pair[1])
            return scattering[l][3]
        return on_weight_grads

    dx1, grads1 = _layer_bwd(dy, meml, W1, sv1, start_scatter(1))
    dx, grads0 = _layer_bwd(dx1, meml, W0, sv0, start_scatter(0))
    pair1, landed1 = _split_wait("scatter_l1_wait", "scatter", scattering[1], dx)
    after_start = scattering[0][3][0, 0]
    grads = [grads0, grads1]

    small_local = dict(
        norm_pre=jnp.stack([g["norm_pre"][0] for g in grads]), norm_post=jnp.stack([g["norm_post"][0] for g in grads]),
        norm_mem=jnp.stack([g["norm_mem"][0] for g in grads]), a_log=jnp.stack([g["a_vec"][0, 4:8] for g in grads]),
        dt_bias=jnp.stack([g["dt_vec"][0, 4:8] for g in grads]), dn_norm=jnp.stack([g["dn_norm"][0] for g in grads]),
        gm_norm=jnp.stack([g["gm_norm"][0] for g in grads]), spatial_w=jnp.stack([g["spatial_w"] for g in grads]),
        spatial_b=jnp.stack([g["spatial_b"][:, 0, :] for g in grads]),
        sinks=jnp.stack([g["sink_vec"][0, :8] for g in grads]))
    packed = jnp.pad(_pack_small(small_local) + after_start, ((0, 8), (0, 0)))
    gsmall_packed = _allreduce_small(packed.reshape(2, -1, 1024)).reshape(-1, 1024)[:_SMALL_ROWS]

    d_s, m_s, v_s = _ew("adamw_small", _adamw_fn, [_pack_small(w), gsmall_packed, _pack_small(m), _pack_small(v)], 3)
    gsmall, dsmall, msmall, vsmall = (_unpack_small(p) for p in (gsmall_packed, d_s, m_s, v_s))
    g_o, d_o, m_o, v_o = dict(gsmall), dict(dsmall), dict(msmall), dict(vsmall)
    tr = lambda a: a.transpose(0, 2, 1)
    w_t, m_t, v_t = tr(w["w_in"]), tr(m["w_in"]), tr(v["w_in"])

    def update(l, totals, into):
        outs = {}
        for n, t in zip(_BIG, totals):
            g_l = t.reshape(local[n].shape[1:])
            if n == "w_in":
                outs[n] = _adamw_rows("adamw_" + n, l, w_t, g_l, m_t, v_t, into and into[n])
            else:
                outs[n] = _adamw_layer("adamw_" + n, l, w[n], g_l, m[n], v[n], into and into[n])
        return outs

    landed1[1] = landed1[1] + after_start.astype(landed1[1].dtype)
    outs1 = update(1, _pair_share(_chip_sums(landed1, pair1, me)), None)
    pair0, landed0 = _split_wait("scatter_l0_wait", "scatter", scattering[0], outs1["w_in"][0])
    outs = update(0, _pair_share(_chip_sums(landed0, pair0, me)), outs1)
    for n in _BIG:
        d_o[n], m_o[n], v_o[n], g_o[n] = [tr(o) for o in outs[n]] if n == "w_in" else outs[n]
    return (loss, dx[None], *[g_o[n] for n in _NAMES], *[d_o[n] for n in _NAMES], *[m_o[n] for n in _NAMES],
            *[v_o[n] for n in _NAMES])
```

```python
import collections
import functools

import jax
import jax.numpy as jnp
from jax import lax
from jax.experimental import pallas as pl
from jax.experimental.pallas import tpu as pltpu

F32 = jnp.float32
BF16 = jnp.bfloat16

D_MODEL = 1024
BRANCH_W = 512
MEM_LEN = 256
N_LAYER = 2
N_CHIP = 4
N_DEV = 8
EPS = 1e-6
NEG_INF = -1e30
DN_CHUNK = 64
LANES = 128
VMEM_LIMIT = 48 * 1024 * 1024

ADAM_LR, ADAM_B1, ADAM_B2, ADAM_EPS, ADAM_WD, ADAM_STEP = 0.001, 0.9, 0.999, 1e-08, 0.01, 10

N_PAD = 10240
O_GATE = 0
O_AQ, O_AK, O_AV, O_AZ = 4096, 4608, 5120, 5632
O_BUV, O_BZ = 6144, 7168
O_CKV, O_BA = 7680, 7936
O_CQ, O_CZ = 8192, 8704
O_MQ, O_MZ = 9216, 9728
O_MISC, W_MISC = O_CKV, 512
_PAD_SEGS = [(5896, 4096), (0, 512), (512, 512), (1024, 512), (1536, 512), (2056, 1024), (3080, 512),
             (4104, 128), (4232, 128), (2048, 8), (None, 120), (None, 128),
             (3592, 512), (4360, 512), (4872, 512), (5384, 512)]
D_IN = 9992
SHARD_IN = D_IN // N_CHIP


SHARD_PAD = 2560


def _pad_parts():
    parts, off = [], 0
    for s, n in _PAD_SEGS:
        a = s
        while s is not None and a < s + n:
            chip = a // SHARD_IN
            b = min(s + n, (chip + 1) * SHARD_IN)
            parts.append((chip, a - chip * SHARD_IN, off + a - s, b - a))
            a = b
        off += n
    return parts


PERM_ROWS = 512
PERM_SLACK = 32


def _permute_rows(name, src, parts, n_out, out_dtype):
    B, Z = PERM_ROWS, PERM_ROWS + PERM_SLACK
    w = src.shape[1]
    plans = []
    for blk in range(n_out // B):
        o, runs = blk * B, []
        for s, d, n in parts:
            lo, hi = max(d, o), min(d + n, o + B)
            if lo < hi:
                s0 = s + lo - d
                wa = s0 // 16 * 16
                wb = min(-(-(s0 + hi - lo) // 16) * 16, src.shape[0])
                runs.append((wa, wb - wa, s0 - (lo - o) - wa, lo - o, hi - o))
        plans.append(runs)
    max_runs = max(len(r) for r in plans)
    nblk = len(plans)

    def body(*refs):
        src_ref, out_ref, inbuf, obuf, insem, outsem = (refs[0],) + refs[-5:]

        def in_copies(blk):
            return [pltpu.make_async_copy(src_ref.at[pl.ds(wa, ws)], inbuf.at[blk % 2, r, pl.ds(0, ws)],
                                          insem.at[blk % 2, r]) for r, (wa, ws, _, _, _) in enumerate(plans[blk])]

        def out_copy(blk):
            return pltpu.make_async_copy(obuf.at[blk % 2], out_ref.at[pl.ds(blk * B, B)], outsem.at[blk % 2])

        for cp in in_copies(0):
            cp.start()
        rid = _iota((B, 1), 0)
        for blk in range(nblk):
            if blk + 1 < nblk:
                for cp in in_copies(blk + 1):
                    cp.start()
            for cp in in_copies(blk):
                cp.wait()
            val = jnp.zeros((B, w), F32)
            for r, (wa, ws, t, l0, l1) in enumerate(plans[blk]):
                win = jnp.concatenate([inbuf[blk % 2, r, pl.ds(0, ws)].astype(F32), jnp.zeros((Z - ws, w), F32)], axis=0)
                moved = pltpu.roll(win, (-t) % Z, 0)[:B]
                val = jnp.where((rid >= l0) & (rid < l1), moved, val)
            if blk >= 2:
                out_copy(blk - 2).wait()
            obuf[blk % 2] = val.astype(out_dtype)
            out_copy(blk).start()
        for blk in range(max(nblk - 2, 0), nblk):
            out_copy(blk).wait()

    return pl.pallas_call(
        body, name=name, in_specs=[_ANY], out_specs=_ANY, out_shape=jax.ShapeDtypeStruct((n_out, w), out_dtype),
        scratch_shapes=[pltpu.VMEM((2, max_runs, Z, w), src.dtype), pltpu.VMEM((2, B, w), out_dtype),
                        pltpu.SemaphoreType.DMA((2, max_runs)), pltpu.SemaphoreType.DMA((2,))],
        compiler_params=_params(),
    )(src)


def _slab_parts():
    h, out = SHARD_PAD // 2, []
    for chip, s, d, n in _pad_parts():
        a = s
        while a < s + n:
            half = a // h
            b = min(s + n, (half + 1) * h)
            out.append(((half * N_CHIP + chip) * h + a - half * h, d + a - s, b - a))
            a = b
    return out


def _w_pad_from_slabs(slabs):
    return _permute_rows("w_pad_rows", slabs.reshape(-1, slabs.shape[-1]), _slab_parts(), N_PAD, BF16)


def _slabs_from_pad(dw):
    slabs = _permute_rows("w_pad_grad_rows", dw, [(d, s, n) for s, d, n in _slab_parts()], N_CHIP * SHARD_PAD, BF16)
    return slabs.reshape(2, N_CHIP, SHARD_PAD // 2, dw.shape[1])


def _dot(a, b, dims, prec):
    if prec == "bf16":
        return lax.dot_general(a.astype(BF16), b.astype(BF16), (dims, ((), ())), preferred_element_type=F32)
    return lax.dot_general(a, b, (dims, ((), ())), precision=lax.Precision.HIGHEST, preferred_element_type=F32)


_NN, _NT, _TN = ((1,), (0,)), ((1,), (1,)), ((0,), (0,))


def _make_mm(prec):
    @jax.custom_vjp
    def nn(a, b):
        return _dot(a, b, _NN, prec)

    @jax.custom_vjp
    def nt(a, b):
        return _dot(a, b, _NT, prec)

    @jax.custom_vjp
    def tn(a, b):
        return _dot(a, b, _TN, prec)

    nn.defvjp(lambda a, b: (nn(a, b), (a, b)), lambda r, g: (nt(g, r[1]), tn(r[0], g)))
    nt.defvjp(lambda a, b: (nt(a, b), (a, b)), lambda r, g: (nn(g, r[1]), tn(g, r[0])))
    tn.defvjp(lambda a, b: (tn(a, b), (a, b)), lambda r, g: (nt(r[1], g), nn(r[0], g)))
    return nn, nt, tn


_nn16, _nt16, _tn16 = _make_mm("bf16")
_nn32, _nt32, _tn32 = _make_mm("f32")


def _make_slice(axis):
    @functools.partial(jax.custom_vjp, nondiff_argnums=(1, 2, 3))
    def sl(x, a, b, n):
        return x[a:b] if axis == 0 else x[:, a:b]

    def fwd(x, a, b, n):
        return sl(x, a, b, n), None

    def bwd(a, b, n, _, g):
        parts = []
        if a > 0:
            parts.append(jnp.zeros((a, g.shape[1]) if axis == 0 else (g.shape[0], a), g.dtype))
        parts.append(g)
        if n - b > 0:
            parts.append(jnp.zeros((n - b, g.shape[1]) if axis == 0 else (g.shape[0], n - b), g.dtype))
        return (jnp.concatenate(parts, axis=axis),)

    sl.defvjp(fwd, bwd)
    return sl


_sl0, _sl1 = _make_slice(0), _make_slice(1)


def _rowsl(x, a, b):
    return _sl0(x, a, b, x.shape[0])


def _cols(x, a, b):
    return _sl1(x, a, b, x.shape[1])


@functools.partial(jax.custom_vjp, nondiff_argnums=(1,))
def _rollr(x, s):
    return pltpu.roll(x, s, 0)


_rollr.defvjp(lambda x, s: (_rollr(x, s), None),
              lambda s, _, g: (pltpu.roll(g, g.shape[0] - s, 0),))


def _iota(shape, axis):
    return lax.broadcasted_iota(jnp.int32, shape, axis)


def _sigmoid(x):
    return lax.logistic(x)


def _silu(x):
    return x * _sigmoid(x)


def _gelu(x):
    return 0.5 * x * (1.0 + jnp.tanh(0.7978845608028654 * (x + 0.044715 * (x * x * x))))


def _softplus(x):
    return jnp.maximum(x, 0.0) + jnp.log(1.0 + jnp.exp(-jnp.abs(x)))


def _rms(x, g):
    return x * lax.rsqrt(jnp.mean(x * x, axis=-1, keepdims=True) + EPS) * g


def _lane_pick(x, lane):
    return jnp.sum(x * (_iota((1, x.shape[1]), 1) == lane).astype(F32), axis=1, keepdims=True)


Row = collections.namedtuple("Row", "arr w cb hb grad", defaults=(0, True))


def _full_spec(shape):
    return pl.BlockSpec(shape, lambda i, _n=len(shape): (0,) * _n)


def _load_params(refs):
    return [[p[g].astype(F32) for g in range(p.shape[0])] if len(p.shape) == 3 else p[...].astype(F32)
            for p in refs]


def _params(**kw):
    return pltpu.CompilerParams(vmem_limit_bytes=VMEM_LIMIT, **kw)


def _rows_fwd(name, fn, rows, params, outs, tr, carry=None):
    T = rows[0].arr.shape[0]
    n = T // tr
    halos = [r for r in rows if r.hb]
    nr, nh, npar, no = len(rows), len(halos), len(params), len(outs)

    def body(*refs):
        row_refs, halo_refs = refs[:nr], refs[nr:nr + nh]
        par_refs = refs[nr + nh:nr + nh + npar]
        out_refs = refs[nr + nh + npar:nr + nh + npar + no]
        rest = refs[nr + nh + npar + no:]
        first = pl.program_id(0) == 0
        cvals = None
        if carry is not None:
            csave_ref, carry_ref = rest

            @pl.when(first)
            def _():
                carry_ref[...] = jnp.zeros_like(carry_ref)

            cvals = [carry_ref[g] for g in range(carry[0])]
            for g in range(carry[0]):
                csave_ref[0, g] = cvals[g]
        c_out, o = fn(first, cvals, [r[...].astype(F32) for r in row_refs],
                      [h[...].astype(F32) for h in halo_refs], _load_params(par_refs))
        for r, v in zip(out_refs, o):
            r[...] = v.astype(r.dtype)
        if carry is not None:
            for g in range(carry[0]):
                carry_ref[g] = c_out[g]

    in_specs = [pl.BlockSpec((tr, r.w), lambda i, c=r.cb: (i, c)) for r in rows]
    in_specs += [pl.BlockSpec((r.hb, r.w), lambda i, c=r.cb, q=tr // r.hb: (jnp.maximum(i * q - 1, 0), c))
                 for r in halos]
    in_specs += [_full_spec(p.shape) for p in params]
    out_shape = [jax.ShapeDtypeStruct((T, w), dt) for w, dt in outs]
    out_specs = [pl.BlockSpec((tr, w), lambda i: (i, 0)) for w, _ in outs]
    scratch = []
    if carry is not None:
        out_shape.append(jax.ShapeDtypeStruct((n,) + carry, F32))
        out_specs.append(pl.BlockSpec((1,) + carry, lambda i: (i, 0, 0, 0)))
        scratch.append(pltpu.VMEM(carry, F32))
    return pl.pallas_call(
        body, name=name, grid=(n,), in_specs=in_specs, out_specs=out_specs, out_shape=out_shape,
        scratch_shapes=scratch, compiler_params=_params(dimension_semantics=("arbitrary",)),
    )(*[r.arr for r in rows], *[r.arr for r in halos], *params)


def _rows_bwd(name, fn, rows, params, douts, tr, carry=None, csave=None, dcols=None):
    T = rows[0].arr.shape[0]
    n = T // tr
    halos = [r for r in rows if r.hb]
    grows = [r for r in rows if r.grad is True]
    crows = [r for r in rows if r.grad == "cols"]
    wcols = sum(r.w for r in crows)
    nr, nh, npar, nd, ng = len(rows), len(halos), len(params), len(douts), len(grows)
    nc = 0 if carry is None else 1
    ncol = 1 if crows else 0
    nalias = 1 if (crows and dcols is not None) else 0

    def body(*refs):
        row_refs, halo_refs = refs[:nr], refs[nr:nr + nh]
        par_refs = refs[nr + nh:nr + nh + npar]
        k = nr + nh + npar
        csave_ref = refs[k] if nc else None
        dout_refs = refs[k + nc:k + nc + nd]
        k = k + nc + nd + nalias
        drow_refs = refs[k:k + ng]
        dcols_ref = refs[k + ng] if ncol else None
        dpar_refs = refs[k + ng + ncol:k + ng + ncol + npar]
        k = k + ng + ncol + npar
        dcarry_ref = refs[k] if nc else None
        hgrad_refs = refs[k + nc:]
        i = pl.program_id(0)
        first_tile = i == n - 1

        @pl.when(i == 0)
        def _():
            for r in dpar_refs:
                r[...] = jnp.zeros_like(r)
            for r in hgrad_refs:
                r[...] = jnp.zeros_like(r)
            if nc:
                dcarry_ref[...] = jnp.zeros_like(dcarry_ref)

        rv = [r[...].astype(F32) for r in row_refs]
        hv = [h[...].astype(F32) for h in halo_refs]
        pv = _load_params(par_refs)
        dov = [d[...].astype(F32) for d in dout_refs]
        if nc:
            cv = [csave_ref[0, g] for g in range(carry[0])]
            _, vjp = jax.vjp(lambda c, r, h, p: fn(first_tile, c, r, h, p), cv, rv, hv, pv)
            dc, dr, dh, dp = vjp(([dcarry_ref[g] for g in range(carry[0])], dov))
            for g in range(carry[0]):
                dcarry_ref[g] = dc[g]
        else:
            _, vjp = jax.vjp(lambda r, h, p: fn(first_tile, None, r, h, p)[1], rv, hv, pv)
            dr, dh, dp = vjp(dov)
        gi = hi = 0
        pieces = []
        for kk, r in enumerate(rows):
            d = dr[kk]
            if r.hb:
                carried = hgrad_refs[hi][...]
                d = d + (carried if tr == r.hb else
                         jnp.concatenate([jnp.zeros((tr - r.hb, r.w), F32), carried], axis=0))
                hgrad_refs[hi][...] = dh[hi]
                hi += 1
            if r.grad is True:
                drow_refs[gi][...] = d.astype(drow_refs[gi].dtype)
                gi += 1
            elif r.grad == "cols":
                pieces.append(d.astype(BF16))
        if ncol:
            dcols_ref[...] = pieces[0] if len(pieces) == 1 else jnp.concatenate(pieces, axis=1)
        for r, d in zip(dpar_refs, dp):
            if len(r.shape) == 3:
                for g in range(r.shape[0]):
                    r[g] += d[g]
            else:
                r[...] += d

    rev = lambda i: n - 1 - i
    in_specs = [pl.BlockSpec((tr, r.w), lambda i, c=r.cb: (rev(i), c)) for r in rows]
    in_specs += [pl.BlockSpec((r.hb, r.w), lambda i, c=r.cb, q=tr // r.hb: (jnp.maximum(rev(i) * q - 1, 0), c))
                 for r in halos]
    in_specs += [_full_spec(p.shape) for p in params]
    args = [r.arr for r in rows] + [r.arr for r in halos] + list(params)
    scratch = []
    if nc:
        in_specs.append(pl.BlockSpec((1,) + carry, lambda i: (rev(i), 0, 0, 0)))
        args.append(csave)
        scratch.append(pltpu.VMEM(carry, F32))
    douts = [d if isinstance(d, Row) else Row(d, d.shape[1], 0) for d in douts]
    in_specs += [pl.BlockSpec((tr, d.w), lambda i, c=d.cb: (rev(i), c)) for d in douts]
    args += [d.arr for d in douts]
    aliases = {}
    if nalias:
        aliases = {len(args): ng}
        in_specs.append(pl.BlockSpec(memory_space=pl.ANY))
        args.append(dcols)
    scratch += [pltpu.VMEM((r.hb, r.w), F32) for r in halos]
    out_shape = [jax.ShapeDtypeStruct((T, r.w), F32) for r in grows]
    out_specs = [pl.BlockSpec((tr, r.w), lambda i: (rev(i), 0)) for r in grows]
    if ncol:
        off = crows[0].cb * crows[0].w
        assert off % wcols == 0 and all(a.cb * a.w + a.w == b.cb * b.w for a, b in zip(crows, crows[1:]))
        out_shape.append(jax.ShapeDtypeStruct((T, N_PAD), BF16))
        out_specs.append(pl.BlockSpec((tr, wcols), lambda i, c=off // wcols: (rev(i), c)))
    out_shape += [jax.ShapeDtypeStruct(p.shape, F32) for p in params]
    out_specs += [_full_spec(p.shape) for p in params]
    res = pl.pallas_call(
        body, name=name, grid=(n,), in_specs=in_specs, out_specs=out_specs, out_shape=out_shape,
        scratch_shapes=scratch, input_output_aliases=aliases,
        compiler_params=_params(dimension_semantics=("arbitrary",)),
    )(*args)
    return list(res[:ng]), list(res[ng + ncol:]), (res[ng] if ncol else dcols)


def _fill_misc(dcols, dkv, dba, tr):
    T = dkv.shape[0]

    def body(kv_ref, ba_ref, _, o_ref):
        o_ref[...] = jnp.concatenate([kv_ref[...], ba_ref[...]], axis=1).astype(BF16)

    return pl.pallas_call(
        body, name="misc_bwd", grid=(T // tr,),
        in_specs=[pl.BlockSpec((tr, 256), lambda i: (i, 0)), pl.BlockSpec((tr, 256), lambda i: (i, 0)),
                  pl.BlockSpec(memory_space=pl.ANY)],
        out_specs=pl.BlockSpec((tr, W_MISC), lambda i: (i, O_MISC // W_MISC)),
        out_shape=jax.ShapeDtypeStruct((T, N_PAD), BF16), input_output_aliases={2: 0},
        compiler_params=_params(dimension_semantics=("arbitrary",)),
    )(dkv, dba, dcols)


def _up_bwd(ys, cols, dm, w_up):
    T, tr = dm.shape[0], UPB_TR

    def body(y_ref, gl_ref, dm_ref, w_ref, dy_ref, dgl_ref, dw_ref):
        @pl.when(pl.program_id(1) == 0)
        def _():
            dw_ref[...] = jnp.zeros_like(dw_ref)

        _, vjp = jax.vjp(lambda y, gl, w: _sigmoid(gl) * _nn16(y, w),
                         y_ref[...].astype(F32), gl_ref[...].astype(F32), w_ref[...].astype(F32))
        dy, dgl, dw = vjp(dm_ref[...])
        dy_ref[...] = dy
        dgl_ref[...] = dgl.astype(BF16)
        dw_ref[...] += dw

    branch_rows = lambda w: pl.BlockSpec((tr, w), lambda n, i: (i, n))
    weight = pl.BlockSpec((None, BRANCH_W, D_MODEL), lambda n, i: (n, 0, 0))
    return pl.pallas_call(
        body, name="up_bwd", grid=(4, T // tr),
        in_specs=[branch_rows(BRANCH_W), branch_rows(D_MODEL), pl.BlockSpec((tr, D_MODEL), lambda n, i: (i, 0)), weight],
        out_specs=[branch_rows(BRANCH_W), branch_rows(D_MODEL), weight],
        out_shape=[jax.ShapeDtypeStruct((T, 4 * BRANCH_W), F32), jax.ShapeDtypeStruct((T, N_PAD), BF16),
                   jax.ShapeDtypeStruct(w_up.shape, F32)],
        compiler_params=_params(dimension_semantics=("arbitrary", "arbitrary")),
    )(ys, cols, dm, w_up)


def _matmul(name, a, b, kind, out_dtype, tm, tn, tk, after=None):
    if kind == "tn":
        (K, M), N = a.shape, b.shape[1]
    else:
        (M, K), N = a.shape, (b.shape[0] if kind == "nt" else b.shape[1])
    tm, tn, tk = min(tm, M), min(tn, N), min(tk, K)
    nk = K // tk
    dims = {"nn": _NN, "nt": _NT, "tn": _TN}[kind]

    n_after = 0 if after is None else 1

    def body(*refs):
        a_ref, b_ref, o_ref, acc = refs[0], refs[1], refs[2 + n_after], refs[3 + n_after:]
        part = lax.dot_general(a_ref[...], b_ref[...], (dims, ((), ())), preferred_element_type=F32)
        if nk == 1:
            o_ref[...] = part.astype(o_ref.dtype)
            return
        acc_ref = acc[0] if acc else o_ref
        k = pl.program_id(2)

        @pl.when(k == 0)
        def _():
            acc_ref[...] = part

        @pl.when(k > 0)
        def _():
            acc_ref[...] += part

        if acc:
            @pl.when(k == nk - 1)
            def _():
                o_ref[...] = acc_ref[...].astype(o_ref.dtype)

    a_spec = pl.BlockSpec((tk, tm), lambda i, j, k: (k, i)) if kind == "tn" else pl.BlockSpec((tm, tk), lambda i, j, k: (i, k))
    b_spec = pl.BlockSpec((tn, tk), lambda i, j, k: (j, k)) if kind == "nt" else pl.BlockSpec((tk, tn), lambda i, j, k: (k, j))
    return pl.pallas_call(
        body, name=name, grid=(M // tm, N // tn, nk), in_specs=[a_spec, b_spec] + [_ANY] * n_after,
        out_specs=pl.BlockSpec((tm, tn), lambda i, j, k: (i, j)),
        out_shape=jax.ShapeDtypeStruct((M, N), out_dtype),
        scratch_shapes=[pltpu.VMEM((tm, tn), F32)] if nk > 1 and out_dtype != F32 else [],
        compiler_params=_params(dimension_semantics=("arbitrary", "arbitrary", "arbitrary")),
    )(a, b, *([] if after is None else [after]))


def _pre_fn(first, _, rows, halos, params):
    return None, [_rms(rows[0], params[0])]


def _pre_fn_res(first, _, rows, halos, params):
    return None, [_rms(rows[0], params[0]), rows[0]]


def _memkv_fn(first, _, rows, halos, params):
    g, w = params
    return None, [_nn16(_rms(rows[0], g), w)]


def _conv_silu(x, halo, w4, keep_halo):
    tr = x.shape[0]
    halo = halo * keep_halo
    rid = _iota((tr, 1), 0)
    acc = w4[3] * x
    for s in (1, 2, 3):
        hs = jnp.concatenate([_rollr(halo, s), jnp.zeros((tr - halo.shape[0], x.shape[1]), F32)], axis=0)
        acc = acc + w4[3 - s] * jnp.where(rid < s, hs, _rollr(x, s))
    return _silu(acc)


def _dn_fn(first, S, rows, halos, params):
    qp, kp, vp, z, ba = rows
    conv, a_vec, dt_vec, dnorm = params
    ba = _cols(ba, 0, LANES)
    tr = qp.shape[0]
    keep = jnp.where(first, 0.0, 1.0)
    q = _conv_silu(qp, halos[0], [conv[3 * j + 0] for j in range(4)], keep)
    k = _conv_silu(kp, halos[1], [conv[3 * j + 1] for j in range(4)], keep)
    v = _conv_silu(vp, halos[2], [conv[3 * j + 2] for j in range(4)], keep)
    qh, kh, vh = [], [], []
    for h in range(4):
        a, b = h * LANES, (h + 1) * LANES
        xq, xk = _cols(q, a, b), _cols(k, a, b)
        qh.append(xq * lax.rsqrt(jnp.sum(xq * xq, axis=1, keepdims=True) + EPS) * (LANES ** -0.5))
        kh.append(xk * lax.rsqrt(jnp.sum(xk * xk, axis=1, keepdims=True) + EPS))
        vh.append(_cols(v, a, b))
    beta_all = _sigmoid(ba)
    g_all = -jnp.exp(a_vec) * _softplus(ba + dt_vec)
    C = DN_CHUNK
    ii, jj = _iota((C, C), 0), _iota((C, C), 1)
    strict, incl = ii > jj, ii >= jj
    eye = (ii == jj).astype(F32)
    last_row = (_iota((C, 1), 0) == C - 1).astype(F32)
    n_chunk = tr // C
    pairs = [(c, h) for c in range(n_chunk) for h in range(4)]
    rows_of = lambda a, c: _rowsl(a, c * C, (c + 1) * C)
    gcs = [_nn32(incl.astype(F32), rows_of(g_all, c)) for c in range(n_chunk)]
    qc = {(c, h): rows_of(qh[h], c) for c, h in pairs}
    kc = {(c, h): rows_of(kh[h], c) for c, h in pairs}
    beta = {(c, h): _lane_pick(rows_of(beta_all, c), h) for c, h in pairs}
    gc = {(c, h): _lane_pick(gcs[c], 4 + h) for c, h in pairs}
    dec = {p: jnp.exp(jnp.where(incl, gc[p] - jnp.sum(eye * gc[p], axis=0, keepdims=True), 0.0)) for p in pairs}
    egc = {p: jnp.exp(gc[p]) for p in pairs}
    kb = {p: kc[p] * beta[p] for p in pairs}
    kq = {p: _nt16(jnp.concatenate([kb[p], qc[p]], axis=0), kc[p]) for p in pairs}
    P = {p: -jnp.where(strict, _rowsl(kq[p], 0, C) * dec[p], 0.0) for p in pairs}
    aqk = {p: jnp.where(incl, _rowsl(kq[p], C, 2 * C) * dec[p], 0.0) for p in pairs}
    tinv = {p: eye + P[p] for p in pairs}
    P = {p: _nn16(P[p], P[p]) for p in pairs}
    for j in range(5):
        if j < 4:
            pt = {p: _nn16(jnp.concatenate([P[p], tinv[p]], axis=0), P[p]) for p in pairs}
            tinv = {p: tinv[p] + _rowsl(pt[p], C, 2 * C) for p in pairs}
            P = {p: _rowsl(pt[p], 0, C) for p in pairs}
        else:
            tinv = {p: tinv[p] + _nn16(tinv[p], P[p]) for p in pairs}
    uw = {(c, h): _nn16(tinv[c, h], jnp.concatenate([rows_of(vh[h], c) * beta[c, h], kb[c, h] * egc[c, h]], axis=1))
          for c, h in pairs}
    S = list(S)
    ychunks = []
    for c in range(n_chunk):
        zc = rows_of(z, c)
        hs = range(4)
        ws = [_nn16(jnp.concatenate([_cols(uw[c, h], LANES, 2 * LANES), qc[c, h] * egc[c, h]], axis=0), S[h]) for h in hs]
        vnew = [_cols(uw[c, h], 0, LANES) - _rowsl(ws[h], 0, C) for h in hs]
        o = [_rowsl(ws[h], C, 2 * C) + _nn16(aqk[c, h], vnew[h]) for h in hs]
        glast = [jnp.sum(gc[c, h] * last_row, axis=0, keepdims=True) for h in hs]
        S = [S[h] * jnp.exp(glast[h]) + _tn16(kc[c, h] * jnp.exp(glast[h] - gc[c, h]), vnew[h]) for h in hs]
        ychunks.append(jnp.concatenate(
            [_rms(o[h], dnorm) * _silu(_cols(zc, h * LANES, (h + 1) * LANES)) for h in hs], axis=1))
    return S, [jnp.concatenate(ychunks, axis=0)]


def _gm_fn(first, _, rows, halos, params):
    uv, z = rows
    gnorm, ws, bs = params
    tr = uv.shape[0]
    guv = _gelu(uv)
    u = _cols(guv, 0, BRANCH_W)
    v = _rms(_cols(guv, BRANCH_W, 2 * BRANCH_W), gnorm)
    ii, jj = _iota((LANES, LANES), 0), _iota((LANES, LANES), 1)
    eye = (ii == jj).astype(F32)
    wsm = [jnp.where(ii >= jj, ws[g], 0.0) for g in range(4)]
    bcol = [jnp.sum(eye * bs[g], axis=1, keepdims=True) for g in range(4)]
    chunks = []
    for c in range(tr // LANES):
        vc = _rowsl(v, c * LANES, (c + 1) * LANES)
        chunks.append(jnp.concatenate(
            [_nn16(wsm[g], _cols(vc, g * LANES, (g + 1) * LANES)) + bcol[g] for g in range(4)], axis=1))
    return None, [u * jnp.concatenate(chunks, axis=0) * _silu(z)]


def _swa_fn(first, _, rows, halos, params):
    q, kvc, z = rows
    sink_vec = params[0]
    P = LANES
    n_blk = q.shape[0] // P
    r, cc = _iota((P, P), 0), _iota((P, P), 1)
    lane = _iota((1, P), 1)
    key = _iota((P, 2 * P), 1)
    dist = _iota((P, 2 * P), 0) + P - key
    in_window = (dist >= 0) & (dist < P)
    valid = [in_window & (key >= jnp.where(first, P, 0))] + [in_window] * (n_blk - 1)
    halves = [(lane < 64).astype(F32), (lane >= 64).astype(F32)]
    dup = [(r == kh * 64 + (cc & 63)).astype(F32) for kh in range(2)]
    kv_blk = [halos[0]] + [_rowsl(kvc, b * P, (b + 1) * P) for b in range(n_blk)]
    pairs = [(b, kh) for b in range(n_blk) for kh in range(2)]
    kkvv = {}
    for b in range(n_blk):
        kv = jnp.concatenate([kv_blk[b], kv_blk[b + 1]], axis=0)
        k_v = jnp.concatenate([_cols(kv, 0, P), _cols(kv, P, 2 * P)], axis=0)
        for kh in range(2):
            kkvv[b, kh] = _nn16(k_v, dup[kh])
    scores = {}
    for b, kh in pairs:
        q_b = _rowsl(q, b * P, (b + 1) * P)
        stacked = jnp.concatenate([_cols(q_b, (2 * kh + g // 2) * P, (2 * kh + g // 2 + 1) * P) * halves[g % 2]
                                   for g in range(4)], axis=0)
        scores[b, kh] = _nt16(stacked, _rowsl(kkvv[b, kh], 0, 2 * P))
    probs = {}
    for b, kh in pairs:
        ps = []
        for g in range(4):
            s = jnp.where(valid[b], _rowsl(scores[b, kh], g * P, (g + 1) * P) * 0.125, NEG_INF)
            sink = _lane_pick(sink_vec, kh * 4 + g)
            m = lax.stop_gradient(jnp.maximum(jnp.max(s, axis=1, keepdims=True), sink))
            e = jnp.exp(s - m)
            ps.append(e / (jnp.sum(e, axis=1, keepdims=True) + jnp.exp(sink - m)))
        probs[b, kh] = jnp.concatenate(ps, axis=0)
    outs = {p: _nn16(probs[p], _rowsl(kkvv[p], 2 * P, 4 * P)) for p in pairs}
    tile = [jnp.concatenate([_rowsl(outs[b, j // 2], (2 * (j % 2)) * P, (2 * (j % 2) + 1) * P) * halves[0]
                             + _rowsl(outs[b, j // 2], (2 * (j % 2) + 1) * P, (2 * (j % 2) + 2) * P) * halves[1]
                             for j in range(4)], axis=1) for b in range(n_blk)]
    return None, [jnp.concatenate(tile, axis=0) * _silu(z)]


def _mem_fn(first, _, rows, halos, params):
    q, z = rows
    mkv = params[0]
    heads = [(h * LANES, (h + 1) * LANES) for h in range(4)]
    scores = [_nt16(_cols(q, a, b), _cols(mkv, a, b)) * (LANES ** -0.5) for a, b in heads]
    probs = []
    for s in scores:
        e = jnp.exp(s - lax.stop_gradient(jnp.max(s, axis=1, keepdims=True)))
        probs.append(e / jnp.sum(e, axis=1, keepdims=True))
    outs = [_nn16(p, _cols(mkv, BRANCH_W + a, BRANCH_W + b)) for p, (a, b) in zip(probs, heads)]
    return None, [jnp.concatenate(outs, axis=1) * _silu(z)]


def _up_fn(first, _, rows, halos, params):
    ys, gl, w_up = rows[:4], rows[4], params[0]
    merged = None
    for n in range(4):
        term = _sigmoid(_cols(gl, n * D_MODEL, (n + 1) * D_MODEL)) * _nn16(ys[n], w_up[n])
        merged = term if merged is None else merged + term
    return None, [merged]


def _out_fn(first, _, rows, halos, params):
    x, merged = rows
    w, g = params
    return None, [x + _rms(_nn16(merged, w), g)]


def _loss_fn(first, _, rows, halos, params):
    y, t = rows
    d = y - t
    lrow = 0.5 * jnp.mean(d * d, axis=1, keepdims=True)
    return None, [d * (1.0 / D_MODEL), jnp.broadcast_to(lrow, (y.shape[0], LANES))]


TR = 256
BIG_TR = 512
SWA_TR = 256
DN_TR = 256
UP_TR = 256
UPB_TR = 512
CONV_HALO = 16
CARRY = (4, LANES, LANES)


def _branch_rows(cols, g):
    hb = CONV_HALO
    a = [Row(cols, 512, O_AQ // 512, hb, g), Row(cols, 512, O_AK // 512, hb, g), Row(cols, 512, O_AV // 512, hb, g),
         Row(cols, 512, O_AZ // 512, 0, g), Row(cols, 256, O_BA // 256)]
    b = [Row(cols, 1024, O_BUV // 1024, 0, g), Row(cols, 512, O_BZ // 512, 0, g)]
    c = [Row(cols, 512, O_CQ // 512, 0, g), Row(cols, 256, O_CKV // 256, LANES), Row(cols, 512, O_CZ // 512, 0, g)]
    m = [Row(cols, 512, O_MQ // 512, 0, g), Row(cols, 512, O_MZ // 512, 0, g)]
    return a, b, c, m


def _layer_fwd(x, mem, W, late_weights=None):
    h = _rows_fwd("prenorm_fwd", _pre_fn, [Row(x, D_MODEL, 0)], [W["norm_pre"]], [(D_MODEL, BF16)], BIG_TR)[0]
    cols = _matmul("in_proj_fwd", h, W["w_pad"], "nt", BF16, 2048, 1024, 1024)
    if late_weights is not None:
        W = dict(W, **late_weights(cols))
    mem_kv = _rows_fwd("memkv_fwd", _memkv_fn, [Row(mem, D_MODEL, 0)], [W["norm_mem"], W["w_mem_kv"]],
                       [(D_MODEL, F32)], MEM_LEN)[0]
    ra, rb, rc, rm = _branch_rows(cols, True)
    y_a, csave = _rows_fwd("dn_fwd", _dn_fn, ra, [W["conv"], W["a_vec"], W["dt_vec"], W["dn_norm"]],
                           [(BRANCH_W, BF16)], DN_TR, CARRY)
    y_b = _rows_fwd("gm_fwd", _gm_fn, rb, [W["gm_norm"], W["spatial_w"], W["spatial_b"]], [(BRANCH_W, BF16)], BIG_TR)[0]
    y_c = _rows_fwd("swa_fwd", _swa_fn, rc, [W["sink_vec"]], [(BRANCH_W, BF16)], SWA_TR)[0]
    y_m = _rows_fwd("mem_fwd", _mem_fn, rm, [mem_kv], [(BRANCH_W, BF16)], BIG_TR)[0]
    ys = [y_a, y_b, y_c, y_m]
    merged = _rows_fwd("up_fwd", _up_fn, [Row(y, BRANCH_W, 0) for y in ys] + [Row(cols, 4 * D_MODEL, 0)],
                       [W["w_up"]], [(D_MODEL, BF16)], UP_TR)[0]
    x_new = _rows_fwd("out_fwd", _out_fn, [Row(x, D_MODEL, 0), Row(merged, D_MODEL, 0)],
                      [W["w_out"], W["norm_post"]], [(D_MODEL, F32)], TR)[0]
    return x_new, dict(x=x, h=h, cols=cols, mem_kv=mem_kv, csave=csave, ys=ys, merged=merged), W


def _layer_bwd(dxn, mem, W, sv, on_weight_grads=None):
    x, cols = sv["x"], sv["cols"]
    (dx_res, dm), (dw_out, dnorm_post), _ = _rows_bwd(
        "out_bwd", _out_fn, [Row(x, D_MODEL, 0), Row(sv["merged"], D_MODEL, 0)], [W["w_out"], W["norm_post"]],
        [dxn], TR)
    dys, dcols, dw_up = _up_bwd(jnp.concatenate(sv["ys"], axis=1), cols, dm, W["w_up"])
    dys = [Row(dys, BRANCH_W, n) for n in range(4)]
    ra, rb, rc, rm = _branch_rows(cols, "cols")
    (dba,), (dconv, da_vec, ddt_vec, ddn_norm), dcols = _rows_bwd(
        "dn_bwd", _dn_fn, ra, [W["conv"], W["a_vec"], W["dt_vec"], W["dn_norm"]], [dys[0]], DN_TR, CARRY,
        sv["csave"], dcols=dcols)
    _, (dgm_norm, dws, dbs), dcols = _rows_bwd(
        "gm_bwd", _gm_fn, rb, [W["gm_norm"], W["spatial_w"], W["spatial_b"]], [dys[1]], BIG_TR, dcols=dcols)
    (dkv_c,), (dsink,), dcols = _rows_bwd("swa_bwd", _swa_fn, rc, [W["sink_vec"]], [dys[2]], SWA_TR, dcols=dcols)
    _, (dmem_kv,), dcols = _rows_bwd("mem_bwd", _mem_fn, rm, [sv["mem_kv"]], [dys[3]], BIG_TR, dcols=dcols)
    dcols = _fill_misc(dcols, dkv_c, dba, BIG_TR)
    _, (dnorm_mem, dw_mem_kv), _ = _rows_bwd("memkv_bwd", _memkv_fn, [Row(mem, D_MODEL, 0, 0, False)],
                                             [W["norm_mem"], W["w_mem_kv"]], [dmem_kv], MEM_LEN)
    dw_pad = _matmul("in_proj_dw", dcols, sv["h"], "tn", BF16, 1024, 1024, 2048)
    grads = dict(norm_post=dnorm_post, norm_mem=dnorm_mem, w_pad=dw_pad, conv=dconv,
                 a_vec=da_vec, dt_vec=ddt_vec, dn_norm=ddn_norm, gm_norm=dgm_norm, spatial_w=dws, spatial_b=dbs,
                 sink_vec=dsink, w_mem_kv=dw_mem_kv, w_up=dw_up, w_out=dw_out)
    started = None if on_weight_grads is None else on_weight_grads(grads)
    dh = _matmul("in_proj_dx", dcols, W["w_pad"], "nn", F32, 1024, 1024, 2048, after=started)
    (dx,), (grads["norm_pre"],), _ = _rows_bwd("prenorm_bwd", _pre_fn_res, [Row(x, D_MODEL, 0)], [W["norm_pre"]],
                                               [dh, dx_res], BIG_TR)
    return dx, grads


def _lane_vec(v, off):
    return jnp.zeros((1, LANES), F32).at[0, off:off + v.shape[0]].set(v)


def _layer_weights(l, w_pad, conv_w, small, **late):
    return dict(
        late, w_pad=w_pad, conv=conv_w.reshape(4, 3, BRANCH_W).reshape(12, 1, BRANCH_W),
        norm_pre=small["norm_pre"][l][None], norm_post=small["norm_post"][l][None],
        norm_mem=small["norm_mem"][l][None],
        a_vec=_lane_vec(small["a_log"][l], 4), dt_vec=_lane_vec(small["dt_bias"][l], 4),
        dn_norm=small["dn_norm"][l][None], gm_norm=small["gm_norm"][l][None],
        spatial_w=small["spatial_w"][l], spatial_b=small["spatial_b"][l][:, None, :],
        sink_vec=_lane_vec(small["sinks"][l], 0))


_MESH = pl.DeviceIdType.MESH
_ANY = pl.BlockSpec(memory_space=pl.ANY)


def _position():
    return lax.axis_index("x"), lax.axis_index("y"), lax.axis_index("c")


def _remote(src, dst, send_sem, recv_sem, dev):
    return pltpu.make_async_remote_copy(src_ref=src, dst_ref=dst, send_sem=send_sem, recv_sem=recv_sem,
                                        device_id=dev, device_id_type=_MESH)


def _hbm_call(name, body, arrs, out_shapes, sems, aliases=None):
    return pl.pallas_call(
        body, name=name, in_specs=[_ANY] * len(arrs), out_specs=[_ANY] * len(out_shapes), out_shape=out_shapes,
        scratch_shapes=[pltpu.SemaphoreType.DMA((k,)) for k in sems], input_output_aliases=aliases or {},
        compiler_params=pltpu.CompilerParams(has_side_effects=True),
    )(*arrs)


def _other_chips(x, y):
    return [(1 - x, y), (x, 1 - y), (1 - x, 1 - y)]


def _gather_weights(arrs, relayed):
    n = len(arrs)

    def body(*refs):
        ins, outs = refs[:n], refs[n:2 * n]
        ici_send, ici_recv, d2d_send, d2d_recv = refs[2 * n:]
        x, y, c = _position()
        me = 2 * x + y
        xn, yn, dg = _other_chips(x, y)
        chip = lambda p: 2 * p[0] + p[1]
        sends = []

        def go(cp):
            cp.start()
            sends.append(cp)

        def ici(a, j, src, dst, to):
            return _remote(src, dst, ici_send.at[4 * a + j], ici_recv.at[4 * a + j], (*to, c))

        for a in range(n):
            go(ici(a, 0, ins[a].at[c], outs[a].at[c, me], xn))
            go(ici(a, 1, ins[a].at[c], outs[a].at[c, me], yn))
            if not relayed[a]:
                go(ici(a, 2, ins[a].at[c], outs[a].at[c, me], dg))
        for a in range(n):
            h = arrs[a].shape[1] // 2
            from_x, from_y = outs[a].at[c, chip(xn)], outs[a].at[c, chip(yn)]
            ici(a, 0, ins[a].at[c], from_x, xn).wait_recv()
            if relayed[a]:
                go(ici(a, 2, from_x.at[pl.ds(0, h)], from_x.at[pl.ds(0, h)], yn))
            ici(a, 1, ins[a].at[c], from_y, yn).wait_recv()
            if relayed[a]:
                go(ici(a, 3, from_y.at[pl.ds(h, h)], from_y.at[pl.ds(h, h)], xn))
            for j, slab in enumerate((from_x, from_y)):
                go(_remote(slab, slab, d2d_send.at[3 * a + j], d2d_recv.at[3 * a + j], (x, y, 1 - c)))
        for a in range(n):
            h = arrs[a].shape[1] // 2
            from_d = outs[a].at[c, chip(dg)]
            if relayed[a]:
                ici(a, 2, from_d.at[pl.ds(0, h)], from_d.at[pl.ds(0, h)], yn).wait_recv()
                ici(a, 3, from_d.at[pl.ds(h, h)], from_d.at[pl.ds(h, h)], xn).wait_recv()
            else:
                ici(a, 2, ins[a].at[c], from_d, dg).wait_recv()
            go(_remote(from_d, from_d, d2d_send.at[3 * a + 2], d2d_recv.at[3 * a + 2], (x, y, 1 - c)))
        for a in range(n):
            for j, p in enumerate((xn, yn, dg)):
                slab = outs[a].at[1 - c, chip(p)]
                _remote(slab, slab, d2d_send.at[3 * a + j], d2d_recv.at[3 * a + j], (x, y, 1 - c)).wait_recv()
        for cp in sends:
            cp.wait_send()

    return _hbm_call("gather_weights", body, arrs,
                     [jax.ShapeDtypeStruct((N_LAYER, N_CHIP) + a.shape[1:], a.dtype) for a in arrs],
                     [4 * n, 4 * n, 3 * n, 3 * n])


def _pair_exchange(arrs):
    n = len(arrs)

    def body(*refs):
        ins, outs = refs[:n], refs[n:2 * n]
        send_sems, recv_sems = refs[2 * n:]
        x, y, c = _position()
        cps = [_remote(ins[a].at[1 - c], outs[a], send_sems.at[a], recv_sems.at[a], (x, y, 1 - c)) for a in range(n)]
        for cp in cps:
            cp.start()
        for cp in cps:
            cp.wait_recv()
        for cp in cps:
            cp.wait_send()

    return _hbm_call("pair_exchange", body, arrs, [jax.ShapeDtypeStruct(a.shape[1:], a.dtype) for a in arrs], [n, n])


def _chip_scatter(arrs):
    n = len(arrs)

    def body(*refs):
        ins, outs = refs[:n], refs[n:2 * n]
        send_sems, recv_sems = refs[2 * n:]
        x, y, c = _position()
        me = 2 * x + y
        sends = []
        for a in range(n):
            for j, (px, py) in enumerate(_other_chips(x, y)):
                sends.append(_remote(ins[a].at[2 * px + py], outs[a].at[me], send_sems.at[3 * a + j],
                                     recv_sems.at[3 * a + j], (px, py, c)))
                sends[-1].start()
        for a in range(n):
            for j, (px, py) in enumerate(_other_chips(x, y)):
                _remote(ins[a].at[me], outs[a].at[2 * px + py], send_sems.at[3 * a + j], recv_sems.at[3 * a + j],
                        (px, py, c)).wait_recv()
        for cp in sends:
            cp.wait_send()

    return _hbm_call("chip_scatter", body, arrs, [jax.ShapeDtypeStruct(a.shape, a.dtype) for a in arrs],
                     [3 * n, 3 * n])


def _pair_share(arrs):
    n = len(arrs)

    def body(*refs):
        ins, outs = refs[:n], refs[n:2 * n]
        send_sems, recv_sems = refs[2 * n:]
        x, y, c = _position()
        cps = [_remote(ins[a].at[c], outs[a].at[c], send_sems.at[a], recv_sems.at[a], (x, y, 1 - c)) for a in range(n)]
        for cp in cps:
            cp.start()
        for a in range(n):
            _remote(ins[a].at[c], outs[a].at[1 - c], send_sems.at[a], recv_sems.at[a], (x, y, 1 - c)).wait_recv()
        for cp in cps:
            cp.wait_send()

    return _hbm_call("pair_share", body, arrs, [jax.ShapeDtypeStruct(a.shape, a.dtype) for a in arrs], [n, n],
                     {a: a for a in range(n)})


def _pair_forward(arrs):
    n = len(arrs)

    def body(*refs):
        ins, outs = refs[:n], refs[n:2 * n]
        send_sems, recv_sems = refs[2 * n:]
        x, y, c = _position()
        sends = []
        for a in range(n):
            for j, (px, py) in enumerate(_other_chips(x, y)):
                sends.append(_remote(ins[a].at[c, 2 * px + py], outs[a].at[c, 2 * px + py], send_sems.at[3 * a + j],
                                     recv_sems.at[3 * a + j], (x, y, 1 - c)))
                sends[-1].start()
        for a in range(n):
            for j, (px, py) in enumerate(_other_chips(x, y)):
                slab = outs[a].at[1 - c, 2 * px + py]
                _remote(slab, slab, send_sems.at[3 * a + j], recv_sems.at[3 * a + j], (x, y, 1 - c)).wait_recv()
        for cp in sends:
            cp.wait_send()

    return _hbm_call("pair_forward", body, arrs, [jax.ShapeDtypeStruct(a.shape, a.dtype) for a in arrs],
                     [3 * n, 3 * n], {a: a for a in range(n)})


_HBM = pl.BlockSpec(memory_space=pltpu.HBM)
_SEM = pl.BlockSpec(memory_space=pltpu.SEMAPHORE)
_EFFECT = pltpu.SideEffectType.DATAFLOW_SIDE_EFFECTING


def _chip_copies(kind, srcs, lands, send_sems, recv_sems):
    x, y, c = _position()
    me = 2 * x + y
    sends, recvs = [], []
    for a in range(len(srcs)):
        for j, (px, py) in enumerate(_other_chips(x, y)):
            s, sems, dev = 2 * px + py, (send_sems.at[3 * a + j], recv_sems.at[3 * a + j]), (px, py, c)
            if kind == "gather":
                sends.append(_remote(srcs[a].at[c], lands[a].at[c, me], *sems, dev))
                recvs.append(_remote(srcs[a].at[c], lands[a].at[c, s], *sems, dev))
            else:
                sends.append(_remote(srcs[a].at[s], lands[a].at[me], *sems, dev))
                recvs.append(_remote(srcs[a].at[me], lands[a].at[s], *sems, dev))
    return sends, recvs


def _split_start(name, kind, srcs, land_shapes, after):
    n = len(srcs)

    def body(*refs):
        sends, _ = _chip_copies(kind, refs[:n], refs[n:2 * n], refs[2 * n + 1], refs[2 * n + 2])
        for cp in sends:
            cp.start()
        refs[-1][...] = jnp.zeros_like(refs[-1])

    hbm = lambda a: pltpu.with_memory_space_constraint(a, pltpu.HBM)
    lands = [lax.empty(s.shape, s.dtype) for s in land_shapes]
    outs = pl.pallas_call(
        body, name=name, in_specs=[_HBM] * (2 * n) + [_ANY],
        out_specs=[_SEM, _SEM] + [_HBM] * (2 * n) + [pl.BlockSpec(memory_space=pltpu.VMEM)],
        out_shape=[pltpu.SemaphoreType.DMA((3 * n,)), pltpu.SemaphoreType.DMA((3 * n,))]
        + [pltpu.HBM(a.shape, a.dtype) for a in list(srcs) + lands] + [jax.ShapeDtypeStruct((8, LANES), F32)],
        input_output_aliases={i: 2 + i for i in range(2 * n)},
        compiler_params=pltpu.CompilerParams(has_side_effects=_EFFECT),
    )(*[hbm(a) for a in srcs], *[hbm(a) for a in lands], after)
    return outs[0], outs[1], list(outs[2:2 + 2 * n]), outs[-1]


def _split_wait(name, kind, started, after):
    send_sems, recv_sems, thru, _ = started
    n = len(thru) // 2

    def body(*refs):
        sends, recvs = _chip_copies(kind, refs[:n], refs[n:2 * n], refs[2 * n], refs[2 * n + 1])
        for cp in sends:
            cp.wait_send()
        for cp in recvs:
            cp.wait_recv()

    outs = pl.pallas_call(
        body, name=name, in_specs=[_HBM] * (2 * n) + [_SEM, _SEM, _ANY], out_specs=[_HBM] * (2 * n),
        out_shape=[pltpu.HBM(a.shape, a.dtype) for a in thru], input_output_aliases={i: i for i in range(2 * n)},
        compiler_params=pltpu.CompilerParams(has_side_effects=_EFFECT),
    )(*thru, send_sems, recv_sems, after)
    return list(outs[:n]), list(outs[n:])


def _allreduce_small(g):
    def body(g_ref, o_ref, pair_buf, chip_buf, send_sems, recv_sems):
        x, y, c = _position()
        me = 2 * x + y
        sib = (x, y, 1 - c)
        to_sib = _remote(g_ref.at[1 - c], pair_buf, send_sems.at[0], recv_sems.at[0], sib)
        to_sib.start()
        to_sib.wait_recv()
        chip_buf[me] = g_ref[c] + pair_buf[...]
        sends = [to_sib]
        chips = _other_chips(x, y)
        for j, (px, py) in enumerate(chips):
            sends.append(_remote(chip_buf.at[me], chip_buf.at[me], send_sems.at[1 + j], recv_sems.at[1 + j], (px, py, c)))
            sends[-1].start()
        for j, (px, py) in enumerate(chips):
            _remote(chip_buf.at[me], chip_buf.at[2 * px + py], send_sems.at[1 + j], recv_sems.at[1 + j],
                    (px, py, c)).wait_recv()
        o_ref[c] = ((chip_buf[0] + chip_buf[1]) + chip_buf[2]) + chip_buf[3]
        sends.append(_remote(o_ref.at[c], o_ref.at[c], send_sems.at[4], recv_sems.at[4], sib))
        sends[-1].start()
        _remote(o_ref.at[c], o_ref.at[1 - c], send_sems.at[4], recv_sems.at[4], sib).wait_recv()
        for cp in sends:
            cp.wait_send()

    vmem = pl.BlockSpec(memory_space=pltpu.VMEM)
    return pl.pallas_call(
        body, name="allreduce_small", in_specs=[vmem], out_specs=vmem, out_shape=jax.ShapeDtypeStruct(g.shape, F32),
        scratch_shapes=[pltpu.VMEM(g.shape[1:], F32), pltpu.VMEM((N_CHIP,) + g.shape[1:], F32),
                        pltpu.SemaphoreType.DMA((5,)), pltpu.SemaphoreType.DMA((5,))],
        compiler_params=_params(),
    )(g)


EW_ROWS = 512


def _ew(name, fn, ins, n_out, out_dtype=F32, out_slot=None, into=None):
    def dims(a):
        return a[0].shape[1:] if isinstance(a, tuple) else a.shape

    R, w = dims(ins[0])
    tr = EW_ROWS if R % EW_ROWS == 0 else R
    n_into = len(into) if into else 0

    def body(c_ref, *refs):
        outs = fn(*[r[...] for r in refs[:len(ins)]])
        for r, v in zip(refs[len(ins) + n_into:], outs):
            r[...] = v.astype(r.dtype)

    def lead_spec(l):
        if l == "c":
            return pl.BlockSpec((None, tr, w), lambda i, c_ref: (c_ref[0], i, 0))
        return pl.BlockSpec((None, tr, w), lambda i, c_ref, s=l: (s, i, 0))

    plain = pl.BlockSpec((tr, w), lambda i, c_ref: (i, 0))
    in_specs = [lead_spec(a[1]) if isinstance(a, tuple) else plain for a in ins] + [_ANY] * n_into
    out_spec = plain if out_slot is None else lead_spec(out_slot)
    out_shape = jax.ShapeDtypeStruct((R, w) if out_slot is None else (2, R, w), out_dtype)
    return pl.pallas_call(
        body, name=name,
        grid_spec=pltpu.PrefetchScalarGridSpec(num_scalar_prefetch=1, grid=(R // tr,), in_specs=in_specs,
                                               out_specs=[out_spec] * n_out),
        out_shape=[out_shape] * n_out, input_output_aliases={1 + len(ins) + j: j for j in range(n_into)},
        compiler_params=_params(dimension_semantics=("arbitrary",)),
    )(lax.axis_index("c").astype(jnp.int32).reshape(1), *[a[0] if isinstance(a, tuple) else a for a in ins],
      *(into or []))


def _adamw_fn(w, g, m, v):
    m = ADAM_B1 * m + (1.0 - ADAM_B1) * g
    v = ADAM_B2 * v + (1.0 - ADAM_B2) * (g * g)
    m_hat = m / (1.0 - ADAM_B1 ** ADAM_STEP)
    v_hat = v / (1.0 - ADAM_B2 ** ADAM_STEP)
    delta = -ADAM_LR * (m_hat / (jnp.sqrt(v_hat) + ADAM_EPS) + ADAM_WD * w)
    return delta, m, v


def _adamw(name, w, g, m, v):
    shape = w.shape
    two = lambda a: a.reshape(-1, shape[-1])
    return [o.reshape(shape) for o in _ew(name, _adamw_fn, [two(w), two(g), two(m), two(v)], 3)]


def _adamw_layer(name, l, w, g, m, v, into):
    k = w.shape[-1]
    three = lambda a: (a.reshape(N_LAYER, -1, k), l)
    fn = lambda w_, g_, m_, v_: _adamw_fn(w_, g_, m_, v_) + (g_,)
    outs = _ew(name, fn, [three(w), g.reshape(-1, k), three(m), three(v)], 4, out_slot=l,
               into=None if into is None else [a.reshape(N_LAYER, -1, k) for a in into])
    return [o.reshape(w.shape) for o in outs]


def _adamw_rows(name, l, w, g, m, v, into):
    _, R, k = w.shape
    n_into = len(into) if into else 0

    def body(*refs):
        w_ref, g_ref, m_ref, v_ref = refs[:4]
        d_out, m_out, v_out, g_out = refs[4 + n_into:]
        g_blk = g_ref[...]
        d_out[...], m_out[...], v_out[...] = _adamw_fn(w_ref[...], g_blk, m_ref[...], v_ref[...])
        g_out[...] = g_blk

    spec = pl.BlockSpec((None, EW_ROWS, k), lambda i: (l, i, 0))
    return pl.pallas_call(
        body, name=name, grid=(-(-R // EW_ROWS),),
        in_specs=[spec, pl.BlockSpec((EW_ROWS, k), lambda i: (i, 0)), spec, spec] + [_ANY] * n_into,
        out_specs=[spec] * 4, out_shape=[jax.ShapeDtypeStruct((N_LAYER, R, k), F32)] * 4,
        input_output_aliases={4 + j: j for j in range(n_into)},
        compiler_params=_params(dimension_semantics=("arbitrary",)),
    )(w, g, m, v, *(into or []))


_SMALL = [("norm_pre", (2, 1024)), ("norm_post", (2, 1024)), ("norm_mem", (2, 1024)), ("a_log", (2, 4)),
          ("dt_bias", (2, 4)), ("dn_norm", (2, 128)), ("gm_norm", (2, 512)), ("spatial_w", (2, 4, 128, 128)),
          ("spatial_b", (2, 4, 128)), ("sinks", (2, 8))]
_SMALL_ROWS = 200
_BIG = ["w_in", "conv_w", "w_mem_kv", "w_up", "w_out"]
_NAMES = ["norm_pre", "norm_post", "norm_mem", "w_in", "conv_w", "a_log", "dt_bias", "dn_norm", "gm_norm",
          "spatial_w", "spatial_b", "sinks", "w_mem_kv", "w_up", "w_out"]


def _size(shape):
    n = 1
    for s in shape:
        n *= s
    return n


def _pack_small(d):
    rows = []
    for n, shp in _SMALL:
        a = d[n].reshape(N_LAYER, -1)
        rows.append(a.reshape(-1, 1024) if a.shape[1] > 1024 else jnp.pad(a, ((0, 6), (0, 1024 - a.shape[1]))))
    assert sum(r.shape[0] for r in rows) == _SMALL_ROWS
    return jnp.concatenate(rows, axis=0)


def _unpack_small(p):
    out, off = {}, 0
    for n, shp in _SMALL:
        c = _size(shp) // N_LAYER
        k = 8 if c <= 1024 else _size(shp) // 1024
        out[n] = (p[off:off + N_LAYER, :c] if c <= 1024 else p[off:off + k]).reshape(shp)
        off += k
    return out


_HALF_SHAPE = {"w_in": (SHARD_PAD // 2, D_MODEL), "conv_w": (2, 3 * BRANCH_W // N_CHIP), "w_mem_kv": (128, D_MODEL),
               "w_up": (2, BRANCH_W, D_MODEL // N_CHIP), "w_out": (128, D_MODEL)}


def _chip_major(g):
    g = jnp.swapaxes(g, 0, 1)
    return g.reshape((N_CHIP, 2 * g.shape[2]) + g.shape[3:])


def _half_major(g):
    g = g.reshape((N_CHIP, 2, g.shape[1] // 2) + g.shape[2:])
    return jnp.swapaxes(g, 0, 1).astype(BF16)


N_EARLY = 2


def _early_views(l, g_in, g_conv, small):
    return _layer_weights(l, _w_pad_from_slabs(g_in),
                          _chip_major(g_conv).transpose(1, 0, 2).reshape(4, 3 * BRANCH_W), small)


def _late_views(g_kv, g_up, g_out):
    return dict(w_mem_kv=_chip_major(g_kv).reshape(D_MODEL, D_MODEL),
                w_up=_chip_major(g_up).transpose(1, 2, 0, 3).reshape(4, BRANCH_W, D_MODEL),
                w_out=_chip_major(g_out).reshape(D_MODEL, D_MODEL))


def _pair_sums(g):
    big = [_slabs_from_pad(g["w_pad"]),
           _half_major(g["conv"].reshape(4, N_CHIP, 3 * BRANCH_W // N_CHIP).transpose(1, 0, 2)),
           _half_major(g["w_mem_kv"].reshape(N_CHIP, D_MODEL // N_CHIP, D_MODEL)),
           _half_major(g["w_up"].reshape(4, BRANCH_W, N_CHIP, D_MODEL // N_CHIP).transpose(2, 0, 1, 3)),
           _half_major(g["w_out"].reshape(N_CHIP, D_MODEL // N_CHIP, D_MODEL))]
    add2 = lambda a, b: [a.astype(F32) + b.astype(F32)]
    pair = []
    for n, b, p in zip(_BIG, big, _pair_exchange(big)):
        k = b.shape[-1]
        pair.append(_ew("pair_sum_" + n, add2, [(b.reshape(2, -1, k), "c"), p.reshape(-1, k)], 1, BF16)[0]
                    .reshape(p.shape))
    return pair


def _chip_sums(landed, pair, me):
    add4 = lambda a, b, c_, d: [((a.astype(F32) + b.astype(F32)) + c_.astype(F32)) + d.astype(F32)]
    totals = []
    for n, r, q in zip(_BIG, landed, pair):
        r = _own_slot(r, lax.dynamic_index_in_dim(q, me, 0), me, 0)
        k = r.shape[-1]
        totals.append(_ew("chip_sum_" + n, add4, [(r.reshape(N_CHIP, -1, k), s) for s in range(N_CHIP)], 1,
                          out_slot="c")[0].reshape((2,) + r.shape[1:]))
    return totals


def _own_slot(buf, mine, me, axis):
    return lax.dynamic_update_index_in_dim(buf, mine.astype(buf.dtype), me, axis)


def kernel(x, mem, norm_pre, norm_post, norm_mem, w_in, conv_w, a_log, dt_bias, dn_norm, gm_norm, spatial_w, spatial_b, sinks, w_mem_kv, w_up, w_out, loss_target, m_norm_pre, m_norm_post, m_norm_mem, m_w_in, m_conv_w, m_a_log, m_dt_bias, m_dn_norm, m_gm_norm, m_spatial_w, m_spatial_b, m_sinks, m_w_mem_kv, m_w_up, m_w_out, v_norm_pre, v_norm_post, v_norm_mem, v_w_in, v_conv_w, v_a_log, v_dt_bias, v_dn_norm, v_gm_norm, v_spatial_w, v_spatial_b, v_sinks, v_w_mem_kv, v_w_up, v_w_out):
    w = dict(norm_pre=norm_pre, norm_post=norm_post, norm_mem=norm_mem, w_in=w_in, conv_w=conv_w, a_log=a_log,
             dt_bias=dt_bias, dn_norm=dn_norm, gm_norm=gm_norm, spatial_w=spatial_w, spatial_b=spatial_b, sinks=sinks,
             w_mem_kv=w_mem_kv, w_up=w_up, w_out=w_out)
    m = dict(norm_pre=m_norm_pre, norm_post=m_norm_post, norm_mem=m_norm_mem, w_in=m_w_in, conv_w=m_conv_w,
             a_log=m_a_log, dt_bias=m_dt_bias, dn_norm=m_dn_norm, gm_norm=m_gm_norm, spatial_w=m_spatial_w,
             spatial_b=m_spatial_b, sinks=m_sinks, w_mem_kv=m_w_mem_kv, w_up=m_w_up, w_out=m_w_out)
    v = dict(norm_pre=v_norm_pre, norm_post=v_norm_post, norm_mem=v_norm_mem, w_in=v_w_in, conv_w=v_conv_w,
             a_log=v_a_log, dt_bias=v_dt_bias, dn_norm=v_dn_norm, gm_norm=v_gm_norm, spatial_w=v_spatial_w,
             spatial_b=v_spatial_b, sinks=v_sinks, w_mem_kv=v_w_mem_kv, w_up=v_w_up, w_out=v_w_out)
    me = 2 * lax.axis_index("x") + lax.axis_index("y")

    w_in_t = jnp.pad(w_in.astype(BF16).transpose(0, 2, 1), ((0, 0), (0, SHARD_PAD - SHARD_IN), (0, 0)))
    local = dict(w_in=w_in_t, conv_w=conv_w, w_mem_kv=w_mem_kv.astype(BF16), w_up=w_up.astype(BF16),
                 w_out=w_out.astype(BF16))
    halves = lambda l: [local[n][l].reshape((2,) + _HALF_SHAPE[n]) for n in _BIG]
    own = lambda gathered, mine: [_own_slot(g, h[:, None], me, 1) for g, h in zip(gathered, mine)]
    lands = [jax.ShapeDtypeStruct((2, N_CHIP) + _HALF_SHAPE[n], local[n].dtype) for n in _BIG]
    h0 = halves(0)
    g0 = own(_gather_weights(h0[:N_EARLY], [True, False]), h0[:N_EARLY])
    rest0 = _split_start("gather_l0_rest_start", "gather", h0[N_EARLY:], lands[N_EARLY:], g0[1])
    started = _split_start("gather_l1_start", "gather", halves(1), lands, rest0[3])

    xl, meml = x[0], mem[0]
    W0 = _early_views(0, g0[0], g0[1], w)
    W0["norm_pre"] = W0["norm_pre"] + started[3][0, 0]

    def late0(cols):
        mine, landed = _split_wait("gather_l0_rest_wait", "gather", rest0, cols)
        return _late_views(*own(_pair_forward(landed), mine))

    x1, sv0, W0 = _layer_fwd(xl, meml, W0, late0)
    mine1, landed1 = _split_wait("gather_l1_wait", "gather", started, x1)
    g1 = own(_pair_forward(landed1), mine1)
    W1 = dict(_early_views(1, g1[0], g1[1], w), **_late_views(*g1[N_EARLY:]))
    x2, sv1, _ = _layer_fwd(x1, meml, W1)
    dy, lrows = _rows_fwd("loss", _loss_fn, [Row(x2, D_MODEL, 0), Row(loss_target[0], D_MODEL, 0)], [],
                          [(D_MODEL, F32), (LANES, F32)], BIG_TR)
    loss = lax.psum(jnp.sum(lrows[:, 0]), ("x", "y", "c"))

    scattering = {}

    def start_scatter(l):
        def on_weight_grads(g):
            pair = _pair_sums(g)
            scattering[l] = _split_start("scatter_l%d_start" % l, "scatter", pair,
                                         [jax.ShapeDtypeStruct(p.shape, p.dtype) for p in pair], pair[1])
            return scattering[l][3]
        return on_weight_grads

    dx1, grads1 = _layer_bwd(dy, meml, W1, sv1, start_scatter(1))
    dx, grads0 = _layer_bwd(dx1, meml, W0, sv0, start_scatter(0))
    pair1, landed1 = _split_wait("scatter_l1_wait", "scatter", scattering[1], dx)
    after_start = scattering[0][3][0, 0]
    grads = [grads0, grads1]

    small_local = dict(
        norm_pre=jnp.stack([g["norm_pre"][0] for g in grads]), norm_post=jnp.stack([g["norm_post"][0] for g in grads]),
        norm_mem=jnp.stack([g["norm_mem"][0] for g in grads]), a_log=jnp.stack([g["a_vec"][0, 4:8] for g in grads]),
        dt_bias=jnp.stack([g["dt_vec"][0, 4:8] for g in grads]), dn_norm=jnp.stack([g["dn_norm"][0] for g in grads]),
        gm_norm=jnp.stack([g["gm_norm"][0] for g in grads]), spatial_w=jnp.stack([g["spatial_w"] for g in grads]),
        spatial_b=jnp.stack([g["spatial_b"][:, 0, :] for g in grads]),
        sinks=jnp.stack([g["sink_vec"][0, :8] for g in grads]))
    packed = jnp.pad(_pack_small(small_local) + after_start, ((0, 8), (0, 0)))
    gsmall_packed = _allreduce_small(packed.reshape(2, -1, 1024)).reshape(-1, 1024)[:_SMALL_ROWS]

    d_s, m_s, v_s = _ew("adamw_small", _adamw_fn, [_pack_small(w), gsmall_packed, _pack_small(m), _pack_small(v)], 3)
    gsmall, dsmall, msmall, vsmall = (_unpack_small(p) for p in (gsmall_packed, d_s, m_s, v_s))
    g_o, d_o, m_o, v_o = dict(gsmall), dict(dsmall), dict(msmall), dict(vsmall)
    tr = lambda a: a.transpose(0, 2, 1)
    w_t, m_t, v_t = tr(w["w_in"]), tr(m["w_in"]), tr(v["w_in"])

    def update(l, totals, into):
        outs = {}
        for n, t in zip(_BIG, totals):
            g_l = t.reshape(local[n].shape[1:])
            if n == "w_in":
                outs[n] = _adamw_rows("adamw_" + n, l, w_t, g_l, m_t, v_t, into and into[n])
            else:
                outs[n] = _adamw_layer("adamw_" + n, l, w[n], g_l, m[n], v[n], into and into[n])
        return outs

    landed1[1] = landed1[1] + after_start.astype(landed1[1].dtype)
    outs1 = update(1, _pair_share(_chip_sums(landed1, pair1, me)), None)
    pair0, landed0 = _split_wait("scatter_l0_wait", "scatter", scattering[0], outs1["w_in"][0])
    outs = update(0, _pair_share(_chip_sums(landed0, pair0, me)), outs1)
    for n in _BIG:
        d_o[n], m_o[n], v_o[n], g_o[n] = [tr(o) for o in outs[n]] if n == "w_in" else outs[n]
    return (loss, dx[None], *[g_o[n] for n in _NAMES], *[d_o[n] for n in _NAMES], *[m_o[n] for n in _NAMES],
            *[v_o[n] for n in _NAMES])
```

```python
import collections
import functools

import jax
import jax.numpy as jnp
from jax import lax
from jax.experimental import pallas as pl
from jax.experimental.pallas import tpu as pltpu

F32 = jnp.float32
BF16 = jnp.bfloat16

D_MODEL = 1024
BRANCH_W = 512
MEM_LEN = 256
N_LAYER = 2
N_CHIP = 4
N_DEV = 8
EPS = 1e-6
NEG_INF = -1e30
DN_CHUNK = 64
LANES = 128
VMEM_LIMIT = 48 * 1024 * 1024

ADAM_LR, ADAM_B1, ADAM_B2, ADAM_EPS, ADAM_WD, ADAM_STEP = 0.001, 0.9, 0.999, 1e-08, 0.01, 10

N_PAD = 10240
O_GATE = 0
O_AQ, O_AK, O_AV, O_AZ = 4096, 4608, 5120, 5632
O_BUV, O_BZ = 6144, 7168
O_CKV, O_BA = 7680, 7936
O_CQ, O_CZ = 8192, 8704
O_MQ, O_MZ = 9216, 9728
O_MISC, W_MISC = O_CKV, 512
_PAD_SEGS = [(5896, 4096), (0, 512), (512, 512), (1024, 512), (1536, 512), (2056, 1024), (3080, 512),
             (4104, 128), (4232, 128), (2048, 8), (None, 120), (None, 128),
             (3592, 512), (4360, 512), (4872, 512), (5384, 512)]
D_IN = 9992
SHARD_IN = D_IN // N_CHIP


SHARD_PAD = 2560


def _pad_parts():
    parts, off = [], 0
    for s, n in _PAD_SEGS:
        a = s
        while s is not None and a < s + n:
            chip = a // SHARD_IN
            b = min(s + n, (chip + 1) * SHARD_IN)
            parts.append((chip, a - chip * SHARD_IN, off + a - s, b - a))
            a = b
        off += n
    return parts


PERM_ROWS = 512
PERM_SLACK = 32


def _permute_rows(name, src, parts, n_out, out_dtype):
    B, Z = PERM_ROWS, PERM_ROWS + PERM_SLACK
    w = src.shape[1]
    plans = []
    for blk in range(n_out // B):
        o, runs = blk * B, []
        for s, d, n in parts:
            lo, hi = max(d, o), min(d + n, o + B)
            if lo < hi:
                s0 = s + lo - d
                wa = s0 // 16 * 16
                wb = min(-(-(s0 + hi - lo) // 16) * 16, src.shape[0])
                runs.append((wa, wb - wa, s0 - (lo - o) - wa, lo - o, hi - o))
        plans.append(runs)
    max_runs = max(len(r) for r in plans)
    nblk = len(plans)

    def body(*refs):
        src_ref, out_ref, inbuf, obuf, insem, outsem = (refs[0],) + refs[-5:]

        def in_copies(blk):
            return [pltpu.make_async_copy(src_ref.at[pl.ds(wa, ws)], inbuf.at[blk % 2, r, pl.ds(0, ws)],
                                          insem.at[blk % 2, r]) for r, (wa, ws, _, _, _) in enumerate(plans[blk])]

        def out_copy(blk):
            return pltpu.make_async_copy(obuf.at[blk % 2], out_ref.at[pl.ds(blk * B, B)], outsem.at[blk % 2])

        for cp in in_copies(0):
            cp.start()
        rid = _iota((B, 1), 0)
        for blk in range(nblk):
            if blk + 1 < nblk:
                for cp in in_copies(blk + 1):
                    cp.start()
            for cp in in_copies(blk):
                cp.wait()
            val = jnp.zeros((B, w), F32)
            for r, (wa, ws, t, l0, l1) in enumerate(plans[blk]):
                win = jnp.concatenate([inbuf[blk % 2, r, pl.ds(0, ws)].astype(F32), jnp.zeros((Z - ws, w), F32)], axis=0)
                moved = pltpu.roll(win, (-t) % Z, 0)[:B]
                val = jnp.where((rid >= l0) & (rid < l1), moved, val)
            if blk >= 2:
                out_copy(blk - 2).wait()
            obuf[blk % 2] = val.astype(out_dtype)
            out_copy(blk).start()
        for blk in range(max(nblk - 2, 0), nblk):
            out_copy(blk).wait()

    return pl.pallas_call(
        body, name=name, in_specs=[_ANY], out_specs=_ANY, out_shape=jax.ShapeDtypeStruct((n_out, w), out_dtype),
        scratch_shapes=[pltpu.VMEM((2, max_runs, Z, w), src.dtype), pltpu.VMEM((2, B, w), out_dtype),
                        pltpu.SemaphoreType.DMA((2, max_runs)), pltpu.SemaphoreType.DMA((2,))],
        compiler_params=_params(),
    )(src)


def _slab_parts():
    h, out = SHARD_PAD // 2, []
    for chip, s, d, n in _pad_parts():
        a = s
        while a < s + n:
            half = a // h
            b = min(s + n, (half + 1) * h)
            out.append(((half * N_CHIP + chip) * h + a - half * h, d + a - s, b - a))
            a = b
    return out


def _w_pad_from_slabs(slabs):
    return _permute_rows("w_pad_rows", slabs.reshape(-1, slabs.shape[-1]), _slab_parts(), N_PAD, BF16)


def _slabs_from_pad(dw):
    slabs = _permute_rows("w_pad_grad_rows", dw, [(d, s, n) for s, d, n in _slab_parts()], N_CHIP * SHARD_PAD, BF16)
    return slabs.reshape(2, N_CHIP, SHARD_PAD // 2, dw.shape[1])


def _dot(a, b, dims, prec):
    if prec == "bf16":
        return lax.dot_general(a.astype(BF16), b.astype(BF16), (dims, ((), ())), preferred_element_type=F32)
    return lax.dot_general(a, b, (dims, ((), ())), precision=lax.Precision.HIGHEST, preferred_element_type=F32)


_NN, _NT, _TN = ((1,), (0,)), ((1,), (1,)), ((0,), (0,))


def _make_mm(prec):
    @jax.custom_vjp
    def nn(a, b):
        return _dot(a, b, _NN, prec)

    @jax.custom_vjp
    def nt(a, b):
        return _dot(a, b, _NT, prec)

    @jax.custom_vjp
    def tn(a, b):
        return _dot(a, b, _TN, prec)

    nn.defvjp(lambda a, b: (nn(a, b), (a, b)), lambda r, g: (nt(g, r[1]), tn(r[0], g)))
    nt.defvjp(lambda a, b: (nt(a, b), (a, b)), lambda r, g: (nn(g, r[1]), tn(g, r[0])))
    tn.defvjp(lambda a, b: (tn(a, b), (a, b)), lambda r, g: (nt(r[1], g), nn(r[0], g)))
    return nn, nt, tn


_nn16, _nt16, _tn16 = _make_mm("bf16")
_nn32, _nt32, _tn32 = _make_mm("f32")


def _make_slice(axis):
    @functools.partial(jax.custom_vjp, nondiff_argnums=(1, 2, 3))
    def sl(x, a, b, n):
        return x[a:b] if axis == 0 else x[:, a:b]

    def fwd(x, a, b, n):
        return sl(x, a, b, n), None

    def bwd(a, b, n, _, g):
        parts = []
        if a > 0:
            parts.append(jnp.zeros((a, g.shape[1]) if axis == 0 else (g.shape[0], a), g.dtype))
        parts.append(g)
        if n - b > 0:
            parts.append(jnp.zeros((n - b, g.shape[1]) if axis == 0 else (g.shape[0], n - b), g.dtype))
        return (jnp.concatenate(parts, axis=axis),)

    sl.defvjp(fwd, bwd)
    return sl


_sl0, _sl1 = _make_slice(0), _make_slice(1)


def _rowsl(x, a, b):
    return _sl0(x, a, b, x.shape[0])


def _cols(x, a, b):
    return _sl1(x, a, b, x.shape[1])


@functools.partial(jax.custom_vjp, nondiff_argnums=(1,))
def _rollr(x, s):
    return pltpu.roll(x, s, 0)


_rollr.defvjp(lambda x, s: (_rollr(x, s), None),
              lambda s, _, g: (pltpu.roll(g, g.shape[0] - s, 0),))


def _iota(shape, axis):
    return lax.broadcasted_iota(jnp.int32, shape, axis)


def _sigmoid(x):
    return lax.logistic(x)


def _silu(x):
    return x * _sigmoid(x)


def _gelu(x):
    return 0.5 * x * (1.0 + jnp.tanh(0.7978845608028654 * (x + 0.044715 * (x * x * x))))


def _softplus(x):
    return jnp.maximum(x, 0.0) + jnp.log(1.0 + jnp.exp(-jnp.abs(x)))


def _rms(x, g):
    return x * lax.rsqrt(jnp.mean(x * x, axis=-1, keepdims=True) + EPS) * g


def _lane_pick(x, lane):
    return jnp.sum(x * (_iota((1, x.shape[1]), 1) == lane).astype(F32), axis=1, keepdims=True)


Row = collections.namedtuple("Row", "arr w cb hb grad", defaults=(0, True))


def _full_spec(shape):
    return pl.BlockSpec(shape, lambda i, _n=len(shape): (0,) * _n)


def _load_params(refs):
    return [[p[g].astype(F32) for g in range(p.shape[0])] if len(p.shape) == 3 else p[...].astype(F32)
            for p in refs]


def _params(**kw):
    return pltpu.CompilerParams(vmem_limit_bytes=VMEM_LIMIT, **kw)


def _rows_fwd(name, fn, rows, params, outs, tr, carry=None):
    T = rows[0].arr.shape[0]
    n = T // tr
    halos = [r for r in rows if r.hb]
    nr, nh, npar, no = len(rows), len(halos), len(params), len(outs)

    def body(*refs):
        row_refs, halo_refs = refs[:nr], refs[nr:nr + nh]
        par_refs = refs[nr + nh:nr + nh + npar]
        out_refs = refs[nr + nh + npar:nr + nh + npar + no]
        rest = refs[nr + nh + npar + no:]
        first = pl.program_id(0) == 0
        cvals = None
        if carry is not None:
            csave_ref, carry_ref = rest

            @pl.when(first)
            def _():
                carry_ref[...] = jnp.zeros_like(carry_ref)

            cvals = [carry_ref[g] for g in range(carry[0])]
            for g in range(carry[0]):
                csave_ref[0, g] = cvals[g]
        c_out, o = fn(first, cvals, [r[...].astype(F32) for r in row_refs],
                      [h[...].astype(F32) for h in halo_refs], _load_params(par_refs))
        for r, v in zip(out_refs, o):
            r[...] = v.astype(r.dtype)
        if carry is not None:
            for g in range(carry[0]):
                carry_ref[g] = c_out[g]

    in_specs = [pl.BlockSpec((tr, r.w), lambda i, c=r.cb: (i, c)) for r in rows]
    in_specs += [pl.BlockSpec((r.hb, r.w), lambda i, c=r.cb, q=tr // r.hb: (jnp.maximum(i * q - 1, 0), c))
                 for r in halos]
    in_specs += [_full_spec(p.shape) for p in params]
    out_shape = [jax.ShapeDtypeStruct((T, w), dt) for w, dt in outs]
    out_specs = [pl.BlockSpec((tr, w), lambda i: (i, 0)) for w, _ in outs]
    scratch = []
    if carry is not None:
        out_shape.append(jax.ShapeDtypeStruct((n,) + carry, F32))
        out_specs.append(pl.BlockSpec((1,) + carry, lambda i: (i, 0, 0, 0)))
        scratch.append(pltpu.VMEM(carry, F32))
    return pl.pallas_call(
        body, name=name, grid=(n,), in_specs=in_specs, out_specs=out_specs, out_shape=out_shape,
        scratch_shapes=scratch, compiler_params=_params(dimension_semantics=("arbitrary",)),
    )(*[r.arr for r in rows], *[r.arr for r in halos], *params)


def _rows_bwd(name, fn, rows, params, douts, tr, carry=None, csave=None, dcols=None):
    T = rows[0].arr.shape[0]
    n = T // tr
    halos = [r for r in rows if r.hb]
    grows = [r for r in rows if r.grad is True]
    crows = [r for r in rows if r.grad == "cols"]
    wcols = sum(r.w for r in crows)
    nr, nh, npar, nd, ng = len(rows), len(halos), len(params), len(douts), len(grows)
    nc = 0 if carry is None else 1
    ncol = 1 if crows else 0
    nalias = 1 if (crows and dcols is not None) else 0

    def body(*refs):
        row_refs, halo_refs = refs[:nr], refs[nr:nr + nh]
        par_refs = refs[nr + nh:nr + nh + npar]
        k = nr + nh + npar
        csave_ref = refs[k] if nc else None
        dout_refs = refs[k + nc:k + nc + nd]
        k = k + nc + nd + nalias
        drow_refs = refs[k:k + ng]
        dcols_ref = refs[k + ng] if ncol else None
        dpar_refs = refs[k + ng + ncol:k + ng + ncol + npar]
        k = k + ng + ncol + npar
        dcarry_ref = refs[k] if nc else None
        hgrad_refs = refs[k + nc:]
        i = pl.program_id(0)
        first_tile = i == n - 1

        @pl.when(i == 0)
        def _():
            for r in dpar_refs:
                r[...] = jnp.zeros_like(r)
            for r in hgrad_refs:
                r[...] = jnp.zeros_like(r)
            if nc:
                dcarry_ref[...] = jnp.zeros_like(dcarry_ref)

        rv = [r[...].astype(F32) for r in row_refs]
        hv = [h[...].astype(F32) for h in halo_refs]
        pv = _load_params(par_refs)
        dov = [d[...].astype(F32) for d in dout_refs]
        if nc:
            cv = [csave_ref[0, g] for g in range(carry[0])]
            _, vjp = jax.vjp(lambda c, r, h, p: fn(first_tile, c, r, h, p), cv, rv, hv, pv)
            dc, dr, dh, dp = vjp(([dcarry_ref[g] for g in range(carry[0])], dov))
            for g in range(carry[0]):
                dcarry_ref[g] = dc[g]
        else:
            _, vjp = jax.vjp(lambda r, h, p: fn(first_tile, None, r, h, p)[1], rv, hv, pv)
            dr, dh, dp = vjp(dov)
        gi = hi = 0
        pieces = []
        for kk, r in enumerate(rows):
            d = dr[kk]
            if r.hb:
                carried = hgrad_refs[hi][...]
                d = d + (carried if tr == r.hb else
                         jnp.concatenate([jnp.zeros((tr - r.hb, r.w), F32), carried], axis=0))
                hgrad_refs[hi][...] = dh[hi]
                hi += 1
            if r.grad is True:
                drow_refs[gi][...] = d.astype(drow_refs[gi].dtype)
                gi += 1
            elif r.grad == "cols":
                pieces.append(d.astype(BF16))
        if ncol:
            dcols_ref[...] = pieces[0] if len(pieces) == 1 else jnp.concatenate(pieces, axis=1)
        for r, d in zip(dpar_refs, dp):
            if len(r.shape) == 3:
                for g in range(r.shape[0]):
                    r[g] += d[g]
            else:
                r[...] += d

    rev = lambda i: n - 1 - i
    in_specs = [pl.BlockSpec((tr, r.w), lambda i, c=r.cb: (rev(i), c)) for r in rows]
    in_specs += [pl.BlockSpec((r.hb, r.w), lambda i, c=r.cb, q=tr // r.hb: (jnp.maximum(rev(i) * q - 1, 0), c))
                 for r in halos]
    in_specs += [_full_spec(p.shape) for p in params]
    args = [r.arr for r in rows] + [r.arr for r in halos] + list(params)
    scratch = []
    if nc:
        in_specs.append(pl.BlockSpec((1,) + carry, lambda i: (rev(i), 0, 0, 0)))
        args.append(csave)
        scratch.append(pltpu.VMEM(carry, F32))
    douts = [d if isinstance(d, Row) else Row(d, d.shape[1], 0) for d in douts]
    in_specs += [pl.BlockSpec((tr, d.w), lambda i, c=d.cb: (rev(i), c)) for d in douts]
    args += [d.arr for d in douts]
    aliases = {}
    if nalias:
        aliases = {len(args): ng}
        in_specs.append(pl.BlockSpec(memory_space=pl.ANY))
        args.append(dcols)
    scratch += [pltpu.VMEM((r.hb, r.w), F32) for r in halos]
    out_shape = [jax.ShapeDtypeStruct((T, r.w), F32) for r in grows]
    out_specs = [pl.BlockSpec((tr, r.w), lambda i: (rev(i), 0)) for r in grows]
    if ncol:
        off = crows[0].cb * crows[0].w
        assert off % wcols == 0 and all(a.cb * a.w + a.w == b.cb * b.w for a, b in zip(crows, crows[1:]))
        out_shape.append(jax.ShapeDtypeStruct((T, N_PAD), BF16))
        out_specs.append(pl.BlockSpec((tr, wcols), lambda i, c=off // wcols: (rev(i), c)))
    out_shape += [jax.ShapeDtypeStruct(p.shape, F32) for p in params]
    out_specs += [_full_spec(p.shape) for p in params]
    res = pl.pallas_call(
        body, name=name, grid=(n,), in_specs=in_specs, out_specs=out_specs, out_shape=out_shape,
        scratch_shapes=scratch, input_output_aliases=aliases,
        compiler_params=_params(dimension_semantics=("arbitrary",)),
    )(*args)
    return list(res[:ng]), list(res[ng + ncol:]), (res[ng] if ncol else dcols)


def _fill_misc(dcols, dkv, dba, tr):
    T = dkv.shape[0]

    def body(kv_ref, ba_ref, _, o_ref):
        o_ref[...] = jnp.concatenate([kv_ref[...], ba_ref[...]], axis=1).astype(BF16)

    return pl.pallas_call(
        body, name="misc_bwd", grid=(T // tr,),
        in_specs=[pl.BlockSpec((tr, 256), lambda i: (i, 0)), pl.BlockSpec((tr, 256), lambda i: (i, 0)),
                  pl.BlockSpec(memory_space=pl.ANY)],
        out_specs=pl.BlockSpec((tr, W_MISC), lambda i: (i, O_MISC // W_MISC)),
        out_shape=jax.ShapeDtypeStruct((T, N_PAD), BF16), input_output_aliases={2: 0},
        compiler_params=_params(dimension_semantics=("arbitrary",)),
    )(dkv, dba, dcols)


def _up_bwd(ys, cols, dm, w_up):
    T, tr = dm.shape[0], UPB_TR

    def body(y_ref, gl_ref, dm_ref, w_ref, dy_ref, dgl_ref, dw_ref):
        @pl.when(pl.program_id(1) == 0)
        def _():
            dw_ref[...] = jnp.zeros_like(dw_ref)

        _, vjp = jax.vjp(lambda y, gl, w: _sigmoid(gl) * _nn16(y, w),
                         y_ref[...].astype(F32), gl_ref[...].astype(F32), w_ref[...].astype(F32))
        dy, dgl, dw = vjp(dm_ref[...])
        dy_ref[...] = dy
        dgl_ref[...] = dgl.astype(BF16)
        dw_ref[...] += dw

    branch_rows = lambda w: pl.BlockSpec((tr, w), lambda n, i: (i, n))
    weight = pl.BlockSpec((None, BRANCH_W, D_MODEL), lambda n, i: (n, 0, 0))
    return pl.pallas_call(
        body, name="up_bwd", grid=(4, T // tr),
        in_specs=[branch_rows(BRANCH_W), branch_rows(D_MODEL), pl.BlockSpec((tr, D_MODEL), lambda n, i: (i, 0)), weight],
        out_specs=[branch_rows(BRANCH_W), branch_rows(D_MODEL), weight],
        out_shape=[jax.ShapeDtypeStruct((T, 4 * BRANCH_W), F32), jax.ShapeDtypeStruct((T, N_PAD), BF16),
                   jax.ShapeDtypeStruct(w_up.shape, F32)],
        compiler_params=_params(dimension_semantics=("arbitrary", "arbitrary")),
    )(ys, cols, dm, w_up)


def _matmul(name, a, b, kind, out_dtype, tm, tn, tk, after=None):
    if kind == "tn":
        (K, M), N = a.shape, b.shape[1]
    else:
        (M, K), N = a.shape, (b.shape[0] if kind == "nt" else b.shape[1])
    tm, tn, tk = min(tm, M), min(tn, N), min(tk, K)
    nk = K // tk
    dims = {"nn": _NN, "nt": _NT, "tn": _TN}[kind]

    n_after = 0 if after is None else 1

    def body(*refs):
        a_ref, b_ref, o_ref, acc = refs[0], refs[1], refs[2 + n_after], refs[3 + n_after:]
        part = lax.dot_general(a_ref[...], b_ref[...], (dims, ((), ())), preferred_element_type=F32)
        if nk == 1:
            o_ref[...] = part.astype(o_ref.dtype)
            return
        acc_ref = acc[0] if acc else o_ref
        k = pl.program_id(2)

        @pl.when(k == 0)
        def _():
            acc_ref[...] = part

        @pl.when(k > 0)
        def _():
            acc_ref[...] += part

        if acc:
            @pl.when(k == nk - 1)
            def _():
                o_ref[...] = acc_ref[...].astype(o_ref.dtype)

    a_spec = pl.BlockSpec((tk, tm), lambda i, j, k: (k, i)) if kind == "tn" else pl.BlockSpec((tm, tk), lambda i, j, k: (i, k))
    b_spec = pl.BlockSpec((tn, tk), lambda i, j, k: (j, k)) if kind == "nt" else pl.BlockSpec((tk, tn), lambda i, j, k: (k, j))
    return pl.pallas_call(
        body, name=name, grid=(M // tm, N // tn, nk), in_specs=[a_spec, b_spec] + [_ANY] * n_after,
        out_specs=pl.BlockSpec((tm, tn), lambda i, j, k: (i, j)),
        out_shape=jax.ShapeDtypeStruct((M, N), out_dtype),
        scratch_shapes=[pltpu.VMEM((tm, tn), F32)] if nk > 1 and out_dtype != F32 else [],
        compiler_params=_params(dimension_semantics=("arbitrary", "arbitrary", "arbitrary")),
    )(a, b, *([] if after is None else [after]))


def _pre_fn(first, _, rows, halos, params):
    return None, [_rms(rows[0], params[0])]


def _pre_fn_res(first, _, rows, halos, params):
    return None, [_rms(rows[0], params[0]), rows[0]]


def _memkv_fn(first, _, rows, halos, params):
    g, w = params
    return None, [_nn16(_rms(rows[0], g), w)]


def _conv_silu(x, halo, w4, keep_halo):
    tr = x.shape[0]
    halo = halo * keep_halo
    rid = _iota((tr, 1), 0)
    acc = w4[3] * x
    for s in (1, 2, 3):
        hs = jnp.concatenate([_rollr(halo, s), jnp.zeros((tr - halo.shape[0], x.shape[1]), F32)], axis=0)
        acc = acc + w4[3 - s] * jnp.where(rid < s, hs, _rollr(x, s))
    return _silu(acc)


def _dn_fn(first, S, rows, halos, params):
    qp, kp, vp, z, ba = rows
    conv, a_vec, dt_vec, dnorm = params
    ba = _cols(ba, 0, LANES)
    tr = qp.shape[0]
    keep = jnp.where(first, 0.0, 1.0)
    q = _conv_silu(qp, halos[0], [conv[3 * j + 0] for j in range(4)], keep)
    k = _conv_silu(kp, halos[1], [conv[3 * j + 1] for j in range(4)], keep)
    v = _conv_silu(vp, halos[2], [conv[3 * j + 2] for j in range(4)], keep)
    qh, kh, vh = [], [], []
    for h in range(4):
        a, b = h * LANES, (h + 1) * LANES
        xq, xk = _cols(q, a, b), _cols(k, a, b)
        qh.append(xq * lax.rsqrt(jnp.sum(xq * xq, axis=1, keepdims=True) + EPS) * (LANES ** -0.5))
        kh.append(xk * lax.rsqrt(jnp.sum(xk * xk, axis=1, keepdims=True) + EPS))
        vh.append(_cols(v, a, b))
    beta_all = _sigmoid(ba)
    g_all = -jnp.exp(a_vec) * _softplus(ba + dt_vec)
    C = DN_CHUNK
    ii, jj = _iota((C, C), 0), _iota((C, C), 1)
    strict, incl = ii > jj, ii >= jj
    eye = (ii == jj).astype(F32)
    last_row = (_iota((C, 1), 0) == C - 1).astype(F32)
    n_chunk = tr // C
    pairs = [(c, h) for c in range(n_chunk) for h in range(4)]
    rows_of = lambda a, c: _rowsl(a, c * C, (c + 1) * C)
    gcs = [_nn32(incl.astype(F32), rows_of(g_all, c)) for c in range(n_chunk)]
    qc = {(c, h): rows_of(qh[h], c) for c, h in pairs}
    kc = {(c, h): rows_of(kh[h], c) for c, h in pairs}
    beta = {(c, h): _lane_pick(rows_of(beta_all, c), h) for c, h in pairs}
    gc = {(c, h): _lane_pick(gcs[c], 4 + h) for c, h in pairs}
    dec = {p: jnp.exp(jnp.where(incl, gc[p] - jnp.sum(eye * gc[p], axis=0, keepdims=True), 0.0)) for p in pairs}
    egc = {p: jnp.exp(gc[p]) for p in pairs}
    kb = {p: kc[p] * beta[p] for p in pairs}
    kq = {p: _nt16(jnp.concatenate([kb[p], qc[p]], axis=0), kc[p]) for p in pairs}
    P = {p: -jnp.where(strict, _rowsl(kq[p], 0, C) * dec[p], 0.0) for p in pairs}
    aqk = {p: jnp.where(incl, _rowsl(kq[p], C, 2 * C) * dec[p], 0.0) for p in pairs}
    tinv = {p: eye + P[p] for p in pairs}
    P = {p: _nn16(P[p], P[p]) for p in pairs}
    for j in range(5):
        if j < 4:
            pt = {p: _nn16(jnp.concatenate([P[p], tinv[p]], axis=0), P[p]) for p in pairs}
            tinv = {p: tinv[p] + _rowsl(pt[p], C, 2 * C) for p in pairs}
            P = {p: _rowsl(pt[p], 0, C) for p in pairs}
        else:
            tinv = {p: tinv[p] + _nn16(tinv[p], P[p]) for p in pairs}
    uw = {(c, h): _nn16(tinv[c, h], jnp.concatenate([rows_of(vh[h], c) * beta[c, h], kb[c, h] * egc[c, h]], axis=1))
          for c, h in pairs}
    S = list(S)
    ychunks = []
    for c in range(n_chunk):
        zc = rows_of(z, c)
        hs = range(4)
        ws = [_nn16(jnp.concatenate([_cols(uw[c, h], LANES, 2 * LANES), qc[c, h] * egc[c, h]], axis=0), S[h]) for h in hs]
        vnew = [_cols(uw[c, h], 0, LANES) - _rowsl(ws[h], 0, C) for h in hs]
        o = [_rowsl(ws[h], C, 2 * C) + _nn16(aqk[c, h], vnew[h]) for h in hs]
        glast = [jnp.sum(gc[c, h] * last_row, axis=0, keepdims=True) for h in hs]
        S = [S[h] * jnp.exp(glast[h]) + _tn16(kc[c, h] * jnp.exp(glast[h] - gc[c, h]), vnew[h]) for h in hs]
        ychunks.append(jnp.concatenate(
            [_rms(o[h], dnorm) * _silu(_cols(zc, h * LANES, (h + 1) * LANES)) for h in hs], axis=1))
    return S, [jnp.concatenate(ychunks, axis=0)]


def _gm_fn(first, _, rows, halos, params):
    uv, z = rows
    gnorm, ws, bs = params
    tr = uv.shape[0]
    guv = _gelu(uv)
    u = _cols(guv, 0, BRANCH_W)
    v = _rms(_cols(guv, BRANCH_W, 2 * BRANCH_W), gnorm)
    ii, jj = _iota((LANES, LANES), 0), _iota((LANES, LANES), 1)
    eye = (ii == jj).astype(F32)
    wsm = [jnp.where(ii >= jj, ws[g], 0.0) for g in range(4)]
    bcol = [jnp.sum(eye * bs[g], axis=1, keepdims=True) for g in range(4)]
    chunks = []
    for c in range(tr // LANES):
        vc = _rowsl(v, c * LANES, (c + 1) * LANES)
        chunks.append(jnp.concatenate(
            [_nn16(wsm[g], _cols(vc, g * LANES, (g + 1) * LANES)) + bcol[g] for g in range(4)], axis=1))
    return None, [u * jnp.concatenate(chunks, axis=0) * _silu(z)]


def _swa_fn(first, _, rows, halos, params):
    q, kvc, z = rows
    sink_vec = params[0]
    P = LANES
    n_blk = q.shape[0] // P
    r, cc = _iota((P, P), 0), _iota((P, P), 1)
    lane = _iota((1, P), 1)
    key = _iota((P, 2 * P), 1)
    dist = _iota((P, 2 * P), 0) + P - key
    in_window = (dist >= 0) & (dist < P)
    valid = [in_window & (key >= jnp.where(first, P, 0))] + [in_window] * (n_blk - 1)
    halves = [(lane < 64).astype(F32), (lane >= 64).astype(F32)]
    dup = [(r == kh * 64 + (cc & 63)).astype(F32) for kh in range(2)]
    kv_blk = [halos[0]] + [_rowsl(kvc, b * P, (b + 1) * P) for b in range(n_blk)]
    pairs = [(b, kh) for b in range(n_blk) for kh in range(2)]
    kkvv = {}
    for b in range(n_blk):
        kv = jnp.concatenate([kv_blk[b], kv_blk[b + 1]], axis=0)
        k_v = jnp.concatenate([_cols(kv, 0, P), _cols(kv, P, 2 * P)], axis=0)
        for kh in range(2):
            kkvv[b, kh] = _nn16(k_v, dup[kh])
    scores = {}
    for b, kh in pairs:
        q_b = _rowsl(q, b * P, (b + 1) * P)
        stacked = jnp.concatenate([_cols(q_b, (2 * kh + g // 2) * P, (2 * kh + g // 2 + 1) * P) * halves[g % 2]
                                   for g in range(4)], axis=0)
        scores[b, kh] = _nt16(stacked, _rowsl(kkvv[b, kh], 0, 2 * P))
    probs = {}
    for b, kh in pairs:
        ps = []
        for g in range(4):
            s = jnp.where(valid[b], _rowsl(scores[b, kh], g * P, (g + 1) * P) * 0.125, NEG_INF)
            sink = _lane_pick(sink_vec, kh * 4 + g)
            m = lax.stop_gradient(jnp.maximum(jnp.max(s, axis=1, keepdims=True), sink))
            e = jnp.exp(s - m)
            ps.append(e / (jnp.sum(e, axis=1, keepdims=True) + jnp.exp(sink - m)))
        probs[b, kh] = jnp.concatenate(ps, axis=0)
    outs = {p: _nn16(probs[p], _rowsl(kkvv[p], 2 * P, 4 * P)) for p in pairs}
    tile = [jnp.concatenate([_rowsl(outs[b, j // 2], (2 * (j % 2)) * P, (2 * (j % 2) + 1) * P) * halves[0]
                             + _rowsl(outs[b, j // 2], (2 * (j % 2) + 1) * P, (2 * (j % 2) + 2) * P) * halves[1]
                             for j in range(4)], axis=1) for b in range(n_blk)]
    return None, [jnp.concatenate(tile, axis=0) * _silu(z)]


def _mem_fn(first, _, rows, halos, params):
    q, z = rows
    mkv = params[0]
    heads = [(h * LANES, (h + 1) * LANES) for h in range(4)]
    scores = [_nt16(_cols(q, a, b), _cols(mkv, a, b)) * (LANES ** -0.5) for a, b in heads]
    probs = []
    for s in scores:
        e = jnp.exp(s - lax.stop_gradient(jnp.max(s, axis=1, keepdims=True)))
        probs.append(e / jnp.sum(e, axis=1, keepdims=True))
    outs = [_nn16(p, _cols(mkv, BRANCH_W + a, BRANCH_W + b)) for p, (a, b) in zip(probs, heads)]
    return None, [jnp.concatenate(outs, axis=1) * _silu(z)]


def _up_fn(first, _, rows, halos, params):
    ys, gl, w_up = rows[:4], rows[4], params[0]
    merged = None
    for n in range(4):
        term = _sigmoid(_cols(gl, n * D_MODEL, (n + 1) * D_MODEL)) * _nn16(ys[n], w_up[n])
        merged = term if merged is None else merged + term
    return None, [merged]


def _out_fn(first, _, rows, halos, params):
    x, merged = rows
    w, g = params
    return None, [x + _rms(_nn16(merged, w), g)]


def _loss_fn(first, _, rows, halos, params):
    y, t = rows
    d = y - t
    lrow = 0.5 * jnp.mean(d * d, axis=1, keepdims=True)
    return None, [d * (1.0 / D_MODEL), jnp.broadcast_to(lrow, (y.shape[0], LANES))]


TR = 256
BIG_TR = 512
SWA_TR = 256
DN_TR = 256
UP_TR = 256
UPB_TR = 512
CONV_HALO = 16
CARRY = (4, LANES, LANES)


def _branch_rows(cols, g):
    hb = CONV_HALO
    a = [Row(cols, 512, O_AQ // 512, hb, g), Row(cols, 512, O_AK // 512, hb, g), Row(cols, 512, O_AV // 512, hb, g),
         Row(cols, 512, O_AZ // 512, 0, g), Row(cols, 256, O_BA // 256)]
    b = [Row(cols, 1024, O_BUV // 1024, 0, g), Row(cols, 512, O_BZ // 512, 0, g)]
    c = [Row(cols, 512, O_CQ // 512, 0, g), Row(cols, 256, O_CKV // 256, LANES), Row(cols, 512, O_CZ // 512, 0, g)]
    m = [Row(cols, 512, O_MQ // 512, 0, g), Row(cols, 512, O_MZ // 512, 0, g)]
    return a, b, c, m


def _layer_fwd(x, mem, W, late_weights=None):
    h = _rows_fwd("prenorm_fwd", _pre_fn, [Row(x, D_MODEL, 0)], [W["norm_pre"]], [(D_MODEL, BF16)], BIG_TR)[0]
    cols = _matmul("in_proj_fwd", h, W["w_pad"], "nt", BF16, 2048, 1024, 1024)
    if late_weights is not None:
        W = dict(W, **late_weights(cols))
    mem_kv = _rows_fwd("memkv_fwd", _memkv_fn, [Row(mem, D_MODEL, 0)], [W["norm_mem"], W["w_mem_kv"]],
                       [(D_MODEL, F32)], MEM_LEN)[0]
    ra, rb, rc, rm = _branch_rows(cols, True)
    y_a, csave = _rows_fwd("dn_fwd", _dn_fn, ra, [W["conv"], W["a_vec"], W["dt_vec"], W["dn_norm"]],
                           [(BRANCH_W, BF16)], DN_TR, CARRY)
    y_b = _rows_fwd("gm_fwd", _gm_fn, rb, [W["gm_norm"], W["spatial_w"], W["spatial_b"]], [(BRANCH_W, BF16)], BIG_TR)[0]
    y_c = _rows_fwd("swa_fwd", _swa_fn, rc, [W["sink_vec"]], [(BRANCH_W, BF16)], SWA_TR)[0]
    y_m = _rows_fwd("mem_fwd", _mem_fn, rm, [mem_kv], [(BRANCH_W, BF16)], BIG_TR)[0]
    ys = [y_a, y_b, y_c, y_m]
    merged = _rows_fwd("up_fwd", _up_fn, [Row(y, BRANCH_W, 0) for y in ys] + [Row(cols, 4 * D_MODEL, 0)],
                       [W["w_up"]], [(D_MODEL, BF16)], UP_TR)[0]
    x_new = _rows_fwd("out_fwd", _out_fn, [Row(x, D_MODEL, 0), Row(merged, D_MODEL, 0)],
                      [W["w_out"], W["norm_post"]], [(D_MODEL, F32)], TR)[0]
    return x_new, dict(x=x, h=h, cols=cols, mem_kv=mem_kv, csave=csave, ys=ys, merged=merged), W


def _layer_bwd(dxn, mem, W, sv, on_weight_grads=None):
    x, cols = sv["x"], sv["cols"]
    (dx_res, dm), (dw_out, dnorm_post), _ = _rows_bwd(
        "out_bwd", _out_fn, [Row(x, D_MODEL, 0), Row(sv["merged"], D_MODEL, 0)], [W["w_out"], W["norm_post"]],
        [dxn], TR)
    dys, dcols, dw_up = _up_bwd(jnp.concatenate(sv["ys"], axis=1), cols, dm, W["w_up"])
    dys = [Row(dys, BRANCH_W, n) for n in range(4)]
    ra, rb, rc, rm = _branch_rows(cols, "cols")
    (dba,), (dconv, da_vec, ddt_vec, ddn_norm), dcols = _rows_bwd(
        "dn_bwd", _dn_fn, ra, [W["conv"], W["a_vec"], W["dt_vec"], W["dn_norm"]], [dys[0]], DN_TR, CARRY,
        sv["csave"], dcols=dcols)
    _, (dgm_norm, dws, dbs), dcols = _rows_bwd(
        "gm_bwd", _gm_fn, rb, [W["gm_norm"], W["spatial_w"], W["spatial_b"]], [dys[1]], BIG_TR, dcols=dcols)
    (dkv_c,), (dsink,), dcols = _rows_bwd("swa_bwd", _swa_fn, rc, [W["sink_vec"]], [dys[2]], SWA_TR, dcols=dcols)
    _, (dmem_kv,), dcols = _rows_bwd("mem_bwd", _mem_fn, rm, [sv["mem_kv"]], [dys[3]], BIG_TR, dcols=dcols)
    dcols = _fill_misc(dcols, dkv_c, dba, BIG_TR)
    _, (dnorm_mem, dw_mem_kv), _ = _rows_bwd("memkv_bwd", _memkv_fn, [Row(mem, D_MODEL, 0, 0, False)],
                                             [W["norm_mem"], W["w_mem_kv"]], [dmem_kv], MEM_LEN)
    dw_pad = _matmul("in_proj_dw", dcols, sv["h"], "tn", BF16, 1024, 1024, 2048)
    grads = dict(norm_post=dnorm_post, norm_mem=dnorm_mem, w_pad=dw_pad, conv=dconv,
                 a_vec=da_vec, dt_vec=ddt_vec, dn_norm=ddn_norm, gm_norm=dgm_norm, spatial_w=dws, spatial_b=dbs,
                 sink_vec=dsink, w_mem_kv=dw_mem_kv, w_up=dw_up, w_out=dw_out)
    started = None if on_weight_grads is None else on_weight_grads(grads)
    dh = _matmul("in_proj_dx", dcols, W["w_pad"], "nn", F32, 1024, 1024, 2048, after=started)
    (dx,), (grads["norm_pre"],), _ = _rows_bwd("prenorm_bwd", _pre_fn_res, [Row(x, D_MODEL, 0)], [W["norm_pre"]],
                                               [dh, dx_res], BIG_TR)
    return dx, grads


def _lane_vec(v, off):
    return jnp.zeros((1, LANES), F32).at[0, off:off + v.shape[0]].set(v)


def _layer_weights(l, w_pad, conv_w, small, **late):
    return dict(
        late, w_pad=w_pad, conv=conv_w.reshape(4, 3, BRANCH_W).reshape(12, 1, BRANCH_W),
        norm_pre=small["norm_pre"][l][None], norm_post=small["norm_post"][l][None],
        norm_mem=small["norm_mem"][l][None],
        a_vec=_lane_vec(small["a_log"][l], 4), dt_vec=_lane_vec(small["dt_bias"][l], 4),
        dn_norm=small["dn_norm"][l][None], gm_norm=small["gm_norm"][l][None],
        spatial_w=small["spatial_w"][l], spatial_b=small["spatial_b"][l][:, None, :],
        sink_vec=_lane_vec(small["sinks"][l], 0))


_MESH = pl.DeviceIdType.MESH
_ANY = pl.BlockSpec(memory_space=pl.ANY)


def _position():
    return lax.axis_index("x"), lax.axis_index("y"), lax.axis_index("c")


def _remote(src, dst, send_sem, recv_sem, dev):
    return pltpu.make_async_remote_copy(src_ref=src, dst_ref=dst, send_sem=send_sem, recv_sem=recv_sem,
                                        device_id=dev, device_id_type=_MESH)


def _hbm_call(name, body, arrs, out_shapes, sems, aliases=None):
    return pl.pallas_call(
        body, name=name, in_specs=[_ANY] * len(arrs), out_specs=[_ANY] * len(out_shapes), out_shape=out_shapes,
        scratch_shapes=[pltpu.SemaphoreType.DMA((k,)) for k in sems], input_output_aliases=aliases or {},
        compiler_params=pltpu.CompilerParams(has_side_effects=True),
    )(*arrs)


def _other_chips(x, y):
    return [(1 - x, y), (x, 1 - y), (1 - x, 1 - y)]


def _gather_weights(arrs, relayed):
    n = len(arrs)

    def body(*refs):
        ins, outs = refs[:n], refs[n:2 * n]
        ici_send, ici_recv, d2d_send, d2d_recv = refs[2 * n:]
        x, y, c = _position()
        me = 2 * x + y
        xn, yn, dg = _other_chips(x, y)
        chip = lambda p: 2 * p[0] + p[1]
        sends = []

        def go(cp):
            cp.start()
            sends.append(cp)

        def ici(a, j, src, dst, to):
            return _remote(src, dst, ici_send.at[4 * a + j], ici_recv.at[4 * a + j], (*to, c))

        for a in range(n):
            go(ici(a, 0, ins[a].at[c], outs[a].at[c, me], xn))
            go(ici(a, 1, ins[a].at[c], outs[a].at[c, me], yn))
            if not relayed[a]:
                go(ici(a, 2, ins[a].at[c], outs[a].at[c, me], dg))
        for a in range(n):
            h = arrs[a].shape[1] // 2
            from_x, from_y = outs[a].at[c, chip(xn)], outs[a].at[c, chip(yn)]
            ici(a, 0, ins[a].at[c], from_x, xn).wait_recv()
            if relayed[a]:
                go(ici(a, 2, from_x.at[pl.ds(0, h)], from_x.at[pl.ds(0, h)], yn))
            ici(a, 1, ins[a].at[c], from_y, yn).wait_recv()
            if relayed[a]:
                go(ici(a, 3, from_y.at[pl.ds(h, h)], from_y.at[pl.ds(h, h)], xn))
            for j, slab in enumerate((from_x, from_y)):
                go(_remote(slab, slab, d2d_send.at[3 * a + j], d2d_recv.at[3 * a + j], (x, y, 1 - c)))
        for a in range(n):
            h = arrs[a].shape[1] // 2
            from_d = outs[a].at[c, chip(dg)]
            if relayed[a]:
                ici(a, 2, from_d.at[pl.ds(0, h)], from_d.at[pl.ds(0, h)], yn).wait_recv()
                ici(a, 3, from_d.at[pl.ds(h, h)], from_d.at[pl.ds(h, h)], xn).wait_recv()
            else:
                ici(a, 2, ins[a].at[c], from_d, dg).wait_recv()
            go(_remote(from_d, from_d, d2d_send.at[3 * a + 2], d2d_recv.at[3 * a + 2], (x, y, 1 - c)))
        for a in range(n):
            for j, p in enumerate((xn, yn, dg)):
                slab = outs[a].at[1 - c, chip(p)]
                _remote(slab, slab, d2d_send.at[3 * a + j], d2d_recv.at[3 * a + j], (x, y, 1 - c)).wait_recv()
        for cp in sends:
            cp.wait_send()

    return _hbm_call("gather_weights", body, arrs,
                     [jax.ShapeDtypeStruct((N_LAYER, N_CHIP) + a.shape[1:], a.dtype) for a in arrs],
                     [4 * n, 4 * n, 3 * n, 3 * n])


def _pair_exchange(arrs):
    n = len(arrs)

    def body(*refs):
        ins, outs = refs[:n], refs[n:2 * n]
        send_sems, recv_sems = refs[2 * n:]
        x, y, c = _position()
        cps = [_remote(ins[a].at[1 - c], outs[a], send_sems.at[a], recv_sems.at[a], (x, y, 1 - c)) for a in range(n)]
        for cp in cps:
            cp.start()
        for cp in cps:
            cp.wait_recv()
        for cp in cps:
            cp.wait_send()

    return _hbm_call("pair_exchange", body, arrs, [jax.ShapeDtypeStruct(a.shape[1:], a.dtype) for a in arrs], [n, n])


def _chip_scatter(arrs):
    n = len(arrs)

    def body(*refs):
        ins, outs = refs[:n], refs[n:2 * n]
        send_sems, recv_sems = refs[2 * n:]
        x, y, c = _position()
        me = 2 * x + y
        sends = []
        for a in range(n):
            for j, (px, py) in enumerate(_other_chips(x, y)):
                sends.append(_remote(ins[a].at[2 * px + py], outs[a].at[me], send_sems.at[3 * a + j],
                                     recv_sems.at[3 * a + j], (px, py, c)))
                sends[-1].start()
        for a in range(n):
            for j, (px, py) in enumerate(_other_chips(x, y)):
                _remote(ins[a].at[me], outs[a].at[2 * px + py], send_sems.at[3 * a + j], recv_sems.at[3 * a + j],
                        (px, py, c)).wait_recv()
        for cp in sends:
            cp.wait_send()

    return _hbm_call("chip_scatter", body, arrs, [jax.ShapeDtypeStruct(a.shape, a.dtype) for a in arrs],
                     [3 * n, 3 * n])


def _pair_share(arrs):
    n = len(arrs)

    def body(*refs):
        ins, outs = refs[:n], refs[n:2 * n]
        send_sems, recv_sems = refs[2 * n:]
        x, y, c = _position()
        cps = [_remote(ins[a].at[c], outs[a].at[c], send_sems.at[a], recv_sems.at[a], (x, y, 1 - c)) for a in range(n)]
        for cp in cps:
            cp.start()
        for a in range(n):
            _remote(ins[a].at[c], outs[a].at[1 - c], send_sems.at[a], recv_sems.at[a], (x, y, 1 - c)).wait_recv()
        for cp in cps:
            cp.wait_send()

    return _hbm_call("pair_share", body, arrs, [jax.ShapeDtypeStruct(a.shape, a.dtype) for a in arrs], [n, n],
                     {a: a for a in range(n)})


def _pair_forward(arrs):
    n = len(arrs)

    def body(*refs):
        ins, outs = refs[:n], refs[n:2 * n]
        send_sems, recv_sems = refs[2 * n:]
        x, y, c = _position()
        sends = []
        for a in range(n):
            for j, (px, py) in enumerate(_other_chips(x, y)):
                sends.append(_remote(ins[a].at[c, 2 * px + py], outs[a].at[c, 2 * px + py], send_sems.at[3 * a + j],
                                     recv_sems.at[3 * a + j], (x, y, 1 - c)))
                sends[-1].start()
        for a in range(n):
            for j, (px, py) in enumerate(_other_chips(x, y)):
                slab = outs[a].at[1 - c, 2 * px + py]
                _remote(slab, slab, send_sems.at[3 * a + j], recv_sems.at[3 * a + j], (x, y, 1 - c)).wait_recv()
        for cp in sends:
            cp.wait_send()

    return _hbm_call("pair_forward", body, arrs, [jax.ShapeDtypeStruct(a.shape, a.dtype) for a in arrs],
                     [3 * n, 3 * n], {a: a for a in range(n)})


_HBM = pl.BlockSpec(memory_space=pltpu.HBM)
_SEM = pl.BlockSpec(memory_space=pltpu.SEMAPHORE)
_EFFECT = pltpu.SideEffectType.DATAFLOW_SIDE_EFFECTING


def _chip_copies(kind, srcs, lands, send_sems, recv_sems):
    x, y, c = _position()
    me = 2 * x + y
    sends, recvs = [], []
    for a in range(len(srcs)):
        for j, (px, py) in enumerate(_other_chips(x, y)):
            s, sems, dev = 2 * px + py, (send_sems.at[3 * a + j], recv_sems.at[3 * a + j]), (px, py, c)
            if kind == "gather":
                sends.append(_remote(srcs[a].at[c], lands[a].at[c, me], *sems, dev))
                recvs.append(_remote(srcs[a].at[c], lands[a].at[c, s], *sems, dev))
            else:
                sends.append(_remote(srcs[a].at[s], lands[a].at[me], *sems, dev))
                recvs.append(_remote(srcs[a].at[me], lands[a].at[s], *sems, dev))
    return sends, recvs


def _split_start(name, kind, srcs, land_shapes, after):
    n = len(srcs)

    def body(*refs):
        sends, _ = _chip_copies(kind, refs[:n], refs[n:2 * n], refs[2 * n + 1], refs[2 * n + 2])
        for cp in sends:
            cp.start()
        refs[-1][...] = jnp.zeros_like(refs[-1])

    hbm = lambda a: pltpu.with_memory_space_constraint(a, pltpu.HBM)
    lands = [lax.empty(s.shape, s.dtype) for s in land_shapes]
    outs = pl.pallas_call(
        body, name=name, in_specs=[_HBM] * (2 * n) + [_ANY],
        out_specs=[_SEM, _SEM] + [_HBM] * (2 * n) + [pl.BlockSpec(memory_space=pltpu.VMEM)],
        out_shape=[pltpu.SemaphoreType.DMA((3 * n,)), pltpu.SemaphoreType.DMA((3 * n,))]
        + [pltpu.HBM(a.shape, a.dtype) for a in list(srcs) + lands] + [jax.ShapeDtypeStruct((8, LANES), F32)],
        input_output_aliases={i: 2 + i for i in range(2 * n)},
        compiler_params=pltpu.CompilerParams(has_side_effects=_EFFECT),
    )(*[hbm(a) for a in srcs], *[hbm(a) for a in lands], after)
    return outs[0], outs[1], list(outs[2:2 + 2 * n]), outs[-1]


def _split_wait(name, kind, started, after):
    send_sems, recv_sems, thru, _ = started
    n = len(thru) // 2

    def body(*refs):
        sends, recvs = _chip_copies(kind, refs[:n], refs[n:2 * n], refs[2 * n], refs[2 * n + 1])
        for cp in sends:
            cp.wait_send()
        for cp in recvs:
            cp.wait_recv()

    outs = pl.pallas_call(
        body, name=name, in_specs=[_HBM] * (2 * n) + [_SEM, _SEM, _ANY], out_specs=[_HBM] * (2 * n),
        out_shape=[pltpu.HBM(a.shape, a.dtype) for a in thru], input_output_aliases={i: i for i in range(2 * n)},
        compiler_params=pltpu.CompilerParams(has_side_effects=_EFFECT),
    )(*thru, send_sems, recv_sems, after)
    return list(outs[:n]), list(outs[n:])


def _allreduce_small(g):
    def body(g_ref, o_ref, pair_buf, chip_buf, send_sems, recv_sems):
        x, y, c = _position()
        me = 2 * x + y
        sib = (x, y, 1 - c)
        to_sib = _remote(g_ref.at[1 - c], pair_buf, send_sems.at[0], recv_sems.at[0], sib)
        to_sib.start()
        to_sib.wait_recv()
        chip_buf[me] = g_ref[c] + pair_buf[...]
        sends = [to_sib]
        chips = _other_chips(x, y)
        for j, (px, py) in enumerate(chips):
            sends.append(_remote(chip_buf.at[me], chip_buf.at[me], send_sems.at[1 + j], recv_sems.at[1 + j], (px, py, c)))
            sends[-1].start()
        for j, (px, py) in enumerate(chips):
            _remote(chip_buf.at[me], chip_buf.at[2 * px + py], send_sems.at[1 + j], recv_sems.at[1 + j],
                    (px, py, c)).wait_recv()
        o_ref[c] = ((chip_buf[0] + chip_buf[1]) + chip_buf[2]) + chip_buf[3]
        sends.append(_remote(o_ref.at[c], o_ref.at[c], send_sems.at[4], recv_sems.at[4], sib))
        sends[-1].start()
        _remote(o_ref.at[c], o_ref.at[1 - c], send_sems.at[4], recv_sems.at[4], sib).wait_recv()
        for cp in sends:
            cp.wait_send()

    vmem = pl.BlockSpec(memory_space=pltpu.VMEM)
    return pl.pallas_call(
        body, name="allreduce_small", in_specs=[vmem], out_specs=vmem, out_shape=jax.ShapeDtypeStruct(g.shape, F32),
        scratch_shapes=[pltpu.VMEM(g.shape[1:], F32), pltpu.VMEM((N_CHIP,) + g.shape[1:], F32),
                        pltpu.SemaphoreType.DMA((5,)), pltpu.SemaphoreType.DMA((5,))],
        compiler_params=_params(),
    )(g)


EW_ROWS = 512


def _ew(name, fn, ins, n_out, out_dtype=F32, out_slot=None, into=None):
    def dims(a):
        return a[0].shape[1:] if isinstance(a, tuple) else a.shape

    R, w = dims(ins[0])
    tr = EW_ROWS if R % EW_ROWS == 0 else R
    n_into = len(into) if into else 0

    def body(c_ref, *refs):
        outs = fn(*[r[...] for r in refs[:len(ins)]])
        for r, v in zip(refs[len(ins) + n_into:], outs):
            r[...] = v.astype(r.dtype)

    def lead_spec(l):
        if l == "c":
            return pl.BlockSpec((None, tr, w), lambda i, c_ref: (c_ref[0], i, 0))
        return pl.BlockSpec((None, tr, w), lambda i, c_ref, s=l: (s, i, 0))

    plain = pl.BlockSpec((tr, w), lambda i, c_ref: (i, 0))
    in_specs = [lead_spec(a[1]) if isinstance(a, tuple) else plain for a in ins] + [_ANY] * n_into
    out_spec = plain if out_slot is None else lead_spec(out_slot)
    out_shape = jax.ShapeDtypeStruct((R, w) if out_slot is None else (2, R, w), out_dtype)
    return pl.pallas_call(
        body, name=name,
        grid_spec=pltpu.PrefetchScalarGridSpec(num_scalar_prefetch=1, grid=(R // tr,), in_specs=in_specs,
                                               out_specs=[out_spec] * n_out),
        out_shape=[out_shape] * n_out, input_output_aliases={1 + len(ins) + j: j for j in range(n_into)},
        compiler_params=_params(dimension_semantics=("arbitrary",)),
    )(lax.axis_index("c").astype(jnp.int32).reshape(1), *[a[0] if isinstance(a, tuple) else a for a in ins],
      *(into or []))


def _adamw_fn(w, g, m, v):
    m = ADAM_B1 * m + (1.0 - ADAM_B1) * g
    v = ADAM_B2 * v + (1.0 - ADAM_B2) * (g * g)
    m_hat = m / (1.0 - ADAM_B1 ** ADAM_STEP)
    v_hat = v / (1.0 - ADAM_B2 ** ADAM_STEP)
    delta = -ADAM_LR * (m_hat / (jnp.sqrt(v_hat) + ADAM_EPS) + ADAM_WD * w)
    return delta, m, v


def _adamw(name, w, g, m, v):
    shape = w.shape
    two = lambda a: a.reshape(-1, shape[-1])
    return [o.reshape(shape) for o in _ew(name, _adamw_fn, [two(w), two(g), two(m), two(v)], 3)]


def _adamw_layer(name, l, w, g, m, v, into):
    k = w.shape[-1]
    three = lambda a: (a.reshape(N_LAYER, -1, k), l)
    fn = lambda w_, g_, m_, v_: _adamw_fn(w_, g_, m_, v_) + (g_,)
    outs = _ew(name, fn, [three(w), g.reshape(-1, k), three(m), three(v)], 4, out_slot=l,
               into=None if into is None else [a.reshape(N_LAYER, -1, k) for a in into])
    return [o.reshape(w.shape) for o in outs]


def _adamw_rows(name, l, w, g, m, v, into):
    _, R, k = w.shape
    n_into = len(into) if into else 0

    def body(*refs):
        w_ref, g_ref, m_ref, v_ref = refs[:4]
        d_out, m_out, v_out, g_out = refs[4 + n_into:]
        g_blk = g_ref[...]
        d_out[...], m_out[...], v_out[...] = _adamw_fn(w_ref[...], g_blk, m_ref[...], v_ref[...])
        g_out[...] = g_blk

    spec = pl.BlockSpec((None, EW_ROWS, k), lambda i: (l, i, 0))
    return pl.pallas_call(
        body, name=name, grid=(-(-R // EW_ROWS),),
        in_specs=[spec, pl.BlockSpec((EW_ROWS, k), lambda i: (i, 0)), spec, spec] + [_ANY] * n_into,
        out_specs=[spec] * 4, out_shape=[jax.ShapeDtypeStruct((N_LAYER, R, k), F32)] * 4,
        input_output_aliases={4 + j: j for j in range(n_into)},
        compiler_params=_params(dimension_semantics=("arbitrary",)),
    )(w, g, m, v, *(into or []))


_SMALL = [("norm_pre", (2, 1024)), ("norm_post", (2, 1024)), ("norm_mem", (2, 1024)), ("a_log", (2, 4)),
          ("dt_bias", (2, 4)), ("dn_norm", (2, 128)), ("gm_norm", (2, 512)), ("spatial_w", (2, 4, 128, 128)),
          ("spatial_b", (2, 4, 128)), ("sinks", (2, 8)), ("loss", (2, 1))]
_SMALL_ROWS = 208
_BIG = ["w_in", "conv_w", "w_mem_kv", "w_up", "w_out"]
_NAMES = ["norm_pre", "norm_post", "norm_mem", "w_in", "conv_w", "a_log", "dt_bias", "dn_norm", "gm_norm",
          "spatial_w", "spatial_b", "sinks", "w_mem_kv", "w_up", "w_out"]


def _size(shape):
    n = 1
    for s in shape:
        n *= s
    return n


def _pack_small(d):
    rows = []
    for n, shp in _SMALL:
        a = d[n].reshape(N_LAYER, -1)
        rows.append(a.reshape(-1, 1024) if a.shape[1] > 1024 else jnp.pad(a, ((0, 6), (0, 1024 - a.shape[1]))))
    assert sum(r.shape[0] for r in rows) == _SMALL_ROWS
    return jnp.concatenate(rows, axis=0)


def _unpack_small(p):
    out, off = {}, 0
    for n, shp in _SMALL:
        c = _size(shp) // N_LAYER
        k = 8 if c <= 1024 else _size(shp) // 1024
        out[n] = (p[off:off + N_LAYER, :c] if c <= 1024 else p[off:off + k]).reshape(shp)
        off += k
    return out


_HALF_SHAPE = {"w_in": (SHARD_PAD // 2, D_MODEL), "conv_w": (2, 3 * BRANCH_W // N_CHIP), "w_mem_kv": (128, D_MODEL),
               "w_up": (2, BRANCH_W, D_MODEL // N_CHIP), "w_out": (128, D_MODEL)}


def _chip_major(g):
    g = jnp.swapaxes(g, 0, 1)
    return g.reshape((N_CHIP, 2 * g.shape[2]) + g.shape[3:])


def _half_major(g):
    g = g.reshape((N_CHIP, 2, g.shape[1] // 2) + g.shape[2:])
    return jnp.swapaxes(g, 0, 1).astype(BF16)


N_EARLY = 2


def _early_views(l, g_in, g_conv, small):
    return _layer_weights(l, _w_pad_from_slabs(g_in),
                          _chip_major(g_conv).transpose(1, 0, 2).reshape(4, 3 * BRANCH_W), small)


def _late_views(g_kv, g_up, g_out):
    return dict(w_mem_kv=_chip_major(g_kv).reshape(D_MODEL, D_MODEL),
                w_up=_chip_major(g_up).transpose(1, 2, 0, 3).reshape(4, BRANCH_W, D_MODEL),
                w_out=_chip_major(g_out).reshape(D_MODEL, D_MODEL))


def _pair_sums(g):
    big = [_slabs_from_pad(g["w_pad"]),
           _half_major(g["conv"].reshape(4, N_CHIP, 3 * BRANCH_W // N_CHIP).transpose(1, 0, 2)),
           _half_major(g["w_mem_kv"].reshape(N_CHIP, D_MODEL // N_CHIP, D_MODEL)),
           _half_major(g["w_up"].reshape(4, BRANCH_W, N_CHIP, D_MODEL // N_CHIP).transpose(2, 0, 1, 3)),
           _half_major(g["w_out"].reshape(N_CHIP, D_MODEL // N_CHIP, D_MODEL))]
    add2 = lambda a, b: [a.astype(F32) + b.astype(F32)]
    pair = []
    for n, b, p in zip(_BIG, big, _pair_exchange(big)):
        k = b.shape[-1]
        pair.append(_ew("pair_sum_" + n, add2, [(b.reshape(2, -1, k), "c"), p.reshape(-1, k)], 1, BF16)[0]
                    .reshape(p.shape))
    return pair


def _chip_sums(landed, pair, me):
    add4 = lambda a, b, c_, d: [((a.astype(F32) + b.astype(F32)) + c_.astype(F32)) + d.astype(F32)]
    totals = []
    for n, r, q in zip(_BIG, landed, pair):
        r = _own_slot(r, lax.dynamic_index_in_dim(q, me, 0), me, 0)
        k = r.shape[-1]
        totals.append(_ew("chip_sum_" + n, add4, [(r.reshape(N_CHIP, -1, k), s) for s in range(N_CHIP)], 1,
                          out_slot="c")[0].reshape((2,) + r.shape[1:]))
    return totals


def _own_slot(buf, mine, me, axis):
    return lax.dynamic_update_index_in_dim(buf, mine.astype(buf.dtype), me, axis)


def kernel(x, mem, norm_pre, norm_post, norm_mem, w_in, conv_w, a_log, dt_bias, dn_norm, gm_norm, spatial_w, spatial_b, sinks, w_mem_kv, w_up, w_out, loss_target, m_norm_pre, m_norm_post, m_norm_mem, m_w_in, m_conv_w, m_a_log, m_dt_bias, m_dn_norm, m_gm_norm, m_spatial_w, m_spatial_b, m_sinks, m_w_mem_kv, m_w_up, m_w_out, v_norm_pre, v_norm_post, v_norm_mem, v_w_in, v_conv_w, v_a_log, v_dt_bias, v_dn_norm, v_gm_norm, v_spatial_w, v_spatial_b, v_sinks, v_w_mem_kv, v_w_up, v_w_out):
    w = dict(norm_pre=norm_pre, norm_post=norm_post, norm_mem=norm_mem, w_in=w_in, conv_w=conv_w, a_log=a_log,
             dt_bias=dt_bias, dn_norm=dn_norm, gm_norm=gm_norm, spatial_w=spatial_w, spatial_b=spatial_b, sinks=sinks,
             w_mem_kv=w_mem_kv, w_up=w_up, w_out=w_out)
    m = dict(norm_pre=m_norm_pre, norm_post=m_norm_post, norm_mem=m_norm_mem, w_in=m_w_in, conv_w=m_conv_w,
             a_log=m_a_log, dt_bias=m_dt_bias, dn_norm=m_dn_norm, gm_norm=m_gm_norm, spatial_w=m_spatial_w,
             spatial_b=m_spatial_b, sinks=m_sinks, w_mem_kv=m_w_mem_kv, w_up=m_w_up, w_out=m_w_out)
    v = dict(norm_pre=v_norm_pre, norm_post=v_norm_post, norm_mem=v_norm_mem, w_in=v_w_in, conv_w=v_conv_w,
             a_log=v_a_log, dt_bias=v_dt_bias, dn_norm=v_dn_norm, gm_norm=v_gm_norm, spatial_w=v_spatial_w,
             spatial_b=v_spatial_b, sinks=v_sinks, w_mem_kv=v_w_mem_kv, w_up=v_w_up, w_out=v_w_out)
    me = 2 * lax.axis_index("x") + lax.axis_index("y")

    tr = lambda a: a.transpose(0, 2, 1)
    w_t = tr(w_in)
    w_in_t = jnp.pad(w_t.astype(BF16), ((0, 0), (0, SHARD_PAD - SHARD_IN), (0, 0)))
    for d in (w, m, v):
        d["loss"] = jnp.zeros((N_LAYER, 1), F32)
    local = dict(w_in=w_in_t, conv_w=conv_w, w_mem_kv=w_mem_kv.astype(BF16), w_up=w_up.astype(BF16),
                 w_out=w_out.astype(BF16))
    halves = lambda l: [local[n][l].reshape((2,) + _HALF_SHAPE[n]) for n in _BIG]
    own = lambda gathered, mine: [_own_slot(g, h[:, None], me, 1) for g, h in zip(gathered, mine)]
    lands = [jax.ShapeDtypeStruct((2, N_CHIP) + _HALF_SHAPE[n], local[n].dtype) for n in _BIG]
    h0 = halves(0)
    g0 = own(_gather_weights(h0[:N_EARLY], [True, False]), h0[:N_EARLY])
    rest0 = _split_start("gather_l0_rest_start", "gather", h0[N_EARLY:], lands[N_EARLY:], g0[1])
    started = _split_start("gather_l1_start", "gather", halves(1), lands, rest0[3])

    xl, meml = x[0], mem[0]
    W0 = _early_views(0, g0[0], g0[1], w)
    W0["norm_pre"] = W0["norm_pre"] + started[3][0, 0]

    def late0(cols):
        mine, landed = _split_wait("gather_l0_rest_wait", "gather", rest0, cols)
        return _late_views(*own(_pair_forward(landed), mine))

    x1, sv0, W0 = _layer_fwd(xl, meml, W0, late0)
    mine1, landed1 = _split_wait("gather_l1_wait", "gather", started, x1)
    g1 = own(_pair_forward(landed1), mine1)
    W1 = dict(_early_views(1, g1[0], g1[1], w), **_late_views(*g1[N_EARLY:]))
    x2, sv1, _ = _layer_fwd(x1, meml, W1)
    dy, lrows = _rows_fwd("loss", _loss_fn, [Row(x2, D_MODEL, 0), Row(loss_target[0], D_MODEL, 0)], [],
                          [(D_MODEL, F32), (LANES, F32)], BIG_TR)
    loss_local = jnp.sum(lrows[:, 0])

    scattering = {}

    def start_scatter(l):
        def on_weight_grads(g):
            pair = _pair_sums(g)
            scattering[l] = _split_start("scatter_l%d_start" % l, "scatter", pair,
                                         [jax.ShapeDtypeStruct(p.shape, p.dtype) for p in pair], pair[1])
            return scattering[l][3]
        return on_weight_grads

    dx1, grads1 = _layer_bwd(dy, meml, W1, sv1, start_scatter(1))
    dx, grads0 = _layer_bwd(dx1, meml, W0, sv0, start_scatter(0))
    pair1, landed1 = _split_wait("scatter_l1_wait", "scatter", scattering[1], dx)
    after_start = scattering[0][3][0, 0]
    grads = [grads0, grads1]

    small_local = dict(
        norm_pre=jnp.stack([g["norm_pre"][0] for g in grads]), norm_post=jnp.stack([g["norm_post"][0] for g in grads]),
        norm_mem=jnp.stack([g["norm_mem"][0] for g in grads]), a_log=jnp.stack([g["a_vec"][0, 4:8] for g in grads]),
        dt_bias=jnp.stack([g["dt_vec"][0, 4:8] for g in grads]), dn_norm=jnp.stack([g["dn_norm"][0] for g in grads]),
        gm_norm=jnp.stack([g["gm_norm"][0] for g in grads]), spatial_w=jnp.stack([g["spatial_w"] for g in grads]),
        spatial_b=jnp.stack([g["spatial_b"][:, 0, :] for g in grads]),
        sinks=jnp.stack([g["sink_vec"][0, :8] for g in grads]),
        loss=jnp.stack([loss_local, jnp.zeros((), F32)]).reshape(N_LAYER, 1))
    packed = _pack_small(small_local) + after_start
    gsmall_packed = _allreduce_small(packed.reshape(2, -1, 1024)).reshape(-1, 1024)

    d_s, m_s, v_s = _ew("adamw_small", _adamw_fn, [_pack_small(w), gsmall_packed, _pack_small(m), _pack_small(v)], 3)
    gsmall, dsmall, msmall, vsmall = (_unpack_small(p) for p in (gsmall_packed, d_s, m_s, v_s))
    g_o, d_o, m_o, v_o = dict(gsmall), dict(dsmall), dict(msmall), dict(vsmall)
    loss = gsmall["loss"][0, 0]
    m_t, v_t = tr(m["w_in"]), tr(v["w_in"])

    def update(l, totals, into):
        outs = {}
        for n, t in zip(_BIG, totals):
            g_l = t.reshape(local[n].shape[1:])
            if n == "w_in":
                outs[n] = _adamw_rows("adamw_" + n, l, w_t, g_l, m_t, v_t, into and into[n])
            else:
                outs[n] = _adamw_layer("adamw_" + n, l, w[n], g_l, m[n], v[n], into and into[n])
        return outs

    landed1[1] = landed1[1] + after_start.astype(landed1[1].dtype)
    outs1 = update(1, _pair_share(_chip_sums(landed1, pair1, me)), None)
    pair0, landed0 = _split_wait("scatter_l0_wait", "scatter", scattering[0], outs1["w_in"][0])
    outs = update(0, _pair_share(_chip_sums(landed0, pair0, me)), outs1)
    for n in _BIG:
        d_o[n], m_o[n], v_o[n], g_o[n] = [tr(o) for o in outs[n]] if n == "w_in" else outs[n]
    return (loss, dx[None], *[g_o[n] for n in _NAMES], *[d_o[n] for n in _NAMES], *[m_o[n] for n in _NAMES],
            *[v_o[n] for n in _NAMES])
```

```python
import collections
import functools

import jax
import jax.numpy as jnp
from jax import lax
from jax.experimental import pallas as pl
from jax.experimental.pallas import tpu as pltpu

F32 = jnp.float32
BF16 = jnp.bfloat16

D_MODEL = 1024
BRANCH_W = 512
MEM_LEN = 256
N_LAYER = 2
N_CHIP = 4
N_DEV = 8
EPS = 1e-6
NEG_INF = -1e30
DN_CHUNK = 64
LANES = 128
VMEM_LIMIT = 48 * 1024 * 1024

ADAM_LR, ADAM_B1, ADAM_B2, ADAM_EPS, ADAM_WD, ADAM_STEP = 0.001, 0.9, 0.999, 1e-08, 0.01, 10

N_PAD = 10240
O_GATE = 0
O_AQ, O_AK, O_AV, O_AZ = 4096, 4608, 5120, 5632
O_BUV, O_BZ = 6144, 7168
O_CKV, O_BA = 7680, 7936
O_CQ, O_CZ = 8192, 8704
O_MQ, O_MZ = 9216, 9728
O_MISC, W_MISC = O_CKV, 512
_PAD_SEGS = [(5896, 4096), (0, 512), (512, 512), (1024, 512), (1536, 512), (2056, 1024), (3080, 512),
             (4104, 128), (4232, 128), (2048, 8), (None, 120), (None, 128),
             (3592, 512), (4360, 512), (4872, 512), (5384, 512)]
D_IN = 9992
SHARD_IN = D_IN // N_CHIP


SHARD_PAD = 2560


def _pad_parts():
    parts, off = [], 0
    for s, n in _PAD_SEGS:
        a = s
        while s is not None and a < s + n:
            chip = a // SHARD_IN
            b = min(s + n, (chip + 1) * SHARD_IN)
            parts.append((chip, a - chip * SHARD_IN, off + a - s, b - a))
            a = b
        off += n
    return parts


PERM_ROWS = 512
PERM_SLACK = 32


def _permute_rows(name, src, parts, n_out, out_dtype, pair_split=False):
    B, Z = PERM_ROWS, PERM_ROWS + PERM_SLACK
    w = src.shape[1]
    plans = []
    for blk in range(n_out // B):
        o, runs = blk * B, []
        for s, d, n in parts:
            lo, hi = max(d, o), min(d + n, o + B)
            if lo < hi:
                s0 = s + lo - d
                wa = s0 // 16 * 16
                wb = min(-(-(s0 + hi - lo) // 16) * 16, src.shape[0])
                runs.append((wa, wb - wa, s0 - (lo - o) - wa, lo - o, hi - o))
        plans.append(runs)
    max_runs = max(len(r) for r in plans)
    nblk = len(plans)

    per_half = nblk // 2

    def body(*refs):
        src_ref, out_ref = refs[0], refs[1]
        if pair_split:
            theirs_ref, (inbuf, obuf, insem, outsem, to_sib_sem, from_sib_sem) = refs[2], refs[3:]
            x, y, c = _position()
            sibling = (x, y, 1 - c)
        else:
            inbuf, obuf, insem, outsem = refs[2:]

        def in_copies(blk):
            return [pltpu.make_async_copy(src_ref.at[pl.ds(wa, ws)], inbuf.at[blk % 2, r, pl.ds(0, ws)],
                                          insem.at[blk % 2, r]) for r, (wa, ws, _, _, _) in enumerate(plans[blk])]

        class out_copy:
            def __init__(self, blk):
                self.blk, self.rows = blk, pl.ds((blk % per_half if pair_split else blk) * B, B)
                self.local = pltpu.make_async_copy(obuf.at[blk % 2], out_ref.at[self.rows], outsem.at[blk % 2])

            def _both(self, local_op, remote_op):
                if not pair_split:
                    return local_op(self.local)
                mine = c == self.blk // per_half
                pl.when(mine)(lambda: local_op(self.local))
                pl.when(jnp.logical_not(mine))(lambda: remote_op(_remote(
                    obuf.at[self.blk % 2], theirs_ref.at[self.rows], to_sib_sem.at[self.blk % 2],
                    from_sib_sem.at[self.blk % per_half], sibling)))

            def start(self):
                self._both(lambda cp: cp.start(), lambda cp: cp.start())

            def wait(self):
                self._both(lambda cp: cp.wait(), lambda cp: cp.wait_send())

        for cp in in_copies(0):
            cp.start()
        rid = _iota((B, 1), 0)
        for blk in range(nblk):
            if blk + 1 < nblk:
                for cp in in_copies(blk + 1):
                    cp.start()
            for cp in in_copies(blk):
                cp.wait()
            val = jnp.zeros((B, w), F32)
            for r, (wa, ws, t, l0, l1) in enumerate(plans[blk]):
                win = jnp.concatenate([inbuf[blk % 2, r, pl.ds(0, ws)].astype(F32), jnp.zeros((Z - ws, w), F32)], axis=0)
                moved = pltpu.roll(win, (-t) % Z, 0)[:B]
                val = jnp.where((rid >= l0) & (rid < l1), moved, val)
            if blk >= 2:
                out_copy(blk - 2).wait()
            obuf[blk % 2] = val.astype(out_dtype)
            out_copy(blk).start()
        for blk in range(max(nblk - 2, 0), nblk):
            out_copy(blk).wait()
        if pair_split:
            for i in range(per_half):
                rows = theirs_ref.at[pl.ds(i * B, B)]
                _remote(rows, rows, to_sib_sem.at[0], from_sib_sem.at[i], sibling).wait_recv()

    scratch = [pltpu.VMEM((2, max_runs, Z, w), src.dtype), pltpu.VMEM((2, B, w), out_dtype),
               pltpu.SemaphoreType.DMA((2, max_runs)), pltpu.SemaphoreType.DMA((2,))]
    if pair_split:
        scratch += [pltpu.SemaphoreType.DMA((2,)), pltpu.SemaphoreType.DMA((per_half,))]
    out_shape = jax.ShapeDtypeStruct((n_out // 2 if pair_split else n_out, w), out_dtype)
    return pl.pallas_call(
        body, name=name, in_specs=[_ANY], out_specs=[_ANY] * 2 if pair_split else _ANY,
        out_shape=[out_shape] * 2 if pair_split else out_shape, scratch_shapes=scratch, compiler_params=_params(),
    )(src)


def _slab_parts():
    h, out = SHARD_PAD // 2, []
    for chip, s, d, n in _pad_parts():
        a = s
        while a < s + n:
            half = a // h
            b = min(s + n, (half + 1) * h)
            out.append(((half * N_CHIP + chip) * h + a - half * h, d + a - s, b - a))
            a = b
    return out


def _w_pad_from_slabs(slabs):
    return _permute_rows("w_pad_rows", slabs.reshape(-1, slabs.shape[-1]), _slab_parts(), N_PAD, BF16)


def _slabs_from_pad(dw):
    mine, theirs = _permute_rows("w_pad_grad_rows", dw, [(d, s, n) for s, d, n in _slab_parts()],
                                 N_CHIP * SHARD_PAD, BF16, pair_split=True)
    shape = (N_CHIP, SHARD_PAD // 2, dw.shape[1])
    return mine.reshape(shape), theirs.reshape(shape)


def _dot(a, b, dims, prec):
    if prec == "bf16":
        return lax.dot_general(a.astype(BF16), b.astype(BF16), (dims, ((), ())), preferred_element_type=F32)
    return lax.dot_general(a, b, (dims, ((), ())), precision=lax.Precision.HIGHEST, preferred_element_type=F32)


_NN, _NT, _TN = ((1,), (0,)), ((1,), (1,)), ((0,), (0,))


def _make_mm(prec):
    @jax.custom_vjp
    def nn(a, b):
        return _dot(a, b, _NN, prec)

    @jax.custom_vjp
    def nt(a, b):
        return _dot(a, b, _NT, prec)

    @jax.custom_vjp
    def tn(a, b):
        return _dot(a, b, _TN, prec)

    nn.defvjp(lambda a, b: (nn(a, b), (a, b)), lambda r, g: (nt(g, r[1]), tn(r[0], g)))
    nt.defvjp(lambda a, b: (nt(a, b), (a, b)), lambda r, g: (nn(g, r[1]), tn(g, r[0])))
    tn.defvjp(lambda a, b: (tn(a, b), (a, b)), lambda r, g: (nt(r[1], g), nn(r[0], g)))
    return nn, nt, tn


_nn16, _nt16, _tn16 = _make_mm("bf16")
_nn32, _nt32, _tn32 = _make_mm("f32")


def _make_slice(axis):
    @functools.partial(jax.custom_vjp, nondiff_argnums=(1, 2, 3))
    def sl(x, a, b, n):
        return x[a:b] if axis == 0 else x[:, a:b]

    def fwd(x, a, b, n):
        return sl(x, a, b, n), None

    def bwd(a, b, n, _, g):
        parts = []
        if a > 0:
            parts.append(jnp.zeros((a, g.shape[1]) if axis == 0 else (g.shape[0], a), g.dtype))
        parts.append(g)
        if n - b > 0:
            parts.append(jnp.zeros((n - b, g.shape[1]) if axis == 0 else (g.shape[0], n - b), g.dtype))
        return (jnp.concatenate(parts, axis=axis),)

    sl.defvjp(fwd, bwd)
    return sl


_sl0, _sl1 = _make_slice(0), _make_slice(1)


def _rowsl(x, a, b):
    return _sl0(x, a, b, x.shape[0])


def _cols(x, a, b):
    return _sl1(x, a, b, x.shape[1])


@functools.partial(jax.custom_vjp, nondiff_argnums=(1,))
def _rollr(x, s):
    return pltpu.roll(x, s, 0)


_rollr.defvjp(lambda x, s: (_rollr(x, s), None),
              lambda s, _, g: (pltpu.roll(g, g.shape[0] - s, 0),))


def _iota(shape, axis):
    return lax.broadcasted_iota(jnp.int32, shape, axis)


def _sigmoid(x):
    return lax.logistic(x)


def _silu(x):
    return x * _sigmoid(x)


def _gelu(x):
    return 0.5 * x * (1.0 + jnp.tanh(0.7978845608028654 * (x + 0.044715 * (x * x * x))))


def _softplus(x):
    return jnp.maximum(x, 0.0) + jnp.log(1.0 + jnp.exp(-jnp.abs(x)))


def _rms(x, g):
    return x * lax.rsqrt(jnp.mean(x * x, axis=-1, keepdims=True) + EPS) * g


def _lane_pick(x, lane):
    return jnp.sum(x * (_iota((1, x.shape[1]), 1) == lane).astype(F32), axis=1, keepdims=True)


Row = collections.namedtuple("Row", "arr w cb hb grad", defaults=(0, True))


def _full_spec(shape):
    return pl.BlockSpec(shape, lambda i, _n=len(shape): (0,) * _n)


def _load_params(refs):
    return [[p[g].astype(F32) for g in range(p.shape[0])] if len(p.shape) == 3 else p[...].astype(F32)
            for p in refs]


def _params(**kw):
    return pltpu.CompilerParams(vmem_limit_bytes=VMEM_LIMIT, **kw)


def _rows_fwd(name, fn, rows, params, outs, tr, carry=None):
    T = rows[0].arr.shape[0]
    n = T // tr
    halos = [r for r in rows if r.hb]
    nr, nh, npar, no = len(rows), len(halos), len(params), len(outs)

    def body(*refs):
        row_refs, halo_refs = refs[:nr], refs[nr:nr + nh]
        par_refs = refs[nr + nh:nr + nh + npar]
        out_refs = refs[nr + nh + npar:nr + nh + npar + no]
        rest = refs[nr + nh + npar + no:]
        first = pl.program_id(0) == 0
        cvals = None
        if carry is not None:
            csave_ref, carry_ref = rest

            @pl.when(first)
            def _():
                carry_ref[...] = jnp.zeros_like(carry_ref)

            cvals = [carry_ref[g] for g in range(carry[0])]
            for g in range(carry[0]):
                csave_ref[0, g] = cvals[g]
        c_out, o = fn(first, cvals, [r[...].astype(F32) for r in row_refs],
                      [h[...].astype(F32) for h in halo_refs], _load_params(par_refs))
        for r, v in zip(out_refs, o):
            r[...] = v.astype(r.dtype)
        if carry is not None:
            for g in range(carry[0]):
                carry_ref[g] = c_out[g]

    in_specs = [pl.BlockSpec((tr, r.w), lambda i, c=r.cb: (i, c)) for r in rows]
    in_specs += [pl.BlockSpec((r.hb, r.w), lambda i, c=r.cb, q=tr // r.hb: (jnp.maximum(i * q - 1, 0), c))
                 for r in halos]
    in_specs += [_full_spec(p.shape) for p in params]
    out_shape = [jax.ShapeDtypeStruct((T, w), dt) for w, dt in outs]
    out_specs = [pl.BlockSpec((tr, w), lambda i: (i, 0)) for w, _ in outs]
    scratch = []
    if carry is not None:
        out_shape.append(jax.ShapeDtypeStruct((n,) + carry, F32))
        out_specs.append(pl.BlockSpec((1,) + carry, lambda i: (i, 0, 0, 0)))
        scratch.append(pltpu.VMEM(carry, F32))
    return pl.pallas_call(
        body, name=name, grid=(n,), in_specs=in_specs, out_specs=out_specs, out_shape=out_shape,
        scratch_shapes=scratch, compiler_params=_params(dimension_semantics=("arbitrary",)),
    )(*[r.arr for r in rows], *[r.arr for r in halos], *params)


def _rows_bwd(name, fn, rows, params, douts, tr, carry=None, csave=None, dcols=None):
    T = rows[0].arr.shape[0]
    n = T // tr
    halos = [r for r in rows if r.hb]
    grows = [r for r in rows if r.grad is True]
    crows = [r for r in rows if r.grad == "cols"]
    wcols = sum(r.w for r in crows)
    nr, nh, npar, nd, ng = len(rows), len(halos), len(params), len(douts), len(grows)
    nc = 0 if carry is None else 1
    ncol = 1 if crows else 0
    nalias = 1 if (crows and dcols is not None) else 0

    def body(*refs):
        row_refs, halo_refs = refs[:nr], refs[nr:nr + nh]
        par_refs = refs[nr + nh:nr + nh + npar]
        k = nr + nh + npar
        csave_ref = refs[k] if nc else None
        dout_refs = refs[k + nc:k + nc + nd]
        k = k + nc + nd + nalias
        drow_refs = refs[k:k + ng]
        dcols_ref = refs[k + ng] if ncol else None
        dpar_refs = refs[k + ng + ncol:k + ng + ncol + npar]
        k = k + ng + ncol + npar
        dcarry_ref = refs[k] if nc else None
        hgrad_refs = refs[k + nc:]
        i = pl.program_id(0)
        first_tile = i == n - 1

        @pl.when(i == 0)
        def _():
            for r in dpar_refs:
                r[...] = jnp.zeros_like(r)
            for r in hgrad_refs:
                r[...] = jnp.zeros_like(r)
            if nc:
                dcarry_ref[...] = jnp.zeros_like(dcarry_ref)

        rv = [r[...].astype(F32) for r in row_refs]
        hv = [h[...].astype(F32) for h in halo_refs]
        pv = _load_params(par_refs)
        dov = [d[...].astype(F32) for d in dout_refs]
        if nc:
            cv = [csave_ref[0, g] for g in range(carry[0])]
            _, vjp = jax.vjp(lambda c, r, h, p: fn(first_tile, c, r, h, p), cv, rv, hv, pv)
            dc, dr, dh, dp = vjp(([dcarry_ref[g] for g in range(carry[0])], dov))
            for g in range(carry[0]):
                dcarry_ref[g] = dc[g]
        else:
            _, vjp = jax.vjp(lambda r, h, p: fn(first_tile, None, r, h, p)[1], rv, hv, pv)
            dr, dh, dp = vjp(dov)
        gi = hi = 0
        pieces = []
        for kk, r in enumerate(rows):
            d = dr[kk]
            if r.hb:
                carried = hgrad_refs[hi][...]
                d = d + (carried if tr == r.hb else
                         jnp.concatenate([jnp.zeros((tr - r.hb, r.w), F32), carried], axis=0))
                hgrad_refs[hi][...] = dh[hi]
                hi += 1
            if r.grad is True:
                drow_refs[gi][...] = d.astype(drow_refs[gi].dtype)
                gi += 1
            elif r.grad == "cols":
                pieces.append(d.astype(BF16))
        if ncol:
            dcols_ref[...] = pieces[0] if len(pieces) == 1 else jnp.concatenate(pieces, axis=1)
        for r, d in zip(dpar_refs, dp):
            if len(r.shape) == 3:
                for g in range(r.shape[0]):
                    r[g] += d[g]
            else:
                r[...] += d

    rev = lambda i: n - 1 - i
    in_specs = [pl.BlockSpec((tr, r.w), lambda i, c=r.cb: (rev(i), c)) for r in rows]
    in_specs += [pl.BlockSpec((r.hb, r.w), lambda i, c=r.cb, q=tr // r.hb: (jnp.maximum(rev(i) * q - 1, 0), c))
                 for r in halos]
    in_specs += [_full_spec(p.shape) for p in params]
    args = [r.arr for r in rows] + [r.arr for r in halos] + list(params)
    scratch = []
    if nc:
        in_specs.append(pl.BlockSpec((1,) + carry, lambda i: (rev(i), 0, 0, 0)))
        args.append(csave)
        scratch.append(pltpu.VMEM(carry, F32))
    douts = [d if isinstance(d, Row) else Row(d, d.shape[1], 0) for d in douts]
    in_specs += [pl.BlockSpec((tr, d.w), lambda i, c=d.cb: (rev(i), c)) for d in douts]
    args += [d.arr for d in douts]
    aliases = {}
    if nalias:
        aliases = {len(args): ng}
        in_specs.append(pl.BlockSpec(memory_space=pl.ANY))
        args.append(dcols)
    scratch += [pltpu.VMEM((r.hb, r.w), F32) for r in halos]
    out_shape = [jax.ShapeDtypeStruct((T, r.w), F32) for r in grows]
    out_specs = [pl.BlockSpec((tr, r.w), lambda i: (rev(i), 0)) for r in grows]
    if ncol:
        off = crows[0].cb * crows[0].w
        assert off % wcols == 0 and all(a.cb * a.w + a.w == b.cb * b.w for a, b in zip(crows, crows[1:]))
        out_shape.append(jax.ShapeDtypeStruct((T, N_PAD), BF16))
        out_specs.append(pl.BlockSpec((tr, wcols), lambda i, c=off // wcols: (rev(i), c)))
    out_shape += [jax.ShapeDtypeStruct(p.shape, F32) for p in params]
    out_specs += [_full_spec(p.shape) for p in params]
    res = pl.pallas_call(
        body, name=name, grid=(n,), in_specs=in_specs, out_specs=out_specs, out_shape=out_shape,
        scratch_shapes=scratch, input_output_aliases=aliases,
        compiler_params=_params(dimension_semantics=("arbitrary",)),
    )(*args)
    return list(res[:ng]), list(res[ng + ncol:]), (res[ng] if ncol else dcols)


def _fill_misc(dcols, dkv, dba, tr):
    T = dkv.shape[0]

    def body(kv_ref, ba_ref, _, o_ref):
        o_ref[...] = jnp.concatenate([kv_ref[...], ba_ref[...]], axis=1).astype(BF16)

    return pl.pallas_call(
        body, name="misc_bwd", grid=(T // tr,),
        in_specs=[pl.BlockSpec((tr, 256), lambda i: (i, 0)), pl.BlockSpec((tr, 256), lambda i: (i, 0)),
                  pl.BlockSpec(memory_space=pl.ANY)],
        out_specs=pl.BlockSpec((tr, W_MISC), lambda i: (i, O_MISC // W_MISC)),
        out_shape=jax.ShapeDtypeStruct((T, N_PAD), BF16), input_output_aliases={2: 0},
        compiler_params=_params(dimension_semantics=("arbitrary",)),
    )(dkv, dba, dcols)


def _up_bwd(ys, cols, dm, w_up):
    T, tr = dm.shape[0], UPB_TR

    def body(y_ref, gl_ref, dm_ref, w_ref, dy_ref, dgl_ref, dw_ref):
        @pl.when(pl.program_id(1) == 0)
        def _():
            dw_ref[...] = jnp.zeros_like(dw_ref)

        _, vjp = jax.vjp(lambda y, gl, w: _sigmoid(gl) * _nn16(y, w),
                         y_ref[...].astype(F32), gl_ref[...].astype(F32), w_ref[...].astype(F32))
        dy, dgl, dw = vjp(dm_ref[...])
        dy_ref[...] = dy
        dgl_ref[...] = dgl.astype(BF16)
        dw_ref[...] += dw

    branch_rows = lambda w: pl.BlockSpec((tr, w), lambda n, i: (i, n))
    weight = pl.BlockSpec((None, BRANCH_W, D_MODEL), lambda n, i: (n, 0, 0))
    return pl.pallas_call(
        body, name="up_bwd", grid=(4, T // tr),
        in_specs=[branch_rows(BRANCH_W), branch_rows(D_MODEL), pl.BlockSpec((tr, D_MODEL), lambda n, i: (i, 0)), weight],
        out_specs=[branch_rows(BRANCH_W), branch_rows(D_MODEL), weight],
        out_shape=[jax.ShapeDtypeStruct((T, 4 * BRANCH_W), F32), jax.ShapeDtypeStruct((T, N_PAD), BF16),
                   jax.ShapeDtypeStruct(w_up.shape, F32)],
        compiler_params=_params(dimension_semantics=("arbitrary", "arbitrary")),
    )(ys, cols, dm, w_up)


def _matmul(name, a, b, kind, out_dtype, tm, tn, tk, after=None):
    if kind == "tn":
        (K, M), N = a.shape, b.shape[1]
    else:
        (M, K), N = a.shape, (b.shape[0] if kind == "nt" else b.shape[1])
    tm, tn, tk = min(tm, M), min(tn, N), min(tk, K)
    nk = K // tk
    dims = {"nn": _NN, "nt": _NT, "tn": _TN}[kind]

    n_after = 0 if after is None else 1

    def body(*refs):
        a_ref, b_ref, o_ref, acc = refs[0], refs[1], refs[2 + n_after], refs[3 + n_after:]
        part = lax.dot_general(a_ref[...], b_ref[...], (dims, ((), ())), preferred_element_type=F32)
        if nk == 1:
            o_ref[...] = part.astype(o_ref.dtype)
            return
        acc_ref = acc[0] if acc else o_ref
        k = pl.program_id(2)

        @pl.when(k == 0)
        def _():
            acc_ref[...] = part

        @pl.when(k > 0)
        def _():
            acc_ref[...] += part

        if acc:
            @pl.when(k == nk - 1)
            def _():
                o_ref[...] = acc_ref[...].astype(o_ref.dtype)

    a_spec = pl.BlockSpec((tk, tm), lambda i, j, k: (k, i)) if kind == "tn" else pl.BlockSpec((tm, tk), lambda i, j, k: (i, k))
    b_spec = pl.BlockSpec((tn, tk), lambda i, j, k: (j, k)) if kind == "nt" else pl.BlockSpec((tk, tn), lambda i, j, k: (k, j))
    return pl.pallas_call(
        body, name=name, grid=(M // tm, N // tn, nk), in_specs=[a_spec, b_spec] + [_ANY] * n_after,
        out_specs=pl.BlockSpec((tm, tn), lambda i, j, k: (i, j)),
        out_shape=jax.ShapeDtypeStruct((M, N), out_dtype),
        scratch_shapes=[pltpu.VMEM((tm, tn), F32)] if nk > 1 and out_dtype != F32 else [],
        compiler_params=_params(dimension_semantics=("arbitrary", "arbitrary", "arbitrary")),
    )(a, b, *([] if after is None else [after]))


def _pre_fn(first, _, rows, halos, params):
    return None, [_rms(rows[0], params[0])]


def _pre_fn_res(first, _, rows, halos, params):
    return None, [_rms(rows[0], params[0]), rows[0]]


def _memkv_fn(first, _, rows, halos, params):
    g, w = params
    return None, [_nn16(_rms(rows[0], g), w)]


def _conv_silu(x, halo, w4, keep_halo):
    tr, hb = x.shape[0], halo.shape[0]
    xh = jnp.concatenate([halo * keep_halo, x], axis=0)
    acc = w4[3] * x
    for s in (1, 2, 3):
        acc = acc + w4[3 - s] * _rowsl(_rollr(xh, s), hb, hb + tr)
    return _silu(acc)


def _dn_fn(first, S, rows, halos, params):
    qp, kp, vp, z, ba = rows
    conv, a_vec, dt_vec, dnorm = params
    ba = _cols(ba, 0, LANES)
    tr = qp.shape[0]
    keep = jnp.where(first, 0.0, 1.0)
    q = _conv_silu(qp, halos[0], [conv[3 * j + 0] for j in range(4)], keep)
    k = _conv_silu(kp, halos[1], [conv[3 * j + 1] for j in range(4)], keep)
    v = _conv_silu(vp, halos[2], [conv[3 * j + 2] for j in range(4)], keep)
    qh, kh, vh = [], [], []
    for h in range(4):
        a, b = h * LANES, (h + 1) * LANES
        xq, xk = _cols(q, a, b), _cols(k, a, b)
        qh.append(xq * lax.rsqrt(jnp.sum(xq * xq, axis=1, keepdims=True) + EPS) * (LANES ** -0.5))
        kh.append(xk * lax.rsqrt(jnp.sum(xk * xk, axis=1, keepdims=True) + EPS))
        vh.append(_cols(v, a, b))
    beta_all = _sigmoid(ba)
    g_all = -jnp.exp(a_vec) * _softplus(ba + dt_vec)
    C = DN_CHUNK
    ii, jj = _iota((C, C), 0), _iota((C, C), 1)
    strict, incl = ii > jj, ii >= jj
    eye = (ii == jj).astype(F32)
    last_row = (_iota((C, 1), 0) == C - 1).astype(F32)
    n_chunk = tr // C
    pairs = [(c, h) for c in range(n_chunk) for h in range(4)]
    rows_of = lambda a, c: _rowsl(a, c * C, (c + 1) * C)
    gcs = [_nn32(incl.astype(F32), rows_of(g_all, c)) for c in range(n_chunk)]
    qc = {(c, h): rows_of(qh[h], c) for c, h in pairs}
    kc = {(c, h): rows_of(kh[h], c) for c, h in pairs}
    beta = {(c, h): _lane_pick(rows_of(beta_all, c), h) for c, h in pairs}
    gc = {(c, h): _lane_pick(gcs[c], 4 + h) for c, h in pairs}
    dec = {p: jnp.exp(jnp.where(incl, gc[p] - jnp.sum(eye * gc[p], axis=0, keepdims=True), 0.0)) for p in pairs}
    egc = {p: jnp.exp(gc[p]) for p in pairs}
    kb = {p: kc[p] * beta[p] for p in pairs}
    kq = {p: _nt16(jnp.concatenate([kb[p], qc[p]], axis=0), kc[p]) for p in pairs}
    P = {p: -jnp.where(strict, _rowsl(kq[p], 0, C) * dec[p], 0.0) for p in pairs}
    aqk = {p: jnp.where(incl, _rowsl(kq[p], C, 2 * C) * dec[p], 0.0) for p in pairs}
    tinv = {p: eye + P[p] for p in pairs}
    P = {p: _nn16(P[p], P[p]) for p in pairs}
    for j in range(5):
        if j < 4:
            pt = {p: _nn16(jnp.concatenate([P[p], tinv[p]], axis=0), P[p]) for p in pairs}
            tinv = {p: tinv[p] + _rowsl(pt[p], C, 2 * C) for p in pairs}
            P = {p: _rowsl(pt[p], 0, C) for p in pairs}
        else:
            tinv = {p: tinv[p] + _nn16(tinv[p], P[p]) for p in pairs}
    uw = {(c, h): _nn16(tinv[c, h], jnp.concatenate([rows_of(vh[h], c) * beta[c, h], kb[c, h] * egc[c, h]], axis=1))
          for c, h in pairs}
    S = list(S)
    ychunks = []
    for c in range(n_chunk):
        zc = rows_of(z, c)
        hs = range(4)
        ws = [_nn16(jnp.concatenate([_cols(uw[c, h], LANES, 2 * LANES), qc[c, h] * egc[c, h]], axis=0), S[h]) for h in hs]
        vnew = [_cols(uw[c, h], 0, LANES) - _rowsl(ws[h], 0, C) for h in hs]
        o = [_rowsl(ws[h], C, 2 * C) + _nn16(aqk[c, h], vnew[h]) for h in hs]
        glast = [jnp.sum(gc[c, h] * last_row, axis=0, keepdims=True) for h in hs]
        S = [S[h] * jnp.exp(glast[h]) + _tn16(kc[c, h] * jnp.exp(glast[h] - gc[c, h]), vnew[h]) for h in hs]
        ychunks.append(jnp.concatenate(
            [_rms(o[h], dnorm) * _silu(_cols(zc, h * LANES, (h + 1) * LANES)) for h in hs], axis=1))
    return S, [jnp.concatenate(ychunks, axis=0)]


def _gm_fn(first, _, rows, halos, params):
    uv, z = rows
    gnorm, ws, bs = params
    tr = uv.shape[0]
    guv = _gelu(uv)
    u = _cols(guv, 0, BRANCH_W)
    v = _rms(_cols(guv, BRANCH_W, 2 * BRANCH_W), gnorm)
    ii, jj = _iota((LANES, LANES), 0), _iota((LANES, LANES), 1)
    eye = (ii == jj).astype(F32)
    wsm = [jnp.where(ii >= jj, ws[g], 0.0) for g in range(4)]
    bcol = [jnp.sum(eye * bs[g], axis=1, keepdims=True) for g in range(4)]
    chunks = []
    for c in range(tr // LANES):
        vc = _rowsl(v, c * LANES, (c + 1) * LANES)
        chunks.append(jnp.concatenate(
            [_nn16(wsm[g], _cols(vc, g * LANES, (g + 1) * LANES)) + bcol[g] for g in range(4)], axis=1))
    return None, [u * jnp.concatenate(chunks, axis=0) * _silu(z)]


def _swa_fn(first, _, rows, halos, params):
    q, kvc, z = rows
    sink_vec = params[0]
    P = LANES
    n_blk = q.shape[0] // P
    r, cc = _iota((P, P), 0), _iota((P, P), 1)
    lane = _iota((1, P), 1)
    key = _iota((P, 2 * P), 1)
    dist = _iota((P, 2 * P), 0) + P - key
    in_window = (dist >= 0) & (dist < P)
    valid = [in_window & (key >= jnp.where(first, P, 0))] + [in_window] * (n_blk - 1)
    halves = [(lane < 64).astype(F32), (lane >= 64).astype(F32)]
    dup = [(r == kh * 64 + (cc & 63)).astype(F32) for kh in range(2)]
    kv_blk = [halos[0]] + [_rowsl(kvc, b * P, (b + 1) * P) for b in range(n_blk)]
    pairs = [(b, kh) for b in range(n_blk) for kh in range(2)]
    kkvv = {}
    for b in range(n_blk):
        kv = jnp.concatenate([kv_blk[b], kv_blk[b + 1]], axis=0)
        k_v = jnp.concatenate([_cols(kv, 0, P), _cols(kv, P, 2 * P)], axis=0)
        for kh in range(2):
            kkvv[b, kh] = _nn16(k_v, dup[kh])
    scores = {}
    for b, kh in pairs:
        q_b = _rowsl(q, b * P, (b + 1) * P)
        stacked = jnp.concatenate([_cols(q_b, (2 * kh + g // 2) * P, (2 * kh + g // 2 + 1) * P) * halves[g % 2]
                                   for g in range(4)], axis=0)
        scores[b, kh] = _nt16(stacked, _rowsl(kkvv[b, kh], 0, 2 * P))
    probs = {}
    for b, kh in pairs:
        ps = []
        for g in range(4):
            s = jnp.where(valid[b], _rowsl(scores[b, kh], g * P, (g + 1) * P) * 0.125, NEG_INF)
            sink = _lane_pick(sink_vec, kh * 4 + g)
            m = lax.stop_gradient(jnp.maximum(jnp.max(s, axis=1, keepdims=True), sink))
            e = jnp.exp(s - m)
            ps.append(e / (jnp.sum(e, axis=1, keepdims=True) + jnp.exp(sink - m)))
        probs[b, kh] = jnp.concatenate(ps, axis=0)
    outs = {p: _nn16(probs[p], _rowsl(kkvv[p], 2 * P, 4 * P)) for p in pairs}
    tile = [jnp.concatenate([_rowsl(outs[b, j // 2], (2 * (j % 2)) * P, (2 * (j % 2) + 1) * P) * halves[0]
                             + _rowsl(outs[b, j // 2], (2 * (j % 2) + 1) * P, (2 * (j % 2) + 2) * P) * halves[1]
                             for j in range(4)], axis=1) for b in range(n_blk)]
    return None, [jnp.concatenate(tile, axis=0) * _silu(z)]


def _mem_fn(first, _, rows, halos, params):
    q, z = rows
    mkv = params[0]
    heads = [(h * LANES, (h + 1) * LANES) for h in range(4)]
    scores = [_nt16(_cols(q, a, b), _cols(mkv, a, b)) * (LANES ** -0.5) for a, b in heads]
    probs = []
    for s in scores:
        e = jnp.exp(s - lax.stop_gradient(jnp.max(s, axis=1, keepdims=True)))
        probs.append(e / jnp.sum(e, axis=1, keepdims=True))
    outs = [_nn16(p, _cols(mkv, BRANCH_W + a, BRANCH_W + b)) for p, (a, b) in zip(probs, heads)]
    return None, [jnp.concatenate(outs, axis=1) * _silu(z)]


def _up_fn(first, _, rows, halos, params):
    ys, gl, w_up = rows[:4], rows[4], params[0]
    merged = None
    for n in range(4):
        term = _sigmoid(_cols(gl, n * D_MODEL, (n + 1) * D_MODEL)) * _nn16(ys[n], w_up[n])
        merged = term if merged is None else merged + term
    return None, [merged]


def _out_fn(first, _, rows, halos, params):
    x, merged = rows
    w, g = params
    return None, [x + _rms(_nn16(merged, w), g)]


def _loss_fn(first, _, rows, halos, params):
    y, t = rows
    d = y - t
    lrow = 0.5 * jnp.mean(d * d, axis=1, keepdims=True)
    return None, [d * (1.0 / D_MODEL), jnp.broadcast_to(lrow, (y.shape[0], LANES))]


TR = 256
BIG_TR = 512
SWA_TR = 256
DN_TR = 256
UP_TR = 256
UPB_TR = 512
CONV_HALO = 16
CARRY = (4, LANES, LANES)


def _branch_rows(cols, g):
    hb = CONV_HALO
    a = [Row(cols, 512, O_AQ // 512, hb, g), Row(cols, 512, O_AK // 512, hb, g), Row(cols, 512, O_AV // 512, hb, g),
         Row(cols, 512, O_AZ // 512, 0, g), Row(cols, 256, O_BA // 256)]
    b = [Row(cols, 1024, O_BUV // 1024, 0, g), Row(cols, 512, O_BZ // 512, 0, g)]
    c = [Row(cols, 512, O_CQ // 512, 0, g), Row(cols, 256, O_CKV // 256, LANES), Row(cols, 512, O_CZ // 512, 0, g)]
    m = [Row(cols, 512, O_MQ // 512, 0, g), Row(cols, 512, O_MZ // 512, 0, g)]
    return a, b, c, m


def _layer_fwd(x, mem, W, late_weights=None):
    h = _rows_fwd("prenorm_fwd", _pre_fn, [Row(x, D_MODEL, 0)], [W["norm_pre"]], [(D_MODEL, BF16)], BIG_TR)[0]
    cols = _matmul("in_proj_fwd", h, W["w_pad"], "nt", BF16, 2048, 1024, 1024)
    if late_weights is not None:
        W = dict(W, **late_weights(cols))
    mem_kv = _rows_fwd("memkv_fwd", _memkv_fn, [Row(mem, D_MODEL, 0)], [W["norm_mem"], W["w_mem_kv"]],
                       [(D_MODEL, F32)], MEM_LEN)[0]
    ra, rb, rc, rm = _branch_rows(cols, True)
    y_a, csave = _rows_fwd("dn_fwd", _dn_fn, ra, [W["conv"], W["a_vec"], W["dt_vec"], W["dn_norm"]],
                           [(BRANCH_W, BF16)], DN_TR, CARRY)
    y_b = _rows_fwd("gm_fwd", _gm_fn, rb, [W["gm_norm"], W["spatial_w"], W["spatial_b"]], [(BRANCH_W, BF16)], BIG_TR)[0]
    y_c = _rows_fwd("swa_fwd", _swa_fn, rc, [W["sink_vec"]], [(BRANCH_W, BF16)], SWA_TR)[0]
    y_m = _rows_fwd("mem_fwd", _mem_fn, rm, [mem_kv], [(BRANCH_W, BF16)], BIG_TR)[0]
    ys = [y_a, y_b, y_c, y_m]
    merged = _rows_fwd("up_fwd", _up_fn, [Row(y, BRANCH_W, 0) for y in ys] + [Row(cols, 4 * D_MODEL, 0)],
                       [W["w_up"]], [(D_MODEL, BF16)], UP_TR)[0]
    x_new = _rows_fwd("out_fwd", _out_fn, [Row(x, D_MODEL, 0), Row(merged, D_MODEL, 0)],
                      [W["w_out"], W["norm_post"]], [(D_MODEL, F32)], TR)[0]
    return x_new, dict(x=x, h=h, cols=cols, mem_kv=mem_kv, csave=csave, ys=ys, merged=merged), W


def _layer_bwd(dxn, mem, W, sv, on_weight_grads=None):
    x, cols = sv["x"], sv["cols"]
    (dx_res, dm), (dw_out, dnorm_post), _ = _rows_bwd(
        "out_bwd", _out_fn, [Row(x, D_MODEL, 0), Row(sv["merged"], D_MODEL, 0)], [W["w_out"], W["norm_post"]],
        [dxn], TR)
    dys, dcols, dw_up = _up_bwd(jnp.concatenate(sv["ys"], axis=1), cols, dm, W["w_up"])
    dys = [Row(dys, BRANCH_W, n) for n in range(4)]
    ra, rb, rc, rm = _branch_rows(cols, "cols")
    (dba,), (dconv, da_vec, ddt_vec, ddn_norm), dcols = _rows_bwd(
        "dn_bwd", _dn_fn, ra, [W["conv"], W["a_vec"], W["dt_vec"], W["dn_norm"]], [dys[0]], DN_TR, CARRY,
        sv["csave"], dcols=dcols)
    _, (dgm_norm, dws, dbs), dcols = _rows_bwd(
        "gm_bwd", _gm_fn, rb, [W["gm_norm"], W["spatial_w"], W["spatial_b"]], [dys[1]], BIG_TR, dcols=dcols)
    (dkv_c,), (dsink,), dcols = _rows_bwd("swa_bwd", _swa_fn, rc, [W["sink_vec"]], [dys[2]], SWA_TR, dcols=dcols)
    _, (dmem_kv,), dcols = _rows_bwd("mem_bwd", _mem_fn, rm, [sv["mem_kv"]], [dys[3]], BIG_TR, dcols=dcols)
    dcols = _fill_misc(dcols, dkv_c, dba, BIG_TR)
    _, (dnorm_mem, dw_mem_kv), _ = _rows_bwd("memkv_bwd", _memkv_fn, [Row(mem, D_MODEL, 0, 0, False)],
                                             [W["norm_mem"], W["w_mem_kv"]], [dmem_kv], MEM_LEN)
    dw_pad = _matmul("in_proj_dw", dcols, sv["h"], "tn", BF16, 1024, 1024, 2048)
    grads = dict(norm_post=dnorm_post, norm_mem=dnorm_mem, w_pad=dw_pad, conv=dconv,
                 a_vec=da_vec, dt_vec=ddt_vec, dn_norm=ddn_norm, gm_norm=dgm_norm, spatial_w=dws, spatial_b=dbs,
                 sink_vec=dsink, w_mem_kv=dw_mem_kv, w_up=dw_up, w_out=dw_out)
    started = None if on_weight_grads is None else on_weight_grads(grads)
    dh = _matmul("in_proj_dx", dcols, W["w_pad"], "nn", F32, 1024, 1024, 2048, after=started)
    (dx,), (grads["norm_pre"],), _ = _rows_bwd("prenorm_bwd", _pre_fn_res, [Row(x, D_MODEL, 0)], [W["norm_pre"]],
                                               [dh, dx_res], BIG_TR)
    return dx, grads


def _lane_vec(v, off):
    return jnp.zeros((1, LANES), F32).at[0, off:off + v.shape[0]].set(v)


def _layer_weights(l, w_pad, conv_w, small, **late):
    return dict(
        late, w_pad=w_pad, conv=conv_w.reshape(4, 3, BRANCH_W).reshape(12, 1, BRANCH_W),
        norm_pre=small["norm_pre"][l][None], norm_post=small["norm_post"][l][None],
        norm_mem=small["norm_mem"][l][None],
        a_vec=_lane_vec(small["a_log"][l], 4), dt_vec=_lane_vec(small["dt_bias"][l], 4),
        dn_norm=small["dn_norm"][l][None], gm_norm=small["gm_norm"][l][None],
        spatial_w=small["spatial_w"][l], spatial_b=small["spatial_b"][l][:, None, :],
        sink_vec=_lane_vec(small["sinks"][l], 0))


_MESH = pl.DeviceIdType.MESH
_ANY = pl.BlockSpec(memory_space=pl.ANY)


def _position():
    return lax.axis_index("x"), lax.axis_index("y"), lax.axis_index("c")


def _remote(src, dst, send_sem, recv_sem, dev):
    return pltpu.make_async_remote_copy(src_ref=src, dst_ref=dst, send_sem=send_sem, recv_sem=recv_sem,
                                        device_id=dev, device_id_type=_MESH)


def _hbm_call(name, body, arrs, out_shapes, sems, aliases=None):
    return pl.pallas_call(
        body, name=name, in_specs=[_ANY] * len(arrs), out_specs=[_ANY] * len(out_shapes), out_shape=out_shapes,
        scratch_shapes=[pltpu.SemaphoreType.DMA((k,)) for k in sems], input_output_aliases=aliases or {},
        compiler_params=pltpu.CompilerParams(has_side_effects=True),
    )(*arrs)


def _other_chips(x, y):
    return [(1 - x, y), (x, 1 - y), (1 - x, 1 - y)]


def _gather_weights(arrs, relayed):
    n = len(arrs)

    def body(*refs):
        ins, outs = refs[:n], refs[n:2 * n]
        ici_send, ici_recv, d2d_send, d2d_recv = refs[2 * n:]
        x, y, c = _position()
        me = 2 * x + y
        xn, yn, dg = _other_chips(x, y)
        chip = lambda p: 2 * p[0] + p[1]
        sends = []

        def go(cp):
            cp.start()
            sends.append(cp)

        def ici(a, j, src, dst, to):
            return _remote(src, dst, ici_send.at[4 * a + j], ici_recv.at[4 * a + j], (*to, c))

        for a in range(n):
            go(ici(a, 0, ins[a].at[c], outs[a].at[c, me], xn))
            go(ici(a, 1, ins[a].at[c], outs[a].at[c, me], yn))
            if not relayed[a]:
                go(ici(a, 2, ins[a].at[c], outs[a].at[c, me], dg))
        for a in range(n):
            h = arrs[a].shape[1] // 2
            from_x, from_y = outs[a].at[c, chip(xn)], outs[a].at[c, chip(yn)]
            ici(a, 0, ins[a].at[c], from_x, xn).wait_recv()
            if relayed[a]:
                go(ici(a, 2, from_x.at[pl.ds(0, h)], from_x.at[pl.ds(0, h)], yn))
            ici(a, 1, ins[a].at[c], from_y, yn).wait_recv()
            if relayed[a]:
                go(ici(a, 3, from_y.at[pl.ds(h, h)], from_y.at[pl.ds(h, h)], xn))
            for j, slab in enumerate((from_x, from_y)):
                go(_remote(slab, slab, d2d_send.at[3 * a + j], d2d_recv.at[3 * a + j], (x, y, 1 - c)))
        for a in range(n):
            h = arrs[a].shape[1] // 2
            from_d = outs[a].at[c, chip(dg)]
            if relayed[a]:
                ici(a, 2, from_d.at[pl.ds(0, h)], from_d.at[pl.ds(0, h)], yn).wait_recv()
                ici(a, 3, from_d.at[pl.ds(h, h)], from_d.at[pl.ds(h, h)], xn).wait_recv()
            else:
                ici(a, 2, ins[a].at[c], from_d, dg).wait_recv()
            go(_remote(from_d, from_d, d2d_send.at[3 * a + 2], d2d_recv.at[3 * a + 2], (x, y, 1 - c)))
        for a in range(n):
            for j, p in enumerate((xn, yn, dg)):
                slab = outs[a].at[1 - c, chip(p)]
                _remote(slab, slab, d2d_send.at[3 * a + j], d2d_recv.at[3 * a + j], (x, y, 1 - c)).wait_recv()
        for cp in sends:
            cp.wait_send()

    return _hbm_call("gather_weights", body, arrs,
                     [jax.ShapeDtypeStruct((N_LAYER, N_CHIP) + a.shape[1:], a.dtype) for a in arrs],
                     [4 * n, 4 * n, 3 * n, 3 * n])


def _pair_exchange(arrs):
    n = len(arrs)

    def body(*refs):
        ins, outs = refs[:n], refs[n:2 * n]
        send_sems, recv_sems = refs[2 * n:]
        x, y, c = _position()
        cps = [_remote(ins[a].at[1 - c], outs[a], send_sems.at[a], recv_sems.at[a], (x, y, 1 - c)) for a in range(n)]
        for cp in cps:
            cp.start()
        for cp in cps:
            cp.wait_recv()
        for cp in cps:
            cp.wait_send()

    return _hbm_call("pair_exchange", body, arrs, [jax.ShapeDtypeStruct(a.shape[1:], a.dtype) for a in arrs], [n, n])


def _chip_scatter(arrs):
    n = len(arrs)

    def body(*refs):
        ins, outs = refs[:n], refs[n:2 * n]
        send_sems, recv_sems = refs[2 * n:]
        x, y, c = _position()
        me = 2 * x + y
        sends = []
        for a in range(n):
            for j, (px, py) in enumerate(_other_chips(x, y)):
                sends.append(_remote(ins[a].at[2 * px + py], outs[a].at[me], send_sems.at[3 * a + j],
                                     recv_sems.at[3 * a + j], (px, py, c)))
                sends[-1].start()
        for a in range(n):
            for j, (px, py) in enumerate(_other_chips(x, y)):
                _remote(ins[a].at[me], outs[a].at[2 * px + py], send_sems.at[3 * a + j], recv_sems.at[3 * a + j],
                        (px, py, c)).wait_recv()
        for cp in sends:
            cp.wait_send()

    return _hbm_call("chip_scatter", body, arrs, [jax.ShapeDtypeStruct(a.shape, a.dtype) for a in arrs],
                     [3 * n, 3 * n])


def _pair_share(arrs):
    n = len(arrs)

    def body(*refs):
        ins, outs = refs[:n], refs[n:2 * n]
        send_sems, recv_sems = refs[2 * n:]
        x, y, c = _position()
        cps = [_remote(ins[a].at[c], outs[a].at[c], send_sems.at[a], recv_sems.at[a], (x, y, 1 - c)) for a in range(n)]
        for cp in cps:
            cp.start()
        for a in range(n):
            _remote(ins[a].at[c], outs[a].at[1 - c], send_sems.at[a], recv_sems.at[a], (x, y, 1 - c)).wait_recv()
        for cp in cps:
            cp.wait_send()

    return _hbm_call("pair_share", body, arrs, [jax.ShapeDtypeStruct(a.shape, a.dtype) for a in arrs], [n, n],
                     {a: a for a in range(n)})


def _pair_forward(arrs):
    n = len(arrs)

    def body(*refs):
        ins, outs = refs[:n], refs[n:2 * n]
        send_sems, recv_sems = refs[2 * n:]
        x, y, c = _position()
        sends = []
        for a in range(n):
            for j, (px, py) in enumerate(_other_chips(x, y)):
                sends.append(_remote(ins[a].at[c, 2 * px + py], outs[a].at[c, 2 * px + py], send_sems.at[3 * a + j],
                                     recv_sems.at[3 * a + j], (x, y, 1 - c)))
                sends[-1].start()
        for a in range(n):
            for j, (px, py) in enumerate(_other_chips(x, y)):
                slab = outs[a].at[1 - c, 2 * px + py]
                _remote(slab, slab, send_sems.at[3 * a + j], recv_sems.at[3 * a + j], (x, y, 1 - c)).wait_recv()
        for cp in sends:
            cp.wait_send()

    return _hbm_call("pair_forward", body, arrs, [jax.ShapeDtypeStruct(a.shape, a.dtype) for a in arrs],
                     [3 * n, 3 * n], {a: a for a in range(n)})


_HBM = pl.BlockSpec(memory_space=pltpu.HBM)
_SEM = pl.BlockSpec(memory_space=pltpu.SEMAPHORE)
_EFFECT = pltpu.SideEffectType.DATAFLOW_SIDE_EFFECTING


def _chip_copies(kind, srcs, lands, send_sems, recv_sems):
    x, y, c = _position()
    me = 2 * x + y
    sends, recvs = [], []
    for a in range(len(srcs)):
        for j, (px, py) in enumerate(_other_chips(x, y)):
            s, sems, dev = 2 * px + py, (send_sems.at[3 * a + j], recv_sems.at[3 * a + j]), (px, py, c)
            if kind == "gather":
                sends.append(_remote(srcs[a].at[c], lands[a].at[c, me], *sems, dev))
                recvs.append(_remote(srcs[a].at[c], lands[a].at[c, s], *sems, dev))
            else:
                sends.append(_remote(srcs[a].at[s], lands[a].at[me], *sems, dev))
                recvs.append(_remote(srcs[a].at[me], lands[a].at[s], *sems, dev))
    return sends, recvs


def _split_start(name, kind, srcs, land_shapes, after):
    n = len(srcs)

    def body(*refs):
        sends, _ = _chip_copies(kind, refs[:n], refs[n:2 * n], refs[2 * n + 1], refs[2 * n + 2])
        for cp in sends:
            cp.start()
        refs[-1][...] = jnp.zeros_like(refs[-1])

    hbm = lambda a: pltpu.with_memory_space_constraint(a, pltpu.HBM)
    lands = [lax.empty(s.shape, s.dtype) for s in land_shapes]
    outs = pl.pallas_call(
        body, name=name, in_specs=[_HBM] * (2 * n) + [_ANY],
        out_specs=[_SEM, _SEM] + [_HBM] * (2 * n) + [pl.BlockSpec(memory_space=pltpu.VMEM)],
        out_shape=[pltpu.SemaphoreType.DMA((3 * n,)), pltpu.SemaphoreType.DMA((3 * n,))]
        + [pltpu.HBM(a.shape, a.dtype) for a in list(srcs) + lands] + [jax.ShapeDtypeStruct((8, LANES), F32)],
        input_output_aliases={i: 2 + i for i in range(2 * n)},
        compiler_params=pltpu.CompilerParams(has_side_effects=_EFFECT),
    )(*[hbm(a) for a in srcs], *[hbm(a) for a in lands], after)
    return outs[0], outs[1], list(outs[2:2 + 2 * n]), outs[-1]


def _split_wait(name, kind, started, after):
    send_sems, recv_sems, thru, _ = started
    n = len(thru) // 2

    def body(*refs):
        sends, recvs = _chip_copies(kind, refs[:n], refs[n:2 * n], refs[2 * n], refs[2 * n + 1])
        for cp in sends:
            cp.wait_send()
        for cp in recvs:
            cp.wait_recv()

    outs = pl.pallas_call(
        body, name=name, in_specs=[_HBM] * (2 * n) + [_SEM, _SEM, _ANY], out_specs=[_HBM] * (2 * n),
        out_shape=[pltpu.HBM(a.shape, a.dtype) for a in thru], input_output_aliases={i: i for i in range(2 * n)},
        compiler_params=pltpu.CompilerParams(has_side_effects=_EFFECT),
    )(*thru, send_sems, recv_sems, after)
    return list(outs[:n]), list(outs[n:])


def _allreduce_small(g):
    def body(g_ref, o_ref, pair_buf, chip_buf, send_sems, recv_sems):
        x, y, c = _position()
        me = 2 * x + y
        sib = (x, y, 1 - c)
        to_sib = _remote(g_ref.at[1 - c], pair_buf, send_sems.at[0], recv_sems.at[0], sib)
        to_sib.start()
        to_sib.wait_recv()
        chip_buf[me] = g_ref[c] + pair_buf[...]
        sends = [to_sib]
        chips = _other_chips(x, y)
        for j, (px, py) in enumerate(chips):
            sends.append(_remote(chip_buf.at[me], chip_buf.at[me], send_sems.at[1 + j], recv_sems.at[1 + j], (px, py, c)))
            sends[-1].start()
        for j, (px, py) in enumerate(chips):
            _remote(chip_buf.at[me], chip_buf.at[2 * px + py], send_sems.at[1 + j], recv_sems.at[1 + j],
                    (px, py, c)).wait_recv()
        o_ref[c] = ((chip_buf[0] + chip_buf[1]) + chip_buf[2]) + chip_buf[3]
        sends.append(_remote(o_ref.at[c], o_ref.at[c], send_sems.at[4], recv_sems.at[4], sib))
        sends[-1].start()
        _remote(o_ref.at[c], o_ref.at[1 - c], send_sems.at[4], recv_sems.at[4], sib).wait_recv()
        for cp in sends:
            cp.wait_send()

    vmem = pl.BlockSpec(memory_space=pltpu.VMEM)
    return pl.pallas_call(
        body, name="allreduce_small", in_specs=[vmem], out_specs=vmem, out_shape=jax.ShapeDtypeStruct(g.shape, F32),
        scratch_shapes=[pltpu.VMEM(g.shape[1:], F32), pltpu.VMEM((N_CHIP,) + g.shape[1:], F32),
                        pltpu.SemaphoreType.DMA((5,)), pltpu.SemaphoreType.DMA((5,))],
        compiler_params=_params(),
    )(g)


EW_ROWS = 512


def _ew(name, fn, ins, n_out, out_dtype=F32, out_slot=None, into=None):
    def dims(a):
        return a[0].shape[1:] if isinstance(a, tuple) else a.shape

    R, w = dims(ins[0])
    tr = EW_ROWS if R % EW_ROWS == 0 else R
    n_into = len(into) if into else 0

    def body(c_ref, *refs):
        outs = fn(*[r[...] for r in refs[:len(ins)]])
        for r, v in zip(refs[len(ins) + n_into:], outs):
            r[...] = v.astype(r.dtype)

    def lead_spec(l):
        if l == "c":
            return pl.BlockSpec((None, tr, w), lambda i, c_ref: (c_ref[0], i, 0))
        return pl.BlockSpec((None, tr, w), lambda i, c_ref, s=l: (s, i, 0))

    plain = pl.BlockSpec((tr, w), lambda i, c_ref: (i, 0))
    in_specs = [lead_spec(a[1]) if isinstance(a, tuple) else plain for a in ins] + [_ANY] * n_into
    out_spec = plain if out_slot is None else lead_spec(out_slot)
    out_shape = jax.ShapeDtypeStruct((R, w) if out_slot is None else (2, R, w), out_dtype)
    return pl.pallas_call(
        body, name=name,
        grid_spec=pltpu.PrefetchScalarGridSpec(num_scalar_prefetch=1, grid=(R // tr,), in_specs=in_specs,
                                               out_specs=[out_spec] * n_out),
        out_shape=[out_shape] * n_out, input_output_aliases={1 + len(ins) + j: j for j in range(n_into)},
        compiler_params=_params(dimension_semantics=("arbitrary",)),
    )(lax.axis_index("c").astype(jnp.int32).reshape(1), *[a[0] if isinstance(a, tuple) else a for a in ins],
      *(into or []))


def _adamw_fn(w, g, m, v):
    m = ADAM_B1 * m + (1.0 - ADAM_B1) * g
    v = ADAM_B2 * v + (1.0 - ADAM_B2) * (g * g)
    m_hat = m / (1.0 - ADAM_B1 ** ADAM_STEP)
    v_hat = v / (1.0 - ADAM_B2 ** ADAM_STEP)
    delta = -ADAM_LR * (m_hat / (jnp.sqrt(v_hat) + ADAM_EPS) + ADAM_WD * w)
    return delta, m, v


def _adamw(name, w, g, m, v):
    shape = w.shape
    two = lambda a: a.reshape(-1, shape[-1])
    return [o.reshape(shape) for o in _ew(name, _adamw_fn, [two(w), two(g), two(m), two(v)], 3)]


def _adamw_layer(name, l, w, g, m, v, into):
    k = w.shape[-1]
    three = lambda a: (a.reshape(N_LAYER, -1, k), l)
    fn = lambda w_, g_, m_, v_: _adamw_fn(w_, g_, m_, v_) + (g_,)
    outs = _ew(name, fn, [three(w), g.reshape(-1, k), three(m), three(v)], 4, out_slot=l,
               into=None if into is None else [a.reshape(N_LAYER, -1, k) for a in into])
    return [o.reshape(w.shape) for o in outs]


def _adamw_rows(name, l, w, g, m, v, into):
    _, R, k = w.shape
    n_into = len(into) if into else 0

    def body(*refs):
        w_ref, g_ref, m_ref, v_ref = refs[:4]
        d_out, m_out, v_out, g_out = refs[4 + n_into:]
        g_blk = g_ref[...]
        d_out[...], m_out[...], v_out[...] = _adamw_fn(w_ref[...], g_blk, m_ref[...], v_ref[...])
        g_out[...] = g_blk

    spec = pl.BlockSpec((None, EW_ROWS, k), lambda i: (l, i, 0))
    return pl.pallas_call(
        body, name=name, grid=(-(-R // EW_ROWS),),
        in_specs=[spec, pl.BlockSpec((EW_ROWS, k), lambda i: (i, 0)), spec, spec] + [_ANY] * n_into,
        out_specs=[spec] * 4, out_shape=[jax.ShapeDtypeStruct((N_LAYER, R, k), F32)] * 4,
        input_output_aliases={4 + j: j for j in range(n_into)},
        compiler_params=_params(dimension_semantics=("arbitrary",)),
    )(w, g, m, v, *(into or []))


_SMALL = [("norm_pre", (2, 1024)), ("norm_post", (2, 1024)), ("norm_mem", (2, 1024)), ("a_log", (2, 4)),
          ("dt_bias", (2, 4)), ("dn_norm", (2, 128)), ("gm_norm", (2, 512)), ("spatial_w", (2, 4, 128, 128)),
          ("spatial_b", (2, 4, 128)), ("sinks", (2, 8)), ("loss", (2, 1))]
_SMALL_ROWS = 208
_BIG = ["w_in", "conv_w", "w_mem_kv", "w_up", "w_out"]
_NAMES = ["norm_pre", "norm_post", "norm_mem", "w_in", "conv_w", "a_log", "dt_bias", "dn_norm", "gm_norm",
          "spatial_w", "spatial_b", "sinks", "w_mem_kv", "w_up", "w_out"]


def _size(shape):
    n = 1
    for s in shape:
        n *= s
    return n


def _pack_small(d):
    rows = []
    for n, shp in _SMALL:
        a = d[n].reshape(N_LAYER, -1)
        rows.append(a.reshape(-1, 1024) if a.shape[1] > 1024 else jnp.pad(a, ((0, 6), (0, 1024 - a.shape[1]))))
    assert sum(r.shape[0] for r in rows) == _SMALL_ROWS
    return jnp.concatenate(rows, axis=0)


def _unpack_small(p):
    out, off = {}, 0
    for n, shp in _SMALL:
        c = _size(shp) // N_LAYER
        k = 8 if c <= 1024 else _size(shp) // 1024
        out[n] = (p[off:off + N_LAYER, :c] if c <= 1024 else p[off:off + k]).reshape(shp)
        off += k
    return out


_HALF_SHAPE = {"w_in": (SHARD_PAD // 2, D_MODEL), "conv_w": (2, 3 * BRANCH_W // N_CHIP), "w_mem_kv": (128, D_MODEL),
               "w_up": (2, BRANCH_W, D_MODEL // N_CHIP), "w_out": (128, D_MODEL)}


def _chip_major(g):
    g = jnp.swapaxes(g, 0, 1)
    return g.reshape((N_CHIP, 2 * g.shape[2]) + g.shape[3:])


def _half_major(g):
    g = g.reshape((N_CHIP, 2, g.shape[1] // 2) + g.shape[2:])
    return jnp.swapaxes(g, 0, 1).astype(BF16)


N_EARLY = 2


def _early_views(l, g_in, g_conv, small):
    return _layer_weights(l, _w_pad_from_slabs(g_in),
                          _chip_major(g_conv).transpose(1, 0, 2).reshape(4, 3 * BRANCH_W), small)


def _late_views(g_kv, g_up, g_out):
    return dict(w_mem_kv=_chip_major(g_kv).reshape(D_MODEL, D_MODEL),
                w_up=_chip_major(g_up).transpose(1, 2, 0, 3).reshape(4, BRANCH_W, D_MODEL),
                w_out=_chip_major(g_out).reshape(D_MODEL, D_MODEL))


def _pair_sums(g):
    add2 = lambda a, b: [a.astype(F32) + b.astype(F32)]
    mine, theirs = _slabs_from_pad(g["w_pad"])
    pair = [_ew("pair_sum_w_in", add2, [mine.reshape(-1, D_MODEL), theirs.reshape(-1, D_MODEL)], 1, BF16)[0]
            .reshape(mine.shape)]
    rest = [_half_major(g["conv"].reshape(4, N_CHIP, 3 * BRANCH_W // N_CHIP).transpose(1, 0, 2)),
            _half_major(g["w_mem_kv"].reshape(N_CHIP, D_MODEL // N_CHIP, D_MODEL)),
            _half_major(g["w_up"].reshape(4, BRANCH_W, N_CHIP, D_MODEL // N_CHIP).transpose(2, 0, 1, 3)),
            _half_major(g["w_out"].reshape(N_CHIP, D_MODEL // N_CHIP, D_MODEL))]
    for n, b, p in zip(_BIG[1:], rest, _pair_exchange(rest)):
        k = b.shape[-1]
        pair.append(_ew("pair_sum_" + n, add2, [(b.reshape(2, -1, k), "c"), p.reshape(-1, k)], 1, BF16)[0]
                    .reshape(p.shape))
    return pair


def _chip_sums(landed, pair, me):
    add4 = lambda a, b, c_, d: [((a.astype(F32) + b.astype(F32)) + c_.astype(F32)) + d.astype(F32)]
    totals = []
    for n, r, q in zip(_BIG, landed, pair):
        r = _own_slot(r, lax.dynamic_index_in_dim(q, me, 0), me, 0)
        k = r.shape[-1]
        totals.append(_ew("chip_sum_" + n, add4, [(r.reshape(N_CHIP, -1, k), s) for s in range(N_CHIP)], 1,
                          out_slot="c")[0].reshape((2,) + r.shape[1:]))
    return totals


def _own_slot(buf, mine, me, axis):
    return lax.dynamic_update_index_in_dim(buf, mine.astype(buf.dtype), me, axis)


def kernel(x, mem, norm_pre, norm_post, norm_mem, w_in, conv_w, a_log, dt_bias, dn_norm, gm_norm, spatial_w, spatial_b, sinks, w_mem_kv, w_up, w_out, loss_target, m_norm_pre, m_norm_post, m_norm_mem, m_w_in, m_conv_w, m_a_log, m_dt_bias, m_dn_norm, m_gm_norm, m_spatial_w, m_spatial_b, m_sinks, m_w_mem_kv, m_w_up, m_w_out, v_norm_pre, v_norm_post, v_norm_mem, v_w_in, v_conv_w, v_a_log, v_dt_bias, v_dn_norm, v_gm_norm, v_spatial_w, v_spatial_b, v_sinks, v_w_mem_kv, v_w_up, v_w_out):
    w = dict(norm_pre=norm_pre, norm_post=norm_post, norm_mem=norm_mem, w_in=w_in, conv_w=conv_w, a_log=a_log,
             dt_bias=dt_bias, dn_norm=dn_norm, gm_norm=gm_norm, spatial_w=spatial_w, spatial_b=spatial_b, sinks=sinks,
             w_mem_kv=w_mem_kv, w_up=w_up, w_out=w_out)
    m = dict(norm_pre=m_norm_pre, norm_post=m_norm_post, norm_mem=m_norm_mem, w_in=m_w_in, conv_w=m_conv_w,
             a_log=m_a_log, dt_bias=m_dt_bias, dn_norm=m_dn_norm, gm_norm=m_gm_norm, spatial_w=m_spatial_w,
             spatial_b=m_spatial_b, sinks=m_sinks, w_mem_kv=m_w_mem_kv, w_up=m_w_up, w_out=m_w_out)
    v = dict(norm_pre=v_norm_pre, norm_post=v_norm_post, norm_mem=v_norm_mem, w_in=v_w_in, conv_w=v_conv_w,
             a_log=v_a_log, dt_bias=v_dt_bias, dn_norm=v_dn_norm, gm_norm=v_gm_norm, spatial_w=v_spatial_w,
             spatial_b=v_spatial_b, sinks=v_sinks, w_mem_kv=v_w_mem_kv, w_up=v_w_up, w_out=v_w_out)
    me = 2 * lax.axis_index("x") + lax.axis_index("y")

    tr = lambda a: a.transpose(0, 2, 1)
    w_t = tr(w_in)
    w_in_t = jnp.pad(w_t.astype(BF16), ((0, 0), (0, SHARD_PAD - SHARD_IN), (0, 0)))
    for d in (w, m, v):
        d["loss"] = jnp.zeros((N_LAYER, 1), F32)
    local = dict(w_in=w_in_t, conv_w=conv_w, w_mem_kv=w_mem_kv.astype(BF16), w_up=w_up.astype(BF16),
                 w_out=w_out.astype(BF16))
    halves = lambda l: [local[n][l].reshape((2,) + _HALF_SHAPE[n]) for n in _BIG]
    own = lambda gathered, mine: [_own_slot(g, h[:, None], me, 1) for g, h in zip(gathered, mine)]
    lands = [jax.ShapeDtypeStruct((2, N_CHIP) + _HALF_SHAPE[n], local[n].dtype) for n in _BIG]
    h0 = halves(0)
    g0 = own(_gather_weights(h0[:N_EARLY], [True, False]), h0[:N_EARLY])
    rest0 = _split_start("gather_l0_rest_start", "gather", h0[N_EARLY:], lands[N_EARLY:], g0[1])
    started = _split_start("gather_l1_start", "gather", halves(1), lands, rest0[3])

    xl, meml = x[0], mem[0]
    W0 = _early_views(0, g0[0], g0[1], w)
    W0["norm_pre"] = W0["norm_pre"] + started[3][0, 0]

    def late0(cols):
        mine, landed = _split_wait("gather_l0_rest_wait", "gather", rest0, cols)
        return _late_views(*own(_pair_forward(landed), mine))

    x1, sv0, W0 = _layer_fwd(xl, meml, W0, late0)
    mine1, landed1 = _split_wait("gather_l1_wait", "gather", started, x1)
    g1 = own(_pair_forward(landed1), mine1)
    W1 = dict(_early_views(1, g1[0], g1[1], w), **_late_views(*g1[N_EARLY:]))
    x2, sv1, _ = _layer_fwd(x1, meml, W1)
    dy, lrows = _rows_fwd("loss", _loss_fn, [Row(x2, D_MODEL, 0), Row(loss_target[0], D_MODEL, 0)], [],
                          [(D_MODEL, F32), (LANES, F32)], BIG_TR)
    loss_local = jnp.sum(lrows[:, 0])

    scattering = {}

    def start_scatter(l):
        def on_weight_grads(g):
            pair = _pair_sums(g)
            scattering[l] = _split_start("scatter_l%d_start" % l, "scatter", pair,
                                         [jax.ShapeDtypeStruct(p.shape, p.dtype) for p in pair], pair[1])
            return scattering[l][3]
        return on_weight_grads

    dx1, grads1 = _layer_bwd(dy, meml, W1, sv1, start_scatter(1))
    dx, grads0 = _layer_bwd(dx1, meml, W0, sv0, start_scatter(0))
    pair1, landed1 = _split_wait("scatter_l1_wait", "scatter", scattering[1], dx)
    after_start = scattering[0][3][0, 0]
    grads = [grads0, grads1]

    small_local = dict(
        norm_pre=jnp.stack([g["norm_pre"][0] for g in grads]), norm_post=jnp.stack([g["norm_post"][0] for g in grads]),
        norm_mem=jnp.stack([g["norm_mem"][0] for g in grads]), a_log=jnp.stack([g["a_vec"][0, 4:8] for g in grads]),
        dt_bias=jnp.stack([g["dt_vec"][0, 4:8] for g in grads]), dn_norm=jnp.stack([g["dn_norm"][0] for g in grads]),
        gm_norm=jnp.stack([g["gm_norm"][0] for g in grads]), spatial_w=jnp.stack([g["spatial_w"] for g in grads]),
        spatial_b=jnp.stack([g["spatial_b"][:, 0, :] for g in grads]),
        sinks=jnp.stack([g["sink_vec"][0, :8] for g in grads]),
        loss=jnp.stack([loss_local, jnp.zeros((), F32)]).reshape(N_LAYER, 1))
    packed = _pack_small(small_local) + after_start
    gsmall_packed = _allreduce_small(packed.reshape(2, -1, 1024)).reshape(-1, 1024)

    d_s, m_s, v_s = _ew("adamw_small", _adamw_fn, [_pack_small(w), gsmall_packed, _pack_small(m), _pack_small(v)], 3)
    gsmall, dsmall, msmall, vsmall = (_unpack_small(p) for p in (gsmall_packed, d_s, m_s, v_s))
    g_o, d_o, m_o, v_o = dict(gsmall), dict(dsmall), dict(msmall), dict(vsmall)
    loss = gsmall["loss"][0, 0]
    m_t, v_t = tr(m["w_in"]), tr(v["w_in"])

    def update(l, totals, into):
        outs = {}
        for n, t in zip(_BIG, totals):
            g_l = t.reshape(local[n].shape[1:])
            if n == "w_in":
                outs[n] = _adamw_rows("adamw_" + n, l, w_t, g_l, m_t, v_t, into and into[n])
            else:
                outs[n] = _adamw_layer("adamw_" + n, l, w[n], g_l, m[n], v[n], into and into[n])
        return outs

    landed1[1] = landed1[1] + after_start.astype(landed1[1].dtype)
    outs1 = update(1, _pair_share(_chip_sums(landed1, pair1, me)), None)
    pair0, landed0 = _split_wait("scatter_l0_wait", "scatter", scattering[0], outs1["w_in"][0])
    outs = update(0, _pair_share(_chip_sums(landed0, pair0, me)), outs1)
    for n in _BIG:
        d_o[n], m_o[n], v_o[n], g_o[n] = [tr(o) for o in outs[n]] if n == "w_in" else outs[n]
    return (loss, dx[None], *[g_o[n] for n in _NAMES], *[d_o[n] for n in _NAMES], *[m_o[n] for n in _NAMES],
            *[v_o[n] for n in _NAMES])
```

```python
import collections
import functools

import jax
import jax.numpy as jnp
from jax import lax
from jax.experimental import pallas as pl
from jax.experimental.pallas import tpu as pltpu

F32 = jnp.float32
BF16 = jnp.bfloat16

D_MODEL = 1024
BRANCH_W = 512
MEM_LEN = 256
N_LAYER = 2
N_CHIP = 4
N_DEV = 8
EPS = 1e-6
NEG_INF = -1e30
DN_CHUNK = 64
LANES = 128
VMEM_LIMIT = 48 * 1024 * 1024

ADAM_LR, ADAM_B1, ADAM_B2, ADAM_EPS, ADAM_WD, ADAM_STEP = 0.001, 0.9, 0.999, 1e-08, 0.01, 10

N_PAD = 10240
O_GATE = 0
O_AQ, O_AK, O_AV, O_AZ = 4096, 4608, 5120, 5632
O_BUV, O_BZ = 6144, 7168
O_CKV, O_BA = 7680, 7936
O_CQ, O_CZ = 8192, 8704
O_MQ, O_MZ = 9216, 9728
O_MISC, W_MISC = O_CKV, 512
_PAD_SEGS = [(5896, 4096), (0, 512), (512, 512), (1024, 512), (1536, 512), (2056, 1024), (3080, 512),
             (4104, 128), (4232, 128), (2048, 8), (None, 120), (None, 128),
             (3592, 512), (4360, 512), (4872, 512), (5384, 512)]
D_IN = 9992
SHARD_IN = D_IN // N_CHIP


SHARD_PAD = 2560


def _pad_parts():
    parts, off = [], 0
    for s, n in _PAD_SEGS:
        a = s
        while s is not None and a < s + n:
            chip = a // SHARD_IN
            b = min(s + n, (chip + 1) * SHARD_IN)
            parts.append((chip, a - chip * SHARD_IN, off + a - s, b - a))
            a = b
        off += n
    return parts


PERM_ROWS = 1024
PERM_SLACK = 32


def _permute_rows(name, src, parts, n_out, out_dtype, pair_split=False):
    B, Z = PERM_ROWS, PERM_ROWS + PERM_SLACK
    w = src.shape[1]
    plans = []
    for blk in range(n_out // B):
        o, runs = blk * B, []
        for s, d, n in parts:
            lo, hi = max(d, o), min(d + n, o + B)
            if lo < hi:
                s0 = s + lo - d
                wa = s0 // 16 * 16
                wb = min(-(-(s0 + hi - lo) // 16) * 16, src.shape[0])
                runs.append((wa, wb - wa, s0 - (lo - o) - wa, lo - o, hi - o))
        plans.append(runs)
    max_runs = max(len(r) for r in plans)
    nblk = len(plans)

    per_half = nblk // 2

    def body(*refs):
        src_ref, out_ref = refs[0], refs[1]
        if pair_split:
            theirs_ref, (inbuf, obuf, insem, outsem, to_sib_sem, from_sib_sem) = refs[2], refs[3:]
            x, y, c = _position()
            sibling = (x, y, 1 - c)
        else:
            inbuf, obuf, insem, outsem = refs[2:]

        def in_copies(blk):
            return [pltpu.make_async_copy(src_ref.at[pl.ds(wa, ws)], inbuf.at[blk % 2, r, pl.ds(0, ws)],
                                          insem.at[blk % 2, r]) for r, (wa, ws, _, _, _) in enumerate(plans[blk])]

        class out_copy:
            def __init__(self, blk):
                self.blk, self.rows = blk, pl.ds((blk % per_half if pair_split else blk) * B, B)
                self.local = pltpu.make_async_copy(obuf.at[blk % 2], out_ref.at[self.rows], outsem.at[blk % 2])

            def _both(self, local_op, remote_op):
                if not pair_split:
                    return local_op(self.local)
                mine = c == self.blk // per_half
                pl.when(mine)(lambda: local_op(self.local))
                pl.when(jnp.logical_not(mine))(lambda: remote_op(_remote(
                    obuf.at[self.blk % 2], theirs_ref.at[self.rows], to_sib_sem.at[self.blk % 2],
                    from_sib_sem.at[self.blk % per_half], sibling)))

            def start(self):
                self._both(lambda cp: cp.start(), lambda cp: cp.start())

            def wait(self):
                self._both(lambda cp: cp.wait(), lambda cp: cp.wait_send())

        for cp in in_copies(0):
            cp.start()
        rid = _iota((B, 1), 0)
        for blk in range(nblk):
            if blk + 1 < nblk:
                for cp in in_copies(blk + 1):
                    cp.start()
            for cp in in_copies(blk):
                cp.wait()
            val = jnp.zeros((B, w), F32)
            for r, (wa, ws, t, l0, l1) in enumerate(plans[blk]):
                win = jnp.concatenate([inbuf[blk % 2, r, pl.ds(0, ws)].astype(F32), jnp.zeros((Z - ws, w), F32)], axis=0)
                moved = pltpu.roll(win, (-t) % Z, 0)[:B]
                val = jnp.where((rid >= l0) & (rid < l1), moved, val)
            if blk >= 2:
                out_copy(blk - 2).wait()
            obuf[blk % 2] = val.astype(out_dtype)
            out_copy(blk).start()
        for blk in range(max(nblk - 2, 0), nblk):
            out_copy(blk).wait()
        if pair_split:
            for i in range(per_half):
                rows = theirs_ref.at[pl.ds(i * B, B)]
                _remote(rows, rows, to_sib_sem.at[0], from_sib_sem.at[i], sibling).wait_recv()

    scratch = [pltpu.VMEM((2, max_runs, Z, w), src.dtype), pltpu.VMEM((2, B, w), out_dtype),
               pltpu.SemaphoreType.DMA((2, max_runs)), pltpu.SemaphoreType.DMA((2,))]
    if pair_split:
        scratch += [pltpu.SemaphoreType.DMA((2,)), pltpu.SemaphoreType.DMA((per_half,))]
    out_shape = jax.ShapeDtypeStruct((n_out // 2 if pair_split else n_out, w), out_dtype)
    return pl.pallas_call(
        body, name=name, in_specs=[_ANY], out_specs=[_ANY] * 2 if pair_split else _ANY,
        out_shape=[out_shape] * 2 if pair_split else out_shape, scratch_shapes=scratch, compiler_params=_params(),
    )(src)


def _slab_parts():
    h, out = SHARD_PAD // 2, []
    for chip, s, d, n in _pad_parts():
        a = s
        while a < s + n:
            half = a // h
            b = min(s + n, (half + 1) * h)
            out.append(((half * N_CHIP + chip) * h + a - half * h, d + a - s, b - a))
            a = b
    return out


def _w_pad_from_slabs(slabs):
    return _permute_rows("w_pad_rows", slabs.reshape(-1, slabs.shape[-1]), _slab_parts(), N_PAD, BF16)


def _slabs_from_pad(dw):
    mine, theirs = _permute_rows("w_pad_grad_rows", dw, [(d, s, n) for s, d, n in _slab_parts()],
                                 N_CHIP * SHARD_PAD, BF16, pair_split=True)
    shape = (N_CHIP, SHARD_PAD // 2, dw.shape[1])
    return mine.reshape(shape), theirs.reshape(shape)


def _dot(a, b, dims, prec):
    if prec == "bf16":
        return lax.dot_general(a.astype(BF16), b.astype(BF16), (dims, ((), ())), preferred_element_type=F32)
    return lax.dot_general(a, b, (dims, ((), ())), precision=lax.Precision.HIGHEST, preferred_element_type=F32)


_NN, _NT, _TN = ((1,), (0,)), ((1,), (1,)), ((0,), (0,))


def _make_mm(prec):
    @jax.custom_vjp
    def nn(a, b):
        return _dot(a, b, _NN, prec)

    @jax.custom_vjp
    def nt(a, b):
        return _dot(a, b, _NT, prec)

    @jax.custom_vjp
    def tn(a, b):
        return _dot(a, b, _TN, prec)

    nn.defvjp(lambda a, b: (nn(a, b), (a, b)), lambda r, g: (nt(g, r[1]), tn(r[0], g)))
    nt.defvjp(lambda a, b: (nt(a, b), (a, b)), lambda r, g: (nn(g, r[1]), tn(g, r[0])))
    tn.defvjp(lambda a, b: (tn(a, b), (a, b)), lambda r, g: (nt(r[1], g), nn(r[0], g)))
    return nn, nt, tn


_nn16, _nt16, _tn16 = _make_mm("bf16")
_nn32, _nt32, _tn32 = _make_mm("f32")


def _make_slice(axis):
    @functools.partial(jax.custom_vjp, nondiff_argnums=(1, 2, 3))
    def sl(x, a, b, n):
        return x[a:b] if axis == 0 else x[:, a:b]

    def fwd(x, a, b, n):
        return sl(x, a, b, n), None

    def bwd(a, b, n, _, g):
        parts = []
        if a > 0:
            parts.append(jnp.zeros((a, g.shape[1]) if axis == 0 else (g.shape[0], a), g.dtype))
        parts.append(g)
        if n - b > 0:
            parts.append(jnp.zeros((n - b, g.shape[1]) if axis == 0 else (g.shape[0], n - b), g.dtype))
        return (jnp.concatenate(parts, axis=axis),)

    sl.defvjp(fwd, bwd)
    return sl


_sl0, _sl1 = _make_slice(0), _make_slice(1)


def _rowsl(x, a, b):
    return _sl0(x, a, b, x.shape[0])


def _cols(x, a, b):
    return _sl1(x, a, b, x.shape[1])


@functools.partial(jax.custom_vjp, nondiff_argnums=(1,))
def _rollr(x, s):
    return pltpu.roll(x, s, 0)


_rollr.defvjp(lambda x, s: (_rollr(x, s), None),
              lambda s, _, g: (pltpu.roll(g, g.shape[0] - s, 0),))


def _iota(shape, axis):
    return lax.broadcasted_iota(jnp.int32, shape, axis)


def _sigmoid(x):
    return lax.logistic(x)


def _silu(x):
    return x * _sigmoid(x)


def _gelu(x):
    return 0.5 * x * (1.0 + jnp.tanh(0.7978845608028654 * (x + 0.044715 * (x * x * x))))


def _softplus(x):
    return jnp.maximum(x, 0.0) + jnp.log(1.0 + jnp.exp(-jnp.abs(x)))


def _rms(x, g):
    return x * lax.rsqrt(jnp.mean(x * x, axis=-1, keepdims=True) + EPS) * g


def _lane_pick(x, lane):
    return jnp.sum(x * (_iota((1, x.shape[1]), 1) == lane).astype(F32), axis=1, keepdims=True)


Row = collections.namedtuple("Row", "arr w cb hb grad", defaults=(0, True))


def _full_spec(shape):
    return pl.BlockSpec(shape, lambda i, _n=len(shape): (0,) * _n)


def _load_params(refs):
    return [[p[g].astype(F32) for g in range(p.shape[0])] if len(p.shape) == 3 else p[...].astype(F32)
            for p in refs]


def _params(**kw):
    return pltpu.CompilerParams(vmem_limit_bytes=VMEM_LIMIT, **kw)


def _rows_fwd(name, fn, rows, params, outs, tr, carry=None):
    T = rows[0].arr.shape[0]
    n = T // tr
    halos = [r for r in rows if r.hb]
    nr, nh, npar, no = len(rows), len(halos), len(params), len(outs)

    def body(*refs):
        row_refs, halo_refs = refs[:nr], refs[nr:nr + nh]
        par_refs = refs[nr + nh:nr + nh + npar]
        out_refs = refs[nr + nh + npar:nr + nh + npar + no]
        rest = refs[nr + nh + npar + no:]
        first = pl.program_id(0) == 0
        cvals = None
        if carry is not None:
            csave_ref, carry_ref = rest

            @pl.when(first)
            def _():
                carry_ref[...] = jnp.zeros_like(carry_ref)

            cvals = [carry_ref[g] for g in range(carry[0])]
            for g in range(carry[0]):
                csave_ref[0, g] = cvals[g]
        c_out, o = fn(first, cvals, [r[...].astype(F32) for r in row_refs],
                      [h[...].astype(F32) for h in halo_refs], _load_params(par_refs))
        for r, v in zip(out_refs, o):
            r[...] = v.astype(r.dtype)
        if carry is not None:
            for g in range(carry[0]):
                carry_ref[g] = c_out[g]

    in_specs = [pl.BlockSpec((tr, r.w), lambda i, c=r.cb: (i, c)) for r in rows]
    in_specs += [pl.BlockSpec((r.hb, r.w), lambda i, c=r.cb, q=tr // r.hb: (jnp.maximum(i * q - 1, 0), c))
                 for r in halos]
    in_specs += [_full_spec(p.shape) for p in params]
    out_shape = [jax.ShapeDtypeStruct((T, w), dt) for w, dt in outs]
    out_specs = [pl.BlockSpec((tr, w), lambda i: (i, 0)) for w, _ in outs]
    scratch = []
    if carry is not None:
        out_shape.append(jax.ShapeDtypeStruct((n,) + carry, F32))
        out_specs.append(pl.BlockSpec((1,) + carry, lambda i: (i, 0, 0, 0)))
        scratch.append(pltpu.VMEM(carry, F32))
    return pl.pallas_call(
        body, name=name, grid=(n,), in_specs=in_specs, out_specs=out_specs, out_shape=out_shape,
        scratch_shapes=scratch, compiler_params=_params(dimension_semantics=("arbitrary",)),
    )(*[r.arr for r in rows], *[r.arr for r in halos], *params)


def _rows_bwd(name, fn, rows, params, douts, tr, carry=None, csave=None, dcols=None):
    T = rows[0].arr.shape[0]
    n = T // tr
    halos = [r for r in rows if r.hb]
    grows = [r for r in rows if r.grad is True]
    crows = [r for r in rows if r.grad == "cols"]
    wcols = sum(r.w for r in crows)
    nr, nh, npar, nd, ng = len(rows), len(halos), len(params), len(douts), len(grows)
    nc = 0 if carry is None else 1
    ncol = 1 if crows else 0
    nalias = 1 if (crows and dcols is not None) else 0

    def body(*refs):
        row_refs, halo_refs = refs[:nr], refs[nr:nr + nh]
        par_refs = refs[nr + nh:nr + nh + npar]
        k = nr + nh + npar
        csave_ref = refs[k] if nc else None
        dout_refs = refs[k + nc:k + nc + nd]
        k = k + nc + nd + nalias
        drow_refs = refs[k:k + ng]
        dcols_ref = refs[k + ng] if ncol else None
        dpar_refs = refs[k + ng + ncol:k + ng + ncol + npar]
        k = k + ng + ncol + npar
        dcarry_ref = refs[k] if nc else None
        hgrad_refs = refs[k + nc:]
        i = pl.program_id(0)
        first_tile = i == n - 1

        @pl.when(i == 0)
        def _():
            for r in dpar_refs:
                r[...] = jnp.zeros_like(r)
            for r in hgrad_refs:
                r[...] = jnp.zeros_like(r)
            if nc:
                dcarry_ref[...] = jnp.zeros_like(dcarry_ref)

        rv = [r[...].astype(F32) for r in row_refs]
        hv = [h[...].astype(F32) for h in halo_refs]
        pv = _load_params(par_refs)
        dov = [d[...].astype(F32) for d in dout_refs]
        if nc:
            cv = [csave_ref[0, g] for g in range(carry[0])]
            _, vjp = jax.vjp(lambda c, r, h, p: fn(first_tile, c, r, h, p), cv, rv, hv, pv)
            dc, dr, dh, dp = vjp(([dcarry_ref[g] for g in range(carry[0])], dov))
            for g in range(carry[0]):
                dcarry_ref[g] = dc[g]
        else:
            _, vjp = jax.vjp(lambda r, h, p: fn(first_tile, None, r, h, p)[1], rv, hv, pv)
            dr, dh, dp = vjp(dov)
        gi = hi = 0
        pieces = []
        for kk, r in enumerate(rows):
            d = dr[kk]
            if r.hb:
                carried = hgrad_refs[hi][...]
                d = d + (carried if tr == r.hb else
                         jnp.concatenate([jnp.zeros((tr - r.hb, r.w), F32), carried], axis=0))
                hgrad_refs[hi][...] = dh[hi]
                hi += 1
            if r.grad is True:
                drow_refs[gi][...] = d.astype(drow_refs[gi].dtype)
                gi += 1
            elif r.grad == "cols":
                pieces.append(d.astype(BF16))
        if ncol:
            dcols_ref[...] = pieces[0] if len(pieces) == 1 else jnp.concatenate(pieces, axis=1)
        for r, d in zip(dpar_refs, dp):
            if len(r.shape) == 3:
                for g in range(r.shape[0]):
                    r[g] += d[g]
            else:
                r[...] += d

    rev = lambda i: n - 1 - i
    in_specs = [pl.BlockSpec((tr, r.w), lambda i, c=r.cb: (rev(i), c)) for r in rows]
    in_specs += [pl.BlockSpec((r.hb, r.w), lambda i, c=r.cb, q=tr // r.hb: (jnp.maximum(rev(i) * q - 1, 0), c))
                 for r in halos]
    in_specs += [_full_spec(p.shape) for p in params]
    args = [r.arr for r in rows] + [r.arr for r in halos] + list(params)
    scratch = []
    if nc:
        in_specs.append(pl.BlockSpec((1,) + carry, lambda i: (rev(i), 0, 0, 0)))
        args.append(csave)
        scratch.append(pltpu.VMEM(carry, F32))
    douts = [d if isinstance(d, Row) else Row(d, d.shape[1], 0) for d in douts]
    in_specs += [pl.BlockSpec((tr, d.w), lambda i, c=d.cb: (rev(i), c)) for d in douts]
    args += [d.arr for d in douts]
    aliases = {}
    if nalias:
        aliases = {len(args): ng}
        in_specs.append(pl.BlockSpec(memory_space=pl.ANY))
        args.append(dcols)
    scratch += [pltpu.VMEM((r.hb, r.w), F32) for r in halos]
    out_shape = [jax.ShapeDtypeStruct((T, r.w), F32) for r in grows]
    out_specs = [pl.BlockSpec((tr, r.w), lambda i: (rev(i), 0)) for r in grows]
    if ncol:
        off = crows[0].cb * crows[0].w
        assert off % wcols == 0 and all(a.cb * a.w + a.w == b.cb * b.w for a, b in zip(crows, crows[1:]))
        out_shape.append(jax.ShapeDtypeStruct((T, N_PAD), BF16))
        out_specs.append(pl.BlockSpec((tr, wcols), lambda i, c=off // wcols: (rev(i), c)))
    out_shape += [jax.ShapeDtypeStruct(p.shape, F32) for p in params]
    out_specs += [_full_spec(p.shape) for p in params]
    res = pl.pallas_call(
        body, name=name, grid=(n,), in_specs=in_specs, out_specs=out_specs, out_shape=out_shape,
        scratch_shapes=scratch, input_output_aliases=aliases,
        compiler_params=_params(dimension_semantics=("arbitrary",)),
    )(*args)
    return list(res[:ng]), list(res[ng + ncol:]), (res[ng] if ncol else dcols)


def _fill_misc(dcols, dkv, dba, tr):
    T = dkv.shape[0]

    def body(kv_ref, ba_ref, _, o_ref):
        o_ref[...] = jnp.concatenate([kv_ref[...], ba_ref[...]], axis=1).astype(BF16)

    return pl.pallas_call(
        body, name="misc_bwd", grid=(T // tr,),
        in_specs=[pl.BlockSpec((tr, 256), lambda i: (i, 0)), pl.BlockSpec((tr, 256), lambda i: (i, 0)),
                  pl.BlockSpec(memory_space=pl.ANY)],
        out_specs=pl.BlockSpec((tr, W_MISC), lambda i: (i, O_MISC // W_MISC)),
        out_shape=jax.ShapeDtypeStruct((T, N_PAD), BF16), input_output_aliases={2: 0},
        compiler_params=_params(dimension_semantics=("arbitrary",)),
    )(dkv, dba, dcols)


def _up_bwd(ys, cols, dm, w_up):
    T, tr = dm.shape[0], UPB_TR

    def body(y_ref, gl_ref, dm_ref, w_ref, dy_ref, dgl_ref, dw_ref):
        @pl.when(pl.program_id(1) == 0)
        def _():
            dw_ref[...] = jnp.zeros_like(dw_ref)

        _, vjp = jax.vjp(lambda y, gl, w: _sigmoid(gl) * _nn16(y, w),
                         y_ref[...].astype(F32), gl_ref[...].astype(F32), w_ref[...].astype(F32))
        dy, dgl, dw = vjp(dm_ref[...])
        dy_ref[...] = dy
        dgl_ref[...] = dgl.astype(BF16)
        dw_ref[...] += dw

    branch_rows = lambda w: pl.BlockSpec((tr, w), lambda n, i: (i, n))
    weight = pl.BlockSpec((None, BRANCH_W, D_MODEL), lambda n, i: (n, 0, 0))
    return pl.pallas_call(
        body, name="up_bwd", grid=(4, T // tr),
        in_specs=[branch_rows(BRANCH_W), branch_rows(D_MODEL), pl.BlockSpec((tr, D_MODEL), lambda n, i: (i, 0)), weight],
        out_specs=[branch_rows(BRANCH_W), branch_rows(D_MODEL), weight],
        out_shape=[jax.ShapeDtypeStruct((T, 4 * BRANCH_W), F32), jax.ShapeDtypeStruct((T, N_PAD), BF16),
                   jax.ShapeDtypeStruct(w_up.shape, F32)],
        compiler_params=_params(dimension_semantics=("arbitrary", "arbitrary")),
    )(ys, cols, dm, w_up)


def _matmul(name, a, b, kind, out_dtype, tm, tn, tk, after=None):
    if kind == "tn":
        (K, M), N = a.shape, b.shape[1]
    else:
        (M, K), N = a.shape, (b.shape[0] if kind == "nt" else b.shape[1])
    tm, tn, tk = min(tm, M), min(tn, N), min(tk, K)
    nk = K // tk
    dims = {"nn": _NN, "nt": _NT, "tn": _TN}[kind]

    n_after = 0 if after is None else 1

    def body(*refs):
        a_ref, b_ref, o_ref, acc = refs[0], refs[1], refs[2 + n_after], refs[3 + n_after:]
        part = lax.dot_general(a_ref[...], b_ref[...], (dims, ((), ())), preferred_element_type=F32)
        if nk == 1:
            o_ref[...] = part.astype(o_ref.dtype)
            return
        acc_ref = acc[0] if acc else o_ref
        k = pl.program_id(2)

        @pl.when(k == 0)
        def _():
            acc_ref[...] = part

        @pl.when(k > 0)
        def _():
            acc_ref[...] += part

        if acc:
            @pl.when(k == nk - 1)
            def _():
                o_ref[...] = acc_ref[...].astype(o_ref.dtype)

    a_spec = pl.BlockSpec((tk, tm), lambda i, j, k: (k, i)) if kind == "tn" else pl.BlockSpec((tm, tk), lambda i, j, k: (i, k))
    b_spec = pl.BlockSpec((tn, tk), lambda i, j, k: (j, k)) if kind == "nt" else pl.BlockSpec((tk, tn), lambda i, j, k: (k, j))
    return pl.pallas_call(
        body, name=name, grid=(M // tm, N // tn, nk), in_specs=[a_spec, b_spec] + [_ANY] * n_after,
        out_specs=pl.BlockSpec((tm, tn), lambda i, j, k: (i, j)),
        out_shape=jax.ShapeDtypeStruct((M, N), out_dtype),
        scratch_shapes=[pltpu.VMEM((tm, tn), F32)] if nk > 1 and out_dtype != F32 else [],
        compiler_params=_params(dimension_semantics=("arbitrary", "arbitrary", "arbitrary")),
    )(a, b, *([] if after is None else [after]))


def _pre_fn(first, _, rows, halos, params):
    return None, [_rms(rows[0], params[0])]


def _pre_fn_res(first, _, rows, halos, params):
    return None, [_rms(rows[0], params[0]), rows[0]]


def _memkv_fn(first, _, rows, halos, params):
    g, w = params
    return None, [_nn16(_rms(rows[0], g), w)]


def _conv_silu(x, halo, w4, keep_halo):
    tr, hb = x.shape[0], halo.shape[0]
    xh = jnp.concatenate([halo * keep_halo, x], axis=0)
    acc = w4[3] * x
    for s in (1, 2, 3):
        acc = acc + w4[3 - s] * _rowsl(_rollr(xh, s), hb, hb + tr)
    return _silu(acc)


def _dn_fn(first, S, rows, halos, params):
    qp, kp, vp, z, ba = rows
    conv, a_vec, dt_vec, dnorm = params
    ba = _cols(ba, 0, LANES)
    tr = qp.shape[0]
    keep = jnp.where(first, 0.0, 1.0)
    q = _conv_silu(qp, halos[0], [conv[3 * j + 0] for j in range(4)], keep)
    k = _conv_silu(kp, halos[1], [conv[3 * j + 1] for j in range(4)], keep)
    v = _conv_silu(vp, halos[2], [conv[3 * j + 2] for j in range(4)], keep)
    qh, kh, vh = [], [], []
    for h in range(4):
        a, b = h * LANES, (h + 1) * LANES
        xq, xk = _cols(q, a, b), _cols(k, a, b)
        qh.append(xq * lax.rsqrt(jnp.sum(xq * xq, axis=1, keepdims=True) + EPS) * (LANES ** -0.5))
        kh.append(xk * lax.rsqrt(jnp.sum(xk * xk, axis=1, keepdims=True) + EPS))
        vh.append(_cols(v, a, b))
    beta_all = _sigmoid(ba)
    g_all = -jnp.exp(a_vec) * _softplus(ba + dt_vec)
    C = DN_CHUNK
    ii, jj = _iota((C, C), 0), _iota((C, C), 1)
    strict, incl = ii > jj, ii >= jj
    eye = (ii == jj).astype(F32)
    last_row = (_iota((C, 1), 0) == C - 1).astype(F32)
    n_chunk = tr // C
    pairs = [(c, h) for c in range(n_chunk) for h in range(4)]
    rows_of = lambda a, c: _rowsl(a, c * C, (c + 1) * C)
    gcs = [_nn32(incl.astype(F32), rows_of(g_all, c)) for c in range(n_chunk)]
    qc = {(c, h): rows_of(qh[h], c) for c, h in pairs}
    kc = {(c, h): rows_of(kh[h], c) for c, h in pairs}
    beta = {(c, h): _lane_pick(rows_of(beta_all, c), h) for c, h in pairs}
    gc = {(c, h): _lane_pick(gcs[c], 4 + h) for c, h in pairs}
    dec = {p: jnp.exp(jnp.where(incl, gc[p] - jnp.sum(eye * gc[p], axis=0, keepdims=True), 0.0)) for p in pairs}
    egc = {p: jnp.exp(gc[p]) for p in pairs}
    kb = {p: kc[p] * beta[p] for p in pairs}
    kq = {p: _nt16(jnp.concatenate([kb[p], qc[p]], axis=0), kc[p]) for p in pairs}
    P = {p: -jnp.where(strict, _rowsl(kq[p], 0, C) * dec[p], 0.0) for p in pairs}
    aqk = {p: jnp.where(incl, _rowsl(kq[p], C, 2 * C) * dec[p], 0.0) for p in pairs}
    tinv = {p: eye + P[p] for p in pairs}
    P = {p: _nn16(P[p], P[p]) for p in pairs}
    for j in range(5):
        if j < 4:
            pt = {p: _nn16(jnp.concatenate([P[p], tinv[p]], axis=0), P[p]) for p in pairs}
            tinv = {p: tinv[p] + _rowsl(pt[p], C, 2 * C) for p in pairs}
            P = {p: _rowsl(pt[p], 0, C) for p in pairs}
        else:
            tinv = {p: tinv[p] + _nn16(tinv[p], P[p]) for p in pairs}
    uw = {(c, h): _nn16(tinv[c, h], jnp.concatenate([rows_of(vh[h], c) * beta[c, h], kb[c, h] * egc[c, h]], axis=1))
          for c, h in pairs}
    S = list(S)
    ychunks = []
    for c in range(n_chunk):
        zc = rows_of(z, c)
        hs = range(4)
        ws = [_nn16(jnp.concatenate([_cols(uw[c, h], LANES, 2 * LANES), qc[c, h] * egc[c, h]], axis=0), S[h]) for h in hs]
        vnew = [_cols(uw[c, h], 0, LANES) - _rowsl(ws[h], 0, C) for h in hs]
        o = [_rowsl(ws[h], C, 2 * C) + _nn16(aqk[c, h], vnew[h]) for h in hs]
        glast = [jnp.sum(gc[c, h] * last_row, axis=0, keepdims=True) for h in hs]
        S = [S[h] * jnp.exp(glast[h]) + _tn16(kc[c, h] * jnp.exp(glast[h] - gc[c, h]), vnew[h]) for h in hs]
        ychunks.append(jnp.concatenate(
            [_rms(o[h], dnorm) * _silu(_cols(zc, h * LANES, (h + 1) * LANES)) for h in hs], axis=1))
    return S, [jnp.concatenate(ychunks, axis=0)]


def _gm_fn(first, _, rows, halos, params):
    uv, z = rows
    gnorm, ws, bs = params
    tr = uv.shape[0]
    guv = _gelu(uv)
    u = _cols(guv, 0, BRANCH_W)
    v = _rms(_cols(guv, BRANCH_W, 2 * BRANCH_W), gnorm)
    ii, jj = _iota((LANES, LANES), 0), _iota((LANES, LANES), 1)
    eye = (ii == jj).astype(F32)
    wsm = [jnp.where(ii >= jj, ws[g], 0.0) for g in range(4)]
    bcol = [jnp.sum(eye * bs[g], axis=1, keepdims=True) for g in range(4)]
    chunks = []
    for c in range(tr // LANES):
        vc = _rowsl(v, c * LANES, (c + 1) * LANES)
        chunks.append(jnp.concatenate(
            [_nn16(wsm[g], _cols(vc, g * LANES, (g + 1) * LANES)) + bcol[g] for g in range(4)], axis=1))
    return None, [u * jnp.concatenate(chunks, axis=0) * _silu(z)]


def _swa_fn(first, _, rows, halos, params):
    q, kvc, z = rows
    sink_vec = params[0]
    P = LANES
    n_blk = q.shape[0] // P
    r, cc = _iota((P, P), 0), _iota((P, P), 1)
    lane = _iota((1, P), 1)
    key = _iota((P, 2 * P), 1)
    dist = _iota((P, 2 * P), 0) + P - key
    in_window = (dist >= 0) & (dist < P)
    valid = [in_window & (key >= jnp.where(first, P, 0))] + [in_window] * (n_blk - 1)
    halves = [(lane < 64).astype(F32), (lane >= 64).astype(F32)]
    dup = [(r == kh * 64 + (cc & 63)).astype(F32) for kh in range(2)]
    kv_blk = [halos[0]] + [_rowsl(kvc, b * P, (b + 1) * P) for b in range(n_blk)]
    pairs = [(b, kh) for b in range(n_blk) for kh in range(2)]
    kkvv = {}
    for b in range(n_blk):
        kv = jnp.concatenate([kv_blk[b], kv_blk[b + 1]], axis=0)
        k_v = jnp.concatenate([_cols(kv, 0, P), _cols(kv, P, 2 * P)], axis=0)
        for kh in range(2):
            kkvv[b, kh] = _nn16(k_v, dup[kh])
    scores = {}
    for b, kh in pairs:
        q_b = _rowsl(q, b * P, (b + 1) * P)
        stacked = jnp.concatenate([_cols(q_b, (2 * kh + g // 2) * P, (2 * kh + g // 2 + 1) * P) * halves[g % 2]
                                   for g in range(4)], axis=0)
        scores[b, kh] = _nt16(stacked, _rowsl(kkvv[b, kh], 0, 2 * P))
    probs = {}
    for b, kh in pairs:
        ps = []
        for g in range(4):
            s = jnp.where(valid[b], _rowsl(scores[b, kh], g * P, (g + 1) * P) * 0.125, NEG_INF)
            sink = _lane_pick(sink_vec, kh * 4 + g)
            m = lax.stop_gradient(jnp.maximum(jnp.max(s, axis=1, keepdims=True), sink))
            e = jnp.exp(s - m)
            ps.append(e / (jnp.sum(e, axis=1, keepdims=True) + jnp.exp(sink - m)))
        probs[b, kh] = jnp.concatenate(ps, axis=0)
    outs = {p: _nn16(probs[p], _rowsl(kkvv[p], 2 * P, 4 * P)) for p in pairs}
    tile = [jnp.concatenate([_rowsl(outs[b, j // 2], (2 * (j % 2)) * P, (2 * (j % 2) + 1) * P) * halves[0]
                             + _rowsl(outs[b, j // 2], (2 * (j % 2) + 1) * P, (2 * (j % 2) + 2) * P) * halves[1]
                             for j in range(4)], axis=1) for b in range(n_blk)]
    return None, [jnp.concatenate(tile, axis=0) * _silu(z)]


def _mem_fn(first, _, rows, halos, params):
    q, z = rows
    mkv = params[0]
    heads = [(h * LANES, (h + 1) * LANES) for h in range(4)]
    scores = [_nt16(_cols(q, a, b), _cols(mkv, a, b)) * (LANES ** -0.5) for a, b in heads]
    probs = []
    for s in scores:
        e = jnp.exp(s - lax.stop_gradient(jnp.max(s, axis=1, keepdims=True)))
        probs.append(e / jnp.sum(e, axis=1, keepdims=True))
    outs = [_nn16(p, _cols(mkv, BRANCH_W + a, BRANCH_W + b)) for p, (a, b) in zip(probs, heads)]
    return None, [jnp.concatenate(outs, axis=1) * _silu(z)]


def _up_fn(first, _, rows, halos, params):
    ys, gl, w_up = rows[:4], rows[4], params[0]
    merged = None
    for n in range(4):
        term = _sigmoid(_cols(gl, n * D_MODEL, (n + 1) * D_MODEL)) * _nn16(ys[n], w_up[n])
        merged = term if merged is None else merged + term
    return None, [merged]


def _out_fn(first, _, rows, halos, params):
    x, merged = rows
    w, g = params
    return None, [x + _rms(_nn16(merged, w), g)]


def _loss_fn(first, _, rows, halos, params):
    y, t = rows
    d = y - t
    lrow = 0.5 * jnp.mean(d * d, axis=1, keepdims=True)
    return None, [d * (1.0 / D_MODEL), jnp.broadcast_to(lrow, (y.shape[0], LANES))]


TR = 256
BIG_TR = 512
SWA_TR = 256
DN_TR = 256
UP_TR = 256
UPB_TR = 512
CONV_HALO = 16
CARRY = (4, LANES, LANES)


def _branch_rows(cols, g):
    hb = CONV_HALO
    a = [Row(cols, 512, O_AQ // 512, hb, g), Row(cols, 512, O_AK // 512, hb, g), Row(cols, 512, O_AV // 512, hb, g),
         Row(cols, 512, O_AZ // 512, 0, g), Row(cols, 256, O_BA // 256)]
    b = [Row(cols, 1024, O_BUV // 1024, 0, g), Row(cols, 512, O_BZ // 512, 0, g)]
    c = [Row(cols, 512, O_CQ // 512, 0, g), Row(cols, 256, O_CKV // 256, LANES), Row(cols, 512, O_CZ // 512, 0, g)]
    m = [Row(cols, 512, O_MQ // 512, 0, g), Row(cols, 512, O_MZ // 512, 0, g)]
    return a, b, c, m


def _layer_fwd(x, mem, W, late_weights=None):
    h = _rows_fwd("prenorm_fwd", _pre_fn, [Row(x, D_MODEL, 0)], [W["norm_pre"]], [(D_MODEL, BF16)], BIG_TR)[0]
    cols = _matmul("in_proj_fwd", h, W["w_pad"], "nt", BF16, 2048, 1024, 1024)
    if late_weights is not None:
        W = dict(W, **late_weights(cols))
    mem_kv = _rows_fwd("memkv_fwd", _memkv_fn, [Row(mem, D_MODEL, 0)], [W["norm_mem"], W["w_mem_kv"]],
                       [(D_MODEL, F32)], MEM_LEN)[0]
    ra, rb, rc, rm = _branch_rows(cols, True)
    y_a, csave = _rows_fwd("dn_fwd", _dn_fn, ra, [W["conv"], W["a_vec"], W["dt_vec"], W["dn_norm"]],
                           [(BRANCH_W, BF16)], DN_TR, CARRY)
    y_b = _rows_fwd("gm_fwd", _gm_fn, rb, [W["gm_norm"], W["spatial_w"], W["spatial_b"]], [(BRANCH_W, BF16)], BIG_TR)[0]
    y_c = _rows_fwd("swa_fwd", _swa_fn, rc, [W["sink_vec"]], [(BRANCH_W, BF16)], SWA_TR)[0]
    y_m = _rows_fwd("mem_fwd", _mem_fn, rm, [mem_kv], [(BRANCH_W, BF16)], BIG_TR)[0]
    ys = [y_a, y_b, y_c, y_m]
    merged = _rows_fwd("up_fwd", _up_fn, [Row(y, BRANCH_W, 0) for y in ys] + [Row(cols, 4 * D_MODEL, 0)],
                       [W["w_up"]], [(D_MODEL, BF16)], UP_TR)[0]
    x_new = _rows_fwd("out_fwd", _out_fn, [Row(x, D_MODEL, 0), Row(merged, D_MODEL, 0)],
                      [W["w_out"], W["norm_post"]], [(D_MODEL, F32)], TR)[0]
    return x_new, dict(x=x, h=h, cols=cols, mem_kv=mem_kv, csave=csave, ys=ys, merged=merged), W


def _layer_bwd(dxn, mem, W, sv, on_weight_grads=None):
    x, cols = sv["x"], sv["cols"]
    (dx_res, dm), (dw_out, dnorm_post), _ = _rows_bwd(
        "out_bwd", _out_fn, [Row(x, D_MODEL, 0), Row(sv["merged"], D_MODEL, 0)], [W["w_out"], W["norm_post"]],
        [dxn], TR)
    dys, dcols, dw_up = _up_bwd(jnp.concatenate(sv["ys"], axis=1), cols, dm, W["w_up"])
    dys = [Row(dys, BRANCH_W, n) for n in range(4)]
    ra, rb, rc, rm = _branch_rows(cols, "cols")
    (dba,), (dconv, da_vec, ddt_vec, ddn_norm), dcols = _rows_bwd(
        "dn_bwd", _dn_fn, ra, [W["conv"], W["a_vec"], W["dt_vec"], W["dn_norm"]], [dys[0]], DN_TR, CARRY,
        sv["csave"], dcols=dcols)
    _, (dgm_norm, dws, dbs), dcols = _rows_bwd(
        "gm_bwd", _gm_fn, rb, [W["gm_norm"], W["spatial_w"], W["spatial_b"]], [dys[1]], BIG_TR, dcols=dcols)
    (dkv_c,), (dsink,), dcols = _rows_bwd("swa_bwd", _swa_fn, rc, [W["sink_vec"]], [dys[2]], SWA_TR, dcols=dcols)
    _, (dmem_kv,), dcols = _rows_bwd("mem_bwd", _mem_fn, rm, [sv["mem_kv"]], [dys[3]], BIG_TR, dcols=dcols)
    dcols = _fill_misc(dcols, dkv_c, dba, BIG_TR)
    _, (dnorm_mem, dw_mem_kv), _ = _rows_bwd("memkv_bwd", _memkv_fn, [Row(mem, D_MODEL, 0, 0, False)],
                                             [W["norm_mem"], W["w_mem_kv"]], [dmem_kv], MEM_LEN)
    dw_pad = _matmul("in_proj_dw", dcols, sv["h"], "tn", BF16, 1024, 1024, 2048)
    grads = dict(norm_post=dnorm_post, norm_mem=dnorm_mem, w_pad=dw_pad, conv=dconv,
                 a_vec=da_vec, dt_vec=ddt_vec, dn_norm=ddn_norm, gm_norm=dgm_norm, spatial_w=dws, spatial_b=dbs,
                 sink_vec=dsink, w_mem_kv=dw_mem_kv, w_up=dw_up, w_out=dw_out)
    started = None if on_weight_grads is None else on_weight_grads(grads)
    dh = _matmul("in_proj_dx", dcols, W["w_pad"], "nn", F32, 1024, 1024, 2048, after=started)
    (dx,), (grads["norm_pre"],), _ = _rows_bwd("prenorm_bwd", _pre_fn_res, [Row(x, D_MODEL, 0)], [W["norm_pre"]],
                                               [dh, dx_res], BIG_TR)
    return dx, grads


def _lane_vec(v, off):
    return jnp.zeros((1, LANES), F32).at[0, off:off + v.shape[0]].set(v)


def _layer_weights(l, w_pad, conv_w, small, **late):
    return dict(
        late, w_pad=w_pad, conv=conv_w.reshape(4, 3, BRANCH_W).reshape(12, 1, BRANCH_W),
        norm_pre=small["norm_pre"][l][None], norm_post=small["norm_post"][l][None],
        norm_mem=small["norm_mem"][l][None],
        a_vec=_lane_vec(small["a_log"][l], 4), dt_vec=_lane_vec(small["dt_bias"][l], 4),
        dn_norm=small["dn_norm"][l][None], gm_norm=small["gm_norm"][l][None],
        spatial_w=small["spatial_w"][l], spatial_b=small["spatial_b"][l][:, None, :],
        sink_vec=_lane_vec(small["sinks"][l], 0))


_MESH = pl.DeviceIdType.MESH
_ANY = pl.BlockSpec(memory_space=pl.ANY)


def _position():
    return lax.axis_index("x"), lax.axis_index("y"), lax.axis_index("c")


def _remote(src, dst, send_sem, recv_sem, dev):
    return pltpu.make_async_remote_copy(src_ref=src, dst_ref=dst, send_sem=send_sem, recv_sem=recv_sem,
                                        device_id=dev, device_id_type=_MESH)


def _hbm_call(name, body, arrs, out_shapes, sems, aliases=None):
    return pl.pallas_call(
        body, name=name, in_specs=[_ANY] * len(arrs), out_specs=[_ANY] * len(out_shapes), out_shape=out_shapes,
        scratch_shapes=[pltpu.SemaphoreType.DMA((k,)) for k in sems], input_output_aliases=aliases or {},
        compiler_params=pltpu.CompilerParams(has_side_effects=True),
    )(*arrs)


def _other_chips(x, y):
    return [(1 - x, y), (x, 1 - y), (1 - x, 1 - y)]


def _gather_weights(arrs, relayed):
    n = len(arrs)

    def body(*refs):
        ins, outs = refs[:n], refs[n:2 * n]
        ici_send, ici_recv, d2d_send, d2d_recv = refs[2 * n:]
        x, y, c = _position()
        me = 2 * x + y
        xn, yn, dg = _other_chips(x, y)
        chip = lambda p: 2 * p[0] + p[1]
        sends = []

        def go(cp):
            cp.start()
            sends.append(cp)

        def ici(a, j, src, dst, to):
            return _remote(src, dst, ici_send.at[4 * a + j], ici_recv.at[4 * a + j], (*to, c))

        for a in range(n):
            go(ici(a, 0, ins[a].at[c], outs[a].at[c, me], xn))
            go(ici(a, 1, ins[a].at[c], outs[a].at[c, me], yn))
            if not relayed[a]:
                go(ici(a, 2, ins[a].at[c], outs[a].at[c, me], dg))
        for a in range(n):
            h = arrs[a].shape[1] // 2
            from_x, from_y = outs[a].at[c, chip(xn)], outs[a].at[c, chip(yn)]
            ici(a, 0, ins[a].at[c], from_x, xn).wait_recv()
            if relayed[a]:
                go(ici(a, 2, from_x.at[pl.ds(0, h)], from_x.at[pl.ds(0, h)], yn))
            ici(a, 1, ins[a].at[c], from_y, yn).wait_recv()
            if relayed[a]:
                go(ici(a, 3, from_y.at[pl.ds(h, h)], from_y.at[pl.ds(h, h)], xn))
            for j, slab in enumerate((from_x, from_y)):
                go(_remote(slab, slab, d2d_send.at[3 * a + j], d2d_recv.at[3 * a + j], (x, y, 1 - c)))
        for a in range(n):
            h = arrs[a].shape[1] // 2
            from_d = outs[a].at[c, chip(dg)]
            if relayed[a]:
                ici(a, 2, from_d.at[pl.ds(0, h)], from_d.at[pl.ds(0, h)], yn).wait_recv()
                ici(a, 3, from_d.at[pl.ds(h, h)], from_d.at[pl.ds(h, h)], xn).wait_recv()
            else:
                ici(a, 2, ins[a].at[c], from_d, dg).wait_recv()
            go(_remote(from_d, from_d, d2d_send.at[3 * a + 2], d2d_recv.at[3 * a + 2], (x, y, 1 - c)))
        for a in range(n):
            for j, p in enumerate((xn, yn, dg)):
                slab = outs[a].at[1 - c, chip(p)]
                _remote(slab, slab, d2d_send.at[3 * a + j], d2d_recv.at[3 * a + j], (x, y, 1 - c)).wait_recv()
        for cp in sends:
            cp.wait_send()

    return _hbm_call("gather_weights", body, arrs,
                     [jax.ShapeDtypeStruct((N_LAYER, N_CHIP) + a.shape[1:], a.dtype) for a in arrs],
                     [4 * n, 4 * n, 3 * n, 3 * n])


def _pair_exchange(arrs):
    n = len(arrs)

    def body(*refs):
        ins, outs = refs[:n], refs[n:2 * n]
        send_sems, recv_sems = refs[2 * n:]
        x, y, c = _position()
        cps = [_remote(ins[a].at[1 - c], outs[a], send_sems.at[a], recv_sems.at[a], (x, y, 1 - c)) for a in range(n)]
        for cp in cps:
            cp.start()
        for cp in cps:
            cp.wait_recv()
        for cp in cps:
            cp.wait_send()

    return _hbm_call("pair_exchange", body, arrs, [jax.ShapeDtypeStruct(a.shape[1:], a.dtype) for a in arrs], [n, n])


def _chip_scatter(arrs):
    n = len(arrs)

    def body(*refs):
        ins, outs = refs[:n], refs[n:2 * n]
        send_sems, recv_sems = refs[2 * n:]
        x, y, c = _position()
        me = 2 * x + y
        sends = []
        for a in range(n):
            for j, (px, py) in enumerate(_other_chips(x, y)):
                sends.append(_remote(ins[a].at[2 * px + py], outs[a].at[me], send_sems.at[3 * a + j],
                                     recv_sems.at[3 * a + j], (px, py, c)))
                sends[-1].start()
        for a in range(n):
            for j, (px, py) in enumerate(_other_chips(x, y)):
                _remote(ins[a].at[me], outs[a].at[2 * px + py], send_sems.at[3 * a + j], recv_sems.at[3 * a + j],
                        (px, py, c)).wait_recv()
        for cp in sends:
            cp.wait_send()

    return _hbm_call("chip_scatter", body, arrs, [jax.ShapeDtypeStruct(a.shape, a.dtype) for a in arrs],
                     [3 * n, 3 * n])


def _pair_share(arrs):
    n = len(arrs)

    def body(*refs):
        ins, outs = refs[:n], refs[n:2 * n]
        send_sems, recv_sems = refs[2 * n:]
        x, y, c = _position()
        cps = [_remote(ins[a].at[c], outs[a].at[c], send_sems.at[a], recv_sems.at[a], (x, y, 1 - c)) for a in range(n)]
        for cp in cps:
            cp.start()
        for a in range(n):
            _remote(ins[a].at[c], outs[a].at[1 - c], send_sems.at[a], recv_sems.at[a], (x, y, 1 - c)).wait_recv()
        for cp in cps:
            cp.wait_send()

    return _hbm_call("pair_share", body, arrs, [jax.ShapeDtypeStruct(a.shape, a.dtype) for a in arrs], [n, n],
                     {a: a for a in range(n)})


def _pair_forward(arrs):
    n = len(arrs)

    def body(*refs):
        ins, outs = refs[:n], refs[n:2 * n]
        send_sems, recv_sems = refs[2 * n:]
        x, y, c = _position()
        sends = []
        for a in range(n):
            for j, (px, py) in enumerate(_other_chips(x, y)):
                sends.append(_remote(ins[a].at[c, 2 * px + py], outs[a].at[c, 2 * px + py], send_sems.at[3 * a + j],
                                     recv_sems.at[3 * a + j], (x, y, 1 - c)))
                sends[-1].start()
        for a in range(n):
            for j, (px, py) in enumerate(_other_chips(x, y)):
                slab = outs[a].at[1 - c, 2 * px + py]
                _remote(slab, slab, send_sems.at[3 * a + j], recv_sems.at[3 * a + j], (x, y, 1 - c)).wait_recv()
        for cp in sends:
            cp.wait_send()

    return _hbm_call("pair_forward", body, arrs, [jax.ShapeDtypeStruct(a.shape, a.dtype) for a in arrs],
                     [3 * n, 3 * n], {a: a for a in range(n)})


_HBM = pl.BlockSpec(memory_space=pltpu.HBM)
_SEM = pl.BlockSpec(memory_space=pltpu.SEMAPHORE)
_EFFECT = pltpu.SideEffectType.DATAFLOW_SIDE_EFFECTING


def _chip_copies(kind, srcs, lands, send_sems, recv_sems):
    x, y, c = _position()
    me = 2 * x + y
    sends, recvs = [], []
    for a in range(len(srcs)):
        for j, (px, py) in enumerate(_other_chips(x, y)):
            s, sems, dev = 2 * px + py, (send_sems.at[3 * a + j], recv_sems.at[3 * a + j]), (px, py, c)
            if kind == "gather":
                sends.append(_remote(srcs[a].at[c], lands[a].at[c, me], *sems, dev))
                recvs.append(_remote(srcs[a].at[c], lands[a].at[c, s], *sems, dev))
            else:
                sends.append(_remote(srcs[a].at[s], lands[a].at[me], *sems, dev))
                recvs.append(_remote(srcs[a].at[me], lands[a].at[s], *sems, dev))
    return sends, recvs


def _split_start(name, kind, srcs, land_shapes, after):
    n = len(srcs)

    def body(*refs):
        sends, _ = _chip_copies(kind, refs[:n], refs[n:2 * n], refs[2 * n + 1], refs[2 * n + 2])
        for cp in sends:
            cp.start()
        refs[-1][...] = jnp.zeros_like(refs[-1])

    hbm = lambda a: pltpu.with_memory_space_constraint(a, pltpu.HBM)
    lands = [lax.empty(s.shape, s.dtype) for s in land_shapes]
    outs = pl.pallas_call(
        body, name=name, in_specs=[_HBM] * (2 * n) + [_ANY],
        out_specs=[_SEM, _SEM] + [_HBM] * (2 * n) + [pl.BlockSpec(memory_space=pltpu.VMEM)],
        out_shape=[pltpu.SemaphoreType.DMA((3 * n,)), pltpu.SemaphoreType.DMA((3 * n,))]
        + [pltpu.HBM(a.shape, a.dtype) for a in list(srcs) + lands] + [jax.ShapeDtypeStruct((8, LANES), F32)],
        input_output_aliases={i: 2 + i for i in range(2 * n)},
        compiler_params=pltpu.CompilerParams(has_side_effects=_EFFECT),
    )(*[hbm(a) for a in srcs], *[hbm(a) for a in lands], after)
    return outs[0], outs[1], list(outs[2:2 + 2 * n]), outs[-1]


def _split_wait(name, kind, started, after):
    send_sems, recv_sems, thru, _ = started
    n = len(thru) // 2

    def body(*refs):
        sends, recvs = _chip_copies(kind, refs[:n], refs[n:2 * n], refs[2 * n], refs[2 * n + 1])
        for cp in sends:
            cp.wait_send()
        for cp in recvs:
            cp.wait_recv()

    outs = pl.pallas_call(
        body, name=name, in_specs=[_HBM] * (2 * n) + [_SEM, _SEM, _ANY], out_specs=[_HBM] * (2 * n),
        out_shape=[pltpu.HBM(a.shape, a.dtype) for a in thru], input_output_aliases={i: i for i in range(2 * n)},
        compiler_params=pltpu.CompilerParams(has_side_effects=_EFFECT),
    )(*thru, send_sems, recv_sems, after)
    return list(outs[:n]), list(outs[n:])


def _allreduce_small(g):
    def body(g_ref, o_ref, pair_buf, chip_buf, send_sems, recv_sems):
        x, y, c = _position()
        me = 2 * x + y
        sib = (x, y, 1 - c)
        to_sib = _remote(g_ref.at[1 - c], pair_buf, send_sems.at[0], recv_sems.at[0], sib)
        to_sib.start()
        to_sib.wait_recv()
        chip_buf[me] = g_ref[c] + pair_buf[...]
        sends = [to_sib]
        chips = _other_chips(x, y)
        for j, (px, py) in enumerate(chips):
            sends.append(_remote(chip_buf.at[me], chip_buf.at[me], send_sems.at[1 + j], recv_sems.at[1 + j], (px, py, c)))
            sends[-1].start()
        for j, (px, py) in enumerate(chips):
            _remote(chip_buf.at[me], chip_buf.at[2 * px + py], send_sems.at[1 + j], recv_sems.at[1 + j],
                    (px, py, c)).wait_recv()
        o_ref[c] = ((chip_buf[0] + chip_buf[1]) + chip_buf[2]) + chip_buf[3]
        sends.append(_remote(o_ref.at[c], o_ref.at[c], send_sems.at[4], recv_sems.at[4], sib))
        sends[-1].start()
        _remote(o_ref.at[c], o_ref.at[1 - c], send_sems.at[4], recv_sems.at[4], sib).wait_recv()
        for cp in sends:
            cp.wait_send()

    vmem = pl.BlockSpec(memory_space=pltpu.VMEM)
    return pl.pallas_call(
        body, name="allreduce_small", in_specs=[vmem], out_specs=vmem, out_shape=jax.ShapeDtypeStruct(g.shape, F32),
        scratch_shapes=[pltpu.VMEM(g.shape[1:], F32), pltpu.VMEM((N_CHIP,) + g.shape[1:], F32),
                        pltpu.SemaphoreType.DMA((5,)), pltpu.SemaphoreType.DMA((5,))],
        compiler_params=_params(),
    )(g)


EW_ROWS = 512


def _ew(name, fn, ins, n_out, out_dtype=F32, out_slot=None, into=None):
    def dims(a):
        return a[0].shape[1:] if isinstance(a, tuple) else a.shape

    R, w = dims(ins[0])
    tr = EW_ROWS if R % EW_ROWS == 0 else R
    n_into = len(into) if into else 0

    def body(c_ref, *refs):
        outs = fn(*[r[...] for r in refs[:len(ins)]])
        for r, v in zip(refs[len(ins) + n_into:], outs):
            r[...] = v.astype(r.dtype)

    def lead_spec(l):
        if l == "c":
            return pl.BlockSpec((None, tr, w), lambda i, c_ref: (c_ref[0], i, 0))
        return pl.BlockSpec((None, tr, w), lambda i, c_ref, s=l: (s, i, 0))

    plain = pl.BlockSpec((tr, w), lambda i, c_ref: (i, 0))
    in_specs = [lead_spec(a[1]) if isinstance(a, tuple) else plain for a in ins] + [_ANY] * n_into
    out_spec = plain if out_slot is None else lead_spec(out_slot)
    out_shape = jax.ShapeDtypeStruct((R, w) if out_slot is None else (2, R, w), out_dtype)
    return pl.pallas_call(
        body, name=name,
        grid_spec=pltpu.PrefetchScalarGridSpec(num_scalar_prefetch=1, grid=(R // tr,), in_specs=in_specs,
                                               out_specs=[out_spec] * n_out),
        out_shape=[out_shape] * n_out, input_output_aliases={1 + len(ins) + j: j for j in range(n_into)},
        compiler_params=_params(dimension_semantics=("arbitrary",)),
    )(lax.axis_index("c").astype(jnp.int32).reshape(1), *[a[0] if isinstance(a, tuple) else a for a in ins],
      *(into or []))


def _adamw_fn(w, g, m, v):
    m = ADAM_B1 * m + (1.0 - ADAM_B1) * g
    v = ADAM_B2 * v + (1.0 - ADAM_B2) * (g * g)
    m_hat = m / (1.0 - ADAM_B1 ** ADAM_STEP)
    v_hat = v / (1.0 - ADAM_B2 ** ADAM_STEP)
    delta = -ADAM_LR * (m_hat / (jnp.sqrt(v_hat) + ADAM_EPS) + ADAM_WD * w)
    return delta, m, v


def _adamw(name, w, g, m, v):
    shape = w.shape
    two = lambda a: a.reshape(-1, shape[-1])
    return [o.reshape(shape) for o in _ew(name, _adamw_fn, [two(w), two(g), two(m), two(v)], 3)]


def _adamw_layer(name, l, w, g, m, v, into):
    k = w.shape[-1]
    three = lambda a: (a.reshape(N_LAYER, -1, k), l)
    fn = lambda w_, g_, m_, v_: _adamw_fn(w_, g_, m_, v_) + (g_,)
    outs = _ew(name, fn, [three(w), g.reshape(-1, k), three(m), three(v)], 4, out_slot=l,
               into=None if into is None else [a.reshape(N_LAYER, -1, k) for a in into])
    return [o.reshape(w.shape) for o in outs]


def _adamw_rows(name, l, w, g, m, v, into):
    _, R, k = w.shape
    n_into = len(into) if into else 0

    def body(*refs):
        w_ref, g_ref, m_ref, v_ref = refs[:4]
        d_out, m_out, v_out, g_out = refs[4 + n_into:]
        g_blk = g_ref[...]
        d_out[...], m_out[...], v_out[...] = _adamw_fn(w_ref[...], g_blk, m_ref[...], v_ref[...])
        g_out[...] = g_blk

    spec = pl.BlockSpec((None, EW_ROWS, k), lambda i: (l, i, 0))
    return pl.pallas_call(
        body, name=name, grid=(-(-R // EW_ROWS),),
        in_specs=[spec, pl.BlockSpec((EW_ROWS, k), lambda i: (i, 0)), spec, spec] + [_ANY] * n_into,
        out_specs=[spec] * 4, out_shape=[jax.ShapeDtypeStruct((N_LAYER, R, k), F32)] * 4,
        input_output_aliases={4 + j: j for j in range(n_into)},
        compiler_params=_params(dimension_semantics=("arbitrary",)),
    )(w, g, m, v, *(into or []))


_SMALL = [("norm_pre", (2, 1024)), ("norm_post", (2, 1024)), ("norm_mem", (2, 1024)), ("a_log", (2, 4)),
          ("dt_bias", (2, 4)), ("dn_norm", (2, 128)), ("gm_norm", (2, 512)), ("spatial_w", (2, 4, 128, 128)),
          ("spatial_b", (2, 4, 128)), ("sinks", (2, 8)), ("loss", (2, 1))]
_SMALL_ROWS = 208
_BIG = ["w_in", "conv_w", "w_mem_kv", "w_up", "w_out"]
_NAMES = ["norm_pre", "norm_post", "norm_mem", "w_in", "conv_w", "a_log", "dt_bias", "dn_norm", "gm_norm",
          "spatial_w", "spatial_b", "sinks", "w_mem_kv", "w_up", "w_out"]


def _size(shape):
    n = 1
    for s in shape:
        n *= s
    return n


def _pack_small(d):
    rows = []
    for n, shp in _SMALL:
        a = d[n].reshape(N_LAYER, -1)
        rows.append(a.reshape(-1, 1024) if a.shape[1] > 1024 else jnp.pad(a, ((0, 6), (0, 1024 - a.shape[1]))))
    assert sum(r.shape[0] for r in rows) == _SMALL_ROWS
    return jnp.concatenate(rows, axis=0)


def _unpack_small(p):
    out, off = {}, 0
    for n, shp in _SMALL:
        c = _size(shp) // N_LAYER
        k = 8 if c <= 1024 else _size(shp) // 1024
        out[n] = (p[off:off + N_LAYER, :c] if c <= 1024 else p[off:off + k]).reshape(shp)
        off += k
    return out


_HALF_SHAPE = {"w_in": (SHARD_PAD // 2, D_MODEL), "conv_w": (2, 3 * BRANCH_W // N_CHIP), "w_mem_kv": (128, D_MODEL),
               "w_up": (2, BRANCH_W, D_MODEL // N_CHIP), "w_out": (128, D_MODEL)}


def _chip_major(g):
    g = jnp.swapaxes(g, 0, 1)
    return g.reshape((N_CHIP, 2 * g.shape[2]) + g.shape[3:])


def _half_major(g):
    g = g.reshape((N_CHIP, 2, g.shape[1] // 2) + g.shape[2:])
    return jnp.swapaxes(g, 0, 1).astype(BF16)


N_EARLY = 2


def _early_views(l, g_in, g_conv, small):
    return _layer_weights(l, _w_pad_from_slabs(g_in),
                          _chip_major(g_conv).transpose(1, 0, 2).reshape(4, 3 * BRANCH_W), small)


def _late_views(g_kv, g_up, g_out):
    return dict(w_mem_kv=_chip_major(g_kv).reshape(D_MODEL, D_MODEL),
                w_up=_chip_major(g_up).transpose(1, 2, 0, 3).reshape(4, BRANCH_W, D_MODEL),
                w_out=_chip_major(g_out).reshape(D_MODEL, D_MODEL))


def _pair_sums(g):
    add2 = lambda a, b: [a.astype(F32) + b.astype(F32)]
    mine, theirs = _slabs_from_pad(g["w_pad"])
    pair = [_ew("pair_sum_w_in", add2, [mine.reshape(-1, D_MODEL), theirs.reshape(-1, D_MODEL)], 1, BF16)[0]
            .reshape(mine.shape)]
    rest = [_half_major(g["conv"].reshape(4, N_CHIP, 3 * BRANCH_W // N_CHIP).transpose(1, 0, 2)),
            _half_major(g["w_mem_kv"].reshape(N_CHIP, D_MODEL // N_CHIP, D_MODEL)),
            _half_major(g["w_up"].reshape(4, BRANCH_W, N_CHIP, D_MODEL // N_CHIP).transpose(2, 0, 1, 3)),
            _half_major(g["w_out"].reshape(N_CHIP, D_MODEL // N_CHIP, D_MODEL))]
    for n, b, p in zip(_BIG[1:], rest, _pair_exchange(rest)):
        k = b.shape[-1]
        pair.append(_ew("pair_sum_" + n, add2, [(b.reshape(2, -1, k), "c"), p.reshape(-1, k)], 1, BF16)[0]
                    .reshape(p.shape))
    return pair


def _chip_sums(landed, pair, me):
    add4 = lambda a, b, c_, d: [((a.astype(F32) + b.astype(F32)) + c_.astype(F32)) + d.astype(F32)]
    totals = []
    for n, r, q in zip(_BIG, landed, pair):
        r = _own_slot(r, lax.dynamic_index_in_dim(q, me, 0), me, 0)
        k = r.shape[-1]
        totals.append(_ew("chip_sum_" + n, add4, [(r.reshape(N_CHIP, -1, k), s) for s in range(N_CHIP)], 1,
                          out_slot="c")[0].reshape((2,) + r.shape[1:]))
    return totals


def _own_slot(buf, mine, me, axis):
    return lax.dynamic_update_index_in_dim(buf, mine.astype(buf.dtype), me, axis)


def kernel(x, mem, norm_pre, norm_post, norm_mem, w_in, conv_w, a_log, dt_bias, dn_norm, gm_norm, spatial_w, spatial_b, sinks, w_mem_kv, w_up, w_out, loss_target, m_norm_pre, m_norm_post, m_norm_mem, m_w_in, m_conv_w, m_a_log, m_dt_bias, m_dn_norm, m_gm_norm, m_spatial_w, m_spatial_b, m_sinks, m_w_mem_kv, m_w_up, m_w_out, v_norm_pre, v_norm_post, v_norm_mem, v_w_in, v_conv_w, v_a_log, v_dt_bias, v_dn_norm, v_gm_norm, v_spatial_w, v_spatial_b, v_sinks, v_w_mem_kv, v_w_up, v_w_out):
    w = dict(norm_pre=norm_pre, norm_post=norm_post, norm_mem=norm_mem, w_in=w_in, conv_w=conv_w, a_log=a_log,
             dt_bias=dt_bias, dn_norm=dn_norm, gm_norm=gm_norm, spatial_w=spatial_w, spatial_b=spatial_b, sinks=sinks,
             w_mem_kv=w_mem_kv, w_up=w_up, w_out=w_out)
    m = dict(norm_pre=m_norm_pre, norm_post=m_norm_post, norm_mem=m_norm_mem, w_in=m_w_in, conv_w=m_conv_w,
             a_log=m_a_log, dt_bias=m_dt_bias, dn_norm=m_dn_norm, gm_norm=m_gm_norm, spatial_w=m_spatial_w,
             spatial_b=m_spatial_b, sinks=m_sinks, w_mem_kv=m_w_mem_kv, w_up=m_w_up, w_out=m_w_out)
    v = dict(norm_pre=v_norm_pre, norm_post=v_norm_post, norm_mem=v_norm_mem, w_in=v_w_in, conv_w=v_conv_w,
             a_log=v_a_log, dt_bias=v_dt_bias, dn_norm=v_dn_norm, gm_norm=v_gm_norm, spatial_w=v_spatial_w,
             spatial_b=v_spatial_b, sinks=v_sinks, w_mem_kv=v_w_mem_kv, w_up=v_w_up, w_out=v_w_out)
    me = 2 * lax.axis_index("x") + lax.axis_index("y")

    tr = lambda a: a.transpose(0, 2, 1)
    w_t = tr(w_in)
    w_in_t = jnp.pad(w_t.astype(BF16), ((0, 0), (0, SHARD_PAD - SHARD_IN), (0, 0)))
    for d in (w, m, v):
        d["loss"] = jnp.zeros((N_LAYER, 1), F32)
    local = dict(w_in=w_in_t, conv_w=conv_w, w_mem_kv=w_mem_kv.astype(BF16), w_up=w_up.astype(BF16),
                 w_out=w_out.astype(BF16))
    halves = lambda l: [local[n][l].reshape((2,) + _HALF_SHAPE[n]) for n in _BIG]
    own = lambda gathered, mine: [_own_slot(g, h[:, None], me, 1) for g, h in zip(gathered, mine)]
    lands = [jax.ShapeDtypeStruct((2, N_CHIP) + _HALF_SHAPE[n], local[n].dtype) for n in _BIG]
    h0 = halves(0)
    g0 = own(_gather_weights(h0[:N_EARLY], [True, False]), h0[:N_EARLY])
    rest0 = _split_start("gather_l0_rest_start", "gather", h0[N_EARLY:], lands[N_EARLY:], g0[1])
    started = _split_start("gather_l1_start", "gather", halves(1), lands, rest0[3])

    xl, meml = x[0], mem[0]
    W0 = _early_views(0, g0[0], g0[1], w)
    W0["norm_pre"] = W0["norm_pre"] + started[3][0, 0]

    def late0(cols):
        mine, landed = _split_wait("gather_l0_rest_wait", "gather", rest0, cols)
        return _late_views(*own(_pair_forward(landed), mine))

    x1, sv0, W0 = _layer_fwd(xl, meml, W0, late0)
    mine1, landed1 = _split_wait("gather_l1_wait", "gather", started, x1)
    g1 = own(_pair_forward(landed1), mine1)
    W1 = dict(_early_views(1, g1[0], g1[1], w), **_late_views(*g1[N_EARLY:]))
    x2, sv1, _ = _layer_fwd(x1, meml, W1)
    dy, lrows = _rows_fwd("loss", _loss_fn, [Row(x2, D_MODEL, 0), Row(loss_target[0], D_MODEL, 0)], [],
                          [(D_MODEL, F32), (LANES, F32)], BIG_TR)
    loss_local = jnp.sum(lrows[:, 0])

    scattering = {}

    def start_scatter(l):
        def on_weight_grads(g):
            pair = _pair_sums(g)
            scattering[l] = _split_start("scatter_l%d_start" % l, "scatter", pair,
                                         [jax.ShapeDtypeStruct(p.shape, p.dtype) for p in pair], pair[1])
            return scattering[l][3]
        return on_weight_grads

    dx1, grads1 = _layer_bwd(dy, meml, W1, sv1, start_scatter(1))
    dx, grads0 = _layer_bwd(dx1, meml, W0, sv0, start_scatter(0))
    pair1, landed1 = _split_wait("scatter_l1_wait", "scatter", scattering[1], dx)
    after_start = scattering[0][3][0, 0]
    grads = [grads0, grads1]

    small_local = dict(
        norm_pre=jnp.stack([g["norm_pre"][0] for g in grads]), norm_post=jnp.stack([g["norm_post"][0] for g in grads]),
        norm_mem=jnp.stack([g["norm_mem"][0] for g in grads]), a_log=jnp.stack([g["a_vec"][0, 4:8] for g in grads]),
        dt_bias=jnp.stack([g["dt_vec"][0, 4:8] for g in grads]), dn_norm=jnp.stack([g["dn_norm"][0] for g in grads]),
        gm_norm=jnp.stack([g["gm_norm"][0] for g in grads]), spatial_w=jnp.stack([g["spatial_w"] for g in grads]),
        spatial_b=jnp.stack([g["spatial_b"][:, 0, :] for g in grads]),
        sinks=jnp.stack([g["sink_vec"][0, :8] for g in grads]),
        loss=jnp.stack([loss_local, jnp.zeros((), F32)]).reshape(N_LAYER, 1))
    packed = _pack_small(small_local) + after_start
    gsmall_packed = _allreduce_small(packed.reshape(2, -1, 1024)).reshape(-1, 1024)

    d_s, m_s, v_s = _ew("adamw_small", _adamw_fn, [_pack_small(w), gsmall_packed, _pack_small(m), _pack_small(v)], 3)
    gsmall, dsmall, msmall, vsmall = (_unpack_small(p) for p in (gsmall_packed, d_s, m_s, v_s))
    g_o, d_o, m_o, v_o = dict(gsmall), dict(dsmall), dict(msmall), dict(vsmall)
    loss = gsmall["loss"][0, 0]
    m_t, v_t = tr(m["w_in"]), tr(v["w_in"])

    def update(l, totals, into):
        outs = {}
        for n, t in zip(_BIG, totals):
            g_l = t.reshape(local[n].shape[1:])
            if n == "w_in":
                outs[n] = _adamw_rows("adamw_" + n, l, w_t, g_l, m_t, v_t, into and into[n])
            else:
                outs[n] = _adamw_layer("adamw_" + n, l, w[n], g_l, m[n], v[n], into and into[n])
        return outs

    landed1[1] = landed1[1] + after_start.astype(landed1[1].dtype)
    outs1 = update(1, _pair_share(_chip_sums(landed1, pair1, me)), None)
    pair0, landed0 = _split_wait("scatter_l0_wait", "scatter", scattering[0], outs1["w_in"][0])
    outs = update(0, _pair_share(_chip_sums(landed0, pair0, me)), outs1)
    for n in _BIG:
        d_o[n], m_o[n], v_o[n], g_o[n] = [tr(o) for o in outs[n]] if n == "w_in" else outs[n]
    return (loss, dx[None], *[g_o[n] for n in _NAMES], *[d_o[n] for n in _NAMES], *[m_o[n] for n in _NAMES],
            *[v_o[n] for n in _NAMES])
```

```python
import collections
import functools

import jax
import jax.numpy as jnp
from jax import lax
from jax.experimental import pallas as pl
from jax.experimental.pallas import tpu as pltpu

F32 = jnp.float32
BF16 = jnp.bfloat16

D_MODEL = 1024
BRANCH_W = 512
MEM_LEN = 256
N_LAYER = 2
N_CHIP = 4
EPS = 1e-6
NEG_INF = -1e30
DN_CHUNK = 64
LANES = 128
VMEM_LIMIT = 48 * 1024 * 1024

ADAM_LR, ADAM_B1, ADAM_B2, ADAM_EPS, ADAM_WD, ADAM_STEP = 0.001, 0.9, 0.999, 1e-08, 0.01, 10

N_PAD = 10240
O_GATE = 0
O_AQ, O_AK, O_AV, O_AZ = 4096, 4608, 5120, 5632
O_BUV, O_BZ = 6144, 7168
O_CKV, O_BA = 7680, 7936
O_CQ, O_CZ = 8192, 8704
O_MQ, O_MZ = 9216, 9728
O_MISC, W_MISC = O_CKV, 512
_PAD_SEGS = [(5896, 4096), (0, 512), (512, 512), (1024, 512), (1536, 512), (2056, 1024), (3080, 512),
             (4104, 128), (4232, 128), (2048, 8), (None, 120), (None, 128),
             (3592, 512), (4360, 512), (4872, 512), (5384, 512)]
D_IN = 9992
SHARD_IN = D_IN // N_CHIP


SHARD_PAD = 2560


def _pad_parts():
    parts, off = [], 0
    for s, n in _PAD_SEGS:
        a = s
        while s is not None and a < s + n:
            chip = a // SHARD_IN
            b = min(s + n, (chip + 1) * SHARD_IN)
            parts.append((chip, a - chip * SHARD_IN, off + a - s, b - a))
            a = b
        off += n
    return parts


PERM_ROWS = 1024
PERM_SLACK = 32


def _permute_rows(name, src, parts, n_out, out_dtype, pair_split=False):
    B, Z = PERM_ROWS, PERM_ROWS + PERM_SLACK
    w = src.shape[1]
    plans = []
    for blk in range(n_out // B):
        o, runs = blk * B, []
        for s, d, n in parts:
            lo, hi = max(d, o), min(d + n, o + B)
            if lo < hi:
                s0 = s + lo - d
                wa = s0 // 16 * 16
                wb = min(-(-(s0 + hi - lo) // 16) * 16, src.shape[0])
                runs.append((wa, wb - wa, s0 - (lo - o) - wa, lo - o, hi - o))
        plans.append(runs)
    max_runs = max(len(r) for r in plans)
    nblk = len(plans)

    per_half = nblk // 2

    def body(*refs):
        src_ref, out_ref = refs[0], refs[1]
        if pair_split:
            theirs_ref, (inbuf, obuf, insem, outsem, to_sib_sem, from_sib_sem) = refs[2], refs[3:]
            x, y, c = _position()
            sibling = (x, y, 1 - c)
        else:
            inbuf, obuf, insem, outsem = refs[2:]

        def in_copies(blk):
            return [pltpu.make_async_copy(src_ref.at[pl.ds(wa, ws)], inbuf.at[blk % 2, r, pl.ds(0, ws)],
                                          insem.at[blk % 2, r]) for r, (wa, ws, _, _, _) in enumerate(plans[blk])]

        class out_copy:
            def __init__(self, blk):
                self.blk, self.rows = blk, pl.ds((blk % per_half if pair_split else blk) * B, B)
                self.local = pltpu.make_async_copy(obuf.at[blk % 2], out_ref.at[self.rows], outsem.at[blk % 2])

            def _both(self, local_op, remote_op):
                if not pair_split:
                    return local_op(self.local)
                mine = c == self.blk // per_half
                pl.when(mine)(lambda: local_op(self.local))
                pl.when(jnp.logical_not(mine))(lambda: remote_op(_remote(
                    obuf.at[self.blk % 2], theirs_ref.at[self.rows], to_sib_sem.at[self.blk % 2],
                    from_sib_sem.at[self.blk % per_half], sibling)))

            def start(self):
                self._both(lambda cp: cp.start(), lambda cp: cp.start())

            def wait(self):
                self._both(lambda cp: cp.wait(), lambda cp: cp.wait_send())

        for cp in in_copies(0):
            cp.start()
        rid = _iota((B, 1), 0)
        for blk in range(nblk):
            if blk + 1 < nblk:
                for cp in in_copies(blk + 1):
                    cp.start()
            for cp in in_copies(blk):
                cp.wait()
            val = jnp.zeros((B, w), F32)
            for r, (wa, ws, t, l0, l1) in enumerate(plans[blk]):
                win = jnp.concatenate([inbuf[blk % 2, r, pl.ds(0, ws)].astype(F32), jnp.zeros((Z - ws, w), F32)], axis=0)
                moved = pltpu.roll(win, (-t) % Z, 0)[:B]
                val = jnp.where((rid >= l0) & (rid < l1), moved, val)
            if blk >= 2:
                out_copy(blk - 2).wait()
            obuf[blk % 2] = val.astype(out_dtype)
            out_copy(blk).start()
        for blk in range(max(nblk - 2, 0), nblk):
            out_copy(blk).wait()
        if pair_split:
            for i in range(per_half):
                rows = theirs_ref.at[pl.ds(i * B, B)]
                _remote(rows, rows, to_sib_sem.at[0], from_sib_sem.at[i], sibling).wait_recv()

    scratch = [pltpu.VMEM((2, max_runs, Z, w), src.dtype), pltpu.VMEM((2, B, w), out_dtype),
               pltpu.SemaphoreType.DMA((2, max_runs)), pltpu.SemaphoreType.DMA((2,))]
    if pair_split:
        scratch += [pltpu.SemaphoreType.DMA((2,)), pltpu.SemaphoreType.DMA((per_half,))]
    out_shape = jax.ShapeDtypeStruct((n_out // 2 if pair_split else n_out, w), out_dtype)
    return pl.pallas_call(
        body, name=name, in_specs=[_ANY], out_specs=[_ANY] * 2 if pair_split else _ANY,
        out_shape=[out_shape] * 2 if pair_split else out_shape, scratch_shapes=scratch, compiler_params=_params(),
    )(src)


def _slab_parts():
    h, out = SHARD_PAD // 2, []
    for chip, s, d, n in _pad_parts():
        a = s
        while a < s + n:
            half = a // h
            b = min(s + n, (half + 1) * h)
            out.append(((half * N_CHIP + chip) * h + a - half * h, d + a - s, b - a))
            a = b
    return out


def _w_pad_from_slabs(slabs):
    return _permute_rows("w_pad_rows", slabs.reshape(-1, slabs.shape[-1]), _slab_parts(), N_PAD, BF16)


def _slabs_from_pad(dw):
    mine, theirs = _permute_rows("w_pad_grad_rows", dw, [(d, s, n) for s, d, n in _slab_parts()],
                                 N_CHIP * SHARD_PAD, BF16, pair_split=True)
    shape = (N_CHIP, SHARD_PAD // 2, dw.shape[1])
    return mine.reshape(shape), theirs.reshape(shape)


def _dot(a, b, dims, prec):
    if prec == "bf16":
        return lax.dot_general(a.astype(BF16), b.astype(BF16), (dims, ((), ())), preferred_element_type=F32)
    return lax.dot_general(a, b, (dims, ((), ())), precision=lax.Precision.HIGHEST, preferred_element_type=F32)


_NN, _NT, _TN = ((1,), (0,)), ((1,), (1,)), ((0,), (0,))


def _make_mm(prec):
    @jax.custom_vjp
    def nn(a, b):
        return _dot(a, b, _NN, prec)

    @jax.custom_vjp
    def nt(a, b):
        return _dot(a, b, _NT, prec)

    @jax.custom_vjp
    def tn(a, b):
        return _dot(a, b, _TN, prec)

    nn.defvjp(lambda a, b: (nn(a, b), (a, b)), lambda r, g: (nt(g, r[1]), tn(r[0], g)))
    nt.defvjp(lambda a, b: (nt(a, b), (a, b)), lambda r, g: (nn(g, r[1]), tn(g, r[0])))
    tn.defvjp(lambda a, b: (tn(a, b), (a, b)), lambda r, g: (nt(r[1], g), nn(r[0], g)))
    return nn, nt, tn


_nn16, _nt16, _tn16 = _make_mm("bf16")
_nn32 = _make_mm("f32")[0]


def _make_slice(axis):
    @functools.partial(jax.custom_vjp, nondiff_argnums=(1, 2, 3))
    def sl(x, a, b, n):
        return x[a:b] if axis == 0 else x[:, a:b]

    def fwd(x, a, b, n):
        return sl(x, a, b, n), None

    def bwd(a, b, n, _, g):
        parts = []
        if a > 0:
            parts.append(jnp.zeros((a, g.shape[1]) if axis == 0 else (g.shape[0], a), g.dtype))
        parts.append(g)
        if n - b > 0:
            parts.append(jnp.zeros((n - b, g.shape[1]) if axis == 0 else (g.shape[0], n - b), g.dtype))
        return (jnp.concatenate(parts, axis=axis),)

    sl.defvjp(fwd, bwd)
    return sl


_sl0, _sl1 = _make_slice(0), _make_slice(1)


def _rowsl(x, a, b):
    return _sl0(x, a, b, x.shape[0])


def _cols(x, a, b):
    return _sl1(x, a, b, x.shape[1])


@functools.partial(jax.custom_vjp, nondiff_argnums=(1,))
def _rollr(x, s):
    return pltpu.roll(x, s, 0)


_rollr.defvjp(lambda x, s: (_rollr(x, s), None),
              lambda s, _, g: (pltpu.roll(g, g.shape[0] - s, 0),))


def _iota(shape, axis):
    return lax.broadcasted_iota(jnp.int32, shape, axis)


def _sigmoid(x):
    return lax.logistic(x)


def _silu(x):
    return x * _sigmoid(x)


def _gelu(x):
    return 0.5 * x * (1.0 + jnp.tanh(0.7978845608028654 * (x + 0.044715 * (x * x * x))))


def _softplus(x):
    return jnp.maximum(x, 0.0) + jnp.log(1.0 + jnp.exp(-jnp.abs(x)))


def _rms(x, g):
    return x * lax.rsqrt(jnp.mean(x * x, axis=-1, keepdims=True) + EPS) * g


def _lane_pick(x, lane):
    return jnp.sum(x * (_iota((1, x.shape[1]), 1) == lane).astype(F32), axis=1, keepdims=True)


Row = collections.namedtuple("Row", "arr w cb hb grad", defaults=(0, True))


def _full_spec(shape):
    return pl.BlockSpec(shape, lambda i, _n=len(shape): (0,) * _n)


def _load_params(refs):
    return [[p[g].astype(F32) for g in range(p.shape[0])] if len(p.shape) == 3 else p[...].astype(F32)
            for p in refs]


def _params(**kw):
    return pltpu.CompilerParams(vmem_limit_bytes=VMEM_LIMIT, **kw)


def _rows_fwd(name, fn, rows, params, outs, tr, carry=None):
    T = rows[0].arr.shape[0]
    n = T // tr
    halos = [r for r in rows if r.hb]
    nr, nh, npar, no = len(rows), len(halos), len(params), len(outs)

    def body(*refs):
        row_refs, halo_refs = refs[:nr], refs[nr:nr + nh]
        par_refs = refs[nr + nh:nr + nh + npar]
        out_refs = refs[nr + nh + npar:nr + nh + npar + no]
        rest = refs[nr + nh + npar + no:]
        first = pl.program_id(0) == 0
        cvals = None
        if carry is not None:
            csave_ref, carry_ref = rest

            @pl.when(first)
            def _():
                carry_ref[...] = jnp.zeros_like(carry_ref)

            cvals = [carry_ref[g] for g in range(carry[0])]
            for g in range(carry[0]):
                csave_ref[0, g] = cvals[g]
        c_out, o = fn(first, cvals, [r[...].astype(F32) for r in row_refs],
                      [h[...].astype(F32) for h in halo_refs], _load_params(par_refs))
        for r, v in zip(out_refs, o):
            r[...] = v.astype(r.dtype)
        if carry is not None:
            for g in range(carry[0]):
                carry_ref[g] = c_out[g]

    in_specs = [pl.BlockSpec((tr, r.w), lambda i, c=r.cb: (i, c)) for r in rows]
    in_specs += [pl.BlockSpec((r.hb, r.w), lambda i, c=r.cb, q=tr // r.hb: (jnp.maximum(i * q - 1, 0), c))
                 for r in halos]
    in_specs += [_full_spec(p.shape) for p in params]
    out_shape = [jax.ShapeDtypeStruct((T, w), dt) for w, dt in outs]
    out_specs = [pl.BlockSpec((tr, w), lambda i: (i, 0)) for w, _ in outs]
    scratch = []
    if carry is not None:
        out_shape.append(jax.ShapeDtypeStruct((n,) + carry, F32))
        out_specs.append(pl.BlockSpec((1,) + carry, lambda i: (i, 0, 0, 0)))
        scratch.append(pltpu.VMEM(carry, F32))
    return pl.pallas_call(
        body, name=name, grid=(n,), in_specs=in_specs, out_specs=out_specs, out_shape=out_shape,
        scratch_shapes=scratch, compiler_params=_params(dimension_semantics=("arbitrary",)),
    )(*[r.arr for r in rows], *[r.arr for r in halos], *params)


def _rows_bwd(name, fn, rows, params, douts, tr, carry=None, csave=None, dcols=None):
    T = rows[0].arr.shape[0]
    n = T // tr
    halos = [r for r in rows if r.hb]
    grows = [r for r in rows if r.grad is True]
    crows = [r for r in rows if r.grad == "cols"]
    wcols = sum(r.w for r in crows)
    nr, nh, npar, nd, ng = len(rows), len(halos), len(params), len(douts), len(grows)
    nc = 0 if carry is None else 1
    ncol = 1 if crows else 0
    nalias = 1 if (crows and dcols is not None) else 0

    def body(*refs):
        row_refs, halo_refs = refs[:nr], refs[nr:nr + nh]
        par_refs = refs[nr + nh:nr + nh + npar]
        k = nr + nh + npar
        csave_ref = refs[k] if nc else None
        dout_refs = refs[k + nc:k + nc + nd]
        k = k + nc + nd + nalias
        drow_refs = refs[k:k + ng]
        dcols_ref = refs[k + ng] if ncol else None
        dpar_refs = refs[k + ng + ncol:k + ng + ncol + npar]
        k = k + ng + ncol + npar
        dcarry_ref = refs[k] if nc else None
        hgrad_refs = refs[k + nc:]
        i = pl.program_id(0)
        first_tile = i == n - 1

        @pl.when(i == 0)
        def _():
            for r in dpar_refs:
                r[...] = jnp.zeros_like(r)
            for r in hgrad_refs:
                r[...] = jnp.zeros_like(r)
            if nc:
                dcarry_ref[...] = jnp.zeros_like(dcarry_ref)

        rv = [r[...].astype(F32) for r in row_refs]
        hv = [h[...].astype(F32) for h in halo_refs]
        pv = _load_params(par_refs)
        dov = [d[...].astype(F32) for d in dout_refs]
        if nc:
            cv = [csave_ref[0, g] for g in range(carry[0])]
            _, vjp = jax.vjp(lambda c, r, h, p: fn(first_tile, c, r, h, p), cv, rv, hv, pv)
            dc, dr, dh, dp = vjp(([dcarry_ref[g] for g in range(carry[0])], dov))
            for g in range(carry[0]):
                dcarry_ref[g] = dc[g]
        else:
            _, vjp = jax.vjp(lambda r, h, p: fn(first_tile, None, r, h, p)[1], rv, hv, pv)
            dr, dh, dp = vjp(dov)
        gi = hi = 0
        pieces = []
        for kk, r in enumerate(rows):
            d = dr[kk]
            if r.hb:
                carried = hgrad_refs[hi][...]
                d = d + (carried if tr == r.hb else
                         jnp.concatenate([jnp.zeros((tr - r.hb, r.w), F32), carried], axis=0))
                hgrad_refs[hi][...] = dh[hi]
                hi += 1
            if r.grad is True:
                drow_refs[gi][...] = d.astype(drow_refs[gi].dtype)
                gi += 1
            elif r.grad == "cols":
                pieces.append(d.astype(BF16))
        if ncol:
            dcols_ref[...] = pieces[0] if len(pieces) == 1 else jnp.concatenate(pieces, axis=1)
        for r, d in zip(dpar_refs, dp):
            if len(r.shape) == 3:
                for g in range(r.shape[0]):
                    r[g] += d[g]
            else:
                r[...] += d

    rev = lambda i: n - 1 - i
    in_specs = [pl.BlockSpec((tr, r.w), lambda i, c=r.cb: (rev(i), c)) for r in rows]
    in_specs += [pl.BlockSpec((r.hb, r.w), lambda i, c=r.cb, q=tr // r.hb: (jnp.maximum(rev(i) * q - 1, 0), c))
                 for r in halos]
    in_specs += [_full_spec(p.shape) for p in params]
    args = [r.arr for r in rows] + [r.arr for r in halos] + list(params)
    scratch = []
    if nc:
        in_specs.append(pl.BlockSpec((1,) + carry, lambda i: (rev(i), 0, 0, 0)))
        args.append(csave)
        scratch.append(pltpu.VMEM(carry, F32))
    douts = [d if isinstance(d, Row) else Row(d, d.shape[1], 0) for d in douts]
    in_specs += [pl.BlockSpec((tr, d.w), lambda i, c=d.cb: (rev(i), c)) for d in douts]
    args += [d.arr for d in douts]
    aliases = {}
    if nalias:
        aliases = {len(args): ng}
        in_specs.append(pl.BlockSpec(memory_space=pl.ANY))
        args.append(dcols)
    scratch += [pltpu.VMEM((r.hb, r.w), F32) for r in halos]
    out_shape = [jax.ShapeDtypeStruct((T, r.w), F32) for r in grows]
    out_specs = [pl.BlockSpec((tr, r.w), lambda i: (rev(i), 0)) for r in grows]
    if ncol:
        off = crows[0].cb * crows[0].w
        assert off % wcols == 0 and all(a.cb * a.w + a.w == b.cb * b.w for a, b in zip(crows, crows[1:]))
        out_shape.append(jax.ShapeDtypeStruct((T, N_PAD), BF16))
        out_specs.append(pl.BlockSpec((tr, wcols), lambda i, c=off // wcols: (rev(i), c)))
    out_shape += [jax.ShapeDtypeStruct(p.shape, F32) for p in params]
    out_specs += [_full_spec(p.shape) for p in params]
    res = pl.pallas_call(
        body, name=name, grid=(n,), in_specs=in_specs, out_specs=out_specs, out_shape=out_shape,
        scratch_shapes=scratch, input_output_aliases=aliases,
        compiler_params=_params(dimension_semantics=("arbitrary",)),
    )(*args)
    return list(res[:ng]), list(res[ng + ncol:]), (res[ng] if ncol else dcols)


def _fill_misc(dcols, dkv, dba, tr):
    T = dkv.shape[0]

    def body(kv_ref, ba_ref, _, o_ref):
        o_ref[...] = jnp.concatenate([kv_ref[...], ba_ref[...]], axis=1).astype(BF16)

    return pl.pallas_call(
        body, name="misc_bwd", grid=(T // tr,),
        in_specs=[pl.BlockSpec((tr, 256), lambda i: (i, 0)), pl.BlockSpec((tr, 256), lambda i: (i, 0)),
                  pl.BlockSpec(memory_space=pl.ANY)],
        out_specs=pl.BlockSpec((tr, W_MISC), lambda i: (i, O_MISC // W_MISC)),
        out_shape=jax.ShapeDtypeStruct((T, N_PAD), BF16), input_output_aliases={2: 0},
        compiler_params=_params(dimension_semantics=("arbitrary",)),
    )(dkv, dba, dcols)


def _up_bwd(ys, cols, dm, w_up):
    T = dm.shape[0]
    tr = min(UPB_TR, T)

    def body(y_ref, gl_ref, dm_ref, w_ref, dy_ref, dgl_ref, dw_ref):
        @pl.when(pl.program_id(1) == 0)
        def _():
            dw_ref[...] = jnp.zeros_like(dw_ref)

        _, vjp = jax.vjp(lambda y, gl, w: _sigmoid(gl) * _nn16(y, w),
                         y_ref[...].astype(F32), gl_ref[...].astype(F32), w_ref[...].astype(F32))
        dy, dgl, dw = vjp(dm_ref[...])
        dy_ref[...] = dy
        dgl_ref[...] = dgl.astype(BF16)
        dw_ref[...] += dw

    branch_rows = lambda w: pl.BlockSpec((tr, w), lambda n, i: (i, n))
    weight = pl.BlockSpec((None, BRANCH_W, D_MODEL), lambda n, i: (n, 0, 0))
    return pl.pallas_call(
        body, name="up_bwd", grid=(4, T // tr),
        in_specs=[branch_rows(BRANCH_W), branch_rows(D_MODEL), pl.BlockSpec((tr, D_MODEL), lambda n, i: (i, 0)), weight],
        out_specs=[branch_rows(BRANCH_W), branch_rows(D_MODEL), weight],
        out_shape=[jax.ShapeDtypeStruct((T, 4 * BRANCH_W), F32), jax.ShapeDtypeStruct((T, N_PAD), BF16),
                   jax.ShapeDtypeStruct(w_up.shape, F32)],
        compiler_params=_params(dimension_semantics=("arbitrary", "arbitrary")),
    )(ys, cols, dm, w_up)


def _matmul(name, a, b, kind, out_dtype, tm, tn, tk, after=None):
    if kind == "tn":
        (K, M), N = a.shape, b.shape[1]
    else:
        (M, K), N = a.shape, (b.shape[0] if kind == "nt" else b.shape[1])
    tm, tn, tk = min(tm, M), min(tn, N), min(tk, K)
    nk = K // tk
    dims = {"nn": _NN, "nt": _NT, "tn": _TN}[kind]

    n_after = 0 if after is None else 1

    def body(*refs):
        a_ref, b_ref, o_ref, acc = refs[0], refs[1], refs[2 + n_after], refs[3 + n_after:]
        part = lax.dot_general(a_ref[...], b_ref[...], (dims, ((), ())), preferred_element_type=F32)
        if nk == 1:
            o_ref[...] = part.astype(o_ref.dtype)
            return
        acc_ref = acc[0] if acc else o_ref
        k = pl.program_id(2)

        @pl.when(k == 0)
        def _():
            acc_ref[...] = part

        @pl.when(k > 0)
        def _():
            acc_ref[...] += part

        if acc:
            @pl.when(k == nk - 1)
            def _():
                o_ref[...] = acc_ref[...].astype(o_ref.dtype)

    a_spec = pl.BlockSpec((tk, tm), lambda i, j, k: (k, i)) if kind == "tn" else pl.BlockSpec((tm, tk), lambda i, j, k: (i, k))
    b_spec = pl.BlockSpec((tn, tk), lambda i, j, k: (j, k)) if kind == "nt" else pl.BlockSpec((tk, tn), lambda i, j, k: (k, j))
    return pl.pallas_call(
        body, name=name, grid=(M // tm, N // tn, nk), in_specs=[a_spec, b_spec] + [_ANY] * n_after,
        out_specs=pl.BlockSpec((tm, tn), lambda i, j, k: (i, j)),
        out_shape=jax.ShapeDtypeStruct((M, N), out_dtype),
        scratch_shapes=[pltpu.VMEM((tm, tn), F32)] if nk > 1 and out_dtype != F32 else [],
        compiler_params=_params(dimension_semantics=("arbitrary", "arbitrary", "arbitrary")),
    )(a, b, *([] if after is None else [after]))


def _pre_fn(first, _, rows, halos, params):
    return None, [_rms(rows[0], params[0])]


def _pre_fn_res(first, _, rows, halos, params):
    return None, [_rms(rows[0], params[0]), rows[0]]


def _memkv_fn(first, _, rows, halos, params):
    g, w = params
    return None, [_nn16(_rms(rows[0], g), w)]


def _conv_silu(x, halo, w4, keep_halo):
    tr, hb = x.shape[0], halo.shape[0]
    xh = jnp.concatenate([halo * keep_halo, x], axis=0)
    acc = w4[3] * x
    for s in (1, 2, 3):
        acc = acc + w4[3 - s] * _rowsl(_rollr(xh, s), hb, hb + tr)
    return _silu(acc)


def _dn_fn(first, S, rows, halos, params):
    qp, kp, vp, z, ba = rows
    conv, a_vec, dt_vec, dnorm = params
    ba = _cols(ba, 0, LANES)
    tr = qp.shape[0]
    keep = jnp.where(first, 0.0, 1.0)
    q = _conv_silu(qp, halos[0], [conv[3 * j + 0] for j in range(4)], keep)
    k = _conv_silu(kp, halos[1], [conv[3 * j + 1] for j in range(4)], keep)
    v = _conv_silu(vp, halos[2], [conv[3 * j + 2] for j in range(4)], keep)
    qh, kh, vh = [], [], []
    for h in range(4):
        a, b = h * LANES, (h + 1) * LANES
        xq, xk = _cols(q, a, b), _cols(k, a, b)
        qh.append(xq * lax.rsqrt(jnp.sum(xq * xq, axis=1, keepdims=True) + EPS) * (LANES ** -0.5))
        kh.append(xk * lax.rsqrt(jnp.sum(xk * xk, axis=1, keepdims=True) + EPS))
        vh.append(_cols(v, a, b))
    beta_all = _sigmoid(ba)
    g_all = -jnp.exp(a_vec) * _softplus(ba + dt_vec)
    C = DN_CHUNK
    ii, jj = _iota((C, C), 0), _iota((C, C), 1)
    strict, incl = ii > jj, ii >= jj
    eye = (ii == jj).astype(F32)
    last_row = (_iota((C, 1), 0) == C - 1).astype(F32)
    n_chunk = tr // C
    pairs = [(c, h) for c in range(n_chunk) for h in range(4)]
    rows_of = lambda a, c: _rowsl(a, c * C, (c + 1) * C)
    gcs = [_nn32(incl.astype(F32), rows_of(g_all, c)) for c in range(n_chunk)]
    qc = {(c, h): rows_of(qh[h], c) for c, h in pairs}
    kc = {(c, h): rows_of(kh[h], c) for c, h in pairs}
    beta = {(c, h): _lane_pick(rows_of(beta_all, c), h) for c, h in pairs}
    gc = {(c, h): _lane_pick(gcs[c], 4 + h) for c, h in pairs}
    dec = {p: jnp.exp(jnp.where(incl, gc[p] - jnp.sum(eye * gc[p], axis=0, keepdims=True), 0.0)) for p in pairs}
    egc = {p: jnp.exp(gc[p]) for p in pairs}
    kb = {p: kc[p] * beta[p] for p in pairs}
    kq = {p: _nt16(jnp.concatenate([kb[p], qc[p]], axis=0), kc[p]) for p in pairs}
    P = {p: -jnp.where(strict, _rowsl(kq[p], 0, C) * dec[p], 0.0) for p in pairs}
    aqk = {p: jnp.where(incl, _rowsl(kq[p], C, 2 * C) * dec[p], 0.0) for p in pairs}
    tinv = {p: eye + P[p] for p in pairs}
    P = {p: _nn16(P[p], P[p]) for p in pairs}
    for j in range(5):
        if j < 4:
            pt = {p: _nn16(jnp.concatenate([P[p], tinv[p]], axis=0), P[p]) for p in pairs}
            tinv = {p: tinv[p] + _rowsl(pt[p], C, 2 * C) for p in pairs}
            P = {p: _rowsl(pt[p], 0, C) for p in pairs}
        else:
            tinv = {p: tinv[p] + _nn16(tinv[p], P[p]) for p in pairs}
    uw = {(c, h): _nn16(tinv[c, h], jnp.concatenate([rows_of(vh[h], c) * beta[c, h], kb[c, h] * egc[c, h]], axis=1))
          for c, h in pairs}
    S = list(S)
    ychunks = []
    for c in range(n_chunk):
        zc = rows_of(z, c)
        hs = range(4)
        ws = [_nn16(jnp.concatenate([_cols(uw[c, h], LANES, 2 * LANES), qc[c, h] * egc[c, h]], axis=0), S[h]) for h in hs]
        vnew = [_cols(uw[c, h], 0, LANES) - _rowsl(ws[h], 0, C) for h in hs]
        o = [_rowsl(ws[h], C, 2 * C) + _nn16(aqk[c, h], vnew[h]) for h in hs]
        glast = [jnp.sum(gc[c, h] * last_row, axis=0, keepdims=True) for h in hs]
        S = [S[h] * jnp.exp(glast[h]) + _tn16(kc[c, h] * jnp.exp(glast[h] - gc[c, h]), vnew[h]) for h in hs]
        ychunks.append(jnp.concatenate(
            [_rms(o[h], dnorm) * _silu(_cols(zc, h * LANES, (h + 1) * LANES)) for h in hs], axis=1))
    return S, [jnp.concatenate(ychunks, axis=0)]


def _gm_fn(first, _, rows, halos, params):
    uv, z = rows
    gnorm, ws, bs = params
    tr = uv.shape[0]
    guv = _gelu(uv)
    u = _cols(guv, 0, BRANCH_W)
    v = _rms(_cols(guv, BRANCH_W, 2 * BRANCH_W), gnorm)
    ii, jj = _iota((LANES, LANES), 0), _iota((LANES, LANES), 1)
    eye = (ii == jj).astype(F32)
    wsm = [jnp.where(ii >= jj, ws[g], 0.0) for g in range(4)]
    bcol = [jnp.sum(eye * bs[g], axis=1, keepdims=True) for g in range(4)]
    chunks = []
    for c in range(tr // LANES):
        vc = _rowsl(v, c * LANES, (c + 1) * LANES)
        chunks.append(jnp.concatenate(
            [_nn16(wsm[g], _cols(vc, g * LANES, (g + 1) * LANES)) + bcol[g] for g in range(4)], axis=1))
    return None, [u * jnp.concatenate(chunks, axis=0) * _silu(z)]


def _swa_fn(first, _, rows, halos, params):
    q, kvc, z = rows
    sink_vec = params[0]
    P = LANES
    n_blk = q.shape[0] // P
    r, cc = _iota((P, P), 0), _iota((P, P), 1)
    lane = _iota((1, P), 1)
    key = _iota((P, 2 * P), 1)
    dist = _iota((P, 2 * P), 0) + P - key
    in_window = (dist >= 0) & (dist < P)
    valid = [in_window & (key >= jnp.where(first, P, 0))] + [in_window] * (n_blk - 1)
    halves = [(lane < 64).astype(F32), (lane >= 64).astype(F32)]
    dup = [(r == kh * 64 + (cc & 63)).astype(F32) for kh in range(2)]
    kv_blk = [halos[0]] + [_rowsl(kvc, b * P, (b + 1) * P) for b in range(n_blk)]
    pairs = [(b, kh) for b in range(n_blk) for kh in range(2)]
    kkvv = {}
    for b in range(n_blk):
        kv = jnp.concatenate([kv_blk[b], kv_blk[b + 1]], axis=0)
        k_v = jnp.concatenate([_cols(kv, 0, P), _cols(kv, P, 2 * P)], axis=0)
        for kh in range(2):
            kkvv[b, kh] = _nn16(k_v, dup[kh])
    scores = {}
    for b, kh in pairs:
        q_b = _rowsl(q, b * P, (b + 1) * P)
        stacked = jnp.concatenate([_cols(q_b, (2 * kh + g // 2) * P, (2 * kh + g // 2 + 1) * P) * halves[g % 2]
                                   for g in range(4)], axis=0)
        scores[b, kh] = _nt16(stacked, _rowsl(kkvv[b, kh], 0, 2 * P))
    probs = {}
    for b, kh in pairs:
        ps = []
        for g in range(4):
            s = jnp.where(valid[b], _rowsl(scores[b, kh], g * P, (g + 1) * P) * 0.125, NEG_INF)
            sink = _lane_pick(sink_vec, kh * 4 + g)
            m = lax.stop_gradient(jnp.maximum(jnp.max(s, axis=1, keepdims=True), sink))
            e = jnp.exp(s - m)
            ps.append(e / (jnp.sum(e, axis=1, keepdims=True) + jnp.exp(sink - m)))
        probs[b, kh] = jnp.concatenate(ps, axis=0)
    outs = {p: _nn16(probs[p], _rowsl(kkvv[p], 2 * P, 4 * P)) for p in pairs}
    tile = [jnp.concatenate([_rowsl(outs[b, j // 2], (2 * (j % 2)) * P, (2 * (j % 2) + 1) * P) * halves[0]
                             + _rowsl(outs[b, j // 2], (2 * (j % 2) + 1) * P, (2 * (j % 2) + 2) * P) * halves[1]
                             for j in range(4)], axis=1) for b in range(n_blk)]
    return None, [jnp.concatenate(tile, axis=0) * _silu(z)]


def _mem_fn(first, _, rows, halos, params):
    q, z = rows
    mkv = params[0]
    heads = [(h * LANES, (h + 1) * LANES) for h in range(4)]
    scores = [_nt16(_cols(q, a, b), _cols(mkv, a, b)) * (LANES ** -0.5) for a, b in heads]
    probs = []
    for s in scores:
        e = jnp.exp(s - lax.stop_gradient(jnp.max(s, axis=1, keepdims=True)))
        probs.append(e / jnp.sum(e, axis=1, keepdims=True))
    outs = [_nn16(p, _cols(mkv, BRANCH_W + a, BRANCH_W + b)) for p, (a, b) in zip(probs, heads)]
    return None, [jnp.concatenate(outs, axis=1) * _silu(z)]


def _up_fn(first, _, rows, halos, params):
    ys, gl, w_up = rows[:4], rows[4], params[0]
    merged = None
    for n in range(4):
        term = _sigmoid(_cols(gl, n * D_MODEL, (n + 1) * D_MODEL)) * _nn16(ys[n], w_up[n])
        merged = term if merged is None else merged + term
    return None, [merged]


def _out_fn(first, _, rows, halos, params):
    x, merged = rows
    w, g = params
    return None, [x + _rms(_nn16(merged, w), g)]


def _loss_fn(first, _, rows, halos, params):
    y, t = rows
    d = y - t
    lrow = 0.5 * jnp.mean(d * d, axis=1, keepdims=True)
    return None, [d * (1.0 / D_MODEL), jnp.broadcast_to(lrow, (y.shape[0], LANES))]


TR = 256
BIG_TR = 512
SWA_TR = 256
DN_TR = 256
UP_TR = 256
UPB_TR = 1024
CONV_HALO = 16
CARRY = (4, LANES, LANES)


def _branch_rows(cols, g):
    hb = CONV_HALO
    a = [Row(cols, 512, O_AQ // 512, hb, g), Row(cols, 512, O_AK // 512, hb, g), Row(cols, 512, O_AV // 512, hb, g),
         Row(cols, 512, O_AZ // 512, 0, g), Row(cols, 256, O_BA // 256)]
    b = [Row(cols, 1024, O_BUV // 1024, 0, g), Row(cols, 512, O_BZ // 512, 0, g)]
    c = [Row(cols, 512, O_CQ // 512, 0, g), Row(cols, 256, O_CKV // 256, LANES), Row(cols, 512, O_CZ // 512, 0, g)]
    m = [Row(cols, 512, O_MQ // 512, 0, g), Row(cols, 512, O_MZ // 512, 0, g)]
    return a, b, c, m


def _layer_fwd(x, mem, W, late_weights=None):
    h = _rows_fwd("prenorm_fwd", _pre_fn, [Row(x, D_MODEL, 0)], [W["norm_pre"]], [(D_MODEL, BF16)], BIG_TR)[0]
    cols = _matmul("in_proj_fwd", h, W["w_pad"], "nt", BF16, 2048, 2048, 1024)
    if late_weights is not None:
        W = dict(W, **late_weights(cols))
    mem_kv = _rows_fwd("memkv_fwd", _memkv_fn, [Row(mem, D_MODEL, 0)], [W["norm_mem"], W["w_mem_kv"]],
                       [(D_MODEL, F32)], MEM_LEN)[0]
    ra, rb, rc, rm = _branch_rows(cols, True)
    y_a, csave = _rows_fwd("dn_fwd", _dn_fn, ra, [W["conv"], W["a_vec"], W["dt_vec"], W["dn_norm"]],
                           [(BRANCH_W, BF16)], DN_TR, CARRY)
    y_b = _rows_fwd("gm_fwd", _gm_fn, rb, [W["gm_norm"], W["spatial_w"], W["spatial_b"]], [(BRANCH_W, BF16)], BIG_TR)[0]
    y_c = _rows_fwd("swa_fwd", _swa_fn, rc, [W["sink_vec"]], [(BRANCH_W, BF16)], SWA_TR)[0]
    y_m = _rows_fwd("mem_fwd", _mem_fn, rm, [mem_kv], [(BRANCH_W, BF16)], BIG_TR)[0]
    ys = [y_a, y_b, y_c, y_m]
    merged = _rows_fwd("up_fwd", _up_fn, [Row(y, BRANCH_W, 0) for y in ys] + [Row(cols, 4 * D_MODEL, 0)],
                       [W["w_up"]], [(D_MODEL, BF16)], UP_TR)[0]
    x_new = _rows_fwd("out_fwd", _out_fn, [Row(x, D_MODEL, 0), Row(merged, D_MODEL, 0)],
                      [W["w_out"], W["norm_post"]], [(D_MODEL, F32)], TR)[0]
    return x_new, dict(x=x, h=h, cols=cols, mem_kv=mem_kv, csave=csave, ys=ys, merged=merged), W


def _layer_bwd(dxn, mem, W, sv, on_weight_grads=None):
    x, cols = sv["x"], sv["cols"]
    (dx_res, dm), (dw_out, dnorm_post), _ = _rows_bwd(
        "out_bwd", _out_fn, [Row(x, D_MODEL, 0), Row(sv["merged"], D_MODEL, 0)], [W["w_out"], W["norm_post"]],
        [dxn], TR)
    dys, dcols, dw_up = _up_bwd(jnp.concatenate(sv["ys"], axis=1), cols, dm, W["w_up"])
    dys = [Row(dys, BRANCH_W, n) for n in range(4)]
    ra, rb, rc, rm = _branch_rows(cols, "cols")
    (dba,), (dconv, da_vec, ddt_vec, ddn_norm), dcols = _rows_bwd(
        "dn_bwd", _dn_fn, ra, [W["conv"], W["a_vec"], W["dt_vec"], W["dn_norm"]], [dys[0]], DN_TR, CARRY,
        sv["csave"], dcols=dcols)
    _, (dgm_norm, dws, dbs), dcols = _rows_bwd(
        "gm_bwd", _gm_fn, rb, [W["gm_norm"], W["spatial_w"], W["spatial_b"]], [dys[1]], BIG_TR, dcols=dcols)
    (dkv_c,), (dsink,), dcols = _rows_bwd("swa_bwd", _swa_fn, rc, [W["sink_vec"]], [dys[2]], SWA_TR, dcols=dcols)
    _, (dmem_kv,), dcols = _rows_bwd("mem_bwd", _mem_fn, rm, [sv["mem_kv"]], [dys[3]], BIG_TR, dcols=dcols)
    dcols = _fill_misc(dcols, dkv_c, dba, BIG_TR)
    _, (dnorm_mem, dw_mem_kv), _ = _rows_bwd("memkv_bwd", _memkv_fn, [Row(mem, D_MODEL, 0, 0, False)],
                                             [W["norm_mem"], W["w_mem_kv"]], [dmem_kv], MEM_LEN)
    dw_pad = _matmul("in_proj_dw", dcols, sv["h"], "tn", BF16, 1024, 1024, 2048)
    grads = dict(norm_post=dnorm_post, norm_mem=dnorm_mem, w_pad=dw_pad, conv=dconv,
                 a_vec=da_vec, dt_vec=ddt_vec, dn_norm=ddn_norm, gm_norm=dgm_norm, spatial_w=dws, spatial_b=dbs,
                 sink_vec=dsink, w_mem_kv=dw_mem_kv, w_up=dw_up, w_out=dw_out)
    started = None if on_weight_grads is None else on_weight_grads(grads)
    dh = _matmul("in_proj_dx", dcols, W["w_pad"], "nn", F32, 1024, 1024, 2048, after=started)
    (dx,), (grads["norm_pre"],), _ = _rows_bwd("prenorm_bwd", _pre_fn_res, [Row(x, D_MODEL, 0)], [W["norm_pre"]],
                                               [dh, dx_res], BIG_TR)
    return dx, grads


def _lane_vec(v, off):
    return jnp.zeros((1, LANES), F32).at[0, off:off + v.shape[0]].set(v)


def _layer_weights(l, w_pad, conv_w, small, **late):
    return dict(
        late, w_pad=w_pad, conv=conv_w.reshape(4, 3, BRANCH_W).reshape(12, 1, BRANCH_W),
        norm_pre=small["norm_pre"][l][None], norm_post=small["norm_post"][l][None],
        norm_mem=small["norm_mem"][l][None],
        a_vec=_lane_vec(small["a_log"][l], 4), dt_vec=_lane_vec(small["dt_bias"][l], 4),
        dn_norm=small["dn_norm"][l][None], gm_norm=small["gm_norm"][l][None],
        spatial_w=small["spatial_w"][l], spatial_b=small["spatial_b"][l][:, None, :],
        sink_vec=_lane_vec(small["sinks"][l], 0))


_MESH = pl.DeviceIdType.MESH
_ANY = pl.BlockSpec(memory_space=pl.ANY)


def _position():
    return lax.axis_index("x"), lax.axis_index("y"), lax.axis_index("c")


def _remote(src, dst, send_sem, recv_sem, dev):
    return pltpu.make_async_remote_copy(src_ref=src, dst_ref=dst, send_sem=send_sem, recv_sem=recv_sem,
                                        device_id=dev, device_id_type=_MESH)


def _hbm_call(name, body, arrs, out_shapes, sems, aliases=None):
    return pl.pallas_call(
        body, name=name, in_specs=[_ANY] * len(arrs), out_specs=[_ANY] * len(out_shapes), out_shape=out_shapes,
        scratch_shapes=[pltpu.SemaphoreType.DMA((k,)) for k in sems], input_output_aliases=aliases or {},
        compiler_params=pltpu.CompilerParams(has_side_effects=True),
    )(*arrs)


def _other_chips(x, y):
    return [(1 - x, y), (x, 1 - y), (1 - x, 1 - y)]


def _gather_weights(arrs, relayed):
    n = len(arrs)

    def body(*refs):
        ins, outs = refs[:n], refs[n:2 * n]
        ici_send, ici_recv, d2d_send, d2d_recv = refs[2 * n:]
        x, y, c = _position()
        me = 2 * x + y
        xn, yn, dg = _other_chips(x, y)
        chip = lambda p: 2 * p[0] + p[1]
        sends = []

        def go(cp):
            cp.start()
            sends.append(cp)

        def ici(a, j, src, dst, to):
            return _remote(src, dst, ici_send.at[4 * a + j], ici_recv.at[4 * a + j], (*to, c))

        for a in range(n):
            go(ici(a, 0, ins[a].at[c], outs[a].at[c, me], xn))
            go(ici(a, 1, ins[a].at[c], outs[a].at[c, me], yn))
            if not relayed[a]:
                go(ici(a, 2, ins[a].at[c], outs[a].at[c, me], dg))
        for a in range(n):
            h = arrs[a].shape[1] // 2
            from_x, from_y = outs[a].at[c, chip(xn)], outs[a].at[c, chip(yn)]
            ici(a, 0, ins[a].at[c], from_x, xn).wait_recv()
            if relayed[a]:
                go(ici(a, 2, from_x.at[pl.ds(0, h)], from_x.at[pl.ds(0, h)], yn))
            ici(a, 1, ins[a].at[c], from_y, yn).wait_recv()
            if relayed[a]:
                go(ici(a, 3, from_y.at[pl.ds(h, h)], from_y.at[pl.ds(h, h)], xn))
            for j, slab in enumerate((from_x, from_y)):
                go(_remote(slab, slab, d2d_send.at[3 * a + j], d2d_recv.at[3 * a + j], (x, y, 1 - c)))
        for a in range(n):
            h = arrs[a].shape[1] // 2
            from_d = outs[a].at[c, chip(dg)]
            if relayed[a]:
                ici(a, 2, from_d.at[pl.ds(0, h)], from_d.at[pl.ds(0, h)], yn).wait_recv()
                ici(a, 3, from_d.at[pl.ds(h, h)], from_d.at[pl.ds(h, h)], xn).wait_recv()
            else:
                ici(a, 2, ins[a].at[c], from_d, dg).wait_recv()
            go(_remote(from_d, from_d, d2d_send.at[3 * a + 2], d2d_recv.at[3 * a + 2], (x, y, 1 - c)))
        for a in range(n):
            for j, p in enumerate((xn, yn, dg)):
                slab = outs[a].at[1 - c, chip(p)]
                _remote(slab, slab, d2d_send.at[3 * a + j], d2d_recv.at[3 * a + j], (x, y, 1 - c)).wait_recv()
        for cp in sends:
            cp.wait_send()

    return _hbm_call("gather_weights", body, arrs,
                     [jax.ShapeDtypeStruct((N_LAYER, N_CHIP) + a.shape[1:], a.dtype) for a in arrs],
                     [4 * n, 4 * n, 3 * n, 3 * n])


def _pair_exchange(arrs):
    n = len(arrs)

    def body(*refs):
        ins, outs = refs[:n], refs[n:2 * n]
        send_sems, recv_sems = refs[2 * n:]
        x, y, c = _position()
        cps = [_remote(ins[a].at[1 - c], outs[a], send_sems.at[a], recv_sems.at[a], (x, y, 1 - c)) for a in range(n)]
        for cp in cps:
            cp.start()
        for cp in cps:
            cp.wait_recv()
        for cp in cps:
            cp.wait_send()

    return _hbm_call("pair_exchange", body, arrs, [jax.ShapeDtypeStruct(a.shape[1:], a.dtype) for a in arrs], [n, n])


def _pair_share(arrs):
    n = len(arrs)

    def body(*refs):
        ins, outs = refs[:n], refs[n:2 * n]
        send_sems, recv_sems = refs[2 * n:]
        x, y, c = _position()
        cps = [_remote(ins[a].at[c], outs[a].at[c], send_sems.at[a], recv_sems.at[a], (x, y, 1 - c)) for a in range(n)]
        for cp in cps:
            cp.start()
        for a in range(n):
            _remote(ins[a].at[c], outs[a].at[1 - c], send_sems.at[a], recv_sems.at[a], (x, y, 1 - c)).wait_recv()
        for cp in cps:
            cp.wait_send()

    return _hbm_call("pair_share", body, arrs, [jax.ShapeDtypeStruct(a.shape, a.dtype) for a in arrs], [n, n],
                     {a: a for a in range(n)})


def _pair_forward(arrs):
    n = len(arrs)

    def body(*refs):
        ins, outs = refs[:n], refs[n:2 * n]
        send_sems, recv_sems = refs[2 * n:]
        x, y, c = _position()
        sends = []
        for a in range(n):
            for j, (px, py) in enumerate(_other_chips(x, y)):
                sends.append(_remote(ins[a].at[c, 2 * px + py], outs[a].at[c, 2 * px + py], send_sems.at[3 * a + j],
                                     recv_sems.at[3 * a + j], (x, y, 1 - c)))
                sends[-1].start()
        for a in range(n):
            for j, (px, py) in enumerate(_other_chips(x, y)):
                slab = outs[a].at[1 - c, 2 * px + py]
                _remote(slab, slab, send_sems.at[3 * a + j], recv_sems.at[3 * a + j], (x, y, 1 - c)).wait_recv()
        for cp in sends:
            cp.wait_send()

    return _hbm_call("pair_forward", body, arrs, [jax.ShapeDtypeStruct(a.shape, a.dtype) for a in arrs],
                     [3 * n, 3 * n], {a: a for a in range(n)})


_HBM = pl.BlockSpec(memory_space=pltpu.HBM)
_SEM = pl.BlockSpec(memory_space=pltpu.SEMAPHORE)
_EFFECT = pltpu.SideEffectType.DATAFLOW_SIDE_EFFECTING


def _chip_copies(kind, srcs, lands, send_sems, recv_sems):
    x, y, c = _position()
    me = 2 * x + y
    sends, recvs = [], []
    for a in range(len(srcs)):
        for j, (px, py) in enumerate(_other_chips(x, y)):
            s, sems, dev = 2 * px + py, (send_sems.at[3 * a + j], recv_sems.at[3 * a + j]), (px, py, c)
            if kind == "gather":
                sends.append(_remote(srcs[a].at[c], lands[a].at[c, me], *sems, dev))
                recvs.append(_remote(srcs[a].at[c], lands[a].at[c, s], *sems, dev))
            else:
                sends.append(_remote(srcs[a].at[s], lands[a].at[me], *sems, dev))
                recvs.append(_remote(srcs[a].at[me], lands[a].at[s], *sems, dev))
    return sends, recvs


def _split_start(name, kind, srcs, land_shapes, after):
    n = len(srcs)

    def body(*refs):
        sends, _ = _chip_copies(kind, refs[:n], refs[n:2 * n], refs[2 * n + 1], refs[2 * n + 2])
        for cp in sends:
            cp.start()
        refs[-1][...] = jnp.zeros_like(refs[-1])

    hbm = lambda a: pltpu.with_memory_space_constraint(a, pltpu.HBM)
    lands = [lax.empty(s.shape, s.dtype) for s in land_shapes]
    outs = pl.pallas_call(
        body, name=name, in_specs=[_HBM] * (2 * n) + [_ANY],
        out_specs=[_SEM, _SEM] + [_HBM] * (2 * n) + [pl.BlockSpec(memory_space=pltpu.VMEM)],
        out_shape=[pltpu.SemaphoreType.DMA((3 * n,)), pltpu.SemaphoreType.DMA((3 * n,))]
        + [pltpu.HBM(a.shape, a.dtype) for a in list(srcs) + lands] + [jax.ShapeDtypeStruct((8, LANES), F32)],
        input_output_aliases={i: 2 + i for i in range(2 * n)},
        compiler_params=pltpu.CompilerParams(has_side_effects=_EFFECT),
    )(*[hbm(a) for a in srcs], *[hbm(a) for a in lands], after)
    return outs[0], outs[1], list(outs[2:2 + 2 * n]), outs[-1]


def _split_wait(name, kind, started, after):
    send_sems, recv_sems, thru, _ = started
    n = len(thru) // 2

    def body(*refs):
        sends, recvs = _chip_copies(kind, refs[:n], refs[n:2 * n], refs[2 * n], refs[2 * n + 1])
        for cp in sends:
            cp.wait_send()
        for cp in recvs:
            cp.wait_recv()

    outs = pl.pallas_call(
        body, name=name, in_specs=[_HBM] * (2 * n) + [_SEM, _SEM, _ANY], out_specs=[_HBM] * (2 * n),
        out_shape=[pltpu.HBM(a.shape, a.dtype) for a in thru], input_output_aliases={i: i for i in range(2 * n)},
        compiler_params=pltpu.CompilerParams(has_side_effects=_EFFECT),
    )(*thru, send_sems, recv_sems, after)
    return list(outs[:n]), list(outs[n:])


def _allreduce_small(g):
    def body(g_ref, o_ref, pair_buf, chip_buf, send_sems, recv_sems):
        x, y, c = _position()
        me = 2 * x + y
        sib = (x, y, 1 - c)
        to_sib = _remote(g_ref.at[1 - c], pair_buf, send_sems.at[0], recv_sems.at[0], sib)
        to_sib.start()
        to_sib.wait_recv()
        chip_buf[me] = g_ref[c] + pair_buf[...]
        sends = [to_sib]
        chips = _other_chips(x, y)
        for j, (px, py) in enumerate(chips):
            sends.append(_remote(chip_buf.at[me], chip_buf.at[me], send_sems.at[1 + j], recv_sems.at[1 + j], (px, py, c)))
            sends[-1].start()
        for j, (px, py) in enumerate(chips):
            _remote(chip_buf.at[me], chip_buf.at[2 * px + py], send_sems.at[1 + j], recv_sems.at[1 + j],
                    (px, py, c)).wait_recv()
        o_ref[c] = ((chip_buf[0] + chip_buf[1]) + chip_buf[2]) + chip_buf[3]
        sends.append(_remote(o_ref.at[c], o_ref.at[c], send_sems.at[4], recv_sems.at[4], sib))
        sends[-1].start()
        _remote(o_ref.at[c], o_ref.at[1 - c], send_sems.at[4], recv_sems.at[4], sib).wait_recv()
        for cp in sends:
            cp.wait_send()

    vmem = pl.BlockSpec(memory_space=pltpu.VMEM)
    return pl.pallas_call(
        body, name="allreduce_small", in_specs=[vmem], out_specs=vmem, out_shape=jax.ShapeDtypeStruct(g.shape, F32),
        scratch_shapes=[pltpu.VMEM(g.shape[1:], F32), pltpu.VMEM((N_CHIP,) + g.shape[1:], F32),
                        pltpu.SemaphoreType.DMA((5,)), pltpu.SemaphoreType.DMA((5,))],
        compiler_params=_params(),
    )(g)


EW_ROWS = 512


def _ew(name, fn, ins, n_out, out_dtype=F32, out_slot=None, into=None):
    def dims(a):
        return a[0].shape[1:] if isinstance(a, tuple) else a.shape

    R, w = dims(ins[0])
    tr = EW_ROWS if R % EW_ROWS == 0 else R
    n_into = len(into) if into else 0

    def body(c_ref, *refs):
        outs = fn(*[r[...] for r in refs[:len(ins)]])
        for r, v in zip(refs[len(ins) + n_into:], outs):
            r[...] = v.astype(r.dtype)

    def lead_spec(l):
        if l == "c":
            return pl.BlockSpec((None, tr, w), lambda i, c_ref: (c_ref[0], i, 0))
        return pl.BlockSpec((None, tr, w), lambda i, c_ref, s=l: (s, i, 0))

    plain = pl.BlockSpec((tr, w), lambda i, c_ref: (i, 0))
    in_specs = [lead_spec(a[1]) if isinstance(a, tuple) else plain for a in ins] + [_ANY] * n_into
    out_spec = plain if out_slot is None else lead_spec(out_slot)
    out_shape = jax.ShapeDtypeStruct((R, w) if out_slot is None else (2, R, w), out_dtype)
    return pl.pallas_call(
        body, name=name,
        grid_spec=pltpu.PrefetchScalarGridSpec(num_scalar_prefetch=1, grid=(R // tr,), in_specs=in_specs,
                                               out_specs=[out_spec] * n_out),
        out_shape=[out_shape] * n_out, input_output_aliases={1 + len(ins) + j: j for j in range(n_into)},
        compiler_params=_params(dimension_semantics=("arbitrary",)),
    )(lax.axis_index("c").astype(jnp.int32).reshape(1), *[a[0] if isinstance(a, tuple) else a for a in ins],
      *(into or []))


def _adamw_fn(w, g, m, v):
    m = ADAM_B1 * m + (1.0 - ADAM_B1) * g
    v = ADAM_B2 * v + (1.0 - ADAM_B2) * (g * g)
    m_hat = m / (1.0 - ADAM_B1 ** ADAM_STEP)
    v_hat = v / (1.0 - ADAM_B2 ** ADAM_STEP)
    delta = -ADAM_LR * (m_hat / (jnp.sqrt(v_hat) + ADAM_EPS) + ADAM_WD * w)
    return delta, m, v


def _adamw_layer(name, l, w, g, m, v, into):
    k = w.shape[-1]
    three = lambda a: (a.reshape(N_LAYER, -1, k), l)
    fn = lambda w_, g_, m_, v_: _adamw_fn(w_, g_, m_, v_) + (g_,)
    outs = _ew(name, fn, [three(w), g.reshape(-1, k), three(m), three(v)], 4, out_slot=l,
               into=None if into is None else [a.reshape(N_LAYER, -1, k) for a in into])
    return [o.reshape(w.shape) for o in outs]


def _adamw_rows(name, l, w, g, m, v, into):
    _, R, k = w.shape
    n_into = len(into) if into else 0

    def body(*refs):
        w_ref, g_ref, m_ref, v_ref = refs[:4]
        d_out, m_out, v_out, g_out = refs[4 + n_into:]
        g_blk = g_ref[...]
        d_out[...], m_out[...], v_out[...] = _adamw_fn(w_ref[...], g_blk, m_ref[...], v_ref[...])
        g_out[...] = g_blk

    spec = pl.BlockSpec((None, EW_ROWS, k), lambda i: (l, i, 0))
    return pl.pallas_call(
        body, name=name, grid=(-(-R // EW_ROWS),),
        in_specs=[spec, pl.BlockSpec((EW_ROWS, k), lambda i: (i, 0)), spec, spec] + [_ANY] * n_into,
        out_specs=[spec] * 4, out_shape=[jax.ShapeDtypeStruct((N_LAYER, R, k), F32)] * 4,
        input_output_aliases={4 + j: j for j in range(n_into)},
        compiler_params=_params(dimension_semantics=("arbitrary",)),
    )(w, g, m, v, *(into or []))


_SMALL = [("norm_pre", (2, 1024)), ("norm_post", (2, 1024)), ("norm_mem", (2, 1024)), ("a_log", (2, 4)),
          ("dt_bias", (2, 4)), ("dn_norm", (2, 128)), ("gm_norm", (2, 512)), ("spatial_w", (2, 4, 128, 128)),
          ("spatial_b", (2, 4, 128)), ("sinks", (2, 8)), ("loss", (2, 1))]
_SMALL_ROWS = 208
_BIG = ["w_in", "conv_w", "w_mem_kv", "w_up", "w_out"]
_NAMES = ["norm_pre", "norm_post", "norm_mem", "w_in", "conv_w", "a_log", "dt_bias", "dn_norm", "gm_norm",
          "spatial_w", "spatial_b", "sinks", "w_mem_kv", "w_up", "w_out"]


def _size(shape):
    n = 1
    for s in shape:
        n *= s
    return n


def _pack_small(d):
    rows = []
    for n, shp in _SMALL:
        a = d[n].reshape(N_LAYER, -1)
        rows.append(a.reshape(-1, 1024) if a.shape[1] > 1024 else jnp.pad(a, ((0, 6), (0, 1024 - a.shape[1]))))
    assert sum(r.shape[0] for r in rows) == _SMALL_ROWS
    return jnp.concatenate(rows, axis=0)


def _unpack_small(p):
    out, off = {}, 0
    for n, shp in _SMALL:
        c = _size(shp) // N_LAYER
        k = 8 if c <= 1024 else _size(shp) // 1024
        out[n] = (p[off:off + N_LAYER, :c] if c <= 1024 else p[off:off + k]).reshape(shp)
        off += k
    return out


_HALF_SHAPE = {"w_in": (SHARD_PAD // 2, D_MODEL), "conv_w": (2, 3 * BRANCH_W // N_CHIP), "w_mem_kv": (128, D_MODEL),
               "w_up": (2, BRANCH_W, D_MODEL // N_CHIP), "w_out": (128, D_MODEL)}


def _chip_major(g):
    g = jnp.swapaxes(g, 0, 1)
    return g.reshape((N_CHIP, 2 * g.shape[2]) + g.shape[3:])


def _half_major(g):
    g = g.reshape((N_CHIP, 2, g.shape[1] // 2) + g.shape[2:])
    return jnp.swapaxes(g, 0, 1).astype(BF16)


N_EARLY = 2


def _early_views(l, g_in, g_conv, small):
    return _layer_weights(l, _w_pad_from_slabs(g_in),
                          _chip_major(g_conv).transpose(1, 0, 2).reshape(4, 3 * BRANCH_W), small)


def _late_views(g_kv, g_up, g_out):
    return dict(w_mem_kv=_chip_major(g_kv).reshape(D_MODEL, D_MODEL),
                w_up=_chip_major(g_up).transpose(1, 2, 0, 3).reshape(4, BRANCH_W, D_MODEL),
                w_out=_chip_major(g_out).reshape(D_MODEL, D_MODEL))


def _pair_sums(g):
    add2 = lambda a, b: [a.astype(F32) + b.astype(F32)]
    mine, theirs = _slabs_from_pad(g["w_pad"])
    pair = [_ew("pair_sum_w_in", add2, [mine.reshape(-1, D_MODEL), theirs.reshape(-1, D_MODEL)], 1, BF16)[0]
            .reshape(mine.shape)]
    rest = [_half_major(g["conv"].reshape(4, N_CHIP, 3 * BRANCH_W // N_CHIP).transpose(1, 0, 2)),
            _half_major(g["w_mem_kv"].reshape(N_CHIP, D_MODEL // N_CHIP, D_MODEL)),
            _half_major(g["w_up"].reshape(4, BRANCH_W, N_CHIP, D_MODEL // N_CHIP).transpose(2, 0, 1, 3)),
            _half_major(g["w_out"].reshape(N_CHIP, D_MODEL // N_CHIP, D_MODEL))]
    for n, b, p in zip(_BIG[1:], rest, _pair_exchange(rest)):
        k = b.shape[-1]
        pair.append(_ew("pair_sum_" + n, add2, [(b.reshape(2, -1, k), "c"), p.reshape(-1, k)], 1, BF16)[0]
                    .reshape(p.shape))
    return pair


def _chip_sums(landed, pair, me):
    add4 = lambda a, b, c_, d: [((a.astype(F32) + b.astype(F32)) + c_.astype(F32)) + d.astype(F32)]
    totals = []
    for n, r, q in zip(_BIG, landed, pair):
        r = _own_slot(r, lax.dynamic_index_in_dim(q, me, 0), me, 0)
        k = r.shape[-1]
        totals.append(_ew("chip_sum_" + n, add4, [(r.reshape(N_CHIP, -1, k), s) for s in range(N_CHIP)], 1,
                          out_slot="c")[0].reshape((2,) + r.shape[1:]))
    return totals


def _own_slot(buf, mine, me, axis):
    return lax.dynamic_update_index_in_dim(buf, mine.astype(buf.dtype), me, axis)


def kernel(x, mem, norm_pre, norm_post, norm_mem, w_in, conv_w, a_log, dt_bias, dn_norm, gm_norm, spatial_w, spatial_b, sinks, w_mem_kv, w_up, w_out, loss_target, m_norm_pre, m_norm_post, m_norm_mem, m_w_in, m_conv_w, m_a_log, m_dt_bias, m_dn_norm, m_gm_norm, m_spatial_w, m_spatial_b, m_sinks, m_w_mem_kv, m_w_up, m_w_out, v_norm_pre, v_norm_post, v_norm_mem, v_w_in, v_conv_w, v_a_log, v_dt_bias, v_dn_norm, v_gm_norm, v_spatial_w, v_spatial_b, v_sinks, v_w_mem_kv, v_w_up, v_w_out):
    w = dict(norm_pre=norm_pre, norm_post=norm_post, norm_mem=norm_mem, w_in=w_in, conv_w=conv_w, a_log=a_log,
             dt_bias=dt_bias, dn_norm=dn_norm, gm_norm=gm_norm, spatial_w=spatial_w, spatial_b=spatial_b, sinks=sinks,
             w_mem_kv=w_mem_kv, w_up=w_up, w_out=w_out)
    m = dict(norm_pre=m_norm_pre, norm_post=m_norm_post, norm_mem=m_norm_mem, w_in=m_w_in, conv_w=m_conv_w,
             a_log=m_a_log, dt_bias=m_dt_bias, dn_norm=m_dn_norm, gm_norm=m_gm_norm, spatial_w=m_spatial_w,
             spatial_b=m_spatial_b, sinks=m_sinks, w_mem_kv=m_w_mem_kv, w_up=m_w_up, w_out=m_w_out)
    v = dict(norm_pre=v_norm_pre, norm_post=v_norm_post, norm_mem=v_norm_mem, w_in=v_w_in, conv_w=v_conv_w,
             a_log=v_a_log, dt_bias=v_dt_bias, dn_norm=v_dn_norm, gm_norm=v_gm_norm, spatial_w=v_spatial_w,
             spatial_b=v_spatial_b, sinks=v_sinks, w_mem_kv=v_w_mem_kv, w_up=v_w_up, w_out=v_w_out)
    me = 2 * lax.axis_index("x") + lax.axis_index("y")

    tr = lambda a: a.transpose(0, 2, 1)
    w_t = tr(w_in)
    w_in_t = jnp.pad(w_t.astype(BF16), ((0, 0), (0, SHARD_PAD - SHARD_IN), (0, 0)))
    for d in (w, m, v):
        d["loss"] = jnp.zeros((N_LAYER, 1), F32)
    local = dict(w_in=w_in_t, conv_w=conv_w, w_mem_kv=w_mem_kv.astype(BF16), w_up=w_up.astype(BF16),
                 w_out=w_out.astype(BF16))
    halves = lambda l: [local[n][l].reshape((2,) + _HALF_SHAPE[n]) for n in _BIG]
    own = lambda gathered, mine: [_own_slot(g, h[:, None], me, 1) for g, h in zip(gathered, mine)]
    lands = [jax.ShapeDtypeStruct((2, N_CHIP) + _HALF_SHAPE[n], local[n].dtype) for n in _BIG]
    h0 = halves(0)
    g0 = own(_gather_weights(h0[:N_EARLY], [True, False]), h0[:N_EARLY])
    rest0 = _split_start("gather_l0_rest_start", "gather", h0[N_EARLY:], lands[N_EARLY:], g0[1])
    started = _split_start("gather_l1_start", "gather", halves(1), lands, rest0[3])

    xl, meml = x[0], mem[0]
    W0 = _early_views(0, g0[0], g0[1], w)
    W0["norm_pre"] = W0["norm_pre"] + started[3][0, 0]

    def late0(cols):
        mine, landed = _split_wait("gather_l0_rest_wait", "gather", rest0, cols)
        return _late_views(*own(_pair_forward(landed), mine))

    x1, sv0, W0 = _layer_fwd(xl, meml, W0, late0)
    mine1, landed1 = _split_wait("gather_l1_wait", "gather", started, x1)
    g1 = own(_pair_forward(landed1), mine1)
    W1 = dict(_early_views(1, g1[0], g1[1], w), **_late_views(*g1[N_EARLY:]))
    x2, sv1, _ = _layer_fwd(x1, meml, W1)
    dy, lrows = _rows_fwd("loss", _loss_fn, [Row(x2, D_MODEL, 0), Row(loss_target[0], D_MODEL, 0)], [],
                          [(D_MODEL, F32), (LANES, F32)], BIG_TR)
    loss_local = jnp.sum(lrows[:, 0])

    scattering = {}

    def start_scatter(l):
        def on_weight_grads(g):
            pair = _pair_sums(g)
            scattering[l] = _split_start("scatter_l%d_start" % l, "scatter", pair,
                                         [jax.ShapeDtypeStruct(p.shape, p.dtype) for p in pair], pair[1])
            return scattering[l][3]
        return on_weight_grads

    dx1, grads1 = _layer_bwd(dy, meml, W1, sv1, start_scatter(1))
    dx, grads0 = _layer_bwd(dx1, meml, W0, sv0, start_scatter(0))
    pair1, landed1 = _split_wait("scatter_l1_wait", "scatter", scattering[1], dx)
    after_start = scattering[0][3][0, 0]
    grads = [grads0, grads1]

    small_local = dict(
        norm_pre=jnp.stack([g["norm_pre"][0] for g in grads]), norm_post=jnp.stack([g["norm_post"][0] for g in grads]),
        norm_mem=jnp.stack([g["norm_mem"][0] for g in grads]), a_log=jnp.stack([g["a_vec"][0, 4:8] for g in grads]),
        dt_bias=jnp.stack([g["dt_vec"][0, 4:8] for g in grads]), dn_norm=jnp.stack([g["dn_norm"][0] for g in grads]),
        gm_norm=jnp.stack([g["gm_norm"][0] for g in grads]), spatial_w=jnp.stack([g["spatial_w"] for g in grads]),
        spatial_b=jnp.stack([g["spatial_b"][:, 0, :] for g in grads]),
        sinks=jnp.stack([g["sink_vec"][0, :8] for g in grads]),
        loss=jnp.stack([loss_local, jnp.zeros((), F32)]).reshape(N_LAYER, 1))
    packed = _pack_small(small_local) + after_start
    gsmall_packed = _allreduce_small(packed.reshape(2, -1, 1024)).reshape(-1, 1024)

    d_s, m_s, v_s = _ew("adamw_small", _adamw_fn, [_pack_small(w), gsmall_packed, _pack_small(m), _pack_small(v)], 3)
    gsmall, dsmall, msmall, vsmall = (_unpack_small(p) for p in (gsmall_packed, d_s, m_s, v_s))
    g_o, d_o, m_o, v_o = dict(gsmall), dict(dsmall), dict(msmall), dict(vsmall)
    loss = gsmall["loss"][0, 0]
    m_t, v_t = tr(m["w_in"]), tr(v["w_in"])

    def update(l, totals, into):
        outs = {}
        for n, t in zip(_BIG, totals):
            g_l = t.reshape(local[n].shape[1:])
            if n == "w_in":
                outs[n] = _adamw_rows("adamw_" + n, l, w_t, g_l, m_t, v_t, into and into[n])
            else:
                outs[n] = _adamw_layer("adamw_" + n, l, w[n], g_l, m[n], v[n], into and into[n])
        return outs

    landed1[1] = landed1[1] + after_start.astype(landed1[1].dtype)
    outs1 = update(1, _pair_share(_chip_sums(landed1, pair1, me)), None)
    pair0, landed0 = _split_wait("scatter_l0_wait", "scatter", scattering[0], outs1["w_in"][0])
    outs = update(0, _pair_share(_chip_sums(landed0, pair0, me)), outs1)
    for n in _BIG:
        d_o[n], m_o[n], v_o[n], g_o[n] = [tr(o) for o in outs[n]] if n == "w_in" else outs[n]
    return (loss, dx[None], *[g_o[n] for n in _NAMES], *[d_o[n] for n in _NAMES], *[m_o[n] for n in _NAMES],
            *[v_o[n] for n in _NAMES])
```

```python
import collections
import functools

import jax
import jax.numpy as jnp
from jax import lax
from jax.experimental import pallas as pl
from jax.experimental.pallas import tpu as pltpu

F32 = jnp.float32
BF16 = jnp.bfloat16

D_MODEL = 1024
BRANCH_W = 512
MEM_LEN = 256
N_LAYER = 2
N_CHIP = 4
EPS = 1e-6
NEG_INF = -1e30
DN_CHUNK = 64
SW_HD = 64
LANES = 128
VMEM_LIMIT = 48 * 1024 * 1024

ADAM_LR, ADAM_B1, ADAM_B2, ADAM_EPS, ADAM_WD, ADAM_STEP = 0.001, 0.9, 0.999, 1e-08, 0.01, 10

N_PAD = 10240
O_GATE = 0
O_AQ, O_AK, O_AV, O_AZ = 4096, 4608, 5120, 5632
O_BUV, O_BZ = 6144, 7168
O_CKV, O_BA = 7680, 7936
O_CQ, O_CZ = 8192, 8704
O_MQ, O_MZ = 9216, 9728
O_MISC, W_MISC = O_CKV, 512
_PAD_SEGS = [(5896, 4096), (0, 512), (512, 512), (1024, 512), (1536, 512), (2056, 1024), (3080, 512),
             (4104, 128), (4232, 128), (2048, 8), (None, 120), (None, 128),
             (3592, 512), (4360, 512), (4872, 512), (5384, 512)]
D_IN = 9992
SHARD_IN = D_IN // N_CHIP


SHARD_PAD = 2560


def _pad_parts():
    parts, off = [], 0
    for s, n in _PAD_SEGS:
        a = s
        while s is not None and a < s + n:
            chip = a // SHARD_IN
            b = min(s + n, (chip + 1) * SHARD_IN)
            parts.append((chip, a - chip * SHARD_IN, off + a - s, b - a))
            a = b
        off += n
    return parts


PERM_ROWS = 1024
PERM_SLACK = 32


def _permute_rows(name, src, parts, n_out, out_dtype, pair_split=False):
    B, Z = PERM_ROWS, PERM_ROWS + PERM_SLACK
    w = src.shape[1]
    plans = []
    for blk in range(n_out // B):
        o, runs = blk * B, []
        for s, d, n in parts:
            lo, hi = max(d, o), min(d + n, o + B)
            if lo < hi:
                s0 = s + lo - d
                wa = s0 // 16 * 16
                wb = min(-(-(s0 + hi - lo) // 16) * 16, src.shape[0])
                runs.append((wa, wb - wa, s0 - (lo - o) - wa, lo - o, hi - o))
        plans.append(runs)
    max_runs = max(len(r) for r in plans)
    nblk = len(plans)

    per_half = nblk // 2

    def body(*refs):
        src_ref, out_ref = refs[0], refs[1]
        if pair_split:
            theirs_ref, (inbuf, obuf, insem, outsem, to_sib_sem, from_sib_sem) = refs[2], refs[3:]
            x, y, c = _position()
            sibling = (x, y, 1 - c)
        else:
            inbuf, obuf, insem, outsem = refs[2:]

        def in_copies(blk):
            return [pltpu.make_async_copy(src_ref.at[pl.ds(wa, ws)], inbuf.at[blk % 2, r, pl.ds(0, ws)],
                                          insem.at[blk % 2, r]) for r, (wa, ws, _, _, _) in enumerate(plans[blk])]

        class out_copy:
            def __init__(self, blk):
                self.blk, self.rows = blk, pl.ds((blk % per_half if pair_split else blk) * B, B)
                self.local = pltpu.make_async_copy(obuf.at[blk % 2], out_ref.at[self.rows], outsem.at[blk % 2])

            def _both(self, local_op, remote_op):
                if not pair_split:
                    return local_op(self.local)
                mine = c == self.blk // per_half
                pl.when(mine)(lambda: local_op(self.local))
                pl.when(jnp.logical_not(mine))(lambda: remote_op(_remote(
                    obuf.at[self.blk % 2], theirs_ref.at[self.rows], to_sib_sem.at[self.blk % 2],
                    from_sib_sem.at[self.blk % per_half], sibling)))

            def start(self):
                self._both(lambda cp: cp.start(), lambda cp: cp.start())

            def wait(self):
                self._both(lambda cp: cp.wait(), lambda cp: cp.wait_send())

        for cp in in_copies(0):
            cp.start()
        rid = _iota((B, 1), 0)
        for blk in range(nblk):
            if blk + 1 < nblk:
                for cp in in_copies(blk + 1):
                    cp.start()
            for cp in in_copies(blk):
                cp.wait()
            val = jnp.zeros((B, w), F32)
            for r, (wa, ws, t, l0, l1) in enumerate(plans[blk]):
                win = jnp.concatenate([inbuf[blk % 2, r, pl.ds(0, ws)].astype(F32), jnp.zeros((Z - ws, w), F32)], axis=0)
                moved = pltpu.roll(win, (-t) % Z, 0)[:B]
                val = jnp.where((rid >= l0) & (rid < l1), moved, val)
            if blk >= 2:
                out_copy(blk - 2).wait()
            obuf[blk % 2] = val.astype(out_dtype)
            out_copy(blk).start()
        for blk in range(max(nblk - 2, 0), nblk):
            out_copy(blk).wait()
        if pair_split:
            for i in range(per_half):
                rows = theirs_ref.at[pl.ds(i * B, B)]
                _remote(rows, rows, to_sib_sem.at[0], from_sib_sem.at[i], sibling).wait_recv()

    scratch = [pltpu.VMEM((2, max_runs, Z, w), src.dtype), pltpu.VMEM((2, B, w), out_dtype),
               pltpu.SemaphoreType.DMA((2, max_runs)), pltpu.SemaphoreType.DMA((2,))]
    if pair_split:
        scratch += [pltpu.SemaphoreType.DMA((2,)), pltpu.SemaphoreType.DMA((per_half,))]
    out_shape = jax.ShapeDtypeStruct((n_out // 2 if pair_split else n_out, w), out_dtype)
    return pl.pallas_call(
        body, name=name, in_specs=[_ANY], out_specs=[_ANY] * 2 if pair_split else _ANY,
        out_shape=[out_shape] * 2 if pair_split else out_shape, scratch_shapes=scratch, compiler_params=_params(),
    )(src)


def _slab_parts():
    h, out = SHARD_PAD // 2, []
    for chip, s, d, n in _pad_parts():
        a = s
        while a < s + n:
            half = a // h
            b = min(s + n, (half + 1) * h)
            out.append(((half * N_CHIP + chip) * h + a - half * h, d + a - s, b - a))
            a = b
    return out


def _w_pad_from_slabs(slabs):
    return _permute_rows("w_pad_rows", slabs.reshape(-1, slabs.shape[-1]), _slab_parts(), N_PAD, BF16)


def _slabs_from_pad(dw):
    mine, theirs = _permute_rows("w_pad_grad_rows", dw, [(d, s, n) for s, d, n in _slab_parts()],
                                 N_CHIP * SHARD_PAD, BF16, pair_split=True)
    shape = (N_CHIP, SHARD_PAD // 2, dw.shape[1])
    return mine.reshape(shape), theirs.reshape(shape)


def _dot(a, b, dims, prec):
    if prec == "bf16":
        return lax.dot_general(a.astype(BF16), b.astype(BF16), (dims, ((), ())), preferred_element_type=F32)
    return lax.dot_general(a, b, (dims, ((), ())), precision=lax.Precision.HIGHEST, preferred_element_type=F32)


_NN, _NT, _TN = ((1,), (0,)), ((1,), (1,)), ((0,), (0,))


def _make_mm(prec):
    @jax.custom_vjp
    def nn(a, b):
        return _dot(a, b, _NN, prec)

    @jax.custom_vjp
    def nt(a, b):
        return _dot(a, b, _NT, prec)

    @jax.custom_vjp
    def tn(a, b):
        return _dot(a, b, _TN, prec)

    nn.defvjp(lambda a, b: (nn(a, b), (a, b)), lambda r, g: (nt(g, r[1]), tn(r[0], g)))
    nt.defvjp(lambda a, b: (nt(a, b), (a, b)), lambda r, g: (nn(g, r[1]), tn(g, r[0])))
    tn.defvjp(lambda a, b: (tn(a, b), (a, b)), lambda r, g: (nt(r[1], g), nn(r[0], g)))
    return nn, nt, tn


_nn16, _nt16, _tn16 = _make_mm("bf16")
_nn32 = _make_mm("f32")[0]


def _make_slice(axis):
    @functools.partial(jax.custom_vjp, nondiff_argnums=(1, 2, 3))
    def sl(x, a, b, n):
        return x[a:b] if axis == 0 else x[:, a:b]

    def fwd(x, a, b, n):
        return sl(x, a, b, n), None

    def bwd(a, b, n, _, g):
        parts = []
        if a > 0:
            parts.append(jnp.zeros((a, g.shape[1]) if axis == 0 else (g.shape[0], a), g.dtype))
        parts.append(g)
        if n - b > 0:
            parts.append(jnp.zeros((n - b, g.shape[1]) if axis == 0 else (g.shape[0], n - b), g.dtype))
        return (jnp.concatenate(parts, axis=axis),)

    sl.defvjp(fwd, bwd)
    return sl


_sl0, _sl1 = _make_slice(0), _make_slice(1)


def _rowsl(x, a, b):
    return _sl0(x, a, b, x.shape[0])


def _cols(x, a, b):
    return _sl1(x, a, b, x.shape[1])


@functools.partial(jax.custom_vjp, nondiff_argnums=(1,))
def _rollr(x, s):
    return pltpu.roll(x, s, 0)


_rollr.defvjp(lambda x, s: (_rollr(x, s), None),
              lambda s, _, g: (pltpu.roll(g, g.shape[0] - s, 0),))


def _iota(shape, axis):
    return lax.broadcasted_iota(jnp.int32, shape, axis)


def _sigmoid(x):
    return lax.logistic(x)


def _silu(x):
    return x * _sigmoid(x)


def _gelu(x):
    return 0.5 * x * (1.0 + jnp.tanh(0.7978845608028654 * (x + 0.044715 * (x * x * x))))


def _softplus(x):
    return jnp.maximum(x, 0.0) + jnp.log(1.0 + jnp.exp(-jnp.abs(x)))


def _rms(x, g):
    return x * lax.rsqrt(jnp.mean(x * x, axis=-1, keepdims=True) + EPS) * g


def _lane_pick(x, lane):
    return jnp.sum(x * (_iota((1, x.shape[1]), 1) == lane).astype(F32), axis=1, keepdims=True)


Row = collections.namedtuple("Row", "arr w cb hb grad", defaults=(0, True))


def _full_spec(shape):
    return pl.BlockSpec(shape, lambda i, _n=len(shape): (0,) * _n)


def _load_params(refs):
    return [[p[g].astype(F32) for g in range(p.shape[0])] if len(p.shape) == 3 else p[...].astype(F32)
            for p in refs]


def _params(**kw):
    return pltpu.CompilerParams(vmem_limit_bytes=VMEM_LIMIT, **kw)


def _rows_fwd(name, fn, rows, params, outs, tr, carry=None):
    T = rows[0].arr.shape[0]
    n = T // tr
    halos = [r for r in rows if r.hb]
    nr, nh, npar, no = len(rows), len(halos), len(params), len(outs)

    def body(*refs):
        row_refs, halo_refs = refs[:nr], refs[nr:nr + nh]
        par_refs = refs[nr + nh:nr + nh + npar]
        out_refs = refs[nr + nh + npar:nr + nh + npar + no]
        rest = refs[nr + nh + npar + no:]
        first = pl.program_id(0) == 0
        cvals = None
        if carry is not None:
            csave_ref, carry_ref = rest

            @pl.when(first)
            def _():
                carry_ref[...] = jnp.zeros_like(carry_ref)

            cvals = [carry_ref[g] for g in range(carry[0])]
            for g in range(carry[0]):
                csave_ref[0, g] = cvals[g]
        c_out, o = fn(first, cvals, [r[...].astype(F32) for r in row_refs],
                      [h[...].astype(F32) for h in halo_refs], _load_params(par_refs))
        for r, v in zip(out_refs, o):
            r[...] = v.astype(r.dtype)
        if carry is not None:
            for g in range(carry[0]):
                carry_ref[g] = c_out[g]

    in_specs = [pl.BlockSpec((tr, r.w), lambda i, c=r.cb: (i, c)) for r in rows]
    in_specs += [pl.BlockSpec((r.hb, r.w), lambda i, c=r.cb, q=tr // r.hb: (jnp.maximum(i * q - 1, 0), c))
                 for r in halos]
    in_specs += [_full_spec(p.shape) for p in params]
    out_shape = [jax.ShapeDtypeStruct((T, w), dt) for w, dt in outs]
    out_specs = [pl.BlockSpec((tr, w), lambda i: (i, 0)) for w, _ in outs]
    scratch = []
    if carry is not None:
        out_shape.append(jax.ShapeDtypeStruct((n,) + carry, F32))
        out_specs.append(pl.BlockSpec((1,) + carry, lambda i: (i, 0, 0, 0)))
        scratch.append(pltpu.VMEM(carry, F32))
    return pl.pallas_call(
        body, name=name, grid=(n,), in_specs=in_specs, out_specs=out_specs, out_shape=out_shape,
        scratch_shapes=scratch, compiler_params=_params(dimension_semantics=("arbitrary",)),
    )(*[r.arr for r in rows], *[r.arr for r in halos], *params)


def _rows_bwd(name, fn, rows, params, douts, tr, carry=None, csave=None, dcols=None):
    T = rows[0].arr.shape[0]
    n = T // tr
    halos = [r for r in rows if r.hb]
    grows = [r for r in rows if r.grad is True]
    crows = [r for r in rows if r.grad == "cols"]
    wcols = sum(r.w for r in crows)
    nr, nh, npar, nd, ng = len(rows), len(halos), len(params), len(douts), len(grows)
    nc = 0 if carry is None else 1
    ncol = 1 if crows else 0
    nalias = 1 if (crows and dcols is not None) else 0

    def body(*refs):
        row_refs, halo_refs = refs[:nr], refs[nr:nr + nh]
        par_refs = refs[nr + nh:nr + nh + npar]
        k = nr + nh + npar
        csave_ref = refs[k] if nc else None
        dout_refs = refs[k + nc:k + nc + nd]
        k = k + nc + nd + nalias
        drow_refs = refs[k:k + ng]
        dcols_ref = refs[k + ng] if ncol else None
        dpar_refs = refs[k + ng + ncol:k + ng + ncol + npar]
        k = k + ng + ncol + npar
        dcarry_ref = refs[k] if nc else None
        hgrad_refs = refs[k + nc:]
        i = pl.program_id(0)
        first_tile = i == n - 1

        @pl.when(i == 0)
        def _():
            for r in dpar_refs:
                r[...] = jnp.zeros_like(r)
            for r in hgrad_refs:
                r[...] = jnp.zeros_like(r)
            if nc:
                dcarry_ref[...] = jnp.zeros_like(dcarry_ref)

        rv = [r[...].astype(F32) for r in row_refs]
        hv = [h[...].astype(F32) for h in halo_refs]
        pv = _load_params(par_refs)
        dov = [d[...].astype(F32) for d in dout_refs]
        if nc:
            cv = [csave_ref[0, g] for g in range(carry[0])]
            _, vjp = jax.vjp(lambda c, r, h, p: fn(first_tile, c, r, h, p), cv, rv, hv, pv)
            dc, dr, dh, dp = vjp(([dcarry_ref[g] for g in range(carry[0])], dov))
            for g in range(carry[0]):
                dcarry_ref[g] = dc[g]
        else:
            _, vjp = jax.vjp(lambda r, h, p: fn(first_tile, None, r, h, p)[1], rv, hv, pv)
            dr, dh, dp = vjp(dov)
        gi = hi = 0
        pieces = []
        for kk, r in enumerate(rows):
            d = dr[kk]
            if r.hb:
                carried = hgrad_refs[hi][...]
                d = d + (carried if tr == r.hb else
                         jnp.concatenate([jnp.zeros((tr - r.hb, r.w), F32), carried], axis=0))
                hgrad_refs[hi][...] = dh[hi]
                hi += 1
            if r.grad is True:
                drow_refs[gi][...] = d.astype(drow_refs[gi].dtype)
                gi += 1
            elif r.grad == "cols":
                pieces.append(d.astype(BF16))
        if ncol:
            dcols_ref[...] = pieces[0] if len(pieces) == 1 else jnp.concatenate(pieces, axis=1)
        for r, d in zip(dpar_refs, dp):
            if len(r.shape) == 3:
                for g in range(r.shape[0]):
                    r[g] += d[g]
            else:
                r[...] += d

    rev = lambda i: n - 1 - i
    in_specs = [pl.BlockSpec((tr, r.w), lambda i, c=r.cb: (rev(i), c)) for r in rows]
    in_specs += [pl.BlockSpec((r.hb, r.w), lambda i, c=r.cb, q=tr // r.hb: (jnp.maximum(rev(i) * q - 1, 0), c))
                 for r in halos]
    in_specs += [_full_spec(p.shape) for p in params]
    args = [r.arr for r in rows] + [r.arr for r in halos] + list(params)
    scratch = []
    if nc:
        in_specs.append(pl.BlockSpec((1,) + carry, lambda i: (rev(i), 0, 0, 0)))
        args.append(csave)
        scratch.append(pltpu.VMEM(carry, F32))
    douts = [d if isinstance(d, Row) else Row(d, d.shape[1], 0) for d in douts]
    in_specs += [pl.BlockSpec((tr, d.w), lambda i, c=d.cb: (rev(i), c)) for d in douts]
    args += [d.arr for d in douts]
    aliases = {}
    if nalias:
        aliases = {len(args): ng}
        in_specs.append(pl.BlockSpec(memory_space=pl.ANY))
        args.append(dcols)
    scratch += [pltpu.VMEM((r.hb, r.w), F32) for r in halos]
    out_shape = [jax.ShapeDtypeStruct((T, r.w), F32) for r in grows]
    out_specs = [pl.BlockSpec((tr, r.w), lambda i: (rev(i), 0)) for r in grows]
    if ncol:
        off = crows[0].cb * crows[0].w
        assert off % wcols == 0 and all(a.cb * a.w + a.w == b.cb * b.w for a, b in zip(crows, crows[1:]))
        out_shape.append(jax.ShapeDtypeStruct((T, N_PAD), BF16))
        out_specs.append(pl.BlockSpec((tr, wcols), lambda i, c=off // wcols: (rev(i), c)))
    out_shape += [jax.ShapeDtypeStruct(p.shape, F32) for p in params]
    out_specs += [_full_spec(p.shape) for p in params]
    res = pl.pallas_call(
        body, name=name, grid=(n,), in_specs=in_specs, out_specs=out_specs, out_shape=out_shape,
        scratch_shapes=scratch, input_output_aliases=aliases,
        compiler_params=_params(dimension_semantics=("arbitrary",)),
    )(*args)
    return list(res[:ng]), list(res[ng + ncol:]), (res[ng] if ncol else dcols)


def _fill_misc(dcols, dkv, dba, tr):
    T = dkv.shape[0]

    def body(kv_ref, ba_ref, _, o_ref):
        o_ref[...] = jnp.concatenate([kv_ref[...], ba_ref[...]], axis=1).astype(BF16)

    return pl.pallas_call(
        body, name="misc_bwd", grid=(T // tr,),
        in_specs=[pl.BlockSpec((tr, 256), lambda i: (i, 0)), pl.BlockSpec((tr, 256), lambda i: (i, 0)),
                  pl.BlockSpec(memory_space=pl.ANY)],
        out_specs=pl.BlockSpec((tr, W_MISC), lambda i: (i, O_MISC // W_MISC)),
        out_shape=jax.ShapeDtypeStruct((T, N_PAD), BF16), input_output_aliases={2: 0},
        compiler_params=_params(dimension_semantics=("arbitrary",)),
    )(dkv, dba, dcols)


def _up_bwd(ys, cols, dm, w_up):
    T = dm.shape[0]
    tr = min(UPB_TR, T)

    def body(y_ref, gl_ref, dm_ref, w_ref, dy_ref, dgl_ref, dw_ref):
        @pl.when(pl.program_id(1) == 0)
        def _():
            dw_ref[...] = jnp.zeros_like(dw_ref)

        _, vjp = jax.vjp(lambda y, gl, w: _sigmoid(gl) * _nn16(y, w),
                         y_ref[...].astype(F32), gl_ref[...].astype(F32), w_ref[...].astype(F32))
        dy, dgl, dw = vjp(dm_ref[...])
        dy_ref[...] = dy
        dgl_ref[...] = dgl.astype(BF16)
        dw_ref[...] += dw

    branch_rows = lambda w: pl.BlockSpec((tr, w), lambda n, i: (i, n))
    weight = pl.BlockSpec((None, BRANCH_W, D_MODEL), lambda n, i: (n, 0, 0))
    return pl.pallas_call(
        body, name="up_bwd", grid=(4, T // tr),
        in_specs=[branch_rows(BRANCH_W), branch_rows(D_MODEL), pl.BlockSpec((tr, D_MODEL), lambda n, i: (i, 0)), weight],
        out_specs=[branch_rows(BRANCH_W), branch_rows(D_MODEL), weight],
        out_shape=[jax.ShapeDtypeStruct((T, 4 * BRANCH_W), F32), jax.ShapeDtypeStruct((T, N_PAD), BF16),
                   jax.ShapeDtypeStruct(w_up.shape, F32)],
        compiler_params=_params(dimension_semantics=("arbitrary", "arbitrary")),
    )(ys, cols, dm, w_up)


def _matmul(name, a, b, kind, out_dtype, tm, tn, tk, after=None):
    if kind == "tn":
        (K, M), N = a.shape, b.shape[1]
    else:
        (M, K), N = a.shape, (b.shape[0] if kind == "nt" else b.shape[1])
    tm, tn, tk = min(tm, M), min(tn, N), min(tk, K)
    nk = K // tk
    dims = {"nn": _NN, "nt": _NT, "tn": _TN}[kind]

    n_after = 0 if after is None else 1

    def body(*refs):
        a_ref, b_ref, o_ref, acc = refs[0], refs[1], refs[2 + n_after], refs[3 + n_after:]
        part = lax.dot_general(a_ref[...], b_ref[...], (dims, ((), ())), preferred_element_type=F32)
        if nk == 1:
            o_ref[...] = part.astype(o_ref.dtype)
            return
        acc_ref = acc[0] if acc else o_ref
        k = pl.program_id(2)

        @pl.when(k == 0)
        def _():
            acc_ref[...] = part

        @pl.when(k > 0)
        def _():
            acc_ref[...] += part

        if acc:
            @pl.when(k == nk - 1)
            def _():
                o_ref[...] = acc_ref[...].astype(o_ref.dtype)

    a_spec = pl.BlockSpec((tk, tm), lambda i, j, k: (k, i)) if kind == "tn" else pl.BlockSpec((tm, tk), lambda i, j, k: (i, k))
    b_spec = pl.BlockSpec((tn, tk), lambda i, j, k: (j, k)) if kind == "nt" else pl.BlockSpec((tk, tn), lambda i, j, k: (k, j))
    return pl.pallas_call(
        body, name=name, grid=(M // tm, N // tn, nk), in_specs=[a_spec, b_spec] + [_ANY] * n_after,
        out_specs=pl.BlockSpec((tm, tn), lambda i, j, k: (i, j)),
        out_shape=jax.ShapeDtypeStruct((M, N), out_dtype),
        scratch_shapes=[pltpu.VMEM((tm, tn), F32)] if nk > 1 and out_dtype != F32 else [],
        compiler_params=_params(dimension_semantics=("arbitrary", "arbitrary", "arbitrary")),
    )(a, b, *([] if after is None else [after]))


def _pre_fn(first, _, rows, halos, params):
    return None, [_rms(rows[0], params[0])]


def _pre_fn_res(first, _, rows, halos, params):
    return None, [_rms(rows[0], params[0]), rows[0]]


def _memkv_fn(first, _, rows, halos, params):
    g, w = params
    return None, [_nn16(_rms(rows[0], g), w)]


def _conv_silu(x, halo, w4, keep_halo):
    tr, hb = x.shape[0], halo.shape[0]
    xh = jnp.concatenate([halo * keep_halo, x], axis=0)
    acc = w4[3] * x
    for s in (1, 2, 3):
        acc = acc + w4[3 - s] * _rowsl(_rollr(xh, s), hb, hb + tr)
    return _silu(acc)


def _dn_fn(first, S, rows, halos, params):
    qp, kp, vp, z, ba = rows
    conv, a_vec, dt_vec, dnorm = params
    ba = _cols(ba, 0, LANES)
    tr = qp.shape[0]
    keep = jnp.where(first, 0.0, 1.0)
    q = _conv_silu(qp, halos[0], [conv[3 * j + 0] for j in range(4)], keep)
    k = _conv_silu(kp, halos[1], [conv[3 * j + 1] for j in range(4)], keep)
    v = _conv_silu(vp, halos[2], [conv[3 * j + 2] for j in range(4)], keep)
    qh, kh, vh = [], [], []
    for h in range(4):
        a, b = h * LANES, (h + 1) * LANES
        xq, xk = _cols(q, a, b), _cols(k, a, b)
        qh.append(xq * lax.rsqrt(jnp.sum(xq * xq, axis=1, keepdims=True) + EPS) * (LANES ** -0.5))
        kh.append(xk * lax.rsqrt(jnp.sum(xk * xk, axis=1, keepdims=True) + EPS))
        vh.append(_cols(v, a, b))
    beta_all = _sigmoid(ba)
    g_all = -jnp.exp(a_vec) * _softplus(ba + dt_vec)
    C = DN_CHUNK
    ii, jj = _iota((C, C), 0), _iota((C, C), 1)
    strict, incl = ii > jj, ii >= jj
    eye = (ii == jj).astype(F32)
    last_row = (_iota((C, 1), 0) == C - 1).astype(F32)
    n_chunk = tr // C
    pairs = [(c, h) for c in range(n_chunk) for h in range(4)]
    rows_of = lambda a, c: _rowsl(a, c * C, (c + 1) * C)
    gcs = [_nn32(incl.astype(F32), rows_of(g_all, c)) for c in range(n_chunk)]
    qc = {(c, h): rows_of(qh[h], c) for c, h in pairs}
    kc = {(c, h): rows_of(kh[h], c) for c, h in pairs}
    beta = {(c, h): _lane_pick(rows_of(beta_all, c), h) for c, h in pairs}
    gc = {(c, h): _lane_pick(gcs[c], 4 + h) for c, h in pairs}
    dec = {p: jnp.exp(jnp.where(incl, gc[p] - jnp.sum(eye * gc[p], axis=0, keepdims=True), 0.0)) for p in pairs}
    egc = {p: jnp.exp(gc[p]) for p in pairs}
    kb = {p: kc[p] * beta[p] for p in pairs}
    kq = {p: _nt16(jnp.concatenate([kb[p], qc[p]], axis=0), kc[p]) for p in pairs}
    P = {p: -jnp.where(strict, _rowsl(kq[p], 0, C) * dec[p], 0.0) for p in pairs}
    aqk = {p: jnp.where(incl, _rowsl(kq[p], C, 2 * C) * dec[p], 0.0) for p in pairs}
    tinv = {p: eye + P[p] for p in pairs}
    P = {p: _nn16(P[p], P[p]) for p in pairs}
    for j in range(5):
        if j < 4:
            pt = {p: _nn16(jnp.concatenate([P[p], tinv[p]], axis=0), P[p]) for p in pairs}
            tinv = {p: tinv[p] + _rowsl(pt[p], C, 2 * C) for p in pairs}
            P = {p: _rowsl(pt[p], 0, C) for p in pairs}
        else:
            tinv = {p: tinv[p] + _nn16(tinv[p], P[p]) for p in pairs}
    uw = {(c, h): _nn16(tinv[c, h], jnp.concatenate([rows_of(vh[h], c) * beta[c, h], kb[c, h] * egc[c, h]], axis=1))
          for c, h in pairs}
    S = list(S)
    ychunks = []
    for c in range(n_chunk):
        zc = rows_of(z, c)
        hs = range(4)
        ws = [_nn16(jnp.concatenate([_cols(uw[c, h], LANES, 2 * LANES), qc[c, h] * egc[c, h]], axis=0), S[h]) for h in hs]
        vnew = [_cols(uw[c, h], 0, LANES) - _rowsl(ws[h], 0, C) for h in hs]
        o = [_rowsl(ws[h], C, 2 * C) + _nn16(aqk[c, h], vnew[h]) for h in hs]
        glast = [jnp.sum(gc[c, h] * last_row, axis=0, keepdims=True) for h in hs]
        S = [S[h] * jnp.exp(glast[h]) + _tn16(kc[c, h] * jnp.exp(glast[h] - gc[c, h]), vnew[h]) for h in hs]
        ychunks.append(jnp.concatenate(
            [_rms(o[h], dnorm) * _silu(_cols(zc, h * LANES, (h + 1) * LANES)) for h in hs], axis=1))
    return S, [jnp.concatenate(ychunks, axis=0)]


def _gm_fn(first, _, rows, halos, params):
    uv, z = rows
    gnorm, ws, bs = params
    tr = uv.shape[0]
    guv = _gelu(uv)
    u = _cols(guv, 0, BRANCH_W)
    v = _rms(_cols(guv, BRANCH_W, 2 * BRANCH_W), gnorm)
    ii, jj = _iota((LANES, LANES), 0), _iota((LANES, LANES), 1)
    eye = (ii == jj).astype(F32)
    wsm = [jnp.where(ii >= jj, ws[g], 0.0) for g in range(4)]
    bcol = [jnp.sum(eye * bs[g], axis=1, keepdims=True) for g in range(4)]
    chunks = []
    for c in range(tr // LANES):
        vc = _rowsl(v, c * LANES, (c + 1) * LANES)
        chunks.append(jnp.concatenate(
            [_nn16(wsm[g], _cols(vc, g * LANES, (g + 1) * LANES)) + bcol[g] for g in range(4)], axis=1))
    return None, [u * jnp.concatenate(chunks, axis=0) * _silu(z)]


def _swa_fn(first, _, rows, halos, params):
    q, kvc, z = rows
    sink_vec = params[0]
    P = LANES
    n_blk = q.shape[0] // P
    r, cc = _iota((P, P), 0), _iota((P, P), 1)
    lane = _iota((1, P), 1)
    key = _iota((P, 2 * P), 1)
    dist = _iota((P, 2 * P), 0) + P - key
    in_window = (dist >= 0) & (dist < P)
    valid = [in_window & (key >= jnp.where(first, P, 0))] + [in_window] * (n_blk - 1)
    halves = [(lane < SW_HD).astype(F32), (lane >= SW_HD).astype(F32)]
    dup = [(r == kh * SW_HD + (cc & (SW_HD - 1))).astype(F32) for kh in range(2)]
    kv_blk = [halos[0]] + [_rowsl(kvc, b * P, (b + 1) * P) for b in range(n_blk)]
    pairs = [(b, kh) for b in range(n_blk) for kh in range(2)]
    kkvv = {}
    for b in range(n_blk):
        kv = jnp.concatenate([kv_blk[b], kv_blk[b + 1]], axis=0)
        k_v = jnp.concatenate([_cols(kv, 0, P), _cols(kv, P, 2 * P)], axis=0)
        for kh in range(2):
            kkvv[b, kh] = _nn16(k_v, dup[kh])
    scores = {}
    for b, kh in pairs:
        q_b = _rowsl(q, b * P, (b + 1) * P)
        stacked = jnp.concatenate([_cols(q_b, (2 * kh + g // 2) * P, (2 * kh + g // 2 + 1) * P) * halves[g % 2]
                                   for g in range(4)], axis=0)
        scores[b, kh] = _nt16(stacked, _rowsl(kkvv[b, kh], 0, 2 * P))
    probs = {}
    for b, kh in pairs:
        ps = []
        for g in range(4):
            s = jnp.where(valid[b], _rowsl(scores[b, kh], g * P, (g + 1) * P) * (SW_HD ** -0.5), NEG_INF)
            sink = _lane_pick(sink_vec, kh * 4 + g)
            m = lax.stop_gradient(jnp.maximum(jnp.max(s, axis=1, keepdims=True), sink))
            e = jnp.exp(s - m)
            ps.append(e / (jnp.sum(e, axis=1, keepdims=True) + jnp.exp(sink - m)))
        probs[b, kh] = jnp.concatenate(ps, axis=0)
    outs = {p: _nn16(probs[p], _rowsl(kkvv[p], 2 * P, 4 * P)) for p in pairs}
    tile = [jnp.concatenate([_rowsl(outs[b, j // 2], (2 * (j % 2)) * P, (2 * (j % 2) + 1) * P) * halves[0]
                             + _rowsl(outs[b, j // 2], (2 * (j % 2) + 1) * P, (2 * (j % 2) + 2) * P) * halves[1]
                             for j in range(4)], axis=1) for b in range(n_blk)]
    return None, [jnp.concatenate(tile, axis=0) * _silu(z)]


def _mem_fn(first, _, rows, halos, params):
    q, z = rows
    mkv = params[0]
    heads = [(h * LANES, (h + 1) * LANES) for h in range(4)]
    scores = [_nt16(_cols(q, a, b), _cols(mkv, a, b)) * (LANES ** -0.5) for a, b in heads]
    probs = []
    for s in scores:
        e = jnp.exp(s - lax.stop_gradient(jnp.max(s, axis=1, keepdims=True)))
        probs.append(e / jnp.sum(e, axis=1, keepdims=True))
    outs = [_nn16(p, _cols(mkv, BRANCH_W + a, BRANCH_W + b)) for p, (a, b) in zip(probs, heads)]
    return None, [jnp.concatenate(outs, axis=1) * _silu(z)]


def _up_fn(first, _, rows, halos, params):
    ys, gl, w_up = rows[:4], rows[4], params[0]
    merged = None
    for n in range(4):
        term = _sigmoid(_cols(gl, n * D_MODEL, (n + 1) * D_MODEL)) * _nn16(ys[n], w_up[n])
        merged = term if merged is None else merged + term
    return None, [merged]


def _out_fn(first, _, rows, halos, params):
    x, merged = rows
    w, g = params
    return None, [x + _rms(_nn16(merged, w), g)]


def _loss_fn(first, _, rows, halos, params):
    y, t = rows
    d = y - t
    lrow = 0.5 * jnp.mean(d * d, axis=1, keepdims=True)
    return None, [d * (1.0 / D_MODEL), jnp.broadcast_to(lrow, (y.shape[0], LANES))]


TR = 256
BIG_TR = 512
SWA_TR = 512
DN_TR = 256
UP_TR = 256
UPB_TR = 1024
CONV_HALO = 16
CARRY = (4, LANES, LANES)


def _branch_rows(cols, g):
    hb = CONV_HALO
    a = [Row(cols, 512, O_AQ // 512, hb, g), Row(cols, 512, O_AK // 512, hb, g), Row(cols, 512, O_AV // 512, hb, g),
         Row(cols, 512, O_AZ // 512, 0, g), Row(cols, 256, O_BA // 256)]
    b = [Row(cols, 1024, O_BUV // 1024, 0, g), Row(cols, 512, O_BZ // 512, 0, g)]
    c = [Row(cols, 512, O_CQ // 512, 0, g), Row(cols, 256, O_CKV // 256, LANES), Row(cols, 512, O_CZ // 512, 0, g)]
    m = [Row(cols, 512, O_MQ // 512, 0, g), Row(cols, 512, O_MZ // 512, 0, g)]
    return a, b, c, m


def _layer_fwd(x, mem, W, late_weights=None):
    h = _rows_fwd("prenorm_fwd", _pre_fn, [Row(x, D_MODEL, 0)], [W["norm_pre"]], [(D_MODEL, BF16)], BIG_TR)[0]
    cols = _matmul("in_proj_fwd", h, W["w_pad"], "nt", BF16, 2048, 2048, 1024)
    if late_weights is not None:
        W = dict(W, **late_weights(cols))
    mem_kv = _rows_fwd("memkv_fwd", _memkv_fn, [Row(mem, D_MODEL, 0)], [W["norm_mem"], W["w_mem_kv"]],
                       [(D_MODEL, F32)], MEM_LEN)[0]
    ra, rb, rc, rm = _branch_rows(cols, True)
    y_a, csave = _rows_fwd("dn_fwd", _dn_fn, ra, [W["conv"], W["a_vec"], W["dt_vec"], W["dn_norm"]],
                           [(BRANCH_W, BF16)], DN_TR, CARRY)
    y_b = _rows_fwd("gm_fwd", _gm_fn, rb, [W["gm_norm"], W["spatial_w"], W["spatial_b"]], [(BRANCH_W, BF16)], BIG_TR)[0]
    y_c = _rows_fwd("swa_fwd", _swa_fn, rc, [W["sink_vec"]], [(BRANCH_W, BF16)], SWA_TR)[0]
    y_m = _rows_fwd("mem_fwd", _mem_fn, rm, [mem_kv], [(BRANCH_W, BF16)], BIG_TR)[0]
    ys = [y_a, y_b, y_c, y_m]
    merged = _rows_fwd("up_fwd", _up_fn, [Row(y, BRANCH_W, 0) for y in ys] + [Row(cols, 4 * D_MODEL, 0)],
                       [W["w_up"]], [(D_MODEL, BF16)], UP_TR)[0]
    x_new = _rows_fwd("out_fwd", _out_fn, [Row(x, D_MODEL, 0), Row(merged, D_MODEL, 0)],
                      [W["w_out"], W["norm_post"]], [(D_MODEL, F32)], TR)[0]
    return x_new, dict(x=x, h=h, cols=cols, mem_kv=mem_kv, csave=csave, ys=ys, merged=merged), W


def _layer_bwd(dxn, mem, W, sv, on_weight_grads=None):
    x, cols = sv["x"], sv["cols"]
    (dx_res, dm), (dw_out, dnorm_post), _ = _rows_bwd(
        "out_bwd", _out_fn, [Row(x, D_MODEL, 0), Row(sv["merged"], D_MODEL, 0)], [W["w_out"], W["norm_post"]],
        [dxn], TR)
    dys, dcols, dw_up = _up_bwd(jnp.concatenate(sv["ys"], axis=1), cols, dm, W["w_up"])
    dys = [Row(dys, BRANCH_W, n) for n in range(4)]
    ra, rb, rc, rm = _branch_rows(cols, "cols")
    (dba,), (dconv, da_vec, ddt_vec, ddn_norm), dcols = _rows_bwd(
        "dn_bwd", _dn_fn, ra, [W["conv"], W["a_vec"], W["dt_vec"], W["dn_norm"]], [dys[0]], DN_TR, CARRY,
        sv["csave"], dcols=dcols)
    _, (dgm_norm, dws, dbs), dcols = _rows_bwd(
        "gm_bwd", _gm_fn, rb, [W["gm_norm"], W["spatial_w"], W["spatial_b"]], [dys[1]], BIG_TR, dcols=dcols)
    (dkv_c,), (dsink,), dcols = _rows_bwd("swa_bwd", _swa_fn, rc, [W["sink_vec"]], [dys[2]], SWA_TR, dcols=dcols)
    _, (dmem_kv,), dcols = _rows_bwd("mem_bwd", _mem_fn, rm, [sv["mem_kv"]], [dys[3]], BIG_TR, dcols=dcols)
    dcols = _fill_misc(dcols, dkv_c, dba, BIG_TR)
    _, (dnorm_mem, dw_mem_kv), _ = _rows_bwd("memkv_bwd", _memkv_fn, [Row(mem, D_MODEL, 0, 0, False)],
                                             [W["norm_mem"], W["w_mem_kv"]], [dmem_kv], MEM_LEN)
    dw_pad = _matmul("in_proj_dw", dcols, sv["h"], "tn", BF16, 1024, 1024, 2048)
    grads = dict(norm_post=dnorm_post, norm_mem=dnorm_mem, w_pad=dw_pad, conv=dconv,
                 a_vec=da_vec, dt_vec=ddt_vec, dn_norm=ddn_norm, gm_norm=dgm_norm, spatial_w=dws, spatial_b=dbs,
                 sink_vec=dsink, w_mem_kv=dw_mem_kv, w_up=dw_up, w_out=dw_out)
    started = None if on_weight_grads is None else on_weight_grads(grads)
    dh = _matmul("in_proj_dx", dcols, W["w_pad"], "nn", F32, 1024, 1024, 2048, after=started)
    (dx,), (grads["norm_pre"],), _ = _rows_bwd("prenorm_bwd", _pre_fn_res, [Row(x, D_MODEL, 0)], [W["norm_pre"]],
                                               [dh, dx_res], BIG_TR)
    return dx, grads


def _lane_vec(v, off):
    return jnp.zeros((1, LANES), F32).at[0, off:off + v.shape[0]].set(v)


def _layer_weights(l, w_pad, conv_w, small, **late):
    return dict(
        late, w_pad=w_pad, conv=conv_w.reshape(4, 3, BRANCH_W).reshape(12, 1, BRANCH_W),
        norm_pre=small["norm_pre"][l][None], norm_post=small["norm_post"][l][None],
        norm_mem=small["norm_mem"][l][None],
        a_vec=_lane_vec(small["a_log"][l], 4), dt_vec=_lane_vec(small["dt_bias"][l], 4),
        dn_norm=small["dn_norm"][l][None], gm_norm=small["gm_norm"][l][None],
        spatial_w=small["spatial_w"][l], spatial_b=small["spatial_b"][l][:, None, :],
        sink_vec=_lane_vec(small["sinks"][l], 0))


_MESH = pl.DeviceIdType.MESH
_ANY = pl.BlockSpec(memory_space=pl.ANY)


def _position():
    return lax.axis_index("x"), lax.axis_index("y"), lax.axis_index("c")


def _remote(src, dst, send_sem, recv_sem, dev):
    return pltpu.make_async_remote_copy(src_ref=src, dst_ref=dst, send_sem=send_sem, recv_sem=recv_sem,
                                        device_id=dev, device_id_type=_MESH)


def _hbm_call(name, body, arrs, out_shapes, sems, aliases=None):
    return pl.pallas_call(
        body, name=name, in_specs=[_ANY] * len(arrs), out_specs=[_ANY] * len(out_shapes), out_shape=out_shapes,
        scratch_shapes=[pltpu.SemaphoreType.DMA((k,)) for k in sems], input_output_aliases=aliases or {},
        compiler_params=pltpu.CompilerParams(has_side_effects=True),
    )(*arrs)


def _other_chips(x, y):
    return [(1 - x, y), (x, 1 - y), (1 - x, 1 - y)]


def _gather_weights(arrs, relayed):
    n = len(arrs)

    def body(*refs):
        ins, outs = refs[:n], refs[n:2 * n]
        ici_send, ici_recv, d2d_send, d2d_recv = refs[2 * n:]
        x, y, c = _position()
        me = 2 * x + y
        xn, yn, dg = _other_chips(x, y)
        chip = lambda p: 2 * p[0] + p[1]
        sends = []

        def go(cp):
            cp.start()
            sends.append(cp)

        def ici(a, j, src, dst, to):
            return _remote(src, dst, ici_send.at[4 * a + j], ici_recv.at[4 * a + j], (*to, c))

        for a in range(n):
            go(ici(a, 0, ins[a].at[c], outs[a].at[c, me], xn))
            go(ici(a, 1, ins[a].at[c], outs[a].at[c, me], yn))
            if not relayed[a]:
                go(ici(a, 2, ins[a].at[c], outs[a].at[c, me], dg))
        for a in range(n):
            h = arrs[a].shape[1] // 2
            from_x, from_y = outs[a].at[c, chip(xn)], outs[a].at[c, chip(yn)]
            ici(a, 0, ins[a].at[c], from_x, xn).wait_recv()
            if relayed[a]:
                go(ici(a, 2, from_x.at[pl.ds(0, h)], from_x.at[pl.ds(0, h)], yn))
            ici(a, 1, ins[a].at[c], from_y, yn).wait_recv()
            if relayed[a]:
                go(ici(a, 3, from_y.at[pl.ds(h, h)], from_y.at[pl.ds(h, h)], xn))
            for j, slab in enumerate((from_x, from_y)):
                go(_remote(slab, slab, d2d_send.at[3 * a + j], d2d_recv.at[3 * a + j], (x, y, 1 - c)))
        for a in range(n):
            h = arrs[a].shape[1] // 2
            from_d = outs[a].at[c, chip(dg)]
            if relayed[a]:
                ici(a, 2, from_d.at[pl.ds(0, h)], from_d.at[pl.ds(0, h)], yn).wait_recv()
                ici(a, 3, from_d.at[pl.ds(h, h)], from_d.at[pl.ds(h, h)], xn).wait_recv()
            else:
                ici(a, 2, ins[a].at[c], from_d, dg).wait_recv()
            go(_remote(from_d, from_d, d2d_send.at[3 * a + 2], d2d_recv.at[3 * a + 2], (x, y, 1 - c)))
        for a in range(n):
            for j, p in enumerate((xn, yn, dg)):
                slab = outs[a].at[1 - c, chip(p)]
                _remote(slab, slab, d2d_send.at[3 * a + j], d2d_recv.at[3 * a + j], (x, y, 1 - c)).wait_recv()
        for cp in sends:
            cp.wait_send()

    return _hbm_call("gather_weights", body, arrs,
                     [jax.ShapeDtypeStruct((N_LAYER, N_CHIP) + a.shape[1:], a.dtype) for a in arrs],
                     [4 * n, 4 * n, 3 * n, 3 * n])


def _pair_exchange(arrs):
    n = len(arrs)

    def body(*refs):
        ins, outs = refs[:n], refs[n:2 * n]
        send_sems, recv_sems = refs[2 * n:]
        x, y, c = _position()
        cps = [_remote(ins[a].at[1 - c], outs[a], send_sems.at[a], recv_sems.at[a], (x, y, 1 - c)) for a in range(n)]
        for cp in cps:
            cp.start()
        for cp in cps:
            cp.wait_recv()
        for cp in cps:
            cp.wait_send()

    return _hbm_call("pair_exchange", body, arrs, [jax.ShapeDtypeStruct(a.shape[1:], a.dtype) for a in arrs], [n, n])


def _pair_share(arrs):
    n = len(arrs)

    def body(*refs):
        ins, outs = refs[:n], refs[n:2 * n]
        send_sems, recv_sems = refs[2 * n:]
        x, y, c = _position()
        cps = [_remote(ins[a].at[c], outs[a].at[c], send_sems.at[a], recv_sems.at[a], (x, y, 1 - c)) for a in range(n)]
        for cp in cps:
            cp.start()
        for a in range(n):
            _remote(ins[a].at[c], outs[a].at[1 - c], send_sems.at[a], recv_sems.at[a], (x, y, 1 - c)).wait_recv()
        for cp in cps:
            cp.wait_send()

    return _hbm_call("pair_share", body, arrs, [jax.ShapeDtypeStruct(a.shape, a.dtype) for a in arrs], [n, n],
                     {a: a for a in range(n)})


def _pair_forward(arrs):
    n = len(arrs)

    def body(*refs):
        ins, outs = refs[:n], refs[n:2 * n]
        send_sems, recv_sems = refs[2 * n:]
        x, y, c = _position()
        sends = []
        for a in range(n):
            for j, (px, py) in enumerate(_other_chips(x, y)):
                sends.append(_remote(ins[a].at[c, 2 * px + py], outs[a].at[c, 2 * px + py], send_sems.at[3 * a + j],
                                     recv_sems.at[3 * a + j], (x, y, 1 - c)))
                sends[-1].start()
        for a in range(n):
            for j, (px, py) in enumerate(_other_chips(x, y)):
                slab = outs[a].at[1 - c, 2 * px + py]
                _remote(slab, slab, send_sems.at[3 * a + j], recv_sems.at[3 * a + j], (x, y, 1 - c)).wait_recv()
        for cp in sends:
            cp.wait_send()

    return _hbm_call("pair_forward", body, arrs, [jax.ShapeDtypeStruct(a.shape, a.dtype) for a in arrs],
                     [3 * n, 3 * n], {a: a for a in range(n)})


_HBM = pl.BlockSpec(memory_space=pltpu.HBM)
_SEM = pl.BlockSpec(memory_space=pltpu.SEMAPHORE)
_EFFECT = pltpu.SideEffectType.DATAFLOW_SIDE_EFFECTING


def _chip_copies(kind, srcs, lands, send_sems, recv_sems):
    x, y, c = _position()
    me = 2 * x + y
    sends, recvs = [], []
    for a in range(len(srcs)):
        for j, (px, py) in enumerate(_other_chips(x, y)):
            s, sems, dev = 2 * px + py, (send_sems.at[3 * a + j], recv_sems.at[3 * a + j]), (px, py, c)
            if kind == "gather":
                sends.append(_remote(srcs[a].at[c], lands[a].at[c, me], *sems, dev))
                recvs.append(_remote(srcs[a].at[c], lands[a].at[c, s], *sems, dev))
            else:
                sends.append(_remote(srcs[a].at[s], lands[a].at[me], *sems, dev))
                recvs.append(_remote(srcs[a].at[me], lands[a].at[s], *sems, dev))
    return sends, recvs


def _split_start(name, kind, srcs, land_shapes, after):
    n = len(srcs)

    def body(*refs):
        sends, _ = _chip_copies(kind, refs[:n], refs[n:2 * n], refs[2 * n + 1], refs[2 * n + 2])
        for cp in sends:
            cp.start()
        refs[-1][...] = jnp.zeros_like(refs[-1])

    hbm = lambda a: pltpu.with_memory_space_constraint(a, pltpu.HBM)
    lands = [lax.empty(s.shape, s.dtype) for s in land_shapes]
    outs = pl.pallas_call(
        body, name=name, in_specs=[_HBM] * (2 * n) + [_ANY],
        out_specs=[_SEM, _SEM] + [_HBM] * (2 * n) + [pl.BlockSpec(memory_space=pltpu.VMEM)],
        out_shape=[pltpu.SemaphoreType.DMA((3 * n,)), pltpu.SemaphoreType.DMA((3 * n,))]
        + [pltpu.HBM(a.shape, a.dtype) for a in list(srcs) + lands] + [jax.ShapeDtypeStruct((8, LANES), F32)],
        input_output_aliases={i: 2 + i for i in range(2 * n)},
        compiler_params=pltpu.CompilerParams(has_side_effects=_EFFECT),
    )(*[hbm(a) for a in srcs], *[hbm(a) for a in lands], after)
    return outs[0], outs[1], list(outs[2:2 + 2 * n]), outs[-1]


def _split_wait(name, kind, started, after):
    send_sems, recv_sems, thru, _ = started
    n = len(thru) // 2

    def body(*refs):
        sends, recvs = _chip_copies(kind, refs[:n], refs[n:2 * n], refs[2 * n], refs[2 * n + 1])
        for cp in sends:
            cp.wait_send()
        for cp in recvs:
            cp.wait_recv()

    outs = pl.pallas_call(
        body, name=name, in_specs=[_HBM] * (2 * n) + [_SEM, _SEM, _ANY], out_specs=[_HBM] * (2 * n),
        out_shape=[pltpu.HBM(a.shape, a.dtype) for a in thru], input_output_aliases={i: i for i in range(2 * n)},
        compiler_params=pltpu.CompilerParams(has_side_effects=_EFFECT),
    )(*thru, send_sems, recv_sems, after)
    return list(outs[:n]), list(outs[n:])


def _allreduce_small(g):
    def body(g_ref, o_ref, pair_buf, chip_buf, send_sems, recv_sems):
        x, y, c = _position()
        me = 2 * x + y
        sib = (x, y, 1 - c)
        to_sib = _remote(g_ref.at[1 - c], pair_buf, send_sems.at[0], recv_sems.at[0], sib)
        to_sib.start()
        to_sib.wait_recv()
        chip_buf[me] = g_ref[c] + pair_buf[...]
        sends = [to_sib]
        chips = _other_chips(x, y)
        for j, (px, py) in enumerate(chips):
            sends.append(_remote(chip_buf.at[me], chip_buf.at[me], send_sems.at[1 + j], recv_sems.at[1 + j], (px, py, c)))
            sends[-1].start()
        for j, (px, py) in enumerate(chips):
            _remote(chip_buf.at[me], chip_buf.at[2 * px + py], send_sems.at[1 + j], recv_sems.at[1 + j],
                    (px, py, c)).wait_recv()
        o_ref[c] = ((chip_buf[0] + chip_buf[1]) + chip_buf[2]) + chip_buf[3]
        sends.append(_remote(o_ref.at[c], o_ref.at[c], send_sems.at[4], recv_sems.at[4], sib))
        sends[-1].start()
        _remote(o_ref.at[c], o_ref.at[1 - c], send_sems.at[4], recv_sems.at[4], sib).wait_recv()
        for cp in sends:
            cp.wait_send()

    vmem = pl.BlockSpec(memory_space=pltpu.VMEM)
    return pl.pallas_call(
        body, name="allreduce_small", in_specs=[vmem], out_specs=vmem, out_shape=jax.ShapeDtypeStruct(g.shape, F32),
        scratch_shapes=[pltpu.VMEM(g.shape[1:], F32), pltpu.VMEM((N_CHIP,) + g.shape[1:], F32),
                        pltpu.SemaphoreType.DMA((5,)), pltpu.SemaphoreType.DMA((5,))],
        compiler_params=_params(),
    )(g)


EW_ROWS = 512


def _ew(name, fn, ins, n_out, out_dtype=F32, out_slot=None, into=None):
    def dims(a):
        return a[0].shape[1:] if isinstance(a, tuple) else a.shape

    R, w = dims(ins[0])
    tr = EW_ROWS if R % EW_ROWS == 0 else R
    n_into = len(into) if into else 0

    def body(c_ref, *refs):
        outs = fn(*[r[...] for r in refs[:len(ins)]])
        for r, v in zip(refs[len(ins) + n_into:], outs):
            r[...] = v.astype(r.dtype)

    def lead_spec(l):
        if l == "c":
            return pl.BlockSpec((None, tr, w), lambda i, c_ref: (c_ref[0], i, 0))
        return pl.BlockSpec((None, tr, w), lambda i, c_ref, s=l: (s, i, 0))

    plain = pl.BlockSpec((tr, w), lambda i, c_ref: (i, 0))
    in_specs = [lead_spec(a[1]) if isinstance(a, tuple) else plain for a in ins] + [_ANY] * n_into
    out_spec = plain if out_slot is None else lead_spec(out_slot)
    out_shape = jax.ShapeDtypeStruct((R, w) if out_slot is None else (2, R, w), out_dtype)
    return pl.pallas_call(
        body, name=name,
        grid_spec=pltpu.PrefetchScalarGridSpec(num_scalar_prefetch=1, grid=(R // tr,), in_specs=in_specs,
                                               out_specs=[out_spec] * n_out),
        out_shape=[out_shape] * n_out, input_output_aliases={1 + len(ins) + j: j for j in range(n_into)},
        compiler_params=_params(dimension_semantics=("arbitrary",)),
    )(lax.axis_index("c").astype(jnp.int32).reshape(1), *[a[0] if isinstance(a, tuple) else a for a in ins],
      *(into or []))


def _adamw_fn(w, g, m, v):
    m = ADAM_B1 * m + (1.0 - ADAM_B1) * g
    v = ADAM_B2 * v + (1.0 - ADAM_B2) * (g * g)
    m_hat = m / (1.0 - ADAM_B1 ** ADAM_STEP)
    v_hat = v / (1.0 - ADAM_B2 ** ADAM_STEP)
    delta = -ADAM_LR * (m_hat / (jnp.sqrt(v_hat) + ADAM_EPS) + ADAM_WD * w)
    return delta, m, v


def _adamw_layer(name, l, w, g, m, v, into):
    k = w.shape[-1]
    three = lambda a: (a.reshape(N_LAYER, -1, k), l)
    fn = lambda w_, g_, m_, v_: _adamw_fn(w_, g_, m_, v_) + (g_,)
    outs = _ew(name, fn, [three(w), g.reshape(-1, k), three(m), three(v)], 4, out_slot=l,
               into=None if into is None else [a.reshape(N_LAYER, -1, k) for a in into])
    return [o.reshape(w.shape) for o in outs]


def _adamw_rows(name, l, w, g, m, v, into):
    _, R, k = w.shape
    n_into = len(into) if into else 0

    def body(*refs):
        w_ref, g_ref, m_ref, v_ref = refs[:4]
        d_out, m_out, v_out, g_out = refs[4 + n_into:]
        g_blk = g_ref[...]
        d_out[...], m_out[...], v_out[...] = _adamw_fn(w_ref[...], g_blk, m_ref[...], v_ref[...])
        g_out[...] = g_blk

    spec = pl.BlockSpec((None, EW_ROWS, k), lambda i: (l, i, 0))
    return pl.pallas_call(
        body, name=name, grid=(-(-R // EW_ROWS),),
        in_specs=[spec, pl.BlockSpec((EW_ROWS, k), lambda i: (i, 0)), spec, spec] + [_ANY] * n_into,
        out_specs=[spec] * 4, out_shape=[jax.ShapeDtypeStruct((N_LAYER, R, k), F32)] * 4,
        input_output_aliases={4 + j: j for j in range(n_into)},
        compiler_params=_params(dimension_semantics=("arbitrary",)),
    )(w, g, m, v, *(into or []))


_SMALL = [("norm_pre", (2, 1024)), ("norm_post", (2, 1024)), ("norm_mem", (2, 1024)), ("a_log", (2, 4)),
          ("dt_bias", (2, 4)), ("dn_norm", (2, 128)), ("gm_norm", (2, 512)), ("spatial_w", (2, 4, 128, 128)),
          ("spatial_b", (2, 4, 128)), ("sinks", (2, 8)), ("loss", (2, 1))]
_SMALL_ROWS = 208
_BIG = ["w_in", "conv_w", "w_mem_kv", "w_up", "w_out"]
_NAMES = ["norm_pre", "norm_post", "norm_mem", "w_in", "conv_w", "a_log", "dt_bias", "dn_norm", "gm_norm",
          "spatial_w", "spatial_b", "sinks", "w_mem_kv", "w_up", "w_out"]


def _size(shape):
    n = 1
    for s in shape:
        n *= s
    return n


def _pack_small(d):
    rows = []
    for n, shp in _SMALL:
        a = d[n].reshape(N_LAYER, -1)
        rows.append(a.reshape(-1, 1024) if a.shape[1] > 1024 else jnp.pad(a, ((0, 6), (0, 1024 - a.shape[1]))))
    assert sum(r.shape[0] for r in rows) == _SMALL_ROWS
    return jnp.concatenate(rows, axis=0)


def _unpack_small(p):
    out, off = {}, 0
    for n, shp in _SMALL:
        c = _size(shp) // N_LAYER
        k = 8 if c <= 1024 else _size(shp) // 1024
        out[n] = (p[off:off + N_LAYER, :c] if c <= 1024 else p[off:off + k]).reshape(shp)
        off += k
    return out


_HALF_SHAPE = {"w_in": (SHARD_PAD // 2, D_MODEL), "conv_w": (2, 3 * BRANCH_W // N_CHIP), "w_mem_kv": (128, D_MODEL),
               "w_up": (2, BRANCH_W, D_MODEL // N_CHIP), "w_out": (128, D_MODEL)}


def _chip_major(g):
    g = jnp.swapaxes(g, 0, 1)
    return g.reshape((N_CHIP, 2 * g.shape[2]) + g.shape[3:])


def _half_major(g):
    g = g.reshape((N_CHIP, 2, g.shape[1] // 2) + g.shape[2:])
    return jnp.swapaxes(g, 0, 1).astype(BF16)


N_EARLY = 2


def _early_views(l, g_in, g_conv, small):
    return _layer_weights(l, _w_pad_from_slabs(g_in),
                          _chip_major(g_conv).transpose(1, 0, 2).reshape(4, 3 * BRANCH_W), small)


def _late_views(g_kv, g_up, g_out):
    return dict(w_mem_kv=_chip_major(g_kv).reshape(D_MODEL, D_MODEL),
                w_up=_chip_major(g_up).transpose(1, 2, 0, 3).reshape(4, BRANCH_W, D_MODEL),
                w_out=_chip_major(g_out).reshape(D_MODEL, D_MODEL))


def _pair_sums(g):
    add2 = lambda a, b: [a.astype(F32) + b.astype(F32)]
    mine, theirs = _slabs_from_pad(g["w_pad"])
    pair = [_ew("pair_sum_w_in", add2, [mine.reshape(-1, D_MODEL), theirs.reshape(-1, D_MODEL)], 1, BF16)[0]
            .reshape(mine.shape)]
    rest = [_half_major(g["conv"].reshape(4, N_CHIP, 3 * BRANCH_W // N_CHIP).transpose(1, 0, 2)),
            _half_major(g["w_mem_kv"].reshape(N_CHIP, D_MODEL // N_CHIP, D_MODEL)),
            _half_major(g["w_up"].reshape(4, BRANCH_W, N_CHIP, D_MODEL // N_CHIP).transpose(2, 0, 1, 3)),
            _half_major(g["w_out"].reshape(N_CHIP, D_MODEL // N_CHIP, D_MODEL))]
    for n, b, p in zip(_BIG[1:], rest, _pair_exchange(rest)):
        k = b.shape[-1]
        pair.append(_ew("pair_sum_" + n, add2, [(b.reshape(2, -1, k), "c"), p.reshape(-1, k)], 1, BF16)[0]
                    .reshape(p.shape))
    return pair


def _chip_sums(landed, pair, me):
    add4 = lambda a, b, c_, d: [((a.astype(F32) + b.astype(F32)) + c_.astype(F32)) + d.astype(F32)]
    totals = []
    for n, r, q in zip(_BIG, landed, pair):
        r = _own_slot(r, lax.dynamic_index_in_dim(q, me, 0), me, 0)
        k = r.shape[-1]
        totals.append(_ew("chip_sum_" + n, add4, [(r.reshape(N_CHIP, -1, k), s) for s in range(N_CHIP)], 1,
                          out_slot="c")[0].reshape((2,) + r.shape[1:]))
    return totals


def _own_slot(buf, mine, me, axis):
    return lax.dynamic_update_index_in_dim(buf, mine.astype(buf.dtype), me, axis)


def kernel(x, mem, norm_pre, norm_post, norm_mem, w_in, conv_w, a_log, dt_bias, dn_norm, gm_norm, spatial_w, spatial_b, sinks, w_mem_kv, w_up, w_out, loss_target, m_norm_pre, m_norm_post, m_norm_mem, m_w_in, m_conv_w, m_a_log, m_dt_bias, m_dn_norm, m_gm_norm, m_spatial_w, m_spatial_b, m_sinks, m_w_mem_kv, m_w_up, m_w_out, v_norm_pre, v_norm_post, v_norm_mem, v_w_in, v_conv_w, v_a_log, v_dt_bias, v_dn_norm, v_gm_norm, v_spatial_w, v_spatial_b, v_sinks, v_w_mem_kv, v_w_up, v_w_out):
    w = dict(norm_pre=norm_pre, norm_post=norm_post, norm_mem=norm_mem, w_in=w_in, conv_w=conv_w, a_log=a_log,
             dt_bias=dt_bias, dn_norm=dn_norm, gm_norm=gm_norm, spatial_w=spatial_w, spatial_b=spatial_b, sinks=sinks,
             w_mem_kv=w_mem_kv, w_up=w_up, w_out=w_out)
    m = dict(norm_pre=m_norm_pre, norm_post=m_norm_post, norm_mem=m_norm_mem, w_in=m_w_in, conv_w=m_conv_w,
             a_log=m_a_log, dt_bias=m_dt_bias, dn_norm=m_dn_norm, gm_norm=m_gm_norm, spatial_w=m_spatial_w,
             spatial_b=m_spatial_b, sinks=m_sinks, w_mem_kv=m_w_mem_kv, w_up=m_w_up, w_out=m_w_out)
    v = dict(norm_pre=v_norm_pre, norm_post=v_norm_post, norm_mem=v_norm_mem, w_in=v_w_in, conv_w=v_conv_w,
             a_log=v_a_log, dt_bias=v_dt_bias, dn_norm=v_dn_norm, gm_norm=v_gm_norm, spatial_w=v_spatial_w,
             spatial_b=v_spatial_b, sinks=v_sinks, w_mem_kv=v_w_mem_kv, w_up=v_w_up, w_out=v_w_out)
    me = 2 * lax.axis_index("x") + lax.axis_index("y")

    tr = lambda a: a.transpose(0, 2, 1)
    w_t = tr(w_in)
    w_in_t = jnp.pad(w_t.astype(BF16), ((0, 0), (0, SHARD_PAD - SHARD_IN), (0, 0)))
    for d in (w, m, v):
        d["loss"] = jnp.zeros((N_LAYER, 1), F32)
    local = dict(w_in=w_in_t, conv_w=conv_w, w_mem_kv=w_mem_kv.astype(BF16), w_up=w_up.astype(BF16),
                 w_out=w_out.astype(BF16))
    halves = lambda l: [local[n][l].reshape((2,) + _HALF_SHAPE[n]) for n in _BIG]
    own = lambda gathered, mine: [_own_slot(g, h[:, None], me, 1) for g, h in zip(gathered, mine)]
    lands = [jax.ShapeDtypeStruct((2, N_CHIP) + _HALF_SHAPE[n], local[n].dtype) for n in _BIG]
    h0 = halves(0)
    g0 = own(_gather_weights(h0[:N_EARLY], [True, False]), h0[:N_EARLY])
    rest0 = _split_start("gather_l0_rest_start", "gather", h0[N_EARLY:], lands[N_EARLY:], g0[1])
    started = _split_start("gather_l1_start", "gather", halves(1), lands, rest0[3])

    xl, meml = x[0], mem[0]
    W0 = _early_views(0, g0[0], g0[1], w)
    W0["norm_pre"] = W0["norm_pre"] + started[3][0, 0]

    def late0(cols):
        mine, landed = _split_wait("gather_l0_rest_wait", "gather", rest0, cols)
        return _late_views(*own(_pair_forward(landed), mine))

    x1, sv0, W0 = _layer_fwd(xl, meml, W0, late0)
    mine1, landed1 = _split_wait("gather_l1_wait", "gather", started, x1)
    g1 = own(_pair_forward(landed1), mine1)
    W1 = dict(_early_views(1, g1[0], g1[1], w), **_late_views(*g1[N_EARLY:]))
    x2, sv1, _ = _layer_fwd(x1, meml, W1)
    dy, lrows = _rows_fwd("loss", _loss_fn, [Row(x2, D_MODEL, 0), Row(loss_target[0], D_MODEL, 0)], [],
                          [(D_MODEL, F32), (LANES, F32)], BIG_TR)
    loss_local = jnp.sum(lrows[:, 0])

    scattering = {}

    def start_scatter(l):
        def on_weight_grads(g):
            pair = _pair_sums(g)
            scattering[l] = _split_start("scatter_l%d_start" % l, "scatter", pair,
                                         [jax.ShapeDtypeStruct(p.shape, p.dtype) for p in pair], pair[1])
            return scattering[l][3]
        return on_weight_grads

    dx1, grads1 = _layer_bwd(dy, meml, W1, sv1, start_scatter(1))
    dx, grads0 = _layer_bwd(dx1, meml, W0, sv0, start_scatter(0))
    pair1, landed1 = _split_wait("scatter_l1_wait", "scatter", scattering[1], dx)
    after_start = scattering[0][3][0, 0]
    grads = [grads0, grads1]

    small_local = dict(
        norm_pre=jnp.stack([g["norm_pre"][0] for g in grads]), norm_post=jnp.stack([g["norm_post"][0] for g in grads]),
        norm_mem=jnp.stack([g["norm_mem"][0] for g in grads]), a_log=jnp.stack([g["a_vec"][0, 4:8] for g in grads]),
        dt_bias=jnp.stack([g["dt_vec"][0, 4:8] for g in grads]), dn_norm=jnp.stack([g["dn_norm"][0] for g in grads]),
        gm_norm=jnp.stack([g["gm_norm"][0] for g in grads]), spatial_w=jnp.stack([g["spatial_w"] for g in grads]),
        spatial_b=jnp.stack([g["spatial_b"][:, 0, :] for g in grads]),
        sinks=jnp.stack([g["sink_vec"][0, :8] for g in grads]),
        loss=jnp.stack([loss_local, jnp.zeros((), F32)]).reshape(N_LAYER, 1))
    packed = _pack_small(small_local) + after_start
    gsmall_packed = _allreduce_small(packed.reshape(2, -1, 1024)).reshape(-1, 1024)

    d_s, m_s, v_s = _ew("adamw_small", _adamw_fn, [_pack_small(w), gsmall_packed, _pack_small(m), _pack_small(v)], 3)
    gsmall, dsmall, msmall, vsmall = (_unpack_small(p) for p in (gsmall_packed, d_s, m_s, v_s))
    g_o, d_o, m_o, v_o = dict(gsmall), dict(dsmall), dict(msmall), dict(vsmall)
    loss = gsmall["loss"][0, 0]
    m_t, v_t = tr(m["w_in"]), tr(v["w_in"])

    def update(l, totals, into):
        outs = {}
        for n, t in zip(_BIG, totals):
            g_l = t.reshape(local[n].shape[1:])
            if n == "w_in":
                outs[n] = _adamw_rows("adamw_" + n, l, w_t, g_l, m_t, v_t, into and into[n])
            else:
                outs[n] = _adamw_layer("adamw_" + n, l, w[n], g_l, m[n], v[n], into and into[n])
        return outs

    landed1[1] = landed1[1] + after_start.astype(landed1[1].dtype)
    outs1 = update(1, _pair_share(_chip_sums(landed1, pair1, me)), None)
    pair0, landed0 = _split_wait("scatter_l0_wait", "scatter", scattering[0], outs1["w_in"][0])
    outs = update(0, _pair_share(_chip_sums(landed0, pair0, me)), outs1)
    for n in _BIG:
        d_o[n], m_o[n], v_o[n], g_o[n] = [tr(o) for o in outs[n]] if n == "w_in" else outs[n]
    return (loss, dx[None], *[g_o[n] for n in _NAMES], *[d_o[n] for n in _NAMES], *[m_o[n] for n in _NAMES],
            *[v_o[n] for n in _NAMES])
```

```python
import collections
import functools

import jax
import jax.numpy as jnp
from jax import lax
from jax.experimental import pallas as pl
from jax.experimental.pallas import tpu as pltpu

F32 = jnp.float32
BF16 = jnp.bfloat16

D_MODEL = 1024
BRANCH_W = 512
MEM_LEN = 256
N_LAYER = 2
N_CHIP = 4
EPS = 1e-6
NEG_INF = -1e30
DN_CHUNK = 64
SW_HD = 64
LANES = 128
VMEM_LIMIT = 48 * 1024 * 1024

ADAM_LR, ADAM_B1, ADAM_B2, ADAM_EPS, ADAM_WD, ADAM_STEP = 0.001, 0.9, 0.999, 1e-08, 0.01, 10

N_PAD = 10240
O_GATE = 0
O_AQ, O_AK, O_AV, O_AZ = 4096, 4608, 5120, 5632
O_BUV, O_BZ = 6144, 7168
O_CKV, O_BA = 7680, 7936
O_CQ, O_CZ = 8192, 8704
O_MQ, O_MZ = 9216, 9728
O_MISC, W_MISC = O_CKV, 512
_PAD_SEGS = [(5896, 4096), (0, 512), (512, 512), (1024, 512), (1536, 512), (2056, 1024), (3080, 512),
             (4104, 128), (4232, 128), (2048, 8), (None, 120), (None, 128),
             (3592, 512), (4360, 512), (4872, 512), (5384, 512)]
D_IN = 9992
SHARD_IN = D_IN // N_CHIP


SHARD_PAD = 2560


def _pad_parts():
    parts, off = [], 0
    for s, n in _PAD_SEGS:
        a = s
        while s is not None and a < s + n:
            chip = a // SHARD_IN
            b = min(s + n, (chip + 1) * SHARD_IN)
            parts.append((chip, a - chip * SHARD_IN, off + a - s, b - a))
            a = b
        off += n
    return parts


PERM_ROWS = 1024
PERM_SLACK = 32


def _permute_rows(name, src, parts, n_out, out_dtype, pair_split=False, exchange=()):
    B, Z = PERM_ROWS, PERM_ROWS + PERM_SLACK
    w = src.shape[1]
    plans = []
    for blk in range(n_out // B):
        o, runs = blk * B, []
        for s, d, n in parts:
            lo, hi = max(d, o), min(d + n, o + B)
            if lo < hi:
                s0 = s + lo - d
                wa = s0 // 16 * 16
                wb = min(-(-(s0 + hi - lo) // 16) * 16, src.shape[0])
                runs.append((wa, wb - wa, s0 - (lo - o) - wa, lo - o, hi - o))
        plans.append(runs)
    max_runs = max(len(r) for r in plans)
    nblk = len(plans)

    per_half = nblk // 2

    n_ex = len(exchange)

    def body(*refs):
        src_ref, out_ref = refs[0], refs[1 + n_ex]
        if pair_split:
            theirs_ref, ex_out = refs[2 + n_ex], refs[3 + n_ex:3 + 2 * n_ex]
            inbuf, obuf, insem, outsem, to_sib_sem, from_sib_sem, ex_send, ex_recv = refs[3 + 2 * n_ex:]
            x, y, c = _position()
            sibling = (x, y, 1 - c)
            ex_copies = [_remote(refs[1 + a].at[1 - c], ex_out[a], ex_send.at[a], ex_recv.at[a], sibling)
                         for a in range(n_ex)]
            for cp in ex_copies:
                cp.start()
        else:
            inbuf, obuf, insem, outsem = refs[2:]

        def in_copies(blk):
            return [pltpu.make_async_copy(src_ref.at[pl.ds(wa, ws)], inbuf.at[blk % 2, r, pl.ds(0, ws)],
                                          insem.at[blk % 2, r]) for r, (wa, ws, _, _, _) in enumerate(plans[blk])]

        class out_copy:
            def __init__(self, blk):
                self.blk, self.rows = blk, pl.ds((blk % per_half if pair_split else blk) * B, B)
                self.local = pltpu.make_async_copy(obuf.at[blk % 2], out_ref.at[self.rows], outsem.at[blk % 2])

            def _both(self, local_op, remote_op):
                if not pair_split:
                    return local_op(self.local)
                mine = c == self.blk // per_half
                pl.when(mine)(lambda: local_op(self.local))
                pl.when(jnp.logical_not(mine))(lambda: remote_op(_remote(
                    obuf.at[self.blk % 2], theirs_ref.at[self.rows], to_sib_sem.at[self.blk % 2],
                    from_sib_sem.at[self.blk % per_half], sibling)))

            def start(self):
                self._both(lambda cp: cp.start(), lambda cp: cp.start())

            def wait(self):
                self._both(lambda cp: cp.wait(), lambda cp: cp.wait_send())

        for cp in in_copies(0):
            cp.start()
        rid = _iota((B, 1), 0)
        for blk in range(nblk):
            if blk + 1 < nblk:
                for cp in in_copies(blk + 1):
                    cp.start()
            for cp in in_copies(blk):
                cp.wait()
            val = jnp.zeros((B, w), F32)
            for r, (wa, ws, t, l0, l1) in enumerate(plans[blk]):
                win = jnp.concatenate([inbuf[blk % 2, r, pl.ds(0, ws)].astype(F32), jnp.zeros((Z - ws, w), F32)], axis=0)
                moved = pltpu.roll(win, (-t) % Z, 0)[:B]
                val = jnp.where((rid >= l0) & (rid < l1), moved, val)
            if blk >= 2:
                out_copy(blk - 2).wait()
            obuf[blk % 2] = val.astype(out_dtype)
            out_copy(blk).start()
        for blk in range(max(nblk - 2, 0), nblk):
            out_copy(blk).wait()
        if pair_split:
            for i in range(per_half):
                rows = theirs_ref.at[pl.ds(i * B, B)]
                _remote(rows, rows, to_sib_sem.at[0], from_sib_sem.at[i], sibling).wait_recv()
            for cp in ex_copies:
                cp.wait_recv()
            for cp in ex_copies:
                cp.wait_send()

    scratch = [pltpu.VMEM((2, max_runs, Z, w), src.dtype), pltpu.VMEM((2, B, w), out_dtype),
               pltpu.SemaphoreType.DMA((2, max_runs)), pltpu.SemaphoreType.DMA((2,))]
    if not pair_split:
        return pl.pallas_call(
            body, name=name, in_specs=[_ANY], out_specs=_ANY, out_shape=jax.ShapeDtypeStruct((n_out, w), out_dtype),
            scratch_shapes=scratch, compiler_params=_params(),
        )(src)
    scratch += [pltpu.SemaphoreType.DMA((2,)), pltpu.SemaphoreType.DMA((per_half,)),
                pltpu.SemaphoreType.DMA((max(n_ex, 1),)), pltpu.SemaphoreType.DMA((max(n_ex, 1),))]
    out_shape = [jax.ShapeDtypeStruct((n_out // 2, w), out_dtype)] * 2
    out_shape += [jax.ShapeDtypeStruct(a.shape[1:], a.dtype) for a in exchange]
    res = pl.pallas_call(
        body, name=name, in_specs=[_ANY] * (1 + n_ex), out_specs=[_ANY] * (2 + n_ex), out_shape=out_shape,
        scratch_shapes=scratch, compiler_params=_params(),
    )(src, *exchange)
    return res[0], res[1], list(res[2:])


def _slab_parts():
    h, out = SHARD_PAD // 2, []
    for chip, s, d, n in _pad_parts():
        a = s
        while a < s + n:
            half = a // h
            b = min(s + n, (half + 1) * h)
            out.append(((half * N_CHIP + chip) * h + a - half * h, d + a - s, b - a))
            a = b
    return out


def _w_pad_from_slabs(slabs):
    return _permute_rows("w_pad_rows", slabs.reshape(-1, slabs.shape[-1]), _slab_parts(), N_PAD, BF16)


def _slabs_from_pad(dw, exchange=()):
    mine, theirs, exchanged = _permute_rows("w_pad_grad_rows", dw, [(d, s, n) for s, d, n in _slab_parts()],
                                            N_CHIP * SHARD_PAD, BF16, pair_split=True, exchange=exchange)
    shape = (N_CHIP, SHARD_PAD // 2, dw.shape[1])
    return mine.reshape(shape), theirs.reshape(shape), exchanged


def _dot(a, b, dims, prec):
    if prec == "bf16":
        return lax.dot_general(a.astype(BF16), b.astype(BF16), (dims, ((), ())), preferred_element_type=F32)
    return lax.dot_general(a, b, (dims, ((), ())), precision=lax.Precision.HIGHEST, preferred_element_type=F32)


_NN, _NT, _TN = ((1,), (0,)), ((1,), (1,)), ((0,), (0,))


def _make_mm(prec):
    @jax.custom_vjp
    def nn(a, b):
        return _dot(a, b, _NN, prec)

    @jax.custom_vjp
    def nt(a, b):
        return _dot(a, b, _NT, prec)

    @jax.custom_vjp
    def tn(a, b):
        return _dot(a, b, _TN, prec)

    nn.defvjp(lambda a, b: (nn(a, b), (a, b)), lambda r, g: (nt(g, r[1]), tn(r[0], g)))
    nt.defvjp(lambda a, b: (nt(a, b), (a, b)), lambda r, g: (nn(g, r[1]), tn(g, r[0])))
    tn.defvjp(lambda a, b: (tn(a, b), (a, b)), lambda r, g: (nt(r[1], g), nn(r[0], g)))
    return nn, nt, tn


_nn16, _nt16, _tn16 = _make_mm("bf16")
_nn32 = _make_mm("f32")[0]


def _make_slice(axis):
    @functools.partial(jax.custom_vjp, nondiff_argnums=(1, 2, 3))
    def sl(x, a, b, n):
        return x[a:b] if axis == 0 else x[:, a:b]

    def fwd(x, a, b, n):
        return sl(x, a, b, n), None

    def bwd(a, b, n, _, g):
        parts = []
        if a > 0:
            parts.append(jnp.zeros((a, g.shape[1]) if axis == 0 else (g.shape[0], a), g.dtype))
        parts.append(g)
        if n - b > 0:
            parts.append(jnp.zeros((n - b, g.shape[1]) if axis == 0 else (g.shape[0], n - b), g.dtype))
        return (jnp.concatenate(parts, axis=axis),)

    sl.defvjp(fwd, bwd)
    return sl


_sl0, _sl1 = _make_slice(0), _make_slice(1)


def _rowsl(x, a, b):
    return _sl0(x, a, b, x.shape[0])


def _cols(x, a, b):
    return _sl1(x, a, b, x.shape[1])


@functools.partial(jax.custom_vjp, nondiff_argnums=(1,))
def _rollr(x, s):
    return pltpu.roll(x, s, 0)


_rollr.defvjp(lambda x, s: (_rollr(x, s), None),
              lambda s, _, g: (pltpu.roll(g, g.shape[0] - s, 0),))


def _iota(shape, axis):
    return lax.broadcasted_iota(jnp.int32, shape, axis)


def _sigmoid(x):
    return lax.logistic(x)


def _silu(x):
    return x * _sigmoid(x)


def _gelu(x):
    return 0.5 * x * (1.0 + jnp.tanh(0.7978845608028654 * (x + 0.044715 * (x * x * x))))


def _softplus(x):
    return jnp.maximum(x, 0.0) + jnp.log(1.0 + jnp.exp(-jnp.abs(x)))


def _rms(x, g):
    return x * lax.rsqrt(jnp.mean(x * x, axis=-1, keepdims=True) + EPS) * g


def _lane_pick(x, lane):
    return jnp.sum(x * (_iota((1, x.shape[1]), 1) == lane).astype(F32), axis=1, keepdims=True)


Row = collections.namedtuple("Row", "arr w cb hb grad", defaults=(0, True))


def _full_spec(shape):
    return pl.BlockSpec(shape, lambda i, _n=len(shape): (0,) * _n)


def _load_params(refs):
    return [[p[g].astype(F32) for g in range(p.shape[0])] if len(p.shape) == 3 else p[...].astype(F32)
            for p in refs]


def _params(**kw):
    return pltpu.CompilerParams(vmem_limit_bytes=VMEM_LIMIT, **kw)


def _rows_fwd(name, fn, rows, params, outs, tr, carry=None):
    T = rows[0].arr.shape[0]
    n = T // tr
    halos = [r for r in rows if r.hb]
    nr, nh, npar, no = len(rows), len(halos), len(params), len(outs)

    def body(*refs):
        row_refs, halo_refs = refs[:nr], refs[nr:nr + nh]
        par_refs = refs[nr + nh:nr + nh + npar]
        out_refs = refs[nr + nh + npar:nr + nh + npar + no]
        rest = refs[nr + nh + npar + no:]
        first = pl.program_id(0) == 0
        cvals = None
        if carry is not None:
            csave_ref, carry_ref = rest

            @pl.when(first)
            def _():
                carry_ref[...] = jnp.zeros_like(carry_ref)

            cvals = [carry_ref[g] for g in range(carry[0])]
            for g in range(carry[0]):
                csave_ref[0, g] = cvals[g]
        c_out, o = fn(first, cvals, [r[...].astype(F32) for r in row_refs],
                      [h[...].astype(F32) for h in halo_refs], _load_params(par_refs))
        for r, v in zip(out_refs, o):
            r[...] = v.astype(r.dtype)
        if carry is not None:
            for g in range(carry[0]):
                carry_ref[g] = c_out[g]

    in_specs = [pl.BlockSpec((tr, r.w), lambda i, c=r.cb: (i, c)) for r in rows]
    in_specs += [pl.BlockSpec((r.hb, r.w), lambda i, c=r.cb, q=tr // r.hb: (jnp.maximum(i * q - 1, 0), c))
                 for r in halos]
    in_specs += [_full_spec(p.shape) for p in params]
    out_shape = [jax.ShapeDtypeStruct((T, w), dt) for w, dt in outs]
    out_specs = [pl.BlockSpec((tr, w), lambda i: (i, 0)) for w, _ in outs]
    scratch = []
    if carry is not None:
        out_shape.append(jax.ShapeDtypeStruct((n,) + carry, F32))
        out_specs.append(pl.BlockSpec((1,) + carry, lambda i: (i, 0, 0, 0)))
        scratch.append(pltpu.VMEM(carry, F32))
    return pl.pallas_call(
        body, name=name, grid=(n,), in_specs=in_specs, out_specs=out_specs, out_shape=out_shape,
        scratch_shapes=scratch, compiler_params=_params(dimension_semantics=("arbitrary",)),
    )(*[r.arr for r in rows], *[r.arr for r in halos], *params)


def _rows_bwd(name, fn, rows, params, douts, tr, carry=None, csave=None, dcols=None):
    T = rows[0].arr.shape[0]
    n = T // tr
    halos = [r for r in rows if r.hb]
    grows = [r for r in rows if r.grad is True]
    crows = [r for r in rows if r.grad == "cols"]
    wcols = sum(r.w for r in crows)
    nr, nh, npar, nd, ng = len(rows), len(halos), len(params), len(douts), len(grows)
    nc = 0 if carry is None else 1
    ncol = 1 if crows else 0
    nalias = 1 if (crows and dcols is not None) else 0

    def body(*refs):
        row_refs, halo_refs = refs[:nr], refs[nr:nr + nh]
        par_refs = refs[nr + nh:nr + nh + npar]
        k = nr + nh + npar
        csave_ref = refs[k] if nc else None
        dout_refs = refs[k + nc:k + nc + nd]
        k = k + nc + nd + nalias
        drow_refs = refs[k:k + ng]
        dcols_ref = refs[k + ng] if ncol else None
        dpar_refs = refs[k + ng + ncol:k + ng + ncol + npar]
        k = k + ng + ncol + npar
        dcarry_ref = refs[k] if nc else None
        hgrad_refs = refs[k + nc:]
        i = pl.program_id(0)
        first_tile = i == n - 1

        @pl.when(i == 0)
        def _():
            for r in dpar_refs:
                r[...] = jnp.zeros_like(r)
            for r in hgrad_refs:
                r[...] = jnp.zeros_like(r)
            if nc:
                dcarry_ref[...] = jnp.zeros_like(dcarry_ref)

        rv = [r[...].astype(F32) for r in row_refs]
        hv = [h[...].astype(F32) for h in halo_refs]
        pv = _load_params(par_refs)
        dov = [d[...].astype(F32) for d in dout_refs]
        if nc:
            cv = [csave_ref[0, g] for g in range(carry[0])]
            _, vjp = jax.vjp(lambda c, r, h, p: fn(first_tile, c, r, h, p), cv, rv, hv, pv)
            dc, dr, dh, dp = vjp(([dcarry_ref[g] for g in range(carry[0])], dov))
            for g in range(carry[0]):
                dcarry_ref[g] = dc[g]
        else:
            _, vjp = jax.vjp(lambda r, h, p: fn(first_tile, None, r, h, p)[1], rv, hv, pv)
            dr, dh, dp = vjp(dov)
        gi = hi = 0
        pieces = []
        for kk, r in enumerate(rows):
            d = dr[kk]
            if r.hb:
                carried = hgrad_refs[hi][...]
                d = d + (carried if tr == r.hb else
                         jnp.concatenate([jnp.zeros((tr - r.hb, r.w), F32), carried], axis=0))
                hgrad_refs[hi][...] = dh[hi]
                hi += 1
            if r.grad is True:
                drow_refs[gi][...] = d.astype(drow_refs[gi].dtype)
                gi += 1
            elif r.grad == "cols":
                pieces.append(d.astype(BF16))
        if ncol:
            dcols_ref[...] = pieces[0] if len(pieces) == 1 else jnp.concatenate(pieces, axis=1)
        for r, d in zip(dpar_refs, dp):
            if len(r.shape) == 3:
                for g in range(r.shape[0]):
                    r[g] += d[g]
            else:
                r[...] += d

    rev = lambda i: n - 1 - i
    in_specs = [pl.BlockSpec((tr, r.w), lambda i, c=r.cb: (rev(i), c)) for r in rows]
    in_specs += [pl.BlockSpec((r.hb, r.w), lambda i, c=r.cb, q=tr // r.hb: (jnp.maximum(rev(i) * q - 1, 0), c))
                 for r in halos]
    in_specs += [_full_spec(p.shape) for p in params]
    args = [r.arr for r in rows] + [r.arr for r in halos] + list(params)
    scratch = []
    if nc:
        in_specs.append(pl.BlockSpec((1,) + carry, lambda i: (rev(i), 0, 0, 0)))
        args.append(csave)
        scratch.append(pltpu.VMEM(carry, F32))
    douts = [d if isinstance(d, Row) else Row(d, d.shape[1], 0) for d in douts]
    in_specs += [pl.BlockSpec((tr, d.w), lambda i, c=d.cb: (rev(i), c)) for d in douts]
    args += [d.arr for d in douts]
    aliases = {}
    if nalias:
        aliases = {len(args): ng}
        in_specs.append(pl.BlockSpec(memory_space=pl.ANY))
        args.append(dcols)
    scratch += [pltpu.VMEM((r.hb, r.w), F32) for r in halos]
    out_shape = [jax.ShapeDtypeStruct((T, r.w), F32) for r in grows]
    out_specs = [pl.BlockSpec((tr, r.w), lambda i: (rev(i), 0)) for r in grows]
    if ncol:
        off = crows[0].cb * crows[0].w
        assert off % wcols == 0 and all(a.cb * a.w + a.w == b.cb * b.w for a, b in zip(crows, crows[1:]))
        out_shape.append(jax.ShapeDtypeStruct((T, N_PAD), BF16))
        out_specs.append(pl.BlockSpec((tr, wcols), lambda i, c=off // wcols: (rev(i), c)))
    out_shape += [jax.ShapeDtypeStruct(p.shape, F32) for p in params]
    out_specs += [_full_spec(p.shape) for p in params]
    res = pl.pallas_call(
        body, name=name, grid=(n,), in_specs=in_specs, out_specs=out_specs, out_shape=out_shape,
        scratch_shapes=scratch, input_output_aliases=aliases,
        compiler_params=_params(dimension_semantics=("arbitrary",)),
    )(*args)
    return list(res[:ng]), list(res[ng + ncol:]), (res[ng] if ncol else dcols)


def _fill_misc(dcols, dkv, dba, tr):
    T = dkv.shape[0]

    def body(kv_ref, ba_ref, _, o_ref):
        o_ref[...] = jnp.concatenate([kv_ref[...], ba_ref[...]], axis=1).astype(BF16)

    return pl.pallas_call(
        body, name="misc_bwd", grid=(T // tr,),
        in_specs=[pl.BlockSpec((tr, 256), lambda i: (i, 0)), pl.BlockSpec((tr, 256), lambda i: (i, 0)),
                  pl.BlockSpec(memory_space=pl.ANY)],
        out_specs=pl.BlockSpec((tr, W_MISC), lambda i: (i, O_MISC // W_MISC)),
        out_shape=jax.ShapeDtypeStruct((T, N_PAD), BF16), input_output_aliases={2: 0},
        compiler_params=_params(dimension_semantics=("arbitrary",)),
    )(dkv, dba, dcols)


def _up_bwd(ys, cols, dm, w_up):
    T = dm.shape[0]
    tr = min(UPB_TR, T)

    def body(y_ref, gl_ref, dm_ref, w_ref, dy_ref, dgl_ref, dw_ref):
        @pl.when(pl.program_id(1) == 0)
        def _():
            dw_ref[...] = jnp.zeros_like(dw_ref)

        _, vjp = jax.vjp(lambda y, gl, w: _sigmoid(gl) * _nn16(y, w),
                         y_ref[...].astype(F32), gl_ref[...].astype(F32), w_ref[...].astype(F32))
        dy, dgl, dw = vjp(dm_ref[...])
        dy_ref[...] = dy
        dgl_ref[...] = dgl.astype(BF16)
        dw_ref[...] += dw

    branch_rows = lambda w: pl.BlockSpec((tr, w), lambda n, i: (i, n))
    weight = pl.BlockSpec((None, BRANCH_W, D_MODEL), lambda n, i: (n, 0, 0))
    return pl.pallas_call(
        body, name="up_bwd", grid=(4, T // tr),
        in_specs=[branch_rows(BRANCH_W), branch_rows(D_MODEL), pl.BlockSpec((tr, D_MODEL), lambda n, i: (i, 0)), weight],
        out_specs=[branch_rows(BRANCH_W), branch_rows(D_MODEL), weight],
        out_shape=[jax.ShapeDtypeStruct((T, 4 * BRANCH_W), F32), jax.ShapeDtypeStruct((T, N_PAD), BF16),
                   jax.ShapeDtypeStruct(w_up.shape, F32)],
        compiler_params=_params(dimension_semantics=("arbitrary", "arbitrary")),
    )(ys, cols, dm, w_up)


def _matmul(name, a, b, kind, out_dtype, tm, tn, tk, after=None):
    if kind == "tn":
        (K, M), N = a.shape, b.shape[1]
    else:
        (M, K), N = a.shape, (b.shape[0] if kind == "nt" else b.shape[1])
    tm, tn, tk = min(tm, M), min(tn, N), min(tk, K)
    nk = K // tk
    dims = {"nn": _NN, "nt": _NT, "tn": _TN}[kind]

    n_after = 0 if after is None else 1

    def body(*refs):
        a_ref, b_ref, o_ref, acc = refs[0], refs[1], refs[2 + n_after], refs[3 + n_after:]
        part = lax.dot_general(a_ref[...], b_ref[...], (dims, ((), ())), preferred_element_type=F32)
        if nk == 1:
            o_ref[...] = part.astype(o_ref.dtype)
            return
        acc_ref = acc[0] if acc else o_ref
        k = pl.program_id(2)

        @pl.when(k == 0)
        def _():
            acc_ref[...] = part

        @pl.when(k > 0)
        def _():
            acc_ref[...] += part

        if acc:
            @pl.when(k == nk - 1)
            def _():
                o_ref[...] = acc_ref[...].astype(o_ref.dtype)

    a_spec = pl.BlockSpec((tk, tm), lambda i, j, k: (k, i)) if kind == "tn" else pl.BlockSpec((tm, tk), lambda i, j, k: (i, k))
    b_spec = pl.BlockSpec((tn, tk), lambda i, j, k: (j, k)) if kind == "nt" else pl.BlockSpec((tk, tn), lambda i, j, k: (k, j))
    return pl.pallas_call(
        body, name=name, grid=(M // tm, N // tn, nk), in_specs=[a_spec, b_spec] + [_ANY] * n_after,
        out_specs=pl.BlockSpec((tm, tn), lambda i, j, k: (i, j)),
        out_shape=jax.ShapeDtypeStruct((M, N), out_dtype),
        scratch_shapes=[pltpu.VMEM((tm, tn), F32)] if nk > 1 and out_dtype != F32 else [],
        compiler_params=_params(dimension_semantics=("arbitrary", "arbitrary", "arbitrary")),
    )(a, b, *([] if after is None else [after]))


def _pre_fn(first, _, rows, halos, params):
    return None, [_rms(rows[0], params[0])]


def _pre_fn_res(first, _, rows, halos, params):
    return None, [_rms(rows[0], params[0]), rows[0]]


def _memkv_fn(first, _, rows, halos, params):
    g, w = params
    return None, [_nn16(_rms(rows[0], g), w)]


def _conv_silu(x, halo, w4, keep_halo):
    tr, hb = x.shape[0], halo.shape[0]
    xh = jnp.concatenate([halo * keep_halo, x], axis=0)
    acc = w4[3] * x
    for s in (1, 2, 3):
        acc = acc + w4[3 - s] * _rowsl(_rollr(xh, s), hb, hb + tr)
    return _silu(acc)


def _dn_fn(first, S, rows, halos, params):
    qp, kp, vp, z, ba = rows
    conv, a_vec, dt_vec, dnorm = params
    ba = _cols(ba, 0, LANES)
    tr = qp.shape[0]
    keep = jnp.where(first, 0.0, 1.0)
    q = _conv_silu(qp, halos[0], [conv[3 * j + 0] for j in range(4)], keep)
    k = _conv_silu(kp, halos[1], [conv[3 * j + 1] for j in range(4)], keep)
    v = _conv_silu(vp, halos[2], [conv[3 * j + 2] for j in range(4)], keep)
    qh, kh, vh = [], [], []
    for h in range(4):
        a, b = h * LANES, (h + 1) * LANES
        xq, xk = _cols(q, a, b), _cols(k, a, b)
        qh.append(xq * lax.rsqrt(jnp.sum(xq * xq, axis=1, keepdims=True) + EPS) * (LANES ** -0.5))
        kh.append(xk * lax.rsqrt(jnp.sum(xk * xk, axis=1, keepdims=True) + EPS))
        vh.append(_cols(v, a, b))
    beta_all = _sigmoid(ba)
    g_all = -jnp.exp(a_vec) * _softplus(ba + dt_vec)
    C = DN_CHUNK
    ii, jj = _iota((C, C), 0), _iota((C, C), 1)
    strict, incl = ii > jj, ii >= jj
    eye = (ii == jj).astype(F32)
    last_row = (_iota((C, 1), 0) == C - 1).astype(F32)
    n_chunk = tr // C
    pairs = [(c, h) for c in range(n_chunk) for h in range(4)]
    rows_of = lambda a, c: _rowsl(a, c * C, (c + 1) * C)
    gcs = [_nn32(incl.astype(F32), rows_of(g_all, c)) for c in range(n_chunk)]
    qc = {(c, h): rows_of(qh[h], c) for c, h in pairs}
    kc = {(c, h): rows_of(kh[h], c) for c, h in pairs}
    beta = {(c, h): _lane_pick(rows_of(beta_all, c), h) for c, h in pairs}
    gc = {(c, h): _lane_pick(gcs[c], 4 + h) for c, h in pairs}
    dec = {p: jnp.exp(jnp.where(incl, gc[p] - jnp.sum(eye * gc[p], axis=0, keepdims=True), 0.0)) for p in pairs}
    egc = {p: jnp.exp(gc[p]) for p in pairs}
    kb = {p: kc[p] * beta[p] for p in pairs}
    kq = {p: _nt16(jnp.concatenate([kb[p], qc[p]], axis=0), kc[p]) for p in pairs}
    P = {p: -jnp.where(strict, _rowsl(kq[p], 0, C) * dec[p], 0.0) for p in pairs}
    aqk = {p: jnp.where(incl, _rowsl(kq[p], C, 2 * C) * dec[p], 0.0) for p in pairs}
    tinv = {p: eye + P[p] for p in pairs}
    P = {p: _nn16(P[p], P[p]) for p in pairs}
    for j in range(5):
        if j < 4:
            pt = {p: _nn16(jnp.concatenate([P[p], tinv[p]], axis=0), P[p]) for p in pairs}
            tinv = {p: tinv[p] + _rowsl(pt[p], C, 2 * C) for p in pairs}
            P = {p: _rowsl(pt[p], 0, C) for p in pairs}
        else:
            tinv = {p: tinv[p] + _nn16(tinv[p], P[p]) for p in pairs}
    uw = {(c, h): _nn16(tinv[c, h], jnp.concatenate([rows_of(vh[h], c) * beta[c, h], kb[c, h] * egc[c, h]], axis=1))
          for c, h in pairs}
    S = list(S)
    ychunks = []
    for c in range(n_chunk):
        zc = rows_of(z, c)
        hs = range(4)
        ws = [_nn16(jnp.concatenate([_cols(uw[c, h], LANES, 2 * LANES), qc[c, h] * egc[c, h]], axis=0), S[h]) for h in hs]
        vnew = [_cols(uw[c, h], 0, LANES) - _rowsl(ws[h], 0, C) for h in hs]
        o = [_rowsl(ws[h], C, 2 * C) + _nn16(aqk[c, h], vnew[h]) for h in hs]
        glast = [jnp.sum(gc[c, h] * last_row, axis=0, keepdims=True) for h in hs]
        S = [S[h] * jnp.exp(glast[h]) + _tn16(kc[c, h] * jnp.exp(glast[h] - gc[c, h]), vnew[h]) for h in hs]
        ychunks.append(jnp.concatenate(
            [_rms(o[h], dnorm) * _silu(_cols(zc, h * LANES, (h + 1) * LANES)) for h in hs], axis=1))
    return S, [jnp.concatenate(ychunks, axis=0)]


def _gm_fn(first, _, rows, halos, params):
    uv, z = rows
    gnorm, ws, bs = params
    tr = uv.shape[0]
    guv = _gelu(uv)
    u = _cols(guv, 0, BRANCH_W)
    v = _rms(_cols(guv, BRANCH_W, 2 * BRANCH_W), gnorm)
    ii, jj = _iota((LANES, LANES), 0), _iota((LANES, LANES), 1)
    eye = (ii == jj).astype(F32)
    wsm = [jnp.where(ii >= jj, ws[g], 0.0) for g in range(4)]
    bcol = [jnp.sum(eye * bs[g], axis=1, keepdims=True) for g in range(4)]
    chunks = []
    for c in range(tr // LANES):
        vc = _rowsl(v, c * LANES, (c + 1) * LANES)
        chunks.append(jnp.concatenate(
            [_nn16(wsm[g], _cols(vc, g * LANES, (g + 1) * LANES)) + bcol[g] for g in range(4)], axis=1))
    return None, [u * jnp.concatenate(chunks, axis=0) * _silu(z)]


def _swa_fn(first, _, rows, halos, params):
    q, kvc, z = rows
    sink_vec = params[0]
    P = LANES
    n_blk = q.shape[0] // P
    r, cc = _iota((P, P), 0), _iota((P, P), 1)
    lane = _iota((1, P), 1)
    key = _iota((P, 2 * P), 1)
    dist = _iota((P, 2 * P), 0) + P - key
    in_window = (dist >= 0) & (dist < P)
    valid = [in_window & (key >= jnp.where(first, P, 0))] + [in_window] * (n_blk - 1)
    halves = [(lane < SW_HD).astype(F32), (lane >= SW_HD).astype(F32)]
    dup = [(r == kh * SW_HD + (cc & (SW_HD - 1))).astype(F32) for kh in range(2)]
    kv_blk = [halos[0]] + [_rowsl(kvc, b * P, (b + 1) * P) for b in range(n_blk)]
    pairs = [(b, kh) for b in range(n_blk) for kh in range(2)]
    kkvv = {}
    for b in range(n_blk):
        kv = jnp.concatenate([kv_blk[b], kv_blk[b + 1]], axis=0)
        k_v = jnp.concatenate([_cols(kv, 0, P), _cols(kv, P, 2 * P)], axis=0)
        for kh in range(2):
            kkvv[b, kh] = _nn16(k_v, dup[kh])
    scores = {}
    for b, kh in pairs:
        q_b = _rowsl(q, b * P, (b + 1) * P)
        stacked = jnp.concatenate([_cols(q_b, (2 * kh + g // 2) * P, (2 * kh + g // 2 + 1) * P) * halves[g % 2]
                                   for g in range(4)], axis=0)
        scores[b, kh] = _nt16(stacked, _rowsl(kkvv[b, kh], 0, 2 * P))
    probs = {}
    for b, kh in pairs:
        ps = []
        for g in range(4):
            s = jnp.where(valid[b], _rowsl(scores[b, kh], g * P, (g + 1) * P) * (SW_HD ** -0.5), NEG_INF)
            sink = _lane_pick(sink_vec, kh * 4 + g)
            m = lax.stop_gradient(jnp.maximum(jnp.max(s, axis=1, keepdims=True), sink))
            e = jnp.exp(s - m)
            ps.append(e / (jnp.sum(e, axis=1, keepdims=True) + jnp.exp(sink - m)))
        probs[b, kh] = jnp.concatenate(ps, axis=0)
    outs = {p: _nn16(probs[p], _rowsl(kkvv[p], 2 * P, 4 * P)) for p in pairs}
    tile = [jnp.concatenate([_rowsl(outs[b, j // 2], (2 * (j % 2)) * P, (2 * (j % 2) + 1) * P) * halves[0]
                             + _rowsl(outs[b, j // 2], (2 * (j % 2) + 1) * P, (2 * (j % 2) + 2) * P) * halves[1]
                             for j in range(4)], axis=1) for b in range(n_blk)]
    return None, [jnp.concatenate(tile, axis=0) * _silu(z)]


def _mem_fn(first, _, rows, halos, params):
    q, z = rows
    mkv = params[0]
    heads = [(h * LANES, (h + 1) * LANES) for h in range(4)]
    scores = [_nt16(_cols(q, a, b), _cols(mkv, a, b)) * (LANES ** -0.5) for a, b in heads]
    probs = []
    for s in scores:
        e = jnp.exp(s - lax.stop_gradient(jnp.max(s, axis=1, keepdims=True)))
        probs.append(e / jnp.sum(e, axis=1, keepdims=True))
    outs = [_nn16(p, _cols(mkv, BRANCH_W + a, BRANCH_W + b)) for p, (a, b) in zip(probs, heads)]
    return None, [jnp.concatenate(outs, axis=1) * _silu(z)]


def _up_fn(first, _, rows, halos, params):
    ys, gl, w_up = rows[:4], rows[4], params[0]
    merged = None
    for n in range(4):
        term = _sigmoid(_cols(gl, n * D_MODEL, (n + 1) * D_MODEL)) * _nn16(ys[n], w_up[n])
        merged = term if merged is None else merged + term
    return None, [merged]


def _out_fn(first, _, rows, halos, params):
    x, merged = rows
    w, g = params
    return None, [x + _rms(_nn16(merged, w), g)]


def _loss_fn(first, _, rows, halos, params):
    y, t = rows
    d = y - t
    lrow = 0.5 * jnp.mean(d * d, axis=1, keepdims=True)
    return None, [d * (1.0 / D_MODEL), jnp.broadcast_to(lrow, (y.shape[0], LANES))]


TR = 256
BIG_TR = 512
SWA_TR = 512
DN_TR = 256
UP_TR = 256
UPB_TR = 1024
CONV_HALO = 16
CARRY = (4, LANES, LANES)


def _branch_rows(cols, g):
    hb = CONV_HALO
    a = [Row(cols, 512, O_AQ // 512, hb, g), Row(cols, 512, O_AK // 512, hb, g), Row(cols, 512, O_AV // 512, hb, g),
         Row(cols, 512, O_AZ // 512, 0, g), Row(cols, 256, O_BA // 256)]
    b = [Row(cols, 1024, O_BUV // 1024, 0, g), Row(cols, 512, O_BZ // 512, 0, g)]
    c = [Row(cols, 512, O_CQ // 512, 0, g), Row(cols, 256, O_CKV // 256, LANES), Row(cols, 512, O_CZ // 512, 0, g)]
    m = [Row(cols, 512, O_MQ // 512, 0, g), Row(cols, 512, O_MZ // 512, 0, g)]
    return a, b, c, m


def _layer_fwd(x, mem, W, late_weights=None):
    h = _rows_fwd("prenorm_fwd", _pre_fn, [Row(x, D_MODEL, 0)], [W["norm_pre"]], [(D_MODEL, BF16)], BIG_TR)[0]
    cols = _matmul("in_proj_fwd", h, W["w_pad"], "nt", BF16, 2048, 2048, 1024)
    if late_weights is not None:
        W = dict(W, **late_weights(cols))
    mem_kv = _rows_fwd("memkv_fwd", _memkv_fn, [Row(mem, D_MODEL, 0)], [W["norm_mem"], W["w_mem_kv"]],
                       [(D_MODEL, F32)], MEM_LEN)[0]
    ra, rb, rc, rm = _branch_rows(cols, True)
    y_a, csave = _rows_fwd("dn_fwd", _dn_fn, ra, [W["conv"], W["a_vec"], W["dt_vec"], W["dn_norm"]],
                           [(BRANCH_W, BF16)], DN_TR, CARRY)
    y_b = _rows_fwd("gm_fwd", _gm_fn, rb, [W["gm_norm"], W["spatial_w"], W["spatial_b"]], [(BRANCH_W, BF16)], BIG_TR)[0]
    y_c = _rows_fwd("swa_fwd", _swa_fn, rc, [W["sink_vec"]], [(BRANCH_W, BF16)], SWA_TR)[0]
    y_m = _rows_fwd("mem_fwd", _mem_fn, rm, [mem_kv], [(BRANCH_W, BF16)], BIG_TR)[0]
    ys = [y_a, y_b, y_c, y_m]
    merged = _rows_fwd("up_fwd", _up_fn, [Row(y, BRANCH_W, 0) for y in ys] + [Row(cols, 4 * D_MODEL, 0)],
                       [W["w_up"]], [(D_MODEL, BF16)], UP_TR)[0]
    x_new = _rows_fwd("out_fwd", _out_fn, [Row(x, D_MODEL, 0), Row(merged, D_MODEL, 0)],
                      [W["w_out"], W["norm_post"]], [(D_MODEL, F32)], TR)[0]
    return x_new, dict(x=x, h=h, cols=cols, mem_kv=mem_kv, csave=csave, ys=ys, merged=merged), W


def _layer_bwd(dxn, mem, W, sv, on_weight_grads=None):
    x, cols = sv["x"], sv["cols"]
    (dx_res, dm), (dw_out, dnorm_post), _ = _rows_bwd(
        "out_bwd", _out_fn, [Row(x, D_MODEL, 0), Row(sv["merged"], D_MODEL, 0)], [W["w_out"], W["norm_post"]],
        [dxn], TR)
    dys, dcols, dw_up = _up_bwd(jnp.concatenate(sv["ys"], axis=1), cols, dm, W["w_up"])
    dys = [Row(dys, BRANCH_W, n) for n in range(4)]
    ra, rb, rc, rm = _branch_rows(cols, "cols")
    (dba,), (dconv, da_vec, ddt_vec, ddn_norm), dcols = _rows_bwd(
        "dn_bwd", _dn_fn, ra, [W["conv"], W["a_vec"], W["dt_vec"], W["dn_norm"]], [dys[0]], DN_TR, CARRY,
        sv["csave"], dcols=dcols)
    _, (dgm_norm, dws, dbs), dcols = _rows_bwd(
        "gm_bwd", _gm_fn, rb, [W["gm_norm"], W["spatial_w"], W["spatial_b"]], [dys[1]], BIG_TR, dcols=dcols)
    (dkv_c,), (dsink,), dcols = _rows_bwd("swa_bwd", _swa_fn, rc, [W["sink_vec"]], [dys[2]], SWA_TR, dcols=dcols)
    _, (dmem_kv,), dcols = _rows_bwd("mem_bwd", _mem_fn, rm, [sv["mem_kv"]], [dys[3]], BIG_TR, dcols=dcols)
    dcols = _fill_misc(dcols, dkv_c, dba, BIG_TR)
    _, (dnorm_mem, dw_mem_kv), _ = _rows_bwd("memkv_bwd", _memkv_fn, [Row(mem, D_MODEL, 0, 0, False)],
                                             [W["norm_mem"], W["w_mem_kv"]], [dmem_kv], MEM_LEN)
    dw_pad = _matmul("in_proj_dw", dcols, sv["h"], "tn", BF16, 1024, 1024, 2048)
    grads = dict(norm_post=dnorm_post, norm_mem=dnorm_mem, w_pad=dw_pad, conv=dconv,
                 a_vec=da_vec, dt_vec=ddt_vec, dn_norm=ddn_norm, gm_norm=dgm_norm, spatial_w=dws, spatial_b=dbs,
                 sink_vec=dsink, w_mem_kv=dw_mem_kv, w_up=dw_up, w_out=dw_out)
    started = None if on_weight_grads is None else on_weight_grads(grads)
    dh = _matmul("in_proj_dx", dcols, W["w_pad"], "nn", F32, 1024, 1024, 2048, after=started)
    (dx,), (grads["norm_pre"],), _ = _rows_bwd("prenorm_bwd", _pre_fn_res, [Row(x, D_MODEL, 0)], [W["norm_pre"]],
                                               [dh, dx_res], BIG_TR)
    return dx, grads


def _lane_vec(v, off):
    return jnp.zeros((1, LANES), F32).at[0, off:off + v.shape[0]].set(v)


def _layer_weights(l, w_pad, conv_w, small, **late):
    return dict(
        late, w_pad=w_pad, conv=conv_w.reshape(4, 3, BRANCH_W).reshape(12, 1, BRANCH_W),
        norm_pre=small["norm_pre"][l][None], norm_post=small["norm_post"][l][None],
        norm_mem=small["norm_mem"][l][None],
        a_vec=_lane_vec(small["a_log"][l], 4), dt_vec=_lane_vec(small["dt_bias"][l], 4),
        dn_norm=small["dn_norm"][l][None], gm_norm=small["gm_norm"][l][None],
        spatial_w=small["spatial_w"][l], spatial_b=small["spatial_b"][l][:, None, :],
        sink_vec=_lane_vec(small["sinks"][l], 0))


_MESH = pl.DeviceIdType.MESH
_ANY = pl.BlockSpec(memory_space=pl.ANY)


def _position():
    return lax.axis_index("x"), lax.axis_index("y"), lax.axis_index("c")


def _remote(src, dst, send_sem, recv_sem, dev):
    return pltpu.make_async_remote_copy(src_ref=src, dst_ref=dst, send_sem=send_sem, recv_sem=recv_sem,
                                        device_id=dev, device_id_type=_MESH)


def _hbm_call(name, body, arrs, out_shapes, sems, aliases=None):
    return pl.pallas_call(
        body, name=name, in_specs=[_ANY] * len(arrs), out_specs=[_ANY] * len(out_shapes), out_shape=out_shapes,
        scratch_shapes=[pltpu.SemaphoreType.DMA((k,)) for k in sems], input_output_aliases=aliases or {},
        compiler_params=pltpu.CompilerParams(has_side_effects=True),
    )(*arrs)


def _other_chips(x, y):
    return [(1 - x, y), (x, 1 - y), (1 - x, 1 - y)]


def _gather_weights(arrs, relayed):
    n = len(arrs)

    def body(*refs):
        ins, outs = refs[:n], refs[n:2 * n]
        ici_send, ici_recv, d2d_send, d2d_recv = refs[2 * n:]
        x, y, c = _position()
        me = 2 * x + y
        xn, yn, dg = _other_chips(x, y)
        chip = lambda p: 2 * p[0] + p[1]
        sends = []

        def go(cp):
            cp.start()
            sends.append(cp)

        def ici(a, j, src, dst, to):
            return _remote(src, dst, ici_send.at[4 * a + j], ici_recv.at[4 * a + j], (*to, c))

        for a in range(n):
            go(ici(a, 0, ins[a].at[c], outs[a].at[c, me], xn))
            go(ici(a, 1, ins[a].at[c], outs[a].at[c, me], yn))
            if not relayed[a]:
                go(ici(a, 2, ins[a].at[c], outs[a].at[c, me], dg))
        for a in range(n):
            h = arrs[a].shape[1] // 2
            from_x, from_y = outs[a].at[c, chip(xn)], outs[a].at[c, chip(yn)]
            ici(a, 0, ins[a].at[c], from_x, xn).wait_recv()
            if relayed[a]:
                go(ici(a, 2, from_x.at[pl.ds(0, h)], from_x.at[pl.ds(0, h)], yn))
            ici(a, 1, ins[a].at[c], from_y, yn).wait_recv()
            if relayed[a]:
                go(ici(a, 3, from_y.at[pl.ds(h, h)], from_y.at[pl.ds(h, h)], xn))
            for j, slab in enumerate((from_x, from_y)):
                go(_remote(slab, slab, d2d_send.at[3 * a + j], d2d_recv.at[3 * a + j], (x, y, 1 - c)))
        for a in range(n):
            h = arrs[a].shape[1] // 2
            from_d = outs[a].at[c, chip(dg)]
            if relayed[a]:
                ici(a, 2, from_d.at[pl.ds(0, h)], from_d.at[pl.ds(0, h)], yn).wait_recv()
                ici(a, 3, from_d.at[pl.ds(h, h)], from_d.at[pl.ds(h, h)], xn).wait_recv()
            else:
                ici(a, 2, ins[a].at[c], from_d, dg).wait_recv()
            go(_remote(from_d, from_d, d2d_send.at[3 * a + 2], d2d_recv.at[3 * a + 2], (x, y, 1 - c)))
        for a in range(n):
            for j, p in enumerate((xn, yn, dg)):
                slab = outs[a].at[1 - c, chip(p)]
                _remote(slab, slab, d2d_send.at[3 * a + j], d2d_recv.at[3 * a + j], (x, y, 1 - c)).wait_recv()
        for cp in sends:
            cp.wait_send()

    return _hbm_call("gather_weights", body, arrs,
                     [jax.ShapeDtypeStruct((N_LAYER, N_CHIP) + a.shape[1:], a.dtype) for a in arrs],
                     [4 * n, 4 * n, 3 * n, 3 * n])


def _pair_share(arrs):
    n = len(arrs)

    def body(*refs):
        ins, outs = refs[:n], refs[n:2 * n]
        send_sems, recv_sems = refs[2 * n:]
        x, y, c = _position()
        cps = [_remote(ins[a].at[c], outs[a].at[c], send_sems.at[a], recv_sems.at[a], (x, y, 1 - c)) for a in range(n)]
        for cp in cps:
            cp.start()
        for a in range(n):
            _remote(ins[a].at[c], outs[a].at[1 - c], send_sems.at[a], recv_sems.at[a], (x, y, 1 - c)).wait_recv()
        for cp in cps:
            cp.wait_send()

    return _hbm_call("pair_share", body, arrs, [jax.ShapeDtypeStruct(a.shape, a.dtype) for a in arrs], [n, n],
                     {a: a for a in range(n)})


def _pair_forward(arrs):
    n = len(arrs)

    def body(*refs):
        ins, outs = refs[:n], refs[n:2 * n]
        send_sems, recv_sems = refs[2 * n:]
        x, y, c = _position()
        sends = []
        for a in range(n):
            for j, (px, py) in enumerate(_other_chips(x, y)):
                sends.append(_remote(ins[a].at[c, 2 * px + py], outs[a].at[c, 2 * px + py], send_sems.at[3 * a + j],
                                     recv_sems.at[3 * a + j], (x, y, 1 - c)))
                sends[-1].start()
        for a in range(n):
            for j, (px, py) in enumerate(_other_chips(x, y)):
                slab = outs[a].at[1 - c, 2 * px + py]
                _remote(slab, slab, send_sems.at[3 * a + j], recv_sems.at[3 * a + j], (x, y, 1 - c)).wait_recv()
        for cp in sends:
            cp.wait_send()

    return _hbm_call("pair_forward", body, arrs, [jax.ShapeDtypeStruct(a.shape, a.dtype) for a in arrs],
                     [3 * n, 3 * n], {a: a for a in range(n)})


_HBM = pl.BlockSpec(memory_space=pltpu.HBM)
_SEM = pl.BlockSpec(memory_space=pltpu.SEMAPHORE)
_EFFECT = pltpu.SideEffectType.DATAFLOW_SIDE_EFFECTING


def _chip_copies(kind, srcs, lands, send_sems, recv_sems):
    x, y, c = _position()
    me = 2 * x + y
    sends, recvs = [], []
    for a in range(len(srcs)):
        for j, (px, py) in enumerate(_other_chips(x, y)):
            s, sems, dev = 2 * px + py, (send_sems.at[3 * a + j], recv_sems.at[3 * a + j]), (px, py, c)
            if kind == "gather":
                sends.append(_remote(srcs[a].at[c], lands[a].at[c, me], *sems, dev))
                recvs.append(_remote(srcs[a].at[c], lands[a].at[c, s], *sems, dev))
            else:
                sends.append(_remote(srcs[a].at[s], lands[a].at[me], *sems, dev))
                recvs.append(_remote(srcs[a].at[me], lands[a].at[s], *sems, dev))
    return sends, recvs


def _split_start(name, kind, srcs, land_shapes, after):
    n = len(srcs)

    def body(*refs):
        sends, _ = _chip_copies(kind, refs[:n], refs[n:2 * n], refs[2 * n + 1], refs[2 * n + 2])
        for cp in sends:
            cp.start()
        refs[-1][...] = jnp.zeros_like(refs[-1])

    hbm = lambda a: pltpu.with_memory_space_constraint(a, pltpu.HBM)
    lands = [lax.empty(s.shape, s.dtype) for s in land_shapes]
    outs = pl.pallas_call(
        body, name=name, in_specs=[_HBM] * (2 * n) + [_ANY],
        out_specs=[_SEM, _SEM] + [_HBM] * (2 * n) + [pl.BlockSpec(memory_space=pltpu.VMEM)],
        out_shape=[pltpu.SemaphoreType.DMA((3 * n,)), pltpu.SemaphoreType.DMA((3 * n,))]
        + [pltpu.HBM(a.shape, a.dtype) for a in list(srcs) + lands] + [jax.ShapeDtypeStruct((8, LANES), F32)],
        input_output_aliases={i: 2 + i for i in range(2 * n)},
        compiler_params=pltpu.CompilerParams(has_side_effects=_EFFECT),
    )(*[hbm(a) for a in srcs], *[hbm(a) for a in lands], after)
    return outs[0], outs[1], list(outs[2:2 + 2 * n]), outs[-1]


def _split_wait(name, kind, started, after):
    send_sems, recv_sems, thru, _ = started
    n = len(thru) // 2

    def body(*refs):
        sends, recvs = _chip_copies(kind, refs[:n], refs[n:2 * n], refs[2 * n], refs[2 * n + 1])
        for cp in sends:
            cp.wait_send()
        for cp in recvs:
            cp.wait_recv()

    outs = pl.pallas_call(
        body, name=name, in_specs=[_HBM] * (2 * n) + [_SEM, _SEM, _ANY], out_specs=[_HBM] * (2 * n),
        out_shape=[pltpu.HBM(a.shape, a.dtype) for a in thru], input_output_aliases={i: i for i in range(2 * n)},
        compiler_params=pltpu.CompilerParams(has_side_effects=_EFFECT),
    )(*thru, send_sems, recv_sems, after)
    return list(outs[:n]), list(outs[n:])


def _allreduce_small(g):
    def body(g_ref, o_ref, pair_buf, chip_buf, send_sems, recv_sems):
        x, y, c = _position()
        me = 2 * x + y
        sib = (x, y, 1 - c)
        to_sib = _remote(g_ref.at[1 - c], pair_buf, send_sems.at[0], recv_sems.at[0], sib)
        to_sib.start()
        to_sib.wait_recv()
        chip_buf[me] = g_ref[c] + pair_buf[...]
        sends = [to_sib]
        chips = _other_chips(x, y)
        for j, (px, py) in enumerate(chips):
            sends.append(_remote(chip_buf.at[me], chip_buf.at[me], send_sems.at[1 + j], recv_sems.at[1 + j], (px, py, c)))
            sends[-1].start()
        for j, (px, py) in enumerate(chips):
            _remote(chip_buf.at[me], chip_buf.at[2 * px + py], send_sems.at[1 + j], recv_sems.at[1 + j],
                    (px, py, c)).wait_recv()
        o_ref[c] = ((chip_buf[0] + chip_buf[1]) + chip_buf[2]) + chip_buf[3]
        sends.append(_remote(o_ref.at[c], o_ref.at[c], send_sems.at[4], recv_sems.at[4], sib))
        sends[-1].start()
        _remote(o_ref.at[c], o_ref.at[1 - c], send_sems.at[4], recv_sems.at[4], sib).wait_recv()
        for cp in sends:
            cp.wait_send()

    vmem = pl.BlockSpec(memory_space=pltpu.VMEM)
    return pl.pallas_call(
        body, name="allreduce_small", in_specs=[vmem], out_specs=vmem, out_shape=jax.ShapeDtypeStruct(g.shape, F32),
        scratch_shapes=[pltpu.VMEM(g.shape[1:], F32), pltpu.VMEM((N_CHIP,) + g.shape[1:], F32),
                        pltpu.SemaphoreType.DMA((5,)), pltpu.SemaphoreType.DMA((5,))],
        compiler_params=_params(),
    )(g)


EW_ROWS = 512


def _ew(name, fn, ins, n_out, out_dtype=F32, out_slot=None, into=None):
    def dims(a):
        return a[0].shape[1:] if isinstance(a, tuple) else a.shape

    R, w = dims(ins[0])
    tr = EW_ROWS if R % EW_ROWS == 0 else R
    n_into = len(into) if into else 0

    def body(c_ref, *refs):
        outs = fn(*[r[...] for r in refs[:len(ins)]])
        for r, v in zip(refs[len(ins) + n_into:], outs):
            r[...] = v.astype(r.dtype)

    def lead_spec(l):
        if l == "c":
            return pl.BlockSpec((None, tr, w), lambda i, c_ref: (c_ref[0], i, 0))
        return pl.BlockSpec((None, tr, w), lambda i, c_ref, s=l: (s, i, 0))

    plain = pl.BlockSpec((tr, w), lambda i, c_ref: (i, 0))
    in_specs = [lead_spec(a[1]) if isinstance(a, tuple) else plain for a in ins] + [_ANY] * n_into
    out_spec = plain if out_slot is None else lead_spec(out_slot)
    out_shape = jax.ShapeDtypeStruct((R, w) if out_slot is None else (2, R, w), out_dtype)
    return pl.pallas_call(
        body, name=name,
        grid_spec=pltpu.PrefetchScalarGridSpec(num_scalar_prefetch=1, grid=(R // tr,), in_specs=in_specs,
                                               out_specs=[out_spec] * n_out),
        out_shape=[out_shape] * n_out, input_output_aliases={1 + len(ins) + j: j for j in range(n_into)},
        compiler_params=_params(dimension_semantics=("arbitrary",)),
    )(lax.axis_index("c").astype(jnp.int32).reshape(1), *[a[0] if isinstance(a, tuple) else a for a in ins],
      *(into or []))


def _adamw_fn(w, g, m, v):
    m = ADAM_B1 * m + (1.0 - ADAM_B1) * g
    v = ADAM_B2 * v + (1.0 - ADAM_B2) * (g * g)
    m_hat = m / (1.0 - ADAM_B1 ** ADAM_STEP)
    v_hat = v / (1.0 - ADAM_B2 ** ADAM_STEP)
    delta = -ADAM_LR * (m_hat / (jnp.sqrt(v_hat) + ADAM_EPS) + ADAM_WD * w)
    return delta, m, v


def _adamw_layer(name, l, w, g, m, v, into):
    k = w.shape[-1]
    three = lambda a: (a.reshape(N_LAYER, -1, k), l)
    fn = lambda w_, g_, m_, v_: _adamw_fn(w_, g_, m_, v_) + (g_,)
    outs = _ew(name, fn, [three(w), g.reshape(-1, k), three(m), three(v)], 4, out_slot=l,
               into=None if into is None else [a.reshape(N_LAYER, -1, k) for a in into])
    return [o.reshape(w.shape) for o in outs]


def _adamw_rows(name, l, w, g, m, v, into):
    _, R, k = w.shape
    n_into = len(into) if into else 0

    def body(*refs):
        w_ref, g_ref, m_ref, v_ref = refs[:4]
        d_out, m_out, v_out, g_out = refs[4 + n_into:]
        g_blk = g_ref[...]
        d_out[...], m_out[...], v_out[...] = _adamw_fn(w_ref[...], g_blk, m_ref[...], v_ref[...])
        g_out[...] = g_blk

    spec = pl.BlockSpec((None, EW_ROWS, k), lambda i: (l, i, 0))
    return pl.pallas_call(
        body, name=name, grid=(-(-R // EW_ROWS),),
        in_specs=[spec, pl.BlockSpec((EW_ROWS, k), lambda i: (i, 0)), spec, spec] + [_ANY] * n_into,
        out_specs=[spec] * 4, out_shape=[jax.ShapeDtypeStruct((N_LAYER, R, k), F32)] * 4,
        input_output_aliases={4 + j: j for j in range(n_into)},
        compiler_params=_params(dimension_semantics=("arbitrary",)),
    )(w, g, m, v, *(into or []))


_SMALL = [("norm_pre", (2, 1024)), ("norm_post", (2, 1024)), ("norm_mem", (2, 1024)), ("a_log", (2, 4)),
          ("dt_bias", (2, 4)), ("dn_norm", (2, 128)), ("gm_norm", (2, 512)), ("spatial_w", (2, 4, 128, 128)),
          ("spatial_b", (2, 4, 128)), ("sinks", (2, 8)), ("loss", (2, 1))]
_SMALL_ROWS = 208
_BIG = ["w_in", "conv_w", "w_mem_kv", "w_up", "w_out"]
_NAMES = ["norm_pre", "norm_post", "norm_mem", "w_in", "conv_w", "a_log", "dt_bias", "dn_norm", "gm_norm",
          "spatial_w", "spatial_b", "sinks", "w_mem_kv", "w_up", "w_out"]


def _size(shape):
    n = 1
    for s in shape:
        n *= s
    return n


def _pack_small(d):
    rows = []
    for n, shp in _SMALL:
        a = d[n].reshape(N_LAYER, -1)
        rows.append(a.reshape(-1, 1024) if a.shape[1] > 1024 else jnp.pad(a, ((0, 6), (0, 1024 - a.shape[1]))))
    assert sum(r.shape[0] for r in rows) == _SMALL_ROWS
    return jnp.concatenate(rows, axis=0)


def _unpack_small(p):
    out, off = {}, 0
    for n, shp in _SMALL:
        c = _size(shp) // N_LAYER
        k = 8 if c <= 1024 else _size(shp) // 1024
        out[n] = (p[off:off + N_LAYER, :c] if c <= 1024 else p[off:off + k]).reshape(shp)
        off += k
    return out


_HALF_SHAPE = {"w_in": (SHARD_PAD // 2, D_MODEL), "conv_w": (2, 3 * BRANCH_W // N_CHIP), "w_mem_kv": (128, D_MODEL),
               "w_up": (2, BRANCH_W, D_MODEL // N_CHIP), "w_out": (128, D_MODEL)}


def _chip_major(g):
    g = jnp.swapaxes(g, 0, 1)
    return g.reshape((N_CHIP, 2 * g.shape[2]) + g.shape[3:])


def _half_major(g):
    g = g.reshape((N_CHIP, 2, g.shape[1] // 2) + g.shape[2:])
    return jnp.swapaxes(g, 0, 1).astype(BF16)


N_EARLY = 2


def _early_views(l, g_in, g_conv, small):
    return _layer_weights(l, _w_pad_from_slabs(g_in),
                          _chip_major(g_conv).transpose(1, 0, 2).reshape(4, 3 * BRANCH_W), small)


def _late_views(g_kv, g_up, g_out):
    return dict(w_mem_kv=_chip_major(g_kv).reshape(D_MODEL, D_MODEL),
                w_up=_chip_major(g_up).transpose(1, 2, 0, 3).reshape(4, BRANCH_W, D_MODEL),
                w_out=_chip_major(g_out).reshape(D_MODEL, D_MODEL))


def _pair_sums(g):
    add2 = lambda a, b: [a.astype(F32) + b.astype(F32)]
    rest = [_half_major(g["conv"].reshape(4, N_CHIP, 3 * BRANCH_W // N_CHIP).transpose(1, 0, 2)),
            _half_major(g["w_mem_kv"].reshape(N_CHIP, D_MODEL // N_CHIP, D_MODEL)),
            _half_major(g["w_up"].reshape(4, BRANCH_W, N_CHIP, D_MODEL // N_CHIP).transpose(2, 0, 1, 3)),
            _half_major(g["w_out"].reshape(N_CHIP, D_MODEL // N_CHIP, D_MODEL))]
    mine, theirs, rest_theirs = _slabs_from_pad(g["w_pad"], rest)
    pair = [_ew("pair_sum_w_in", add2, [mine.reshape(-1, D_MODEL), theirs.reshape(-1, D_MODEL)], 1, BF16)[0]
            .reshape(mine.shape)]
    for n, b, p in zip(_BIG[1:], rest, rest_theirs):
        k = b.shape[-1]
        pair.append(_ew("pair_sum_" + n, add2, [(b.reshape(2, -1, k), "c"), p.reshape(-1, k)], 1, BF16)[0]
                    .reshape(p.shape))
    return pair


def _chip_sums(landed, pair, me):
    add4 = lambda a, b, c_, d: [((a.astype(F32) + b.astype(F32)) + c_.astype(F32)) + d.astype(F32)]
    totals = []
    for n, r, q in zip(_BIG, landed, pair):
        r = _own_slot(r, lax.dynamic_index_in_dim(q, me, 0), me, 0)
        k = r.shape[-1]
        totals.append(_ew("chip_sum_" + n, add4, [(r.reshape(N_CHIP, -1, k), s) for s in range(N_CHIP)], 1,
                          out_slot="c")[0].reshape((2,) + r.shape[1:]))
    return totals


def _own_slot(buf, mine, me, axis):
    return lax.dynamic_update_index_in_dim(buf, mine.astype(buf.dtype), me, axis)


def kernel(x, mem, norm_pre, norm_post, norm_mem, w_in, conv_w, a_log, dt_bias, dn_norm, gm_norm, spatial_w, spatial_b, sinks, w_mem_kv, w_up, w_out, loss_target, m_norm_pre, m_norm_post, m_norm_mem, m_w_in, m_conv_w, m_a_log, m_dt_bias, m_dn_norm, m_gm_norm, m_spatial_w, m_spatial_b, m_sinks, m_w_mem_kv, m_w_up, m_w_out, v_norm_pre, v_norm_post, v_norm_mem, v_w_in, v_conv_w, v_a_log, v_dt_bias, v_dn_norm, v_gm_norm, v_spatial_w, v_spatial_b, v_sinks, v_w_mem_kv, v_w_up, v_w_out):
    w = dict(norm_pre=norm_pre, norm_post=norm_post, norm_mem=norm_mem, w_in=w_in, conv_w=conv_w, a_log=a_log,
             dt_bias=dt_bias, dn_norm=dn_norm, gm_norm=gm_norm, spatial_w=spatial_w, spatial_b=spatial_b, sinks=sinks,
             w_mem_kv=w_mem_kv, w_up=w_up, w_out=w_out)
    m = dict(norm_pre=m_norm_pre, norm_post=m_norm_post, norm_mem=m_norm_mem, w_in=m_w_in, conv_w=m_conv_w,
             a_log=m_a_log, dt_bias=m_dt_bias, dn_norm=m_dn_norm, gm_norm=m_gm_norm, spatial_w=m_spatial_w,
             spatial_b=m_spatial_b, sinks=m_sinks, w_mem_kv=m_w_mem_kv, w_up=m_w_up, w_out=m_w_out)
    v = dict(norm_pre=v_norm_pre, norm_post=v_norm_post, norm_mem=v_norm_mem, w_in=v_w_in, conv_w=v_conv_w,
             a_log=v_a_log, dt_bias=v_dt_bias, dn_norm=v_dn_norm, gm_norm=v_gm_norm, spatial_w=v_spatial_w,
             spatial_b=v_spatial_b, sinks=v_sinks, w_mem_kv=v_w_mem_kv, w_up=v_w_up, w_out=v_w_out)
    me = 2 * lax.axis_index("x") + lax.axis_index("y")

    tr = lambda a: a.transpose(0, 2, 1)
    w_t = tr(w_in)
    w_in_t = jnp.pad(w_t.astype(BF16), ((0, 0), (0, SHARD_PAD - SHARD_IN), (0, 0)))
    for d in (w, m, v):
        d["loss"] = jnp.zeros((N_LAYER, 1), F32)
    local = dict(w_in=w_in_t, conv_w=conv_w, w_mem_kv=w_mem_kv.astype(BF16), w_up=w_up.astype(BF16),
                 w_out=w_out.astype(BF16))
    halves = lambda l: [local[n][l].reshape((2,) + _HALF_SHAPE[n]) for n in _BIG]
    own = lambda gathered, mine: [_own_slot(g, h[:, None], me, 1) for g, h in zip(gathered, mine)]
    lands = [jax.ShapeDtypeStruct((2, N_CHIP) + _HALF_SHAPE[n], local[n].dtype) for n in _BIG]
    h0 = halves(0)
    g0 = own(_gather_weights(h0[:N_EARLY], [True, False]), h0[:N_EARLY])
    rest0 = _split_start("gather_l0_rest_start", "gather", h0[N_EARLY:], lands[N_EARLY:], g0[1])
    started = _split_start("gather_l1_start", "gather", halves(1), lands, rest0[3])

    xl, meml = x[0], mem[0]
    W0 = _early_views(0, g0[0], g0[1], w)
    W0["norm_pre"] = W0["norm_pre"] + started[3][0, 0]

    def late0(cols):
        mine, landed = _split_wait("gather_l0_rest_wait", "gather", rest0, cols)
        return _late_views(*own(_pair_forward(landed), mine))

    x1, sv0, W0 = _layer_fwd(xl, meml, W0, late0)
    mine1, landed1 = _split_wait("gather_l1_wait", "gather", started, x1)
    g1 = own(_pair_forward(landed1), mine1)
    W1 = dict(_early_views(1, g1[0], g1[1], w), **_late_views(*g1[N_EARLY:]))
    x2, sv1, _ = _layer_fwd(x1, meml, W1)
    dy, lrows = _rows_fwd("loss", _loss_fn, [Row(x2, D_MODEL, 0), Row(loss_target[0], D_MODEL, 0)], [],
                          [(D_MODEL, F32), (LANES, F32)], BIG_TR)
    loss_local = jnp.sum(lrows[:, 0])

    scattering = {}

    def start_scatter(l):
        def on_weight_grads(g):
            pair = _pair_sums(g)
            scattering[l] = _split_start("scatter_l%d_start" % l, "scatter", pair,
                                         [jax.ShapeDtypeStruct(p.shape, p.dtype) for p in pair], pair[1])
            return scattering[l][3]
        return on_weight_grads

    dx1, grads1 = _layer_bwd(dy, meml, W1, sv1, start_scatter(1))
    dx, grads0 = _layer_bwd(dx1, meml, W0, sv0, start_scatter(0))
    pair1, landed1 = _split_wait("scatter_l1_wait", "scatter", scattering[1], dx)
    after_start = scattering[0][3][0, 0]
    grads = [grads0, grads1]

    small_local = dict(
        norm_pre=jnp.stack([g["norm_pre"][0] for g in grads]), norm_post=jnp.stack([g["norm_post"][0] for g in grads]),
        norm_mem=jnp.stack([g["norm_mem"][0] for g in grads]), a_log=jnp.stack([g["a_vec"][0, 4:8] for g in grads]),
        dt_bias=jnp.stack([g["dt_vec"][0, 4:8] for g in grads]), dn_norm=jnp.stack([g["dn_norm"][0] for g in grads]),
        gm_norm=jnp.stack([g["gm_norm"][0] for g in grads]), spatial_w=jnp.stack([g["spatial_w"] for g in grads]),
        spatial_b=jnp.stack([g["spatial_b"][:, 0, :] for g in grads]),
        sinks=jnp.stack([g["sink_vec"][0, :8] for g in grads]),
        loss=jnp.stack([loss_local, jnp.zeros((), F32)]).reshape(N_LAYER, 1))
    packed = _pack_small(small_local) + after_start
    gsmall_packed = _allreduce_small(packed.reshape(2, -1, 1024)).reshape(-1, 1024)

    d_s, m_s, v_s = _ew("adamw_small", _adamw_fn, [_pack_small(w), gsmall_packed, _pack_small(m), _pack_small(v)], 3)
    gsmall, dsmall, msmall, vsmall = (_unpack_small(p) for p in (gsmall_packed, d_s, m_s, v_s))
    g_o, d_o, m_o, v_o = dict(gsmall), dict(dsmall), dict(msmall), dict(vsmall)
    loss = gsmall["loss"][0, 0]
    m_t, v_t = tr(m["w_in"]), tr(v["w_in"])

    def update(l, totals, into):
        outs = {}
        for n, t in zip(_BIG, totals):
            g_l = t.reshape(local[n].shape[1:])
            if n == "w_in":
                outs[n] = _adamw_rows("adamw_" + n, l, w_t, g_l, m_t, v_t, into and into[n])
            else:
                outs[n] = _adamw_layer("adamw_" + n, l, w[n], g_l, m[n], v[n], into and into[n])
        return outs

    landed1[1] = landed1[1] + after_start.astype(landed1[1].dtype)
    outs1 = update(1, _pair_share(_chip_sums(landed1, pair1, me)), None)
    pair0, landed0 = _split_wait("scatter_l0_wait", "scatter", scattering[0], outs1["w_in"][0])
    outs = update(0, _pair_share(_chip_sums(landed0, pair0, me)), outs1)
    for n in _BIG:
        d_o[n], m_o[n], v_o[n], g_o[n] = [tr(o) for o in outs[n]] if n == "w_in" else outs[n]
    return (loss, dx[None], *[g_o[n] for n in _NAMES], *[d_o[n] for n in _NAMES], *[m_o[n] for n in _NAMES],
            *[v_o[n] for n in _NAMES])
```

```python
import collections
import functools

import jax
import jax.numpy as jnp
from jax import lax
from jax.experimental import pallas as pl
from jax.experimental.pallas import tpu as pltpu

F32 = jnp.float32
BF16 = jnp.bfloat16

D_MODEL = 1024
BRANCH_W = 512
MEM_LEN = 256
N_LAYER = 2
N_CHIP = 4
EPS = 1e-6
NEG_INF = -1e30
DN_CHUNK = 64
SW_HD = 64
LANES = 128
VMEM_LIMIT = 48 * 1024 * 1024

ADAM_LR, ADAM_B1, ADAM_B2, ADAM_EPS, ADAM_WD, ADAM_STEP = 0.001, 0.9, 0.999, 1e-08, 0.01, 10

N_PAD = 10240
O_GATE = 0
O_AQ, O_AK, O_AV, O_AZ = 4096, 4608, 5120, 5632
O_BUV, O_BZ = 6144, 7168
O_CKV, O_BA = 7680, 7936
O_CQ, O_CZ = 8192, 8704
O_MQ, O_MZ = 9216, 9728
O_MISC, W_MISC = O_CKV, 512
_PAD_SEGS = [(5896, 4096), (0, 512), (512, 512), (1024, 512), (1536, 512), (2056, 1024), (3080, 512),
             (4104, 128), (4232, 128), (2048, 8), (None, 120), (None, 128),
             (3592, 512), (4360, 512), (4872, 512), (5384, 512)]
D_IN = 9992
SHARD_IN = D_IN // N_CHIP


SHARD_PAD = 2560


def _pad_parts():
    parts, off = [], 0
    for s, n in _PAD_SEGS:
        a = s
        while s is not None and a < s + n:
            chip = a // SHARD_IN
            b = min(s + n, (chip + 1) * SHARD_IN)
            parts.append((chip, a - chip * SHARD_IN, off + a - s, b - a))
            a = b
        off += n
    return parts


PERM_ROWS = 1024
PERM_SLACK = 32


def _permute_rows(name, src, parts, n_out, out_dtype, pair_split=False, exchange=()):
    B, Z = PERM_ROWS, PERM_ROWS + PERM_SLACK
    w = src.shape[1]
    plans = []
    for blk in range(n_out // B):
        o, runs = blk * B, []
        for s, d, n in parts:
            lo, hi = max(d, o), min(d + n, o + B)
            if lo < hi:
                s0 = s + lo - d
                wa = s0 // 16 * 16
                wb = min(-(-(s0 + hi - lo) // 16) * 16, src.shape[0])
                runs.append((wa, wb - wa, s0 - (lo - o) - wa, lo - o, hi - o))
        plans.append(runs)
    max_runs = max(len(r) for r in plans)
    nblk = len(plans)

    per_half = nblk // 2

    n_ex = len(exchange)

    def body(*refs):
        src_ref, out_ref = refs[0], refs[1 + n_ex]
        if pair_split:
            theirs_ref, ex_out = refs[2 + n_ex], refs[3 + n_ex:3 + 2 * n_ex]
            inbuf, obuf, insem, outsem, to_sib_sem, from_sib_sem, ex_send, ex_recv = refs[3 + 2 * n_ex:]
            x, y, c = _position()
            sibling = (x, y, 1 - c)
            ex_copies = [_remote(refs[1 + a].at[1 - c], ex_out[a], ex_send.at[a], ex_recv.at[a], sibling)
                         for a in range(n_ex)]
            for cp in ex_copies:
                cp.start()
        else:
            inbuf, obuf, insem, outsem = refs[2:]

        def in_copies(blk):
            return [pltpu.make_async_copy(src_ref.at[pl.ds(wa, ws)], inbuf.at[blk % 2, r, pl.ds(0, ws)],
                                          insem.at[blk % 2, r]) for r, (wa, ws, _, _, _) in enumerate(plans[blk])]

        class out_copy:
            def __init__(self, blk):
                self.blk, self.rows = blk, pl.ds((blk % per_half if pair_split else blk) * B, B)
                self.local = pltpu.make_async_copy(obuf.at[blk % 2], out_ref.at[self.rows], outsem.at[blk % 2])

            def _both(self, local_op, remote_op):
                if not pair_split:
                    return local_op(self.local)
                mine = c == self.blk // per_half
                pl.when(mine)(lambda: local_op(self.local))
                pl.when(jnp.logical_not(mine))(lambda: remote_op(_remote(
                    obuf.at[self.blk % 2], theirs_ref.at[self.rows], to_sib_sem.at[self.blk % 2],
                    from_sib_sem.at[self.blk % per_half], sibling)))

            def start(self):
                self._both(lambda cp: cp.start(), lambda cp: cp.start())

            def wait(self):
                self._both(lambda cp: cp.wait(), lambda cp: cp.wait_send())

        for cp in in_copies(0):
            cp.start()
        rid = _iota((B, 1), 0)
        for blk in range(nblk):
            if blk + 1 < nblk:
                for cp in in_copies(blk + 1):
                    cp.start()
            for cp in in_copies(blk):
                cp.wait()
            val = jnp.zeros((B, w), F32)
            for r, (wa, ws, t, l0, l1) in enumerate(plans[blk]):
                win = jnp.concatenate([inbuf[blk % 2, r, pl.ds(0, ws)].astype(F32), jnp.zeros((Z - ws, w), F32)], axis=0)
                moved = pltpu.roll(win, (-t) % Z, 0)[:B]
                val = jnp.where((rid >= l0) & (rid < l1), moved, val)
            if blk >= 2:
                out_copy(blk - 2).wait()
            obuf[blk % 2] = val.astype(out_dtype)
            out_copy(blk).start()
        for blk in range(max(nblk - 2, 0), nblk):
            out_copy(blk).wait()
        if pair_split:
            for i in range(per_half):
                rows = theirs_ref.at[pl.ds(i * B, B)]
                _remote(rows, rows, to_sib_sem.at[0], from_sib_sem.at[i], sibling).wait_recv()
            for cp in ex_copies:
                cp.wait_recv()
            for cp in ex_copies:
                cp.wait_send()

    scratch = [pltpu.VMEM((2, max_runs, Z, w), src.dtype), pltpu.VMEM((2, B, w), out_dtype),
               pltpu.SemaphoreType.DMA((2, max_runs)), pltpu.SemaphoreType.DMA((2,))]
    if not pair_split:
        return pl.pallas_call(
            body, name=name, in_specs=[_ANY], out_specs=_ANY, out_shape=jax.ShapeDtypeStruct((n_out, w), out_dtype),
            scratch_shapes=scratch, compiler_params=_params(),
        )(src)
    scratch += [pltpu.SemaphoreType.DMA((2,)), pltpu.SemaphoreType.DMA((per_half,)),
                pltpu.SemaphoreType.DMA((max(n_ex, 1),)), pltpu.SemaphoreType.DMA((max(n_ex, 1),))]
    out_shape = [jax.ShapeDtypeStruct((n_out // 2, w), out_dtype)] * 2
    out_shape += [jax.ShapeDtypeStruct(a.shape[1:], a.dtype) for a in exchange]
    res = pl.pallas_call(
        body, name=name, in_specs=[_ANY] * (1 + n_ex), out_specs=[_ANY] * (2 + n_ex), out_shape=out_shape,
        scratch_shapes=scratch, compiler_params=_params(),
    )(src, *exchange)
    return res[0], res[1], list(res[2:])


def _slab_parts():
    h, out = SHARD_PAD // 2, []
    for chip, s, d, n in _pad_parts():
        a = s
        while a < s + n:
            half = a // h
            b = min(s + n, (half + 1) * h)
            out.append(((half * N_CHIP + chip) * h + a - half * h, d + a - s, b - a))
            a = b
    return out


def _w_pad_from_slabs(slabs):
    return _permute_rows("w_pad_rows", slabs.reshape(-1, slabs.shape[-1]), _slab_parts(), N_PAD, BF16)


def _slabs_from_pad(dw, exchange=()):
    mine, theirs, exchanged = _permute_rows("w_pad_grad_rows", dw, [(d, s, n) for s, d, n in _slab_parts()],
                                            N_CHIP * SHARD_PAD, BF16, pair_split=True, exchange=exchange)
    shape = (N_CHIP, SHARD_PAD // 2, dw.shape[1])
    return mine.reshape(shape), theirs.reshape(shape), exchanged


def _dot(a, b, dims, prec):
    if prec == "bf16":
        return lax.dot_general(a.astype(BF16), b.astype(BF16), (dims, ((), ())), preferred_element_type=F32)
    return lax.dot_general(a, b, (dims, ((), ())), precision=lax.Precision.HIGHEST, preferred_element_type=F32)


_NN, _NT, _TN = ((1,), (0,)), ((1,), (1,)), ((0,), (0,))


def _make_mm(prec):
    @jax.custom_vjp
    def nn(a, b):
        return _dot(a, b, _NN, prec)

    @jax.custom_vjp
    def nt(a, b):
        return _dot(a, b, _NT, prec)

    @jax.custom_vjp
    def tn(a, b):
        return _dot(a, b, _TN, prec)

    nn.defvjp(lambda a, b: (nn(a, b), (a, b)), lambda r, g: (nt(g, r[1]), tn(r[0], g)))
    nt.defvjp(lambda a, b: (nt(a, b), (a, b)), lambda r, g: (nn(g, r[1]), tn(g, r[0])))
    tn.defvjp(lambda a, b: (tn(a, b), (a, b)), lambda r, g: (nt(r[1], g), nn(r[0], g)))
    return nn, nt, tn


_nn16, _nt16, _tn16 = _make_mm("bf16")
_nn32 = _make_mm("f32")[0]


def _make_slice(axis):
    @functools.partial(jax.custom_vjp, nondiff_argnums=(1, 2, 3))
    def sl(x, a, b, n):
        return x[a:b] if axis == 0 else x[:, a:b]

    def fwd(x, a, b, n):
        return sl(x, a, b, n), None

    def bwd(a, b, n, _, g):
        parts = []
        if a > 0:
            parts.append(jnp.zeros((a, g.shape[1]) if axis == 0 else (g.shape[0], a), g.dtype))
        parts.append(g)
        if n - b > 0:
            parts.append(jnp.zeros((n - b, g.shape[1]) if axis == 0 else (g.shape[0], n - b), g.dtype))
        return (jnp.concatenate(parts, axis=axis),)

    sl.defvjp(fwd, bwd)
    return sl


_sl0, _sl1 = _make_slice(0), _make_slice(1)


def _rowsl(x, a, b):
    return _sl0(x, a, b, x.shape[0])


def _cols(x, a, b):
    return _sl1(x, a, b, x.shape[1])


@functools.partial(jax.custom_vjp, nondiff_argnums=(1,))
def _rollr(x, s):
    return pltpu.roll(x, s, 0)


_rollr.defvjp(lambda x, s: (_rollr(x, s), None),
              lambda s, _, g: (pltpu.roll(g, g.shape[0] - s, 0),))


def _iota(shape, axis):
    return lax.broadcasted_iota(jnp.int32, shape, axis)


def _sigmoid(x):
    return lax.logistic(x)


def _silu(x):
    return x * _sigmoid(x)


def _gelu(x):
    return 0.5 * x * (1.0 + jnp.tanh(0.7978845608028654 * (x + 0.044715 * (x * x * x))))


def _softplus(x):
    return jnp.maximum(x, 0.0) + jnp.log(1.0 + jnp.exp(-jnp.abs(x)))


def _rms(x, g):
    return x * lax.rsqrt(jnp.mean(x * x, axis=-1, keepdims=True) + EPS) * g


def _lane_pick(x, lane):
    return jnp.sum(x * (_iota((1, x.shape[1]), 1) == lane).astype(F32), axis=1, keepdims=True)


Row = collections.namedtuple("Row", "arr w cb hb grad", defaults=(0, True))


def _full_spec(shape):
    return pl.BlockSpec(shape, lambda i, _n=len(shape): (0,) * _n)


def _load_params(refs):
    return [[p[g].astype(F32) for g in range(p.shape[0])] if len(p.shape) == 3 else p[...].astype(F32)
            for p in refs]


def _params(**kw):
    return pltpu.CompilerParams(vmem_limit_bytes=VMEM_LIMIT, **kw)


def _rows_fwd(name, fn, rows, params, outs, tr, carry=None):
    T = rows[0].arr.shape[0]
    n = T // tr
    halos = [r for r in rows if r.hb]
    nr, nh, npar, no = len(rows), len(halos), len(params), len(outs)

    def body(*refs):
        row_refs, halo_refs = refs[:nr], refs[nr:nr + nh]
        par_refs = refs[nr + nh:nr + nh + npar]
        out_refs = refs[nr + nh + npar:nr + nh + npar + no]
        rest = refs[nr + nh + npar + no:]
        first = pl.program_id(0) == 0
        cvals = None
        if carry is not None:
            csave_ref, carry_ref = rest

            @pl.when(first)
            def _():
                carry_ref[...] = jnp.zeros_like(carry_ref)

            cvals = [carry_ref[g] for g in range(carry[0])]
            for g in range(carry[0]):
                csave_ref[0, g] = cvals[g]
        c_out, o = fn(first, cvals, [r[...].astype(F32) for r in row_refs],
                      [h[...].astype(F32) for h in halo_refs], _load_params(par_refs))
        for r, v in zip(out_refs, o):
            r[...] = v.astype(r.dtype)
        if carry is not None:
            for g in range(carry[0]):
                carry_ref[g] = c_out[g]

    in_specs = [pl.BlockSpec((tr, r.w), lambda i, c=r.cb: (i, c)) for r in rows]
    in_specs += [pl.BlockSpec((r.hb, r.w), lambda i, c=r.cb, q=tr // r.hb: (jnp.maximum(i * q - 1, 0), c))
                 for r in halos]
    in_specs += [_full_spec(p.shape) for p in params]
    out_shape = [jax.ShapeDtypeStruct((T, w), dt) for w, dt in outs]
    out_specs = [pl.BlockSpec((tr, w), lambda i: (i, 0)) for w, _ in outs]
    scratch = []
    if carry is not None:
        out_shape.append(jax.ShapeDtypeStruct((n,) + carry, F32))
        out_specs.append(pl.BlockSpec((1,) + carry, lambda i: (i, 0, 0, 0)))
        scratch.append(pltpu.VMEM(carry, F32))
    return pl.pallas_call(
        body, name=name, grid=(n,), in_specs=in_specs, out_specs=out_specs, out_shape=out_shape,
        scratch_shapes=scratch, compiler_params=_params(dimension_semantics=("arbitrary",)),
    )(*[r.arr for r in rows], *[r.arr for r in halos], *params)


def _rows_bwd(name, fn, rows, params, douts, tr, carry=None, csave=None, dcols=None):
    T = rows[0].arr.shape[0]
    n = T // tr
    halos = [r for r in rows if r.hb]
    grows = [r for r in rows if r.grad is True]
    crows = [r for r in rows if r.grad == "cols"]
    wcols = sum(r.w for r in crows)
    nr, nh, npar, nd, ng = len(rows), len(halos), len(params), len(douts), len(grows)
    nc = 0 if carry is None else 1
    ncol = 1 if crows else 0
    nalias = 1 if (crows and dcols is not None) else 0

    def body(*refs):
        row_refs, halo_refs = refs[:nr], refs[nr:nr + nh]
        par_refs = refs[nr + nh:nr + nh + npar]
        k = nr + nh + npar
        csave_ref = refs[k] if nc else None
        dout_refs = refs[k + nc:k + nc + nd]
        k = k + nc + nd + nalias
        drow_refs = refs[k:k + ng]
        dcols_ref = refs[k + ng] if ncol else None
        dpar_refs = refs[k + ng + ncol:k + ng + ncol + npar]
        k = k + ng + ncol + npar
        dcarry_ref = refs[k] if nc else None
        hgrad_refs = refs[k + nc:]
        i = pl.program_id(0)
        first_tile = i == n - 1

        @pl.when(i == 0)
        def _():
            for r in dpar_refs:
                r[...] = jnp.zeros_like(r)
            for r in hgrad_refs:
                r[...] = jnp.zeros_like(r)
            if nc:
                dcarry_ref[...] = jnp.zeros_like(dcarry_ref)

        rv = [r[...].astype(F32) for r in row_refs]
        hv = [h[...].astype(F32) for h in halo_refs]
        pv = _load_params(par_refs)
        dov = [d[...].astype(F32) for d in dout_refs]
        if nc:
            cv = [csave_ref[0, g] for g in range(carry[0])]
            _, vjp = jax.vjp(lambda c, r, h, p: fn(first_tile, c, r, h, p), cv, rv, hv, pv)
            dc, dr, dh, dp = vjp(([dcarry_ref[g] for g in range(carry[0])], dov))
            for g in range(carry[0]):
                dcarry_ref[g] = dc[g]
        else:
            _, vjp = jax.vjp(lambda r, h, p: fn(first_tile, None, r, h, p)[1], rv, hv, pv)
            dr, dh, dp = vjp(dov)
        gi = hi = 0
        pieces = []
        for kk, r in enumerate(rows):
            d = dr[kk]
            if r.hb:
                carried = hgrad_refs[hi][...]
                d = d + (carried if tr == r.hb else
                         jnp.concatenate([jnp.zeros((tr - r.hb, r.w), F32), carried], axis=0))
                hgrad_refs[hi][...] = dh[hi]
                hi += 1
            if r.grad is True:
                drow_refs[gi][...] = d.astype(drow_refs[gi].dtype)
                gi += 1
            elif r.grad == "cols":
                pieces.append(d.astype(BF16))
        if ncol:
            dcols_ref[...] = pieces[0] if len(pieces) == 1 else jnp.concatenate(pieces, axis=1)
        for r, d in zip(dpar_refs, dp):
            if len(r.shape) == 3:
                for g in range(r.shape[0]):
                    r[g] += d[g]
            else:
                r[...] += d

    rev = lambda i: n - 1 - i
    in_specs = [pl.BlockSpec((tr, r.w), lambda i, c=r.cb: (rev(i), c)) for r in rows]
    in_specs += [pl.BlockSpec((r.hb, r.w), lambda i, c=r.cb, q=tr // r.hb: (jnp.maximum(rev(i) * q - 1, 0), c))
                 for r in halos]
    in_specs += [_full_spec(p.shape) for p in params]
    args = [r.arr for r in rows] + [r.arr for r in halos] + list(params)
    scratch = []
    if nc:
        in_specs.append(pl.BlockSpec((1,) + carry, lambda i: (rev(i), 0, 0, 0)))
        args.append(csave)
        scratch.append(pltpu.VMEM(carry, F32))
    douts = [d if isinstance(d, Row) else Row(d, d.shape[1], 0) for d in douts]
    in_specs += [pl.BlockSpec((tr, d.w), lambda i, c=d.cb: (rev(i), c)) for d in douts]
    args += [d.arr for d in douts]
    aliases = {}
    if nalias:
        aliases = {len(args): ng}
        in_specs.append(pl.BlockSpec(memory_space=pl.ANY))
        args.append(dcols)
    scratch += [pltpu.VMEM((r.hb, r.w), F32) for r in halos]
    out_shape = [jax.ShapeDtypeStruct((T, r.w), F32) for r in grows]
    out_specs = [pl.BlockSpec((tr, r.w), lambda i: (rev(i), 0)) for r in grows]
    if ncol:
        off = crows[0].cb * crows[0].w
        assert off % wcols == 0 and all(a.cb * a.w + a.w == b.cb * b.w for a, b in zip(crows, crows[1:]))
        out_shape.append(jax.ShapeDtypeStruct((T, N_PAD), BF16))
        out_specs.append(pl.BlockSpec((tr, wcols), lambda i, c=off // wcols: (rev(i), c)))
    out_shape += [jax.ShapeDtypeStruct(p.shape, F32) for p in params]
    out_specs += [_full_spec(p.shape) for p in params]
    res = pl.pallas_call(
        body, name=name, grid=(n,), in_specs=in_specs, out_specs=out_specs, out_shape=out_shape,
        scratch_shapes=scratch, input_output_aliases=aliases,
        compiler_params=_params(dimension_semantics=("arbitrary",)),
    )(*args)
    return list(res[:ng]), list(res[ng + ncol:]), (res[ng] if ncol else dcols)


def _fill_misc(dcols, dkv, dba, tr):
    T = dkv.shape[0]

    def body(kv_ref, ba_ref, _, o_ref):
        o_ref[...] = jnp.concatenate([kv_ref[...], ba_ref[...]], axis=1).astype(BF16)

    return pl.pallas_call(
        body, name="misc_bwd", grid=(T // tr,),
        in_specs=[pl.BlockSpec((tr, 256), lambda i: (i, 0)), pl.BlockSpec((tr, 256), lambda i: (i, 0)),
                  pl.BlockSpec(memory_space=pl.ANY)],
        out_specs=pl.BlockSpec((tr, W_MISC), lambda i: (i, O_MISC // W_MISC)),
        out_shape=jax.ShapeDtypeStruct((T, N_PAD), BF16), input_output_aliases={2: 0},
        compiler_params=_params(dimension_semantics=("arbitrary",)),
    )(dkv, dba, dcols)


def _up_bwd(ys, cols, dm, w_up):
    T = dm.shape[0]
    tr = min(UPB_TR, T)

    def body(y_ref, gl_ref, dm_ref, w_ref, dy_ref, dgl_ref, dw_ref):
        @pl.when(pl.program_id(1) == 0)
        def _():
            dw_ref[...] = jnp.zeros_like(dw_ref)

        _, vjp = jax.vjp(lambda y, gl, w: _sigmoid(gl) * _nn16(y, w),
                         y_ref[...].astype(F32), gl_ref[...].astype(F32), w_ref[...].astype(F32))
        dy, dgl, dw = vjp(dm_ref[...])
        dy_ref[...] = dy
        dgl_ref[...] = dgl.astype(BF16)
        dw_ref[...] += dw

    branch_rows = lambda w: pl.BlockSpec((tr, w), lambda n, i: (i, n))
    weight = pl.BlockSpec((None, BRANCH_W, D_MODEL), lambda n, i: (n, 0, 0))
    return pl.pallas_call(
        body, name="up_bwd", grid=(4, T // tr),
        in_specs=[branch_rows(BRANCH_W), branch_rows(D_MODEL), pl.BlockSpec((tr, D_MODEL), lambda n, i: (i, 0)), weight],
        out_specs=[branch_rows(BRANCH_W), branch_rows(D_MODEL), weight],
        out_shape=[jax.ShapeDtypeStruct((T, 4 * BRANCH_W), F32), jax.ShapeDtypeStruct((T, N_PAD), BF16),
                   jax.ShapeDtypeStruct(w_up.shape, F32)],
        compiler_params=_params(dimension_semantics=("arbitrary", "arbitrary")),
    )(ys, cols, dm, w_up)


def _matmul(name, a, b, kind, out_dtype, tm, tn, tk, after=None):
    if kind == "tn":
        (K, M), N = a.shape, b.shape[1]
    else:
        (M, K), N = a.shape, (b.shape[0] if kind == "nt" else b.shape[1])
    tm, tn, tk = min(tm, M), min(tn, N), min(tk, K)
    nk = K // tk
    dims = {"nn": _NN, "nt": _NT, "tn": _TN}[kind]

    n_after = 0 if after is None else 1

    def body(*refs):
        a_ref, b_ref, o_ref, acc = refs[0], refs[1], refs[2 + n_after], refs[3 + n_after:]
        part = lax.dot_general(a_ref[...], b_ref[...], (dims, ((), ())), preferred_element_type=F32)
        if nk == 1:
            o_ref[...] = part.astype(o_ref.dtype)
            return
        acc_ref = acc[0] if acc else o_ref
        k = pl.program_id(2)

        @pl.when(k == 0)
        def _():
            acc_ref[...] = part

        @pl.when(k > 0)
        def _():
            acc_ref[...] += part

        if acc:
            @pl.when(k == nk - 1)
            def _():
                o_ref[...] = acc_ref[...].astype(o_ref.dtype)

    a_spec = pl.BlockSpec((tk, tm), lambda i, j, k: (k, i)) if kind == "tn" else pl.BlockSpec((tm, tk), lambda i, j, k: (i, k))
    b_spec = pl.BlockSpec((tn, tk), lambda i, j, k: (j, k)) if kind == "nt" else pl.BlockSpec((tk, tn), lambda i, j, k: (k, j))
    return pl.pallas_call(
        body, name=name, grid=(M // tm, N // tn, nk), in_specs=[a_spec, b_spec] + [_ANY] * n_after,
        out_specs=pl.BlockSpec((tm, tn), lambda i, j, k: (i, j)),
        out_shape=jax.ShapeDtypeStruct((M, N), out_dtype),
        scratch_shapes=[pltpu.VMEM((tm, tn), F32)] if nk > 1 and out_dtype != F32 else [],
        compiler_params=_params(dimension_semantics=("arbitrary", "arbitrary", "arbitrary")),
    )(a, b, *([] if after is None else [after]))


def _pre_fn(first, _, rows, halos, params):
    return None, [_rms(rows[0], params[0])]


def _pre_fn_res(first, _, rows, halos, params):
    return None, [_rms(rows[0], params[0]), rows[0]]


def _memkv_fn(first, _, rows, halos, params):
    g, w = params
    return None, [_nn16(_rms(rows[0], g), w)]


def _conv_silu(x, halo, w4, keep_halo):
    tr, hb = x.shape[0], halo.shape[0]
    xh = jnp.concatenate([halo * keep_halo, x], axis=0)
    acc = w4[3] * x
    for s in (1, 2, 3):
        acc = acc + w4[3 - s] * _rowsl(_rollr(xh, s), hb, hb + tr)
    return _silu(acc)


def _dn_fn(first, S, rows, halos, params):
    qp, kp, vp, z, ba = rows
    conv, a_vec, dt_vec, dnorm = params
    ba = _cols(ba, 0, LANES)
    tr = qp.shape[0]
    keep = jnp.where(first, 0.0, 1.0)
    q = _conv_silu(qp, halos[0], [conv[3 * j + 0] for j in range(4)], keep)
    k = _conv_silu(kp, halos[1], [conv[3 * j + 1] for j in range(4)], keep)
    v = _conv_silu(vp, halos[2], [conv[3 * j + 2] for j in range(4)], keep)
    qh, kh, vh = [], [], []
    for h in range(4):
        a, b = h * LANES, (h + 1) * LANES
        xq, xk = _cols(q, a, b), _cols(k, a, b)
        qh.append(xq * lax.rsqrt(jnp.sum(xq * xq, axis=1, keepdims=True) + EPS) * (LANES ** -0.5))
        kh.append(xk * lax.rsqrt(jnp.sum(xk * xk, axis=1, keepdims=True) + EPS))
        vh.append(_cols(v, a, b))
    beta_all = _sigmoid(ba)
    g_all = -jnp.exp(a_vec) * _softplus(ba + dt_vec)
    C = DN_CHUNK
    ii, jj = _iota((C, C), 0), _iota((C, C), 1)
    strict, incl = ii > jj, ii >= jj
    eye = (ii == jj).astype(F32)
    last_row = (_iota((C, 1), 0) == C - 1).astype(F32)
    n_chunk = tr // C
    pairs = [(c, h) for c in range(n_chunk) for h in range(4)]
    rows_of = lambda a, c: _rowsl(a, c * C, (c + 1) * C)
    gcs = [_nn32(incl.astype(F32), rows_of(g_all, c)) for c in range(n_chunk)]
    qc = {(c, h): rows_of(qh[h], c) for c, h in pairs}
    kc = {(c, h): rows_of(kh[h], c) for c, h in pairs}
    beta = {(c, h): _lane_pick(rows_of(beta_all, c), h) for c, h in pairs}
    gc = {(c, h): _lane_pick(gcs[c], 4 + h) for c, h in pairs}
    dec = {p: jnp.exp(jnp.where(incl, gc[p] - jnp.sum(eye * gc[p], axis=0, keepdims=True), 0.0)) for p in pairs}
    egc = {p: jnp.exp(gc[p]) for p in pairs}
    kb = {p: kc[p] * beta[p] for p in pairs}
    kq = {p: _nt16(jnp.concatenate([kb[p], qc[p]], axis=0), kc[p]) for p in pairs}
    P = {p: -jnp.where(strict, _rowsl(kq[p], 0, C) * dec[p], 0.0) for p in pairs}
    aqk = {p: jnp.where(incl, _rowsl(kq[p], C, 2 * C) * dec[p], 0.0) for p in pairs}
    tinv = {p: eye + P[p] for p in pairs}
    P = {p: _nn16(P[p], P[p]) for p in pairs}
    for j in range(5):
        if j < 4:
            pt = {p: _nn16(jnp.concatenate([P[p], tinv[p]], axis=0), P[p]) for p in pairs}
            tinv = {p: tinv[p] + _rowsl(pt[p], C, 2 * C) for p in pairs}
            P = {p: _rowsl(pt[p], 0, C) for p in pairs}
        else:
            tinv = {p: tinv[p] + _nn16(tinv[p], P[p]) for p in pairs}
    uw = {(c, h): _nn16(tinv[c, h], jnp.concatenate([rows_of(vh[h], c) * beta[c, h], kb[c, h] * egc[c, h]], axis=1))
          for c, h in pairs}
    S = list(S)
    ychunks = []
    for c in range(n_chunk):
        zc = rows_of(z, c)
        hs = range(4)
        ws = [_nn16(jnp.concatenate([_cols(uw[c, h], LANES, 2 * LANES), qc[c, h] * egc[c, h]], axis=0), S[h]) for h in hs]
        vnew = [_cols(uw[c, h], 0, LANES) - _rowsl(ws[h], 0, C) for h in hs]
        o = [_rowsl(ws[h], C, 2 * C) + _nn16(aqk[c, h], vnew[h]) for h in hs]
        glast = [jnp.sum(gc[c, h] * last_row, axis=0, keepdims=True) for h in hs]
        S = [S[h] * jnp.exp(glast[h]) + _tn16(kc[c, h] * jnp.exp(glast[h] - gc[c, h]), vnew[h]) for h in hs]
        ychunks.append(jnp.concatenate(
            [_rms(o[h], dnorm) * _silu(_cols(zc, h * LANES, (h + 1) * LANES)) for h in hs], axis=1))
    return S, [jnp.concatenate(ychunks, axis=0)]


def _gm_fn(first, _, rows, halos, params):
    uv, z = rows
    gnorm, ws, bs = params
    tr = uv.shape[0]
    guv = _gelu(uv)
    u = _cols(guv, 0, BRANCH_W)
    v = _rms(_cols(guv, BRANCH_W, 2 * BRANCH_W), gnorm)
    ii, jj = _iota((LANES, LANES), 0), _iota((LANES, LANES), 1)
    eye = (ii == jj).astype(F32)
    wsm = [jnp.where(ii >= jj, ws[g], 0.0) for g in range(4)]
    bcol = [jnp.sum(eye * bs[g], axis=1, keepdims=True) for g in range(4)]
    chunks = []
    for c in range(tr // LANES):
        vc = _rowsl(v, c * LANES, (c + 1) * LANES)
        chunks.append(jnp.concatenate(
            [_nn16(wsm[g], _cols(vc, g * LANES, (g + 1) * LANES)) + bcol[g] for g in range(4)], axis=1))
    return None, [u * jnp.concatenate(chunks, axis=0) * _silu(z)]


def _swa_fn(first, _, rows, halos, params):
    q, kvc, z = rows
    sink_vec = params[0]
    P = LANES
    n_blk = q.shape[0] // P
    r, cc = _iota((P, P), 0), _iota((P, P), 1)
    lane = _iota((1, P), 1)
    key = _iota((P, 2 * P), 1)
    dist = _iota((P, 2 * P), 0) + P - key
    in_window = (dist >= 0) & (dist < P)
    valid = [in_window & (key >= jnp.where(first, P, 0))] + [in_window] * (n_blk - 1)
    halves = [(lane < SW_HD).astype(F32), (lane >= SW_HD).astype(F32)]
    dup = [(r == kh * SW_HD + (cc & (SW_HD - 1))).astype(F32) for kh in range(2)]
    kv_blk = [halos[0]] + [_rowsl(kvc, b * P, (b + 1) * P) for b in range(n_blk)]
    pairs = [(b, kh) for b in range(n_blk) for kh in range(2)]
    kkvv = {}
    for b in range(n_blk):
        kv = jnp.concatenate([kv_blk[b], kv_blk[b + 1]], axis=0)
        k_v = jnp.concatenate([_cols(kv, 0, P), _cols(kv, P, 2 * P)], axis=0)
        for kh in range(2):
            kkvv[b, kh] = _nn16(k_v, dup[kh])
    scores = {}
    for b, kh in pairs:
        q_b = _rowsl(q, b * P, (b + 1) * P)
        stacked = jnp.concatenate([_cols(q_b, (2 * kh + g // 2) * P, (2 * kh + g // 2 + 1) * P) * halves[g % 2]
                                   for g in range(4)], axis=0)
        scores[b, kh] = _nt16(stacked, _rowsl(kkvv[b, kh], 0, 2 * P))
    probs = {}
    for b, kh in pairs:
        ps = []
        for g in range(4):
            s = jnp.where(valid[b], _rowsl(scores[b, kh], g * P, (g + 1) * P) * (SW_HD ** -0.5), NEG_INF)
            sink = _lane_pick(sink_vec, kh * 4 + g)
            m = lax.stop_gradient(jnp.maximum(jnp.max(s, axis=1, keepdims=True), sink))
            e = jnp.exp(s - m)
            ps.append(e / (jnp.sum(e, axis=1, keepdims=True) + jnp.exp(sink - m)))
        probs[b, kh] = jnp.concatenate(ps, axis=0)
    outs = {p: _nn16(probs[p], _rowsl(kkvv[p], 2 * P, 4 * P)) for p in pairs}
    tile = [jnp.concatenate([_rowsl(outs[b, j // 2], (2 * (j % 2)) * P, (2 * (j % 2) + 1) * P) * halves[0]
                             + _rowsl(outs[b, j // 2], (2 * (j % 2) + 1) * P, (2 * (j % 2) + 2) * P) * halves[1]
                             for j in range(4)], axis=1) for b in range(n_blk)]
    return None, [jnp.concatenate(tile, axis=0) * _silu(z)]


def _mem_fn(first, _, rows, halos, params):
    q, z = rows
    mkv = params[0]
    heads = [(h * LANES, (h + 1) * LANES) for h in range(4)]
    scores = [_nt16(_cols(q, a, b), _cols(mkv, a, b)) * (LANES ** -0.5) for a, b in heads]
    probs = []
    for s in scores:
        e = jnp.exp(s - lax.stop_gradient(jnp.max(s, axis=1, keepdims=True)))
        probs.append(e / jnp.sum(e, axis=1, keepdims=True))
    outs = [_nn16(p, _cols(mkv, BRANCH_W + a, BRANCH_W + b)) for p, (a, b) in zip(probs, heads)]
    return None, [jnp.concatenate(outs, axis=1) * _silu(z)]


def _up_fn(first, _, rows, halos, params):
    ys, gl, w_up = rows[:4], rows[4], params[0]
    merged = None
    for n in range(4):
        term = _sigmoid(_cols(gl, n * D_MODEL, (n + 1) * D_MODEL)) * _nn16(ys[n], w_up[n])
        merged = term if merged is None else merged + term
    return None, [merged]


def _out_fn(first, _, rows, halos, params):
    x, merged = rows
    w, g = params
    return None, [x + _rms(_nn16(merged, w), g)]


def _loss_fn(first, _, rows, halos, params):
    y, t = rows
    d = y - t
    lrow = 0.5 * jnp.mean(d * d, axis=1, keepdims=True)
    return None, [d * (1.0 / D_MODEL), jnp.broadcast_to(lrow, (y.shape[0], LANES))]


TR = 256
BIG_TR = 512
SWA_TR = 512
DN_TR = 256
UP_TR = 256
UPB_TR = 1024
CONV_HALO = 16
CARRY = (4, LANES, LANES)


def _branch_rows(cols, g):
    hb = CONV_HALO
    a = [Row(cols, 512, O_AQ // 512, hb, g), Row(cols, 512, O_AK // 512, hb, g), Row(cols, 512, O_AV // 512, hb, g),
         Row(cols, 512, O_AZ // 512, 0, g), Row(cols, 256, O_BA // 256)]
    b = [Row(cols, 1024, O_BUV // 1024, 0, g), Row(cols, 512, O_BZ // 512, 0, g)]
    c = [Row(cols, 512, O_CQ // 512, 0, g), Row(cols, 256, O_CKV // 256, LANES), Row(cols, 512, O_CZ // 512, 0, g)]
    m = [Row(cols, 512, O_MQ // 512, 0, g), Row(cols, 512, O_MZ // 512, 0, g)]
    return a, b, c, m


def _layer_fwd(x, mem, W, late_weights=None):
    h = _rows_fwd("prenorm_fwd", _pre_fn, [Row(x, D_MODEL, 0)], [W["norm_pre"]], [(D_MODEL, BF16)], BIG_TR)[0]
    cols = _matmul("in_proj_fwd", h, W["w_pad"], "nt", BF16, 2048, 2048, 1024)
    if late_weights is not None:
        W = dict(W, **late_weights(cols))
    mem_kv = _rows_fwd("memkv_fwd", _memkv_fn, [Row(mem, D_MODEL, 0)], [W["norm_mem"], W["w_mem_kv"]],
                       [(D_MODEL, F32)], MEM_LEN)[0]
    ra, rb, rc, rm = _branch_rows(cols, True)
    y_a, csave = _rows_fwd("dn_fwd", _dn_fn, ra, [W["conv"], W["a_vec"], W["dt_vec"], W["dn_norm"]],
                           [(BRANCH_W, BF16)], DN_TR, CARRY)
    y_b = _rows_fwd("gm_fwd", _gm_fn, rb, [W["gm_norm"], W["spatial_w"], W["spatial_b"]], [(BRANCH_W, BF16)], BIG_TR)[0]
    y_c = _rows_fwd("swa_fwd", _swa_fn, rc, [W["sink_vec"]], [(BRANCH_W, BF16)], SWA_TR)[0]
    y_m = _rows_fwd("mem_fwd", _mem_fn, rm, [mem_kv], [(BRANCH_W, BF16)], BIG_TR)[0]
    ys = [y_a, y_b, y_c, y_m]
    merged = _rows_fwd("up_fwd", _up_fn, [Row(y, BRANCH_W, 0) for y in ys] + [Row(cols, 4 * D_MODEL, 0)],
                       [W["w_up"]], [(D_MODEL, BF16)], UP_TR)[0]
    x_new = _rows_fwd("out_fwd", _out_fn, [Row(x, D_MODEL, 0), Row(merged, D_MODEL, 0)],
                      [W["w_out"], W["norm_post"]], [(D_MODEL, F32)], TR)[0]
    return x_new, dict(x=x, h=h, cols=cols, mem_kv=mem_kv, csave=csave, ys=ys, merged=merged), W


def _layer_bwd(dxn, mem, W, sv, on_weight_grads=None):
    x, cols = sv["x"], sv["cols"]
    (dx_res, dm), (dw_out, dnorm_post), _ = _rows_bwd(
        "out_bwd", _out_fn, [Row(x, D_MODEL, 0), Row(sv["merged"], D_MODEL, 0)], [W["w_out"], W["norm_post"]],
        [dxn], TR)
    dys, dcols, dw_up = _up_bwd(jnp.concatenate(sv["ys"], axis=1), cols, dm, W["w_up"])
    dys = [Row(dys, BRANCH_W, n) for n in range(4)]
    ra, rb, rc, rm = _branch_rows(cols, "cols")
    (dba,), (dconv, da_vec, ddt_vec, ddn_norm), dcols = _rows_bwd(
        "dn_bwd", _dn_fn, ra, [W["conv"], W["a_vec"], W["dt_vec"], W["dn_norm"]], [dys[0]], DN_TR, CARRY,
        sv["csave"], dcols=dcols)
    _, (dgm_norm, dws, dbs), dcols = _rows_bwd(
        "gm_bwd", _gm_fn, rb, [W["gm_norm"], W["spatial_w"], W["spatial_b"]], [dys[1]], BIG_TR, dcols=dcols)
    (dkv_c,), (dsink,), dcols = _rows_bwd("swa_bwd", _swa_fn, rc, [W["sink_vec"]], [dys[2]], SWA_TR, dcols=dcols)
    _, (dmem_kv,), dcols = _rows_bwd("mem_bwd", _mem_fn, rm, [sv["mem_kv"]], [dys[3]], BIG_TR, dcols=dcols)
    dcols = _fill_misc(dcols, dkv_c, dba, BIG_TR)
    _, (dnorm_mem, dw_mem_kv), _ = _rows_bwd("memkv_bwd", _memkv_fn, [Row(mem, D_MODEL, 0, 0, False)],
                                             [W["norm_mem"], W["w_mem_kv"]], [dmem_kv], MEM_LEN)
    dw_pad = _matmul("in_proj_dw", dcols, sv["h"], "tn", BF16, 1024, 1024, 2048)
    grads = dict(norm_post=dnorm_post, norm_mem=dnorm_mem, w_pad=dw_pad, conv=dconv,
                 a_vec=da_vec, dt_vec=ddt_vec, dn_norm=ddn_norm, gm_norm=dgm_norm, spatial_w=dws, spatial_b=dbs,
                 sink_vec=dsink, w_mem_kv=dw_mem_kv, w_up=dw_up, w_out=dw_out)
    started = None if on_weight_grads is None else on_weight_grads(grads)
    dh = _matmul("in_proj_dx", dcols, W["w_pad"], "nn", F32, 1024, 1024, 2048, after=started)
    (dx,), (grads["norm_pre"],), _ = _rows_bwd("prenorm_bwd", _pre_fn_res, [Row(x, D_MODEL, 0)], [W["norm_pre"]],
                                               [dh, dx_res], BIG_TR)
    return dx, grads


def _lane_vec(v, off):
    return jnp.zeros((1, LANES), F32).at[0, off:off + v.shape[0]].set(v)


def _layer_weights(l, w_pad, conv_w, small, **late):
    return dict(
        late, w_pad=w_pad, conv=conv_w.reshape(4, 3, BRANCH_W).reshape(12, 1, BRANCH_W),
        norm_pre=small["norm_pre"][l][None], norm_post=small["norm_post"][l][None],
        norm_mem=small["norm_mem"][l][None],
        a_vec=_lane_vec(small["a_log"][l], 4), dt_vec=_lane_vec(small["dt_bias"][l], 4),
        dn_norm=small["dn_norm"][l][None], gm_norm=small["gm_norm"][l][None],
        spatial_w=small["spatial_w"][l], spatial_b=small["spatial_b"][l][:, None, :],
        sink_vec=_lane_vec(small["sinks"][l], 0))


_MESH = pl.DeviceIdType.MESH
_ANY = pl.BlockSpec(memory_space=pl.ANY)


def _position():
    return lax.axis_index("x"), lax.axis_index("y"), lax.axis_index("c")


def _remote(src, dst, send_sem, recv_sem, dev):
    return pltpu.make_async_remote_copy(src_ref=src, dst_ref=dst, send_sem=send_sem, recv_sem=recv_sem,
                                        device_id=dev, device_id_type=_MESH)


def _hbm_call(name, body, arrs, out_shapes, sems, aliases=None):
    return pl.pallas_call(
        body, name=name, in_specs=[_ANY] * len(arrs), out_specs=[_ANY] * len(out_shapes), out_shape=out_shapes,
        scratch_shapes=[pltpu.SemaphoreType.DMA((k,)) for k in sems], input_output_aliases=aliases or {},
        compiler_params=pltpu.CompilerParams(has_side_effects=True),
    )(*arrs)


def _other_chips(x, y):
    return [(1 - x, y), (x, 1 - y), (1 - x, 1 - y)]


def _gather_weights(arrs, relayed):
    n = len(arrs)

    def body(*refs):
        ins, outs = refs[:n], refs[n:2 * n]
        ici_send, ici_recv, d2d_send, d2d_recv = refs[2 * n:]
        x, y, c = _position()
        me = 2 * x + y
        xn, yn, dg = _other_chips(x, y)
        chip = lambda p: 2 * p[0] + p[1]
        sends = []

        def go(cp):
            cp.start()
            sends.append(cp)

        def ici(a, j, src, dst, to):
            return _remote(src, dst, ici_send.at[4 * a + j], ici_recv.at[4 * a + j], (*to, c))

        for a in range(n):
            go(ici(a, 0, ins[a].at[c], outs[a].at[c, me], xn))
            go(ici(a, 1, ins[a].at[c], outs[a].at[c, me], yn))
            if not relayed[a]:
                go(ici(a, 2, ins[a].at[c], outs[a].at[c, me], dg))
        for a in range(n):
            h = arrs[a].shape[1] // 2
            from_x, from_y = outs[a].at[c, chip(xn)], outs[a].at[c, chip(yn)]
            ici(a, 0, ins[a].at[c], from_x, xn).wait_recv()
            if relayed[a]:
                go(ici(a, 2, from_x.at[pl.ds(0, h)], from_x.at[pl.ds(0, h)], yn))
            ici(a, 1, ins[a].at[c], from_y, yn).wait_recv()
            if relayed[a]:
                go(ici(a, 3, from_y.at[pl.ds(h, h)], from_y.at[pl.ds(h, h)], xn))
            for j, slab in enumerate((from_x, from_y)):
                go(_remote(slab, slab, d2d_send.at[3 * a + j], d2d_recv.at[3 * a + j], (x, y, 1 - c)))
        for a in range(n):
            h = arrs[a].shape[1] // 2
            from_d = outs[a].at[c, chip(dg)]
            if relayed[a]:
                ici(a, 2, from_d.at[pl.ds(0, h)], from_d.at[pl.ds(0, h)], yn).wait_recv()
                ici(a, 3, from_d.at[pl.ds(h, h)], from_d.at[pl.ds(h, h)], xn).wait_recv()
            else:
                ici(a, 2, ins[a].at[c], from_d, dg).wait_recv()
            go(_remote(from_d, from_d, d2d_send.at[3 * a + 2], d2d_recv.at[3 * a + 2], (x, y, 1 - c)))
        for a in range(n):
            for j, p in enumerate((xn, yn, dg)):
                slab = outs[a].at[1 - c, chip(p)]
                _remote(slab, slab, d2d_send.at[3 * a + j], d2d_recv.at[3 * a + j], (x, y, 1 - c)).wait_recv()
        for cp in sends:
            cp.wait_send()

    return _hbm_call("gather_weights", body, arrs,
                     [jax.ShapeDtypeStruct((N_LAYER, N_CHIP) + a.shape[1:], a.dtype) for a in arrs],
                     [4 * n, 4 * n, 3 * n, 3 * n])


def _pair_share(arrs):
    n = len(arrs)

    def body(*refs):
        ins, outs = refs[:n], refs[n:2 * n]
        send_sems, recv_sems = refs[2 * n:]
        x, y, c = _position()
        cps = [_remote(ins[a].at[c], outs[a].at[c], send_sems.at[a], recv_sems.at[a], (x, y, 1 - c)) for a in range(n)]
        for cp in cps:
            cp.start()
        for a in range(n):
            _remote(ins[a].at[c], outs[a].at[1 - c], send_sems.at[a], recv_sems.at[a], (x, y, 1 - c)).wait_recv()
        for cp in cps:
            cp.wait_send()

    return _hbm_call("pair_share", body, arrs, [jax.ShapeDtypeStruct(a.shape, a.dtype) for a in arrs], [n, n],
                     {a: a for a in range(n)})


def _pair_forward(arrs):
    n = len(arrs)

    def body(*refs):
        ins, outs = refs[:n], refs[n:2 * n]
        send_sems, recv_sems = refs[2 * n:]
        x, y, c = _position()
        sends = []
        for a in range(n):
            for j, (px, py) in enumerate(_other_chips(x, y)):
                sends.append(_remote(ins[a].at[c, 2 * px + py], outs[a].at[c, 2 * px + py], send_sems.at[3 * a + j],
                                     recv_sems.at[3 * a + j], (x, y, 1 - c)))
                sends[-1].start()
        for a in range(n):
            for j, (px, py) in enumerate(_other_chips(x, y)):
                slab = outs[a].at[1 - c, 2 * px + py]
                _remote(slab, slab, send_sems.at[3 * a + j], recv_sems.at[3 * a + j], (x, y, 1 - c)).wait_recv()
        for cp in sends:
            cp.wait_send()

    return _hbm_call("pair_forward", body, arrs, [jax.ShapeDtypeStruct(a.shape, a.dtype) for a in arrs],
                     [3 * n, 3 * n], {a: a for a in range(n)})


_HBM = pl.BlockSpec(memory_space=pltpu.HBM)
_SEM = pl.BlockSpec(memory_space=pltpu.SEMAPHORE)
_EFFECT = pltpu.SideEffectType.DATAFLOW_SIDE_EFFECTING


def _chip_copies(kind, srcs, lands, send_sems, recv_sems):
    x, y, c = _position()
    me = 2 * x + y
    sends, recvs = [], []
    for a in range(len(srcs)):
        for j, (px, py) in enumerate(_other_chips(x, y)):
            s, sems, dev = 2 * px + py, (send_sems.at[3 * a + j], recv_sems.at[3 * a + j]), (px, py, c)
            if kind == "gather":
                sends.append(_remote(srcs[a].at[c], lands[a].at[c, me], *sems, dev))
                recvs.append(_remote(srcs[a].at[c], lands[a].at[c, s], *sems, dev))
            else:
                sends.append(_remote(srcs[a].at[s], lands[a].at[me], *sems, dev))
                recvs.append(_remote(srcs[a].at[me], lands[a].at[s], *sems, dev))
    return sends, recvs


def _split_start(name, kind, srcs, land_shapes, after):
    n = len(srcs)

    def body(*refs):
        sends, _ = _chip_copies(kind, refs[:n], refs[n:2 * n], refs[2 * n + 1], refs[2 * n + 2])
        for cp in sends:
            cp.start()
        refs[-1][...] = jnp.zeros_like(refs[-1])

    hbm = lambda a: pltpu.with_memory_space_constraint(a, pltpu.HBM)
    lands = [lax.empty(s.shape, s.dtype) for s in land_shapes]
    outs = pl.pallas_call(
        body, name=name, in_specs=[_HBM] * (2 * n) + [_ANY],
        out_specs=[_SEM, _SEM] + [_HBM] * (2 * n) + [pl.BlockSpec(memory_space=pltpu.VMEM)],
        out_shape=[pltpu.SemaphoreType.DMA((3 * n,)), pltpu.SemaphoreType.DMA((3 * n,))]
        + [pltpu.HBM(a.shape, a.dtype) for a in list(srcs) + lands] + [jax.ShapeDtypeStruct((8, LANES), F32)],
        input_output_aliases={i: 2 + i for i in range(2 * n)},
        compiler_params=pltpu.CompilerParams(has_side_effects=_EFFECT),
    )(*[hbm(a) for a in srcs], *[hbm(a) for a in lands], after)
    return outs[0], outs[1], list(outs[2:2 + 2 * n]), outs[-1]


def _split_wait(name, kind, started, after):
    send_sems, recv_sems, thru, _ = started
    n = len(thru) // 2

    def body(*refs):
        sends, recvs = _chip_copies(kind, refs[:n], refs[n:2 * n], refs[2 * n], refs[2 * n + 1])
        for cp in sends:
            cp.wait_send()
        for cp in recvs:
            cp.wait_recv()

    outs = pl.pallas_call(
        body, name=name, in_specs=[_HBM] * (2 * n) + [_SEM, _SEM, _ANY], out_specs=[_HBM] * (2 * n),
        out_shape=[pltpu.HBM(a.shape, a.dtype) for a in thru], input_output_aliases={i: i for i in range(2 * n)},
        compiler_params=pltpu.CompilerParams(has_side_effects=_EFFECT),
    )(*thru, send_sems, recv_sems, after)
    return list(outs[:n]), list(outs[n:])


def _allreduce_small(g, share):
    n = len(share)

    def body(*refs):
        g_ref, o_ref, share_out = refs[0], refs[1 + n], refs[2 + n:2 + 2 * n]
        pair_buf, chip_buf, send_sems, recv_sems, share_send, share_recv = refs[2 + 2 * n:]
        x, y, c = _position()
        me = 2 * x + y
        sib = (x, y, 1 - c)
        to_sib = _remote(g_ref.at[1 - c], pair_buf, send_sems.at[0], recv_sems.at[0], sib)
        to_sib.start()
        sends = [to_sib]
        for a in range(n):
            sends.append(_remote(refs[1 + a].at[c], share_out[a].at[c], share_send.at[a], share_recv.at[a], sib))
            sends[-1].start()
        to_sib.wait_recv()
        chip_buf[me] = g_ref[c] + pair_buf[...]
        chips = _other_chips(x, y)
        for j, (px, py) in enumerate(chips):
            sends.append(_remote(chip_buf.at[me], chip_buf.at[me], send_sems.at[1 + j], recv_sems.at[1 + j], (px, py, c)))
            sends[-1].start()
        for j, (px, py) in enumerate(chips):
            _remote(chip_buf.at[me], chip_buf.at[2 * px + py], send_sems.at[1 + j], recv_sems.at[1 + j],
                    (px, py, c)).wait_recv()
        o_ref[c] = ((chip_buf[0] + chip_buf[1]) + chip_buf[2]) + chip_buf[3]
        sends.append(_remote(o_ref.at[c], o_ref.at[c], send_sems.at[4], recv_sems.at[4], sib))
        sends[-1].start()
        _remote(o_ref.at[c], o_ref.at[1 - c], send_sems.at[4], recv_sems.at[4], sib).wait_recv()
        for a in range(n):
            _remote(refs[1 + a].at[c], share_out[a].at[1 - c], share_send.at[a], share_recv.at[a], sib).wait_recv()
        for cp in sends:
            cp.wait_send()

    vmem = pl.BlockSpec(memory_space=pltpu.VMEM)
    res = pl.pallas_call(
        body, name="allreduce_small", in_specs=[vmem] + [_ANY] * n, out_specs=[vmem] + [_ANY] * n,
        out_shape=[jax.ShapeDtypeStruct(g.shape, F32)] + [jax.ShapeDtypeStruct(a.shape, a.dtype) for a in share],
        scratch_shapes=[pltpu.VMEM(g.shape[1:], F32), pltpu.VMEM((N_CHIP,) + g.shape[1:], F32),
                        pltpu.SemaphoreType.DMA((5,)), pltpu.SemaphoreType.DMA((5,)),
                        pltpu.SemaphoreType.DMA((n,)), pltpu.SemaphoreType.DMA((n,))],
        input_output_aliases={1 + a: 1 + a for a in range(n)}, compiler_params=_params(),
    )(g, *share)
    return res[0], list(res[1:])


EW_ROWS = 512


def _ew(name, fn, ins, n_out, out_dtype=F32, out_slot=None, into=None):
    def dims(a):
        return a[0].shape[1:] if isinstance(a, tuple) else a.shape

    R, w = dims(ins[0])
    tr = EW_ROWS if R % EW_ROWS == 0 else R
    n_into = len(into) if into else 0

    def body(c_ref, *refs):
        outs = fn(*[r[...] for r in refs[:len(ins)]])
        for r, v in zip(refs[len(ins) + n_into:], outs):
            r[...] = v.astype(r.dtype)

    def lead_spec(l):
        if l == "c":
            return pl.BlockSpec((None, tr, w), lambda i, c_ref: (c_ref[0], i, 0))
        return pl.BlockSpec((None, tr, w), lambda i, c_ref, s=l: (s, i, 0))

    plain = pl.BlockSpec((tr, w), lambda i, c_ref: (i, 0))
    in_specs = [lead_spec(a[1]) if isinstance(a, tuple) else plain for a in ins] + [_ANY] * n_into
    out_spec = plain if out_slot is None else lead_spec(out_slot)
    out_shape = jax.ShapeDtypeStruct((R, w) if out_slot is None else (2, R, w), out_dtype)
    return pl.pallas_call(
        body, name=name,
        grid_spec=pltpu.PrefetchScalarGridSpec(num_scalar_prefetch=1, grid=(R // tr,), in_specs=in_specs,
                                               out_specs=[out_spec] * n_out),
        out_shape=[out_shape] * n_out, input_output_aliases={1 + len(ins) + j: j for j in range(n_into)},
        compiler_params=_params(dimension_semantics=("arbitrary",)),
    )(lax.axis_index("c").astype(jnp.int32).reshape(1), *[a[0] if isinstance(a, tuple) else a for a in ins],
      *(into or []))


def _adamw_fn(w, g, m, v):
    m = ADAM_B1 * m + (1.0 - ADAM_B1) * g
    v = ADAM_B2 * v + (1.0 - ADAM_B2) * (g * g)
    m_hat = m / (1.0 - ADAM_B1 ** ADAM_STEP)
    v_hat = v / (1.0 - ADAM_B2 ** ADAM_STEP)
    delta = -ADAM_LR * (m_hat / (jnp.sqrt(v_hat) + ADAM_EPS) + ADAM_WD * w)
    return delta, m, v


def _adamw_layer(name, l, w, g, m, v, into):
    k = w.shape[-1]
    three = lambda a: (a.reshape(N_LAYER, -1, k), l)
    fn = lambda w_, g_, m_, v_: _adamw_fn(w_, g_, m_, v_) + (g_,)
    outs = _ew(name, fn, [three(w), g.reshape(-1, k), three(m), three(v)], 4, out_slot=l,
               into=None if into is None else [a.reshape(N_LAYER, -1, k) for a in into])
    return [o.reshape(w.shape) for o in outs]


def _adamw_rows(name, l, w, g, m, v, into):
    _, R, k = w.shape
    n_into = len(into) if into else 0

    def body(*refs):
        w_ref, g_ref, m_ref, v_ref = refs[:4]
        d_out, m_out, v_out, g_out = refs[4 + n_into:]
        g_blk = g_ref[...]
        d_out[...], m_out[...], v_out[...] = _adamw_fn(w_ref[...], g_blk, m_ref[...], v_ref[...])
        g_out[...] = g_blk

    spec = pl.BlockSpec((None, EW_ROWS, k), lambda i: (l, i, 0))
    return pl.pallas_call(
        body, name=name, grid=(-(-R // EW_ROWS),),
        in_specs=[spec, pl.BlockSpec((EW_ROWS, k), lambda i: (i, 0)), spec, spec] + [_ANY] * n_into,
        out_specs=[spec] * 4, out_shape=[jax.ShapeDtypeStruct((N_LAYER, R, k), F32)] * 4,
        input_output_aliases={4 + j: j for j in range(n_into)},
        compiler_params=_params(dimension_semantics=("arbitrary",)),
    )(w, g, m, v, *(into or []))


_SMALL = [("norm_pre", (2, 1024)), ("norm_post", (2, 1024)), ("norm_mem", (2, 1024)), ("a_log", (2, 4)),
          ("dt_bias", (2, 4)), ("dn_norm", (2, 128)), ("gm_norm", (2, 512)), ("spatial_w", (2, 4, 128, 128)),
          ("spatial_b", (2, 4, 128)), ("sinks", (2, 8)), ("loss", (2, 1))]
_SMALL_ROWS = 208
_BIG = ["w_in", "conv_w", "w_mem_kv", "w_up", "w_out"]
_NAMES = ["norm_pre", "norm_post", "norm_mem", "w_in", "conv_w", "a_log", "dt_bias", "dn_norm", "gm_norm",
          "spatial_w", "spatial_b", "sinks", "w_mem_kv", "w_up", "w_out"]


def _size(shape):
    n = 1
    for s in shape:
        n *= s
    return n


def _pack_small(d):
    rows = []
    for n, shp in _SMALL:
        a = d[n].reshape(N_LAYER, -1)
        rows.append(a.reshape(-1, 1024) if a.shape[1] > 1024 else jnp.pad(a, ((0, 6), (0, 1024 - a.shape[1]))))
    assert sum(r.shape[0] for r in rows) == _SMALL_ROWS
    return jnp.concatenate(rows, axis=0)


def _unpack_small(p):
    out, off = {}, 0
    for n, shp in _SMALL:
        c = _size(shp) // N_LAYER
        k = 8 if c <= 1024 else _size(shp) // 1024
        out[n] = (p[off:off + N_LAYER, :c] if c <= 1024 else p[off:off + k]).reshape(shp)
        off += k
    return out


_HALF_SHAPE = {"w_in": (SHARD_PAD // 2, D_MODEL), "conv_w": (2, 3 * BRANCH_W // N_CHIP), "w_mem_kv": (128, D_MODEL),
               "w_up": (2, BRANCH_W, D_MODEL // N_CHIP), "w_out": (128, D_MODEL)}


def _chip_major(g):
    g = jnp.swapaxes(g, 0, 1)
    return g.reshape((N_CHIP, 2 * g.shape[2]) + g.shape[3:])


def _half_major(g):
    g = g.reshape((N_CHIP, 2, g.shape[1] // 2) + g.shape[2:])
    return jnp.swapaxes(g, 0, 1).astype(BF16)


N_EARLY = 2


def _early_views(l, g_in, g_conv, small):
    return _layer_weights(l, _w_pad_from_slabs(g_in),
                          _chip_major(g_conv).transpose(1, 0, 2).reshape(4, 3 * BRANCH_W), small)


def _late_views(g_kv, g_up, g_out):
    return dict(w_mem_kv=_chip_major(g_kv).reshape(D_MODEL, D_MODEL),
                w_up=_chip_major(g_up).transpose(1, 2, 0, 3).reshape(4, BRANCH_W, D_MODEL),
                w_out=_chip_major(g_out).reshape(D_MODEL, D_MODEL))


def _pair_sums(g):
    add2 = lambda a, b: [a.astype(F32) + b.astype(F32)]
    rest = [_half_major(g["conv"].reshape(4, N_CHIP, 3 * BRANCH_W // N_CHIP).transpose(1, 0, 2)),
            _half_major(g["w_mem_kv"].reshape(N_CHIP, D_MODEL // N_CHIP, D_MODEL)),
            _half_major(g["w_up"].reshape(4, BRANCH_W, N_CHIP, D_MODEL // N_CHIP).transpose(2, 0, 1, 3)),
            _half_major(g["w_out"].reshape(N_CHIP, D_MODEL // N_CHIP, D_MODEL))]
    mine, theirs, rest_theirs = _slabs_from_pad(g["w_pad"], rest)
    pair = [_ew("pair_sum_w_in", add2, [mine.reshape(-1, D_MODEL), theirs.reshape(-1, D_MODEL)], 1, BF16)[0]
            .reshape(mine.shape)]
    for n, b, p in zip(_BIG[1:], rest, rest_theirs):
        k = b.shape[-1]
        pair.append(_ew("pair_sum_" + n, add2, [(b.reshape(2, -1, k), "c"), p.reshape(-1, k)], 1, BF16)[0]
                    .reshape(p.shape))
    return pair


def _chip_sums(landed, pair, me):
    add4 = lambda a, b, c_, d: [((a.astype(F32) + b.astype(F32)) + c_.astype(F32)) + d.astype(F32)]
    totals = []
    for n, r, q in zip(_BIG, landed, pair):
        r = _own_slot(r, lax.dynamic_index_in_dim(q, me, 0), me, 0)
        k = r.shape[-1]
        totals.append(_ew("chip_sum_" + n, add4, [(r.reshape(N_CHIP, -1, k), s) for s in range(N_CHIP)], 1,
                          out_slot="c")[0].reshape((2,) + r.shape[1:]))
    return totals


def _own_slot(buf, mine, me, axis):
    return lax.dynamic_update_index_in_dim(buf, mine.astype(buf.dtype), me, axis)


def kernel(x, mem, norm_pre, norm_post, norm_mem, w_in, conv_w, a_log, dt_bias, dn_norm, gm_norm, spatial_w, spatial_b, sinks, w_mem_kv, w_up, w_out, loss_target, m_norm_pre, m_norm_post, m_norm_mem, m_w_in, m_conv_w, m_a_log, m_dt_bias, m_dn_norm, m_gm_norm, m_spatial_w, m_spatial_b, m_sinks, m_w_mem_kv, m_w_up, m_w_out, v_norm_pre, v_norm_post, v_norm_mem, v_w_in, v_conv_w, v_a_log, v_dt_bias, v_dn_norm, v_gm_norm, v_spatial_w, v_spatial_b, v_sinks, v_w_mem_kv, v_w_up, v_w_out):
    w = dict(norm_pre=norm_pre, norm_post=norm_post, norm_mem=norm_mem, w_in=w_in, conv_w=conv_w, a_log=a_log,
             dt_bias=dt_bias, dn_norm=dn_norm, gm_norm=gm_norm, spatial_w=spatial_w, spatial_b=spatial_b, sinks=sinks,
             w_mem_kv=w_mem_kv, w_up=w_up, w_out=w_out)
    m = dict(norm_pre=m_norm_pre, norm_post=m_norm_post, norm_mem=m_norm_mem, w_in=m_w_in, conv_w=m_conv_w,
             a_log=m_a_log, dt_bias=m_dt_bias, dn_norm=m_dn_norm, gm_norm=m_gm_norm, spatial_w=m_spatial_w,
             spatial_b=m_spatial_b, sinks=m_sinks, w_mem_kv=m_w_mem_kv, w_up=m_w_up, w_out=m_w_out)
    v = dict(norm_pre=v_norm_pre, norm_post=v_norm_post, norm_mem=v_norm_mem, w_in=v_w_in, conv_w=v_conv_w,
             a_log=v_a_log, dt_bias=v_dt_bias, dn_norm=v_dn_norm, gm_norm=v_gm_norm, spatial_w=v_spatial_w,
             spatial_b=v_spatial_b, sinks=v_sinks, w_mem_kv=v_w_mem_kv, w_up=v_w_up, w_out=v_w_out)
    me = 2 * lax.axis_index("x") + lax.axis_index("y")

    tr = lambda a: a.transpose(0, 2, 1)
    w_t = tr(w_in)
    w_in_t = jnp.pad(w_t.astype(BF16), ((0, 0), (0, SHARD_PAD - SHARD_IN), (0, 0)))
    for d in (w, m, v):
        d["loss"] = jnp.zeros((N_LAYER, 1), F32)
    local = dict(w_in=w_in_t, conv_w=conv_w, w_mem_kv=w_mem_kv.astype(BF16), w_up=w_up.astype(BF16),
                 w_out=w_out.astype(BF16))
    halves = lambda l: [local[n][l].reshape((2,) + _HALF_SHAPE[n]) for n in _BIG]
    own = lambda gathered, mine: [_own_slot(g, h[:, None], me, 1) for g, h in zip(gathered, mine)]
    lands = [jax.ShapeDtypeStruct((2, N_CHIP) + _HALF_SHAPE[n], local[n].dtype) for n in _BIG]
    h0 = halves(0)
    g0 = own(_gather_weights(h0[:N_EARLY], [True, False]), h0[:N_EARLY])
    rest0 = _split_start("gather_l0_rest_start", "gather", h0[N_EARLY:], lands[N_EARLY:], g0[1])
    started = _split_start("gather_l1_start", "gather", halves(1), lands, rest0[3])

    xl, meml = x[0], mem[0]
    W0 = _early_views(0, g0[0], g0[1], w)
    W0["norm_pre"] = W0["norm_pre"] + started[3][0, 0]

    def late0(cols):
        mine, landed = _split_wait("gather_l0_rest_wait", "gather", rest0, cols)
        return _late_views(*own(_pair_forward(landed), mine))

    x1, sv0, W0 = _layer_fwd(xl, meml, W0, late0)
    mine1, landed1 = _split_wait("gather_l1_wait", "gather", started, x1)
    g1 = own(_pair_forward(landed1), mine1)
    W1 = dict(_early_views(1, g1[0], g1[1], w), **_late_views(*g1[N_EARLY:]))
    x2, sv1, _ = _layer_fwd(x1, meml, W1)
    dy, lrows = _rows_fwd("loss", _loss_fn, [Row(x2, D_MODEL, 0), Row(loss_target[0], D_MODEL, 0)], [],
                          [(D_MODEL, F32), (LANES, F32)], BIG_TR)
    loss_local = jnp.sum(lrows[:, 0])

    scattering = {}

    def start_scatter(l):
        def on_weight_grads(g):
            pair = _pair_sums(g)
            scattering[l] = _split_start("scatter_l%d_start" % l, "scatter", pair,
                                         [jax.ShapeDtypeStruct(p.shape, p.dtype) for p in pair], pair[1])
            return scattering[l][3]
        return on_weight_grads

    dx1, grads1 = _layer_bwd(dy, meml, W1, sv1, start_scatter(1))
    dx, grads0 = _layer_bwd(dx1, meml, W0, sv0, start_scatter(0))
    pair1, landed1 = _split_wait("scatter_l1_wait", "scatter", scattering[1], dx)
    after_start = scattering[0][3][0, 0]
    grads = [grads0, grads1]

    small_local = dict(
        norm_pre=jnp.stack([g["norm_pre"][0] for g in grads]), norm_post=jnp.stack([g["norm_post"][0] for g in grads]),
        norm_mem=jnp.stack([g["norm_mem"][0] for g in grads]), a_log=jnp.stack([g["a_vec"][0, 4:8] for g in grads]),
        dt_bias=jnp.stack([g["dt_vec"][0, 4:8] for g in grads]), dn_norm=jnp.stack([g["dn_norm"][0] for g in grads]),
        gm_norm=jnp.stack([g["gm_norm"][0] for g in grads]), spatial_w=jnp.stack([g["spatial_w"] for g in grads]),
        spatial_b=jnp.stack([g["spatial_b"][:, 0, :] for g in grads]),
        sinks=jnp.stack([g["sink_vec"][0, :8] for g in grads]),
        loss=jnp.stack([loss_local, jnp.zeros((), F32)]).reshape(N_LAYER, 1))
    packed = _pack_small(small_local) + after_start
    landed1[1] = landed1[1] + after_start.astype(landed1[1].dtype)
    summed, totals1 = _allreduce_small(packed.reshape(2, -1, 1024), _chip_sums(landed1, pair1, me))
    gsmall_packed = summed.reshape(-1, 1024)

    d_s, m_s, v_s = _ew("adamw_small", _adamw_fn, [_pack_small(w), gsmall_packed, _pack_small(m), _pack_small(v)], 3)
    gsmall, dsmall, msmall, vsmall = (_unpack_small(p) for p in (gsmall_packed, d_s, m_s, v_s))
    g_o, d_o, m_o, v_o = dict(gsmall), dict(dsmall), dict(msmall), dict(vsmall)
    loss = gsmall["loss"][0, 0]
    m_t, v_t = tr(m["w_in"]), tr(v["w_in"])

    def update(l, totals, into):
        outs = {}
        for n, t in zip(_BIG, totals):
            g_l = t.reshape(local[n].shape[1:])
            if n == "w_in":
                outs[n] = _adamw_rows("adamw_" + n, l, w_t, g_l, m_t, v_t, into and into[n])
            else:
                outs[n] = _adamw_layer("adamw_" + n, l, w[n], g_l, m[n], v[n], into and into[n])
        return outs

    outs1 = update(1, totals1, None)
    pair0, landed0 = _split_wait("scatter_l0_wait", "scatter", scattering[0], outs1["w_in"][0])
    outs = update(0, _pair_share(_chip_sums(landed0, pair0, me)), outs1)
    for n in _BIG:
        d_o[n], m_o[n], v_o[n], g_o[n] = [tr(o) for o in outs[n]] if n == "w_in" else outs[n]
    return (loss, dx[None], *[g_o[n] for n in _NAMES], *[d_o[n] for n in _NAMES], *[m_o[n] for n in _NAMES],
            *[v_o[n] for n in _NAMES])
```

```python
import collections
import functools

import jax
import jax.numpy as jnp
from jax import lax
from jax.experimental import pallas as pl
from jax.experimental.pallas import tpu as pltpu

F32 = jnp.float32
BF16 = jnp.bfloat16

D_MODEL = 1024
BRANCH_W = 512
MEM_LEN = 256
N_LAYER = 2
N_CHIP = 4
EPS = 1e-6
NEG_INF = -1e30
DN_CHUNK = 64
SW_HD = 64
LANES = 128
VMEM_LIMIT = 48 * 1024 * 1024

ADAM_LR, ADAM_B1, ADAM_B2, ADAM_EPS, ADAM_WD, ADAM_STEP = 0.001, 0.9, 0.999, 1e-08, 0.01, 10

N_PAD = 10240
O_GATE = 0
O_AQ, O_AK, O_AV, O_AZ = 4096, 4608, 5120, 5632
O_BUV, O_BZ = 6144, 7168
O_CKV, O_BA = 7680, 7936
O_CQ, O_CZ = 8192, 8704
O_MQ, O_MZ = 9216, 9728
O_MISC, W_MISC = O_CKV, 512
_PAD_SEGS = [(5896, 4096), (0, 512), (512, 512), (1024, 512), (1536, 512), (2056, 1024), (3080, 512),
             (4104, 128), (4232, 128), (2048, 8), (None, 120), (None, 128),
             (3592, 512), (4360, 512), (4872, 512), (5384, 512)]
D_IN = 9992
SHARD_IN = D_IN // N_CHIP


SHARD_PAD = 2560


def _pad_parts():
    parts, off = [], 0
    for s, n in _PAD_SEGS:
        a = s
        while s is not None and a < s + n:
            chip = a // SHARD_IN
            b = min(s + n, (chip + 1) * SHARD_IN)
            parts.append((chip, a - chip * SHARD_IN, off + a - s, b - a))
            a = b
        off += n
    return parts


PERM_ROWS = 1024
PERM_SLACK = 32


def _permute_rows(name, src, parts, n_out, out_dtype, pair_split=False, exchange=()):
    B, Z = PERM_ROWS, PERM_ROWS + PERM_SLACK
    w = src.shape[1]
    plans = []
    for blk in range(n_out // B):
        o, runs = blk * B, []
        for s, d, n in parts:
            lo, hi = max(d, o), min(d + n, o + B)
            if lo < hi:
                s0 = s + lo - d
                wa = s0 // 16 * 16
                wb = min(-(-(s0 + hi - lo) // 16) * 16, src.shape[0])
                runs.append((wa, wb - wa, s0 - (lo - o) - wa, lo - o, hi - o))
        plans.append(runs)
    max_runs = max(len(r) for r in plans)
    nblk = len(plans)

    per_half = nblk // 2

    n_ex = len(exchange)

    def body(*refs):
        src_ref, out_ref = refs[0], refs[1 + n_ex]
        if pair_split:
            theirs_ref, ex_out = refs[2 + n_ex], refs[3 + n_ex:3 + 2 * n_ex]
            inbuf, obuf, insem, outsem, to_sib_sem, from_sib_sem, ex_send, ex_recv = refs[3 + 2 * n_ex:]
            x, y, c = _position()
            sibling = (x, y, 1 - c)
            ex_copies = [_remote(refs[1 + a].at[1 - c], ex_out[a], ex_send.at[a], ex_recv.at[a], sibling)
                         for a in range(n_ex)]
            for cp in ex_copies:
                cp.start()
        else:
            inbuf, obuf, insem, outsem = refs[2:]

        def in_copies(blk):
            return [pltpu.make_async_copy(src_ref.at[pl.ds(wa, ws)], inbuf.at[blk % 2, r, pl.ds(0, ws)],
                                          insem.at[blk % 2, r]) for r, (wa, ws, _, _, _) in enumerate(plans[blk])]

        class out_copy:
            def __init__(self, blk):
                self.blk, self.rows = blk, pl.ds((blk % per_half if pair_split else blk) * B, B)
                self.local = pltpu.make_async_copy(obuf.at[blk % 2], out_ref.at[self.rows], outsem.at[blk % 2])

            def _both(self, local_op, remote_op):
                if not pair_split:
                    return local_op(self.local)
                mine = c == self.blk // per_half
                pl.when(mine)(lambda: local_op(self.local))
                pl.when(jnp.logical_not(mine))(lambda: remote_op(_remote(
                    obuf.at[self.blk % 2], theirs_ref.at[self.rows], to_sib_sem.at[self.blk % 2],
                    from_sib_sem.at[self.blk % per_half], sibling)))

            def start(self):
                self._both(lambda cp: cp.start(), lambda cp: cp.start())

            def wait(self):
                self._both(lambda cp: cp.wait(), lambda cp: cp.wait_send())

        for cp in in_copies(0):
            cp.start()
        rid = _iota((B, 1), 0)
        for blk in range(nblk):
            if blk + 1 < nblk:
                for cp in in_copies(blk + 1):
                    cp.start()
            for cp in in_copies(blk):
                cp.wait()
            val = jnp.zeros((B, w), F32)
            for r, (wa, ws, t, l0, l1) in enumerate(plans[blk]):
                win = jnp.concatenate([inbuf[blk % 2, r, pl.ds(0, ws)].astype(F32), jnp.zeros((Z - ws, w), F32)], axis=0)
                moved = pltpu.roll(win, (-t) % Z, 0)[:B]
                val = jnp.where((rid >= l0) & (rid < l1), moved, val)
            if blk >= 2:
                out_copy(blk - 2).wait()
            obuf[blk % 2] = val.astype(out_dtype)
            out_copy(blk).start()
        for blk in range(max(nblk - 2, 0), nblk):
            out_copy(blk).wait()
        if pair_split:
            for i in range(per_half):
                rows = theirs_ref.at[pl.ds(i * B, B)]
                _remote(rows, rows, to_sib_sem.at[0], from_sib_sem.at[i], sibling).wait_recv()
            for cp in ex_copies:
                cp.wait_recv()
            for cp in ex_copies:
                cp.wait_send()

    scratch = [pltpu.VMEM((2, max_runs, Z, w), src.dtype), pltpu.VMEM((2, B, w), out_dtype),
               pltpu.SemaphoreType.DMA((2, max_runs)), pltpu.SemaphoreType.DMA((2,))]
    if not pair_split:
        return pl.pallas_call(
            body, name=name, in_specs=[_ANY], out_specs=_ANY, out_shape=jax.ShapeDtypeStruct((n_out, w), out_dtype),
            scratch_shapes=scratch, compiler_params=_params(),
        )(src)
    scratch += [pltpu.SemaphoreType.DMA((2,)), pltpu.SemaphoreType.DMA((per_half,)),
                pltpu.SemaphoreType.DMA((max(n_ex, 1),)), pltpu.SemaphoreType.DMA((max(n_ex, 1),))]
    out_shape = [jax.ShapeDtypeStruct((n_out // 2, w), out_dtype)] * 2
    out_shape += [jax.ShapeDtypeStruct(a.shape[1:], a.dtype) for a in exchange]
    res = pl.pallas_call(
        body, name=name, in_specs=[_ANY] * (1 + n_ex), out_specs=[_ANY] * (2 + n_ex), out_shape=out_shape,
        scratch_shapes=scratch, compiler_params=_params(),
    )(src, *exchange)
    return res[0], res[1], list(res[2:])


def _slab_parts():
    h, out = SHARD_PAD // 2, []
    for chip, s, d, n in _pad_parts():
        a = s
        while a < s + n:
            half = a // h
            b = min(s + n, (half + 1) * h)
            out.append(((half * N_CHIP + chip) * h + a - half * h, d + a - s, b - a))
            a = b
    return out


def _w_pad_from_slabs(slabs):
    return _permute_rows("w_pad_rows", slabs.reshape(-1, slabs.shape[-1]), _slab_parts(), N_PAD, BF16)


def _slabs_from_pad(dw, exchange=()):
    mine, theirs, exchanged = _permute_rows("w_pad_grad_rows", dw, [(d, s, n) for s, d, n in _slab_parts()],
                                            N_CHIP * SHARD_PAD, BF16, pair_split=True, exchange=exchange)
    shape = (N_CHIP, SHARD_PAD // 2, dw.shape[1])
    return mine.reshape(shape), theirs.reshape(shape), exchanged


def _dot(a, b, dims, prec):
    if prec == "bf16":
        return lax.dot_general(a.astype(BF16), b.astype(BF16), (dims, ((), ())), preferred_element_type=F32)
    return lax.dot_general(a, b, (dims, ((), ())), precision=lax.Precision.HIGHEST, preferred_element_type=F32)


_NN, _NT, _TN = ((1,), (0,)), ((1,), (1,)), ((0,), (0,))


def _make_mm(prec):
    @jax.custom_vjp
    def nn(a, b):
        return _dot(a, b, _NN, prec)

    @jax.custom_vjp
    def nt(a, b):
        return _dot(a, b, _NT, prec)

    @jax.custom_vjp
    def tn(a, b):
        return _dot(a, b, _TN, prec)

    nn.defvjp(lambda a, b: (nn(a, b), (a, b)), lambda r, g: (nt(g, r[1]), tn(r[0], g)))
    nt.defvjp(lambda a, b: (nt(a, b), (a, b)), lambda r, g: (nn(g, r[1]), tn(g, r[0])))
    tn.defvjp(lambda a, b: (tn(a, b), (a, b)), lambda r, g: (nt(r[1], g), nn(r[0], g)))
    return nn, nt, tn


_nn16, _nt16, _tn16 = _make_mm("bf16")
_nn32 = _make_mm("f32")[0]


def _make_slice(axis):
    @functools.partial(jax.custom_vjp, nondiff_argnums=(1, 2, 3))
    def sl(x, a, b, n):
        return x[a:b] if axis == 0 else x[:, a:b]

    def fwd(x, a, b, n):
        return sl(x, a, b, n), None

    def bwd(a, b, n, _, g):
        parts = []
        if a > 0:
            parts.append(jnp.zeros((a, g.shape[1]) if axis == 0 else (g.shape[0], a), g.dtype))
        parts.append(g)
        if n - b > 0:
            parts.append(jnp.zeros((n - b, g.shape[1]) if axis == 0 else (g.shape[0], n - b), g.dtype))
        return (jnp.concatenate(parts, axis=axis),)

    sl.defvjp(fwd, bwd)
    return sl


_sl0, _sl1 = _make_slice(0), _make_slice(1)


def _rowsl(x, a, b):
    return _sl0(x, a, b, x.shape[0])


def _cols(x, a, b):
    return _sl1(x, a, b, x.shape[1])


@functools.partial(jax.custom_vjp, nondiff_argnums=(1,))
def _rollr(x, s):
    return pltpu.roll(x, s, 0)


_rollr.defvjp(lambda x, s: (_rollr(x, s), None),
              lambda s, _, g: (pltpu.roll(g, g.shape[0] - s, 0),))


def _iota(shape, axis):
    return lax.broadcasted_iota(jnp.int32, shape, axis)


def _sigmoid(x):
    return lax.logistic(x)


def _silu(x):
    return x * _sigmoid(x)


def _gelu(x):
    return 0.5 * x * (1.0 + jnp.tanh(0.7978845608028654 * (x + 0.044715 * (x * x * x))))


def _softplus(x):
    return jnp.maximum(x, 0.0) + jnp.log(1.0 + jnp.exp(-jnp.abs(x)))


def _rms(x, g):
    return x * lax.rsqrt(jnp.mean(x * x, axis=-1, keepdims=True) + EPS) * g


def _lane_pick(x, lane):
    return jnp.sum(x * (_iota((1, x.shape[1]), 1) == lane).astype(F32), axis=1, keepdims=True)


Row = collections.namedtuple("Row", "arr w cb hb grad", defaults=(0, True))


def _full_spec(shape):
    return pl.BlockSpec(shape, lambda i, _n=len(shape): (0,) * _n)


def _load_params(refs):
    return [[p[g].astype(F32) for g in range(p.shape[0])] if len(p.shape) == 3 else p[...].astype(F32)
            for p in refs]


def _params(**kw):
    return pltpu.CompilerParams(vmem_limit_bytes=VMEM_LIMIT, **kw)


def _rows_fwd(name, fn, rows, params, outs, tr, carry=None):
    T = rows[0].arr.shape[0]
    n = T // tr
    halos = [r for r in rows if r.hb]
    nr, nh, npar, no = len(rows), len(halos), len(params), len(outs)

    def body(*refs):
        row_refs, halo_refs = refs[:nr], refs[nr:nr + nh]
        par_refs = refs[nr + nh:nr + nh + npar]
        out_refs = refs[nr + nh + npar:nr + nh + npar + no]
        rest = refs[nr + nh + npar + no:]
        first = pl.program_id(0) == 0
        cvals = None
        if carry is not None:
            csave_ref, carry_ref = rest

            @pl.when(first)
            def _():
                carry_ref[...] = jnp.zeros_like(carry_ref)

            cvals = [carry_ref[g] for g in range(carry[0])]
            for g in range(carry[0]):
                csave_ref[0, g] = cvals[g]
        c_out, o = fn(first, cvals, [r[...].astype(F32) for r in row_refs],
                      [h[...].astype(F32) for h in halo_refs], _load_params(par_refs))
        for r, v in zip(out_refs, o):
            r[...] = v.astype(r.dtype)
        if carry is not None:
            for g in range(carry[0]):
                carry_ref[g] = c_out[g]

    in_specs = [pl.BlockSpec((tr, r.w), lambda i, c=r.cb: (i, c)) for r in rows]
    in_specs += [pl.BlockSpec((r.hb, r.w), lambda i, c=r.cb, q=tr // r.hb: (jnp.maximum(i * q - 1, 0), c))
                 for r in halos]
    in_specs += [_full_spec(p.shape) for p in params]
    out_shape = [jax.ShapeDtypeStruct((T, w), dt) for w, dt in outs]
    out_specs = [pl.BlockSpec((tr, w), lambda i: (i, 0)) for w, _ in outs]
    scratch = []
    if carry is not None:
        out_shape.append(jax.ShapeDtypeStruct((n,) + carry, F32))
        out_specs.append(pl.BlockSpec((1,) + carry, lambda i: (i, 0, 0, 0)))
        scratch.append(pltpu.VMEM(carry, F32))
    return pl.pallas_call(
        body, name=name, grid=(n,), in_specs=in_specs, out_specs=out_specs, out_shape=out_shape,
        scratch_shapes=scratch, compiler_params=_params(dimension_semantics=("arbitrary",)),
    )(*[r.arr for r in rows], *[r.arr for r in halos], *params)


def _rows_bwd(name, fn, rows, params, douts, tr, carry=None, csave=None, dcols=None):
    T = rows[0].arr.shape[0]
    n = T // tr
    halos = [r for r in rows if r.hb]
    grows = [r for r in rows if r.grad is True]
    crows = [r for r in rows if r.grad == "cols"]
    wcols = sum(r.w for r in crows)
    nr, nh, npar, nd, ng = len(rows), len(halos), len(params), len(douts), len(grows)
    nc = 0 if carry is None else 1
    ncol = 1 if crows else 0
    nalias = 1 if (crows and dcols is not None) else 0

    def body(*refs):
        row_refs, halo_refs = refs[:nr], refs[nr:nr + nh]
        par_refs = refs[nr + nh:nr + nh + npar]
        k = nr + nh + npar
        csave_ref = refs[k] if nc else None
        dout_refs = refs[k + nc:k + nc + nd]
        k = k + nc + nd + nalias
        drow_refs = refs[k:k + ng]
        dcols_ref = refs[k + ng] if ncol else None
        dpar_refs = refs[k + ng + ncol:k + ng + ncol + npar]
        k = k + ng + ncol + npar
        dcarry_ref = refs[k] if nc else None
        hgrad_refs = refs[k + nc:]
        i = pl.program_id(0)
        first_tile = i == n - 1

        @pl.when(i == 0)
        def _():
            for r in dpar_refs:
                r[...] = jnp.zeros_like(r)
            for r in hgrad_refs:
                r[...] = jnp.zeros_like(r)
            if nc:
                dcarry_ref[...] = jnp.zeros_like(dcarry_ref)

        rv = [r[...].astype(F32) for r in row_refs]
        hv = [h[...].astype(F32) for h in halo_refs]
        pv = _load_params(par_refs)
        dov = [d[...].astype(F32) for d in dout_refs]
        if nc:
            cv = [csave_ref[0, g] for g in range(carry[0])]
            _, vjp = jax.vjp(lambda c, r, h, p: fn(first_tile, c, r, h, p), cv, rv, hv, pv)
            dc, dr, dh, dp = vjp(([dcarry_ref[g] for g in range(carry[0])], dov))
            for g in range(carry[0]):
                dcarry_ref[g] = dc[g]
        else:
            _, vjp = jax.vjp(lambda r, h, p: fn(first_tile, None, r, h, p)[1], rv, hv, pv)
            dr, dh, dp = vjp(dov)
        gi = hi = 0
        pieces = []
        for kk, r in enumerate(rows):
            d = dr[kk]
            if r.hb:
                carried = hgrad_refs[hi][...]
                d = d + (carried if tr == r.hb else
                         jnp.concatenate([jnp.zeros((tr - r.hb, r.w), F32), carried], axis=0))
                hgrad_refs[hi][...] = dh[hi]
                hi += 1
            if r.grad is True:
                drow_refs[gi][...] = d.astype(drow_refs[gi].dtype)
                gi += 1
            elif r.grad == "cols":
                pieces.append(d.astype(BF16))
        if ncol:
            dcols_ref[...] = pieces[0] if len(pieces) == 1 else jnp.concatenate(pieces, axis=1)
        for r, d in zip(dpar_refs, dp):
            if len(r.shape) == 3:
                for g in range(r.shape[0]):
                    r[g] += d[g]
            else:
                r[...] += d

    rev = lambda i: n - 1 - i
    in_specs = [pl.BlockSpec((tr, r.w), lambda i, c=r.cb: (rev(i), c)) for r in rows]
    in_specs += [pl.BlockSpec((r.hb, r.w), lambda i, c=r.cb, q=tr // r.hb: (jnp.maximum(rev(i) * q - 1, 0), c))
                 for r in halos]
    in_specs += [_full_spec(p.shape) for p in params]
    args = [r.arr for r in rows] + [r.arr for r in halos] + list(params)
    scratch = []
    if nc:
        in_specs.append(pl.BlockSpec((1,) + carry, lambda i: (rev(i), 0, 0, 0)))
        args.append(csave)
        scratch.append(pltpu.VMEM(carry, F32))
    douts = [d if isinstance(d, Row) else Row(d, d.shape[1], 0) for d in douts]
    in_specs += [pl.BlockSpec((tr, d.w), lambda i, c=d.cb: (rev(i), c)) for d in douts]
    args += [d.arr for d in douts]
    aliases = {}
    if nalias:
        aliases = {len(args): ng}
        in_specs.append(pl.BlockSpec(memory_space=pl.ANY))
        args.append(dcols)
    scratch += [pltpu.VMEM((r.hb, r.w), F32) for r in halos]
    out_shape = [jax.ShapeDtypeStruct((T, r.w), F32) for r in grows]
    out_specs = [pl.BlockSpec((tr, r.w), lambda i: (rev(i), 0)) for r in grows]
    if ncol:
        off = crows[0].cb * crows[0].w
        assert off % wcols == 0 and all(a.cb * a.w + a.w == b.cb * b.w for a, b in zip(crows, crows[1:]))
        out_shape.append(jax.ShapeDtypeStruct((T, N_PAD), BF16))
        out_specs.append(pl.BlockSpec((tr, wcols), lambda i, c=off // wcols: (rev(i), c)))
    out_shape += [jax.ShapeDtypeStruct(p.shape, F32) for p in params]
    out_specs += [_full_spec(p.shape) for p in params]
    res = pl.pallas_call(
        body, name=name, grid=(n,), in_specs=in_specs, out_specs=out_specs, out_shape=out_shape,
        scratch_shapes=scratch, input_output_aliases=aliases,
        compiler_params=_params(dimension_semantics=("arbitrary",)),
    )(*args)
    return list(res[:ng]), list(res[ng + ncol:]), (res[ng] if ncol else dcols)


def _fill_misc(dcols, dkv, dba, tr):
    T = dkv.shape[0]

    def body(kv_ref, ba_ref, _, o_ref):
        o_ref[...] = jnp.concatenate([kv_ref[...], ba_ref[...]], axis=1).astype(BF16)

    return pl.pallas_call(
        body, name="misc_bwd", grid=(T // tr,),
        in_specs=[pl.BlockSpec((tr, 256), lambda i: (i, 0)), pl.BlockSpec((tr, 256), lambda i: (i, 0)),
                  pl.BlockSpec(memory_space=pl.ANY)],
        out_specs=pl.BlockSpec((tr, W_MISC), lambda i: (i, O_MISC // W_MISC)),
        out_shape=jax.ShapeDtypeStruct((T, N_PAD), BF16), input_output_aliases={2: 0},
        compiler_params=_params(dimension_semantics=("arbitrary",)),
    )(dkv, dba, dcols)


def _up_bwd(ys, cols, dm, w_up):
    T = dm.shape[0]
    tr = min(UPB_TR, T)

    def body(y_ref, gl_ref, dm_ref, w_ref, dy_ref, dgl_ref, dw_ref):
        @pl.when(pl.program_id(1) == 0)
        def _():
            dw_ref[...] = jnp.zeros_like(dw_ref)

        _, vjp = jax.vjp(lambda y, gl, w: _sigmoid(gl) * _nn16(y, w),
                         y_ref[...].astype(F32), gl_ref[...].astype(F32), w_ref[...].astype(F32))
        dy, dgl, dw = vjp(dm_ref[...])
        dy_ref[...] = dy
        dgl_ref[...] = dgl.astype(BF16)
        dw_ref[...] += dw

    branch_rows = lambda w: pl.BlockSpec((tr, w), lambda n, i: (i, n))
    weight = pl.BlockSpec((None, BRANCH_W, D_MODEL), lambda n, i: (n, 0, 0))
    return pl.pallas_call(
        body, name="up_bwd", grid=(4, T // tr),
        in_specs=[branch_rows(BRANCH_W), branch_rows(D_MODEL), pl.BlockSpec((tr, D_MODEL), lambda n, i: (i, 0)), weight],
        out_specs=[branch_rows(BRANCH_W), branch_rows(D_MODEL), weight],
        out_shape=[jax.ShapeDtypeStruct((T, 4 * BRANCH_W), F32), jax.ShapeDtypeStruct((T, N_PAD), BF16),
                   jax.ShapeDtypeStruct(w_up.shape, F32)],
        compiler_params=_params(dimension_semantics=("arbitrary", "arbitrary")),
    )(ys, cols, dm, w_up)


def _matmul(name, a, b, kind, out_dtype, tm, tn, tk, after=None):
    if kind == "tn":
        (K, M), N = a.shape, b.shape[1]
    else:
        (M, K), N = a.shape, (b.shape[0] if kind == "nt" else b.shape[1])
    tm, tn, tk = min(tm, M), min(tn, N), min(tk, K)
    nk = K // tk
    dims = {"nn": _NN, "nt": _NT, "tn": _TN}[kind]

    n_after = 0 if after is None else 1

    def body(*refs):
        a_ref, b_ref, o_ref, acc = refs[0], refs[1], refs[2 + n_after], refs[3 + n_after:]
        part = lax.dot_general(a_ref[...], b_ref[...], (dims, ((), ())), preferred_element_type=F32)
        if nk == 1:
            o_ref[...] = part.astype(o_ref.dtype)
            return
        acc_ref = acc[0] if acc else o_ref
        k = pl.program_id(2)

        @pl.when(k == 0)
        def _():
            acc_ref[...] = part

        @pl.when(k > 0)
        def _():
            acc_ref[...] += part

        if acc:
            @pl.when(k == nk - 1)
            def _():
                o_ref[...] = acc_ref[...].astype(o_ref.dtype)

    a_spec = pl.BlockSpec((tk, tm), lambda i, j, k: (k, i)) if kind == "tn" else pl.BlockSpec((tm, tk), lambda i, j, k: (i, k))
    b_spec = pl.BlockSpec((tn, tk), lambda i, j, k: (j, k)) if kind == "nt" else pl.BlockSpec((tk, tn), lambda i, j, k: (k, j))
    return pl.pallas_call(
        body, name=name, grid=(M // tm, N // tn, nk), in_specs=[a_spec, b_spec] + [_ANY] * n_after,
        out_specs=pl.BlockSpec((tm, tn), lambda i, j, k: (i, j)),
        out_shape=jax.ShapeDtypeStruct((M, N), out_dtype),
        scratch_shapes=[pltpu.VMEM((tm, tn), F32)] if nk > 1 and out_dtype != F32 else [],
        compiler_params=_params(dimension_semantics=("arbitrary", "arbitrary", "arbitrary")),
    )(a, b, *([] if after is None else [after]))


def _pre_fn(first, _, rows, halos, params):
    return None, [_rms(rows[0], params[0])]


def _pre_fn_res(first, _, rows, halos, params):
    return None, [_rms(rows[0], params[0]), rows[0]]


def _memkv_fn(first, _, rows, halos, params):
    g, w = params
    return None, [_nn16(_rms(rows[0], g), w)]


def _conv_silu(x, halo, w4, keep_halo):
    tr, hb = x.shape[0], halo.shape[0]
    xh = jnp.concatenate([halo * keep_halo, x], axis=0)
    acc = w4[3] * x
    for s in (1, 2, 3):
        acc = acc + w4[3 - s] * _rowsl(_rollr(xh, s), hb, hb + tr)
    return _silu(acc)


def _dn_fn(first, S, rows, halos, params):
    qp, kp, vp, z, ba = rows
    conv, a_vec, dt_vec, dnorm = params
    ba = _cols(ba, 0, LANES)
    tr = qp.shape[0]
    keep = jnp.where(first, 0.0, 1.0)
    q = _conv_silu(qp, halos[0], [conv[3 * j + 0] for j in range(4)], keep)
    k = _conv_silu(kp, halos[1], [conv[3 * j + 1] for j in range(4)], keep)
    v = _conv_silu(vp, halos[2], [conv[3 * j + 2] for j in range(4)], keep)
    qh, kh, vh = [], [], []
    for h in range(4):
        a, b = h * LANES, (h + 1) * LANES
        xq, xk = _cols(q, a, b), _cols(k, a, b)
        qh.append(xq * lax.rsqrt(jnp.sum(xq * xq, axis=1, keepdims=True) + EPS) * (LANES ** -0.5))
        kh.append(xk * lax.rsqrt(jnp.sum(xk * xk, axis=1, keepdims=True) + EPS))
        vh.append(_cols(v, a, b))
    beta_all = _sigmoid(ba)
    g_all = -jnp.exp(a_vec) * _softplus(ba + dt_vec)
    C = DN_CHUNK
    ii, jj = _iota((C, C), 0), _iota((C, C), 1)
    strict, incl = ii > jj, ii >= jj
    eye = (ii == jj).astype(F32)
    last_row = (_iota((C, 1), 0) == C - 1).astype(F32)
    n_chunk = tr // C
    pairs = [(c, h) for c in range(n_chunk) for h in range(4)]
    rows_of = lambda a, c: _rowsl(a, c * C, (c + 1) * C)
    gcs = [_nn32(incl.astype(F32), rows_of(g_all, c)) for c in range(n_chunk)]
    qc = {(c, h): rows_of(qh[h], c) for c, h in pairs}
    kc = {(c, h): rows_of(kh[h], c) for c, h in pairs}
    beta = {(c, h): _lane_pick(rows_of(beta_all, c), h) for c, h in pairs}
    gc = {(c, h): _lane_pick(gcs[c], 4 + h) for c, h in pairs}
    dec = {p: jnp.exp(jnp.where(incl, gc[p] - jnp.sum(eye * gc[p], axis=0, keepdims=True), 0.0)) for p in pairs}
    egc = {p: jnp.exp(gc[p]) for p in pairs}
    kb = {p: kc[p] * beta[p] for p in pairs}
    kq = {p: _nt16(jnp.concatenate([kb[p], qc[p]], axis=0), kc[p]) for p in pairs}
    P = {p: -jnp.where(strict, _rowsl(kq[p], 0, C) * dec[p], 0.0) for p in pairs}
    aqk = {p: jnp.where(incl, _rowsl(kq[p], C, 2 * C) * dec[p], 0.0) for p in pairs}
    tinv = {p: eye + P[p] for p in pairs}
    P = {p: _nn16(P[p], P[p]) for p in pairs}
    for j in range(5):
        if j < 4:
            pt = {p: _nn16(jnp.concatenate([P[p], tinv[p]], axis=0), P[p]) for p in pairs}
            tinv = {p: tinv[p] + _rowsl(pt[p], C, 2 * C) for p in pairs}
            P = {p: _rowsl(pt[p], 0, C) for p in pairs}
        else:
            tinv = {p: tinv[p] + _nn16(tinv[p], P[p]) for p in pairs}
    uw = {(c, h): _nn16(tinv[c, h], jnp.concatenate([rows_of(vh[h], c) * beta[c, h], kb[c, h] * egc[c, h]], axis=1))
          for c, h in pairs}
    S = list(S)
    ychunks = []
    for c in range(n_chunk):
        zc = rows_of(z, c)
        hs = range(4)
        ws = [_nn16(jnp.concatenate([_cols(uw[c, h], LANES, 2 * LANES), qc[c, h] * egc[c, h]], axis=0), S[h]) for h in hs]
        vnew = [_cols(uw[c, h], 0, LANES) - _rowsl(ws[h], 0, C) for h in hs]
        o = [_rowsl(ws[h], C, 2 * C) + _nn16(aqk[c, h], vnew[h]) for h in hs]
        glast = [jnp.sum(gc[c, h] * last_row, axis=0, keepdims=True) for h in hs]
        S = [S[h] * jnp.exp(glast[h]) + _tn16(kc[c, h] * jnp.exp(glast[h] - gc[c, h]), vnew[h]) for h in hs]
        ychunks.append(jnp.concatenate(
            [_rms(o[h], dnorm) * _silu(_cols(zc, h * LANES, (h + 1) * LANES)) for h in hs], axis=1))
    return S, [jnp.concatenate(ychunks, axis=0)]


def _gm_fn(first, _, rows, halos, params):
    uv, z = rows
    gnorm, ws, bs = params
    tr = uv.shape[0]
    guv = _gelu(uv)
    u = _cols(guv, 0, BRANCH_W)
    v = _rms(_cols(guv, BRANCH_W, 2 * BRANCH_W), gnorm)
    ii, jj = _iota((LANES, LANES), 0), _iota((LANES, LANES), 1)
    eye = (ii == jj).astype(F32)
    wsm = [jnp.where(ii >= jj, ws[g], 0.0) for g in range(4)]
    bcol = [jnp.sum(eye * bs[g], axis=1, keepdims=True) for g in range(4)]
    chunks = []
    for c in range(tr // LANES):
        vc = _rowsl(v, c * LANES, (c + 1) * LANES)
        chunks.append(jnp.concatenate(
            [_nn16(wsm[g], _cols(vc, g * LANES, (g + 1) * LANES)) + bcol[g] for g in range(4)], axis=1))
    return None, [u * jnp.concatenate(chunks, axis=0) * _silu(z)]


def _swa_fn(first, _, rows, halos, params):
    q, kvc, z = rows
    sink_vec = params[0]
    P = LANES
    n_blk = q.shape[0] // P
    r, cc = _iota((P, P), 0), _iota((P, P), 1)
    lane = _iota((1, P), 1)
    key = _iota((P, 2 * P), 1)
    dist = _iota((P, 2 * P), 0) + P - key
    in_window = (dist >= 0) & (dist < P)
    valid = [in_window & (key >= jnp.where(first, P, 0))] + [in_window] * (n_blk - 1)
    halves = [(lane < SW_HD).astype(F32), (lane >= SW_HD).astype(F32)]
    dup = [(r == kh * SW_HD + (cc & (SW_HD - 1))).astype(F32) for kh in range(2)]
    kv_blk = [halos[0]] + [_rowsl(kvc, b * P, (b + 1) * P) for b in range(n_blk)]
    pairs = [(b, kh) for b in range(n_blk) for kh in range(2)]
    kkvv = {}
    for b in range(n_blk):
        kv = jnp.concatenate([kv_blk[b], kv_blk[b + 1]], axis=0)
        k_v = jnp.concatenate([_cols(kv, 0, P), _cols(kv, P, 2 * P)], axis=0)
        for kh in range(2):
            kkvv[b, kh] = _nn16(k_v, dup[kh])
    scores = {}
    for b, kh in pairs:
        q_b = _rowsl(q, b * P, (b + 1) * P)
        stacked = jnp.concatenate([_cols(q_b, (2 * kh + g // 2) * P, (2 * kh + g // 2 + 1) * P) * halves[g % 2]
                                   for g in range(4)], axis=0)
        scores[b, kh] = _nt16(stacked, _rowsl(kkvv[b, kh], 0, 2 * P))
    probs = {}
    for b, kh in pairs:
        ps = []
        for g in range(4):
            s = jnp.where(valid[b], _rowsl(scores[b, kh], g * P, (g + 1) * P) * (SW_HD ** -0.5), NEG_INF)
            sink = _lane_pick(sink_vec, kh * 4 + g)
            m = lax.stop_gradient(jnp.maximum(jnp.max(s, axis=1, keepdims=True), sink))
            e = jnp.exp(s - m)
            ps.append(e / (jnp.sum(e, axis=1, keepdims=True) + jnp.exp(sink - m)))
        probs[b, kh] = jnp.concatenate(ps, axis=0)
    outs = {p: _nn16(probs[p], _rowsl(kkvv[p], 2 * P, 4 * P)) for p in pairs}
    tile = [jnp.concatenate([_rowsl(outs[b, j // 2], (2 * (j % 2)) * P, (2 * (j % 2) + 1) * P) * halves[0]
                             + _rowsl(outs[b, j // 2], (2 * (j % 2) + 1) * P, (2 * (j % 2) + 2) * P) * halves[1]
                             for j in range(4)], axis=1) for b in range(n_blk)]
    return None, [jnp.concatenate(tile, axis=0) * _silu(z)]


def _mem_fn(first, _, rows, halos, params):
    q, z = rows
    mkv = params[0]
    heads = [(h * LANES, (h + 1) * LANES) for h in range(4)]
    scores = [_nt16(_cols(q, a, b), _cols(mkv, a, b)) * (LANES ** -0.5) for a, b in heads]
    probs = []
    for s in scores:
        e = jnp.exp(s - lax.stop_gradient(jnp.max(s, axis=1, keepdims=True)))
        probs.append(e / jnp.sum(e, axis=1, keepdims=True))
    outs = [_nn16(p, _cols(mkv, BRANCH_W + a, BRANCH_W + b)) for p, (a, b) in zip(probs, heads)]
    return None, [jnp.concatenate(outs, axis=1) * _silu(z)]


def _up_fn(first, _, rows, halos, params):
    ys, gl, w_up = rows[:4], rows[4], params[0]
    merged = None
    for n in range(4):
        term = _sigmoid(_cols(gl, n * D_MODEL, (n + 1) * D_MODEL)) * _nn16(ys[n], w_up[n])
        merged = term if merged is None else merged + term
    return None, [merged]


def _out_fn(first, _, rows, halos, params):
    x, merged = rows
    w, g = params
    return None, [x + _rms(_nn16(merged, w), g)]


def _loss_fn(first, _, rows, halos, params):
    y, t = rows
    d = y - t
    lrow = 0.5 * jnp.mean(d * d, axis=1, keepdims=True)
    return None, [d * (1.0 / D_MODEL), jnp.broadcast_to(lrow, (y.shape[0], LANES))]


TR = 256
BIG_TR = 512
SWA_TR = 512
DN_TR = 128
UP_TR = 256
UPB_TR = 1024
CONV_HALO = 16
CARRY = (4, LANES, LANES)


def _branch_rows(cols, g):
    hb = CONV_HALO
    a = [Row(cols, 512, O_AQ // 512, hb, g), Row(cols, 512, O_AK // 512, hb, g), Row(cols, 512, O_AV // 512, hb, g),
         Row(cols, 512, O_AZ // 512, 0, g), Row(cols, 256, O_BA // 256)]
    b = [Row(cols, 1024, O_BUV // 1024, 0, g), Row(cols, 512, O_BZ // 512, 0, g)]
    c = [Row(cols, 512, O_CQ // 512, 0, g), Row(cols, 256, O_CKV // 256, LANES), Row(cols, 512, O_CZ // 512, 0, g)]
    m = [Row(cols, 512, O_MQ // 512, 0, g), Row(cols, 512, O_MZ // 512, 0, g)]
    return a, b, c, m


def _layer_fwd(x, mem, W, late_weights=None):
    h = _rows_fwd("prenorm_fwd", _pre_fn, [Row(x, D_MODEL, 0)], [W["norm_pre"]], [(D_MODEL, BF16)], BIG_TR)[0]
    cols = _matmul("in_proj_fwd", h, W["w_pad"], "nt", BF16, 2048, 2048, 1024)
    if late_weights is not None:
        W = dict(W, **late_weights(cols))
    mem_kv = _rows_fwd("memkv_fwd", _memkv_fn, [Row(mem, D_MODEL, 0)], [W["norm_mem"], W["w_mem_kv"]],
                       [(D_MODEL, F32)], MEM_LEN)[0]
    ra, rb, rc, rm = _branch_rows(cols, True)
    y_a, csave = _rows_fwd("dn_fwd", _dn_fn, ra, [W["conv"], W["a_vec"], W["dt_vec"], W["dn_norm"]],
                           [(BRANCH_W, BF16)], DN_TR, CARRY)
    y_b = _rows_fwd("gm_fwd", _gm_fn, rb, [W["gm_norm"], W["spatial_w"], W["spatial_b"]], [(BRANCH_W, BF16)], BIG_TR)[0]
    y_c = _rows_fwd("swa_fwd", _swa_fn, rc, [W["sink_vec"]], [(BRANCH_W, BF16)], SWA_TR)[0]
    y_m = _rows_fwd("mem_fwd", _mem_fn, rm, [mem_kv], [(BRANCH_W, BF16)], BIG_TR)[0]
    ys = [y_a, y_b, y_c, y_m]
    merged = _rows_fwd("up_fwd", _up_fn, [Row(y, BRANCH_W, 0) for y in ys] + [Row(cols, 4 * D_MODEL, 0)],
                       [W["w_up"]], [(D_MODEL, BF16)], UP_TR)[0]
    x_new = _rows_fwd("out_fwd", _out_fn, [Row(x, D_MODEL, 0), Row(merged, D_MODEL, 0)],
                      [W["w_out"], W["norm_post"]], [(D_MODEL, F32)], TR)[0]
    return x_new, dict(x=x, h=h, cols=cols, mem_kv=mem_kv, csave=csave, ys=ys, merged=merged), W


def _layer_bwd(dxn, mem, W, sv, on_weight_grads=None):
    x, cols = sv["x"], sv["cols"]
    (dx_res, dm), (dw_out, dnorm_post), _ = _rows_bwd(
        "out_bwd", _out_fn, [Row(x, D_MODEL, 0), Row(sv["merged"], D_MODEL, 0)], [W["w_out"], W["norm_post"]],
        [dxn], TR)
    dys, dcols, dw_up = _up_bwd(jnp.concatenate(sv["ys"], axis=1), cols, dm, W["w_up"])
    dys = [Row(dys, BRANCH_W, n) for n in range(4)]
    ra, rb, rc, rm = _branch_rows(cols, "cols")
    (dba,), (dconv, da_vec, ddt_vec, ddn_norm), dcols = _rows_bwd(
        "dn_bwd", _dn_fn, ra, [W["conv"], W["a_vec"], W["dt_vec"], W["dn_norm"]], [dys[0]], DN_TR, CARRY,
        sv["csave"], dcols=dcols)
    _, (dgm_norm, dws, dbs), dcols = _rows_bwd(
        "gm_bwd", _gm_fn, rb, [W["gm_norm"], W["spatial_w"], W["spatial_b"]], [dys[1]], BIG_TR, dcols=dcols)
    (dkv_c,), (dsink,), dcols = _rows_bwd("swa_bwd", _swa_fn, rc, [W["sink_vec"]], [dys[2]], SWA_TR, dcols=dcols)
    _, (dmem_kv,), dcols = _rows_bwd("mem_bwd", _mem_fn, rm, [sv["mem_kv"]], [dys[3]], BIG_TR, dcols=dcols)
    dcols = _fill_misc(dcols, dkv_c, dba, BIG_TR)
    _, (dnorm_mem, dw_mem_kv), _ = _rows_bwd("memkv_bwd", _memkv_fn, [Row(mem, D_MODEL, 0, 0, False)],
                                             [W["norm_mem"], W["w_mem_kv"]], [dmem_kv], MEM_LEN)
    dw_pad = _matmul("in_proj_dw", dcols, sv["h"], "tn", BF16, 1024, 1024, 2048)
    grads = dict(norm_post=dnorm_post, norm_mem=dnorm_mem, w_pad=dw_pad, conv=dconv,
                 a_vec=da_vec, dt_vec=ddt_vec, dn_norm=ddn_norm, gm_norm=dgm_norm, spatial_w=dws, spatial_b=dbs,
                 sink_vec=dsink, w_mem_kv=dw_mem_kv, w_up=dw_up, w_out=dw_out)
    started = None if on_weight_grads is None else on_weight_grads(grads)
    dh = _matmul("in_proj_dx", dcols, W["w_pad"], "nn", F32, 1024, 1024, 2048, after=started)
    (dx,), (grads["norm_pre"],), _ = _rows_bwd("prenorm_bwd", _pre_fn_res, [Row(x, D_MODEL, 0)], [W["norm_pre"]],
                                               [dh, dx_res], BIG_TR)
    return dx, grads


def _lane_vec(v, off):
    return jnp.zeros((1, LANES), F32).at[0, off:off + v.shape[0]].set(v)


def _layer_weights(l, w_pad, conv_w, small, **late):
    return dict(
        late, w_pad=w_pad, conv=conv_w.reshape(4, 3, BRANCH_W).reshape(12, 1, BRANCH_W),
        norm_pre=small["norm_pre"][l][None], norm_post=small["norm_post"][l][None],
        norm_mem=small["norm_mem"][l][None],
        a_vec=_lane_vec(small["a_log"][l], 4), dt_vec=_lane_vec(small["dt_bias"][l], 4),
        dn_norm=small["dn_norm"][l][None], gm_norm=small["gm_norm"][l][None],
        spatial_w=small["spatial_w"][l], spatial_b=small["spatial_b"][l][:, None, :],
        sink_vec=_lane_vec(small["sinks"][l], 0))


_MESH = pl.DeviceIdType.MESH
_ANY = pl.BlockSpec(memory_space=pl.ANY)


def _position():
    return lax.axis_index("x"), lax.axis_index("y"), lax.axis_index("c")


def _remote(src, dst, send_sem, recv_sem, dev):
    return pltpu.make_async_remote_copy(src_ref=src, dst_ref=dst, send_sem=send_sem, recv_sem=recv_sem,
                                        device_id=dev, device_id_type=_MESH)


def _hbm_call(name, body, arrs, out_shapes, sems, aliases=None):
    return pl.pallas_call(
        body, name=name, in_specs=[_ANY] * len(arrs), out_specs=[_ANY] * len(out_shapes), out_shape=out_shapes,
        scratch_shapes=[pltpu.SemaphoreType.DMA((k,)) for k in sems], input_output_aliases=aliases or {},
        compiler_params=pltpu.CompilerParams(has_side_effects=True),
    )(*arrs)


def _other_chips(x, y):
    return [(1 - x, y), (x, 1 - y), (1 - x, 1 - y)]


def _gather_weights(arrs, relayed):
    n = len(arrs)

    def body(*refs):
        ins, outs = refs[:n], refs[n:2 * n]
        ici_send, ici_recv, d2d_send, d2d_recv = refs[2 * n:]
        x, y, c = _position()
        me = 2 * x + y
        xn, yn, dg = _other_chips(x, y)
        chip = lambda p: 2 * p[0] + p[1]
        sends = []

        def go(cp):
            cp.start()
            sends.append(cp)

        def ici(a, j, src, dst, to):
            return _remote(src, dst, ici_send.at[4 * a + j], ici_recv.at[4 * a + j], (*to, c))

        for a in range(n):
            go(ici(a, 0, ins[a].at[c], outs[a].at[c, me], xn))
            go(ici(a, 1, ins[a].at[c], outs[a].at[c, me], yn))
            if not relayed[a]:
                go(ici(a, 2, ins[a].at[c], outs[a].at[c, me], dg))
        for a in range(n):
            h = arrs[a].shape[1] // 2
            from_x, from_y = outs[a].at[c, chip(xn)], outs[a].at[c, chip(yn)]
            ici(a, 0, ins[a].at[c], from_x, xn).wait_recv()
            if relayed[a]:
                go(ici(a, 2, from_x.at[pl.ds(0, h)], from_x.at[pl.ds(0, h)], yn))
            ici(a, 1, ins[a].at[c], from_y, yn).wait_recv()
            if relayed[a]:
                go(ici(a, 3, from_y.at[pl.ds(h, h)], from_y.at[pl.ds(h, h)], xn))
            for j, slab in enumerate((from_x, from_y)):
                go(_remote(slab, slab, d2d_send.at[3 * a + j], d2d_recv.at[3 * a + j], (x, y, 1 - c)))
        for a in range(n):
            h = arrs[a].shape[1] // 2
            from_d = outs[a].at[c, chip(dg)]
            if relayed[a]:
                ici(a, 2, from_d.at[pl.ds(0, h)], from_d.at[pl.ds(0, h)], yn).wait_recv()
                ici(a, 3, from_d.at[pl.ds(h, h)], from_d.at[pl.ds(h, h)], xn).wait_recv()
            else:
                ici(a, 2, ins[a].at[c], from_d, dg).wait_recv()
            go(_remote(from_d, from_d, d2d_send.at[3 * a + 2], d2d_recv.at[3 * a + 2], (x, y, 1 - c)))
        for a in range(n):
            for j, p in enumerate((xn, yn, dg)):
                slab = outs[a].at[1 - c, chip(p)]
                _remote(slab, slab, d2d_send.at[3 * a + j], d2d_recv.at[3 * a + j], (x, y, 1 - c)).wait_recv()
        for cp in sends:
            cp.wait_send()

    return _hbm_call("gather_weights", body, arrs,
                     [jax.ShapeDtypeStruct((N_LAYER, N_CHIP) + a.shape[1:], a.dtype) for a in arrs],
                     [4 * n, 4 * n, 3 * n, 3 * n])


def _pair_share(arrs):
    n = len(arrs)

    def body(*refs):
        ins, outs = refs[:n], refs[n:2 * n]
        send_sems, recv_sems = refs[2 * n:]
        x, y, c = _position()
        cps = [_remote(ins[a].at[c], outs[a].at[c], send_sems.at[a], recv_sems.at[a], (x, y, 1 - c)) for a in range(n)]
        for cp in cps:
            cp.start()
        for a in range(n):
            _remote(ins[a].at[c], outs[a].at[1 - c], send_sems.at[a], recv_sems.at[a], (x, y, 1 - c)).wait_recv()
        for cp in cps:
            cp.wait_send()

    return _hbm_call("pair_share", body, arrs, [jax.ShapeDtypeStruct(a.shape, a.dtype) for a in arrs], [n, n],
                     {a: a for a in range(n)})


def _pair_forward(arrs):
    n = len(arrs)

    def body(*refs):
        ins, outs = refs[:n], refs[n:2 * n]
        send_sems, recv_sems = refs[2 * n:]
        x, y, c = _position()
        sends = []
        for a in range(n):
            for j, (px, py) in enumerate(_other_chips(x, y)):
                sends.append(_remote(ins[a].at[c, 2 * px + py], outs[a].at[c, 2 * px + py], send_sems.at[3 * a + j],
                                     recv_sems.at[3 * a + j], (x, y, 1 - c)))
                sends[-1].start()
        for a in range(n):
            for j, (px, py) in enumerate(_other_chips(x, y)):
                slab = outs[a].at[1 - c, 2 * px + py]
                _remote(slab, slab, send_sems.at[3 * a + j], recv_sems.at[3 * a + j], (x, y, 1 - c)).wait_recv()
        for cp in sends:
            cp.wait_send()

    return _hbm_call("pair_forward", body, arrs, [jax.ShapeDtypeStruct(a.shape, a.dtype) for a in arrs],
                     [3 * n, 3 * n], {a: a for a in range(n)})


_HBM = pl.BlockSpec(memory_space=pltpu.HBM)
_SEM = pl.BlockSpec(memory_space=pltpu.SEMAPHORE)
_EFFECT = pltpu.SideEffectType.DATAFLOW_SIDE_EFFECTING


def _chip_copies(kind, srcs, lands, send_sems, recv_sems):
    x, y, c = _position()
    me = 2 * x + y
    sends, recvs = [], []
    for a in range(len(srcs)):
        for j, (px, py) in enumerate(_other_chips(x, y)):
            s, sems, dev = 2 * px + py, (send_sems.at[3 * a + j], recv_sems.at[3 * a + j]), (px, py, c)
            if kind == "gather":
                sends.append(_remote(srcs[a].at[c], lands[a].at[c, me], *sems, dev))
                recvs.append(_remote(srcs[a].at[c], lands[a].at[c, s], *sems, dev))
            else:
                sends.append(_remote(srcs[a].at[s], lands[a].at[me], *sems, dev))
                recvs.append(_remote(srcs[a].at[me], lands[a].at[s], *sems, dev))
    return sends, recvs


def _split_start(name, kind, srcs, land_shapes, after):
    n = len(srcs)

    def body(*refs):
        sends, _ = _chip_copies(kind, refs[:n], refs[n:2 * n], refs[2 * n + 1], refs[2 * n + 2])
        for cp in sends:
            cp.start()
        refs[-1][...] = jnp.zeros_like(refs[-1])

    hbm = lambda a: pltpu.with_memory_space_constraint(a, pltpu.HBM)
    lands = [lax.empty(s.shape, s.dtype) for s in land_shapes]
    outs = pl.pallas_call(
        body, name=name, in_specs=[_HBM] * (2 * n) + [_ANY],
        out_specs=[_SEM, _SEM] + [_HBM] * (2 * n) + [pl.BlockSpec(memory_space=pltpu.VMEM)],
        out_shape=[pltpu.SemaphoreType.DMA((3 * n,)), pltpu.SemaphoreType.DMA((3 * n,))]
        + [pltpu.HBM(a.shape, a.dtype) for a in list(srcs) + lands] + [jax.ShapeDtypeStruct((8, LANES), F32)],
        input_output_aliases={i: 2 + i for i in range(2 * n)},
        compiler_params=pltpu.CompilerParams(has_side_effects=_EFFECT),
    )(*[hbm(a) for a in srcs], *[hbm(a) for a in lands], after)
    return outs[0], outs[1], list(outs[2:2 + 2 * n]), outs[-1]


def _split_wait(name, kind, started, after):
    send_sems, recv_sems, thru, _ = started
    n = len(thru) // 2

    def body(*refs):
        sends, recvs = _chip_copies(kind, refs[:n], refs[n:2 * n], refs[2 * n], refs[2 * n + 1])
        for cp in sends:
            cp.wait_send()
        for cp in recvs:
            cp.wait_recv()

    outs = pl.pallas_call(
        body, name=name, in_specs=[_HBM] * (2 * n) + [_SEM, _SEM, _ANY], out_specs=[_HBM] * (2 * n),
        out_shape=[pltpu.HBM(a.shape, a.dtype) for a in thru], input_output_aliases={i: i for i in range(2 * n)},
        compiler_params=pltpu.CompilerParams(has_side_effects=_EFFECT),
    )(*thru, send_sems, recv_sems, after)
    return list(outs[:n]), list(outs[n:])


def _allreduce_small(g):
    def body(g_ref, o_ref, pair_buf, chip_buf, send_sems, recv_sems):
        x, y, c = _position()
        me = 2 * x + y
        sib = (x, y, 1 - c)
        to_sib = _remote(g_ref.at[1 - c], pair_buf, send_sems.at[0], recv_sems.at[0], sib)
        to_sib.start()
        to_sib.wait_recv()
        chip_buf[me] = g_ref[c] + pair_buf[...]
        sends = [to_sib]
        chips = _other_chips(x, y)
        for j, (px, py) in enumerate(chips):
            sends.append(_remote(chip_buf.at[me], chip_buf.at[me], send_sems.at[1 + j], recv_sems.at[1 + j], (px, py, c)))
            sends[-1].start()
        for j, (px, py) in enumerate(chips):
            _remote(chip_buf.at[me], chip_buf.at[2 * px + py], send_sems.at[1 + j], recv_sems.at[1 + j],
                    (px, py, c)).wait_recv()
        o_ref[c] = ((chip_buf[0] + chip_buf[1]) + chip_buf[2]) + chip_buf[3]
        sends.append(_remote(o_ref.at[c], o_ref.at[c], send_sems.at[4], recv_sems.at[4], sib))
        sends[-1].start()
        _remote(o_ref.at[c], o_ref.at[1 - c], send_sems.at[4], recv_sems.at[4], sib).wait_recv()
        for cp in sends:
            cp.wait_send()

    vmem = pl.BlockSpec(memory_space=pltpu.VMEM)
    return pl.pallas_call(
        body, name="allreduce_small", in_specs=[vmem], out_specs=vmem, out_shape=jax.ShapeDtypeStruct(g.shape, F32),
        scratch_shapes=[pltpu.VMEM(g.shape[1:], F32), pltpu.VMEM((N_CHIP,) + g.shape[1:], F32),
                        pltpu.SemaphoreType.DMA((5,)), pltpu.SemaphoreType.DMA((5,))],
        compiler_params=_params(),
    )(g)


EW_ROWS = 512


def _ew(name, fn, ins, n_out, out_dtype=F32, out_slot=None, into=None):
    def dims(a):
        return a[0].shape[1:] if isinstance(a, tuple) else a.shape

    R, w = dims(ins[0])
    tr = EW_ROWS if R % EW_ROWS == 0 else R
    n_into = len(into) if into else 0

    def body(c_ref, *refs):
        outs = fn(*[r[...] for r in refs[:len(ins)]])
        for r, v in zip(refs[len(ins) + n_into:], outs):
            r[...] = v.astype(r.dtype)

    def lead_spec(l):
        if l == "c":
            return pl.BlockSpec((None, tr, w), lambda i, c_ref: (c_ref[0], i, 0))
        return pl.BlockSpec((None, tr, w), lambda i, c_ref, s=l: (s, i, 0))

    plain = pl.BlockSpec((tr, w), lambda i, c_ref: (i, 0))
    in_specs = [lead_spec(a[1]) if isinstance(a, tuple) else plain for a in ins] + [_ANY] * n_into
    out_spec = plain if out_slot is None else lead_spec(out_slot)
    out_shape = jax.ShapeDtypeStruct((R, w) if out_slot is None else (2, R, w), out_dtype)
    return pl.pallas_call(
        body, name=name,
        grid_spec=pltpu.PrefetchScalarGridSpec(num_scalar_prefetch=1, grid=(R // tr,), in_specs=in_specs,
                                               out_specs=[out_spec] * n_out),
        out_shape=[out_shape] * n_out, input_output_aliases={1 + len(ins) + j: j for j in range(n_into)},
        compiler_params=_params(dimension_semantics=("arbitrary",)),
    )(lax.axis_index("c").astype(jnp.int32).reshape(1), *[a[0] if isinstance(a, tuple) else a for a in ins],
      *(into or []))


def _adamw_fn(w, g, m, v):
    m = ADAM_B1 * m + (1.0 - ADAM_B1) * g
    v = ADAM_B2 * v + (1.0 - ADAM_B2) * (g * g)
    m_hat = m / (1.0 - ADAM_B1 ** ADAM_STEP)
    v_hat = v / (1.0 - ADAM_B2 ** ADAM_STEP)
    delta = -ADAM_LR * (m_hat / (jnp.sqrt(v_hat) + ADAM_EPS) + ADAM_WD * w)
    return delta, m, v


def _adamw_layer(name, l, w, g, m, v, into):
    k = w.shape[-1]
    three = lambda a: (a.reshape(N_LAYER, -1, k), l)
    fn = lambda w_, g_, m_, v_: _adamw_fn(w_, g_, m_, v_) + (g_,)
    outs = _ew(name, fn, [three(w), g.reshape(-1, k), three(m), three(v)], 4, out_slot=l,
               into=None if into is None else [a.reshape(N_LAYER, -1, k) for a in into])
    return [o.reshape(w.shape) for o in outs]


def _adamw_rows(name, l, w, g, m, v, into):
    _, R, k = w.shape
    n_into = len(into) if into else 0

    def body(*refs):
        w_ref, g_ref, m_ref, v_ref = refs[:4]
        d_out, m_out, v_out, g_out = refs[4 + n_into:]
        g_blk = g_ref[...]
        d_out[...], m_out[...], v_out[...] = _adamw_fn(w_ref[...], g_blk, m_ref[...], v_ref[...])
        g_out[...] = g_blk

    spec = pl.BlockSpec((None, EW_ROWS, k), lambda i: (l, i, 0))
    return pl.pallas_call(
        body, name=name, grid=(-(-R // EW_ROWS),),
        in_specs=[spec, pl.BlockSpec((EW_ROWS, k), lambda i: (i, 0)), spec, spec] + [_ANY] * n_into,
        out_specs=[spec] * 4, out_shape=[jax.ShapeDtypeStruct((N_LAYER, R, k), F32)] * 4,
        input_output_aliases={4 + j: j for j in range(n_into)},
        compiler_params=_params(dimension_semantics=("arbitrary",)),
    )(w, g, m, v, *(into or []))


_SMALL = [("norm_pre", (2, 1024)), ("norm_post", (2, 1024)), ("norm_mem", (2, 1024)), ("a_log", (2, 4)),
          ("dt_bias", (2, 4)), ("dn_norm", (2, 128)), ("gm_norm", (2, 512)), ("spatial_w", (2, 4, 128, 128)),
          ("spatial_b", (2, 4, 128)), ("sinks", (2, 8)), ("loss", (2, 1))]
_SMALL_ROWS = 208
_BIG = ["w_in", "conv_w", "w_mem_kv", "w_up", "w_out"]
_NAMES = ["norm_pre", "norm_post", "norm_mem", "w_in", "conv_w", "a_log", "dt_bias", "dn_norm", "gm_norm",
          "spatial_w", "spatial_b", "sinks", "w_mem_kv", "w_up", "w_out"]


def _size(shape):
    n = 1
    for s in shape:
        n *= s
    return n


def _pack_small(d):
    rows = []
    for n, shp in _SMALL:
        a = d[n].reshape(N_LAYER, -1)
        rows.append(a.reshape(-1, 1024) if a.shape[1] > 1024 else jnp.pad(a, ((0, 6), (0, 1024 - a.shape[1]))))
    assert sum(r.shape[0] for r in rows) == _SMALL_ROWS
    return jnp.concatenate(rows, axis=0)


def _unpack_small(p):
    out, off = {}, 0
    for n, shp in _SMALL:
        c = _size(shp) // N_LAYER
        k = 8 if c <= 1024 else _size(shp) // 1024
        out[n] = (p[off:off + N_LAYER, :c] if c <= 1024 else p[off:off + k]).reshape(shp)
        off += k
    return out


_HALF_SHAPE = {"w_in": (SHARD_PAD // 2, D_MODEL), "conv_w": (2, 3 * BRANCH_W // N_CHIP), "w_mem_kv": (128, D_MODEL),
               "w_up": (2, BRANCH_W, D_MODEL // N_CHIP), "w_out": (128, D_MODEL)}


def _chip_major(g):
    g = jnp.swapaxes(g, 0, 1)
    return g.reshape((N_CHIP, 2 * g.shape[2]) + g.shape[3:])


def _half_major(g):
    g = g.reshape((N_CHIP, 2, g.shape[1] // 2) + g.shape[2:])
    return jnp.swapaxes(g, 0, 1).astype(BF16)


N_EARLY = 2


def _early_views(l, g_in, g_conv, small):
    return _layer_weights(l, _w_pad_from_slabs(g_in),
                          _chip_major(g_conv).transpose(1, 0, 2).reshape(4, 3 * BRANCH_W), small)


def _late_views(g_kv, g_up, g_out):
    return dict(w_mem_kv=_chip_major(g_kv).reshape(D_MODEL, D_MODEL),
                w_up=_chip_major(g_up).transpose(1, 2, 0, 3).reshape(4, BRANCH_W, D_MODEL),
                w_out=_chip_major(g_out).reshape(D_MODEL, D_MODEL))


def _pair_sums(g):
    add2 = lambda a, b: [a.astype(F32) + b.astype(F32)]
    rest = [_half_major(g["conv"].reshape(4, N_CHIP, 3 * BRANCH_W // N_CHIP).transpose(1, 0, 2)),
            _half_major(g["w_mem_kv"].reshape(N_CHIP, D_MODEL // N_CHIP, D_MODEL)),
            _half_major(g["w_up"].reshape(4, BRANCH_W, N_CHIP, D_MODEL // N_CHIP).transpose(2, 0, 1, 3)),
            _half_major(g["w_out"].reshape(N_CHIP, D_MODEL // N_CHIP, D_MODEL))]
    mine, theirs, rest_theirs = _slabs_from_pad(g["w_pad"], rest)
    pair = [_ew("pair_sum_w_in", add2, [mine.reshape(-1, D_MODEL), theirs.reshape(-1, D_MODEL)], 1, BF16)[0]
            .reshape(mine.shape)]
    for n, b, p in zip(_BIG[1:], rest, rest_theirs):
        k = b.shape[-1]
        pair.append(_ew("pair_sum_" + n, add2, [(b.reshape(2, -1, k), "c"), p.reshape(-1, k)], 1, BF16)[0]
                    .reshape(p.shape))
    return pair


def _chip_sums(landed, pair, me):
    add4 = lambda a, b, c_, d: [((a.astype(F32) + b.astype(F32)) + c_.astype(F32)) + d.astype(F32)]
    totals = []
    for n, r, q in zip(_BIG, landed, pair):
        r = _own_slot(r, lax.dynamic_index_in_dim(q, me, 0), me, 0)
        k = r.shape[-1]
        totals.append(_ew("chip_sum_" + n, add4, [(r.reshape(N_CHIP, -1, k), s) for s in range(N_CHIP)], 1,
                          out_slot="c")[0].reshape((2,) + r.shape[1:]))
    return totals


def _own_slot(buf, mine, me, axis):
    return lax.dynamic_update_index_in_dim(buf, mine.astype(buf.dtype), me, axis)


def kernel(x, mem, norm_pre, norm_post, norm_mem, w_in, conv_w, a_log, dt_bias, dn_norm, gm_norm, spatial_w, spatial_b, sinks, w_mem_kv, w_up, w_out, loss_target, m_norm_pre, m_norm_post, m_norm_mem, m_w_in, m_conv_w, m_a_log, m_dt_bias, m_dn_norm, m_gm_norm, m_spatial_w, m_spatial_b, m_sinks, m_w_mem_kv, m_w_up, m_w_out, v_norm_pre, v_norm_post, v_norm_mem, v_w_in, v_conv_w, v_a_log, v_dt_bias, v_dn_norm, v_gm_norm, v_spatial_w, v_spatial_b, v_sinks, v_w_mem_kv, v_w_up, v_w_out):
    w = dict(norm_pre=norm_pre, norm_post=norm_post, norm_mem=norm_mem, w_in=w_in, conv_w=conv_w, a_log=a_log,
             dt_bias=dt_bias, dn_norm=dn_norm, gm_norm=gm_norm, spatial_w=spatial_w, spatial_b=spatial_b, sinks=sinks,
             w_mem_kv=w_mem_kv, w_up=w_up, w_out=w_out)
    m = dict(norm_pre=m_norm_pre, norm_post=m_norm_post, norm_mem=m_norm_mem, w_in=m_w_in, conv_w=m_conv_w,
             a_log=m_a_log, dt_bias=m_dt_bias, dn_norm=m_dn_norm, gm_norm=m_gm_norm, spatial_w=m_spatial_w,
             spatial_b=m_spatial_b, sinks=m_sinks, w_mem_kv=m_w_mem_kv, w_up=m_w_up, w_out=m_w_out)
    v = dict(norm_pre=v_norm_pre, norm_post=v_norm_post, norm_mem=v_norm_mem, w_in=v_w_in, conv_w=v_conv_w,
             a_log=v_a_log, dt_bias=v_dt_bias, dn_norm=v_dn_norm, gm_norm=v_gm_norm, spatial_w=v_spatial_w,
             spatial_b=v_spatial_b, sinks=v_sinks, w_mem_kv=v_w_mem_kv, w_up=v_w_up, w_out=v_w_out)
    me = 2 * lax.axis_index("x") + lax.axis_index("y")

    tr = lambda a: a.transpose(0, 2, 1)
    w_t = tr(w_in)
    w_in_t = jnp.pad(w_t.astype(BF16), ((0, 0), (0, SHARD_PAD - SHARD_IN), (0, 0)))
    for d in (w, m, v):
        d["loss"] = jnp.zeros((N_LAYER, 1), F32)
    local = dict(w_in=w_in_t, conv_w=conv_w, w_mem_kv=w_mem_kv.astype(BF16), w_up=w_up.astype(BF16),
                 w_out=w_out.astype(BF16))
    halves = lambda l: [local[n][l].reshape((2,) + _HALF_SHAPE[n]) for n in _BIG]
    own = lambda gathered, mine: [_own_slot(g, h[:, None], me, 1) for g, h in zip(gathered, mine)]
    lands = [jax.ShapeDtypeStruct((2, N_CHIP) + _HALF_SHAPE[n], local[n].dtype) for n in _BIG]
    h0 = halves(0)
    g0 = own(_gather_weights(h0[:N_EARLY], [True, False]), h0[:N_EARLY])
    rest0 = _split_start("gather_l0_rest_start", "gather", h0[N_EARLY:], lands[N_EARLY:], g0[1])
    started = _split_start("gather_l1_start", "gather", halves(1), lands, rest0[3])

    xl, meml = x[0], mem[0]
    W0 = _early_views(0, g0[0], g0[1], w)
    W0["norm_pre"] = W0["norm_pre"] + started[3][0, 0]

    def late0(cols):
        mine, landed = _split_wait("gather_l0_rest_wait", "gather", rest0, cols)
        return _late_views(*own(_pair_forward(landed), mine))

    x1, sv0, W0 = _layer_fwd(xl, meml, W0, late0)
    mine1, landed1 = _split_wait("gather_l1_wait", "gather", started, x1)
    g1 = own(_pair_forward(landed1), mine1)
    W1 = dict(_early_views(1, g1[0], g1[1], w), **_late_views(*g1[N_EARLY:]))
    x2, sv1, _ = _layer_fwd(x1, meml, W1)
    dy, lrows = _rows_fwd("loss", _loss_fn, [Row(x2, D_MODEL, 0), Row(loss_target[0], D_MODEL, 0)], [],
                          [(D_MODEL, F32), (LANES, F32)], BIG_TR)
    loss_local = jnp.sum(lrows[:, 0])

    scattering = {}

    def start_scatter(l):
        def on_weight_grads(g):
            pair = _pair_sums(g)
            scattering[l] = _split_start("scatter_l%d_start" % l, "scatter", pair,
                                         [jax.ShapeDtypeStruct(p.shape, p.dtype) for p in pair], pair[1])
            return scattering[l][3]
        return on_weight_grads

    dx1, grads1 = _layer_bwd(dy, meml, W1, sv1, start_scatter(1))
    dx, grads0 = _layer_bwd(dx1, meml, W0, sv0, start_scatter(0))
    pair1, landed1 = _split_wait("scatter_l1_wait", "scatter", scattering[1], dx)
    after_start = scattering[0][3][0, 0]
    grads = [grads0, grads1]

    small_local = dict(
        norm_pre=jnp.stack([g["norm_pre"][0] for g in grads]), norm_post=jnp.stack([g["norm_post"][0] for g in grads]),
        norm_mem=jnp.stack([g["norm_mem"][0] for g in grads]), a_log=jnp.stack([g["a_vec"][0, 4:8] for g in grads]),
        dt_bias=jnp.stack([g["dt_vec"][0, 4:8] for g in grads]), dn_norm=jnp.stack([g["dn_norm"][0] for g in grads]),
        gm_norm=jnp.stack([g["gm_norm"][0] for g in grads]), spatial_w=jnp.stack([g["spatial_w"] for g in grads]),
        spatial_b=jnp.stack([g["spatial_b"][:, 0, :] for g in grads]),
        sinks=jnp.stack([g["sink_vec"][0, :8] for g in grads]),
        loss=jnp.stack([loss_local, jnp.zeros((), F32)]).reshape(N_LAYER, 1))
    packed = _pack_small(small_local) + after_start
    gsmall_packed = _allreduce_small(packed.reshape(2, -1, 1024)).reshape(-1, 1024)

    d_s, m_s, v_s = _ew("adamw_small", _adamw_fn, [_pack_small(w), gsmall_packed, _pack_small(m), _pack_small(v)], 3)
    gsmall, dsmall, msmall, vsmall = (_unpack_small(p) for p in (gsmall_packed, d_s, m_s, v_s))
    g_o, d_o, m_o, v_o = dict(gsmall), dict(dsmall), dict(msmall), dict(vsmall)
    loss = gsmall["loss"][0, 0]
    m_t, v_t = tr(m["w_in"]), tr(v["w_in"])

    def update(l, totals, into):
        outs = {}
        for n, t in zip(_BIG, totals):
            g_l = t.reshape(local[n].shape[1:])
            if n == "w_in":
                outs[n] = _adamw_rows("adamw_" + n, l, w_t, g_l, m_t, v_t, into and into[n])
            else:
                outs[n] = _adamw_layer("adamw_" + n, l, w[n], g_l, m[n], v[n], into and into[n])
        return outs

    landed1[1] = landed1[1] + after_start.astype(landed1[1].dtype)
    outs1 = update(1, _pair_share(_chip_sums(landed1, pair1, me)), None)
    pair0, landed0 = _split_wait("scatter_l0_wait", "scatter", scattering[0], outs1["w_in"][0])
    outs = update(0, _pair_share(_chip_sums(landed0, pair0, me)), outs1)
    for n in _BIG:
        d_o[n], m_o[n], v_o[n], g_o[n] = [tr(o) for o in outs[n]] if n == "w_in" else outs[n]
    return (loss, dx[None], *[g_o[n] for n in _NAMES], *[d_o[n] for n in _NAMES], *[m_o[n] for n in _NAMES],
            *[v_o[n] for n in _NAMES])
```
